```python
import jax, jax.numpy as jnp
from jax import lax
import numpy as np

D_MODEL = 1024
BATCH = 8
SEQ = 2048
DEPTH = 2

CHUNK = 64
PLE_DIM = 256
N_A = DEPTH // 2
N_B = DEPTH - N_A
POOL_WINDOWS = (2, 4, 8, 16)
N_POOL_GROUPS = len(POOL_WINDOWS)
POOL_GROUP_DIM = D_MODEL // N_POOL_GROUPS
POOL_WMAX = max(POOL_WINDOWS)
SB_HEADS = 16
SB_HEAD_DIM = D_MODEL // SB_HEADS
SB_SCALE = SB_HEAD_DIM ** -0.5
Q_BLOCK = 128
EPS = 1e-6

kernel_name = "yoco_pool_stickbreaking_hybrid"


def rms_norm(x, g):
    xf = x.astype(jnp.float32)
    y = xf * lax.rsqrt(jnp.mean(xf * xf, axis=-1, keepdims=True) + EPS)
    return (y * g.astype(jnp.float32)).astype(x.dtype)


def pool_mixer(h, w_in, w_group, scale, w_out):
    B, S, _ = h.shape
    u, z = jnp.split(h @ w_in, 2, axis=-1)
    u = u.reshape(B, S, N_POOL_GROUPS, POOL_GROUP_DIM)
    uf = u.astype(jnp.float32)
    cs = jnp.cumsum(uf, axis=1)
    cs_ext = jnp.pad(cs, ((0, 0), (POOL_WMAX, 0), (0, 0), (0, 0)))
    pos1 = jnp.arange(1, S + 1)
    means = []
    for g, w in enumerate(POOL_WINDOWS):
        lo = cs_ext[:, POOL_WMAX - w: POOL_WMAX - w + S, g]
        cnt = jnp.minimum(pos1, w).astype(jnp.float32)[None, :, None]
        means.append((cs[:, :, g] - lo) / cnt)
    pooled = (jnp.stack(means, axis=2) - uf).astype(h.dtype)
    mixed = jnp.einsum('bsgc,gcd->bsgd', pooled, w_group).reshape(B, S, D_MODEL) * scale
    return (mixed * jax.nn.silu(z)) @ w_out


def split_heads(t):
    B, S, _ = t.shape
    return t.reshape(B, S, SB_HEADS, SB_HEAD_DIM).transpose(0, 2, 1, 3)


def stick_breaking_attention(q, k, v):
    S = q.shape[2]
    outs = []
    for blk in range(S // Q_BLOCK):
        t0 = blk * Q_BLOCK
        L = t0 + Q_BLOCK
        logits = jnp.einsum('bhtd,bhsd->bhts', q[:, :, t0:L], k[:, :, :L]).astype(jnp.float32) * SB_SCALE
        t_idx = t0 + jnp.arange(Q_BLOCK)[:, None]
        s_idx = jnp.arange(L)[None, :]
        mask = s_idx < t_idx
        log_keep = jnp.where(mask, jax.nn.log_sigmoid(-logits), 0.0)
        later = lax.cumsum(log_keep, axis=log_keep.ndim - 1, reverse=True) - log_keep
        log_a = jax.nn.log_sigmoid(logits) + later
        a = jnp.where(mask, jnp.exp(log_a), 0.0)
        outs.append(jnp.einsum('bhts,bhsd->bhtd', a.astype(v.dtype), v[:, :, :L]))
    return jnp.concatenate(outs, axis=2)


def _fwd_setup_inputs(seed: int = 0) -> dict:
    key = jax.random.key(seed)
    ks = jax.random.split(key, 20)
    D, C = D_MODEL, POOL_GROUP_DIM
    f32 = jnp.float32

    def nrm(k, shape, fan_in, gain=1.0):
        return jax.random.normal(k, shape, f32) * (gain * fan_in ** -0.5)

    def gain(k, shape):
        return 1.0 + 0.05 * jax.random.normal(k, shape, f32)

    return {
        "x": jax.random.normal(ks[0], (BATCH, SEQ, D), f32),
        "p": jax.random.normal(ks[1], (DEPTH, BATCH, SEQ, PLE_DIM), f32),
        "a_norm": gain(ks[2], (N_A, D)),
        "a_w_in": nrm(ks[3], (N_A, D, 2 * D), D),
        "a_w_group": nrm(ks[4], (N_A, N_POOL_GROUPS, C, C), C),
        "a_scale": gain(ks[5], (N_A, D)),
        "a_w_out": nrm(ks[6], (N_A, D, D), D, 0.5),
        "kv_norm": gain(ks[7], (D,)),
        "w_kv": nrm(ks[8], (D, 2 * D), D),
        "k_norm": gain(ks[9], (SB_HEAD_DIM,)),
        "b_norm": gain(ks[10], (N_B, D)),
        "b_w_in": nrm(ks[11], (N_B, D, 2 * D), D),
        "b_q_norm": gain(ks[12], (N_B, SB_HEAD_DIM)),
        "b_w_out": nrm(ks[13], (N_B, D, D), D, 0.5),
        "ple_w": nrm(ks[14], (DEPTH, PLE_DIM, D), PLE_DIM, 0.5),
        "ple_gate_w": nrm(ks[15], (DEPTH, D, D), D),
    }


def _fwd_reference(x, p, a_norm, a_w_in, a_w_group, a_scale, a_w_out, kv_norm, w_kv, k_norm,
              b_norm, b_w_in, b_q_norm, b_w_out, ple_w, ple_gate_w):
    B, S, _ = x.shape
    k_sh = v_sh = None
    for i in range(DEPTH):
        if i < N_A:
            h = rms_norm(x, a_norm[i])
            x = x + pool_mixer(h, a_w_in[i], a_w_group[i], a_scale[i], a_w_out[i])
        else:
            j = i - N_A
            if j == 0:
                kv_in = rms_norm(x, kv_norm)
                k_all, v_all = jnp.split(kv_in @ w_kv, 2, axis=-1)
                k_sh = rms_norm(split_heads(k_all), k_norm)
                v_sh = split_heads(v_all)
            h = rms_norm(x, b_norm[j])
            q, z = jnp.split(h @ b_w_in[j], 2, axis=-1)
            q = rms_norm(split_heads(q), b_q_norm[j])
            o = stick_breaking_attention(q, k_sh, v_sh)
            o = o.transpose(0, 2, 1, 3).reshape(B, S, D_MODEL)
            x = x + (o * jax.nn.silu(z)) @ b_w_out[j]
        x = x + (p[i] @ ple_w[i]) * jax.nn.sigmoid(x @ ple_gate_w[i])
    return x


import jax as _jax
import jax.numpy as _jnp

TWIN_FORMAT = 'train_step'
FWD_PARAMS = ['x', 'p', 'a_norm', 'a_w_in', 'a_w_group', 'a_scale', 'a_w_out', 'kv_norm', 'w_kv', 'k_norm', 'b_norm', 'b_w_in', 'b_q_norm', 'b_w_out', 'ple_w', 'ple_gate_w']
TWIN_WEIGHTS = ['a_norm', 'a_w_in', 'a_w_group', 'a_scale', 'a_w_out', 'kv_norm', 'w_kv', 'k_norm', 'b_norm', 'b_w_in', 'b_q_norm', 'b_w_out', 'ple_w', 'ple_gate_w']
TWIN_DIFF_INPUT = 'x'
TWIN_INPUTS = ['x', 'p', 'a_norm', 'a_w_in', 'a_w_group', 'a_scale', 'a_w_out', 'kv_norm', 'w_kv', 'k_norm', 'b_norm', 'b_w_in', 'b_q_norm', 'b_w_out', 'ple_w', 'ple_gate_w', 'loss_target', 'm_a_norm', 'm_a_w_in', 'm_a_w_group', 'm_a_scale', 'm_a_w_out', 'm_kv_norm', 'm_w_kv', 'm_k_norm', 'm_b_norm', 'm_b_w_in', 'm_b_q_norm', 'm_b_w_out', 'm_ple_w', 'm_ple_gate_w', 'v_a_norm', 'v_a_w_in', 'v_a_w_group', 'v_a_scale', 'v_a_w_out', 'v_kv_norm', 'v_w_kv', 'v_k_norm', 'v_b_norm', 'v_b_w_in', 'v_b_q_norm', 'v_b_w_out', 'v_ple_w', 'v_ple_gate_w']
TWIN_OUTPUTS = ['loss', 'grad_x', 'grad_a_norm', 'grad_a_w_in', 'grad_a_w_group', 'grad_a_scale', 'grad_a_w_out', 'grad_kv_norm', 'grad_w_kv', 'grad_k_norm', 'grad_b_norm', 'grad_b_w_in', 'grad_b_q_norm', 'grad_b_w_out', 'grad_ple_w', 'grad_ple_gate_w', 'delta_a_norm', 'delta_a_w_in', 'delta_a_w_group', 'delta_a_scale', 'delta_a_w_out', 'delta_kv_norm', 'delta_w_kv', 'delta_k_norm', 'delta_b_norm', 'delta_b_w_in', 'delta_b_q_norm', 'delta_b_w_out', 'delta_ple_w', 'delta_ple_gate_w', 'new_m_a_norm', 'new_m_a_w_in', 'new_m_a_w_group', 'new_m_a_scale', 'new_m_a_w_out', 'new_m_kv_norm', 'new_m_w_kv', 'new_m_k_norm', 'new_m_b_norm', 'new_m_b_w_in', 'new_m_b_q_norm', 'new_m_b_w_out', 'new_m_ple_w', 'new_m_ple_gate_w', 'new_v_a_norm', 'new_v_a_w_in', 'new_v_a_w_group', 'new_v_a_scale', 'new_v_a_w_out', 'new_v_kv_norm', 'new_v_w_kv', 'new_v_k_norm', 'new_v_b_norm', 'new_v_b_w_in', 'new_v_b_q_norm', 'new_v_b_w_out', 'new_v_ple_w', 'new_v_ple_gate_w']
TWIN_LEAF_KINDS = {'loss': 'loss', 'grad_x': 'grad_x', 'grad_a_norm': 'grad_w', 'grad_a_w_in': 'grad_w', 'grad_a_w_group': 'grad_w', 'grad_a_scale': 'grad_w', 'grad_a_w_out': 'grad_w', 'grad_kv_norm': 'grad_w', 'grad_w_kv': 'grad_w', 'grad_k_norm': 'grad_w', 'grad_b_norm': 'grad_w', 'grad_b_w_in': 'grad_w', 'grad_b_q_norm': 'grad_w', 'grad_b_w_out': 'grad_w', 'grad_ple_w': 'grad_w', 'grad_ple_gate_w': 'grad_w', 'delta_a_norm': 'delta_w', 'delta_a_w_in': 'delta_w', 'delta_a_w_group': 'delta_w', 'delta_a_scale': 'delta_w', 'delta_a_w_out': 'delta_w', 'delta_kv_norm': 'delta_w', 'delta_w_kv': 'delta_w', 'delta_k_norm': 'delta_w', 'delta_b_norm': 'delta_w', 'delta_b_w_in': 'delta_w', 'delta_b_q_norm': 'delta_w', 'delta_b_w_out': 'delta_w', 'delta_ple_w': 'delta_w', 'delta_ple_gate_w': 'delta_w', 'new_m_a_norm': 'new_m', 'new_m_a_w_in': 'new_m', 'new_m_a_w_group': 'new_m', 'new_m_a_scale': 'new_m', 'new_m_a_w_out': 'new_m', 'new_m_kv_norm': 'new_m', 'new_m_w_kv': 'new_m', 'new_m_k_norm': 'new_m', 'new_m_b_norm': 'new_m', 'new_m_b_w_in': 'new_m', 'new_m_b_q_norm': 'new_m', 'new_m_b_w_out': 'new_m', 'new_m_ple_w': 'new_m', 'new_m_ple_gate_w': 'new_m', 'new_v_a_norm': 'new_v', 'new_v_a_w_in': 'new_v', 'new_v_a_w_group': 'new_v', 'new_v_a_scale': 'new_v', 'new_v_a_w_out': 'new_v', 'new_v_kv_norm': 'new_v', 'new_v_w_kv': 'new_v', 'new_v_k_norm': 'new_v', 'new_v_b_norm': 'new_v', 'new_v_b_w_in': 'new_v', 'new_v_b_q_norm': 'new_v', 'new_v_b_w_out': 'new_v', 'new_v_ple_w': 'new_v', 'new_v_ple_gate_w': 'new_v'}


def _forward(args):
    return _fwd_reference(*[args[k] for k in FWD_PARAMS])


def _output_shape():
    out = _jax.eval_shape(lambda: _forward(_fwd_setup_inputs(0)))
    return out.shape, out.dtype

N_MICROBATCH = 1
ADAM_LR = 0.001
ADAM_B1 = 0.9
ADAM_B2 = 0.999
ADAM_EPS = 1e-08
ADAM_WD = 0.01
ADAM_STEP = 10
PER_EXAMPLE_BATCH_AXIS = {'x': 0, 'p': 1, 'loss_target': 0}
SHARED_INPUTS = []
_WEIGHT_DTYPES = {'a_norm': _jnp.float32, 'a_w_in': _jnp.float32, 'a_w_group': _jnp.float32, 'a_scale': _jnp.float32, 'a_w_out': _jnp.float32, 'kv_norm': _jnp.float32, 'w_kv': _jnp.float32, 'k_norm': _jnp.float32, 'b_norm': _jnp.float32, 'b_w_in': _jnp.float32, 'b_q_norm': _jnp.float32, 'b_w_out': _jnp.float32, 'ple_w': _jnp.float32, 'ple_gate_w': _jnp.float32}
MOMENT_SCALE = {'a_norm': 2.408962e+00, 'a_w_in': 7.055383e-02, 'a_w_group': 8.737089e-02, 'a_scale': 1.135547e+00, 'a_w_out': 1.298346e-01, 'kv_norm': 5.932345e-01, 'w_kv': 3.279335e-02, 'k_norm': 1.472591e+00, 'b_norm': 6.967796e-01, 'b_w_in': 3.400398e-02, 'b_q_norm': 1.462253e+00, 'b_w_out': 7.750991e-02, 'ple_w': 1.501454e-01, 'ple_gate_w': 2.224338e-02}


def _to_microbatches(a, axis):
    t = _jnp.moveaxis(a, axis, 0)
    t = t.reshape((N_MICROBATCH, t.shape[0] // N_MICROBATCH) + t.shape[1:])
    return _jnp.moveaxis(t, 1, axis + 1)


def setup_inputs(seed: int = 0) -> dict:
    inp = _fwd_setup_inputs(seed)
    key = _jax.random.fold_in(_jax.random.key(seed), 7919)
    shape, _ = _output_shape()
    out = dict(inp)
    out["loss_target"] = _jax.random.normal(_jax.random.fold_in(key, 0), shape, _jnp.float32)
    for i, name in enumerate(TWIN_WEIGHTS):
        w = inp[name].astype(_jnp.float32)
        if MOMENT_SCALE is None:
            s = _jnp.sqrt(_jnp.mean(_jnp.square(w)) + 1e-30)
        else:
            s = MOMENT_SCALE[name]
        km, kv = _jax.random.split(_jax.random.fold_in(key, i + 1))
        out[name] = w
        out["m_" + name] = s * _jax.random.normal(km, w.shape, _jnp.float32)
        out["v_" + name] = (s * s) * _jax.random.uniform(kv, w.shape, _jnp.float32, 0.5, 1.5)
    if N_MICROBATCH > 1:
        for name, axis in PER_EXAMPLE_BATCH_AXIS.items():
            out[name] = _to_microbatches(out[name], axis)
    return {'x': out['x'], 'p': out['p'], 'a_norm': out['a_norm'], 'a_w_in': out['a_w_in'], 'a_w_group': out['a_w_group'], 'a_scale': out['a_scale'], 'a_w_out': out['a_w_out'], 'kv_norm': out['kv_norm'], 'w_kv': out['w_kv'], 'k_norm': out['k_norm'], 'b_norm': out['b_norm'], 'b_w_in': out['b_w_in'], 'b_q_norm': out['b_q_norm'], 'b_w_out': out['b_w_out'], 'ple_w': out['ple_w'], 'ple_gate_w': out['ple_gate_w'], 'loss_target': out['loss_target'], 'm_a_norm': out['m_a_norm'], 'm_a_w_in': out['m_a_w_in'], 'm_a_w_group': out['m_a_w_group'], 'm_a_scale': out['m_a_scale'], 'm_a_w_out': out['m_a_w_out'], 'm_kv_norm': out['m_kv_norm'], 'm_w_kv': out['m_w_kv'], 'm_k_norm': out['m_k_norm'], 'm_b_norm': out['m_b_norm'], 'm_b_w_in': out['m_b_w_in'], 'm_b_q_norm': out['m_b_q_norm'], 'm_b_w_out': out['m_b_w_out'], 'm_ple_w': out['m_ple_w'], 'm_ple_gate_w': out['m_ple_gate_w'], 'v_a_norm': out['v_a_norm'], 'v_a_w_in': out['v_a_w_in'], 'v_a_w_group': out['v_a_w_group'], 'v_a_scale': out['v_a_scale'], 'v_a_w_out': out['v_a_w_out'], 'v_kv_norm': out['v_kv_norm'], 'v_w_kv': out['v_w_kv'], 'v_k_norm': out['v_k_norm'], 'v_b_norm': out['v_b_norm'], 'v_b_w_in': out['v_b_w_in'], 'v_b_q_norm': out['v_b_q_norm'], 'v_b_w_out': out['v_b_w_out'], 'v_ple_w': out['v_ple_w'], 'v_ple_gate_w': out['v_ple_gate_w']}


def _loss(weights, diff, rest, loss_target):
    with _jax.named_scope("forward"):
        args = {**rest, TWIN_DIFF_INPUT: diff, **{k: w.astype(_WEIGHT_DTYPES[k]) for k, w in weights.items()}}
        y = _forward(args)
    with _jax.named_scope("loss_head"):
        err = _jnp.square(y.astype(_jnp.float32) - loss_target)
        return 0.5 * _jnp.sum(_jnp.mean(err, axis=-1)) if err.ndim else 0.5 * err


def _adamw(w, g, m, v):
    m = ADAM_B1 * m + (1.0 - ADAM_B1) * g
    v = ADAM_B2 * v + (1.0 - ADAM_B2) * _jnp.square(g)
    m_hat = m / (1.0 - ADAM_B1 ** ADAM_STEP)
    v_hat = v / (1.0 - ADAM_B2 ** ADAM_STEP)
    delta = -ADAM_LR * (m_hat / (_jnp.sqrt(v_hat) + ADAM_EPS) + ADAM_WD * w)
    return delta, m, v


def reference(x, p, a_norm, a_w_in, a_w_group, a_scale, a_w_out, kv_norm, w_kv, k_norm, b_norm, b_w_in, b_q_norm, b_w_out, ple_w, ple_gate_w, loss_target, m_a_norm, m_a_w_in, m_a_w_group, m_a_scale, m_a_w_out, m_kv_norm, m_w_kv, m_k_norm, m_b_norm, m_b_w_in, m_b_q_norm, m_b_w_out, m_ple_w, m_ple_gate_w, v_a_norm, v_a_w_in, v_a_w_group, v_a_scale, v_a_w_out, v_kv_norm, v_w_kv, v_k_norm, v_b_norm, v_b_w_in, v_b_q_norm, v_b_w_out, v_ple_w, v_ple_gate_w):
    given = dict(x=x, p=p, a_norm=a_norm, a_w_in=a_w_in, a_w_group=a_w_group, a_scale=a_scale, a_w_out=a_w_out, kv_norm=kv_norm, w_kv=w_kv, k_norm=k_norm, b_norm=b_norm, b_w_in=b_w_in, b_q_norm=b_q_norm, b_w_out=b_w_out, ple_w=ple_w, ple_gate_w=ple_gate_w, loss_target=loss_target, m_a_norm=m_a_norm, m_a_w_in=m_a_w_in, m_a_w_group=m_a_w_group, m_a_scale=m_a_scale, m_a_w_out=m_a_w_out, m_kv_norm=m_kv_norm, m_w_kv=m_w_kv, m_k_norm=m_k_norm, m_b_norm=m_b_norm, m_b_w_in=m_b_w_in, m_b_q_norm=m_b_q_norm, m_b_w_out=m_b_w_out, m_ple_w=m_ple_w, m_ple_gate_w=m_ple_gate_w, v_a_norm=v_a_norm, v_a_w_in=v_a_w_in, v_a_w_group=v_a_w_group, v_a_scale=v_a_scale, v_a_w_out=v_a_w_out, v_kv_norm=v_kv_norm, v_w_kv=v_w_kv, v_k_norm=v_k_norm, v_b_norm=v_b_norm, v_b_w_in=v_b_w_in, v_b_q_norm=v_b_q_norm, v_b_w_out=v_b_w_out, v_ple_w=v_ple_w, v_ple_gate_w=v_ple_gate_w)
    weights = {n: given[n] for n in TWIN_WEIGHTS}
    shared = {n: given[n] for n in SHARED_INPUTS}
    per_example = {n: given[n] for n in ['x', 'p']}
    grad_fn = _jax.value_and_grad(_loss, argnums=(0, 1))

    def one_microbatch(ex, loss_target):
        ex = dict(ex)
        diff = ex.pop(TWIN_DIFF_INPUT)
        return grad_fn(weights, diff, {**shared, **ex}, loss_target)

    if N_MICROBATCH == 1:
        loss, (grad_w, grad_x) = one_microbatch(per_example, given["loss_target"])
    else:
        def body(carry, xs):
            loss_sum, grad_sum = carry
            l_k, (gw_k, gx_k) = one_microbatch(xs[0], xs[1])
            with _jax.named_scope("update"):
                return (loss_sum + l_k, _jax.tree.map(_jnp.add, grad_sum, gw_k)), gx_k

        init = (_jnp.zeros((), _jnp.float32), _jax.tree.map(_jnp.zeros_like, weights))
        (loss, grad_w), grad_x = _jax.lax.scan(body, init, (per_example, given["loss_target"]))
    with _jax.named_scope("update"):
        delta_w, new_m, new_v = {}, {}, {}
        for n in TWIN_WEIGHTS:
            delta_w[n], new_m[n], new_v[n] = _adamw(weights[n], grad_w[n], given["m_" + n], given["v_" + n])
    return (loss, grad_x, *[grad_w[n] for n in TWIN_WEIGHTS], *[delta_w[n] for n in TWIN_WEIGHTS],
            *[new_m[n] for n in TWIN_WEIGHTS], *[new_v[n] for n in TWIN_WEIGHTS])
```

```python
import functools

import jax
import jax.numpy as jnp
from jax import lax
from jax.experimental import pallas as pl
from jax.experimental.pallas import tpu as pltpu

F32 = jnp.float32
BF16 = jnp.bfloat16
MESH = pl.DeviceIdType.MESH

N_DEV = 8
D = 1024
N_GROUPS = 4
GROUP_DIM = D // N_GROUPS
HALO = 16
HEADS = 16
HEAD_DIM = D // HEADS
SB_SCALE = HEAD_DIM ** -0.5
BLK = 128
EPS = 1e-6
ADAM_LR = 0.001
ADAM_B1 = 0.9
ADAM_B2 = 0.999
ADAM_EPS = 1e-08
ADAM_WD = 0.01
ADAM_STEP = 10
TM = 256
VMEM_LIMIT = 56 * 1024 * 1024

HBM_SPEC = pl.BlockSpec(memory_space=pltpu.HBM)


def _dot(a, b):
    return jnp.dot(a, b, preferred_element_type=F32)


def _dot_nt(a, b):
    return lax.dot_general(a, b, (((1,), (1,)), ((), ())), preferred_element_type=F32)


def _dot_tn(a, b):
    return lax.dot_general(a, b, (((0,), (0,)), ((), ())), preferred_element_type=F32)


def _sigmoid(x):
    return jax.nn.sigmoid(x)


def _split_dot(x, mat):
    hi = x.astype(BF16)
    lo = (x - hi.astype(F32)).astype(BF16)
    return _dot(hi, mat) + _dot(lo, mat)


def _rms(x):
    return lax.rsqrt(jnp.mean(x * x, axis=-1, keepdims=True) + EPS)


def _rows_call(body, name, n_rows, row_ins, const_ins, row_outs, const_outs=(), scratch=(),
               reverse=False, tm=TM):
    nb = n_rows // tm

    def row_map(i):
        return ((nb - 1 - i) if reverse else i, 0)

    def const_map(nd):
        return lambda i: (0,) * nd

    in_specs = [pl.BlockSpec((tm, a.shape[1]), row_map) for a in row_ins]
    in_specs += [pl.BlockSpec(a.shape, const_map(a.ndim)) for a in const_ins]
    out_specs = [pl.BlockSpec((tm, w), row_map) for (w, _) in row_outs]
    out_specs += [pl.BlockSpec(s, const_map(len(s))) for (s, _) in const_outs]
    out_shape = [jax.ShapeDtypeStruct((n_rows, w), dt) for (w, dt) in row_outs]
    out_shape += [jax.ShapeDtypeStruct(s, dt) for (s, dt) in const_outs]
    return pl.pallas_call(
        body, name=name, grid=(nb,), in_specs=in_specs, out_specs=out_specs, out_shape=out_shape,
        scratch_shapes=list(scratch),
        compiler_params=pltpu.CompilerParams(dimension_semantics=("arbitrary",),
                                             vmem_limit_bytes=VMEM_LIMIT),
    )(*row_ins, *const_ins)


def _ple_fwd(p_ref, xin, wple_ref, wgate_ref, e_ref, gt_ref):
    pb = p_ref[...].astype(BF16)
    for j in range(N_DEV):
        e_ref[:, j * 128:(j + 1) * 128] = _dot(pb, wple_ref[j])
    gt = _sigmoid(_dot(xin.astype(BF16), wgate_ref[...]))
    gt_ref[...] = gt
    return xin + e_ref[...] * gt


def _layer_a_fwd(x0, p0, a_norm, a_scale, w_in, w_group, w_out, w_ple, w_gate):
    s = x0.shape[0]
    tm = TM

    def body(x_ref, p_ref, an_ref, as_ref, win_ref, wg_ref, wout_ref, wple_ref, wgate_ref,
             h_ref, z_ref, pooled_ref, m_ref, y_ref, x1_ref, e_ref, gt_ref, x2_ref, uext):
        i = pl.program_id(0)

        @pl.when(i == 0)
        def _():
            uext[0:HALO, :] = jnp.zeros((HALO, D), F32)

        x = x_ref[...]
        h = (x * _rms(x) * an_ref[...]).astype(BF16)
        h_ref[...] = h
        for j in range(N_DEV):
            uz = _dot(h, win_ref[j])
            if j < 4:
                uext[HALO:HALO + tm, j * 256:(j + 1) * 256] = uz
            else:
                z_ref[:, (j - 4) * 256:(j - 3) * 256] = uz
        t = i * tm + lax.broadcasted_iota(jnp.int32, (tm, 1), 0)
        for g in range(N_GROUPS):
            w = 2 ** (g + 1)
            cols = slice(g * GROUP_DIM, (g + 1) * GROUP_DIM)
            ext = uext[:, cols]
            acc = ext
            k = 1
            while k < w:
                acc = acc + pltpu.roll(acc, k, 0)
                k *= 2
            cnt = jnp.minimum(t + 1, w).astype(F32)
            pooled = (acc[HALO:] / cnt - ext[HALO:]).astype(BF16)
            pooled_ref[:, cols] = pooled
            m_ref[:, cols] = _dot(pooled, wg_ref[g])
        uext[0:HALO, :] = uext[tm:tm + HALO, :]
        z = z_ref[...]
        y = (m_ref[...] * as_ref[...] * (z * _sigmoid(z))).astype(BF16)
        y_ref[...] = y
        x1 = x + _dot(y, wout_ref[...])
        x1_ref[...] = x1
        x2_ref[...] = _ple_fwd(p_ref, x1, wple_ref, wgate_ref, e_ref, gt_ref)

    row_outs = [(D, BF16), (D, F32), (D, BF16), (D, F32), (D, BF16), (D, F32), (D, F32), (D, F32), (D, F32)]
    return _rows_call(body, "layer_a_fwd", s, [x0, p0], [a_norm, a_scale, w_in, w_group, w_out, w_ple, w_gate],
                      row_outs, scratch=[pltpu.VMEM((tm + HALO, D), F32)])


def _layer_b_in_fwd(x2, kv_norm, b_norm, w_kv, w_bin):
    s = x2.shape[0]

    def body(x_ref, kvn_ref, bn_ref, wkv_ref, wbin_ref, hkv_ref, hb_ref, k_ref, v_ref, q_ref, zb_ref):
        x = x_ref[...]
        n = x * _rms(x)
        hkv = (n * kvn_ref[...]).astype(BF16)
        hb = (n * bn_ref[...]).astype(BF16)
        hkv_ref[...] = hkv
        hb_ref[...] = hb
        for j in range(N_DEV):
            kv = _dot(hkv, wkv_ref[j])
            qz = _dot(hb, wbin_ref[j])
            if j < 4:
                cols = slice(j * 256, (j + 1) * 256)
                k_ref[:, cols] = kv
                q_ref[:, cols] = qz
            else:
                cols = slice((j - 4) * 256, (j - 3) * 256)
                v_ref[:, cols] = kv.astype(BF16)
                zb_ref[:, cols] = qz

    row_outs = [(D, BF16), (D, BF16), (D, F32), (D, BF16), (D, F32), (D, F32)]
    return _rows_call(body, "layer_b_in_fwd", s, [x2], [kv_norm, b_norm, w_kv, w_bin], row_outs)


def _tri_mats():
    r = lax.broadcasted_iota(jnp.int32, (BLK, 2 * BLK), 0)
    c = lax.broadcasted_iota(jnp.int32, (BLK, 2 * BLK), 1)
    after = jnp.where((c >= BLK) | (r > c), 1.0, 0.0).astype(BF16)
    before = jnp.where((c >= BLK) | (r < c), 1.0, 0.0).astype(BF16)
    return after, before


def _head_norm(x):
    r = _rms(x)
    return x * r, r


def _block_logits(qblk, kblk, r0, c0):
    l = _dot_nt(qblk, kblk)
    row = r0 + lax.broadcasted_iota(jnp.int32, (BLK, BLK), 0)
    col = c0 + lax.broadcasted_iota(jnp.int32, (BLK, BLK), 1)
    mask = col < row
    sp = jnp.maximum(l, 0.0) + jnp.log(1.0 + jnp.exp(-jnp.abs(l)))
    lk = jnp.where(mask, -sp, 0.0)
    ls = l - sp
    return mask, lk, ls


def _attn_fwd(q_all, k_all, v_all, q_gain, k_gain):
    s = q_all.shape[0]
    nqb = s // BLK

    def body(q_ref, k_ref, v_ref, qg_ref, kg_ref, o_ref, c_ref, qs, ks, vs, acc, right, cmat):
        after, _ = _tri_mats()
        lane = lax.broadcasted_iota(jnp.int32, (BLK, BLK), 1)
        for hh in range(2):
            sl = slice(hh * HEAD_DIM, (hh + 1) * HEAD_DIM)
            qn, _ = _head_norm(q_ref[:, sl])
            kn, _ = _head_norm(k_ref[:, sl])
            qs[...] = (qn * qg_ref[...] * SB_SCALE).astype(BF16)
            ks[...] = (kn * kg_ref[...]).astype(BF16)
            vs[...] = v_ref[:, sl]

            def q_step(qb, _):
                r0 = pl.multiple_of(qb * BLK, BLK)
                qblk = qs[pl.ds(r0, BLK), :]
                acc[...] = jnp.zeros((BLK, HEAD_DIM), F32)
                right[...] = jnp.zeros((BLK, BLK), F32)
                cmat[...] = jnp.zeros((BLK, BLK), F32)

                def k_step(it, _):
                    kb = qb - it
                    c0 = pl.multiple_of(kb * BLK, BLK)
                    mask, lk, ls = _block_logits(qblk, ks[pl.ds(c0, BLK), :], r0, c0)
                    sums = _split_dot(lk, after)
                    rt = right[...]
                    a = jnp.where(mask, jnp.exp(ls + sums[:, :BLK] + rt), 0.0)
                    acc[...] += _dot(a.astype(BF16), vs[pl.ds(c0, BLK), :])
                    cmat[...] = jnp.where(lane == kb, rt, cmat[...])
                    right[...] = rt + sums[:, BLK:]
                    return 0

                lax.fori_loop(0, qb + 1, k_step, 0)
                o_ref[pl.ds(r0, BLK), sl] = acc[...]
                c_ref[hh, pl.ds(r0, BLK), :] = cmat[...]
                return 0

            lax.fori_loop(0, nqb, q_step, 0)

    pair = pl.BlockSpec((s, 2 * HEAD_DIM), lambda h: (0, h))
    gain = pl.BlockSpec((1, HEAD_DIM), lambda h: (0, 0))
    return pl.pallas_call(
        body, name="attn_fwd", grid=(HEADS // 2,),
        in_specs=[pair, pair, pair, gain, gain],
        out_specs=[pair, pl.BlockSpec((2, s, BLK), lambda h: (h, 0, 0))],
        out_shape=[jax.ShapeDtypeStruct((s, D), F32), jax.ShapeDtypeStruct((HEADS, s, BLK), F32)],
        scratch_shapes=[pltpu.VMEM((s, HEAD_DIM), BF16)] * 3
        + [pltpu.VMEM((BLK, HEAD_DIM), F32), pltpu.VMEM((BLK, BLK), F32), pltpu.VMEM((BLK, BLK), F32)],
        compiler_params=pltpu.CompilerParams(dimension_semantics=("arbitrary",), vmem_limit_bytes=VMEM_LIMIT),
    )(q_all, k_all, v_all, q_gain, k_gain)


def _layer_b_out_fwd(o, zb, x2, p1, target, w_out, w_ple, w_gate):
    s = o.shape[0]

    def body(o_ref, zb_ref, x2_ref, p_ref, t_ref, wout_ref, wple_ref, wgate_ref,
             yb_ref, x3_ref, e_ref, gt_ref, dx4_ref, loss_ref):
        zb = zb_ref[...]
        yb = (o_ref[...] * (zb * _sigmoid(zb))).astype(BF16)
        yb_ref[...] = yb
        x3 = x2_ref[...] + _dot(yb, wout_ref[...])
        x3_ref[...] = x3
        x4 = _ple_fwd(p_ref, x3, wple_ref, wgate_ref, e_ref, gt_ref)
        d = x4 - t_ref[...]
        dx4_ref[...] = d * (1.0 / D)

        @pl.when(pl.program_id(0) == 0)
        def _():
            loss_ref[...] = jnp.zeros((1, D), F32)

        loss_ref[...] += jnp.sum(d * d, axis=0, keepdims=True)

    row_outs = [(D, BF16), (D, F32), (D, F32), (D, F32), (D, F32)]
    return _rows_call(body, "layer_b_out_fwd", s, [o, zb, x2, p1, target], [w_out, w_ple, w_gate], row_outs,
                      const_outs=[((1, D), F32)])


def _ple_bwd(dxo, e_ref, gt_ref, wgate_ref, de_ref, dgp_ref):
    e = e_ref[...]
    gt = gt_ref[...]
    de_ref[...] = (dxo * gt).astype(BF16)
    dgp = (dxo * e * gt * (1.0 - gt)).astype(BF16)
    dgp_ref[...] = dgp
    return dxo + _dot_nt(dgp, wgate_ref[...])


def _silu_grads(z):
    sg = _sigmoid(z)
    return z * sg, sg * (1.0 + z * (1.0 - sg))


def _layer_b_out_bwd(dx4, e1, gt1, o, zb, w_gate, w_out):
    s = dx4.shape[0]

    def body(dx4_ref, e_ref, gt_ref, o_ref, zb_ref, wgate_ref, wout_ref,
             de_ref, dgp_ref, dx3_ref, do_ref, dzb_ref):
        dx3 = _ple_bwd(dx4_ref[...], e_ref, gt_ref, wgate_ref, de_ref, dgp_ref)
        dx3_ref[...] = dx3
        dyb = _dot_nt(dx3.astype(BF16), wout_ref[...])
        silu, dsilu = _silu_grads(zb_ref[...])
        do_ref[...] = (dyb * silu).astype(BF16)
        dzb_ref[...] = (dyb * o_ref[...] * dsilu).astype(BF16)

    row_outs = [(D, BF16), (D, BF16), (D, F32), (D, BF16), (D, BF16)]
    return _rows_call(body, "layer_b_out_bwd", s, [dx4, e1, gt1, o, zb], [w_gate, w_out], row_outs)


def _attn_bwd(q_all, k_all, v_all, q_gain, k_gain, d_o, csave):
    s = q_all.shape[0]
    nqb = s // BLK

    def body(q_ref, k_ref, v_ref, qg_ref, kg_ref, do_ref, c_ref,
             dq_ref, dk_ref, dv_ref, dqg_ref, dkg_ref, qs, ks, vs, dos, dqa, dka, dva, dqb, left):
        after, before = _tri_mats()
        lane = lax.broadcasted_iota(jnp.int32, (BLK, BLK), 1)
        for hh in range(2):
            sl = slice(hh * HEAD_DIM, (hh + 1) * HEAD_DIM)
            qn, qr = _head_norm(q_ref[:, sl])
            kn, kr = _head_norm(k_ref[:, sl])
            qs[...] = (qn * qg_ref[...] * SB_SCALE).astype(BF16)
            ks[...] = (kn * kg_ref[...]).astype(BF16)
            vs[...] = v_ref[:, sl]
            dos[...] = do_ref[:, sl]
            dka[...] = jnp.zeros((s, HEAD_DIM), F32)
            dva[...] = jnp.zeros((s, HEAD_DIM), F32)

            def q_step(qb, _):
                r0 = pl.multiple_of(qb * BLK, BLK)
                qblk = qs[pl.ds(r0, BLK), :]
                doblk = dos[pl.ds(r0, BLK), :]
                dqb[...] = jnp.zeros((BLK, HEAD_DIM), F32)
                left[...] = jnp.zeros((BLK, BLK), F32)

                def k_step(kb, _):
                    c0 = pl.multiple_of(kb * BLK, BLK)
                    rows = pl.ds(c0, BLK)
                    kblk = ks[rows, :]
                    mask, lk, ls = _block_logits(qblk, kblk, r0, c0)
                    beta = jnp.exp(ls)
                    right = jnp.sum(jnp.where(lane == kb, c_ref[hh, pl.ds(r0, BLK), :], 0.0), axis=1, keepdims=True)
                    a = jnp.where(mask, jnp.exp(ls + _split_dot(lk, after[:, :BLK]) + right), 0.0)
                    g = a * _dot_nt(doblk, vs[rows, :])
                    sums = _split_dot(g, before)
                    lf = left[...]
                    dl = jnp.where(mask, g * (1.0 - beta) - (sums[:, :BLK] + lf) * beta, 0.0).astype(BF16)
                    left[...] = lf + sums[:, BLK:]
                    dka[rows, :] += _dot_tn(dl, qblk)
                    dva[rows, :] += _dot_tn(a.astype(BF16), doblk)
                    dqb[...] += _dot(dl, kblk)
                    return 0

                lax.fori_loop(0, qb + 1, k_step, 0)
                dqa[pl.ds(r0, BLK), :] = dqb[...] * SB_SCALE
                return 0

            lax.fori_loop(0, nqb, q_step, 0)

            def norm_bwd(dy, xn, r, g_ref, dx_ref, dg_ref):
                dg_ref[hh:hh + 1, :] = jnp.sum(dy * xn, axis=0, keepdims=True)
                dxn = dy * g_ref[...]
                dx_ref[:, sl] = r * (dxn - xn * jnp.mean(dxn * xn, axis=-1, keepdims=True))

            norm_bwd(dqa[...], qn, qr, qg_ref, dq_ref, dqg_ref)
            norm_bwd(dka[...], kn, kr, kg_ref, dk_ref, dkg_ref)
            dv_ref[:, sl] = dva[...]

    pair = pl.BlockSpec((s, 2 * HEAD_DIM), lambda h: (0, h))
    gain = pl.BlockSpec((1, HEAD_DIM), lambda h: (0, 0))
    dgain = pl.BlockSpec((None, 2, HEAD_DIM), lambda h: (h, 0, 0))
    return pl.pallas_call(
        body, name="attn_bwd", grid=(HEADS // 2,),
        in_specs=[pair, pair, pair, gain, gain, pair, pl.BlockSpec((2, s, BLK), lambda h: (h, 0, 0))],
        out_specs=[pair, pair, pair, dgain, dgain],
        out_shape=[jax.ShapeDtypeStruct((s, D), F32)] * 3
        + [jax.ShapeDtypeStruct((HEADS // 2, 2, HEAD_DIM), F32)] * 2,
        scratch_shapes=[pltpu.VMEM((s, HEAD_DIM), BF16)] * 4 + [pltpu.VMEM((s, HEAD_DIM), F32)] * 3
        + [pltpu.VMEM((BLK, HEAD_DIM), F32), pltpu.VMEM((BLK, BLK), F32)],
        compiler_params=pltpu.CompilerParams(dimension_semantics=("arbitrary",), vmem_limit_bytes=VMEM_LIMIT),
    )(q_all, k_all, v_all, q_gain, k_gain, d_o, csave)


def _norm_bwd_rows(dh, x, gain, dgain_ref):
    r = _rms(x)
    n = x * r
    dgain_ref[...] += jnp.sum(dh * n, axis=0, keepdims=True)
    dn = dh * gain
    return r * (dn - n * jnp.mean(dn * n, axis=-1, keepdims=True))


def _layer_b_in_bwd(dq, dzb, dk, dv, x2, dx3, w_bin, w_kv, b_norm, kv_norm):
    s = x2.shape[0]

    def body(dq_ref, dzb_ref, dk_ref, dv_ref, x_ref, dx3_ref, wbin_ref, wkv_ref, bn_ref, kvn_ref,
             dqz_ref, dkv_ref, dx2_ref, dbn_ref, dkvn_ref):
        @pl.when(pl.program_id(0) == 0)
        def _():
            dbn_ref[...] = jnp.zeros((1, D), F32)
            dkvn_ref[...] = jnp.zeros((1, D), F32)

        dqz_ref[:, :D] = dq_ref[...].astype(BF16)
        dqz_ref[:, D:] = dzb_ref[...]
        dkv_ref[:, :D] = dk_ref[...].astype(BF16)
        dkv_ref[:, D:] = dv_ref[...].astype(BF16)
        dhb = jnp.zeros((TM, D), F32)
        dhkv = jnp.zeros((TM, D), F32)
        for j in range(N_DEV):
            cols = slice(j * 256, (j + 1) * 256)
            dhb = dhb + _dot_nt(dqz_ref[:, cols], wbin_ref[j])
            dhkv = dhkv + _dot_nt(dkv_ref[:, cols], wkv_ref[j])
        x = x_ref[...]
        dx2 = dx3_ref[...] + _norm_bwd_rows(dhb, x, bn_ref[...], dbn_ref)
        dx2_ref[...] = dx2 + _norm_bwd_rows(dhkv, x, kvn_ref[...], dkvn_ref)

    row_outs = [(2 * D, BF16), (2 * D, BF16), (D, F32)]
    return _rows_call(body, "layer_b_in_bwd", s, [dq, dzb, dk, dv, x2, dx3], [w_bin, w_kv, b_norm, kv_norm],
                      row_outs, const_outs=[((1, D), F32), ((1, D), F32)])


def _layer_a_out_bwd(dx2, e0, gt0, z, m, w_gate, w_out, a_scale, w_group):
    s = dx2.shape[0]
    tm = TM
    nb = s // tm

    def body(dx2_ref, e_ref, gt_ref, z_ref, m_ref, wgate_ref, wout_ref, as_ref, wg_ref,
             de_ref, dgp_ref, dx1_ref, dm_ref, duz_ref, das_ref, ext):
        i = pl.program_id(0)

        @pl.when(i == 0)
        def _():
            das_ref[...] = jnp.zeros((1, D), F32)
            ext[tm:tm + HALO, :] = jnp.zeros((HALO, D), F32)

        dx1 = _ple_bwd(dx2_ref[...], e_ref, gt_ref, wgate_ref, de_ref, dgp_ref)
        dx1_ref[...] = dx1
        dy = _dot_nt(dx1.astype(BF16), wout_ref[...])
        silu, dsilu = _silu_grads(z_ref[...])
        m = m_ref[...]
        dmixed = dy * silu
        duz_ref[:, D:] = (dy * (m * as_ref[...]) * dsilu).astype(BF16)
        das_ref[...] += jnp.sum(dmixed * m, axis=0, keepdims=True)
        dm_ref[...] = (dmixed * as_ref[...]).astype(BF16)
        t = (nb - 1 - i) * tm + lax.broadcasted_iota(jnp.int32, (tm, 1), 0)
        n_ext = tm + HALO
        for g in range(N_GROUPS):
            w = 2 ** (g + 1)
            cols = slice(g * GROUP_DIM, (g + 1) * GROUP_DIM)
            dpool = _dot_nt(dm_ref[:, cols], wg_ref[g])
            ext[0:tm, cols] = dpool / jnp.minimum(t + 1, w).astype(F32)
            acc = ext[:, cols]
            k = 1
            while k < w:
                acc = acc + pltpu.roll(acc, n_ext - k, 0)
                k *= 2
            duz_ref[:, cols] = (acc[:tm] - dpool).astype(BF16)
        ext[tm:tm + HALO, :] = ext[0:HALO, :]

    row_outs = [(D, BF16), (D, BF16), (D, F32), (D, BF16), (2 * D, BF16)]
    return _rows_call(body, "layer_a_out_bwd", s, [dx2, e0, gt0, z, m], [w_gate, w_out, a_scale, w_group],
                      row_outs, const_outs=[((1, D), F32)], scratch=[pltpu.VMEM((tm + HALO, D), F32)],
                      reverse=True)


def _layer_a_in_bwd(duz, x0, dx1, w_in, a_norm):
    s = x0.shape[0]

    def body(duz_ref, x_ref, dx1_ref, win_ref, an_ref, dx0_ref, dan_ref):
        @pl.when(pl.program_id(0) == 0)
        def _():
            dan_ref[...] = jnp.zeros((1, D), F32)

        dh = jnp.zeros((TM, D), F32)
        for j in range(N_DEV):
            dh = dh + _dot_nt(duz_ref[:, j * 256:(j + 1) * 256], win_ref[j])
        dx0_ref[...] = dx1_ref[...] + _norm_bwd_rows(dh, x_ref[...], an_ref[...], dan_ref)

    return _rows_call(body, "layer_a_in_bwd", s, [duz, x0, dx1], [w_in, a_norm], [(D, F32)],
                      const_outs=[((1, D), F32)])


def _wgrad(a, b, name, n_split=1, a_blocked_b=False):
    s, k = a.shape
    n = b.shape[1]
    tk = 256
    nb = n // n_split

    def body(a_ref, b_ref, o_ref):
        res = _dot_tn(a_ref[...].astype(BF16), b_ref[...].astype(BF16))
        if n_split == 1:
            o_ref[...] = res.astype(BF16)
        else:
            for j in range(n_split):
                o_ref[j] = res[:, j * nb:(j + 1) * nb].astype(BF16)

    if a_blocked_b:
        b_spec = pl.BlockSpec((s, tk), lambda i: (0, i))
        out_spec = pl.BlockSpec((None, tk, tk), lambda i: (i, 0, 0))
        out_shape = jax.ShapeDtypeStruct((k // tk, tk, tk), BF16)
    elif n_split == 1:
        b_spec = pl.BlockSpec((s, n), lambda i: (0, 0))
        out_spec = pl.BlockSpec((tk, n), lambda i: (i, 0))
        out_shape = jax.ShapeDtypeStruct((k, n), BF16)
    else:
        b_spec = pl.BlockSpec((s, n), lambda i: (0, 0))
        out_spec = pl.BlockSpec((n_split, tk, nb), lambda i: (0, i, 0))
        out_shape = jax.ShapeDtypeStruct((n_split, k, nb), BF16)
    return pl.pallas_call(
        body, name=name, grid=(k // tk,),
        in_specs=[pl.BlockSpec((s, tk), lambda i: (0, i)), b_spec],
        out_specs=out_spec, out_shape=out_shape,
        compiler_params=pltpu.CompilerParams(dimension_semantics=("arbitrary",), vmem_limit_bytes=VMEM_LIMIT),
    )(a, b)


def _cast_shards(shards):
    n = len(shards)

    def body(*refs):
        for a in range(n):
            refs[n + a][...] = refs[a][...].astype(BF16)

    vmem = pl.BlockSpec(memory_space=pltpu.VMEM)
    return pl.pallas_call(
        body, name="cast_shards", in_specs=[vmem] * n, out_specs=[vmem] * n,
        out_shape=[jax.ShapeDtypeStruct(a.shape, BF16) for a in shards],
        compiler_params=pltpu.CompilerParams(vmem_limit_bytes=VMEM_LIMIT),
    )(*shards)


def _pair_sum(mine, theirs, name):
    _, r, c = mine.shape

    def body(a_ref, b_ref, f_ref, h_ref):
        tot = a_ref[...].astype(F32) + b_ref[...].astype(F32)
        f_ref[...] = tot
        h_ref[...] = tot.astype(BF16)

    spec = pl.BlockSpec((None, r, c), lambda i: (i, 0, 0))
    return pl.pallas_call(
        body, name=name, grid=(4,), in_specs=[spec, spec], out_specs=[spec, spec],
        out_shape=[jax.ShapeDtypeStruct(mine.shape, F32), jax.ShapeDtypeStruct(mine.shape, BF16)],
        compiler_params=pltpu.CompilerParams(dimension_semantics=("arbitrary",)),
    )(mine, theirs)


def _adamw(w, g, m, v):
    m = ADAM_B1 * m + (1.0 - ADAM_B1) * g
    v = ADAM_B2 * v + (1.0 - ADAM_B2) * jnp.square(g)
    m_hat = m / (1.0 - ADAM_B1 ** ADAM_STEP)
    v_hat = v / (1.0 - ADAM_B2 ** ADAM_STEP)
    delta = -ADAM_LR * (m_hat / (jnp.sqrt(v_hat) + ADAM_EPS) + ADAM_WD * w)
    return delta, m, v


def _finish_shard(part, recv, w, m, v, name):
    r, c = part.shape
    tr = min(r, 256)

    def body(p_ref, r_ref, w_ref, m_ref, v_ref, g_ref, d_ref, m2_ref, v2_ref):
        g = p_ref[...]
        for k in range(3):
            g = g + r_ref[k].astype(F32)
        g_ref[...] = g
        d_ref[...], m2_ref[...], v2_ref[...] = _adamw(w_ref[...], g, m_ref[...], v_ref[...])

    spec = pl.BlockSpec((tr, c), lambda i: (i, 0))
    return pl.pallas_call(
        body, name=name, grid=(r // tr,),
        in_specs=[spec, pl.BlockSpec((3, tr, c), lambda i: (0, i, 0)), spec, spec, spec],
        out_specs=[spec] * 4, out_shape=[jax.ShapeDtypeStruct((r, c), F32)] * 4,
        compiler_params=pltpu.CompilerParams(dimension_semantics=("arbitrary",)),
    )(part, recv, w, m, v)


def _adamw_small(w, g, m, v, name):
    def body(w_ref, g_ref, m_ref, v_ref, d_ref, m2_ref, v2_ref):
        d_ref[...], m2_ref[...], v2_ref[...] = _adamw(w_ref[...], g_ref[...], m_ref[...], v_ref[...])

    vmem = pl.BlockSpec(memory_space=pltpu.VMEM)
    return pl.pallas_call(
        body, name=name, in_specs=[vmem] * 4, out_specs=[vmem] * 3,
        out_shape=[jax.ShapeDtypeStruct(w.shape, F32)] * 3,
    )(w, g, m, v)


def _place():
    return lax.axis_index("x"), lax.axis_index("y"), lax.axis_index("c")


def _all_gather(shards):
    n = len(shards)

    def body(*refs):
        ins, outs = refs[:n], refs[n:2 * n]
        send_sems, recv_sems, local_sems = refs[2 * n:]
        x, y, c = _place()
        me, sibling = (x, y, c), (x, y, 1 - c)
        chips = [(1 - x, y), (x, 1 - y), (1 - x, 1 - y)]

        def copy(a, k, block, to, src=None):
            px, py, pc = block
            slot = outs[a].at[4 * px + 2 * py + pc]
            return pltpu.make_async_remote_copy(
                src_ref=slot if src is None else src, dst_ref=slot,
                send_sem=send_sems.at[7 * a + k], recv_sem=recv_sems.at[7 * a + k],
                device_id=to, device_id_type=MESH)

        started = []
        for a in range(n):
            mine = pltpu.make_async_copy(ins[a], outs[a].at[4 * x + 2 * y + c], local_sems.at[a])
            mine.start()
            started.append(mine)
        sends = []
        for a in range(n):
            first = [copy(a, 0, me, sibling, src=ins[a])]
            first += [copy(a, 1 + j, me, (*chip, c), src=ins[a]) for j, chip in enumerate(chips)]
            for cp in first:
                cp.start()
            sends += first
        for a in range(n):
            for j, chip in enumerate(chips):
                copy(a, 1 + j, (*chip, c), me).wait_recv()
                passed = copy(a, 4 + j, (*chip, c), sibling)
                passed.start()
                sends.append(passed)
        for a in range(n):
            copy(a, 0, sibling, me).wait_recv()
            for j, chip in enumerate(chips):
                copy(a, 4 + j, (*chip, 1 - c), me).wait_recv()
        for cp in sends:
            cp.wait_send()
        for mine in started:
            mine.wait()

    return pl.pallas_call(
        body, name="all_gather_weights",
        in_specs=[HBM_SPEC] * n, out_specs=[HBM_SPEC] * n,
        out_shape=[jax.ShapeDtypeStruct((N_DEV,) + a.shape, a.dtype) for a in shards],
        scratch_shapes=[pltpu.SemaphoreType.DMA((7 * n,)), pltpu.SemaphoreType.DMA((7 * n,)),
                        pltpu.SemaphoreType.DMA((n,))],
    )(*shards)


def _exchange_cores(partials):
    n = len(partials)

    def body(*refs):
        ins, outs = refs[:n], refs[n:2 * n]
        send_sems, recv_sems = refs[2 * n:]
        x, y, c = _place()
        copies = []
        for a in range(n):
            for k in range(4):
                copies.append(pltpu.make_async_remote_copy(
                    src_ref=ins[a].at[2 * k + (1 - c)], dst_ref=outs[a].at[k],
                    send_sem=send_sems.at[4 * a + k], recv_sem=recv_sems.at[4 * a + k],
                    device_id=(x, y, 1 - c), device_id_type=MESH))
        for cp in copies:
            cp.start()
        for cp in copies:
            cp.wait_recv()
        for cp in copies:
            cp.wait_send()

    return pl.pallas_call(
        body, name="grad_exchange_cores",
        in_specs=[HBM_SPEC] * n, out_specs=[HBM_SPEC] * n,
        out_shape=[jax.ShapeDtypeStruct((4,) + a.shape[1:], a.dtype) for a in partials],
        scratch_shapes=[pltpu.SemaphoreType.DMA((4 * n,)), pltpu.SemaphoreType.DMA((4 * n,))],
    )(*partials)


def _exchange_chips(partials):
    n = len(partials)

    def body(*refs):
        ins, outs = refs[:n], refs[n:2 * n]
        send_sems, recv_sems = refs[2 * n:]
        x, y, c = _place()
        peers = [(1 - x, y), (x, 1 - y), (1 - x, 1 - y)]
        copies = []
        for a in range(n):
            for k, (px, py) in enumerate(peers):
                copies.append(pltpu.make_async_remote_copy(
                    src_ref=ins[a].at[2 * px + py], dst_ref=outs[a].at[k],
                    send_sem=send_sems.at[3 * a + k], recv_sem=recv_sems.at[3 * a + k],
                    device_id=(px, py, c), device_id_type=MESH))
        for cp in copies:
            cp.start()
        for cp in copies:
            cp.wait_recv()
        for cp in copies:
            cp.wait_send()

    return pl.pallas_call(
        body, name="grad_exchange_chips",
        in_specs=[HBM_SPEC] * n, out_specs=[HBM_SPEC] * n,
        out_shape=[jax.ShapeDtypeStruct((3,) + a.shape[1:], a.dtype) for a in partials],
        scratch_shapes=[pltpu.SemaphoreType.DMA((3 * n,)), pltpu.SemaphoreType.DMA((3 * n,))],
    )(*partials)


def _all_reduce_small(rows, gain_parts):
    def body(rows_ref, dqg_ref, dkg_ref, out_ref, buf, send_sems, recv_sems):
        x, y, c = _place()
        me = 4 * x + 2 * y + c
        buf[0] = rows_ref[...]
        buf[0, 4:5, 0:HEAD_DIM] = jnp.sum(dqg_ref[...].reshape(HEADS, HEAD_DIM), axis=0, keepdims=True)
        buf[0, 5:6, 0:HEAD_DIM] = jnp.sum(dkg_ref[...].reshape(HEADS, HEAD_DIM), axis=0, keepdims=True)
        copies = []
        for r in range(1, N_DEV):
            bx, by, bc = (r >> 2) & 1, (r >> 1) & 1, r & 1
            to = (x ^ bx, y ^ by, c ^ bc)
            copies.append(pltpu.make_async_remote_copy(
                src_ref=buf.at[0], dst_ref=buf.at[r], send_sem=send_sems.at[r - 1], recv_sem=recv_sems.at[r - 1],
                device_id=to, device_id_type=MESH))
        for cp in copies:
            cp.start()
        for cp in copies:
            cp.wait_recv()
        for cp in copies:
            cp.wait_send()
        tot = buf[me]
        for j in range(1, N_DEV):
            tot = tot + buf[j ^ me]
        out_ref[...] = tot
        loss = (0.5 / D) * jnp.sum(tot[6:7, :], axis=1, keepdims=True)
        out_ref[6:7, :] = jnp.broadcast_to(loss, (1, D))

    vmem = pl.BlockSpec(memory_space=pltpu.VMEM)
    return pl.pallas_call(
        body, name="all_reduce_small", in_specs=[vmem] * 3, out_specs=vmem,
        out_shape=jax.ShapeDtypeStruct((8, D), F32),
        scratch_shapes=[pltpu.VMEM((N_DEV, 8, D), F32), pltpu.SemaphoreType.DMA((N_DEV - 1,)),
                        pltpu.SemaphoreType.DMA((N_DEV - 1,))],
    )(rows, *gain_parts)


def kernel(x, p, a_norm, a_w_in, a_w_group, a_scale, a_w_out, kv_norm, w_kv, k_norm, b_norm, b_w_in, b_q_norm, b_w_out, ple_w, ple_gate_w, loss_target, m_a_norm, m_a_w_in, m_a_w_group, m_a_scale, m_a_w_out, m_kv_norm, m_w_kv, m_k_norm, m_b_norm, m_b_w_in, m_b_q_norm, m_b_w_out, m_ple_w, m_ple_gate_w, v_a_norm, v_a_w_in, v_a_w_group, v_a_scale, v_a_w_out, v_kv_norm, v_w_kv, v_k_norm, v_b_norm, v_b_w_in, v_b_q_norm, v_b_w_out, v_ple_w, v_ple_gate_w):
    s = x.shape[1]
    xi, yi, ci = _place()
    me = 4 * xi + 2 * yi + ci
    plane = 2 * xi + yi

    big = {
        "a_w_in": a_w_in.reshape(D, 256), "a_w_group": a_w_group.reshape(128, 256),
        "a_w_out": a_w_out.reshape(128, D), "w_kv": w_kv, "b_w_in": b_w_in.reshape(D, 256),
        "b_w_out": b_w_out.reshape(128, D), "ple_w0": ple_w[0], "ple_w1": ple_w[1],
        "gate0": ple_gate_w[0], "gate1": ple_gate_w[1],
    }
    names = list(big)
    cast = _cast_shards([big[k] for k in names])
    small = jnp.concatenate([a_norm, a_scale, jnp.zeros((6, 128), F32)], axis=0)
    gathered = _all_gather(list(cast) + [small])
    full = dict(zip(names, gathered[:-1]))
    small_all = gathered[-1]
    a_norm_f = small_all[:, 0, :].reshape(1, D)
    a_scale_f = small_all[:, 1, :].reshape(1, D)
    w_a_in, w_kv_f, w_b_in = full["a_w_in"], full["w_kv"], full["b_w_in"]
    w_a_out = full["a_w_out"].reshape(D, D)
    w_b_out = full["b_w_out"].reshape(D, D)
    w_gate0 = full["gate0"].reshape(D, D)
    w_gate1 = full["gate1"].reshape(D, D)
    w_ple0, w_ple1 = full["ple_w0"], full["ple_w1"]
    w_group = full["a_w_group"].reshape(N_DEV, 4, 32, 256).transpose(1, 0, 2, 3).reshape(4, 256, 256)
    kvn, bn = kv_norm.reshape(1, D), b_norm
    kg, qg = k_norm.reshape(1, HEAD_DIM), b_q_norm

    x0, p0, p1, target = x[0], p[0, 0], p[1, 0], loss_target[0]
    h0, z, pooled, mcat, y, x1, e0, gt0, x2 = _layer_a_fwd(
        x0, p0, a_norm_f, a_scale_f, w_a_in, w_group, w_a_out, w_ple0, w_gate0)
    hkv, hb, k_all, v_all, q_all, zb = _layer_b_in_fwd(x2, kvn, bn, w_kv_f, w_b_in)
    o, csave = _attn_fwd(q_all, k_all, v_all, qg, kg)
    yb, x3, e1, gt1, dx4, sq_err = _layer_b_out_fwd(o, zb, x2, p1, target, w_b_out, w_ple1, w_gate1)

    de1, dgp1, dx3, d_o, dzb = _layer_b_out_bwd(dx4, e1, gt1, o, zb, w_gate1, w_b_out)
    dq, dk, dv, dqg, dkg = _attn_bwd(q_all, k_all, v_all, qg, kg, d_o, csave)
    dqz, dkv, dx2, d_bn, d_kvn = _layer_b_in_bwd(dq, dzb, dk, dv, x2, dx3, w_b_in, w_kv_f, bn, kvn)
    de0, dgp0, dx1, dm, duz, d_as = _layer_a_out_bwd(dx2, e0, gt0, z, mcat, w_gate0, w_a_out, a_scale_f, w_group)
    dx0, d_an = _layer_a_in_bwd(duz, x0, dx1, w_a_in, a_norm_f)

    dw_group = _wgrad(pooled, dm, "wgrad_a_w_group", a_blocked_b=True)
    partial = {
        "a_w_in": _wgrad(h0, duz, "wgrad_a_w_in", n_split=8),
        "a_w_group": dw_group.reshape(4, N_DEV, 32, 256).transpose(1, 0, 2, 3).reshape(N_DEV, 128, 256),
        "a_w_out": _wgrad(y, dx1, "wgrad_a_w_out").reshape(N_DEV, 128, D),
        "w_kv": _wgrad(hkv, dkv, "wgrad_w_kv", n_split=8),
        "b_w_in": _wgrad(hb, dqz, "wgrad_b_w_in", n_split=8),
        "b_w_out": _wgrad(yb, dx3, "wgrad_b_w_out").reshape(N_DEV, 128, D),
        "ple_w0": _wgrad(p0, de0, "wgrad_ple_w0", n_split=8),
        "ple_w1": _wgrad(p1, de1, "wgrad_ple_w1", n_split=8),
        "gate0": _wgrad(x1, dgp0, "wgrad_gate0").reshape(N_DEV, 128, D),
        "gate1": _wgrad(x3, dgp1, "wgrad_gate1").reshape(N_DEV, 128, D),
    }

    parts = [partial[k] for k in names]
    from_core = _exchange_cores(parts)
    chip_f32, chip_bf16 = [], []
    for k, mine, theirs in zip(names, parts, from_core):
        own = lax.dynamic_index_in_dim(mine.reshape((4, 2) + mine.shape[1:]), ci, axis=1, keepdims=False)
        f, hlf = _pair_sum(own, theirs, "pair_sum_" + k)
        chip_f32.append(f)
        chip_bf16.append(hlf)
    from_chips = _exchange_chips(chip_bf16)
    mom = {
        "a_w_in": (m_a_w_in, v_a_w_in), "a_w_group": (m_a_w_group, v_a_w_group), "a_w_out": (m_a_w_out, v_a_w_out),
        "w_kv": (m_w_kv, v_w_kv), "b_w_in": (m_b_w_in, v_b_w_in), "b_w_out": (m_b_w_out, v_b_w_out),
        "ple_w0": (m_ple_w[0], v_ple_w[0]), "ple_w1": (m_ple_w[1], v_ple_w[1]),
        "gate0": (m_ple_gate_w[0], v_ple_gate_w[0]), "gate1": (m_ple_gate_w[1], v_ple_gate_w[1]),
    }
    res = {}
    for k, f, recv in zip(names, chip_f32, from_chips):
        part = lax.dynamic_index_in_dim(f, plane, axis=0, keepdims=False)
        shp = big[k].shape
        res[k] = _finish_shard(part, recv, big[k], mom[k][0].reshape(shp), mom[k][1].reshape(shp), "finish_" + k)

    rows = jnp.concatenate([d_kvn, d_bn, d_an, d_as, jnp.zeros((2, D), F32), sq_err, jnp.zeros((1, D), F32)], axis=0)
    tot = _all_reduce_small(rows, (dqg, dkg))
    loss = tot[6, 0]
    g_kvn, g_bn = tot[0:1], tot[1:2]
    g_an = lax.dynamic_slice_in_dim(tot[2:3], me * 128, 128, axis=1)
    g_as = lax.dynamic_slice_in_dim(tot[3:4], me * 128, 128, axis=1)
    g_qg, g_kg = tot[4:5, :HEAD_DIM], tot[5:6, :HEAD_DIM]
    sm = {
        "a_norm": (g_an,) + _adamw_small(a_norm, g_an, m_a_norm, v_a_norm, "adamw_a_norm"),
        "a_scale": (g_as,) + _adamw_small(a_scale, g_as, m_a_scale, v_a_scale, "adamw_a_scale"),
        "kv_norm": tuple(t.reshape(D) for t in (g_kvn,) + _adamw_small(
            kvn, g_kvn, m_kv_norm.reshape(1, D), v_kv_norm.reshape(1, D), "adamw_kv_norm")),
        "k_norm": tuple(t.reshape(HEAD_DIM) for t in (g_kg,) + _adamw_small(
            kg, g_kg, m_k_norm.reshape(1, HEAD_DIM), v_k_norm.reshape(1, HEAD_DIM), "adamw_k_norm")),
        "b_norm": (g_bn,) + _adamw_small(b_norm, g_bn, m_b_norm, v_b_norm, "adamw_b_norm"),
        "b_q_norm": (g_qg,) + _adamw_small(b_q_norm, g_qg, m_b_q_norm, v_b_q_norm, "adamw_b_q_norm"),
    }

    def out(kind):
        def big_one(k, shape):
            return res[k][kind].reshape(shape)

        return [
            sm["a_norm"][kind], big_one("a_w_in", a_w_in.shape), big_one("a_w_group", a_w_group.shape),
            sm["a_scale"][kind], big_one("a_w_out", a_w_out.shape), sm["kv_norm"][kind],
            big_one("w_kv", w_kv.shape), sm["k_norm"][kind], sm["b_norm"][kind],
            big_one("b_w_in", b_w_in.shape), sm["b_q_norm"][kind], big_one("b_w_out", b_w_out.shape),
            jnp.stack([res["ple_w0"][kind], res["ple_w1"][kind]]),
            jnp.stack([res["gate0"][kind], res["gate1"][kind]]),
        ]

    return (loss, dx0.reshape(x.shape), *out(0), *out(1), *out(2), *out(3))
```

```python
import functools

import jax
import jax.numpy as jnp
from jax import lax
from jax.experimental import pallas as pl
from jax.experimental.pallas import tpu as pltpu

F32 = jnp.float32
BF16 = jnp.bfloat16
MESH = pl.DeviceIdType.MESH

N_DEV = 8
D = 1024
N_GROUPS = 4
GROUP_DIM = D // N_GROUPS
HALO = 16
HEADS = 16
HEAD_DIM = D // HEADS
SB_SCALE = HEAD_DIM ** -0.5
TILE = 256
LANES = 128
EPS = 1e-6
ADAM_LR = 0.001
ADAM_B1 = 0.9
ADAM_B2 = 0.999
ADAM_EPS = 1e-08
ADAM_WD = 0.01
ADAM_STEP = 10
TM = 256
VMEM_LIMIT = 56 * 1024 * 1024

HBM_SPEC = pl.BlockSpec(memory_space=pltpu.HBM)


def _dot(a, b):
    return jnp.dot(a, b, preferred_element_type=F32)


def _dot_nt(a, b):
    return lax.dot_general(a, b, (((1,), (1,)), ((), ())), preferred_element_type=F32)


def _dot_tn(a, b):
    return lax.dot_general(a, b, (((0,), (0,)), ((), ())), preferred_element_type=F32)


def _sigmoid(x):
    return jax.nn.sigmoid(x)


def _split_dot(x, mat):
    hi = x.astype(BF16)
    lo = (x - hi.astype(F32)).astype(BF16)
    return _dot(hi, mat) + _dot(lo, mat)


def _rms(x):
    return lax.rsqrt(jnp.mean(x * x, axis=-1, keepdims=True) + EPS)


def _rows_call(body, name, n_rows, row_ins, const_ins, row_outs, const_outs=(), scratch=(),
               reverse=False, tm=TM):
    nb = n_rows // tm

    def row_map(i):
        return ((nb - 1 - i) if reverse else i, 0)

    def const_map(nd):
        return lambda i: (0,) * nd

    in_specs = [pl.BlockSpec((tm, a.shape[1]), row_map) for a in row_ins]
    in_specs += [pl.BlockSpec(a.shape, const_map(a.ndim)) for a in const_ins]
    out_specs = [pl.BlockSpec((tm, w), row_map) for (w, _) in row_outs]
    out_specs += [pl.BlockSpec(s, const_map(len(s))) for (s, _) in const_outs]
    out_shape = [jax.ShapeDtypeStruct((n_rows, w), dt) for (w, dt) in row_outs]
    out_shape += [jax.ShapeDtypeStruct(s, dt) for (s, dt) in const_outs]
    return pl.pallas_call(
        body, name=name, grid=(nb,), in_specs=in_specs, out_specs=out_specs, out_shape=out_shape,
        scratch_shapes=list(scratch),
        compiler_params=pltpu.CompilerParams(dimension_semantics=("arbitrary",),
                                             vmem_limit_bytes=VMEM_LIMIT),
    )(*row_ins, *const_ins)


def _ple_fwd(p_ref, xin, wple_ref, wgate_ref, e_ref, gt_ref):
    pb = p_ref[...].astype(BF16)
    for j in range(N_DEV):
        e_ref[:, j * 128:(j + 1) * 128] = _dot(pb, wple_ref[j])
    gt = _sigmoid(_dot(xin.astype(BF16), wgate_ref[...]))
    gt_ref[...] = gt
    return xin + e_ref[...] * gt


def _layer_a_fwd(x0, p0, a_norm, a_scale, w_in, w_group, w_out, w_ple, w_gate):
    s = x0.shape[0]
    tm = TM

    def body(x_ref, p_ref, an_ref, as_ref, win_ref, wg_ref, wout_ref, wple_ref, wgate_ref,
             h_ref, z_ref, pooled_ref, m_ref, y_ref, x1_ref, e_ref, gt_ref, x2_ref, uext):
        i = pl.program_id(0)

        @pl.when(i == 0)
        def _():
            uext[0:HALO, :] = jnp.zeros((HALO, D), F32)

        x = x_ref[...]
        h = (x * _rms(x) * an_ref[...]).astype(BF16)
        h_ref[...] = h
        for j in range(N_DEV):
            uz = _dot(h, win_ref[j])
            if j < 4:
                uext[HALO:HALO + tm, j * 256:(j + 1) * 256] = uz
            else:
                z_ref[:, (j - 4) * 256:(j - 3) * 256] = uz
        t = i * tm + lax.broadcasted_iota(jnp.int32, (tm, 1), 0)
        for g in range(N_GROUPS):
            w = 2 ** (g + 1)
            cols = slice(g * GROUP_DIM, (g + 1) * GROUP_DIM)
            ext = uext[:, cols]
            acc = ext
            k = 1
            while k < w:
                acc = acc + pltpu.roll(acc, k, 0)
                k *= 2
            cnt = jnp.minimum(t + 1, w).astype(F32)
            pooled = (acc[HALO:] / cnt - ext[HALO:]).astype(BF16)
            pooled_ref[:, cols] = pooled
            m_ref[:, cols] = _dot(pooled, wg_ref[g])
        uext[0:HALO, :] = uext[tm:tm + HALO, :]
        z = z_ref[...]
        y = (m_ref[...] * as_ref[...] * (z * _sigmoid(z))).astype(BF16)
        y_ref[...] = y
        x1 = x + _dot(y, wout_ref[...])
        x1_ref[...] = x1
        x2_ref[...] = _ple_fwd(p_ref, x1, wple_ref, wgate_ref, e_ref, gt_ref)

    row_outs = [(D, BF16), (D, F32), (D, BF16), (D, F32), (D, BF16), (D, F32), (D, F32), (D, F32), (D, F32)]
    return _rows_call(body, "layer_a_fwd", s, [x0, p0], [a_norm, a_scale, w_in, w_group, w_out, w_ple, w_gate],
                      row_outs, scratch=[pltpu.VMEM((tm + HALO, D), F32)])


def _layer_b_in_fwd(x2, kv_norm, b_norm, w_kv, w_bin):
    s = x2.shape[0]

    def body(x_ref, kvn_ref, bn_ref, wkv_ref, wbin_ref, hkv_ref, hb_ref, k_ref, v_ref, q_ref, zb_ref):
        x = x_ref[...]
        n = x * _rms(x)
        hkv = (n * kvn_ref[...]).astype(BF16)
        hb = (n * bn_ref[...]).astype(BF16)
        hkv_ref[...] = hkv
        hb_ref[...] = hb
        for j in range(N_DEV):
            kv = _dot(hkv, wkv_ref[j])
            qz = _dot(hb, wbin_ref[j])
            if j < 4:
                cols = slice(j * 256, (j + 1) * 256)
                k_ref[:, cols] = kv
                q_ref[:, cols] = qz
            else:
                cols = slice((j - 4) * 256, (j - 3) * 256)
                v_ref[:, cols] = kv.astype(BF16)
                zb_ref[:, cols] = qz

    row_outs = [(D, BF16), (D, BF16), (D, F32), (D, BF16), (D, F32), (D, F32)]
    return _rows_call(body, "layer_b_in_fwd", s, [x2], [kv_norm, b_norm, w_kv, w_bin], row_outs)


def _tri(after):
    r = lax.broadcasted_iota(jnp.int32, (TILE, TILE), 0)
    c = lax.broadcasted_iota(jnp.int32, (TILE, TILE), 1)
    return jnp.where((r > c) if after else (r < c), 1.0, 0.0).astype(BF16)


def _half_sums(v):
    lo = lax.broadcasted_iota(jnp.int32, v.shape, 1) < HEAD_DIM
    s_lo = jnp.sum(jnp.where(lo, v, 0.0), axis=1, keepdims=True)
    s_hi = jnp.sum(jnp.where(lo, 0.0, v), axis=1, keepdims=True)
    return jnp.where(lo, s_lo, s_hi)


def _pair_norm(x):
    r = lax.rsqrt(_half_sums(x * x) * (1.0 / HEAD_DIM) + EPS)
    return x * r, r


def _tile_logits(qblk, kblk, diagonal):
    l = _dot_nt(qblk, kblk)
    sp = jnp.maximum(l, 0.0) + jnp.log(1.0 + jnp.exp(-jnp.abs(l)))
    ls = l - sp
    if not diagonal:
        return None, -sp, ls
    mask = lax.broadcasted_iota(jnp.int32, l.shape, 1) < lax.broadcasted_iota(jnp.int32, l.shape, 0)
    return mask, jnp.where(mask, -sp, 0.0), ls


def _attn_fwd(q_all, k_all, v_all, q_gain2, k_gain2):
    s = q_all.shape[0]
    nt = s // TILE

    def body(q_ref, k_ref, v_ref, qg_ref, kg_ref, o_ref, c_ref, qs0, qs1, ks0, ks1, vs0, vs1, tri, acc, right, cmat):
        tri[...] = _tri(True)
        lane = lax.broadcasted_iota(jnp.int32, (TILE, LANES), 1)
        qn, _ = _pair_norm(q_ref[...])
        kn, _ = _pair_norm(k_ref[...])
        qsc = (qn * qg_ref[...] * SB_SCALE).astype(BF16)
        ksc = (kn * kg_ref[...]).astype(BF16)
        for hh, (qs, ks, vs) in enumerate(((qs0, ks0, vs0), (qs1, ks1, vs1))):
            sl = slice(hh * HEAD_DIM, (hh + 1) * HEAD_DIM)
            qs[...] = qsc[:, sl]
            ks[...] = ksc[:, sl]
            vs[...] = v_ref[:, sl]

            def tile(qblk, kb, diagonal):
                rows = pl.ds(pl.multiple_of(kb * TILE, TILE), TILE)
                mask, lk, ls = _tile_logits(qblk, ks[rows, :], diagonal)
                rt = right[...]
                a = jnp.exp(ls + _split_dot(lk, tri[...]) + rt)
                if diagonal:
                    a = jnp.where(mask, a, 0.0)
                acc[...] += _dot(a.astype(BF16), vs[rows, :])
                cmat[...] = jnp.where(lane == kb, rt[:, :LANES], cmat[...])
                right[...] = rt + jnp.sum(lk, axis=1, keepdims=True)

            def q_step(qb, _):
                r0 = pl.multiple_of(qb * TILE, TILE)
                qblk = qs[pl.ds(r0, TILE), :]
                acc[...] = jnp.zeros((TILE, HEAD_DIM), F32)
                right[...] = jnp.zeros((TILE, TILE), F32)
                cmat[...] = jnp.zeros((TILE, LANES), F32)
                tile(qblk, qb, True)

                def k_step(it, _):
                    tile(qblk, qb - 1 - it, False)
                    return 0

                lax.fori_loop(0, qb, k_step, 0)
                o_ref[pl.ds(r0, TILE), sl] = acc[...]
                c_ref[hh, pl.ds(r0, TILE), :] = cmat[...]
                return 0

            lax.fori_loop(0, nt, q_step, 0)

    pair = pl.BlockSpec((s, LANES), lambda h: (0, h))
    gain = pl.BlockSpec((1, LANES), lambda h: (0, 0))
    return pl.pallas_call(
        body, name="attn_fwd", grid=(HEADS // 2,),
        in_specs=[pair, pair, pair, gain, gain],
        out_specs=[pair, pl.BlockSpec((2, s, LANES), lambda h: (h, 0, 0))],
        out_shape=[jax.ShapeDtypeStruct((s, D), F32), jax.ShapeDtypeStruct((HEADS, s, LANES), F32)],
        scratch_shapes=[pltpu.VMEM((s, HEAD_DIM), BF16)] * 6
        + [pltpu.VMEM((TILE, TILE), BF16), pltpu.VMEM((TILE, HEAD_DIM), F32), pltpu.VMEM((TILE, TILE), F32),
           pltpu.VMEM((TILE, LANES), F32)],
        compiler_params=pltpu.CompilerParams(dimension_semantics=("arbitrary",), vmem_limit_bytes=VMEM_LIMIT),
    )(q_all, k_all, v_all, q_gain2, k_gain2)


def _layer_b_out_fwd(o, zb, x2, p1, target, w_out, w_ple, w_gate):
    s = o.shape[0]

    def body(o_ref, zb_ref, x2_ref, p_ref, t_ref, wout_ref, wple_ref, wgate_ref,
             yb_ref, x3_ref, e_ref, gt_ref, dx4_ref, loss_ref):
        zb = zb_ref[...]
        yb = (o_ref[...] * (zb * _sigmoid(zb))).astype(BF16)
        yb_ref[...] = yb
        x3 = x2_ref[...] + _dot(yb, wout_ref[...])
        x3_ref[...] = x3
        x4 = _ple_fwd(p_ref, x3, wple_ref, wgate_ref, e_ref, gt_ref)
        d = x4 - t_ref[...]
        dx4_ref[...] = d * (1.0 / D)

        @pl.when(pl.program_id(0) == 0)
        def _():
            loss_ref[...] = jnp.zeros((1, D), F32)

        loss_ref[...] += jnp.sum(d * d, axis=0, keepdims=True)

    row_outs = [(D, BF16), (D, F32), (D, F32), (D, F32), (D, F32)]
    return _rows_call(body, "layer_b_out_fwd", s, [o, zb, x2, p1, target], [w_out, w_ple, w_gate], row_outs,
                      const_outs=[((1, D), F32)])


def _ple_bwd(dxo, e_ref, gt_ref, wgate_ref, de_ref, dgp_ref):
    e = e_ref[...]
    gt = gt_ref[...]
    de_ref[...] = (dxo * gt).astype(BF16)
    dgp = (dxo * e * gt * (1.0 - gt)).astype(BF16)
    dgp_ref[...] = dgp
    return dxo + _dot_nt(dgp, wgate_ref[...])


def _silu_grads(z):
    sg = _sigmoid(z)
    return z * sg, sg * (1.0 + z * (1.0 - sg))


def _layer_b_out_bwd(dx4, e1, gt1, o, zb, w_gate, w_out):
    s = dx4.shape[0]

    def body(dx4_ref, e_ref, gt_ref, o_ref, zb_ref, wgate_ref, wout_ref,
             de_ref, dgp_ref, dx3_ref, do_ref, dzb_ref):
        dx3 = _ple_bwd(dx4_ref[...], e_ref, gt_ref, wgate_ref, de_ref, dgp_ref)
        dx3_ref[...] = dx3
        dyb = _dot_nt(dx3.astype(BF16), wout_ref[...])
        silu, dsilu = _silu_grads(zb_ref[...])
        do_ref[...] = (dyb * silu).astype(BF16)
        dzb_ref[...] = (dyb * o_ref[...] * dsilu).astype(BF16)

    row_outs = [(D, BF16), (D, BF16), (D, F32), (D, BF16), (D, BF16)]
    return _rows_call(body, "layer_b_out_bwd", s, [dx4, e1, gt1, o, zb], [w_gate, w_out], row_outs)


def _attn_bwd(q_all, k_all, v_all, q_gain2, k_gain2, d_o, csave):
    s = q_all.shape[0]
    nt = s // TILE

    def body(q_ref, k_ref, v_ref, qg_ref, kg_ref, do_ref, c_ref,
             dq_ref, dk_ref, dv_ref, dqg_ref, dkg_ref,
             qs0, qs1, ks0, ks1, vs0, vs1, dos0, dos1, qt, dot_t, tri_a, tri_b, dqa, dkt, dvt, dqb, left):
        tri_a[...] = _tri(True)
        tri_b[...] = _tri(False)
        lane = lax.broadcasted_iota(jnp.int32, (TILE, LANES), 1)
        qn, qr = _pair_norm(q_ref[...])
        kn, kr = _pair_norm(k_ref[...])
        qsc = qn * qg_ref[...] * SB_SCALE
        ksc = (kn * kg_ref[...]).astype(BF16)
        q_t = qsc.T.astype(BF16)
        do_t = do_ref[...].astype(F32).T.astype(BF16)
        for j in range(nt):
            qt[j] = q_t[:, j * TILE:(j + 1) * TILE]
            dot_t[j] = do_t[:, j * TILE:(j + 1) * TILE]
        dkt[...] = jnp.zeros((nt, LANES, TILE), F32)
        dvt[...] = jnp.zeros((nt, LANES, TILE), F32)
        qsc = qsc.astype(BF16)
        for hh, (qs, ks, vs, dos) in enumerate(((qs0, ks0, vs0, dos0), (qs1, ks1, vs1, dos1))):
            sl = slice(hh * HEAD_DIM, (hh + 1) * HEAD_DIM)
            qs[...] = qsc[:, sl]
            ks[...] = ksc[:, sl]
            vs[...] = v_ref[:, sl]
            dos[...] = do_ref[:, sl]

            def tile(qb, r0, qblk, doblk, kb, diagonal):
                rows = pl.ds(pl.multiple_of(kb * TILE, TILE), TILE)
                kblk = ks[rows, :]
                mask, lk, ls = _tile_logits(qblk, kblk, diagonal)
                beta = jnp.exp(ls)
                right = jnp.sum(jnp.where(lane == kb, c_ref[hh, pl.ds(r0, TILE), :], 0.0), axis=1, keepdims=True)
                a = jnp.exp(ls + _split_dot(lk, tri_a[...]) + right)
                if diagonal:
                    a = jnp.where(mask, a, 0.0)
                g = a * _dot_nt(doblk, vs[rows, :])
                lf = left[...]
                dl = g * (1.0 - beta) - (_split_dot(g, tri_b[...]) + lf) * beta
                if diagonal:
                    dl = jnp.where(mask, dl, 0.0)
                dl = dl.astype(BF16)
                left[...] = lf + jnp.sum(g, axis=1, keepdims=True)
                dqb[...] += _dot(dl, kblk)
                dkt[kb, sl, :] += _dot(qt[qb, sl, :], dl)
                dvt[kb, sl, :] += _dot(dot_t[qb, sl, :], a.astype(BF16))

            def q_step(qb, _):
                r0 = pl.multiple_of(qb * TILE, TILE)
                qblk = qs[pl.ds(r0, TILE), :]
                doblk = dos[pl.ds(r0, TILE), :]
                dqb[...] = jnp.zeros((TILE, HEAD_DIM), F32)
                left[...] = jnp.zeros((TILE, TILE), F32)

                def k_step(kb, _):
                    tile(qb, r0, qblk, doblk, kb, False)
                    return 0

                lax.fori_loop(0, qb, k_step, 0)
                tile(qb, r0, qblk, doblk, qb, True)
                dqa[pl.ds(r0, TILE), sl] = dqb[...] * SB_SCALE
                return 0

            lax.fori_loop(0, nt, q_step, 0)

        def norm_bwd(dy, xn, r, g_ref, dx_ref, dg_ref):
            dg_ref[...] = jnp.sum(dy * xn, axis=0, keepdims=True)
            dxn = dy * g_ref[...]
            dx_ref[...] = r * (dxn - xn * (_half_sums(dxn * xn) * (1.0 / HEAD_DIM)))

        norm_bwd(dqa[...], qn, qr, qg_ref, dq_ref, dqg_ref)
        for j in range(nt):
            dqa[j * TILE:(j + 1) * TILE, :] = dkt[j].T
            dv_ref[j * TILE:(j + 1) * TILE, :] = dvt[j].T
        norm_bwd(dqa[...], kn, kr, kg_ref, dk_ref, dkg_ref)

    pair = pl.BlockSpec((s, LANES), lambda h: (0, h))
    gain = pl.BlockSpec((1, LANES), lambda h: (0, 0))
    dgain = pl.BlockSpec((None, 1, LANES), lambda h: (h, 0, 0))
    return pl.pallas_call(
        body, name="attn_bwd", grid=(HEADS // 2,),
        in_specs=[pair, pair, pair, gain, gain, pair, pl.BlockSpec((2, s, LANES), lambda h: (h, 0, 0))],
        out_specs=[pair, pair, pair, dgain, dgain],
        out_shape=[jax.ShapeDtypeStruct((s, D), F32)] * 3
        + [jax.ShapeDtypeStruct((HEADS // 2, 1, LANES), F32)] * 2,
        scratch_shapes=[pltpu.VMEM((s, HEAD_DIM), BF16)] * 8
        + [pltpu.VMEM((nt, LANES, TILE), BF16)] * 2 + [pltpu.VMEM((TILE, TILE), BF16)] * 2
        + [pltpu.VMEM((s, LANES), F32)] + [pltpu.VMEM((nt, LANES, TILE), F32)] * 2
        + [pltpu.VMEM((TILE, HEAD_DIM), F32), pltpu.VMEM((TILE, TILE), F32)],
        compiler_params=pltpu.CompilerParams(dimension_semantics=("arbitrary",), vmem_limit_bytes=VMEM_LIMIT),
    )(q_all, k_all, v_all, q_gain2, k_gain2, d_o, csave)


def _norm_bwd_rows(dh, x, gain, dgain_ref):
    r = _rms(x)
    n = x * r
    dgain_ref[...] += jnp.sum(dh * n, axis=0, keepdims=True)
    dn = dh * gain
    return r * (dn - n * jnp.mean(dn * n, axis=-1, keepdims=True))


def _layer_b_in_bwd(dq, dzb, dk, dv, x2, dx3, w_bin, w_kv, b_norm, kv_norm):
    s = x2.shape[0]

    def body(dq_ref, dzb_ref, dk_ref, dv_ref, x_ref, dx3_ref, wbin_ref, wkv_ref, bn_ref, kvn_ref,
             dqz_ref, dkv_ref, dx2_ref, dbn_ref, dkvn_ref):
        @pl.when(pl.program_id(0) == 0)
        def _():
            dbn_ref[...] = jnp.zeros((1, D), F32)
            dkvn_ref[...] = jnp.zeros((1, D), F32)

        dqz_ref[:, :D] = dq_ref[...].astype(BF16)
        dqz_ref[:, D:] = dzb_ref[...]
        dkv_ref[:, :D] = dk_ref[...].astype(BF16)
        dkv_ref[:, D:] = dv_ref[...].astype(BF16)
        dhb = jnp.zeros((TM, D), F32)
        dhkv = jnp.zeros((TM, D), F32)
        for j in range(N_DEV):
            cols = slice(j * 256, (j + 1) * 256)
            dhb = dhb + _dot_nt(dqz_ref[:, cols], wbin_ref[j])
            dhkv = dhkv + _dot_nt(dkv_ref[:, cols], wkv_ref[j])
        x = x_ref[...]
        dx2 = dx3_ref[...] + _norm_bwd_rows(dhb, x, bn_ref[...], dbn_ref)
        dx2_ref[...] = dx2 + _norm_bwd_rows(dhkv, x, kvn_ref[...], dkvn_ref)

    row_outs = [(2 * D, BF16), (2 * D, BF16), (D, F32)]
    return _rows_call(body, "layer_b_in_bwd", s, [dq, dzb, dk, dv, x2, dx3], [w_bin, w_kv, b_norm, kv_norm],
                      row_outs, const_outs=[((1, D), F32), ((1, D), F32)])


def _layer_a_out_bwd(dx2, e0, gt0, z, m, w_gate, w_out, a_scale, w_group):
    s = dx2.shape[0]
    tm = TM
    nb = s // tm

    def body(dx2_ref, e_ref, gt_ref, z_ref, m_ref, wgate_ref, wout_ref, as_ref, wg_ref,
             de_ref, dgp_ref, dx1_ref, dm_ref, duz_ref, das_ref, ext):
        i = pl.program_id(0)

        @pl.when(i == 0)
        def _():
            das_ref[...] = jnp.zeros((1, D), F32)
            ext[tm:tm + HALO, :] = jnp.zeros((HALO, D), F32)

        dx1 = _ple_bwd(dx2_ref[...], e_ref, gt_ref, wgate_ref, de_ref, dgp_ref)
        dx1_ref[...] = dx1
        dy = _dot_nt(dx1.astype(BF16), wout_ref[...])
        silu, dsilu = _silu_grads(z_ref[...])
        m = m_ref[...]
        dmixed = dy * silu
        duz_ref[:, D:] = (dy * (m * as_ref[...]) * dsilu).astype(BF16)
        das_ref[...] += jnp.sum(dmixed * m, axis=0, keepdims=True)
        dm_ref[...] = (dmixed * as_ref[...]).astype(BF16)
        t = (nb - 1 - i) * tm + lax.broadcasted_iota(jnp.int32, (tm, 1), 0)
        n_ext = tm + HALO
        for g in range(N_GROUPS):
            w = 2 ** (g + 1)
            cols = slice(g * GROUP_DIM, (g + 1) * GROUP_DIM)
            dpool = _dot_nt(dm_ref[:, cols], wg_ref[g])
            ext[0:tm, cols] = dpool / jnp.minimum(t + 1, w).astype(F32)
            acc = ext[:, cols]
            k = 1
            while k < w:
                acc = acc + pltpu.roll(acc, n_ext - k, 0)
                k *= 2
            duz_ref[:, cols] = (acc[:tm] - dpool).astype(BF16)
        ext[tm:tm + HALO, :] = ext[0:HALO, :]

    row_outs = [(D, BF16), (D, BF16), (D, F32), (D, BF16), (2 * D, BF16)]
    return _rows_call(body, "layer_a_out_bwd", s, [dx2, e0, gt0, z, m], [w_gate, w_out, a_scale, w_group],
                      row_outs, const_outs=[((1, D), F32)], scratch=[pltpu.VMEM((tm + HALO, D), F32)],
                      reverse=True)


def _layer_a_in_bwd(duz, x0, dx1, w_in, a_norm):
    s = x0.shape[0]

    def body(duz_ref, x_ref, dx1_ref, win_ref, an_ref, dx0_ref, dan_ref):
        @pl.when(pl.program_id(0) == 0)
        def _():
            dan_ref[...] = jnp.zeros((1, D), F32)

        dh = jnp.zeros((TM, D), F32)
        for j in range(N_DEV):
            dh = dh + _dot_nt(duz_ref[:, j * 256:(j + 1) * 256], win_ref[j])
        dx0_ref[...] = dx1_ref[...] + _norm_bwd_rows(dh, x_ref[...], an_ref[...], dan_ref)

    return _rows_call(body, "layer_a_in_bwd", s, [duz, x0, dx1], [w_in, a_norm], [(D, F32)],
                      const_outs=[((1, D), F32)])


def _wgrad(a, b, name, n_split=1, a_blocked_b=False):
    s, k = a.shape
    n = b.shape[1]
    tk = 256
    nb = n // n_split

    def body(a_ref, b_ref, o_ref):
        res = _dot_tn(a_ref[...].astype(BF16), b_ref[...].astype(BF16))
        if n_split == 1:
            o_ref[...] = res.astype(BF16)
        else:
            for j in range(n_split):
                o_ref[j] = res[:, j * nb:(j + 1) * nb].astype(BF16)

    if a_blocked_b:
        b_spec = pl.BlockSpec((s, tk), lambda i: (0, i))
        out_spec = pl.BlockSpec((None, tk, tk), lambda i: (i, 0, 0))
        out_shape = jax.ShapeDtypeStruct((k // tk, tk, tk), BF16)
    elif n_split == 1:
        b_spec = pl.BlockSpec((s, n), lambda i: (0, 0))
        out_spec = pl.BlockSpec((tk, n), lambda i: (i, 0))
        out_shape = jax.ShapeDtypeStruct((k, n), BF16)
    else:
        b_spec = pl.BlockSpec((s, n), lambda i: (0, 0))
        out_spec = pl.BlockSpec((n_split, tk, nb), lambda i: (0, i, 0))
        out_shape = jax.ShapeDtypeStruct((n_split, k, nb), BF16)
    return pl.pallas_call(
        body, name=name, grid=(k // tk,),
        in_specs=[pl.BlockSpec((s, tk), lambda i: (0, i)), b_spec],
        out_specs=out_spec, out_shape=out_shape,
        compiler_params=pltpu.CompilerParams(dimension_semantics=("arbitrary",), vmem_limit_bytes=VMEM_LIMIT),
    )(a, b)


def _cast_shards(shards):
    n = len(shards)

    def body(*refs):
        for a in range(n):
            refs[n + a][...] = refs[a][...].astype(BF16)

    vmem = pl.BlockSpec(memory_space=pltpu.VMEM)
    return pl.pallas_call(
        body, name="cast_shards", in_specs=[vmem] * n, out_specs=[vmem] * n,
        out_shape=[jax.ShapeDtypeStruct(a.shape, BF16) for a in shards],
        compiler_params=pltpu.CompilerParams(vmem_limit_bytes=VMEM_LIMIT),
    )(*shards)


def _pair_sum(mine, theirs, name):
    _, r, c = mine.shape

    def body(a_ref, b_ref, f_ref, h_ref):
        tot = a_ref[...].astype(F32) + b_ref[...].astype(F32)
        f_ref[...] = tot
        h_ref[...] = tot.astype(BF16)

    spec = pl.BlockSpec((None, r, c), lambda i: (i, 0, 0))
    return pl.pallas_call(
        body, name=name, grid=(4,), in_specs=[spec, spec], out_specs=[spec, spec],
        out_shape=[jax.ShapeDtypeStruct(mine.shape, F32), jax.ShapeDtypeStruct(mine.shape, BF16)],
        compiler_params=pltpu.CompilerParams(dimension_semantics=("arbitrary",)),
    )(mine, theirs)


def _adamw(w, g, m, v):
    m = ADAM_B1 * m + (1.0 - ADAM_B1) * g
    v = ADAM_B2 * v + (1.0 - ADAM_B2) * jnp.square(g)
    m_hat = m / (1.0 - ADAM_B1 ** ADAM_STEP)
    v_hat = v / (1.0 - ADAM_B2 ** ADAM_STEP)
    delta = -ADAM_LR * (m_hat / (jnp.sqrt(v_hat) + ADAM_EPS) + ADAM_WD * w)
    return delta, m, v


def _finish_shard(part, recv, w, m, v, name):
    r, c = part.shape
    tr = min(r, 256)

    def body(p_ref, r_ref, w_ref, m_ref, v_ref, g_ref, d_ref, m2_ref, v2_ref):
        g = p_ref[...]
        for k in range(3):
            g = g + r_ref[k].astype(F32)
        g_ref[...] = g
        d_ref[...], m2_ref[...], v2_ref[...] = _adamw(w_ref[...], g, m_ref[...], v_ref[...])

    spec = pl.BlockSpec((tr, c), lambda i: (i, 0))
    return pl.pallas_call(
        body, name=name, grid=(r // tr,),
        in_specs=[spec, pl.BlockSpec((3, tr, c), lambda i: (0, i, 0)), spec, spec, spec],
        out_specs=[spec] * 4, out_shape=[jax.ShapeDtypeStruct((r, c), F32)] * 4,
        compiler_params=pltpu.CompilerParams(dimension_semantics=("arbitrary",)),
    )(part, recv, w, m, v)


def _adamw_small(w, g, m, v, name):
    def body(w_ref, g_ref, m_ref, v_ref, d_ref, m2_ref, v2_ref):
        d_ref[...], m2_ref[...], v2_ref[...] = _adamw(w_ref[...], g_ref[...], m_ref[...], v_ref[...])

    vmem = pl.BlockSpec(memory_space=pltpu.VMEM)
    return pl.pallas_call(
        body, name=name, in_specs=[vmem] * 4, out_specs=[vmem] * 3,
        out_shape=[jax.ShapeDtypeStruct(w.shape, F32)] * 3,
    )(w, g, m, v)


def _place():
    return lax.axis_index("x"), lax.axis_index("y"), lax.axis_index("c")


def _all_gather(shards):
    n = len(shards)

    def body(*refs):
        ins, outs = refs[:n], refs[n:2 * n]
        send_sems, recv_sems, local_sems = refs[2 * n:]
        x, y, c = _place()
        me, sibling = (x, y, c), (x, y, 1 - c)
        chips = [(1 - x, y), (x, 1 - y), (1 - x, 1 - y)]

        def copy(a, k, block, to, src=None):
            px, py, pc = block
            slot = outs[a].at[4 * px + 2 * py + pc]
            return pltpu.make_async_remote_copy(
                src_ref=slot if src is None else src, dst_ref=slot,
                send_sem=send_sems.at[7 * a + k], recv_sem=recv_sems.at[7 * a + k],
                device_id=to, device_id_type=MESH)

        started = []
        for a in range(n):
            mine = pltpu.make_async_copy(ins[a], outs[a].at[4 * x + 2 * y + c], local_sems.at[a])
            mine.start()
            started.append(mine)
        sends = []
        for a in range(n):
            first = [copy(a, 0, me, sibling, src=ins[a])]
            first += [copy(a, 1 + j, me, (*chip, c), src=ins[a]) for j, chip in enumerate(chips)]
            for cp in first:
                cp.start()
            sends += first
        for a in range(n):
            for j, chip in enumerate(chips):
                copy(a, 1 + j, (*chip, c), me).wait_recv()
                passed = copy(a, 4 + j, (*chip, c), sibling)
                passed.start()
                sends.append(passed)
        for a in range(n):
            copy(a, 0, sibling, me).wait_recv()
            for j, chip in enumerate(chips):
                copy(a, 4 + j, (*chip, 1 - c), me).wait_recv()
        for cp in sends:
            cp.wait_send()
        for mine in started:
            mine.wait()

    return pl.pallas_call(
        body, name="all_gather_weights",
        in_specs=[HBM_SPEC] * n, out_specs=[HBM_SPEC] * n,
        out_shape=[jax.ShapeDtypeStruct((N_DEV,) + a.shape, a.dtype) for a in shards],
        scratch_shapes=[pltpu.SemaphoreType.DMA((7 * n,)), pltpu.SemaphoreType.DMA((7 * n,)),
                        pltpu.SemaphoreType.DMA((n,))],
    )(*shards)


def _exchange_cores(partials):
    n = len(partials)

    def body(*refs):
        ins, outs = refs[:n], refs[n:2 * n]
        send_sems, recv_sems = refs[2 * n:]
        x, y, c = _place()
        copies = []
        for a in range(n):
            for k in range(4):
                copies.append(pltpu.make_async_remote_copy(
                    src_ref=ins[a].at[2 * k + (1 - c)], dst_ref=outs[a].at[k],
                    send_sem=send_sems.at[4 * a + k], recv_sem=recv_sems.at[4 * a + k],
                    device_id=(x, y, 1 - c), device_id_type=MESH))
        for cp in copies:
            cp.start()
        for cp in copies:
            cp.wait_recv()
        for cp in copies:
            cp.wait_send()

    return pl.pallas_call(
        body, name="grad_exchange_cores",
        in_specs=[HBM_SPEC] * n, out_specs=[HBM_SPEC] * n,
        out_shape=[jax.ShapeDtypeStruct((4,) + a.shape[1:], a.dtype) for a in partials],
        scratch_shapes=[pltpu.SemaphoreType.DMA((4 * n,)), pltpu.SemaphoreType.DMA((4 * n,))],
    )(*partials)


def _exchange_chips(partials):
    n = len(partials)

    def body(*refs):
        ins, outs = refs[:n], refs[n:2 * n]
        send_sems, recv_sems = refs[2 * n:]
        x, y, c = _place()
        peers = [(1 - x, y), (x, 1 - y), (1 - x, 1 - y)]
        copies = []
        for a in range(n):
            for k, (px, py) in enumerate(peers):
                copies.append(pltpu.make_async_remote_copy(
                    src_ref=ins[a].at[2 * px + py], dst_ref=outs[a].at[k],
                    send_sem=send_sems.at[3 * a + k], recv_sem=recv_sems.at[3 * a + k],
                    device_id=(px, py, c), device_id_type=MESH))
        for cp in copies:
            cp.start()
        for cp in copies:
            cp.wait_recv()
        for cp in copies:
            cp.wait_send()

    return pl.pallas_call(
        body, name="grad_exchange_chips",
        in_specs=[HBM_SPEC] * n, out_specs=[HBM_SPEC] * n,
        out_shape=[jax.ShapeDtypeStruct((3,) + a.shape[1:], a.dtype) for a in partials],
        scratch_shapes=[pltpu.SemaphoreType.DMA((3 * n,)), pltpu.SemaphoreType.DMA((3 * n,))],
    )(*partials)


def _all_reduce_small(rows, gain_parts):
    def body(rows_ref, dqg_ref, dkg_ref, out_ref, buf, send_sems, recv_sems):
        x, y, c = _place()
        me = 4 * x + 2 * y + c
        buf[0] = rows_ref[...]
        for row, part in ((4, dqg_ref), (5, dkg_ref)):
            both = jnp.sum(part[...].reshape(HEADS // 2, LANES), axis=0, keepdims=True)
            buf[0, row:row + 1, 0:HEAD_DIM] = both[:, :HEAD_DIM] + both[:, HEAD_DIM:]
        copies = []
        for r in range(1, N_DEV):
            bx, by, bc = (r >> 2) & 1, (r >> 1) & 1, r & 1
            to = (x ^ bx, y ^ by, c ^ bc)
            copies.append(pltpu.make_async_remote_copy(
                src_ref=buf.at[0], dst_ref=buf.at[r], send_sem=send_sems.at[r - 1], recv_sem=recv_sems.at[r - 1],
                device_id=to, device_id_type=MESH))
        for cp in copies:
            cp.start()
        for cp in copies:
            cp.wait_recv()
        for cp in copies:
            cp.wait_send()
        tot = buf[me]
        for j in range(1, N_DEV):
            tot = tot + buf[j ^ me]
        out_ref[...] = tot
        loss = (0.5 / D) * jnp.sum(tot[6:7, :], axis=1, keepdims=True)
        out_ref[6:7, :] = jnp.broadcast_to(loss, (1, D))

    vmem = pl.BlockSpec(memory_space=pltpu.VMEM)
    return pl.pallas_call(
        body, name="all_reduce_small", in_specs=[vmem] * 3, out_specs=vmem,
        out_shape=jax.ShapeDtypeStruct((8, D), F32),
        scratch_shapes=[pltpu.VMEM((N_DEV, 8, D), F32), pltpu.SemaphoreType.DMA((N_DEV - 1,)),
                        pltpu.SemaphoreType.DMA((N_DEV - 1,))],
    )(rows, *gain_parts)


def kernel(x, p, a_norm, a_w_in, a_w_group, a_scale, a_w_out, kv_norm, w_kv, k_norm, b_norm, b_w_in, b_q_norm, b_w_out, ple_w, ple_gate_w, loss_target, m_a_norm, m_a_w_in, m_a_w_group, m_a_scale, m_a_w_out, m_kv_norm, m_w_kv, m_k_norm, m_b_norm, m_b_w_in, m_b_q_norm, m_b_w_out, m_ple_w, m_ple_gate_w, v_a_norm, v_a_w_in, v_a_w_group, v_a_scale, v_a_w_out, v_kv_norm, v_w_kv, v_k_norm, v_b_norm, v_b_w_in, v_b_q_norm, v_b_w_out, v_ple_w, v_ple_gate_w):
    s = x.shape[1]
    xi, yi, ci = _place()
    me = 4 * xi + 2 * yi + ci
    plane = 2 * xi + yi

    big = {
        "a_w_in": a_w_in.reshape(D, 256), "a_w_group": a_w_group.reshape(128, 256),
        "a_w_out": a_w_out.reshape(128, D), "w_kv": w_kv, "b_w_in": b_w_in.reshape(D, 256),
        "b_w_out": b_w_out.reshape(128, D), "ple_w0": ple_w[0], "ple_w1": ple_w[1],
        "gate0": ple_gate_w[0], "gate1": ple_gate_w[1],
    }
    names = list(big)
    cast = _cast_shards([big[k] for k in names])
    small = jnp.concatenate([a_norm, a_scale, jnp.zeros((6, 128), F32)], axis=0)
    gathered = _all_gather(list(cast) + [small])
    full = dict(zip(names, gathered[:-1]))
    small_all = gathered[-1]
    a_norm_f = small_all[:, 0, :].reshape(1, D)
    a_scale_f = small_all[:, 1, :].reshape(1, D)
    w_a_in, w_kv_f, w_b_in = full["a_w_in"], full["w_kv"], full["b_w_in"]
    w_a_out = full["a_w_out"].reshape(D, D)
    w_b_out = full["b_w_out"].reshape(D, D)
    w_gate0 = full["gate0"].reshape(D, D)
    w_gate1 = full["gate1"].reshape(D, D)
    w_ple0, w_ple1 = full["ple_w0"], full["ple_w1"]
    w_group = full["a_w_group"].reshape(N_DEV, 4, 32, 256).transpose(1, 0, 2, 3).reshape(4, 256, 256)
    kvn, bn = kv_norm.reshape(1, D), b_norm
    kg, qg = k_norm.reshape(1, HEAD_DIM), b_q_norm

    x0, p0, p1, target = x[0], p[0, 0], p[1, 0], loss_target[0]
    h0, z, pooled, mcat, y, x1, e0, gt0, x2 = _layer_a_fwd(
        x0, p0, a_norm_f, a_scale_f, w_a_in, w_group, w_a_out, w_ple0, w_gate0)
    hkv, hb, k_all, v_all, q_all, zb = _layer_b_in_fwd(x2, kvn, bn, w_kv_f, w_b_in)
    qg2, kg2 = jnp.concatenate([qg, qg], axis=1), jnp.concatenate([kg, kg], axis=1)
    o, csave = _attn_fwd(q_all, k_all, v_all, qg2, kg2)
    yb, x3, e1, gt1, dx4, sq_err = _layer_b_out_fwd(o, zb, x2, p1, target, w_b_out, w_ple1, w_gate1)

    de1, dgp1, dx3, d_o, dzb = _layer_b_out_bwd(dx4, e1, gt1, o, zb, w_gate1, w_b_out)
    dq, dk, dv, dqg, dkg = _attn_bwd(q_all, k_all, v_all, qg2, kg2, d_o, csave)
    dqz, dkv, dx2, d_bn, d_kvn = _layer_b_in_bwd(dq, dzb, dk, dv, x2, dx3, w_b_in, w_kv_f, bn, kvn)
    de0, dgp0, dx1, dm, duz, d_as = _layer_a_out_bwd(dx2, e0, gt0, z, mcat, w_gate0, w_a_out, a_scale_f, w_group)
    dx0, d_an = _layer_a_in_bwd(duz, x0, dx1, w_a_in, a_norm_f)

    dw_group = _wgrad(pooled, dm, "wgrad_a_w_group", a_blocked_b=True)
    partial = {
        "a_w_in": _wgrad(h0, duz, "wgrad_a_w_in", n_split=8),
        "a_w_group": dw_group.reshape(4, N_DEV, 32, 256).transpose(1, 0, 2, 3).reshape(N_DEV, 128, 256),
        "a_w_out": _wgrad(y, dx1, "wgrad_a_w_out").reshape(N_DEV, 128, D),
        "w_kv": _wgrad(hkv, dkv, "wgrad_w_kv", n_split=8),
        "b_w_in": _wgrad(hb, dqz, "wgrad_b_w_in", n_split=8),
        "b_w_out": _wgrad(yb, dx3, "wgrad_b_w_out").reshape(N_DEV, 128, D),
        "ple_w0": _wgrad(p0, de0, "wgrad_ple_w0", n_split=8),
        "ple_w1": _wgrad(p1, de1, "wgrad_ple_w1", n_split=8),
        "gate0": _wgrad(x1, dgp0, "wgrad_gate0").reshape(N_DEV, 128, D),
        "gate1": _wgrad(x3, dgp1, "wgrad_gate1").reshape(N_DEV, 128, D),
    }

    parts = [partial[k] for k in names]
    from_core = _exchange_cores(parts)
    chip_f32, chip_bf16 = [], []
    for k, mine, theirs in zip(names, parts, from_core):
        own = lax.dynamic_index_in_dim(mine.reshape((4, 2) + mine.shape[1:]), ci, axis=1, keepdims=False)
        f, hlf = _pair_sum(own, theirs, "pair_sum_" + k)
        chip_f32.append(f)
        chip_bf16.append(hlf)
    from_chips = _exchange_chips(chip_bf16)
    mom = {
        "a_w_in": (m_a_w_in, v_a_w_in), "a_w_group": (m_a_w_group, v_a_w_group), "a_w_out": (m_a_w_out, v_a_w_out),
        "w_kv": (m_w_kv, v_w_kv), "b_w_in": (m_b_w_in, v_b_w_in), "b_w_out": (m_b_w_out, v_b_w_out),
        "ple_w0": (m_ple_w[0], v_ple_w[0]), "ple_w1": (m_ple_w[1], v_ple_w[1]),
        "gate0": (m_ple_gate_w[0], v_ple_gate_w[0]), "gate1": (m_ple_gate_w[1], v_ple_gate_w[1]),
    }
    res = {}
    for k, f, recv in zip(names, chip_f32, from_chips):
        part = lax.dynamic_index_in_dim(f, plane, axis=0, keepdims=False)
        shp = big[k].shape
        res[k] = _finish_shard(part, recv, big[k], mom[k][0].reshape(shp), mom[k][1].reshape(shp), "finish_" + k)

    rows = jnp.concatenate([d_kvn, d_bn, d_an, d_as, jnp.zeros((2, D), F32), sq_err, jnp.zeros((1, D), F32)], axis=0)
    tot = _all_reduce_small(rows, (dqg, dkg))
    loss = tot[6, 0]
    g_kvn, g_bn = tot[0:1], tot[1:2]
    g_an = lax.dynamic_slice_in_dim(tot[2:3], me * 128, 128, axis=1)
    g_as = lax.dynamic_slice_in_dim(tot[3:4], me * 128, 128, axis=1)
    g_qg, g_kg = tot[4:5, :HEAD_DIM], tot[5:6, :HEAD_DIM]
    sm = {
        "a_norm": (g_an,) + _adamw_small(a_norm, g_an, m_a_norm, v_a_norm, "adamw_a_norm"),
        "a_scale": (g_as,) + _adamw_small(a_scale, g_as, m_a_scale, v_a_scale, "adamw_a_scale"),
        "kv_norm": tuple(t.reshape(D) for t in (g_kvn,) + _adamw_small(
            kvn, g_kvn, m_kv_norm.reshape(1, D), v_kv_norm.reshape(1, D), "adamw_kv_norm")),
        "k_norm": tuple(t.reshape(HEAD_DIM) for t in (g_kg,) + _adamw_small(
            kg, g_kg, m_k_norm.reshape(1, HEAD_DIM), v_k_norm.reshape(1, HEAD_DIM), "adamw_k_norm")),
        "b_norm": (g_bn,) + _adamw_small(b_norm, g_bn, m_b_norm, v_b_norm, "adamw_b_norm"),
        "b_q_norm": (g_qg,) + _adamw_small(b_q_norm, g_qg, m_b_q_norm, v_b_q_norm, "adamw_b_q_norm"),
    }

    def out(kind):
        def big_one(k, shape):
            return res[k][kind].reshape(shape)

        return [
            sm["a_norm"][kind], big_one("a_w_in", a_w_in.shape), big_one("a_w_group", a_w_group.shape),
            sm["a_scale"][kind], big_one("a_w_out", a_w_out.shape), sm["kv_norm"][kind],
            big_one("w_kv", w_kv.shape), sm["k_norm"][kind], sm["b_norm"][kind],
            big_one("b_w_in", b_w_in.shape), sm["b_q_norm"][kind], big_one("b_w_out", b_w_out.shape),
            jnp.stack([res["ple_w0"][kind], res["ple_w1"][kind]]),
            jnp.stack([res["gate0"][kind], res["gate1"][kind]]),
        ]

    return (loss, dx0.reshape(x.shape), *out(0), *out(1), *out(2), *out(3))
```

```python
import functools

import jax
import jax.numpy as jnp
from jax import lax
from jax.experimental import pallas as pl
from jax.experimental.pallas import tpu as pltpu

F32 = jnp.float32
BF16 = jnp.bfloat16
MESH = pl.DeviceIdType.MESH

N_DEV = 8
D = 1024
N_GROUPS = 4
GROUP_DIM = D // N_GROUPS
HALO = 16
HEADS = 16
HEAD_DIM = D // HEADS
SB_SCALE = HEAD_DIM ** -0.5
TILE = 256
LANES = 128
DEAD_LOG = -120.0
EPS = 1e-6
ADAM_LR = 0.001
ADAM_B1 = 0.9
ADAM_B2 = 0.999
ADAM_EPS = 1e-08
ADAM_WD = 0.01
ADAM_STEP = 10
TM = 256
VMEM_LIMIT = 56 * 1024 * 1024

HBM_SPEC = pl.BlockSpec(memory_space=pltpu.HBM)


def _dot(a, b):
    return jnp.dot(a, b, preferred_element_type=F32)


def _dot_nt(a, b):
    return lax.dot_general(a, b, (((1,), (1,)), ((), ())), preferred_element_type=F32)


def _dot_tn(a, b):
    return lax.dot_general(a, b, (((0,), (0,)), ((), ())), preferred_element_type=F32)


def _sigmoid(x):
    return jax.nn.sigmoid(x)


def _split_dot(x, mat):
    hi = x.astype(BF16)
    lo = (x - hi.astype(F32)).astype(BF16)
    return _dot(hi, mat) + _dot(lo, mat)


def _rms(x):
    return lax.rsqrt(jnp.mean(x * x, axis=-1, keepdims=True) + EPS)


def _rows_call(body, name, n_rows, row_ins, const_ins, row_outs, const_outs=(), scratch=(),
               reverse=False, tm=TM):
    nb = n_rows // tm

    def row_map(i):
        return ((nb - 1 - i) if reverse else i, 0)

    def const_map(nd):
        return lambda i: (0,) * nd

    in_specs = [pl.BlockSpec((tm, a.shape[1]), row_map) for a in row_ins]
    in_specs += [pl.BlockSpec(a.shape, const_map(a.ndim)) for a in const_ins]
    out_specs = [pl.BlockSpec((tm, w), row_map) for (w, _) in row_outs]
    out_specs += [pl.BlockSpec(s, const_map(len(s))) for (s, _) in const_outs]
    out_shape = [jax.ShapeDtypeStruct((n_rows, w), dt) for (w, dt) in row_outs]
    out_shape += [jax.ShapeDtypeStruct(s, dt) for (s, dt) in const_outs]
    return pl.pallas_call(
        body, name=name, grid=(nb,), in_specs=in_specs, out_specs=out_specs, out_shape=out_shape,
        scratch_shapes=list(scratch),
        compiler_params=pltpu.CompilerParams(dimension_semantics=("arbitrary",),
                                             vmem_limit_bytes=VMEM_LIMIT),
    )(*row_ins, *const_ins)


def _ple_fwd(p_ref, xin, wple_ref, wgate_ref, e_ref, gt_ref):
    pb = p_ref[...].astype(BF16)
    for j in range(N_DEV):
        e_ref[:, j * 128:(j + 1) * 128] = _dot(pb, wple_ref[j])
    gt = _sigmoid(_dot(xin.astype(BF16), wgate_ref[...]))
    gt_ref[...] = gt
    return xin + e_ref[...] * gt


def _layer_a_fwd(x0, p0, a_norm, a_scale, w_in, w_group, w_out, w_ple, w_gate):
    s = x0.shape[0]
    tm = TM

    def body(x_ref, p_ref, an_ref, as_ref, win_ref, wg_ref, wout_ref, wple_ref, wgate_ref,
             h_ref, z_ref, pooled_ref, m_ref, y_ref, x1_ref, e_ref, gt_ref, x2_ref, uext):
        i = pl.program_id(0)

        @pl.when(i == 0)
        def _():
            uext[0:HALO, :] = jnp.zeros((HALO, D), F32)

        x = x_ref[...]
        h = (x * _rms(x) * an_ref[...]).astype(BF16)
        h_ref[...] = h
        for j in range(N_DEV):
            uz = _dot(h, win_ref[j])
            if j < 4:
                uext[HALO:HALO + tm, j * 256:(j + 1) * 256] = uz
            else:
                z_ref[:, (j - 4) * 256:(j - 3) * 256] = uz
        t = i * tm + lax.broadcasted_iota(jnp.int32, (tm, 1), 0)
        for g in range(N_GROUPS):
            w = 2 ** (g + 1)
            cols = slice(g * GROUP_DIM, (g + 1) * GROUP_DIM)
            ext = uext[:, cols]
            acc = ext
            k = 1
            while k < w:
                acc = acc + pltpu.roll(acc, k, 0)
                k *= 2
            cnt = jnp.minimum(t + 1, w).astype(F32)
            pooled = (acc[HALO:] / cnt - ext[HALO:]).astype(BF16)
            pooled_ref[:, cols] = pooled
            m_ref[:, cols] = _dot(pooled, wg_ref[g])
        uext[0:HALO, :] = uext[tm:tm + HALO, :]
        z = z_ref[...]
        y = (m_ref[...] * as_ref[...] * (z * _sigmoid(z))).astype(BF16)
        y_ref[...] = y
        x1 = x + _dot(y, wout_ref[...])
        x1_ref[...] = x1
        x2_ref[...] = _ple_fwd(p_ref, x1, wple_ref, wgate_ref, e_ref, gt_ref)

    row_outs = [(D, BF16), (D, F32), (D, BF16), (D, F32), (D, BF16), (D, F32), (D, F32), (D, F32), (D, F32)]
    return _rows_call(body, "layer_a_fwd", s, [x0, p0], [a_norm, a_scale, w_in, w_group, w_out, w_ple, w_gate],
                      row_outs, scratch=[pltpu.VMEM((tm + HALO, D), F32)])


def _layer_b_in_fwd(x2, kv_norm, b_norm, w_kv, w_bin):
    s = x2.shape[0]

    def body(x_ref, kvn_ref, bn_ref, wkv_ref, wbin_ref, hkv_ref, hb_ref, k_ref, v_ref, q_ref, zb_ref):
        x = x_ref[...]
        n = x * _rms(x)
        hkv = (n * kvn_ref[...]).astype(BF16)
        hb = (n * bn_ref[...]).astype(BF16)
        hkv_ref[...] = hkv
        hb_ref[...] = hb
        for j in range(N_DEV):
            kv = _dot(hkv, wkv_ref[j])
            qz = _dot(hb, wbin_ref[j])
            if j < 4:
                cols = slice(j * 256, (j + 1) * 256)
                k_ref[:, cols] = kv
                q_ref[:, cols] = qz
            else:
                cols = slice((j - 4) * 256, (j - 3) * 256)
                v_ref[:, cols] = kv.astype(BF16)
                zb_ref[:, cols] = qz

    row_outs = [(D, BF16), (D, BF16), (D, F32), (D, BF16), (D, F32), (D, F32)]
    return _rows_call(body, "layer_b_in_fwd", s, [x2], [kv_norm, b_norm, w_kv, w_bin], row_outs)


def _tri(after):
    r = lax.broadcasted_iota(jnp.int32, (TILE, TILE), 0)
    c = lax.broadcasted_iota(jnp.int32, (TILE, TILE), 1)
    return jnp.where((r > c) if after else (r < c), 1.0, 0.0).astype(BF16)


def _half_sums(v):
    lo = lax.broadcasted_iota(jnp.int32, v.shape, 1) < HEAD_DIM
    s_lo = jnp.sum(jnp.where(lo, v, 0.0), axis=1, keepdims=True)
    s_hi = jnp.sum(jnp.where(lo, 0.0, v), axis=1, keepdims=True)
    return jnp.where(lo, s_lo, s_hi)


def _pair_norm(x):
    r = lax.rsqrt(_half_sums(x * x) * (1.0 / HEAD_DIM) + EPS)
    return x * r, r


def _tile_logits(qblk, kblk, diagonal):
    l = _dot_nt(qblk, kblk)
    sp = jnp.maximum(l, 0.0) + jnp.log(1.0 + jnp.exp(-jnp.abs(l)))
    ls = l - sp
    if not diagonal:
        return None, -sp, ls
    mask = lax.broadcasted_iota(jnp.int32, l.shape, 1) < lax.broadcasted_iota(jnp.int32, l.shape, 0)
    return mask, jnp.where(mask, -sp, 0.0), ls


def _attn_fwd(q_all, k_all, v_all, q_gain2, k_gain2):
    s = q_all.shape[0]
    nt = s // TILE

    def body(q_ref, k_ref, v_ref, qg_ref, kg_ref, o_ref, c_ref, qs, ks, vs, tri, acc, right, cmat):
        tri[...] = _tri(True)
        lane = lax.broadcasted_iota(jnp.int32, (TILE, LANES), 1)
        qn, _ = _pair_norm(q_ref[...])
        kn, _ = _pair_norm(k_ref[...])
        qsc = (qn * qg_ref[...] * SB_SCALE).astype(BF16)
        ksc = (kn * kg_ref[...]).astype(BF16)
        for hh in range(2):
            sl = slice(hh * HEAD_DIM, (hh + 1) * HEAD_DIM)
            qs[hh] = qsc[:, sl]
            ks[hh] = ksc[:, sl]
            vs[hh] = v_ref[:, sl]

        def tile(qrows, kb, diagonal):
            rows = pl.ds(pl.multiple_of(kb * TILE, TILE), TILE)
            for hh in range(2):
                mask, lk, ls = _tile_logits(qs[hh, qrows, :], ks[hh, rows, :], diagonal)
                rt = right[hh]
                a = jnp.exp(ls + _split_dot(lk, tri[...]) + rt)
                if diagonal:
                    a = jnp.where(mask, a, 0.0)
                acc[hh] += _dot(a.astype(BF16), vs[hh, rows, :])
                cmat[hh] = jnp.where(lane == kb, rt[:, :LANES], cmat[hh])
                right[hh] = rt + jnp.sum(lk, axis=1, keepdims=True)

        def q_step(qb, _):
            r0 = pl.multiple_of(qb * TILE, TILE)
            qrows = pl.ds(r0, TILE)
            acc[...] = jnp.zeros((2, TILE, HEAD_DIM), F32)
            right[...] = jnp.zeros((2, TILE, TILE), F32)
            cmat[...] = jnp.zeros((2, TILE, LANES), F32)
            tile(qrows, qb, True)

            def live():
                return (jnp.max(right[:, :, :LANES]) > DEAD_LOG).astype(jnp.int32)

            def k_step(c):
                kb = c[0] - 1
                tile(qrows, kb, False)
                return kb, live()

            first, _ = lax.while_loop(lambda c: (c[0] > 0) & (c[1] > 0), k_step, (qb, live()))
            for hh in range(2):
                o_ref[qrows, hh * HEAD_DIM:(hh + 1) * HEAD_DIM] = acc[hh]
                c_ref[hh, qrows, :] = jnp.where(lane == LANES - 1, first.astype(F32), cmat[hh])
            return 0

        lax.fori_loop(0, nt, q_step, 0)

    pair = pl.BlockSpec((s, LANES), lambda h: (0, h))
    gain = pl.BlockSpec((1, LANES), lambda h: (0, 0))
    return pl.pallas_call(
        body, name="attn_fwd", grid=(HEADS // 2,),
        in_specs=[pair, pair, pair, gain, gain],
        out_specs=[pair, pl.BlockSpec((2, s, LANES), lambda h: (h, 0, 0))],
        out_shape=[jax.ShapeDtypeStruct((s, D), F32), jax.ShapeDtypeStruct((HEADS, s, LANES), F32)],
        scratch_shapes=[pltpu.VMEM((2, s, HEAD_DIM), BF16)] * 3
        + [pltpu.VMEM((TILE, TILE), BF16), pltpu.VMEM((2, TILE, HEAD_DIM), F32), pltpu.VMEM((2, TILE, TILE), F32),
           pltpu.VMEM((2, TILE, LANES), F32)],
        compiler_params=pltpu.CompilerParams(dimension_semantics=("arbitrary",), vmem_limit_bytes=VMEM_LIMIT),
    )(q_all, k_all, v_all, q_gain2, k_gain2)


def _layer_b_out_fwd(o, zb, x2, p1, target, w_out, w_ple, w_gate):
    s = o.shape[0]

    def body(o_ref, zb_ref, x2_ref, p_ref, t_ref, wout_ref, wple_ref, wgate_ref,
             yb_ref, x3_ref, e_ref, gt_ref, dx4_ref, loss_ref):
        zb = zb_ref[...]
        yb = (o_ref[...] * (zb * _sigmoid(zb))).astype(BF16)
        yb_ref[...] = yb
        x3 = x2_ref[...] + _dot(yb, wout_ref[...])
        x3_ref[...] = x3
        x4 = _ple_fwd(p_ref, x3, wple_ref, wgate_ref, e_ref, gt_ref)
        d = x4 - t_ref[...]
        dx4_ref[...] = d * (1.0 / D)

        @pl.when(pl.program_id(0) == 0)
        def _():
            loss_ref[...] = jnp.zeros((1, D), F32)

        loss_ref[...] += jnp.sum(d * d, axis=0, keepdims=True)

    row_outs = [(D, BF16), (D, F32), (D, F32), (D, F32), (D, F32)]
    return _rows_call(body, "layer_b_out_fwd", s, [o, zb, x2, p1, target], [w_out, w_ple, w_gate], row_outs,
                      const_outs=[((1, D), F32)])


def _ple_bwd(dxo, e_ref, gt_ref, wgate_ref, de_ref, dgp_ref):
    e = e_ref[...]
    gt = gt_ref[...]
    de_ref[...] = (dxo * gt).astype(BF16)
    dgp = (dxo * e * gt * (1.0 - gt)).astype(BF16)
    dgp_ref[...] = dgp
    return dxo + _dot_nt(dgp, wgate_ref[...])


def _silu_grads(z):
    sg = _sigmoid(z)
    return z * sg, sg * (1.0 + z * (1.0 - sg))


def _layer_b_out_bwd(dx4, e1, gt1, o, zb, w_gate, w_out):
    s = dx4.shape[0]

    def body(dx4_ref, e_ref, gt_ref, o_ref, zb_ref, wgate_ref, wout_ref,
             de_ref, dgp_ref, dx3_ref, do_ref, dzb_ref):
        dx3 = _ple_bwd(dx4_ref[...], e_ref, gt_ref, wgate_ref, de_ref, dgp_ref)
        dx3_ref[...] = dx3
        dyb = _dot_nt(dx3.astype(BF16), wout_ref[...])
        silu, dsilu = _silu_grads(zb_ref[...])
        do_ref[...] = (dyb * silu).astype(BF16)
        dzb_ref[...] = (dyb * o_ref[...] * dsilu).astype(BF16)

    row_outs = [(D, BF16), (D, BF16), (D, F32), (D, BF16), (D, BF16)]
    return _rows_call(body, "layer_b_out_bwd", s, [dx4, e1, gt1, o, zb], [w_gate, w_out], row_outs)


def _attn_bwd(q_all, k_all, v_all, q_gain2, k_gain2, d_o, csave):
    s = q_all.shape[0]
    nt = s // TILE

    def body(q_ref, k_ref, v_ref, qg_ref, kg_ref, do_ref, c_ref,
             dq_ref, dk_ref, dv_ref, dqg_ref, dkg_ref,
             qs, ks, vs, dos, qt, dot_t, tri_a, tri_b, dqa, dkt, dvt, dqb, left):
        tri_a[...] = _tri(True)
        tri_b[...] = _tri(False)
        lane = lax.broadcasted_iota(jnp.int32, (TILE, LANES), 1)
        qn, qr = _pair_norm(q_ref[...])
        kn, kr = _pair_norm(k_ref[...])
        qsc = qn * qg_ref[...] * SB_SCALE
        ksc = (kn * kg_ref[...]).astype(BF16)
        q_t = qsc.T.astype(BF16)
        do_t = do_ref[...].astype(F32).T.astype(BF16)
        for j in range(nt):
            qt[j] = q_t[:, j * TILE:(j + 1) * TILE]
            dot_t[j] = do_t[:, j * TILE:(j + 1) * TILE]
        dkt[...] = jnp.zeros((nt, LANES, TILE), F32)
        dvt[...] = jnp.zeros((nt, LANES, TILE), F32)
        qsc = qsc.astype(BF16)
        for hh in range(2):
            sl = slice(hh * HEAD_DIM, (hh + 1) * HEAD_DIM)
            qs[hh] = qsc[:, sl]
            ks[hh] = ksc[:, sl]
            vs[hh] = v_ref[:, sl]
            dos[hh] = do_ref[:, sl]

        def tile(qb, qrows, kb, diagonal):
            rows = pl.ds(pl.multiple_of(kb * TILE, TILE), TILE)
            for hh in range(2):
                sl = slice(hh * HEAD_DIM, (hh + 1) * HEAD_DIM)
                kblk = ks[hh, rows, :]
                mask, lk, ls = _tile_logits(qs[hh, qrows, :], kblk, diagonal)
                beta = jnp.exp(ls)
                right = jnp.sum(jnp.where(lane == kb, c_ref[hh, qrows, :], 0.0), axis=1, keepdims=True)
                a = jnp.exp(ls + _split_dot(lk, tri_a[...]) + right)
                if diagonal:
                    a = jnp.where(mask, a, 0.0)
                g = a * _dot_nt(dos[hh, qrows, :], vs[hh, rows, :])
                lf = left[hh]
                dl = g * (1.0 - beta) - (_split_dot(g, tri_b[...]) + lf) * beta
                if diagonal:
                    dl = jnp.where(mask, dl, 0.0)
                dl = dl.astype(BF16)
                left[hh] = lf + jnp.sum(g, axis=1, keepdims=True)
                dqb[hh] += _dot(dl, kblk)
                dkt[kb, sl, :] += _dot(qt[qb, sl, :], dl)
                dvt[kb, sl, :] += _dot(dot_t[qb, sl, :], a.astype(BF16))

        def q_step(qb, _):
            qrows = pl.ds(pl.multiple_of(qb * TILE, TILE), TILE)
            dqb[...] = jnp.zeros((2, TILE, HEAD_DIM), F32)
            left[...] = jnp.zeros((2, TILE, TILE), F32)

            def k_step(kb, _):
                tile(qb, qrows, kb, False)
                return 0

            first = jnp.max(jnp.where(lane == LANES - 1, c_ref[0, qrows, :], 0.0)).astype(jnp.int32)
            lax.fori_loop(first, qb, k_step, 0)
            tile(qb, qrows, qb, True)
            for hh in range(2):
                dqa[qrows, hh * HEAD_DIM:(hh + 1) * HEAD_DIM] = dqb[hh] * SB_SCALE
            return 0

        lax.fori_loop(0, nt, q_step, 0)

        def norm_bwd(dy, xn, r, g_ref, dx_ref, dg_ref):
            dg_ref[...] = jnp.sum(dy * xn, axis=0, keepdims=True)
            dxn = dy * g_ref[...]
            dx_ref[...] = r * (dxn - xn * (_half_sums(dxn * xn) * (1.0 / HEAD_DIM)))

        norm_bwd(dqa[...], qn, qr, qg_ref, dq_ref, dqg_ref)
        for j in range(nt):
            dqa[j * TILE:(j + 1) * TILE, :] = dkt[j].T
            dv_ref[j * TILE:(j + 1) * TILE, :] = dvt[j].T
        norm_bwd(dqa[...], kn, kr, kg_ref, dk_ref, dkg_ref)

    pair = pl.BlockSpec((s, LANES), lambda h: (0, h))
    gain = pl.BlockSpec((1, LANES), lambda h: (0, 0))
    dgain = pl.BlockSpec((None, 1, LANES), lambda h: (h, 0, 0))
    return pl.pallas_call(
        body, name="attn_bwd", grid=(HEADS // 2,),
        in_specs=[pair, pair, pair, gain, gain, pair, pl.BlockSpec((2, s, LANES), lambda h: (h, 0, 0))],
        out_specs=[pair, pair, pair, dgain, dgain],
        out_shape=[jax.ShapeDtypeStruct((s, D), F32)] * 3
        + [jax.ShapeDtypeStruct((HEADS // 2, 1, LANES), F32)] * 2,
        scratch_shapes=[pltpu.VMEM((2, s, HEAD_DIM), BF16)] * 4
        + [pltpu.VMEM((nt, LANES, TILE), BF16)] * 2 + [pltpu.VMEM((TILE, TILE), BF16)] * 2
        + [pltpu.VMEM((s, LANES), F32)] + [pltpu.VMEM((nt, LANES, TILE), F32)] * 2
        + [pltpu.VMEM((2, TILE, HEAD_DIM), F32), pltpu.VMEM((2, TILE, TILE), F32)],
        compiler_params=pltpu.CompilerParams(dimension_semantics=("arbitrary",), vmem_limit_bytes=VMEM_LIMIT),
    )(q_all, k_all, v_all, q_gain2, k_gain2, d_o, csave)


def _norm_bwd_rows(dh, x, gain, dgain_ref):
    r = _rms(x)
    n = x * r
    dgain_ref[...] += jnp.sum(dh * n, axis=0, keepdims=True)
    dn = dh * gain
    return r * (dn - n * jnp.mean(dn * n, axis=-1, keepdims=True))


def _layer_b_in_bwd(dq, dzb, dk, dv, x2, dx3, w_bin, w_kv, b_norm, kv_norm):
    s = x2.shape[0]

    def body(dq_ref, dzb_ref, dk_ref, dv_ref, x_ref, dx3_ref, wbin_ref, wkv_ref, bn_ref, kvn_ref,
             dqz_ref, dkv_ref, dx2_ref, dbn_ref, dkvn_ref):
        @pl.when(pl.program_id(0) == 0)
        def _():
            dbn_ref[...] = jnp.zeros((1, D), F32)
            dkvn_ref[...] = jnp.zeros((1, D), F32)

        dqz_ref[:, :D] = dq_ref[...].astype(BF16)
        dqz_ref[:, D:] = dzb_ref[...]
        dkv_ref[:, :D] = dk_ref[...].astype(BF16)
        dkv_ref[:, D:] = dv_ref[...].astype(BF16)
        dhb = jnp.zeros((TM, D), F32)
        dhkv = jnp.zeros((TM, D), F32)
        for j in range(N_DEV):
            cols = slice(j * 256, (j + 1) * 256)
            dhb = dhb + _dot_nt(dqz_ref[:, cols], wbin_ref[j])
            dhkv = dhkv + _dot_nt(dkv_ref[:, cols], wkv_ref[j])
        x = x_ref[...]
        dx2 = dx3_ref[...] + _norm_bwd_rows(dhb, x, bn_ref[...], dbn_ref)
        dx2_ref[...] = dx2 + _norm_bwd_rows(dhkv, x, kvn_ref[...], dkvn_ref)

    row_outs = [(2 * D, BF16), (2 * D, BF16), (D, F32)]
    return _rows_call(body, "layer_b_in_bwd", s, [dq, dzb, dk, dv, x2, dx3], [w_bin, w_kv, b_norm, kv_norm],
                      row_outs, const_outs=[((1, D), F32), ((1, D), F32)])


def _layer_a_out_bwd(dx2, e0, gt0, z, m, w_gate, w_out, a_scale, w_group):
    s = dx2.shape[0]
    tm = TM
    nb = s // tm

    def body(dx2_ref, e_ref, gt_ref, z_ref, m_ref, wgate_ref, wout_ref, as_ref, wg_ref,
             de_ref, dgp_ref, dx1_ref, dm_ref, duz_ref, das_ref, ext):
        i = pl.program_id(0)

        @pl.when(i == 0)
        def _():
            das_ref[...] = jnp.zeros((1, D), F32)
            ext[tm:tm + HALO, :] = jnp.zeros((HALO, D), F32)

        dx1 = _ple_bwd(dx2_ref[...], e_ref, gt_ref, wgate_ref, de_ref, dgp_ref)
        dx1_ref[...] = dx1
        dy = _dot_nt(dx1.astype(BF16), wout_ref[...])
        silu, dsilu = _silu_grads(z_ref[...])
        m = m_ref[...]
        dmixed = dy * silu
        duz_ref[:, D:] = (dy * (m * as_ref[...]) * dsilu).astype(BF16)
        das_ref[...] += jnp.sum(dmixed * m, axis=0, keepdims=True)
        dm_ref[...] = (dmixed * as_ref[...]).astype(BF16)
        t = (nb - 1 - i) * tm + lax.broadcasted_iota(jnp.int32, (tm, 1), 0)
        n_ext = tm + HALO
        for g in range(N_GROUPS):
            w = 2 ** (g + 1)
            cols = slice(g * GROUP_DIM, (g + 1) * GROUP_DIM)
            dpool = _dot_nt(dm_ref[:, cols], wg_ref[g])
            ext[0:tm, cols] = dpool / jnp.minimum(t + 1, w).astype(F32)
            acc = ext[:, cols]
            k = 1
            while k < w:
                acc = acc + pltpu.roll(acc, n_ext - k, 0)
                k *= 2
            duz_ref[:, cols] = (acc[:tm] - dpool).astype(BF16)
        ext[tm:tm + HALO, :] = ext[0:HALO, :]

    row_outs = [(D, BF16), (D, BF16), (D, F32), (D, BF16), (2 * D, BF16)]
    return _rows_call(body, "layer_a_out_bwd", s, [dx2, e0, gt0, z, m], [w_gate, w_out, a_scale, w_group],
                      row_outs, const_outs=[((1, D), F32)], scratch=[pltpu.VMEM((tm + HALO, D), F32)],
                      reverse=True)


def _layer_a_in_bwd(duz, x0, dx1, w_in, a_norm):
    s = x0.shape[0]

    def body(duz_ref, x_ref, dx1_ref, win_ref, an_ref, dx0_ref, dan_ref):
        @pl.when(pl.program_id(0) == 0)
        def _():
            dan_ref[...] = jnp.zeros((1, D), F32)

        dh = jnp.zeros((TM, D), F32)
        for j in range(N_DEV):
            dh = dh + _dot_nt(duz_ref[:, j * 256:(j + 1) * 256], win_ref[j])
        dx0_ref[...] = dx1_ref[...] + _norm_bwd_rows(dh, x_ref[...], an_ref[...], dan_ref)

    return _rows_call(body, "layer_a_in_bwd", s, [duz, x0, dx1], [w_in, a_norm], [(D, F32)],
                      const_outs=[((1, D), F32)])


def _wgrad(a, b, name, n_split=1, a_blocked_b=False):
    s, k = a.shape
    n = b.shape[1]
    tk = 256
    nb = n // n_split

    def body(a_ref, b_ref, o_ref):
        res = _dot_tn(a_ref[...].astype(BF16), b_ref[...].astype(BF16))
        if n_split == 1:
            o_ref[...] = res.astype(BF16)
        else:
            for j in range(n_split):
                o_ref[j] = res[:, j * nb:(j + 1) * nb].astype(BF16)

    if a_blocked_b:
        b_spec = pl.BlockSpec((s, tk), lambda i: (0, i))
        out_spec = pl.BlockSpec((None, tk, tk), lambda i: (i, 0, 0))
        out_shape = jax.ShapeDtypeStruct((k // tk, tk, tk), BF16)
    elif n_split == 1:
        b_spec = pl.BlockSpec((s, n), lambda i: (0, 0))
        out_spec = pl.BlockSpec((tk, n), lambda i: (i, 0))
        out_shape = jax.ShapeDtypeStruct((k, n), BF16)
    else:
        b_spec = pl.BlockSpec((s, n), lambda i: (0, 0))
        out_spec = pl.BlockSpec((n_split, tk, nb), lambda i: (0, i, 0))
        out_shape = jax.ShapeDtypeStruct((n_split, k, nb), BF16)
    return pl.pallas_call(
        body, name=name, grid=(k // tk,),
        in_specs=[pl.BlockSpec((s, tk), lambda i: (0, i)), b_spec],
        out_specs=out_spec, out_shape=out_shape,
        compiler_params=pltpu.CompilerParams(dimension_semantics=("arbitrary",), vmem_limit_bytes=VMEM_LIMIT),
    )(a, b)


def _cast_shards(shards):
    n = len(shards)

    def body(*refs):
        for a in range(n):
            refs[n + a][...] = refs[a][...].astype(BF16)

    vmem = pl.BlockSpec(memory_space=pltpu.VMEM)
    return pl.pallas_call(
        body, name="cast_shards", in_specs=[vmem] * n, out_specs=[vmem] * n,
        out_shape=[jax.ShapeDtypeStruct(a.shape, BF16) for a in shards],
        compiler_params=pltpu.CompilerParams(vmem_limit_bytes=VMEM_LIMIT),
    )(*shards)


def _pair_sum(mine, theirs, name):
    _, r, c = mine.shape

    def body(a_ref, b_ref, f_ref, h_ref):
        tot = a_ref[...].astype(F32) + b_ref[...].astype(F32)
        f_ref[...] = tot
        h_ref[...] = tot.astype(BF16)

    spec = pl.BlockSpec((None, r, c), lambda i: (i, 0, 0))
    return pl.pallas_call(
        body, name=name, grid=(4,), in_specs=[spec, spec], out_specs=[spec, spec],
        out_shape=[jax.ShapeDtypeStruct(mine.shape, F32), jax.ShapeDtypeStruct(mine.shape, BF16)],
        compiler_params=pltpu.CompilerParams(dimension_semantics=("arbitrary",)),
    )(mine, theirs)


def _adamw(w, g, m, v):
    m = ADAM_B1 * m + (1.0 - ADAM_B1) * g
    v = ADAM_B2 * v + (1.0 - ADAM_B2) * jnp.square(g)
    m_hat = m / (1.0 - ADAM_B1 ** ADAM_STEP)
    v_hat = v / (1.0 - ADAM_B2 ** ADAM_STEP)
    delta = -ADAM_LR * (m_hat / (jnp.sqrt(v_hat) + ADAM_EPS) + ADAM_WD * w)
    return delta, m, v


def _finish_shard(part, recv, w, m, v, name):
    r, c = part.shape
    tr = min(r, 256)

    def body(p_ref, r_ref, w_ref, m_ref, v_ref, g_ref, d_ref, m2_ref, v2_ref):
        g = p_ref[...]
        for k in range(3):
            g = g + r_ref[k].astype(F32)
        g_ref[...] = g
        d_ref[...], m2_ref[...], v2_ref[...] = _adamw(w_ref[...], g, m_ref[...], v_ref[...])

    spec = pl.BlockSpec((tr, c), lambda i: (i, 0))
    return pl.pallas_call(
        body, name=name, grid=(r // tr,),
        in_specs=[spec, pl.BlockSpec((3, tr, c), lambda i: (0, i, 0)), spec, spec, spec],
        out_specs=[spec] * 4, out_shape=[jax.ShapeDtypeStruct((r, c), F32)] * 4,
        compiler_params=pltpu.CompilerParams(dimension_semantics=("arbitrary",)),
    )(part, recv, w, m, v)


def _adamw_small(w, g, m, v, name):
    def body(w_ref, g_ref, m_ref, v_ref, d_ref, m2_ref, v2_ref):
        d_ref[...], m2_ref[...], v2_ref[...] = _adamw(w_ref[...], g_ref[...], m_ref[...], v_ref[...])

    vmem = pl.BlockSpec(memory_space=pltpu.VMEM)
    return pl.pallas_call(
        body, name=name, in_specs=[vmem] * 4, out_specs=[vmem] * 3,
        out_shape=[jax.ShapeDtypeStruct(w.shape, F32)] * 3,
    )(w, g, m, v)


def _place():
    return lax.axis_index("x"), lax.axis_index("y"), lax.axis_index("c")


def _all_gather(shards):
    n = len(shards)

    def body(*refs):
        ins, outs = refs[:n], refs[n:2 * n]
        send_sems, recv_sems, local_sems = refs[2 * n:]
        x, y, c = _place()
        me, sibling = (x, y, c), (x, y, 1 - c)
        chips = [(1 - x, y), (x, 1 - y), (1 - x, 1 - y)]

        def copy(a, k, block, to, src=None):
            px, py, pc = block
            slot = outs[a].at[4 * px + 2 * py + pc]
            return pltpu.make_async_remote_copy(
                src_ref=slot if src is None else src, dst_ref=slot,
                send_sem=send_sems.at[7 * a + k], recv_sem=recv_sems.at[7 * a + k],
                device_id=to, device_id_type=MESH)

        started = []
        for a in range(n):
            mine = pltpu.make_async_copy(ins[a], outs[a].at[4 * x + 2 * y + c], local_sems.at[a])
            mine.start()
            started.append(mine)
        sends = []
        for a in range(n):
            first = [copy(a, 0, me, sibling, src=ins[a])]
            first += [copy(a, 1 + j, me, (*chip, c), src=ins[a]) for j, chip in enumerate(chips)]
            for cp in first:
                cp.start()
            sends += first
        for a in range(n):
            for j, chip in enumerate(chips):
                copy(a, 1 + j, (*chip, c), me).wait_recv()
                passed = copy(a, 4 + j, (*chip, c), sibling)
                passed.start()
                sends.append(passed)
        for a in range(n):
            copy(a, 0, sibling, me).wait_recv()
            for j, chip in enumerate(chips):
                copy(a, 4 + j, (*chip, 1 - c), me).wait_recv()
        for cp in sends:
            cp.wait_send()
        for mine in started:
            mine.wait()

    return pl.pallas_call(
        body, name="all_gather_weights",
        in_specs=[HBM_SPEC] * n, out_specs=[HBM_SPEC] * n,
        out_shape=[jax.ShapeDtypeStruct((N_DEV,) + a.shape, a.dtype) for a in shards],
        scratch_shapes=[pltpu.SemaphoreType.DMA((7 * n,)), pltpu.SemaphoreType.DMA((7 * n,)),
                        pltpu.SemaphoreType.DMA((n,))],
    )(*shards)


def _exchange_cores(partials):
    n = len(partials)

    def body(*refs):
        ins, outs = refs[:n], refs[n:2 * n]
        send_sems, recv_sems = refs[2 * n:]
        x, y, c = _place()
        copies = []
        for a in range(n):
            for k in range(4):
                copies.append(pltpu.make_async_remote_copy(
                    src_ref=ins[a].at[2 * k + (1 - c)], dst_ref=outs[a].at[k],
                    send_sem=send_sems.at[4 * a + k], recv_sem=recv_sems.at[4 * a + k],
                    device_id=(x, y, 1 - c), device_id_type=MESH))
        for cp in copies:
            cp.start()
        for cp in copies:
            cp.wait_recv()
        for cp in copies:
            cp.wait_send()

    return pl.pallas_call(
        body, name="grad_exchange_cores",
        in_specs=[HBM_SPEC] * n, out_specs=[HBM_SPEC] * n,
        out_shape=[jax.ShapeDtypeStruct((4,) + a.shape[1:], a.dtype) for a in partials],
        scratch_shapes=[pltpu.SemaphoreType.DMA((4 * n,)), pltpu.SemaphoreType.DMA((4 * n,))],
    )(*partials)


def _exchange_chips(partials):
    n = len(partials)

    def body(*refs):
        ins, outs = refs[:n], refs[n:2 * n]
        send_sems, recv_sems = refs[2 * n:]
        x, y, c = _place()
        peers = [(1 - x, y), (x, 1 - y), (1 - x, 1 - y)]
        copies = []
        for a in range(n):
            for k, (px, py) in enumerate(peers):
                copies.append(pltpu.make_async_remote_copy(
                    src_ref=ins[a].at[2 * px + py], dst_ref=outs[a].at[k],
                    send_sem=send_sems.at[3 * a + k], recv_sem=recv_sems.at[3 * a + k],
                    device_id=(px, py, c), device_id_type=MESH))
        for cp in copies:
            cp.start()
        for cp in copies:
            cp.wait_recv()
        for cp in copies:
            cp.wait_send()

    return pl.pallas_call(
        body, name="grad_exchange_chips",
        in_specs=[HBM_SPEC] * n, out_specs=[HBM_SPEC] * n,
        out_shape=[jax.ShapeDtypeStruct((3,) + a.shape[1:], a.dtype) for a in partials],
        scratch_shapes=[pltpu.SemaphoreType.DMA((3 * n,)), pltpu.SemaphoreType.DMA((3 * n,))],
    )(*partials)


def _all_reduce_small(rows, gain_parts):
    def body(rows_ref, dqg_ref, dkg_ref, out_ref, buf, send_sems, recv_sems):
        x, y, c = _place()
        me = 4 * x + 2 * y + c
        buf[0] = rows_ref[...]
        for row, part in ((4, dqg_ref), (5, dkg_ref)):
            both = jnp.sum(part[...].reshape(HEADS // 2, LANES), axis=0, keepdims=True)
            buf[0, row:row + 1, 0:HEAD_DIM] = both[:, :HEAD_DIM] + both[:, HEAD_DIM:]
        copies = []
        for r in range(1, N_DEV):
            bx, by, bc = (r >> 2) & 1, (r >> 1) & 1, r & 1
            to = (x ^ bx, y ^ by, c ^ bc)
            copies.append(pltpu.make_async_remote_copy(
                src_ref=buf.at[0], dst_ref=buf.at[r], send_sem=send_sems.at[r - 1], recv_sem=recv_sems.at[r - 1],
                device_id=to, device_id_type=MESH))
        for cp in copies:
            cp.start()
        for cp in copies:
            cp.wait_recv()
        for cp in copies:
            cp.wait_send()
        tot = buf[me]
        for j in range(1, N_DEV):
            tot = tot + buf[j ^ me]
        out_ref[...] = tot
        loss = (0.5 / D) * jnp.sum(tot[6:7, :], axis=1, keepdims=True)
        out_ref[6:7, :] = jnp.broadcast_to(loss, (1, D))

    vmem = pl.BlockSpec(memory_space=pltpu.VMEM)
    return pl.pallas_call(
        body, name="all_reduce_small", in_specs=[vmem] * 3, out_specs=vmem,
        out_shape=jax.ShapeDtypeStruct((8, D), F32),
        scratch_shapes=[pltpu.VMEM((N_DEV, 8, D), F32), pltpu.SemaphoreType.DMA((N_DEV - 1,)),
                        pltpu.SemaphoreType.DMA((N_DEV - 1,))],
    )(rows, *gain_parts)


def kernel(x, p, a_norm, a_w_in, a_w_group, a_scale, a_w_out, kv_norm, w_kv, k_norm, b_norm, b_w_in, b_q_norm, b_w_out, ple_w, ple_gate_w, loss_target, m_a_norm, m_a_w_in, m_a_w_group, m_a_scale, m_a_w_out, m_kv_norm, m_w_kv, m_k_norm, m_b_norm, m_b_w_in, m_b_q_norm, m_b_w_out, m_ple_w, m_ple_gate_w, v_a_norm, v_a_w_in, v_a_w_group, v_a_scale, v_a_w_out, v_kv_norm, v_w_kv, v_k_norm, v_b_norm, v_b_w_in, v_b_q_norm, v_b_w_out, v_ple_w, v_ple_gate_w):
    s = x.shape[1]
    xi, yi, ci = _place()
    me = 4 * xi + 2 * yi + ci
    plane = 2 * xi + yi

    big = {
        "a_w_in": a_w_in.reshape(D, 256), "a_w_group": a_w_group.reshape(128, 256),
        "a_w_out": a_w_out.reshape(128, D), "w_kv": w_kv, "b_w_in": b_w_in.reshape(D, 256),
        "b_w_out": b_w_out.reshape(128, D), "ple_w0": ple_w[0], "ple_w1": ple_w[1],
        "gate0": ple_gate_w[0], "gate1": ple_gate_w[1],
    }
    names = list(big)
    cast = _cast_shards([big[k] for k in names])
    small = jnp.concatenate([a_norm, a_scale, jnp.zeros((6, 128), F32)], axis=0)
    gathered = _all_gather(list(cast) + [small])
    full = dict(zip(names, gathered[:-1]))
    small_all = gathered[-1]
    a_norm_f = small_all[:, 0, :].reshape(1, D)
    a_scale_f = small_all[:, 1, :].reshape(1, D)
    w_a_in, w_kv_f, w_b_in = full["a_w_in"], full["w_kv"], full["b_w_in"]
    w_a_out = full["a_w_out"].reshape(D, D)
    w_b_out = full["b_w_out"].reshape(D, D)
    w_gate0 = full["gate0"].reshape(D, D)
    w_gate1 = full["gate1"].reshape(D, D)
    w_ple0, w_ple1 = full["ple_w0"], full["ple_w1"]
    w_group = full["a_w_group"].reshape(N_DEV, 4, 32, 256).transpose(1, 0, 2, 3).reshape(4, 256, 256)
    kvn, bn = kv_norm.reshape(1, D), b_norm
    kg, qg = k_norm.reshape(1, HEAD_DIM), b_q_norm

    x0, p0, p1, target = x[0], p[0, 0], p[1, 0], loss_target[0]
    h0, z, pooled, mcat, y, x1, e0, gt0, x2 = _layer_a_fwd(
        x0, p0, a_norm_f, a_scale_f, w_a_in, w_group, w_a_out, w_ple0, w_gate0)
    hkv, hb, k_all, v_all, q_all, zb = _layer_b_in_fwd(x2, kvn, bn, w_kv_f, w_b_in)
    qg2, kg2 = jnp.concatenate([qg, qg], axis=1), jnp.concatenate([kg, kg], axis=1)
    o, csave = _attn_fwd(q_all, k_all, v_all, qg2, kg2)
    yb, x3, e1, gt1, dx4, sq_err = _layer_b_out_fwd(o, zb, x2, p1, target, w_b_out, w_ple1, w_gate1)

    de1, dgp1, dx3, d_o, dzb = _layer_b_out_bwd(dx4, e1, gt1, o, zb, w_gate1, w_b_out)
    dq, dk, dv, dqg, dkg = _attn_bwd(q_all, k_all, v_all, qg2, kg2, d_o, csave)
    dqz, dkv, dx2, d_bn, d_kvn = _layer_b_in_bwd(dq, dzb, dk, dv, x2, dx3, w_b_in, w_kv_f, bn, kvn)
    de0, dgp0, dx1, dm, duz, d_as = _layer_a_out_bwd(dx2, e0, gt0, z, mcat, w_gate0, w_a_out, a_scale_f, w_group)
    dx0, d_an = _layer_a_in_bwd(duz, x0, dx1, w_a_in, a_norm_f)

    dw_group = _wgrad(pooled, dm, "wgrad_a_w_group", a_blocked_b=True)
    partial = {
        "a_w_in": _wgrad(h0, duz, "wgrad_a_w_in", n_split=8),
        "a_w_group": dw_group.reshape(4, N_DEV, 32, 256).transpose(1, 0, 2, 3).reshape(N_DEV, 128, 256),
        "a_w_out": _wgrad(y, dx1, "wgrad_a_w_out").reshape(N_DEV, 128, D),
        "w_kv": _wgrad(hkv, dkv, "wgrad_w_kv", n_split=8),
        "b_w_in": _wgrad(hb, dqz, "wgrad_b_w_in", n_split=8),
        "b_w_out": _wgrad(yb, dx3, "wgrad_b_w_out").reshape(N_DEV, 128, D),
        "ple_w0": _wgrad(p0, de0, "wgrad_ple_w0", n_split=8),
        "ple_w1": _wgrad(p1, de1, "wgrad_ple_w1", n_split=8),
        "gate0": _wgrad(x1, dgp0, "wgrad_gate0").reshape(N_DEV, 128, D),
        "gate1": _wgrad(x3, dgp1, "wgrad_gate1").reshape(N_DEV, 128, D),
    }

    parts = [partial[k] for k in names]
    from_core = _exchange_cores(parts)
    chip_f32, chip_bf16 = [], []
    for k, mine, theirs in zip(names, parts, from_core):
        own = lax.dynamic_index_in_dim(mine.reshape((4, 2) + mine.shape[1:]), ci, axis=1, keepdims=False)
        f, hlf = _pair_sum(own, theirs, "pair_sum_" + k)
        chip_f32.append(f)
        chip_bf16.append(hlf)
    from_chips = _exchange_chips(chip_bf16)
    mom = {
        "a_w_in": (m_a_w_in, v_a_w_in), "a_w_group": (m_a_w_group, v_a_w_group), "a_w_out": (m_a_w_out, v_a_w_out),
        "w_kv": (m_w_kv, v_w_kv), "b_w_in": (m_b_w_in, v_b_w_in), "b_w_out": (m_b_w_out, v_b_w_out),
        "ple_w0": (m_ple_w[0], v_ple_w[0]), "ple_w1": (m_ple_w[1], v_ple_w[1]),
        "gate0": (m_ple_gate_w[0], v_ple_gate_w[0]), "gate1": (m_ple_gate_w[1], v_ple_gate_w[1]),
    }
    res = {}
    for k, f, recv in zip(names, chip_f32, from_chips):
        part = lax.dynamic_index_in_dim(f, plane, axis=0, keepdims=False)
        shp = big[k].shape
        res[k] = _finish_shard(part, recv, big[k], mom[k][0].reshape(shp), mom[k][1].reshape(shp), "finish_" + k)

    rows = jnp.concatenate([d_kvn, d_bn, d_an, d_as, jnp.zeros((2, D), F32), sq_err, jnp.zeros((1, D), F32)], axis=0)
    tot = _all_reduce_small(rows, (dqg, dkg))
    loss = tot[6, 0]
    g_kvn, g_bn = tot[0:1], tot[1:2]
    g_an = lax.dynamic_slice_in_dim(tot[2:3], me * 128, 128, axis=1)
    g_as = lax.dynamic_slice_in_dim(tot[3:4], me * 128, 128, axis=1)
    g_qg, g_kg = tot[4:5, :HEAD_DIM], tot[5:6, :HEAD_DIM]
    sm = {
        "a_norm": (g_an,) + _adamw_small(a_norm, g_an, m_a_norm, v_a_norm, "adamw_a_norm"),
        "a_scale": (g_as,) + _adamw_small(a_scale, g_as, m_a_scale, v_a_scale, "adamw_a_scale"),
        "kv_norm": tuple(t.reshape(D) for t in (g_kvn,) + _adamw_small(
            kvn, g_kvn, m_kv_norm.reshape(1, D), v_kv_norm.reshape(1, D), "adamw_kv_norm")),
        "k_norm": tuple(t.reshape(HEAD_DIM) for t in (g_kg,) + _adamw_small(
            kg, g_kg, m_k_norm.reshape(1, HEAD_DIM), v_k_norm.reshape(1, HEAD_DIM), "adamw_k_norm")),
        "b_norm": (g_bn,) + _adamw_small(b_norm, g_bn, m_b_norm, v_b_norm, "adamw_b_norm"),
        "b_q_norm": (g_qg,) + _adamw_small(b_q_norm, g_qg, m_b_q_norm, v_b_q_norm, "adamw_b_q_norm"),
    }

    def out(kind):
        def big_one(k, shape):
            return res[k][kind].reshape(shape)

        return [
            sm["a_norm"][kind], big_one("a_w_in", a_w_in.shape), big_one("a_w_group", a_w_group.shape),
            sm["a_scale"][kind], big_one("a_w_out", a_w_out.shape), sm["kv_norm"][kind],
            big_one("w_kv", w_kv.shape), sm["k_norm"][kind], sm["b_norm"][kind],
            big_one("b_w_in", b_w_in.shape), sm["b_q_norm"][kind], big_one("b_w_out", b_w_out.shape),
            jnp.stack([res["ple_w0"][kind], res["ple_w1"][kind]]),
            jnp.stack([res["gate0"][kind], res["gate1"][kind]]),
        ]

    return (loss, dx0.reshape(x.shape), *out(0), *out(1), *out(2), *out(3))
```

```python
import functools

import jax
import jax.numpy as jnp
from jax import lax
from jax.experimental import pallas as pl
from jax.experimental.pallas import tpu as pltpu

F32 = jnp.float32
BF16 = jnp.bfloat16
MESH = pl.DeviceIdType.MESH

N_DEV = 8
D = 1024
N_GROUPS = 4
GROUP_DIM = D // N_GROUPS
HALO = 16
HEADS = 16
HEAD_DIM = D // HEADS
SB_SCALE = HEAD_DIM ** -0.5
TILE = 256
LANES = 128
DEAD_LOG = -120.0
EPS = 1e-6
ADAM_LR = 0.001
ADAM_B1 = 0.9
ADAM_B2 = 0.999
ADAM_EPS = 1e-08
ADAM_WD = 0.01
ADAM_STEP = 10
TM = 256
VMEM_LIMIT = 56 * 1024 * 1024

HBM_SPEC = pl.BlockSpec(memory_space=pltpu.HBM)


def _dot(a, b):
    return jnp.dot(a, b, preferred_element_type=F32)


def _dot_nt(a, b):
    return lax.dot_general(a, b, (((1,), (1,)), ((), ())), preferred_element_type=F32)


def _dot_tn(a, b):
    return lax.dot_general(a, b, (((0,), (0,)), ((), ())), preferred_element_type=F32)


def _sigmoid(x):
    return jax.nn.sigmoid(x)


def _split_dot(x, mat):
    hi = x.astype(BF16)
    lo = (x - hi.astype(F32)).astype(BF16)
    return _dot(hi, mat) + _dot(lo, mat)


def _rms(x):
    return lax.rsqrt(jnp.mean(x * x, axis=-1, keepdims=True) + EPS)


def _rows_call(body, name, n_rows, row_ins, const_ins, row_outs, const_outs=(), scratch=(),
               reverse=False, tm=TM, gather=()):
    nb = n_rows // tm
    ng = len(gather)
    n_in = len(row_ins) + len(const_ins)
    n_out = len(row_outs) + len(const_outs)

    def row_map(i):
        return ((nb - 1 - i) if reverse else i, 0)

    def const_map(nd):
        return lambda i: (0,) * nd

    def with_gather(*refs):
        ins, shards = refs[:n_in], refs[n_in:n_in + ng]
        outs, gathered = refs[n_in + ng:n_in + ng + n_out], refs[n_in + ng + n_out:n_in + 2 * ng + n_out]
        rest = refs[n_in + 2 * ng + n_out:]
        moving = _Gather(shards, gathered, *rest[len(scratch):])
        pl.when(pl.program_id(0) == 0)(moving.start)
        body(*ins, *outs, *rest[:len(scratch)])
        pl.when(pl.program_id(0) == nb - 1)(moving.finish)

    in_specs = [pl.BlockSpec((tm, a.shape[1]), row_map) for a in row_ins]
    in_specs += [pl.BlockSpec(a.shape, const_map(a.ndim)) for a in const_ins] + [HBM_SPEC] * ng
    out_specs = [pl.BlockSpec((tm, w), row_map) for (w, _) in row_outs]
    out_specs += [pl.BlockSpec(s, const_map(len(s))) for (s, _) in const_outs] + [HBM_SPEC] * ng
    out_shape = [jax.ShapeDtypeStruct((n_rows, w), dt) for (w, dt) in row_outs]
    out_shape += [jax.ShapeDtypeStruct(s, dt) for (s, dt) in const_outs] + _Gather.out_shape(gather)
    return pl.pallas_call(
        with_gather if ng else body, name=name, grid=(nb,), in_specs=in_specs, out_specs=out_specs,
        out_shape=out_shape, scratch_shapes=list(scratch) + (_Gather.semaphores(ng) if ng else []),
        compiler_params=pltpu.CompilerParams(dimension_semantics=("arbitrary",),
                                             vmem_limit_bytes=VMEM_LIMIT),
    )(*row_ins, *const_ins, *gather)


def _ple_fwd(p_ref, xin, wple_ref, wgate_ref, e_ref, gt_ref):
    pb = p_ref[...].astype(BF16)
    for j in range(N_DEV):
        e_ref[:, j * 128:(j + 1) * 128] = _dot(pb, wple_ref[j])
    gt = _sigmoid(_dot(xin.astype(BF16), wgate_ref[...]))
    gt_ref[...] = gt
    return xin + e_ref[...] * gt


def _layer_a_fwd(x0, p0, a_norm, a_scale, w_in, w_group, w_out, w_ple, w_gate, gather=()):
    s = x0.shape[0]
    tm = TM

    def body(x_ref, p_ref, an_ref, as_ref, win_ref, wg_ref, wout_ref, wple_ref, wgate_ref,
             h_ref, z_ref, pooled_ref, m_ref, y_ref, x1_ref, e_ref, gt_ref, x2_ref, uext):
        i = pl.program_id(0)

        @pl.when(i == 0)
        def _():
            uext[0:HALO, :] = jnp.zeros((HALO, D), F32)

        x = x_ref[...]
        h = (x * _rms(x) * an_ref[...]).astype(BF16)
        h_ref[...] = h
        for j in range(N_DEV):
            uz = _dot(h, win_ref[j])
            if j < 4:
                uext[HALO:HALO + tm, j * 256:(j + 1) * 256] = uz
            else:
                z_ref[:, (j - 4) * 256:(j - 3) * 256] = uz
        t = i * tm + lax.broadcasted_iota(jnp.int32, (tm, 1), 0)
        for g in range(N_GROUPS):
            w = 2 ** (g + 1)
            cols = slice(g * GROUP_DIM, (g + 1) * GROUP_DIM)
            ext = uext[:, cols]
            acc = ext
            k = 1
            while k < w:
                acc = acc + pltpu.roll(acc, k, 0)
                k *= 2
            cnt = jnp.minimum(t + 1, w).astype(F32)
            pooled = (acc[HALO:] / cnt - ext[HALO:]).astype(BF16)
            pooled_ref[:, cols] = pooled
            m_ref[:, cols] = _dot(pooled, wg_ref[g])
        uext[0:HALO, :] = uext[tm:tm + HALO, :]
        z = z_ref[...]
        y = (m_ref[...] * as_ref[...] * (z * _sigmoid(z))).astype(BF16)
        y_ref[...] = y
        x1 = x + _dot(y, wout_ref[...])
        x1_ref[...] = x1
        x2_ref[...] = _ple_fwd(p_ref, x1, wple_ref, wgate_ref, e_ref, gt_ref)

    row_outs = [(D, BF16), (D, F32), (D, BF16), (D, F32), (D, BF16), (D, F32), (D, F32), (D, F32), (D, F32)]
    return _rows_call(body, "layer_a_fwd", s, [x0, p0], [a_norm, a_scale, w_in, w_group, w_out, w_ple, w_gate],
                      row_outs, scratch=[pltpu.VMEM((tm + HALO, D), F32)], gather=gather)


def _layer_b_in_fwd(x2, kv_norm, b_norm, w_kv, w_bin, gather=()):
    s = x2.shape[0]

    def body(x_ref, kvn_ref, bn_ref, wkv_ref, wbin_ref, hkv_ref, hb_ref, k_ref, v_ref, q_ref, zb_ref):
        x = x_ref[...]
        n = x * _rms(x)
        hkv = (n * kvn_ref[...]).astype(BF16)
        hb = (n * bn_ref[...]).astype(BF16)
        hkv_ref[...] = hkv
        hb_ref[...] = hb
        for j in range(N_DEV):
            kv = _dot(hkv, wkv_ref[j])
            qz = _dot(hb, wbin_ref[j])
            if j < 4:
                cols = slice(j * 256, (j + 1) * 256)
                k_ref[:, cols] = kv
                q_ref[:, cols] = qz
            else:
                cols = slice((j - 4) * 256, (j - 3) * 256)
                v_ref[:, cols] = kv.astype(BF16)
                zb_ref[:, cols] = qz

    row_outs = [(D, BF16), (D, BF16), (D, F32), (D, BF16), (D, F32), (D, F32)]
    return _rows_call(body, "layer_b_in_fwd", s, [x2], [kv_norm, b_norm, w_kv, w_bin], row_outs, gather=gather)


def _tri(after):
    r = lax.broadcasted_iota(jnp.int32, (TILE, TILE), 0)
    c = lax.broadcasted_iota(jnp.int32, (TILE, TILE), 1)
    return jnp.where((r > c) if after else (r < c), 1.0, 0.0).astype(BF16)


def _half_sums(v):
    lo = lax.broadcasted_iota(jnp.int32, v.shape, 1) < HEAD_DIM
    s_lo = jnp.sum(jnp.where(lo, v, 0.0), axis=1, keepdims=True)
    s_hi = jnp.sum(jnp.where(lo, 0.0, v), axis=1, keepdims=True)
    return jnp.where(lo, s_lo, s_hi)


def _pair_norm(x):
    r = lax.rsqrt(_half_sums(x * x) * (1.0 / HEAD_DIM) + EPS)
    return x * r, r


def _tile_logits(qblk, kblk, diagonal):
    l = _dot_nt(qblk, kblk)
    sp = jnp.maximum(l, 0.0) + jnp.log(1.0 + jnp.exp(-jnp.abs(l)))
    ls = l - sp
    if not diagonal:
        return None, -sp, ls
    mask = lax.broadcasted_iota(jnp.int32, l.shape, 1) < lax.broadcasted_iota(jnp.int32, l.shape, 0)
    return mask, jnp.where(mask, -sp, 0.0), ls


def _attn_fwd(q_all, k_all, v_all, q_gain2, k_gain2):
    s = q_all.shape[0]
    nt = s // TILE

    def body(q_ref, k_ref, v_ref, qg_ref, kg_ref, o_ref, c_ref, qs, ks, vs, tri, acc, right, cmat):
        tri[...] = _tri(True)
        lane = lax.broadcasted_iota(jnp.int32, (TILE, LANES), 1)
        qn, _ = _pair_norm(q_ref[...])
        kn, _ = _pair_norm(k_ref[...])
        qsc = (qn * qg_ref[...] * SB_SCALE).astype(BF16)
        ksc = (kn * kg_ref[...]).astype(BF16)
        for hh in range(2):
            sl = slice(hh * HEAD_DIM, (hh + 1) * HEAD_DIM)
            qs[hh] = qsc[:, sl]
            ks[hh] = ksc[:, sl]
            vs[hh] = v_ref[:, sl]

        def tile(qrows, kb, diagonal):
            rows = pl.ds(pl.multiple_of(kb * TILE, TILE), TILE)
            loaded = [(qs[hh, qrows, :], ks[hh, rows, :], vs[hh, rows, :], right[hh], cmat[hh], acc[hh])
                      for hh in range(2)]
            tri_m = tri[...]
            results = []
            for q, k, v, rt, cm, ac in loaded:
                mask, lk, ls = _tile_logits(q, k, diagonal)
                a = jnp.exp(ls + _split_dot(lk, tri_m) + rt)
                if diagonal:
                    a = jnp.where(mask, a, 0.0)
                results.append((ac + _dot(a.astype(BF16), v), jnp.where(lane == kb, rt[:, :LANES], cm),
                                rt + jnp.sum(lk, axis=1, keepdims=True)))
            for hh, (ac, cm, rt) in enumerate(results):
                acc[hh] = ac
                cmat[hh] = cm
                right[hh] = rt

        def q_step(qb, _):
            r0 = pl.multiple_of(qb * TILE, TILE)
            qrows = pl.ds(r0, TILE)
            acc[...] = jnp.zeros((2, TILE, HEAD_DIM), F32)
            right[...] = jnp.zeros((2, TILE, TILE), F32)
            cmat[...] = jnp.zeros((2, TILE, LANES), F32)
            tile(qrows, qb, True)

            def live():
                return (jnp.max(right[:, :, :LANES]) > DEAD_LOG).astype(jnp.int32)

            def k_step(c):
                kb = c[0] - 1
                tile(qrows, kb, False)
                return kb, live()

            first, _ = lax.while_loop(lambda c: (c[0] > 0) & (c[1] > 0), k_step, (qb, live()))
            for hh in range(2):
                o_ref[qrows, hh * HEAD_DIM:(hh + 1) * HEAD_DIM] = acc[hh]
                c_ref[hh, qrows, :] = jnp.where(lane == LANES - 1, first.astype(F32), cmat[hh])
            return 0

        lax.fori_loop(0, nt, q_step, 0)

    pair = pl.BlockSpec((s, LANES), lambda h: (0, h))
    gain = pl.BlockSpec((1, LANES), lambda h: (0, 0))
    return pl.pallas_call(
        body, name="attn_fwd", grid=(HEADS // 2,),
        in_specs=[pair, pair, pair, gain, gain],
        out_specs=[pair, pl.BlockSpec((2, s, LANES), lambda h: (h, 0, 0))],
        out_shape=[jax.ShapeDtypeStruct((s, D), F32), jax.ShapeDtypeStruct((HEADS, s, LANES), F32)],
        scratch_shapes=[pltpu.VMEM((2, s, HEAD_DIM), BF16)] * 3
        + [pltpu.VMEM((TILE, TILE), BF16), pltpu.VMEM((2, TILE, HEAD_DIM), F32), pltpu.VMEM((2, TILE, TILE), F32),
           pltpu.VMEM((2, TILE, LANES), F32)],
        compiler_params=pltpu.CompilerParams(dimension_semantics=("arbitrary",), vmem_limit_bytes=VMEM_LIMIT),
    )(q_all, k_all, v_all, q_gain2, k_gain2)


def _layer_b_out_fwd(o, zb, x2, p1, target, w_out, w_ple, w_gate):
    s = o.shape[0]

    def body(o_ref, zb_ref, x2_ref, p_ref, t_ref, wout_ref, wple_ref, wgate_ref,
             yb_ref, x3_ref, e_ref, gt_ref, dx4_ref, loss_ref):
        zb = zb_ref[...]
        yb = (o_ref[...] * (zb * _sigmoid(zb))).astype(BF16)
        yb_ref[...] = yb
        x3 = x2_ref[...] + _dot(yb, wout_ref[...])
        x3_ref[...] = x3
        x4 = _ple_fwd(p_ref, x3, wple_ref, wgate_ref, e_ref, gt_ref)
        d = x4 - t_ref[...]
        dx4_ref[...] = d * (1.0 / D)

        @pl.when(pl.program_id(0) == 0)
        def _():
            loss_ref[...] = jnp.zeros((1, D), F32)

        loss_ref[...] += jnp.sum(d * d, axis=0, keepdims=True)

    row_outs = [(D, BF16), (D, F32), (D, F32), (D, F32), (D, F32)]
    return _rows_call(body, "layer_b_out_fwd", s, [o, zb, x2, p1, target], [w_out, w_ple, w_gate], row_outs,
                      const_outs=[((1, D), F32)])


def _ple_bwd(dxo, e_ref, gt_ref, wgate_ref, de_ref, dgp_ref):
    e = e_ref[...]
    gt = gt_ref[...]
    de_ref[...] = (dxo * gt).astype(BF16)
    dgp = (dxo * e * gt * (1.0 - gt)).astype(BF16)
    dgp_ref[...] = dgp
    return dxo + _dot_nt(dgp, wgate_ref[...])


def _silu_grads(z):
    sg = _sigmoid(z)
    return z * sg, sg * (1.0 + z * (1.0 - sg))


def _layer_b_out_bwd(dx4, e1, gt1, o, zb, w_gate, w_out):
    s = dx4.shape[0]

    def body(dx4_ref, e_ref, gt_ref, o_ref, zb_ref, wgate_ref, wout_ref,
             de_ref, dgp_ref, dx3_ref, do_ref, dzb_ref):
        dx3 = _ple_bwd(dx4_ref[...], e_ref, gt_ref, wgate_ref, de_ref, dgp_ref)
        dx3_ref[...] = dx3
        dyb = _dot_nt(dx3.astype(BF16), wout_ref[...])
        silu, dsilu = _silu_grads(zb_ref[...])
        do_ref[...] = (dyb * silu).astype(BF16)
        dzb_ref[...] = (dyb * o_ref[...] * dsilu).astype(BF16)

    row_outs = [(D, BF16), (D, BF16), (D, F32), (D, BF16), (D, BF16)]
    return _rows_call(body, "layer_b_out_bwd", s, [dx4, e1, gt1, o, zb], [w_gate, w_out], row_outs)


def _attn_bwd(q_all, k_all, v_all, q_gain2, k_gain2, d_o, csave):
    s = q_all.shape[0]
    nt = s // TILE

    def body(q_ref, k_ref, v_ref, qg_ref, kg_ref, do_ref, c_ref,
             dq_ref, dk_ref, dv_ref, dqg_ref, dkg_ref,
             qs, ks, vs, dos, qt, dot_t, tri_a, tri_b, dqa, dkt, dvt, dqb, left):
        tri_a[...] = _tri(True)
        tri_b[...] = _tri(False)
        lane = lax.broadcasted_iota(jnp.int32, (TILE, LANES), 1)
        qn, qr = _pair_norm(q_ref[...])
        kn, kr = _pair_norm(k_ref[...])
        qsc = qn * qg_ref[...] * SB_SCALE
        ksc = (kn * kg_ref[...]).astype(BF16)
        q_t = qsc.T.astype(BF16)
        do_t = do_ref[...].astype(F32).T.astype(BF16)
        for j in range(nt):
            qt[j] = q_t[:, j * TILE:(j + 1) * TILE]
            dot_t[j] = do_t[:, j * TILE:(j + 1) * TILE]
        dkt[...] = jnp.zeros((nt, LANES, TILE), F32)
        dvt[...] = jnp.zeros((nt, LANES, TILE), F32)
        qsc = qsc.astype(BF16)
        for hh in range(2):
            sl = slice(hh * HEAD_DIM, (hh + 1) * HEAD_DIM)
            qs[hh] = qsc[:, sl]
            ks[hh] = ksc[:, sl]
            vs[hh] = v_ref[:, sl]
            dos[hh] = do_ref[:, sl]

        def tile(qb, qrows, kb, diagonal):
            rows = pl.ds(pl.multiple_of(kb * TILE, TILE), TILE)
            for hh in range(2):
                sl = slice(hh * HEAD_DIM, (hh + 1) * HEAD_DIM)
                kblk = ks[hh, rows, :]
                mask, lk, ls = _tile_logits(qs[hh, qrows, :], kblk, diagonal)
                beta = jnp.exp(ls)
                right = jnp.sum(jnp.where(lane == kb, c_ref[hh, qrows, :], 0.0), axis=1, keepdims=True)
                a = jnp.exp(ls + _split_dot(lk, tri_a[...]) + right)
                if diagonal:
                    a = jnp.where(mask, a, 0.0)
                g = a * _dot_nt(dos[hh, qrows, :], vs[hh, rows, :])
                lf = left[hh]
                dl = g * (1.0 - beta) - (_split_dot(g, tri_b[...]) + lf) * beta
                if diagonal:
                    dl = jnp.where(mask, dl, 0.0)
                dl = dl.astype(BF16)
                left[hh] = lf + jnp.sum(g, axis=1, keepdims=True)
                dqb[hh] += _dot(dl, kblk)
                dkt[kb, sl, :] += _dot(qt[qb, sl, :], dl)
                dvt[kb, sl, :] += _dot(dot_t[qb, sl, :], a.astype(BF16))

        def q_step(qb, _):
            qrows = pl.ds(pl.multiple_of(qb * TILE, TILE), TILE)
            dqb[...] = jnp.zeros((2, TILE, HEAD_DIM), F32)
            left[...] = jnp.zeros((2, TILE, TILE), F32)

            def k_step(kb, _):
                tile(qb, qrows, kb, False)
                return 0

            first = jnp.max(jnp.where(lane == LANES - 1, c_ref[0, qrows, :], 0.0)).astype(jnp.int32)
            lax.fori_loop(first, qb, k_step, 0)
            tile(qb, qrows, qb, True)
            for hh in range(2):
                dqa[qrows, hh * HEAD_DIM:(hh + 1) * HEAD_DIM] = dqb[hh] * SB_SCALE
            return 0

        lax.fori_loop(0, nt, q_step, 0)

        def norm_bwd(dy, xn, r, g_ref, dx_ref, dg_ref):
            dg_ref[...] = jnp.sum(dy * xn, axis=0, keepdims=True)
            dxn = dy * g_ref[...]
            dx_ref[...] = r * (dxn - xn * (_half_sums(dxn * xn) * (1.0 / HEAD_DIM)))

        norm_bwd(dqa[...], qn, qr, qg_ref, dq_ref, dqg_ref)
        for j in range(nt):
            dqa[j * TILE:(j + 1) * TILE, :] = dkt[j].T
            dv_ref[j * TILE:(j + 1) * TILE, :] = dvt[j].T
        norm_bwd(dqa[...], kn, kr, kg_ref, dk_ref, dkg_ref)

    pair = pl.BlockSpec((s, LANES), lambda h: (0, h))
    gain = pl.BlockSpec((1, LANES), lambda h: (0, 0))
    dgain = pl.BlockSpec((None, 1, LANES), lambda h: (h, 0, 0))
    return pl.pallas_call(
        body, name="attn_bwd", grid=(HEADS // 2,),
        in_specs=[pair, pair, pair, gain, gain, pair, pl.BlockSpec((2, s, LANES), lambda h: (h, 0, 0))],
        out_specs=[pair, pair, pair, dgain, dgain],
        out_shape=[jax.ShapeDtypeStruct((s, D), F32)] * 3
        + [jax.ShapeDtypeStruct((HEADS // 2, 1, LANES), F32)] * 2,
        scratch_shapes=[pltpu.VMEM((2, s, HEAD_DIM), BF16)] * 4
        + [pltpu.VMEM((nt, LANES, TILE), BF16)] * 2 + [pltpu.VMEM((TILE, TILE), BF16)] * 2
        + [pltpu.VMEM((s, LANES), F32)] + [pltpu.VMEM((nt, LANES, TILE), F32)] * 2
        + [pltpu.VMEM((2, TILE, HEAD_DIM), F32), pltpu.VMEM((2, TILE, TILE), F32)],
        compiler_params=pltpu.CompilerParams(dimension_semantics=("arbitrary",), vmem_limit_bytes=VMEM_LIMIT),
    )(q_all, k_all, v_all, q_gain2, k_gain2, d_o, csave)


def _norm_bwd_rows(dh, x, gain, dgain_ref):
    r = _rms(x)
    n = x * r
    dgain_ref[...] += jnp.sum(dh * n, axis=0, keepdims=True)
    dn = dh * gain
    return r * (dn - n * jnp.mean(dn * n, axis=-1, keepdims=True))


def _layer_b_in_bwd(dq, dzb, dk, dv, x2, dx3, w_bin, w_kv, b_norm, kv_norm):
    s = x2.shape[0]

    def body(dq_ref, dzb_ref, dk_ref, dv_ref, x_ref, dx3_ref, wbin_ref, wkv_ref, bn_ref, kvn_ref,
             dqz_ref, dkv_ref, dx2_ref, dbn_ref, dkvn_ref):
        @pl.when(pl.program_id(0) == 0)
        def _():
            dbn_ref[...] = jnp.zeros((1, D), F32)
            dkvn_ref[...] = jnp.zeros((1, D), F32)

        dqz_ref[:, :D] = dq_ref[...].astype(BF16)
        dqz_ref[:, D:] = dzb_ref[...]
        dkv_ref[:, :D] = dk_ref[...].astype(BF16)
        dkv_ref[:, D:] = dv_ref[...].astype(BF16)
        dhb = jnp.zeros((TM, D), F32)
        dhkv = jnp.zeros((TM, D), F32)
        for j in range(N_DEV):
            cols = slice(j * 256, (j + 1) * 256)
            dhb = dhb + _dot_nt(dqz_ref[:, cols], wbin_ref[j])
            dhkv = dhkv + _dot_nt(dkv_ref[:, cols], wkv_ref[j])
        x = x_ref[...]
        dx2 = dx3_ref[...] + _norm_bwd_rows(dhb, x, bn_ref[...], dbn_ref)
        dx2_ref[...] = dx2 + _norm_bwd_rows(dhkv, x, kvn_ref[...], dkvn_ref)

    row_outs = [(2 * D, BF16), (2 * D, BF16), (D, F32)]
    return _rows_call(body, "layer_b_in_bwd", s, [dq, dzb, dk, dv, x2, dx3], [w_bin, w_kv, b_norm, kv_norm],
                      row_outs, const_outs=[((1, D), F32), ((1, D), F32)])


def _layer_a_out_bwd(dx2, e0, gt0, z, m, w_gate, w_out, a_scale, w_group):
    s = dx2.shape[0]
    tm = TM
    nb = s // tm

    def body(dx2_ref, e_ref, gt_ref, z_ref, m_ref, wgate_ref, wout_ref, as_ref, wg_ref,
             de_ref, dgp_ref, dx1_ref, dm_ref, duz_ref, das_ref, ext):
        i = pl.program_id(0)

        @pl.when(i == 0)
        def _():
            das_ref[...] = jnp.zeros((1, D), F32)
            ext[tm:tm + HALO, :] = jnp.zeros((HALO, D), F32)

        dx1 = _ple_bwd(dx2_ref[...], e_ref, gt_ref, wgate_ref, de_ref, dgp_ref)
        dx1_ref[...] = dx1
        dy = _dot_nt(dx1.astype(BF16), wout_ref[...])
        silu, dsilu = _silu_grads(z_ref[...])
        m = m_ref[...]
        dmixed = dy * silu
        duz_ref[:, D:] = (dy * (m * as_ref[...]) * dsilu).astype(BF16)
        das_ref[...] += jnp.sum(dmixed * m, axis=0, keepdims=True)
        dm_ref[...] = (dmixed * as_ref[...]).astype(BF16)
        t = (nb - 1 - i) * tm + lax.broadcasted_iota(jnp.int32, (tm, 1), 0)
        n_ext = tm + HALO
        for g in range(N_GROUPS):
            w = 2 ** (g + 1)
            cols = slice(g * GROUP_DIM, (g + 1) * GROUP_DIM)
            dpool = _dot_nt(dm_ref[:, cols], wg_ref[g])
            ext[0:tm, cols] = dpool / jnp.minimum(t + 1, w).astype(F32)
            acc = ext[:, cols]
            k = 1
            while k < w:
                acc = acc + pltpu.roll(acc, n_ext - k, 0)
                k *= 2
            duz_ref[:, cols] = (acc[:tm] - dpool).astype(BF16)
        ext[tm:tm + HALO, :] = ext[0:HALO, :]

    row_outs = [(D, BF16), (D, BF16), (D, F32), (D, BF16), (2 * D, BF16)]
    return _rows_call(body, "layer_a_out_bwd", s, [dx2, e0, gt0, z, m], [w_gate, w_out, a_scale, w_group],
                      row_outs, const_outs=[((1, D), F32)], scratch=[pltpu.VMEM((tm + HALO, D), F32)],
                      reverse=True)


def _layer_a_in_bwd(duz, x0, dx1, w_in, a_norm):
    s = x0.shape[0]

    def body(duz_ref, x_ref, dx1_ref, win_ref, an_ref, dx0_ref, dan_ref):
        @pl.when(pl.program_id(0) == 0)
        def _():
            dan_ref[...] = jnp.zeros((1, D), F32)

        dh = jnp.zeros((TM, D), F32)
        for j in range(N_DEV):
            dh = dh + _dot_nt(duz_ref[:, j * 256:(j + 1) * 256], win_ref[j])
        dx0_ref[...] = dx1_ref[...] + _norm_bwd_rows(dh, x_ref[...], an_ref[...], dan_ref)

    return _rows_call(body, "layer_a_in_bwd", s, [duz, x0, dx1], [w_in, a_norm], [(D, F32)],
                      const_outs=[((1, D), F32)])


def _wgrad(a, b, name, n_split=1, a_blocked_b=False):
    s, k = a.shape
    n = b.shape[1]
    tk = 256
    nb = n // n_split

    def body(a_ref, b_ref, o_ref):
        res = _dot_tn(a_ref[...].astype(BF16), b_ref[...].astype(BF16))
        if n_split == 1:
            o_ref[...] = res.astype(BF16)
        else:
            for j in range(n_split):
                o_ref[j] = res[:, j * nb:(j + 1) * nb].astype(BF16)

    if a_blocked_b:
        b_spec = pl.BlockSpec((s, tk), lambda i: (0, i))
        out_spec = pl.BlockSpec((None, tk, tk), lambda i: (i, 0, 0))
        out_shape = jax.ShapeDtypeStruct((k // tk, tk, tk), BF16)
    elif n_split == 1:
        b_spec = pl.BlockSpec((s, n), lambda i: (0, 0))
        out_spec = pl.BlockSpec((tk, n), lambda i: (i, 0))
        out_shape = jax.ShapeDtypeStruct((k, n), BF16)
    else:
        b_spec = pl.BlockSpec((s, n), lambda i: (0, 0))
        out_spec = pl.BlockSpec((n_split, tk, nb), lambda i: (0, i, 0))
        out_shape = jax.ShapeDtypeStruct((n_split, k, nb), BF16)
    return pl.pallas_call(
        body, name=name, grid=(k // tk,),
        in_specs=[pl.BlockSpec((s, tk), lambda i: (0, i)), b_spec],
        out_specs=out_spec, out_shape=out_shape,
        compiler_params=pltpu.CompilerParams(dimension_semantics=("arbitrary",), vmem_limit_bytes=VMEM_LIMIT),
    )(a, b)


def _cast_shards(shards):
    n = len(shards)

    def body(*refs):
        for a in range(n):
            refs[n + a][...] = refs[a][...].astype(BF16)

    vmem = pl.BlockSpec(memory_space=pltpu.VMEM)
    return pl.pallas_call(
        body, name="cast_shards", in_specs=[vmem] * n, out_specs=[vmem] * n,
        out_shape=[jax.ShapeDtypeStruct(a.shape, BF16) for a in shards],
        compiler_params=pltpu.CompilerParams(vmem_limit_bytes=VMEM_LIMIT),
    )(*shards)


def _adamw(w, g, m, v):
    m = ADAM_B1 * m + (1.0 - ADAM_B1) * g
    v = ADAM_B2 * v + (1.0 - ADAM_B2) * jnp.square(g)
    m_hat = m / (1.0 - ADAM_B1 ** ADAM_STEP)
    v_hat = v / (1.0 - ADAM_B2 ** ADAM_STEP)
    delta = -ADAM_LR * (m_hat / (jnp.sqrt(v_hat) + ADAM_EPS) + ADAM_WD * w)
    return delta, m, v


def _adamw_small(w, g, m, v, name):
    def body(w_ref, g_ref, m_ref, v_ref, d_ref, m2_ref, v2_ref):
        d_ref[...], m2_ref[...], v2_ref[...] = _adamw(w_ref[...], g_ref[...], m_ref[...], v_ref[...])

    vmem = pl.BlockSpec(memory_space=pltpu.VMEM)
    return pl.pallas_call(
        body, name=name, in_specs=[vmem] * 4, out_specs=[vmem] * 3,
        out_shape=[jax.ShapeDtypeStruct(w.shape, F32)] * 3,
    )(w, g, m, v)


def _place():
    return lax.axis_index("x"), lax.axis_index("y"), lax.axis_index("c")


def _all_gather(shards):
    n = len(shards)

    def body(*refs):
        gather = _Gather(refs[:n], refs[n:2 * n], *refs[2 * n:])
        gather.start()
        gather.finish()

    return pl.pallas_call(
        body, name="all_gather_weights",
        in_specs=[HBM_SPEC] * n, out_specs=[HBM_SPEC] * n,
        out_shape=_Gather.out_shape(shards), scratch_shapes=_Gather.semaphores(n),
    )(*shards)


class _Gather:
    def __init__(self, ins, outs, send_sems, recv_sems, local_sems):
        self.ins, self.outs = ins, outs
        self.send_sems, self.recv_sems, self.local_sems = send_sems, recv_sems, local_sems
        self.x, self.y, self.c = _place()

    @staticmethod
    def out_shape(shards):
        return [jax.ShapeDtypeStruct((N_DEV,) + a.shape, a.dtype) for a in shards]

    @staticmethod
    def semaphores(n):
        return [pltpu.SemaphoreType.DMA((7 * n,)), pltpu.SemaphoreType.DMA((7 * n,)), pltpu.SemaphoreType.DMA((n,))]

    def _chips(self):
        x, y = self.x, self.y
        return [(1 - x, y), (x, 1 - y), (1 - x, 1 - y)]

    def _copy(self, a, k, block, to, own=False):
        px, py, pc = block
        slot = self.outs[a].at[4 * px + 2 * py + pc]
        return pltpu.make_async_remote_copy(
            src_ref=self.ins[a] if own else slot, dst_ref=slot,
            send_sem=self.send_sems.at[7 * a + k], recv_sem=self.recv_sems.at[7 * a + k],
            device_id=to, device_id_type=MESH)

    def _local(self, a):
        return pltpu.make_async_copy(self.ins[a], self.outs[a].at[4 * self.x + 2 * self.y + self.c],
                                     self.local_sems.at[a])

    def _first(self, a):
        me, sibling = (self.x, self.y, self.c), (self.x, self.y, 1 - self.c)
        return [self._copy(a, 0, me, sibling, own=True)] + [
            self._copy(a, 1 + j, me, (*chip, self.c), own=True) for j, chip in enumerate(self._chips())]

    def start(self):
        for a in range(len(self.ins)):
            self._local(a).start()
            for cp in self._first(a):
                cp.start()

    def finish(self):
        c = self.c
        me, sibling = (self.x, self.y, c), (self.x, self.y, 1 - c)
        n = len(self.ins)
        passed = []
        for a in range(n):
            for j, chip in enumerate(self._chips()):
                self._copy(a, 1 + j, (*chip, c), me).wait_recv()
                passed.append(self._copy(a, 4 + j, (*chip, c), sibling))
                passed[-1].start()
        for a in range(n):
            self._copy(a, 0, sibling, me).wait_recv()
            for j, chip in enumerate(self._chips()):
                self._copy(a, 4 + j, (*chip, 1 - c), me).wait_recv()
        for a in range(n):
            for cp in self._first(a):
                cp.wait_send()
            self._local(a).wait()
        for cp in passed:
            cp.wait_send()


def _reduce_cores(partials):
    n = len(partials)

    def body(*refs):
        ins, own32, chip16 = refs[:n], refs[n:2 * n], refs[2 * n:3 * n]
        landed, mine = refs[3 * n:4 * n], refs[4 * n:5 * n]
        send_sems, recv_sems, local_sems = refs[5 * n:]
        x, y, c = _place()
        plane = 2 * x + y
        remote, local = [], []
        for a in range(n):
            for k in range(4):
                remote.append(pltpu.make_async_remote_copy(
                    src_ref=ins[a].at[2 * k + (1 - c)], dst_ref=landed[a].at[k],
                    send_sem=send_sems.at[4 * a + k], recv_sem=recv_sems.at[4 * a + k],
                    device_id=(x, y, 1 - c), device_id_type=MESH))
                local.append(pltpu.make_async_copy(ins[a].at[2 * k + c], mine[a].at[k], local_sems.at[4 * a + k]))
                remote[-1].start()
                local[-1].start()
        for a in range(n):
            for k in range(4):
                local[4 * a + k].wait()
                remote[4 * a + k].wait_recv()
                chip16[a][k] = (mine[a][k].astype(F32) + landed[a][k].astype(F32)).astype(BF16)
            own32[a][...] = mine[a][plane].astype(F32) + landed[a][plane].astype(F32)
        for cp in remote:
            cp.wait_send()

    vmem = pl.BlockSpec(memory_space=pltpu.VMEM)
    quarter = [pltpu.VMEM((4,) + a.shape[1:], BF16) for a in partials]
    return pl.pallas_call(
        body, name="reduce_cores",
        in_specs=[HBM_SPEC] * n, out_specs=[vmem] * (2 * n),
        out_shape=[jax.ShapeDtypeStruct(a.shape[1:], F32) for a in partials]
        + [jax.ShapeDtypeStruct((4,) + a.shape[1:], BF16) for a in partials],
        scratch_shapes=quarter + quarter
        + [pltpu.SemaphoreType.DMA((4 * n,)), pltpu.SemaphoreType.DMA((4 * n,)), pltpu.SemaphoreType.DMA((4 * n,))],
        compiler_params=pltpu.CompilerParams(vmem_limit_bytes=VMEM_LIMIT),
    )(*partials)


def _reduce_chips(own32, chip16):
    n = len(own32)

    def body(*refs):
        p32, p16, grads, landed = refs[:n], refs[n:2 * n], refs[2 * n:3 * n], refs[3 * n:4 * n]
        send_sems, recv_sems = refs[4 * n:]
        x, y, c = _place()
        peers = [(1 - x, y), (x, 1 - y), (1 - x, 1 - y)]
        copies = []
        for a in range(n):
            for k, (px, py) in enumerate(peers):
                copies.append(pltpu.make_async_remote_copy(
                    src_ref=p16[a].at[2 * px + py], dst_ref=landed[a].at[k],
                    send_sem=send_sems.at[3 * a + k], recv_sem=recv_sems.at[3 * a + k],
                    device_id=(px, py, c), device_id_type=MESH))
                copies[-1].start()
        for a in range(n):
            g = p32[a][...]
            for k in range(3):
                copies[3 * a + k].wait_recv()
                g = g + landed[a][k].astype(F32)
            grads[a][...] = g
        for cp in copies:
            cp.wait_send()

    vmem = pl.BlockSpec(memory_space=pltpu.VMEM)
    return pl.pallas_call(
        body, name="reduce_chips",
        in_specs=[vmem] * (2 * n), out_specs=[vmem] * n,
        out_shape=[jax.ShapeDtypeStruct(a.shape, F32) for a in own32],
        scratch_shapes=[pltpu.VMEM((3,) + a.shape, BF16) for a in own32]
        + [pltpu.SemaphoreType.DMA((3 * n,)), pltpu.SemaphoreType.DMA((3 * n,))],
        compiler_params=pltpu.CompilerParams(vmem_limit_bytes=VMEM_LIMIT),
    )(*own32, *chip16)


def _adamw_shards(ws, gs, ms, vs):
    n = len(ws)

    def body(*refs):
        w, g, m, v = refs[:n], refs[n:2 * n], refs[2 * n:3 * n], refs[3 * n:4 * n]
        outs = refs[4 * n:]
        for a in range(n):
            outs[a][...], outs[n + a][...], outs[2 * n + a][...] = _adamw(w[a][...], g[a][...], m[a][...], v[a][...])

    vmem = pl.BlockSpec(memory_space=pltpu.VMEM)
    res = pl.pallas_call(
        body, name="adamw_shards", in_specs=[vmem] * (4 * n), out_specs=[vmem] * (3 * n),
        out_shape=[jax.ShapeDtypeStruct(a.shape, F32) for a in ws] * 3,
        compiler_params=pltpu.CompilerParams(vmem_limit_bytes=VMEM_LIMIT),
    )(*ws, *gs, *ms, *vs)
    return res[:n], res[n:2 * n], res[2 * n:]


def _all_reduce_small(rows, gain_parts):
    def body(rows_ref, dqg_ref, dkg_ref, out_ref, buf, send_sems, recv_sems):
        x, y, c = _place()
        me = 4 * x + 2 * y + c
        buf[0] = rows_ref[...]
        for row, part in ((4, dqg_ref), (5, dkg_ref)):
            both = jnp.sum(part[...].reshape(HEADS // 2, LANES), axis=0, keepdims=True)
            buf[0, row:row + 1, 0:HEAD_DIM] = both[:, :HEAD_DIM] + both[:, HEAD_DIM:]
        copies = []
        for r in range(1, N_DEV):
            bx, by, bc = (r >> 2) & 1, (r >> 1) & 1, r & 1
            to = (x ^ bx, y ^ by, c ^ bc)
            copies.append(pltpu.make_async_remote_copy(
                src_ref=buf.at[0], dst_ref=buf.at[r], send_sem=send_sems.at[r - 1], recv_sem=recv_sems.at[r - 1],
                device_id=to, device_id_type=MESH))
        for cp in copies:
            cp.start()
        for cp in copies:
            cp.wait_recv()
        for cp in copies:
            cp.wait_send()
        tot = buf[me]
        for j in range(1, N_DEV):
            tot = tot + buf[j ^ me]
        out_ref[...] = tot
        loss = (0.5 / D) * jnp.sum(tot[6:7, :], axis=1, keepdims=True)
        out_ref[6:7, :] = jnp.broadcast_to(loss, (1, D))

    vmem = pl.BlockSpec(memory_space=pltpu.VMEM)
    return pl.pallas_call(
        body, name="all_reduce_small", in_specs=[vmem] * 3, out_specs=vmem,
        out_shape=jax.ShapeDtypeStruct((8, D), F32),
        scratch_shapes=[pltpu.VMEM((N_DEV, 8, D), F32), pltpu.SemaphoreType.DMA((N_DEV - 1,)),
                        pltpu.SemaphoreType.DMA((N_DEV - 1,))],
    )(rows, *gain_parts)


def kernel(x, p, a_norm, a_w_in, a_w_group, a_scale, a_w_out, kv_norm, w_kv, k_norm, b_norm, b_w_in, b_q_norm, b_w_out, ple_w, ple_gate_w, loss_target, m_a_norm, m_a_w_in, m_a_w_group, m_a_scale, m_a_w_out, m_kv_norm, m_w_kv, m_k_norm, m_b_norm, m_b_w_in, m_b_q_norm, m_b_w_out, m_ple_w, m_ple_gate_w, v_a_norm, v_a_w_in, v_a_w_group, v_a_scale, v_a_w_out, v_kv_norm, v_w_kv, v_k_norm, v_b_norm, v_b_w_in, v_b_q_norm, v_b_w_out, v_ple_w, v_ple_gate_w):
    s = x.shape[1]
    xi, yi, ci = _place()
    me = 4 * xi + 2 * yi + ci
    plane = 2 * xi + yi

    big = {
        "a_w_in": a_w_in.reshape(D, 256), "a_w_group": a_w_group.reshape(128, 256),
        "a_w_out": a_w_out.reshape(128, D), "w_kv": w_kv, "b_w_in": b_w_in.reshape(D, 256),
        "b_w_out": b_w_out.reshape(128, D), "ple_w0": ple_w[0], "ple_w1": ple_w[1],
        "gate0": ple_gate_w[0], "gate1": ple_gate_w[1],
    }
    names = list(big)
    cast = dict(zip(names, _cast_shards([big[k] for k in names])))
    small = jnp.concatenate([a_norm, a_scale, jnp.zeros((6, 128), F32)], axis=0)
    first = ["a_w_in", "a_w_group", "a_w_out", "ple_w0", "gate0"]
    behind_a = ["w_kv", "b_w_in"]
    behind_b = ["b_w_out", "ple_w1", "gate1"]
    gathered = _all_gather([cast[k] for k in first] + [small])
    full = dict(zip(first, gathered[:-1]))
    small_all = gathered[-1]
    a_norm_f = small_all[:, 0, :].reshape(1, D)
    a_scale_f = small_all[:, 1, :].reshape(1, D)
    w_a_in = full["a_w_in"]
    w_a_out = full["a_w_out"].reshape(D, D)
    w_gate0 = full["gate0"].reshape(D, D)
    w_ple0 = full["ple_w0"]
    w_group = full["a_w_group"].reshape(N_DEV, 4, 32, 256).transpose(1, 0, 2, 3).reshape(4, 256, 256)
    kvn, bn = kv_norm.reshape(1, D), b_norm
    kg, qg = k_norm.reshape(1, HEAD_DIM), b_q_norm

    x0, p0, p1, target = x[0], p[0, 0], p[1, 0], loss_target[0]
    h0, z, pooled, mcat, y, x1, e0, gt0, x2, w_kv_f, w_b_in = _layer_a_fwd(
        x0, p0, a_norm_f, a_scale_f, w_a_in, w_group, w_a_out, w_ple0, w_gate0, gather=[cast[k] for k in behind_a])
    hkv, hb, k_all, v_all, q_all, zb, w_b_out, w_ple1, w_gate1 = _layer_b_in_fwd(
        x2, kvn, bn, w_kv_f, w_b_in, gather=[cast[k] for k in behind_b])
    w_b_out, w_gate1 = w_b_out.reshape(D, D), w_gate1.reshape(D, D)
    qg2, kg2 = jnp.concatenate([qg, qg], axis=1), jnp.concatenate([kg, kg], axis=1)
    o, csave = _attn_fwd(q_all, k_all, v_all, qg2, kg2)
    yb, x3, e1, gt1, dx4, sq_err = _layer_b_out_fwd(o, zb, x2, p1, target, w_b_out, w_ple1, w_gate1)

    de1, dgp1, dx3, d_o, dzb = _layer_b_out_bwd(dx4, e1, gt1, o, zb, w_gate1, w_b_out)
    dq, dk, dv, dqg, dkg = _attn_bwd(q_all, k_all, v_all, qg2, kg2, d_o, csave)
    dqz, dkv, dx2, d_bn, d_kvn = _layer_b_in_bwd(dq, dzb, dk, dv, x2, dx3, w_b_in, w_kv_f, bn, kvn)
    de0, dgp0, dx1, dm, duz, d_as = _layer_a_out_bwd(dx2, e0, gt0, z, mcat, w_gate0, w_a_out, a_scale_f, w_group)
    dx0, d_an = _layer_a_in_bwd(duz, x0, dx1, w_a_in, a_norm_f)

    dw_group = _wgrad(pooled, dm, "wgrad_a_w_group", a_blocked_b=True)
    partial = {
        "a_w_in": _wgrad(h0, duz, "wgrad_a_w_in", n_split=8),
        "a_w_group": dw_group.reshape(4, N_DEV, 32, 256).transpose(1, 0, 2, 3).reshape(N_DEV, 128, 256),
        "a_w_out": _wgrad(y, dx1, "wgrad_a_w_out").reshape(N_DEV, 128, D),
        "w_kv": _wgrad(hkv, dkv, "wgrad_w_kv", n_split=8),
        "b_w_in": _wgrad(hb, dqz, "wgrad_b_w_in", n_split=8),
        "b_w_out": _wgrad(yb, dx3, "wgrad_b_w_out").reshape(N_DEV, 128, D),
        "ple_w0": _wgrad(p0, de0, "wgrad_ple_w0", n_split=8),
        "ple_w1": _wgrad(p1, de1, "wgrad_ple_w1", n_split=8),
        "gate0": _wgrad(x1, dgp0, "wgrad_gate0").reshape(N_DEV, 128, D),
        "gate1": _wgrad(x3, dgp1, "wgrad_gate1").reshape(N_DEV, 128, D),
    }

    n_big = len(names)
    chip = _reduce_cores([partial[k] for k in names])
    grads = _reduce_chips(chip[:n_big], chip[n_big:])
    mom = {
        "a_w_in": (m_a_w_in, v_a_w_in), "a_w_group": (m_a_w_group, v_a_w_group), "a_w_out": (m_a_w_out, v_a_w_out),
        "w_kv": (m_w_kv, v_w_kv), "b_w_in": (m_b_w_in, v_b_w_in), "b_w_out": (m_b_w_out, v_b_w_out),
        "ple_w0": (m_ple_w[0], v_ple_w[0]), "ple_w1": (m_ple_w[1], v_ple_w[1]),
        "gate0": (m_ple_gate_w[0], v_ple_gate_w[0]), "gate1": (m_ple_gate_w[1], v_ple_gate_w[1]),
    }
    deltas, new_ms, new_vs = _adamw_shards(
        [big[k] for k in names], grads, [mom[k][0].reshape(big[k].shape) for k in names],
        [mom[k][1].reshape(big[k].shape) for k in names])
    res = {k: (grads[i], deltas[i], new_ms[i], new_vs[i]) for i, k in enumerate(names)}

    rows = jnp.concatenate([d_kvn, d_bn, d_an, d_as, jnp.zeros((2, D), F32), sq_err, jnp.zeros((1, D), F32)], axis=0)
    tot = _all_reduce_small(rows, (dqg, dkg))
    loss = tot[6, 0]
    g_kvn, g_bn = tot[0:1], tot[1:2]
    g_an = lax.dynamic_slice_in_dim(tot[2:3], me * 128, 128, axis=1)
    g_as = lax.dynamic_slice_in_dim(tot[3:4], me * 128, 128, axis=1)
    g_qg, g_kg = tot[4:5, :HEAD_DIM], tot[5:6, :HEAD_DIM]
    sm = {
        "a_norm": (g_an,) + _adamw_small(a_norm, g_an, m_a_norm, v_a_norm, "adamw_a_norm"),
        "a_scale": (g_as,) + _adamw_small(a_scale, g_as, m_a_scale, v_a_scale, "adamw_a_scale"),
        "kv_norm": tuple(t.reshape(D) for t in (g_kvn,) + _adamw_small(
            kvn, g_kvn, m_kv_norm.reshape(1, D), v_kv_norm.reshape(1, D), "adamw_kv_norm")),
        "k_norm": tuple(t.reshape(HEAD_DIM) for t in (g_kg,) + _adamw_small(
            kg, g_kg, m_k_norm.reshape(1, HEAD_DIM), v_k_norm.reshape(1, HEAD_DIM), "adamw_k_norm")),
        "b_norm": (g_bn,) + _adamw_small(b_norm, g_bn, m_b_norm, v_b_norm, "adamw_b_norm"),
        "b_q_norm": (g_qg,) + _adamw_small(b_q_norm, g_qg, m_b_q_norm, v_b_q_norm, "adamw_b_q_norm"),
    }

    def out(kind):
        def big_one(k, shape):
            return res[k][kind].reshape(shape)

        return [
            sm["a_norm"][kind], big_one("a_w_in", a_w_in.shape), big_one("a_w_group", a_w_group.shape),
            sm["a_scale"][kind], big_one("a_w_out", a_w_out.shape), sm["kv_norm"][kind],
            big_one("w_kv", w_kv.shape), sm["k_norm"][kind], sm["b_norm"][kind],
            big_one("b_w_in", b_w_in.shape), sm["b_q_norm"][kind], big_one("b_w_out", b_w_out.shape),
            jnp.stack([res["ple_w0"][kind], res["ple_w1"][kind]]),
            jnp.stack([res["gate0"][kind], res["gate1"][kind]]),
        ]

    return (loss, dx0.reshape(x.shape), *out(0), *out(1), *out(2), *out(3))
```

```python
import functools

import jax
import jax.numpy as jnp
from jax import lax
from jax.experimental import pallas as pl
from jax.experimental.pallas import tpu as pltpu

F32 = jnp.float32
BF16 = jnp.bfloat16
MESH = pl.DeviceIdType.MESH

N_DEV = 8
D = 1024
N_GROUPS = 4
GROUP_DIM = D // N_GROUPS
HALO = 16
HEADS = 16
HEAD_DIM = D // HEADS
SB_SCALE = HEAD_DIM ** -0.5
TILE = 256
LANES = 128
DEAD_LOG = -120.0
EPS = 1e-6
ADAM_LR = 0.001
ADAM_B1 = 0.9
ADAM_B2 = 0.999
ADAM_EPS = 1e-08
ADAM_WD = 0.01
ADAM_STEP = 10
TM = 256
VMEM_LIMIT = 56 * 1024 * 1024

HBM_SPEC = pl.BlockSpec(memory_space=pltpu.HBM)


def _dot(a, b):
    return jnp.dot(a, b, preferred_element_type=F32)


def _dot_nt(a, b):
    return lax.dot_general(a, b, (((1,), (1,)), ((), ())), preferred_element_type=F32)


def _dot_tn(a, b):
    return lax.dot_general(a, b, (((0,), (0,)), ((), ())), preferred_element_type=F32)


def _sigmoid(x):
    return jax.nn.sigmoid(x)


def _split_dot(x, mat):
    hi = x.astype(BF16)
    lo = (x - hi.astype(F32)).astype(BF16)
    return _dot(hi, mat) + _dot(lo, mat)


def _split_dot_many(xs, mat):
    rows = xs[0].shape[0]
    his = [x.astype(BF16) for x in xs]
    los = [(x - hi.astype(F32)).astype(BF16) for x, hi in zip(xs, his)]
    out = _dot(jnp.concatenate(his + los, axis=0), mat)
    n = len(xs)
    return [out[i * rows:(i + 1) * rows] + out[(n + i) * rows:(n + i + 1) * rows] for i in range(n)]


def _rms(x):
    return lax.rsqrt(jnp.mean(x * x, axis=-1, keepdims=True) + EPS)


def _rows_call(body, name, n_rows, row_ins, const_ins, row_outs, const_outs=(), scratch=(),
               reverse=False, tm=TM, gather=()):
    nb = n_rows // tm
    ng = len(gather)
    n_in = len(row_ins) + len(const_ins)
    n_out = len(row_outs) + len(const_outs)

    def row_map(i):
        return ((nb - 1 - i) if reverse else i, 0)

    def const_map(nd):
        return lambda i: (0,) * nd

    def with_gather(*refs):
        ins, shards = refs[:n_in], refs[n_in:n_in + ng]
        outs, gathered = refs[n_in + ng:n_in + ng + n_out], refs[n_in + ng + n_out:n_in + 2 * ng + n_out]
        rest = refs[n_in + 2 * ng + n_out:]
        moving = _Gather(shards, gathered, *rest[len(scratch):])
        pl.when(pl.program_id(0) == 0)(moving.start)
        body(*ins, *outs, *rest[:len(scratch)])
        pl.when(pl.program_id(0) == nb - 1)(moving.finish)

    in_specs = [pl.BlockSpec((tm, a.shape[1]), row_map) for a in row_ins]
    in_specs += [pl.BlockSpec(a.shape, const_map(a.ndim)) for a in const_ins] + [HBM_SPEC] * ng
    out_specs = [pl.BlockSpec((tm, w), row_map) for (w, _) in row_outs]
    out_specs += [pl.BlockSpec(s, const_map(len(s))) for (s, _) in const_outs] + [HBM_SPEC] * ng
    out_shape = [jax.ShapeDtypeStruct((n_rows, w), dt) for (w, dt) in row_outs]
    out_shape += [jax.ShapeDtypeStruct(s, dt) for (s, dt) in const_outs] + _Gather.out_shape(gather)
    return pl.pallas_call(
        with_gather if ng else body, name=name, grid=(nb,), in_specs=in_specs, out_specs=out_specs,
        out_shape=out_shape, scratch_shapes=list(scratch) + (_Gather.semaphores(ng) if ng else []),
        compiler_params=pltpu.CompilerParams(dimension_semantics=("arbitrary",),
                                             vmem_limit_bytes=VMEM_LIMIT),
    )(*row_ins, *const_ins, *gather)


def _ple_fwd(p_ref, xin, wple_ref, wgate_ref, e_ref, gt_ref):
    pb = p_ref[...].astype(BF16)
    for j in range(N_DEV):
        e_ref[:, j * 128:(j + 1) * 128] = _dot(pb, wple_ref[j])
    gt = _sigmoid(_dot(xin.astype(BF16), wgate_ref[...]))
    gt_ref[...] = gt
    return xin + e_ref[...] * gt


def _layer_a_fwd(x0, p0, a_norm, a_scale, w_in, w_group, w_out, w_ple, w_gate, gather=()):
    s = x0.shape[0]
    tm = TM

    def body(x_ref, p_ref, an_ref, as_ref, win_ref, wg_ref, wout_ref, wple_ref, wgate_ref,
             h_ref, z_ref, pooled_ref, m_ref, y_ref, x1_ref, e_ref, gt_ref, x2_ref, uext):
        i = pl.program_id(0)

        @pl.when(i == 0)
        def _():
            uext[0:HALO, :] = jnp.zeros((HALO, D), F32)

        x = x_ref[...]
        h = (x * _rms(x) * an_ref[...]).astype(BF16)
        h_ref[...] = h
        for j in range(N_DEV):
            uz = _dot(h, win_ref[j])
            if j < 4:
                uext[HALO:HALO + tm, j * 256:(j + 1) * 256] = uz
            else:
                z_ref[:, (j - 4) * 256:(j - 3) * 256] = uz
        t = i * tm + lax.broadcasted_iota(jnp.int32, (tm, 1), 0)
        for g in range(N_GROUPS):
            w = 2 ** (g + 1)
            cols = slice(g * GROUP_DIM, (g + 1) * GROUP_DIM)
            ext = uext[:, cols]
            acc = ext
            k = 1
            while k < w:
                acc = acc + pltpu.roll(acc, k, 0)
                k *= 2
            cnt = jnp.minimum(t + 1, w).astype(F32)
            pooled = (acc[HALO:] / cnt - ext[HALO:]).astype(BF16)
            pooled_ref[:, cols] = pooled
            m_ref[:, cols] = _dot(pooled, wg_ref[g])
        uext[0:HALO, :] = uext[tm:tm + HALO, :]
        z = z_ref[...]
        y = (m_ref[...] * as_ref[...] * (z * _sigmoid(z))).astype(BF16)
        y_ref[...] = y
        x1 = x + _dot(y, wout_ref[...])
        x1_ref[...] = x1
        x2_ref[...] = _ple_fwd(p_ref, x1, wple_ref, wgate_ref, e_ref, gt_ref)

    row_outs = [(D, BF16), (D, F32), (D, BF16), (D, F32), (D, BF16), (D, F32), (D, F32), (D, F32), (D, F32)]
    return _rows_call(body, "layer_a_fwd", s, [x0, p0], [a_norm, a_scale, w_in, w_group, w_out, w_ple, w_gate],
                      row_outs, scratch=[pltpu.VMEM((tm + HALO, D), F32)], gather=gather)


def _layer_b_in_fwd(x2, kv_norm, b_norm, w_kv, w_bin, gather=()):
    s = x2.shape[0]

    def body(x_ref, kvn_ref, bn_ref, wkv_ref, wbin_ref, hkv_ref, hb_ref, k_ref, v_ref, q_ref, zb_ref):
        x = x_ref[...]
        n = x * _rms(x)
        hkv = (n * kvn_ref[...]).astype(BF16)
        hb = (n * bn_ref[...]).astype(BF16)
        hkv_ref[...] = hkv
        hb_ref[...] = hb
        for j in range(N_DEV):
            kv = _dot(hkv, wkv_ref[j])
            qz = _dot(hb, wbin_ref[j])
            if j < 4:
                cols = slice(j * 256, (j + 1) * 256)
                k_ref[:, cols] = kv
                q_ref[:, cols] = qz
            else:
                cols = slice((j - 4) * 256, (j - 3) * 256)
                v_ref[:, cols] = kv.astype(BF16)
                zb_ref[:, cols] = qz

    row_outs = [(D, BF16), (D, BF16), (D, F32), (D, BF16), (D, F32), (D, F32)]
    return _rows_call(body, "layer_b_in_fwd", s, [x2], [kv_norm, b_norm, w_kv, w_bin], row_outs, gather=gather)


def _tri(after):
    r = lax.broadcasted_iota(jnp.int32, (TILE, TILE), 0)
    c = lax.broadcasted_iota(jnp.int32, (TILE, TILE), 1)
    return jnp.where((r > c) if after else (r < c), 1.0, 0.0).astype(BF16)


def _half_sums(v):
    r = lax.broadcasted_iota(jnp.int32, (LANES, LANES), 0) < HEAD_DIM
    c = lax.broadcasted_iota(jnp.int32, (LANES, LANES), 1) < HEAD_DIM
    same_head = jnp.where(r == c, 1.0, 0.0).astype(BF16)
    return _split_dot(v, same_head)


def _pair_norm(x):
    r = lax.rsqrt(_half_sums(x * x) * (1.0 / HEAD_DIM) + EPS)
    return x * r, r


def _tile_logits(qblk, kblk, diagonal):
    l = _dot_nt(qblk, kblk)
    sp = jnp.maximum(l, 0.0) + jnp.log(1.0 + jnp.exp(-jnp.abs(l)))
    ls = l - sp
    if not diagonal:
        return None, -sp, ls
    mask = lax.broadcasted_iota(jnp.int32, l.shape, 1) < lax.broadcasted_iota(jnp.int32, l.shape, 0)
    return mask, jnp.where(mask, -sp, 0.0), ls


def _attn_fwd(q_all, k_all, v_all, q_gain2, k_gain2):
    s = q_all.shape[0]
    nt = s // TILE

    def body(q_ref, k_ref, v_ref, qg_ref, kg_ref, o_ref, c_ref, qs, ks, vs, tri, acc, right, cmat):
        tri[...] = _tri(True)
        lane = lax.broadcasted_iota(jnp.int32, (TILE, LANES), 1)
        qn, _ = _pair_norm(q_ref[...])
        kn, _ = _pair_norm(k_ref[...])
        qsc = (qn * qg_ref[...] * SB_SCALE).astype(BF16)
        ksc = (kn * kg_ref[...]).astype(BF16)
        for hh in range(2):
            sl = slice(hh * HEAD_DIM, (hh + 1) * HEAD_DIM)
            qs[hh] = qsc[:, sl]
            ks[hh] = ksc[:, sl]
            vs[hh] = v_ref[:, sl]

        def tile(qrows, kb, diagonal):
            rows = pl.ds(pl.multiple_of(kb * TILE, TILE), TILE)
            loaded = [(qs[hh, qrows, :], ks[hh, rows, :], vs[hh, rows, :], right[hh], cmat[hh], acc[hh])
                      for hh in range(2)]
            logits = [_tile_logits(q, k, diagonal) for q, k, _, _, _, _ in loaded]
            later = _split_dot_many([lk for _, lk, _ in logits], tri[...])
            results = []
            for (q, k, v, rt, cm, ac), (mask, lk, ls), lt in zip(loaded, logits, later):
                a = jnp.exp(ls + lt + rt)
                if diagonal:
                    a = jnp.where(mask, a, 0.0)
                results.append((ac + _dot(a.astype(BF16), v), jnp.where(lane == kb, rt[:, :LANES], cm),
                                rt + jnp.sum(lk, axis=1, keepdims=True)))
            for hh, (ac, cm, rt) in enumerate(results):
                acc[hh] = ac
                cmat[hh] = cm
                right[hh] = rt

        def q_step(qb, _):
            r0 = pl.multiple_of(qb * TILE, TILE)
            qrows = pl.ds(r0, TILE)
            acc[...] = jnp.zeros((2, TILE, HEAD_DIM), F32)
            right[...] = jnp.zeros((2, TILE, TILE), F32)
            cmat[...] = jnp.zeros((2, TILE, LANES), F32)
            tile(qrows, qb, True)

            def live():
                return (jnp.max(right[:, :, :LANES]) > DEAD_LOG).astype(jnp.int32)

            def k_step(c):
                kb = c[0] - 1
                tile(qrows, kb, False)
                return kb, live()

            first, _ = lax.while_loop(lambda c: (c[0] > 0) & (c[1] > 0), k_step, (qb, live()))
            for hh in range(2):
                o_ref[qrows, hh * HEAD_DIM:(hh + 1) * HEAD_DIM] = acc[hh]
                c_ref[hh, qrows, :] = jnp.where(lane == LANES - 1, first.astype(F32), cmat[hh])
            return 0

        lax.fori_loop(0, nt, q_step, 0)

    pair = pl.BlockSpec((s, LANES), lambda h: (0, h))
    gain = pl.BlockSpec((1, LANES), lambda h: (0, 0))
    return pl.pallas_call(
        body, name="attn_fwd", grid=(HEADS // 2,),
        in_specs=[pair, pair, pair, gain, gain],
        out_specs=[pair, pl.BlockSpec((2, s, LANES), lambda h: (h, 0, 0))],
        out_shape=[jax.ShapeDtypeStruct((s, D), F32), jax.ShapeDtypeStruct((HEADS, s, LANES), F32)],
        scratch_shapes=[pltpu.VMEM((2, s, HEAD_DIM), BF16)] * 3
        + [pltpu.VMEM((TILE, TILE), BF16), pltpu.VMEM((2, TILE, HEAD_DIM), F32), pltpu.VMEM((2, TILE, TILE), F32),
           pltpu.VMEM((2, TILE, LANES), F32)],
        compiler_params=pltpu.CompilerParams(dimension_semantics=("arbitrary",), vmem_limit_bytes=VMEM_LIMIT),
    )(q_all, k_all, v_all, q_gain2, k_gain2)


def _layer_b_out_fwd(o, zb, x2, p1, target, w_out, w_ple, w_gate):
    s = o.shape[0]

    def body(o_ref, zb_ref, x2_ref, p_ref, t_ref, wout_ref, wple_ref, wgate_ref,
             yb_ref, x3_ref, e_ref, gt_ref, dx4_ref, loss_ref):
        zb = zb_ref[...]
        yb = (o_ref[...] * (zb * _sigmoid(zb))).astype(BF16)
        yb_ref[...] = yb
        x3 = x2_ref[...] + _dot(yb, wout_ref[...])
        x3_ref[...] = x3
        x4 = _ple_fwd(p_ref, x3, wple_ref, wgate_ref, e_ref, gt_ref)
        d = x4 - t_ref[...]
        dx4_ref[...] = d * (1.0 / D)

        @pl.when(pl.program_id(0) == 0)
        def _():
            loss_ref[...] = jnp.zeros((1, D), F32)

        loss_ref[...] += jnp.sum(d * d, axis=0, keepdims=True)

    row_outs = [(D, BF16), (D, F32), (D, F32), (D, F32), (D, F32)]
    return _rows_call(body, "layer_b_out_fwd", s, [o, zb, x2, p1, target], [w_out, w_ple, w_gate], row_outs,
                      const_outs=[((1, D), F32)])


def _ple_bwd(dxo, e_ref, gt_ref, wgate_ref, de_ref, dgp_ref):
    e = e_ref[...]
    gt = gt_ref[...]
    de_ref[...] = (dxo * gt).astype(BF16)
    dgp = (dxo * e * gt * (1.0 - gt)).astype(BF16)
    dgp_ref[...] = dgp
    return dxo + _dot_nt(dgp, wgate_ref[...])


def _silu_grads(z):
    sg = _sigmoid(z)
    return z * sg, sg * (1.0 + z * (1.0 - sg))


def _layer_b_out_bwd(dx4, e1, gt1, o, zb, w_gate, w_out):
    s = dx4.shape[0]

    def body(dx4_ref, e_ref, gt_ref, o_ref, zb_ref, wgate_ref, wout_ref,
             de_ref, dgp_ref, dx3_ref, do_ref, dzb_ref):
        dx3 = _ple_bwd(dx4_ref[...], e_ref, gt_ref, wgate_ref, de_ref, dgp_ref)
        dx3_ref[...] = dx3
        dyb = _dot_nt(dx3.astype(BF16), wout_ref[...])
        silu, dsilu = _silu_grads(zb_ref[...])
        do_ref[...] = (dyb * silu).astype(BF16)
        dzb_ref[...] = (dyb * o_ref[...] * dsilu).astype(BF16)

    row_outs = [(D, BF16), (D, BF16), (D, F32), (D, BF16), (D, BF16)]
    return _rows_call(body, "layer_b_out_bwd", s, [dx4, e1, gt1, o, zb], [w_gate, w_out], row_outs)


def _attn_bwd(q_all, k_all, v_all, q_gain2, k_gain2, d_o, csave):
    s = q_all.shape[0]
    nt = s // TILE

    def body(q_ref, k_ref, v_ref, qg_ref, kg_ref, do_ref, c_ref,
             dq_ref, dk_ref, dv_ref, dqg_ref, dkg_ref,
             qs, ks, vs, dos, qt, dot_t, tri_a, tri_b, dqa, dkt, dvt, dqb, left):
        tri_a[...] = _tri(True)
        tri_b[...] = _tri(False)
        lane = lax.broadcasted_iota(jnp.int32, (TILE, LANES), 1)
        qn, qr = _pair_norm(q_ref[...])
        kn, kr = _pair_norm(k_ref[...])
        qsc = qn * qg_ref[...] * SB_SCALE
        ksc = (kn * kg_ref[...]).astype(BF16)
        q_t = qsc.T.astype(BF16)
        do_t = do_ref[...].astype(F32).T.astype(BF16)
        for j in range(nt):
            qt[j] = q_t[:, j * TILE:(j + 1) * TILE]
            dot_t[j] = do_t[:, j * TILE:(j + 1) * TILE]
        dkt[...] = jnp.zeros((nt, LANES, TILE), F32)
        dvt[...] = jnp.zeros((nt, LANES, TILE), F32)
        qsc = qsc.astype(BF16)
        for hh in range(2):
            sl = slice(hh * HEAD_DIM, (hh + 1) * HEAD_DIM)
            qs[hh] = qsc[:, sl]
            ks[hh] = ksc[:, sl]
            vs[hh] = v_ref[:, sl]
            dos[hh] = do_ref[:, sl]

        def tile(qb, qrows, kb, diagonal):
            rows = pl.ds(pl.multiple_of(kb * TILE, TILE), TILE)
            heads = range(2)
            kblk = [ks[hh, rows, :] for hh in heads]
            logits = [_tile_logits(qs[hh, qrows, :], kblk[hh], diagonal) for hh in heads]
            later = _split_dot_many([lk for _, lk, _ in logits], tri_a[...])
            a, g = [], []
            for hh in heads:
                mask, _, ls = logits[hh]
                right = jnp.sum(jnp.where(lane == kb, c_ref[hh, qrows, :], 0.0), axis=1, keepdims=True)
                a_h = jnp.exp(ls + later[hh] + right)
                a.append(jnp.where(mask, a_h, 0.0) if diagonal else a_h)
                g.append(a[hh] * _dot_nt(dos[hh, qrows, :], vs[hh, rows, :]))
            before = _split_dot_many(g, tri_b[...])
            for hh in heads:
                sl = slice(hh * HEAD_DIM, (hh + 1) * HEAD_DIM)
                mask, _, ls = logits[hh]
                beta = jnp.exp(ls)
                lf = left[hh]
                dl = g[hh] * (1.0 - beta) - (before[hh] + lf) * beta
                if diagonal:
                    dl = jnp.where(mask, dl, 0.0)
                dl = dl.astype(BF16)
                left[hh] = lf + jnp.sum(g[hh], axis=1, keepdims=True)
                dqb[hh] += _dot(dl, kblk[hh])
                dkt[kb, sl, :] += _dot(qt[qb, sl, :], dl)
                dvt[kb, sl, :] += _dot(dot_t[qb, sl, :], a[hh].astype(BF16))

        def q_step(qb, _):
            qrows = pl.ds(pl.multiple_of(qb * TILE, TILE), TILE)
            dqb[...] = jnp.zeros((2, TILE, HEAD_DIM), F32)
            left[...] = jnp.zeros((2, TILE, TILE), F32)

            def k_step(kb, _):
                tile(qb, qrows, kb, False)
                return 0

            first = jnp.max(jnp.where(lane == LANES - 1, c_ref[0, qrows, :], 0.0)).astype(jnp.int32)
            lax.fori_loop(first, qb, k_step, 0)
            tile(qb, qrows, qb, True)
            for hh in range(2):
                dqa[qrows, hh * HEAD_DIM:(hh + 1) * HEAD_DIM] = dqb[hh] * SB_SCALE
            return 0

        lax.fori_loop(0, nt, q_step, 0)

        def norm_bwd(dy, xn, r, g_ref, dx_ref, dg_ref):
            dg_ref[...] = jnp.sum(dy * xn, axis=0, keepdims=True)
            dxn = dy * g_ref[...]
            dx_ref[...] = r * (dxn - xn * (_half_sums(dxn * xn) * (1.0 / HEAD_DIM)))

        norm_bwd(dqa[...], qn, qr, qg_ref, dq_ref, dqg_ref)
        for j in range(nt):
            dqa[j * TILE:(j + 1) * TILE, :] = dkt[j].T
            dv_ref[j * TILE:(j + 1) * TILE, :] = dvt[j].T
        norm_bwd(dqa[...], kn, kr, kg_ref, dk_ref, dkg_ref)

    pair = pl.BlockSpec((s, LANES), lambda h: (0, h))
    gain = pl.BlockSpec((1, LANES), lambda h: (0, 0))
    dgain = pl.BlockSpec((None, 1, LANES), lambda h: (h, 0, 0))
    return pl.pallas_call(
        body, name="attn_bwd", grid=(HEADS // 2,),
        in_specs=[pair, pair, pair, gain, gain, pair, pl.BlockSpec((2, s, LANES), lambda h: (h, 0, 0))],
        out_specs=[pair, pair, pair, dgain, dgain],
        out_shape=[jax.ShapeDtypeStruct((s, D), F32)] * 3
        + [jax.ShapeDtypeStruct((HEADS // 2, 1, LANES), F32)] * 2,
        scratch_shapes=[pltpu.VMEM((2, s, HEAD_DIM), BF16)] * 4
        + [pltpu.VMEM((nt, LANES, TILE), BF16)] * 2 + [pltpu.VMEM((TILE, TILE), BF16)] * 2
        + [pltpu.VMEM((s, LANES), F32)] + [pltpu.VMEM((nt, LANES, TILE), F32)] * 2
        + [pltpu.VMEM((2, TILE, HEAD_DIM), F32), pltpu.VMEM((2, TILE, TILE), F32)],
        compiler_params=pltpu.CompilerParams(dimension_semantics=("arbitrary",), vmem_limit_bytes=VMEM_LIMIT),
    )(q_all, k_all, v_all, q_gain2, k_gain2, d_o, csave)


def _norm_bwd_rows(dh, x, gain, dgain_ref):
    r = _rms(x)
    n = x * r
    dgain_ref[...] += jnp.sum(dh * n, axis=0, keepdims=True)
    dn = dh * gain
    return r * (dn - n * jnp.mean(dn * n, axis=-1, keepdims=True))


def _layer_b_in_bwd(dq, dzb, dk, dv, x2, dx3, w_bin, w_kv, b_norm, kv_norm):
    s = x2.shape[0]

    def body(dq_ref, dzb_ref, dk_ref, dv_ref, x_ref, dx3_ref, wbin_ref, wkv_ref, bn_ref, kvn_ref,
             dqz_ref, dkv_ref, dx2_ref, dbn_ref, dkvn_ref):
        @pl.when(pl.program_id(0) == 0)
        def _():
            dbn_ref[...] = jnp.zeros((1, D), F32)
            dkvn_ref[...] = jnp.zeros((1, D), F32)

        dqz_ref[:, :D] = dq_ref[...].astype(BF16)
        dqz_ref[:, D:] = dzb_ref[...]
        dkv_ref[:, :D] = dk_ref[...].astype(BF16)
        dkv_ref[:, D:] = dv_ref[...].astype(BF16)
        dhb = jnp.zeros((TM, D), F32)
        dhkv = jnp.zeros((TM, D), F32)
        for j in range(N_DEV):
            cols = slice(j * 256, (j + 1) * 256)
            dhb = dhb + _dot_nt(dqz_ref[:, cols], wbin_ref[j])
            dhkv = dhkv + _dot_nt(dkv_ref[:, cols], wkv_ref[j])
        x = x_ref[...]
        dx2 = dx3_ref[...] + _norm_bwd_rows(dhb, x, bn_ref[...], dbn_ref)
        dx2_ref[...] = dx2 + _norm_bwd_rows(dhkv, x, kvn_ref[...], dkvn_ref)

    row_outs = [(2 * D, BF16), (2 * D, BF16), (D, F32)]
    return _rows_call(body, "layer_b_in_bwd", s, [dq, dzb, dk, dv, x2, dx3], [w_bin, w_kv, b_norm, kv_norm],
                      row_outs, const_outs=[((1, D), F32), ((1, D), F32)])


def _layer_a_out_bwd(dx2, e0, gt0, z, m, w_gate, w_out, a_scale, w_group):
    s = dx2.shape[0]
    tm = TM
    nb = s // tm

    def body(dx2_ref, e_ref, gt_ref, z_ref, m_ref, wgate_ref, wout_ref, as_ref, wg_ref,
             de_ref, dgp_ref, dx1_ref, dm_ref, duz_ref, das_ref, ext):
        i = pl.program_id(0)

        @pl.when(i == 0)
        def _():
            das_ref[...] = jnp.zeros((1, D), F32)
            ext[tm:tm + HALO, :] = jnp.zeros((HALO, D), F32)

        dx1 = _ple_bwd(dx2_ref[...], e_ref, gt_ref, wgate_ref, de_ref, dgp_ref)
        dx1_ref[...] = dx1
        dy = _dot_nt(dx1.astype(BF16), wout_ref[...])
        silu, dsilu = _silu_grads(z_ref[...])
        m = m_ref[...]
        dmixed = dy * silu
        duz_ref[:, D:] = (dy * (m * as_ref[...]) * dsilu).astype(BF16)
        das_ref[...] += jnp.sum(dmixed * m, axis=0, keepdims=True)
        dm_ref[...] = (dmixed * as_ref[...]).astype(BF16)
        t = (nb - 1 - i) * tm + lax.broadcasted_iota(jnp.int32, (tm, 1), 0)
        n_ext = tm + HALO
        for g in range(N_GROUPS):
            w = 2 ** (g + 1)
            cols = slice(g * GROUP_DIM, (g + 1) * GROUP_DIM)
            dpool = _dot_nt(dm_ref[:, cols], wg_ref[g])
            ext[0:tm, cols] = dpool / jnp.minimum(t + 1, w).astype(F32)
            acc = ext[:, cols]
            k = 1
            while k < w:
                acc = acc + pltpu.roll(acc, n_ext - k, 0)
                k *= 2
            duz_ref[:, cols] = (acc[:tm] - dpool).astype(BF16)
        ext[tm:tm + HALO, :] = ext[0:HALO, :]

    row_outs = [(D, BF16), (D, BF16), (D, F32), (D, BF16), (2 * D, BF16)]
    return _rows_call(body, "layer_a_out_bwd", s, [dx2, e0, gt0, z, m], [w_gate, w_out, a_scale, w_group],
                      row_outs, const_outs=[((1, D), F32)], scratch=[pltpu.VMEM((tm + HALO, D), F32)],
                      reverse=True)


def _layer_a_in_bwd(duz, x0, dx1, w_in, a_norm):
    s = x0.shape[0]

    def body(duz_ref, x_ref, dx1_ref, win_ref, an_ref, dx0_ref, dan_ref):
        @pl.when(pl.program_id(0) == 0)
        def _():
            dan_ref[...] = jnp.zeros((1, D), F32)

        dh = jnp.zeros((TM, D), F32)
        for j in range(N_DEV):
            dh = dh + _dot_nt(duz_ref[:, j * 256:(j + 1) * 256], win_ref[j])
        dx0_ref[...] = dx1_ref[...] + _norm_bwd_rows(dh, x_ref[...], an_ref[...], dan_ref)

    return _rows_call(body, "layer_a_in_bwd", s, [duz, x0, dx1], [w_in, a_norm], [(D, F32)],
                      const_outs=[((1, D), F32)])


def _wgrad(a, b, name, n_split=1, a_blocked_b=False):
    s, k = a.shape
    n = b.shape[1]
    tk = 256
    nb = n // n_split

    def body(a_ref, b_ref, o_ref):
        res = _dot_tn(a_ref[...].astype(BF16), b_ref[...].astype(BF16))
        if n_split == 1:
            o_ref[...] = res.astype(BF16)
        else:
            for j in range(n_split):
                o_ref[j] = res[:, j * nb:(j + 1) * nb].astype(BF16)

    if a_blocked_b:
        b_spec = pl.BlockSpec((s, tk), lambda i: (0, i))
        out_spec = pl.BlockSpec((None, tk, tk), lambda i: (i, 0, 0))
        out_shape = jax.ShapeDtypeStruct((k // tk, tk, tk), BF16)
    elif n_split == 1:
        b_spec = pl.BlockSpec((s, n), lambda i: (0, 0))
        out_spec = pl.BlockSpec((tk, n), lambda i: (i, 0))
        out_shape = jax.ShapeDtypeStruct((k, n), BF16)
    else:
        b_spec = pl.BlockSpec((s, n), lambda i: (0, 0))
        out_spec = pl.BlockSpec((n_split, tk, nb), lambda i: (0, i, 0))
        out_shape = jax.ShapeDtypeStruct((n_split, k, nb), BF16)
    return pl.pallas_call(
        body, name=name, grid=(k // tk,),
        in_specs=[pl.BlockSpec((s, tk), lambda i: (0, i)), b_spec],
        out_specs=out_spec, out_shape=out_shape,
        compiler_params=pltpu.CompilerParams(dimension_semantics=("arbitrary",), vmem_limit_bytes=VMEM_LIMIT),
    )(a, b)


def _cast_shards(shards):
    n = len(shards)

    def body(*refs):
        for a in range(n):
            refs[n + a][...] = refs[a][...].astype(BF16)

    vmem = pl.BlockSpec(memory_space=pltpu.VMEM)
    return pl.pallas_call(
        body, name="cast_shards", in_specs=[vmem] * n, out_specs=[vmem] * n,
        out_shape=[jax.ShapeDtypeStruct(a.shape, BF16) for a in shards],
        compiler_params=pltpu.CompilerParams(vmem_limit_bytes=VMEM_LIMIT),
    )(*shards)


def _adamw(w, g, m, v):
    m = ADAM_B1 * m + (1.0 - ADAM_B1) * g
    v = ADAM_B2 * v + (1.0 - ADAM_B2) * jnp.square(g)
    m_hat = m / (1.0 - ADAM_B1 ** ADAM_STEP)
    v_hat = v / (1.0 - ADAM_B2 ** ADAM_STEP)
    delta = -ADAM_LR * (m_hat / (jnp.sqrt(v_hat) + ADAM_EPS) + ADAM_WD * w)
    return delta, m, v


def _adamw_small(w, g, m, v, name):
    def body(w_ref, g_ref, m_ref, v_ref, d_ref, m2_ref, v2_ref):
        d_ref[...], m2_ref[...], v2_ref[...] = _adamw(w_ref[...], g_ref[...], m_ref[...], v_ref[...])

    vmem = pl.BlockSpec(memory_space=pltpu.VMEM)
    return pl.pallas_call(
        body, name=name, in_specs=[vmem] * 4, out_specs=[vmem] * 3,
        out_shape=[jax.ShapeDtypeStruct(w.shape, F32)] * 3,
    )(w, g, m, v)


def _place():
    return lax.axis_index("x"), lax.axis_index("y"), lax.axis_index("c")


def _all_gather(shards):
    n = len(shards)

    def body(*refs):
        gather = _Gather(refs[:n], refs[n:2 * n], *refs[2 * n:])
        gather.start()
        gather.finish()

    return pl.pallas_call(
        body, name="all_gather_weights",
        in_specs=[HBM_SPEC] * n, out_specs=[HBM_SPEC] * n,
        out_shape=_Gather.out_shape(shards), scratch_shapes=_Gather.semaphores(n),
    )(*shards)


class _Gather:
    def __init__(self, ins, outs, send_sems, recv_sems, local_sems):
        self.ins, self.outs = ins, outs
        self.send_sems, self.recv_sems, self.local_sems = send_sems, recv_sems, local_sems
        self.x, self.y, self.c = _place()

    @staticmethod
    def out_shape(shards):
        return [jax.ShapeDtypeStruct((N_DEV,) + a.shape, a.dtype) for a in shards]

    @staticmethod
    def semaphores(n):
        return [pltpu.SemaphoreType.DMA((7 * n,)), pltpu.SemaphoreType.DMA((7 * n,)), pltpu.SemaphoreType.DMA((n,))]

    def _chips(self):
        x, y = self.x, self.y
        return [(1 - x, y), (x, 1 - y), (1 - x, 1 - y)]

    def _copy(self, a, k, block, to, own=False):
        px, py, pc = block
        slot = self.outs[a].at[4 * px + 2 * py + pc]
        return pltpu.make_async_remote_copy(
            src_ref=self.ins[a] if own else slot, dst_ref=slot,
            send_sem=self.send_sems.at[7 * a + k], recv_sem=self.recv_sems.at[7 * a + k],
            device_id=to, device_id_type=MESH)

    def _local(self, a):
        return pltpu.make_async_copy(self.ins[a], self.outs[a].at[4 * self.x + 2 * self.y + self.c],
                                     self.local_sems.at[a])

    def _first(self, a):
        me, sibling = (self.x, self.y, self.c), (self.x, self.y, 1 - self.c)
        return [self._copy(a, 0, me, sibling, own=True)] + [
            self._copy(a, 1 + j, me, (*chip, self.c), own=True) for j, chip in enumerate(self._chips())]

    def start(self):
        for a in range(len(self.ins)):
            self._local(a).start()
            for cp in self._first(a):
                cp.start()

    def finish(self):
        c = self.c
        me, sibling = (self.x, self.y, c), (self.x, self.y, 1 - c)
        n = len(self.ins)
        passed = []
        for a in range(n):
            for j, chip in enumerate(self._chips()):
                self._copy(a, 1 + j, (*chip, c), me).wait_recv()
                passed.append(self._copy(a, 4 + j, (*chip, c), sibling))
                passed[-1].start()
        for a in range(n):
            self._copy(a, 0, sibling, me).wait_recv()
            for j, chip in enumerate(self._chips()):
                self._copy(a, 4 + j, (*chip, 1 - c), me).wait_recv()
        for a in range(n):
            for cp in self._first(a):
                cp.wait_send()
            self._local(a).wait()
        for cp in passed:
            cp.wait_send()


def _reduce_cores(partials):
    n = len(partials)

    def body(*refs):
        ins, own32, chip16 = refs[:n], refs[n:2 * n], refs[2 * n:3 * n]
        landed, mine = refs[3 * n:4 * n], refs[4 * n:5 * n]
        send_sems, recv_sems, local_sems = refs[5 * n:]
        x, y, c = _place()
        plane = 2 * x + y
        remote, local = [], []
        for a in range(n):
            for k in range(4):
                remote.append(pltpu.make_async_remote_copy(
                    src_ref=ins[a].at[2 * k + (1 - c)], dst_ref=landed[a].at[k],
                    send_sem=send_sems.at[4 * a + k], recv_sem=recv_sems.at[4 * a + k],
                    device_id=(x, y, 1 - c), device_id_type=MESH))
                local.append(pltpu.make_async_copy(ins[a].at[2 * k + c], mine[a].at[k], local_sems.at[4 * a + k]))
                remote[-1].start()
                local[-1].start()
        for a in range(n):
            for k in range(4):
                local[4 * a + k].wait()
                remote[4 * a + k].wait_recv()
                chip16[a][k] = (mine[a][k].astype(F32) + landed[a][k].astype(F32)).astype(BF16)
            own32[a][...] = mine[a][plane].astype(F32) + landed[a][plane].astype(F32)
        for cp in remote:
            cp.wait_send()

    vmem = pl.BlockSpec(memory_space=pltpu.VMEM)
    quarter = [pltpu.VMEM((4,) + a.shape[1:], BF16) for a in partials]
    return pl.pallas_call(
        body, name="reduce_cores",
        in_specs=[HBM_SPEC] * n, out_specs=[vmem] * (2 * n),
        out_shape=[jax.ShapeDtypeStruct(a.shape[1:], F32) for a in partials]
        + [jax.ShapeDtypeStruct((4,) + a.shape[1:], BF16) for a in partials],
        scratch_shapes=quarter + quarter
        + [pltpu.SemaphoreType.DMA((4 * n,)), pltpu.SemaphoreType.DMA((4 * n,)), pltpu.SemaphoreType.DMA((4 * n,))],
        compiler_params=pltpu.CompilerParams(vmem_limit_bytes=VMEM_LIMIT),
    )(*partials)


def _reduce_chips(own32, chip16):
    n = len(own32)

    def body(*refs):
        p32, p16, grads, landed = refs[:n], refs[n:2 * n], refs[2 * n:3 * n], refs[3 * n:4 * n]
        send_sems, recv_sems = refs[4 * n:]
        x, y, c = _place()
        peers = [(1 - x, y), (x, 1 - y), (1 - x, 1 - y)]
        copies = []
        for a in range(n):
            for k, (px, py) in enumerate(peers):
                copies.append(pltpu.make_async_remote_copy(
                    src_ref=p16[a].at[2 * px + py], dst_ref=landed[a].at[k],
                    send_sem=send_sems.at[3 * a + k], recv_sem=recv_sems.at[3 * a + k],
                    device_id=(px, py, c), device_id_type=MESH))
                copies[-1].start()
        for a in range(n):
            g = p32[a][...]
            for k in range(3):
                copies[3 * a + k].wait_recv()
                g = g + landed[a][k].astype(F32)
            grads[a][...] = g
        for cp in copies:
            cp.wait_send()

    vmem = pl.BlockSpec(memory_space=pltpu.VMEM)
    return pl.pallas_call(
        body, name="reduce_chips",
        in_specs=[vmem] * (2 * n), out_specs=[vmem] * n,
        out_shape=[jax.ShapeDtypeStruct(a.shape, F32) for a in own32],
        scratch_shapes=[pltpu.VMEM((3,) + a.shape, BF16) for a in own32]
        + [pltpu.SemaphoreType.DMA((3 * n,)), pltpu.SemaphoreType.DMA((3 * n,))],
        compiler_params=pltpu.CompilerParams(vmem_limit_bytes=VMEM_LIMIT),
    )(*own32, *chip16)


def _adamw_shards(ws, gs, ms, vs):
    n = len(ws)

    def body(*refs):
        w, g, m, v = refs[:n], refs[n:2 * n], refs[2 * n:3 * n], refs[3 * n:4 * n]
        outs = refs[4 * n:]
        for a in range(n):
            outs[a][...], outs[n + a][...], outs[2 * n + a][...] = _adamw(w[a][...], g[a][...], m[a][...], v[a][...])

    vmem = pl.BlockSpec(memory_space=pltpu.VMEM)
    res = pl.pallas_call(
        body, name="adamw_shards", in_specs=[vmem] * (4 * n), out_specs=[vmem] * (3 * n),
        out_shape=[jax.ShapeDtypeStruct(a.shape, F32) for a in ws] * 3,
        compiler_params=pltpu.CompilerParams(vmem_limit_bytes=VMEM_LIMIT),
    )(*ws, *gs, *ms, *vs)
    return res[:n], res[n:2 * n], res[2 * n:]


def _all_reduce_small(rows, gain_parts):
    def body(rows_ref, dqg_ref, dkg_ref, out_ref, buf, send_sems, recv_sems):
        x, y, c = _place()
        me = 4 * x + 2 * y + c
        buf[0] = rows_ref[...]
        for row, part in ((4, dqg_ref), (5, dkg_ref)):
            both = jnp.sum(part[...].reshape(HEADS // 2, LANES), axis=0, keepdims=True)
            buf[0, row:row + 1, 0:HEAD_DIM] = both[:, :HEAD_DIM] + both[:, HEAD_DIM:]
        copies = []
        for r in range(1, N_DEV):
            bx, by, bc = (r >> 2) & 1, (r >> 1) & 1, r & 1
            to = (x ^ bx, y ^ by, c ^ bc)
            copies.append(pltpu.make_async_remote_copy(
                src_ref=buf.at[0], dst_ref=buf.at[r], send_sem=send_sems.at[r - 1], recv_sem=recv_sems.at[r - 1],
                device_id=to, device_id_type=MESH))
        for cp in copies:
            cp.start()
        for cp in copies:
            cp.wait_recv()
        for cp in copies:
            cp.wait_send()
        tot = buf[me]
        for j in range(1, N_DEV):
            tot = tot + buf[j ^ me]
        out_ref[...] = tot
        loss = (0.5 / D) * jnp.sum(tot[6:7, :], axis=1, keepdims=True)
        out_ref[6:7, :] = jnp.broadcast_to(loss, (1, D))

    vmem = pl.BlockSpec(memory_space=pltpu.VMEM)
    return pl.pallas_call(
        body, name="all_reduce_small", in_specs=[vmem] * 3, out_specs=vmem,
        out_shape=jax.ShapeDtypeStruct((8, D), F32),
        scratch_shapes=[pltpu.VMEM((N_DEV, 8, D), F32), pltpu.SemaphoreType.DMA((N_DEV - 1,)),
                        pltpu.SemaphoreType.DMA((N_DEV - 1,))],
    )(rows, *gain_parts)


def kernel(x, p, a_norm, a_w_in, a_w_group, a_scale, a_w_out, kv_norm, w_kv, k_norm, b_norm, b_w_in, b_q_norm, b_w_out, ple_w, ple_gate_w, loss_target, m_a_norm, m_a_w_in, m_a_w_group, m_a_scale, m_a_w_out, m_kv_norm, m_w_kv, m_k_norm, m_b_norm, m_b_w_in, m_b_q_norm, m_b_w_out, m_ple_w, m_ple_gate_w, v_a_norm, v_a_w_in, v_a_w_group, v_a_scale, v_a_w_out, v_kv_norm, v_w_kv, v_k_norm, v_b_norm, v_b_w_in, v_b_q_norm, v_b_w_out, v_ple_w, v_ple_gate_w):
    s = x.shape[1]
    xi, yi, ci = _place()
    me = 4 * xi + 2 * yi + ci
    plane = 2 * xi + yi

    big = {
        "a_w_in": a_w_in.reshape(D, 256), "a_w_group": a_w_group.reshape(128, 256),
        "a_w_out": a_w_out.reshape(128, D), "w_kv": w_kv, "b_w_in": b_w_in.reshape(D, 256),
        "b_w_out": b_w_out.reshape(128, D), "ple_w0": ple_w[0], "ple_w1": ple_w[1],
        "gate0": ple_gate_w[0], "gate1": ple_gate_w[1],
    }
    names = list(big)
    cast = dict(zip(names, _cast_shards([big[k] for k in names])))
    small = jnp.concatenate([a_norm, a_scale, jnp.zeros((6, 128), F32)], axis=0)
    first = ["a_w_in", "a_w_group", "a_w_out", "ple_w0", "gate0"]
    behind_a = ["w_kv", "b_w_in"]
    behind_b = ["b_w_out", "ple_w1", "gate1"]
    gathered = _all_gather([cast[k] for k in first] + [small])
    full = dict(zip(first, gathered[:-1]))
    small_all = gathered[-1]
    a_norm_f = small_all[:, 0, :].reshape(1, D)
    a_scale_f = small_all[:, 1, :].reshape(1, D)
    w_a_in = full["a_w_in"]
    w_a_out = full["a_w_out"].reshape(D, D)
    w_gate0 = full["gate0"].reshape(D, D)
    w_ple0 = full["ple_w0"]
    w_group = full["a_w_group"].reshape(N_DEV, 4, 32, 256).transpose(1, 0, 2, 3).reshape(4, 256, 256)
    kvn, bn = kv_norm.reshape(1, D), b_norm
    kg, qg = k_norm.reshape(1, HEAD_DIM), b_q_norm

    x0, p0, p1, target = x[0], p[0, 0], p[1, 0], loss_target[0]
    h0, z, pooled, mcat, y, x1, e0, gt0, x2, w_kv_f, w_b_in = _layer_a_fwd(
        x0, p0, a_norm_f, a_scale_f, w_a_in, w_group, w_a_out, w_ple0, w_gate0, gather=[cast[k] for k in behind_a])
    hkv, hb, k_all, v_all, q_all, zb, w_b_out, w_ple1, w_gate1 = _layer_b_in_fwd(
        x2, kvn, bn, w_kv_f, w_b_in, gather=[cast[k] for k in behind_b])
    w_b_out, w_gate1 = w_b_out.reshape(D, D), w_gate1.reshape(D, D)
    qg2, kg2 = jnp.concatenate([qg, qg], axis=1), jnp.concatenate([kg, kg], axis=1)
    o, csave = _attn_fwd(q_all, k_all, v_all, qg2, kg2)
    yb, x3, e1, gt1, dx4, sq_err = _layer_b_out_fwd(o, zb, x2, p1, target, w_b_out, w_ple1, w_gate1)

    de1, dgp1, dx3, d_o, dzb = _layer_b_out_bwd(dx4, e1, gt1, o, zb, w_gate1, w_b_out)
    dq, dk, dv, dqg, dkg = _attn_bwd(q_all, k_all, v_all, qg2, kg2, d_o, csave)
    dqz, dkv, dx2, d_bn, d_kvn = _layer_b_in_bwd(dq, dzb, dk, dv, x2, dx3, w_b_in, w_kv_f, bn, kvn)
    de0, dgp0, dx1, dm, duz, d_as = _layer_a_out_bwd(dx2, e0, gt0, z, mcat, w_gate0, w_a_out, a_scale_f, w_group)
    dx0, d_an = _layer_a_in_bwd(duz, x0, dx1, w_a_in, a_norm_f)

    dw_group = _wgrad(pooled, dm, "wgrad_a_w_group", a_blocked_b=True)
    partial = {
        "a_w_in": _wgrad(h0, duz, "wgrad_a_w_in", n_split=8),
        "a_w_group": dw_group.reshape(4, N_DEV, 32, 256).transpose(1, 0, 2, 3).reshape(N_DEV, 128, 256),
        "a_w_out": _wgrad(y, dx1, "wgrad_a_w_out").reshape(N_DEV, 128, D),
        "w_kv": _wgrad(hkv, dkv, "wgrad_w_kv", n_split=8),
        "b_w_in": _wgrad(hb, dqz, "wgrad_b_w_in", n_split=8),
        "b_w_out": _wgrad(yb, dx3, "wgrad_b_w_out").reshape(N_DEV, 128, D),
        "ple_w0": _wgrad(p0, de0, "wgrad_ple_w0", n_split=8),
        "ple_w1": _wgrad(p1, de1, "wgrad_ple_w1", n_split=8),
        "gate0": _wgrad(x1, dgp0, "wgrad_gate0").reshape(N_DEV, 128, D),
        "gate1": _wgrad(x3, dgp1, "wgrad_gate1").reshape(N_DEV, 128, D),
    }

    n_big = len(names)
    chip = _reduce_cores([partial[k] for k in names])
    grads = _reduce_chips(chip[:n_big], chip[n_big:])
    mom = {
        "a_w_in": (m_a_w_in, v_a_w_in), "a_w_group": (m_a_w_group, v_a_w_group), "a_w_out": (m_a_w_out, v_a_w_out),
        "w_kv": (m_w_kv, v_w_kv), "b_w_in": (m_b_w_in, v_b_w_in), "b_w_out": (m_b_w_out, v_b_w_out),
        "ple_w0": (m_ple_w[0], v_ple_w[0]), "ple_w1": (m_ple_w[1], v_ple_w[1]),
        "gate0": (m_ple_gate_w[0], v_ple_gate_w[0]), "gate1": (m_ple_gate_w[1], v_ple_gate_w[1]),
    }
    deltas, new_ms, new_vs = _adamw_shards(
        [big[k] for k in names], grads, [mom[k][0].reshape(big[k].shape) for k in names],
        [mom[k][1].reshape(big[k].shape) for k in names])
    res = {k: (grads[i], deltas[i], new_ms[i], new_vs[i]) for i, k in enumerate(names)}

    rows = jnp.concatenate([d_kvn, d_bn, d_an, d_as, jnp.zeros((2, D), F32), sq_err, jnp.zeros((1, D), F32)], axis=0)
    tot = _all_reduce_small(rows, (dqg, dkg))
    loss = tot[6, 0]
    g_kvn, g_bn = tot[0:1], tot[1:2]
    g_an = lax.dynamic_slice_in_dim(tot[2:3], me * 128, 128, axis=1)
    g_as = lax.dynamic_slice_in_dim(tot[3:4], me * 128, 128, axis=1)
    g_qg, g_kg = tot[4:5, :HEAD_DIM], tot[5:6, :HEAD_DIM]
    sm = {
        "a_norm": (g_an,) + _adamw_small(a_norm, g_an, m_a_norm, v_a_norm, "adamw_a_norm"),
        "a_scale": (g_as,) + _adamw_small(a_scale, g_as, m_a_scale, v_a_scale, "adamw_a_scale"),
        "kv_norm": tuple(t.reshape(D) for t in (g_kvn,) + _adamw_small(
            kvn, g_kvn, m_kv_norm.reshape(1, D), v_kv_norm.reshape(1, D), "adamw_kv_norm")),
        "k_norm": tuple(t.reshape(HEAD_DIM) for t in (g_kg,) + _adamw_small(
            kg, g_kg, m_k_norm.reshape(1, HEAD_DIM), v_k_norm.reshape(1, HEAD_DIM), "adamw_k_norm")),
        "b_norm": (g_bn,) + _adamw_small(b_norm, g_bn, m_b_norm, v_b_norm, "adamw_b_norm"),
        "b_q_norm": (g_qg,) + _adamw_small(b_q_norm, g_qg, m_b_q_norm, v_b_q_norm, "adamw_b_q_norm"),
    }

    def out(kind):
        def big_one(k, shape):
            return res[k][kind].reshape(shape)

        return [
            sm["a_norm"][kind], big_one("a_w_in", a_w_in.shape), big_one("a_w_group", a_w_group.shape),
            sm["a_scale"][kind], big_one("a_w_out", a_w_out.shape), sm["kv_norm"][kind],
            big_one("w_kv", w_kv.shape), sm["k_norm"][kind], sm["b_norm"][kind],
            big_one("b_w_in", b_w_in.shape), sm["b_q_norm"][kind], big_one("b_w_out", b_w_out.shape),
            jnp.stack([res["ple_w0"][kind], res["ple_w1"][kind]]),
            jnp.stack([res["gate0"][kind], res["gate1"][kind]]),
        ]

    return (loss, dx0.reshape(x.shape), *out(0), *out(1), *out(2), *out(3))
```

```python
import functools

import jax
import jax.numpy as jnp
from jax import lax
from jax.experimental import pallas as pl
from jax.experimental.pallas import tpu as pltpu

F32 = jnp.float32
BF16 = jnp.bfloat16
MESH = pl.DeviceIdType.MESH

N_DEV = 8
D = 1024
N_GROUPS = 4
GROUP_DIM = D // N_GROUPS
HALO = 16
HEADS = 16
HEAD_DIM = D // HEADS
SB_SCALE = HEAD_DIM ** -0.5
TILE = 256
LANES = 128
DEAD_LOG = -120.0
EPS = 1e-6
ADAM_LR = 0.001
ADAM_B1 = 0.9
ADAM_B2 = 0.999
ADAM_EPS = 1e-08
ADAM_WD = 0.01
ADAM_STEP = 10
TM = 256
VMEM_LIMIT = 56 * 1024 * 1024

HBM_SPEC = pl.BlockSpec(memory_space=pltpu.HBM)


def _dot(a, b):
    return jnp.dot(a, b, preferred_element_type=F32)


def _dot_nt(a, b):
    return lax.dot_general(a, b, (((1,), (1,)), ((), ())), preferred_element_type=F32)


def _dot_tn(a, b):
    return lax.dot_general(a, b, (((0,), (0,)), ((), ())), preferred_element_type=F32)


def _sigmoid(x):
    return jax.nn.sigmoid(x)


def _split_dot(x, mat):
    hi = x.astype(BF16)
    lo = (x - hi.astype(F32)).astype(BF16)
    return _dot(hi, mat) + _dot(lo, mat)


def _split_dot_many(xs, mat):
    rows = xs[0].shape[0]
    his = [x.astype(BF16) for x in xs]
    los = [(x - hi.astype(F32)).astype(BF16) for x, hi in zip(xs, his)]
    out = _dot(jnp.concatenate(his + los, axis=0), mat)
    n = len(xs)
    return [out[i * rows:(i + 1) * rows] + out[(n + i) * rows:(n + i + 1) * rows] for i in range(n)]


def _rms(x):
    return lax.rsqrt(jnp.mean(x * x, axis=-1, keepdims=True) + EPS)


def _hosted_call(body, name, n_steps, ins, in_specs, out_specs, out_shape, scratch, rider=None):
    ins, scratch = list(ins), list(scratch)
    if rider is None:
        wrapped, extra_in, extra_out, extra_scratch = body, [], [], []
    else:
        extra_in, extra_out, extra_scratch = rider.arrays, rider.out_shape(), rider.scratch()
        n_in, n_out, n_scr = len(ins), len(out_shape), len(scratch)
        k_in, k_out = len(extra_in), len(extra_out)

        def wrapped(*refs):
            own_in, r_in = refs[:n_in], refs[n_in:n_in + k_in]
            own_out = refs[n_in + k_in:n_in + k_in + n_out]
            r_out = refs[n_in + k_in + n_out:n_in + k_in + n_out + k_out]
            rest = refs[n_in + k_in + n_out + k_out:]
            phases = rider.bind(r_in, r_out, rest[n_scr:])
            step = pl.program_id(0)
            pl.when(step == 0)(phases[0])
            body(*own_in, *own_out, *rest[:n_scr])
            for at, phase in zip((min(2, n_steps - 1), n_steps - 1), phases[1:]):
                if phase is not None:
                    pl.when(step == at)(phase)

    return pl.pallas_call(
        wrapped, name=name, grid=(n_steps,),
        in_specs=list(in_specs) + [HBM_SPEC] * len(extra_in),
        out_specs=list(out_specs) + [HBM_SPEC] * len(extra_out),
        out_shape=list(out_shape) + list(extra_out), scratch_shapes=scratch + list(extra_scratch),
        compiler_params=pltpu.CompilerParams(dimension_semantics=("arbitrary",), vmem_limit_bytes=VMEM_LIMIT),
    )(*ins, *extra_in)


def _rows_call(body, name, n_rows, row_ins, const_ins, row_outs, const_outs=(), scratch=(),
               reverse=False, tm=TM, rider=None):
    nb = n_rows // tm

    def row_map(i):
        return ((nb - 1 - i) if reverse else i, 0)

    def const_map(nd):
        return lambda i: (0,) * nd

    in_specs = [pl.BlockSpec((tm, a.shape[1]), row_map) for a in row_ins]
    in_specs += [pl.BlockSpec(a.shape, const_map(a.ndim)) for a in const_ins]
    out_specs = [pl.BlockSpec((tm, w), row_map) for (w, _) in row_outs]
    out_specs += [pl.BlockSpec(s, const_map(len(s))) for (s, _) in const_outs]
    out_shape = [jax.ShapeDtypeStruct((n_rows, w), dt) for (w, dt) in row_outs]
    out_shape += [jax.ShapeDtypeStruct(s, dt) for (s, dt) in const_outs]
    return _hosted_call(body, name, nb, list(row_ins) + list(const_ins), in_specs, out_specs, out_shape,
                        scratch, rider)


def _ple_fwd(p_ref, xin, wple_ref, wgate_ref, e_ref, gt_ref):
    pb = p_ref[...].astype(BF16)
    for j in range(N_DEV):
        e_ref[:, j * 128:(j + 1) * 128] = _dot(pb, wple_ref[j])
    gt = _sigmoid(_dot(xin.astype(BF16), wgate_ref[...]))
    gt_ref[...] = gt
    return xin + e_ref[...] * gt


def _layer_a_fwd(x0, p0, a_norm, a_scale, w_in, w_group, w_out, w_ple, w_gate, rider=None):
    s = x0.shape[0]
    tm = TM

    def body(x_ref, p_ref, an_ref, as_ref, win_ref, wg_ref, wout_ref, wple_ref, wgate_ref,
             h_ref, z_ref, pooled_ref, m_ref, y_ref, x1_ref, e_ref, gt_ref, x2_ref, uext):
        i = pl.program_id(0)

        @pl.when(i == 0)
        def _():
            uext[0:HALO, :] = jnp.zeros((HALO, D), F32)

        x = x_ref[...]
        h = (x * _rms(x) * an_ref[...]).astype(BF16)
        h_ref[...] = h
        for j in range(N_DEV):
            uz = _dot(h, win_ref[j])
            if j < 4:
                uext[HALO:HALO + tm, j * 256:(j + 1) * 256] = uz
            else:
                z_ref[:, (j - 4) * 256:(j - 3) * 256] = uz
        t = i * tm + lax.broadcasted_iota(jnp.int32, (tm, 1), 0)
        for g in range(N_GROUPS):
            w = 2 ** (g + 1)
            cols = slice(g * GROUP_DIM, (g + 1) * GROUP_DIM)
            ext = uext[:, cols]
            acc = ext
            k = 1
            while k < w:
                acc = acc + pltpu.roll(acc, k, 0)
                k *= 2
            cnt = jnp.minimum(t + 1, w).astype(F32)
            pooled = (acc[HALO:] / cnt - ext[HALO:]).astype(BF16)
            pooled_ref[:, cols] = pooled
            m_ref[:, cols] = _dot(pooled, wg_ref[g])
        uext[0:HALO, :] = uext[tm:tm + HALO, :]
        z = z_ref[...]
        y = (m_ref[...] * as_ref[...] * (z * _sigmoid(z))).astype(BF16)
        y_ref[...] = y
        x1 = x + _dot(y, wout_ref[...])
        x1_ref[...] = x1
        x2_ref[...] = _ple_fwd(p_ref, x1, wple_ref, wgate_ref, e_ref, gt_ref)

    row_outs = [(D, BF16), (D, F32), (D, BF16), (D, F32), (D, BF16), (D, F32), (D, F32), (D, F32), (D, F32)]
    return _rows_call(body, "layer_a_fwd", s, [x0, p0], [a_norm, a_scale, w_in, w_group, w_out, w_ple, w_gate],
                      row_outs, scratch=[pltpu.VMEM((tm + HALO, D), F32)], rider=rider)


def _layer_b_in_fwd(x2, kv_norm, b_norm, w_kv, w_bin):
    s = x2.shape[0]

    def body(x_ref, kvn_ref, bn_ref, wkv_ref, wbin_ref, hkv_ref, hb_ref, k_ref, v_ref, q_ref, zb_ref):
        x = x_ref[...]
        n = x * _rms(x)
        hkv = (n * kvn_ref[...]).astype(BF16)
        hb = (n * bn_ref[...]).astype(BF16)
        hkv_ref[...] = hkv
        hb_ref[...] = hb
        for j in range(N_DEV):
            kv = _dot(hkv, wkv_ref[j])
            qz = _dot(hb, wbin_ref[j])
            if j < 4:
                cols = slice(j * 256, (j + 1) * 256)
                k_ref[:, cols] = kv
                q_ref[:, cols] = qz
            else:
                cols = slice((j - 4) * 256, (j - 3) * 256)
                v_ref[:, cols] = kv.astype(BF16)
                zb_ref[:, cols] = qz

    row_outs = [(D, BF16), (D, BF16), (D, F32), (D, BF16), (D, F32), (D, F32)]
    return _rows_call(body, "layer_b_in_fwd", s, [x2], [kv_norm, b_norm, w_kv, w_bin], row_outs)


def _tri(after):
    r = lax.broadcasted_iota(jnp.int32, (TILE, TILE), 0)
    c = lax.broadcasted_iota(jnp.int32, (TILE, TILE), 1)
    return jnp.where((r > c) if after else (r < c), 1.0, 0.0).astype(BF16)


def _half_sums(v):
    r = lax.broadcasted_iota(jnp.int32, (LANES, LANES), 0) < HEAD_DIM
    c = lax.broadcasted_iota(jnp.int32, (LANES, LANES), 1) < HEAD_DIM
    same_head = jnp.where(r == c, 1.0, 0.0).astype(BF16)
    return _split_dot(v, same_head)


def _pair_norm(x):
    r = lax.rsqrt(_half_sums(x * x) * (1.0 / HEAD_DIM) + EPS)
    return x * r, r


def _tile_logits(qblk, kblk, diagonal):
    l = _dot_nt(qblk, kblk)
    sp = jnp.maximum(l, 0.0) + jnp.log(1.0 + jnp.exp(-jnp.abs(l)))
    ls = l - sp
    if not diagonal:
        return None, -sp, ls
    mask = lax.broadcasted_iota(jnp.int32, l.shape, 1) < lax.broadcasted_iota(jnp.int32, l.shape, 0)
    return mask, jnp.where(mask, -sp, 0.0), ls


def _attn_fwd(q_all, k_all, v_all, q_gain2, k_gain2, rider=None):
    s = q_all.shape[0]
    nt = s // TILE

    def body(q_ref, k_ref, v_ref, qg_ref, kg_ref, o_ref, c_ref, qs, ks, vs, tri, acc, right, cmat):
        tri[...] = _tri(True)
        lane = lax.broadcasted_iota(jnp.int32, (TILE, LANES), 1)
        qn, _ = _pair_norm(q_ref[...])
        kn, _ = _pair_norm(k_ref[...])
        qsc = (qn * qg_ref[...] * SB_SCALE).astype(BF16)
        ksc = (kn * kg_ref[...]).astype(BF16)
        for hh in range(2):
            sl = slice(hh * HEAD_DIM, (hh + 1) * HEAD_DIM)
            qs[hh] = qsc[:, sl]
            ks[hh] = ksc[:, sl]
            vs[hh] = v_ref[:, sl]

        def tile(qrows, kb, diagonal):
            rows = pl.ds(pl.multiple_of(kb * TILE, TILE), TILE)
            loaded = [(qs[hh, qrows, :], ks[hh, rows, :], vs[hh, rows, :], right[hh], cmat[hh], acc[hh])
                      for hh in range(2)]
            logits = [_tile_logits(q, k, diagonal) for q, k, _, _, _, _ in loaded]
            later = _split_dot_many([lk for _, lk, _ in logits], tri[...])
            results = []
            for (q, k, v, rt, cm, ac), (mask, lk, ls), lt in zip(loaded, logits, later):
                a = jnp.exp(ls + lt + rt)
                if diagonal:
                    a = jnp.where(mask, a, 0.0)
                results.append((ac + _dot(a.astype(BF16), v), jnp.where(lane == kb, rt[:, :LANES], cm),
                                rt + jnp.sum(lk, axis=1, keepdims=True)))
            for hh, (ac, cm, rt) in enumerate(results):
                acc[hh] = ac
                cmat[hh] = cm
                right[hh] = rt

        def q_step(qb, _):
            r0 = pl.multiple_of(qb * TILE, TILE)
            qrows = pl.ds(r0, TILE)
            acc[...] = jnp.zeros((2, TILE, HEAD_DIM), F32)
            right[...] = jnp.zeros((2, TILE, TILE), F32)
            cmat[...] = jnp.zeros((2, TILE, LANES), F32)
            tile(qrows, qb, True)

            def live():
                return (jnp.max(right[:, :, :LANES]) > DEAD_LOG).astype(jnp.int32)

            def k_step(c):
                kb = c[0] - 1
                tile(qrows, kb, False)
                return kb, live()

            first, _ = lax.while_loop(lambda c: (c[0] > 0) & (c[1] > 0), k_step, (qb, live()))
            for hh in range(2):
                o_ref[qrows, hh * HEAD_DIM:(hh + 1) * HEAD_DIM] = acc[hh]
                c_ref[hh, qrows, :] = jnp.where(lane == LANES - 1, first.astype(F32), cmat[hh])
            return 0

        lax.fori_loop(0, nt, q_step, 0)

    pair = pl.BlockSpec((s, LANES), lambda h: (0, h))
    gain = pl.BlockSpec((1, LANES), lambda h: (0, 0))
    return _hosted_call(
        body, "attn_fwd", HEADS // 2, [q_all, k_all, v_all, q_gain2, k_gain2],
        [pair, pair, pair, gain, gain], [pair, pl.BlockSpec((2, s, LANES), lambda h: (h, 0, 0))],
        [jax.ShapeDtypeStruct((s, D), F32), jax.ShapeDtypeStruct((HEADS, s, LANES), F32)],
        [pltpu.VMEM((2, s, HEAD_DIM), BF16)] * 3
        + [pltpu.VMEM((TILE, TILE), BF16), pltpu.VMEM((2, TILE, HEAD_DIM), F32), pltpu.VMEM((2, TILE, TILE), F32),
           pltpu.VMEM((2, TILE, LANES), F32)], rider)


def _layer_b_out_fwd(o, zb, x2, p1, target, w_out, w_ple, w_gate):
    s = o.shape[0]

    def body(o_ref, zb_ref, x2_ref, p_ref, t_ref, wout_ref, wple_ref, wgate_ref,
             yb_ref, x3_ref, e_ref, gt_ref, dx4_ref, loss_ref):
        zb = zb_ref[...]
        yb = (o_ref[...] * (zb * _sigmoid(zb))).astype(BF16)
        yb_ref[...] = yb
        x3 = x2_ref[...] + _dot(yb, wout_ref[...])
        x3_ref[...] = x3
        x4 = _ple_fwd(p_ref, x3, wple_ref, wgate_ref, e_ref, gt_ref)
        d = x4 - t_ref[...]
        dx4_ref[...] = d * (1.0 / D)

        @pl.when(pl.program_id(0) == 0)
        def _():
            loss_ref[...] = jnp.zeros((1, D), F32)

        loss_ref[...] += jnp.sum(d * d, axis=0, keepdims=True)

    row_outs = [(D, BF16), (D, F32), (D, F32), (D, F32), (D, F32)]
    return _rows_call(body, "layer_b_out_fwd", s, [o, zb, x2, p1, target], [w_out, w_ple, w_gate], row_outs,
                      const_outs=[((1, D), F32)])


def _ple_bwd(dxo, e_ref, gt_ref, wgate_ref, de_ref, dgp_ref):
    e = e_ref[...]
    gt = gt_ref[...]
    de_ref[...] = (dxo * gt).astype(BF16)
    dgp = (dxo * e * gt * (1.0 - gt)).astype(BF16)
    dgp_ref[...] = dgp
    return dxo + _dot_nt(dgp, wgate_ref[...])


def _silu_grads(z):
    sg = _sigmoid(z)
    return z * sg, sg * (1.0 + z * (1.0 - sg))


def _layer_b_out_bwd(dx4, e1, gt1, o, zb, w_gate, w_out):
    s = dx4.shape[0]

    def body(dx4_ref, e_ref, gt_ref, o_ref, zb_ref, wgate_ref, wout_ref,
             de_ref, dgp_ref, dx3_ref, do_ref, dzb_ref):
        dx3 = _ple_bwd(dx4_ref[...], e_ref, gt_ref, wgate_ref, de_ref, dgp_ref)
        dx3_ref[...] = dx3
        dyb = _dot_nt(dx3.astype(BF16), wout_ref[...])
        silu, dsilu = _silu_grads(zb_ref[...])
        do_ref[...] = (dyb * silu).astype(BF16)
        dzb_ref[...] = (dyb * o_ref[...] * dsilu).astype(BF16)

    row_outs = [(D, BF16), (D, BF16), (D, F32), (D, BF16), (D, BF16)]
    return _rows_call(body, "layer_b_out_bwd", s, [dx4, e1, gt1, o, zb], [w_gate, w_out], row_outs)


def _attn_bwd(q_all, k_all, v_all, q_gain2, k_gain2, d_o, csave, rider=None):
    s = q_all.shape[0]
    nt = s // TILE

    def body(q_ref, k_ref, v_ref, qg_ref, kg_ref, do_ref, c_ref,
             dq_ref, dk_ref, dv_ref, dqg_ref, dkg_ref,
             qs, ks, vs, dos, qt, dot_t, tri_a, tri_b, dqa, dkt, dvt, dqb, left):
        tri_a[...] = _tri(True)
        tri_b[...] = _tri(False)
        lane = lax.broadcasted_iota(jnp.int32, (TILE, LANES), 1)
        qn, qr = _pair_norm(q_ref[...])
        kn, kr = _pair_norm(k_ref[...])
        qsc = qn * qg_ref[...] * SB_SCALE
        ksc = (kn * kg_ref[...]).astype(BF16)
        q_t = qsc.T.astype(BF16)
        do_t = do_ref[...].astype(F32).T.astype(BF16)
        for j in range(nt):
            qt[j] = q_t[:, j * TILE:(j + 1) * TILE]
            dot_t[j] = do_t[:, j * TILE:(j + 1) * TILE]
        dkt[...] = jnp.zeros((nt, LANES, TILE), F32)
        dvt[...] = jnp.zeros((nt, LANES, TILE), F32)
        qsc = qsc.astype(BF16)
        for hh in range(2):
            sl = slice(hh * HEAD_DIM, (hh + 1) * HEAD_DIM)
            qs[hh] = qsc[:, sl]
            ks[hh] = ksc[:, sl]
            vs[hh] = v_ref[:, sl]
            dos[hh] = do_ref[:, sl]

        def tile(qb, qrows, kb, diagonal):
            rows = pl.ds(pl.multiple_of(kb * TILE, TILE), TILE)
            heads = range(2)
            kblk = [ks[hh, rows, :] for hh in heads]
            logits = [_tile_logits(qs[hh, qrows, :], kblk[hh], diagonal) for hh in heads]
            later = _split_dot_many([lk for _, lk, _ in logits], tri_a[...])
            a, g = [], []
            for hh in heads:
                mask, _, ls = logits[hh]
                right = jnp.sum(jnp.where(lane == kb, c_ref[hh, qrows, :], 0.0), axis=1, keepdims=True)
                a_h = jnp.exp(ls + later[hh] + right)
                a.append(jnp.where(mask, a_h, 0.0) if diagonal else a_h)
                g.append(a[hh] * _dot_nt(dos[hh, qrows, :], vs[hh, rows, :]))
            before = _split_dot_many(g, tri_b[...])
            for hh in heads:
                sl = slice(hh * HEAD_DIM, (hh + 1) * HEAD_DIM)
                mask, _, ls = logits[hh]
                beta = jnp.exp(ls)
                lf = left[hh]
                dl = g[hh] * (1.0 - beta) - (before[hh] + lf) * beta
                if diagonal:
                    dl = jnp.where(mask, dl, 0.0)
                dl = dl.astype(BF16)
                left[hh] = lf + jnp.sum(g[hh], axis=1, keepdims=True)
                dqb[hh] += _dot(dl, kblk[hh])
                dkt[kb, sl, :] += _dot(qt[qb, sl, :], dl)
                dvt[kb, sl, :] += _dot(dot_t[qb, sl, :], a[hh].astype(BF16))

        def q_step(qb, _):
            qrows = pl.ds(pl.multiple_of(qb * TILE, TILE), TILE)
            dqb[...] = jnp.zeros((2, TILE, HEAD_DIM), F32)
            left[...] = jnp.zeros((2, TILE, TILE), F32)

            def k_step(kb, _):
                tile(qb, qrows, kb, False)
                return 0

            first = jnp.max(jnp.where(lane == LANES - 1, c_ref[0, qrows, :], 0.0)).astype(jnp.int32)
            lax.fori_loop(first, qb, k_step, 0)
            tile(qb, qrows, qb, True)
            for hh in range(2):
                dqa[qrows, hh * HEAD_DIM:(hh + 1) * HEAD_DIM] = dqb[hh] * SB_SCALE
            return 0

        lax.fori_loop(0, nt, q_step, 0)

        def norm_bwd(dy, xn, r, g_ref, dx_ref, dg_ref):
            dg_ref[...] = jnp.sum(dy * xn, axis=0, keepdims=True)
            dxn = dy * g_ref[...]
            dx_ref[...] = r * (dxn - xn * (_half_sums(dxn * xn) * (1.0 / HEAD_DIM)))

        norm_bwd(dqa[...], qn, qr, qg_ref, dq_ref, dqg_ref)
        for j in range(nt):
            dqa[j * TILE:(j + 1) * TILE, :] = dkt[j].T
            dv_ref[j * TILE:(j + 1) * TILE, :] = dvt[j].T
        norm_bwd(dqa[...], kn, kr, kg_ref, dk_ref, dkg_ref)

    pair = pl.BlockSpec((s, LANES), lambda h: (0, h))
    gain = pl.BlockSpec((1, LANES), lambda h: (0, 0))
    dgain = pl.BlockSpec((None, 1, LANES), lambda h: (h, 0, 0))
    return _hosted_call(
        body, "attn_bwd", HEADS // 2, [q_all, k_all, v_all, q_gain2, k_gain2, d_o, csave],
        [pair, pair, pair, gain, gain, pair, pl.BlockSpec((2, s, LANES), lambda h: (h, 0, 0))],
        [pair, pair, pair, dgain, dgain],
        [jax.ShapeDtypeStruct((s, D), F32)] * 3 + [jax.ShapeDtypeStruct((HEADS // 2, 1, LANES), F32)] * 2,
        [pltpu.VMEM((2, s, HEAD_DIM), BF16)] * 4
        + [pltpu.VMEM((nt, LANES, TILE), BF16)] * 2 + [pltpu.VMEM((TILE, TILE), BF16)] * 2
        + [pltpu.VMEM((s, LANES), F32)] + [pltpu.VMEM((nt, LANES, TILE), F32)] * 2
        + [pltpu.VMEM((2, TILE, HEAD_DIM), F32), pltpu.VMEM((2, TILE, TILE), F32)], rider)


def _norm_bwd_rows(dh, x, gain, dgain_ref):
    r = _rms(x)
    n = x * r
    dgain_ref[...] += jnp.sum(dh * n, axis=0, keepdims=True)
    dn = dh * gain
    return r * (dn - n * jnp.mean(dn * n, axis=-1, keepdims=True))


def _layer_b_in_bwd(dq, dzb, dk, dv, x2, dx3, w_bin, w_kv, b_norm, kv_norm):
    s = x2.shape[0]

    def body(dq_ref, dzb_ref, dk_ref, dv_ref, x_ref, dx3_ref, wbin_ref, wkv_ref, bn_ref, kvn_ref,
             dqz_ref, dkv_ref, dx2_ref, dbn_ref, dkvn_ref):
        @pl.when(pl.program_id(0) == 0)
        def _():
            dbn_ref[...] = jnp.zeros((1, D), F32)
            dkvn_ref[...] = jnp.zeros((1, D), F32)

        dqz_ref[:, :D] = dq_ref[...].astype(BF16)
        dqz_ref[:, D:] = dzb_ref[...]
        dkv_ref[:, :D] = dk_ref[...].astype(BF16)
        dkv_ref[:, D:] = dv_ref[...].astype(BF16)
        dhb = jnp.zeros((TM, D), F32)
        dhkv = jnp.zeros((TM, D), F32)
        for j in range(N_DEV):
            cols = slice(j * 256, (j + 1) * 256)
            dhb = dhb + _dot_nt(dqz_ref[:, cols], wbin_ref[j])
            dhkv = dhkv + _dot_nt(dkv_ref[:, cols], wkv_ref[j])
        x = x_ref[...]
        dx2 = dx3_ref[...] + _norm_bwd_rows(dhb, x, bn_ref[...], dbn_ref)
        dx2_ref[...] = dx2 + _norm_bwd_rows(dhkv, x, kvn_ref[...], dkvn_ref)

    row_outs = [(2 * D, BF16), (2 * D, BF16), (D, F32)]
    return _rows_call(body, "layer_b_in_bwd", s, [dq, dzb, dk, dv, x2, dx3], [w_bin, w_kv, b_norm, kv_norm],
                      row_outs, const_outs=[((1, D), F32), ((1, D), F32)])


def _layer_a_out_bwd(dx2, e0, gt0, z, m, w_gate, w_out, a_scale, w_group, rider=None):
    s = dx2.shape[0]
    tm = TM
    nb = s // tm

    def body(dx2_ref, e_ref, gt_ref, z_ref, m_ref, wgate_ref, wout_ref, as_ref, wg_ref,
             de_ref, dgp_ref, dx1_ref, dm_ref, duz_ref, das_ref, ext):
        i = pl.program_id(0)

        @pl.when(i == 0)
        def _():
            das_ref[...] = jnp.zeros((1, D), F32)
            ext[tm:tm + HALO, :] = jnp.zeros((HALO, D), F32)

        dx1 = _ple_bwd(dx2_ref[...], e_ref, gt_ref, wgate_ref, de_ref, dgp_ref)
        dx1_ref[...] = dx1
        dy = _dot_nt(dx1.astype(BF16), wout_ref[...])
        silu, dsilu = _silu_grads(z_ref[...])
        m = m_ref[...]
        dmixed = dy * silu
        duz_ref[:, D:] = (dy * (m * as_ref[...]) * dsilu).astype(BF16)
        das_ref[...] += jnp.sum(dmixed * m, axis=0, keepdims=True)
        dm_ref[...] = (dmixed * as_ref[...]).astype(BF16)
        t = (nb - 1 - i) * tm + lax.broadcasted_iota(jnp.int32, (tm, 1), 0)
        n_ext = tm + HALO
        for g in range(N_GROUPS):
            w = 2 ** (g + 1)
            cols = slice(g * GROUP_DIM, (g + 1) * GROUP_DIM)
            dpool = _dot_nt(dm_ref[:, cols], wg_ref[g])
            ext[0:tm, cols] = dpool / jnp.minimum(t + 1, w).astype(F32)
            acc = ext[:, cols]
            k = 1
            while k < w:
                acc = acc + pltpu.roll(acc, n_ext - k, 0)
                k *= 2
            duz_ref[:, cols] = (acc[:tm] - dpool).astype(BF16)
        ext[tm:tm + HALO, :] = ext[0:HALO, :]

    row_outs = [(D, BF16), (D, BF16), (D, F32), (D, BF16), (2 * D, BF16)]
    return _rows_call(body, "layer_a_out_bwd", s, [dx2, e0, gt0, z, m], [w_gate, w_out, a_scale, w_group],
                      row_outs, const_outs=[((1, D), F32)], scratch=[pltpu.VMEM((tm + HALO, D), F32)],
                      reverse=True, rider=rider)


def _layer_a_in_bwd(duz, x0, dx1, w_in, a_norm, rider=None):
    s = x0.shape[0]

    def body(duz_ref, x_ref, dx1_ref, win_ref, an_ref, dx0_ref, dan_ref):
        @pl.when(pl.program_id(0) == 0)
        def _():
            dan_ref[...] = jnp.zeros((1, D), F32)

        dh = jnp.zeros((TM, D), F32)
        for j in range(N_DEV):
            dh = dh + _dot_nt(duz_ref[:, j * 256:(j + 1) * 256], win_ref[j])
        dx0_ref[...] = dx1_ref[...] + _norm_bwd_rows(dh, x_ref[...], an_ref[...], dan_ref)

    return _rows_call(body, "layer_a_in_bwd", s, [duz, x0, dx1], [w_in, a_norm], [(D, F32)],
                      const_outs=[((1, D), F32)], rider=rider)


def _wgrad(a, b, name, n_split=1, a_blocked_b=False):
    s, k = a.shape
    n = b.shape[1]
    tk = 256
    nb = n // n_split

    def body(a_ref, b_ref, o_ref):
        res = _dot_tn(a_ref[...].astype(BF16), b_ref[...].astype(BF16))
        if n_split == 1:
            o_ref[...] = res.astype(BF16)
        else:
            for j in range(n_split):
                o_ref[j] = res[:, j * nb:(j + 1) * nb].astype(BF16)

    if a_blocked_b:
        b_spec = pl.BlockSpec((s, tk), lambda i: (0, i))
        out_spec = pl.BlockSpec((None, tk, tk), lambda i: (i, 0, 0))
        out_shape = jax.ShapeDtypeStruct((k // tk, tk, tk), BF16)
    elif n_split == 1:
        b_spec = pl.BlockSpec((s, n), lambda i: (0, 0))
        out_spec = pl.BlockSpec((tk, n), lambda i: (i, 0))
        out_shape = jax.ShapeDtypeStruct((k, n), BF16)
    else:
        b_spec = pl.BlockSpec((s, n), lambda i: (0, 0))
        out_spec = pl.BlockSpec((n_split, tk, nb), lambda i: (0, i, 0))
        out_shape = jax.ShapeDtypeStruct((n_split, k, nb), BF16)
    return pl.pallas_call(
        body, name=name, grid=(k // tk,),
        in_specs=[pl.BlockSpec((s, tk), lambda i: (0, i)), b_spec],
        out_specs=out_spec, out_shape=out_shape,
        compiler_params=pltpu.CompilerParams(dimension_semantics=("arbitrary",), vmem_limit_bytes=VMEM_LIMIT),
    )(a, b)


def _cast_shards(shards):
    n = len(shards)

    def body(*refs):
        for a in range(n):
            refs[n + a][...] = refs[a][...].astype(BF16)

    vmem = pl.BlockSpec(memory_space=pltpu.VMEM)
    return pl.pallas_call(
        body, name="cast_shards", in_specs=[vmem] * n, out_specs=[vmem] * n,
        out_shape=[jax.ShapeDtypeStruct(a.shape, BF16) for a in shards],
        compiler_params=pltpu.CompilerParams(vmem_limit_bytes=VMEM_LIMIT),
    )(*shards)


def _adamw(w, g, m, v):
    m = ADAM_B1 * m + (1.0 - ADAM_B1) * g
    v = ADAM_B2 * v + (1.0 - ADAM_B2) * jnp.square(g)
    m_hat = m / (1.0 - ADAM_B1 ** ADAM_STEP)
    v_hat = v / (1.0 - ADAM_B2 ** ADAM_STEP)
    delta = -ADAM_LR * (m_hat / (jnp.sqrt(v_hat) + ADAM_EPS) + ADAM_WD * w)
    return delta, m, v


def _adamw_small(w, g, m, v, name):
    def body(w_ref, g_ref, m_ref, v_ref, d_ref, m2_ref, v2_ref):
        d_ref[...], m2_ref[...], v2_ref[...] = _adamw(w_ref[...], g_ref[...], m_ref[...], v_ref[...])

    vmem = pl.BlockSpec(memory_space=pltpu.VMEM)
    return pl.pallas_call(
        body, name=name, in_specs=[vmem] * 4, out_specs=[vmem] * 3,
        out_shape=[jax.ShapeDtypeStruct(w.shape, F32)] * 3,
    )(w, g, m, v)


def _place():
    return lax.axis_index("x"), lax.axis_index("y"), lax.axis_index("c")


def _all_gather(shards):
    return _alone("all_gather_weights", _GatherRider(shards))


def _alone(name, rider):
    n_in, n_out = len(rider.arrays), len(rider.out_shape())

    def body(*refs):
        for phase in rider.bind(refs[:n_in], refs[n_in:n_in + n_out], refs[n_in + n_out:]):
            if phase is not None:
                phase()

    return pl.pallas_call(
        body, name=name, in_specs=[HBM_SPEC] * n_in, out_specs=[HBM_SPEC] * n_out,
        out_shape=rider.out_shape(), scratch_shapes=rider.scratch(),
        compiler_params=pltpu.CompilerParams(vmem_limit_bytes=VMEM_LIMIT),
    )(*rider.arrays)


class _GatherRider:
    def __init__(self, shards):
        self.arrays = list(shards)

    def out_shape(self):
        return _Gather.out_shape(self.arrays)

    def scratch(self):
        return _Gather.semaphores(len(self.arrays))

    def bind(self, ins, outs, scratch):
        moving = _Gather(ins, outs, *scratch)
        return moving.start, None, moving.finish


class _ReduceRider:
    def __init__(self, partials):
        self.arrays = list(partials)

    def out_shape(self):
        return [jax.ShapeDtypeStruct(a.shape[1:], F32) for a in self.arrays]

    def scratch(self):
        n = len(self.arrays)
        dma = pltpu.SemaphoreType.DMA

        def blocks(k):
            return [pltpu.VMEM((k,) + a.shape[1:], BF16) for a in self.arrays]

        return (blocks(4) + blocks(4) + blocks(4) + blocks(3) + [pltpu.VMEM(a.shape[1:], F32) for a in self.arrays]
                + [dma((4 * n,)), dma((4 * n,)), dma((4 * n,)), dma((3 * n,)), dma((3 * n,)), dma((n,))])

    def bind(self, ins, outs, scratch):
        n = len(ins)
        mine, landed, chip16, arrived, total = (scratch[i * n:(i + 1) * n] for i in range(5))
        send1, recv1, local1, send2, recv2, out_sems = scratch[5 * n:]
        x, y, c = _place()
        plane = 2 * x + y
        peers = [(1 - x, y), (x, 1 - y), (1 - x, 1 - y)]

        def to_sibling(a, k):
            return pltpu.make_async_remote_copy(
                src_ref=ins[a].at[2 * k + (1 - c)], dst_ref=landed[a].at[k],
                send_sem=send1.at[4 * a + k], recv_sem=recv1.at[4 * a + k],
                device_id=(x, y, 1 - c), device_id_type=MESH)

        def own_block(a, k):
            return pltpu.make_async_copy(ins[a].at[2 * k + c], mine[a].at[k], local1.at[4 * a + k])

        def to_owner(a, j):
            px, py = peers[j]
            return pltpu.make_async_remote_copy(
                src_ref=chip16[a].at[2 * px + py], dst_ref=arrived[a].at[j],
                send_sem=send2.at[3 * a + j], recv_sem=recv2.at[3 * a + j],
                device_id=(px, py, c), device_id_type=MESH)

        def result(a):
            return pltpu.make_async_copy(total[a], outs[a], out_sems.at[a])

        def first():
            for a in range(n):
                for k in range(4):
                    to_sibling(a, k).start()
                    own_block(a, k).start()

        def middle():
            for a in range(n):
                for k in range(4):
                    own_block(a, k).wait()
                    to_sibling(a, k).wait_recv()
                    chip16[a][k] = (mine[a][k].astype(F32) + landed[a][k].astype(F32)).astype(BF16)
                total[a][...] = mine[a][plane].astype(F32) + landed[a][plane].astype(F32)
                for j in range(3):
                    to_owner(a, j).start()

        def last():
            for a in range(n):
                for j in range(3):
                    to_owner(a, j).wait_recv()
                    total[a][...] += arrived[a][j].astype(F32)
                result(a).start()
            for a in range(n):
                for k in range(4):
                    to_sibling(a, k).wait_send()
                for j in range(3):
                    to_owner(a, j).wait_send()
                result(a).wait()

        return first, middle, last


class _Gather:
    def __init__(self, ins, outs, send_sems, recv_sems, local_sems):
        self.ins, self.outs = ins, outs
        self.send_sems, self.recv_sems, self.local_sems = send_sems, recv_sems, local_sems
        self.x, self.y, self.c = _place()

    @staticmethod
    def out_shape(shards):
        return [jax.ShapeDtypeStruct((N_DEV,) + a.shape, a.dtype) for a in shards]

    @staticmethod
    def semaphores(n):
        return [pltpu.SemaphoreType.DMA((7 * n,)), pltpu.SemaphoreType.DMA((7 * n,)), pltpu.SemaphoreType.DMA((n,))]

    def _chips(self):
        x, y = self.x, self.y
        return [(1 - x, y), (x, 1 - y), (1 - x, 1 - y)]

    def _copy(self, a, k, block, to, own=False):
        px, py, pc = block
        slot = self.outs[a].at[4 * px + 2 * py + pc]
        return pltpu.make_async_remote_copy(
            src_ref=self.ins[a] if own else slot, dst_ref=slot,
            send_sem=self.send_sems.at[7 * a + k], recv_sem=self.recv_sems.at[7 * a + k],
            device_id=to, device_id_type=MESH)

    def _local(self, a):
        return pltpu.make_async_copy(self.ins[a], self.outs[a].at[4 * self.x + 2 * self.y + self.c],
                                     self.local_sems.at[a])

    def _first(self, a):
        me, sibling = (self.x, self.y, self.c), (self.x, self.y, 1 - self.c)
        return [self._copy(a, 0, me, sibling, own=True)] + [
            self._copy(a, 1 + j, me, (*chip, self.c), own=True) for j, chip in enumerate(self._chips())]

    def start(self):
        for a in range(len(self.ins)):
            self._local(a).start()
            for cp in self._first(a):
                cp.start()

    def finish(self):
        c = self.c
        me, sibling = (self.x, self.y, c), (self.x, self.y, 1 - c)
        n = len(self.ins)
        passed = []
        for a in range(n):
            for j, chip in enumerate(self._chips()):
                self._copy(a, 1 + j, (*chip, c), me).wait_recv()
                passed.append(self._copy(a, 4 + j, (*chip, c), sibling))
                passed[-1].start()
        for a in range(n):
            self._copy(a, 0, sibling, me).wait_recv()
            for j, chip in enumerate(self._chips()):
                self._copy(a, 4 + j, (*chip, 1 - c), me).wait_recv()
        for a in range(n):
            for cp in self._first(a):
                cp.wait_send()
            self._local(a).wait()
        for cp in passed:
            cp.wait_send()


def _adamw_shards(ws, gs, ms, vs):
    n = len(ws)

    def body(*refs):
        w, g, m, v = refs[:n], refs[n:2 * n], refs[2 * n:3 * n], refs[3 * n:4 * n]
        outs = refs[4 * n:]
        for a in range(n):
            outs[a][...], outs[n + a][...], outs[2 * n + a][...] = _adamw(w[a][...], g[a][...], m[a][...], v[a][...])

    vmem = pl.BlockSpec(memory_space=pltpu.VMEM)
    res = pl.pallas_call(
        body, name="adamw_shards", in_specs=[vmem] * (4 * n), out_specs=[vmem] * (3 * n),
        out_shape=[jax.ShapeDtypeStruct(a.shape, F32) for a in ws] * 3,
        compiler_params=pltpu.CompilerParams(vmem_limit_bytes=VMEM_LIMIT),
    )(*ws, *gs, *ms, *vs)
    return res[:n], res[n:2 * n], res[2 * n:]


def _all_reduce_small(rows, gain_parts):
    def body(rows_ref, dqg_ref, dkg_ref, out_ref, buf, send_sems, recv_sems):
        x, y, c = _place()
        me = 4 * x + 2 * y + c
        buf[0] = rows_ref[...]
        for row, part in ((4, dqg_ref), (5, dkg_ref)):
            both = jnp.sum(part[...].reshape(HEADS // 2, LANES), axis=0, keepdims=True)
            buf[0, row:row + 1, 0:HEAD_DIM] = both[:, :HEAD_DIM] + both[:, HEAD_DIM:]
        copies = []
        for r in range(1, N_DEV):
            bx, by, bc = (r >> 2) & 1, (r >> 1) & 1, r & 1
            to = (x ^ bx, y ^ by, c ^ bc)
            copies.append(pltpu.make_async_remote_copy(
                src_ref=buf.at[0], dst_ref=buf.at[r], send_sem=send_sems.at[r - 1], recv_sem=recv_sems.at[r - 1],
                device_id=to, device_id_type=MESH))
        for cp in copies:
            cp.start()
        for cp in copies:
            cp.wait_recv()
        for cp in copies:
            cp.wait_send()
        tot = buf[me]
        for j in range(1, N_DEV):
            tot = tot + buf[j ^ me]
        out_ref[...] = tot
        loss = (0.5 / D) * jnp.sum(tot[6:7, :], axis=1, keepdims=True)
        out_ref[6:7, :] = jnp.broadcast_to(loss, (1, D))

    vmem = pl.BlockSpec(memory_space=pltpu.VMEM)
    return pl.pallas_call(
        body, name="all_reduce_small", in_specs=[vmem] * 3, out_specs=vmem,
        out_shape=jax.ShapeDtypeStruct((8, D), F32),
        scratch_shapes=[pltpu.VMEM((N_DEV, 8, D), F32), pltpu.SemaphoreType.DMA((N_DEV - 1,)),
                        pltpu.SemaphoreType.DMA((N_DEV - 1,))],
    )(rows, *gain_parts)


def kernel(x, p, a_norm, a_w_in, a_w_group, a_scale, a_w_out, kv_norm, w_kv, k_norm, b_norm, b_w_in, b_q_norm, b_w_out, ple_w, ple_gate_w, loss_target, m_a_norm, m_a_w_in, m_a_w_group, m_a_scale, m_a_w_out, m_kv_norm, m_w_kv, m_k_norm, m_b_norm, m_b_w_in, m_b_q_norm, m_b_w_out, m_ple_w, m_ple_gate_w, v_a_norm, v_a_w_in, v_a_w_group, v_a_scale, v_a_w_out, v_kv_norm, v_w_kv, v_k_norm, v_b_norm, v_b_w_in, v_b_q_norm, v_b_w_out, v_ple_w, v_ple_gate_w):
    s = x.shape[1]
    xi, yi, ci = _place()
    me = 4 * xi + 2 * yi + ci
    plane = 2 * xi + yi

    big = {
        "a_w_in": a_w_in.reshape(D, 256), "a_w_group": a_w_group.reshape(128, 256),
        "a_w_out": a_w_out.reshape(128, D), "w_kv": w_kv, "b_w_in": b_w_in.reshape(D, 256),
        "b_w_out": b_w_out.reshape(128, D), "ple_w0": ple_w[0], "ple_w1": ple_w[1],
        "gate0": ple_gate_w[0], "gate1": ple_gate_w[1],
    }
    names = list(big)
    cast = dict(zip(names, _cast_shards([big[k] for k in names])))
    small = jnp.concatenate([a_norm, a_scale, jnp.zeros((6, 128), F32)], axis=0)
    first = ["a_w_in", "a_w_group", "a_w_out", "ple_w0", "gate0"]
    behind_a = ["w_kv", "b_w_in"]
    behind_attn = ["b_w_out", "ple_w1", "gate1"]
    gathered = _all_gather([cast[k] for k in first] + [small])
    full = dict(zip(first, gathered[:-1]))
    small_all = gathered[-1]
    a_norm_f = small_all[:, 0, :].reshape(1, D)
    a_scale_f = small_all[:, 1, :].reshape(1, D)
    w_a_in = full["a_w_in"]
    w_a_out = full["a_w_out"].reshape(D, D)
    w_gate0 = full["gate0"].reshape(D, D)
    w_ple0 = full["ple_w0"]
    w_group = full["a_w_group"].reshape(N_DEV, 4, 32, 256).transpose(1, 0, 2, 3).reshape(4, 256, 256)
    kvn, bn = kv_norm.reshape(1, D), b_norm
    kg, qg = k_norm.reshape(1, HEAD_DIM), b_q_norm

    x0, p0, p1, target = x[0], p[0, 0], p[1, 0], loss_target[0]
    h0, z, pooled, mcat, y, x1, e0, gt0, x2, w_kv_f, w_b_in = _layer_a_fwd(
        x0, p0, a_norm_f, a_scale_f, w_a_in, w_group, w_a_out, w_ple0, w_gate0,
        rider=_GatherRider([cast[k] for k in behind_a]))
    hkv, hb, k_all, v_all, q_all, zb = _layer_b_in_fwd(x2, kvn, bn, w_kv_f, w_b_in)
    qg2, kg2 = jnp.concatenate([qg, qg], axis=1), jnp.concatenate([kg, kg], axis=1)
    o, csave, w_b_out, w_ple1, w_gate1 = _attn_fwd(
        q_all, k_all, v_all, qg2, kg2, rider=_GatherRider([cast[k] for k in behind_attn]))
    w_b_out, w_gate1 = w_b_out.reshape(D, D), w_gate1.reshape(D, D)
    yb, x3, e1, gt1, dx4, sq_err = _layer_b_out_fwd(o, zb, x2, p1, target, w_b_out, w_ple1, w_gate1)

    de1, dgp1, dx3, d_o, dzb = _layer_b_out_bwd(dx4, e1, gt1, o, zb, w_gate1, w_b_out)
    partial = {
        "b_w_out": _wgrad(yb, dx3, "wgrad_b_w_out").reshape(N_DEV, 128, D),
        "ple_w1": _wgrad(p1, de1, "wgrad_ple_w1", n_split=8),
        "gate1": _wgrad(x3, dgp1, "wgrad_gate1").reshape(N_DEV, 128, D),
    }
    grad = {}
    dq, dk, dv, dqg, dkg, grad["b_w_out"], grad["ple_w1"], grad["gate1"] = _attn_bwd(
        q_all, k_all, v_all, qg2, kg2, d_o, csave,
        rider=_ReduceRider([partial[k] for k in ("b_w_out", "ple_w1", "gate1")]))
    dqz, dkv, dx2, d_bn, d_kvn = _layer_b_in_bwd(dq, dzb, dk, dv, x2, dx3, w_b_in, w_kv_f, bn, kvn)
    partial["w_kv"] = _wgrad(hkv, dkv, "wgrad_w_kv", n_split=8)
    de0, dgp0, dx1, dm, duz, d_as, grad["w_kv"] = _layer_a_out_bwd(
        dx2, e0, gt0, z, mcat, w_gate0, w_a_out, a_scale_f, w_group, rider=_ReduceRider([partial["w_kv"]]))
    partial["b_w_in"] = _wgrad(hb, dqz, "wgrad_b_w_in", n_split=8)
    dx0, d_an, grad["b_w_in"] = _layer_a_in_bwd(duz, x0, dx1, w_a_in, a_norm_f, rider=_ReduceRider([partial["b_w_in"]]))

    dw_group = _wgrad(pooled, dm, "wgrad_a_w_group", a_blocked_b=True)
    partial.update({
        "a_w_in": _wgrad(h0, duz, "wgrad_a_w_in", n_split=8),
        "a_w_group": dw_group.reshape(4, N_DEV, 32, 256).transpose(1, 0, 2, 3).reshape(N_DEV, 128, 256),
        "a_w_out": _wgrad(y, dx1, "wgrad_a_w_out").reshape(N_DEV, 128, D),
        "ple_w0": _wgrad(p0, de0, "wgrad_ple_w0", n_split=8),
        "gate0": _wgrad(x1, dgp0, "wgrad_gate0").reshape(N_DEV, 128, D),
    })
    grad.update(zip(first, _alone("reduce_scatter_layer_a", _ReduceRider([partial[k] for k in first]))))

    grads = [grad[k] for k in names]
    mom = {
        "a_w_in": (m_a_w_in, v_a_w_in), "a_w_group": (m_a_w_group, v_a_w_group), "a_w_out": (m_a_w_out, v_a_w_out),
        "w_kv": (m_w_kv, v_w_kv), "b_w_in": (m_b_w_in, v_b_w_in), "b_w_out": (m_b_w_out, v_b_w_out),
        "ple_w0": (m_ple_w[0], v_ple_w[0]), "ple_w1": (m_ple_w[1], v_ple_w[1]),
        "gate0": (m_ple_gate_w[0], v_ple_gate_w[0]), "gate1": (m_ple_gate_w[1], v_ple_gate_w[1]),
    }
    deltas, new_ms, new_vs = _adamw_shards(
        [big[k] for k in names], grads, [mom[k][0].reshape(big[k].shape) for k in names],
        [mom[k][1].reshape(big[k].shape) for k in names])
    res = {k: (grads[i], deltas[i], new_ms[i], new_vs[i]) for i, k in enumerate(names)}

    rows = jnp.concatenate([d_kvn, d_bn, d_an, d_as, jnp.zeros((2, D), F32), sq_err, jnp.zeros((1, D), F32)], axis=0)
    tot = _all_reduce_small(rows, (dqg, dkg))
    loss = tot[6, 0]
    g_kvn, g_bn = tot[0:1], tot[1:2]
    g_an = lax.dynamic_slice_in_dim(tot[2:3], me * 128, 128, axis=1)
    g_as = lax.dynamic_slice_in_dim(tot[3:4], me * 128, 128, axis=1)
    g_qg, g_kg = tot[4:5, :HEAD_DIM], tot[5:6, :HEAD_DIM]
    sm = {
        "a_norm": (g_an,) + _adamw_small(a_norm, g_an, m_a_norm, v_a_norm, "adamw_a_norm"),
        "a_scale": (g_as,) + _adamw_small(a_scale, g_as, m_a_scale, v_a_scale, "adamw_a_scale"),
        "kv_norm": tuple(t.reshape(D) for t in (g_kvn,) + _adamw_small(
            kvn, g_kvn, m_kv_norm.reshape(1, D), v_kv_norm.reshape(1, D), "adamw_kv_norm")),
        "k_norm": tuple(t.reshape(HEAD_DIM) for t in (g_kg,) + _adamw_small(
            kg, g_kg, m_k_norm.reshape(1, HEAD_DIM), v_k_norm.reshape(1, HEAD_DIM), "adamw_k_norm")),
        "b_norm": (g_bn,) + _adamw_small(b_norm, g_bn, m_b_norm, v_b_norm, "adamw_b_norm"),
        "b_q_norm": (g_qg,) + _adamw_small(b_q_norm, g_qg, m_b_q_norm, v_b_q_norm, "adamw_b_q_norm"),
    }

    def out(kind):
        def big_one(k, shape):
            return res[k][kind].reshape(shape)

        return [
            sm["a_norm"][kind], big_one("a_w_in", a_w_in.shape), big_one("a_w_group", a_w_group.shape),
            sm["a_scale"][kind], big_one("a_w_out", a_w_out.shape), sm["kv_norm"][kind],
            big_one("w_kv", w_kv.shape), sm["k_norm"][kind], sm["b_norm"][kind],
            big_one("b_w_in", b_w_in.shape), sm["b_q_norm"][kind], big_one("b_w_out", b_w_out.shape),
            jnp.stack([res["ple_w0"][kind], res["ple_w1"][kind]]),
            jnp.stack([res["gate0"][kind], res["gate1"][kind]]),
        ]

    return (loss, dx0.reshape(x.shape), *out(0), *out(1), *out(2), *out(3))
```

```python
import functools

import jax
import jax.numpy as jnp
from jax import lax
from jax.experimental import pallas as pl
from jax.experimental.pallas import tpu as pltpu

F32 = jnp.float32
BF16 = jnp.bfloat16
MESH = pl.DeviceIdType.MESH

N_DEV = 8
D = 1024
N_GROUPS = 4
GROUP_DIM = D // N_GROUPS
HALO = 16
HEADS = 16
HEAD_DIM = D // HEADS
SB_SCALE = HEAD_DIM ** -0.5
TILE = 256
LANES = 128
DEAD_LOG = -120.0
EPS = 1e-6
ADAM_LR = 0.001
ADAM_B1 = 0.9
ADAM_B2 = 0.999
ADAM_EPS = 1e-08
ADAM_WD = 0.01
ADAM_STEP = 10
TM = 256
VMEM_LIMIT = 56 * 1024 * 1024

HBM_SPEC = pl.BlockSpec(memory_space=pltpu.HBM)


def _dot(a, b):
    return jnp.dot(a, b, preferred_element_type=F32)


def _dot_nt(a, b):
    return lax.dot_general(a, b, (((1,), (1,)), ((), ())), preferred_element_type=F32)


def _dot_tn(a, b):
    return lax.dot_general(a, b, (((0,), (0,)), ((), ())), preferred_element_type=F32)


def _sigmoid(x):
    return jax.nn.sigmoid(x)


def _split_dot(x, mat):
    hi = x.astype(BF16)
    lo = (x - hi.astype(F32)).astype(BF16)
    return _dot(hi, mat) + _dot(lo, mat)


def _split_dot_many(xs, mat):
    rows = xs[0].shape[0]
    his = [x.astype(BF16) for x in xs]
    los = [(x - hi.astype(F32)).astype(BF16) for x, hi in zip(xs, his)]
    out = _dot(jnp.concatenate(his + los, axis=0), mat)
    n = len(xs)
    return [out[i * rows:(i + 1) * rows] + out[(n + i) * rows:(n + i + 1) * rows] for i in range(n)]


def _rms(x):
    return lax.rsqrt(jnp.mean(x * x, axis=-1, keepdims=True) + EPS)


def _hosted_call(body, name, n_steps, ins, in_specs, out_specs, out_shape, scratch, rider=None):
    ins, scratch = list(ins), list(scratch)
    if rider is None:
        wrapped, extra_in, extra_out, extra_scratch = body, [], [], []
    else:
        extra_in, extra_out, extra_scratch = rider.arrays, rider.out_shape(), rider.scratch()
        n_in, n_out, n_scr = len(ins), len(out_shape), len(scratch)
        k_in, k_out = len(extra_in), len(extra_out)

        def wrapped(*refs):
            own_in, r_in = refs[:n_in], refs[n_in:n_in + k_in]
            own_out = refs[n_in + k_in:n_in + k_in + n_out]
            r_out = refs[n_in + k_in + n_out:n_in + k_in + n_out + k_out]
            rest = refs[n_in + k_in + n_out + k_out:]
            phases = rider.bind(r_in, r_out, rest[n_scr:])
            step = pl.program_id(0)
            pl.when(step == 0)(phases[0])
            body(*own_in, *own_out, *rest[:n_scr])
            for at, phase in zip((min(2, n_steps - 1), n_steps - 1), phases[1:]):
                if phase is not None:
                    pl.when(step == at)(phase)

    return pl.pallas_call(
        wrapped, name=name, grid=(n_steps,),
        in_specs=list(in_specs) + [HBM_SPEC] * len(extra_in),
        out_specs=list(out_specs) + [HBM_SPEC] * len(extra_out),
        out_shape=list(out_shape) + list(extra_out), scratch_shapes=scratch + list(extra_scratch),
        compiler_params=pltpu.CompilerParams(dimension_semantics=("arbitrary",), vmem_limit_bytes=VMEM_LIMIT),
    )(*ins, *extra_in)


def _rows_call(body, name, n_rows, row_ins, const_ins, row_outs, const_outs=(), scratch=(),
               reverse=False, tm=TM, rider=None):
    nb = n_rows // tm

    def row_map(i):
        return ((nb - 1 - i) if reverse else i, 0)

    def const_map(nd):
        return lambda i: (0,) * nd

    in_specs = [pl.BlockSpec((tm, a.shape[1]), row_map) for a in row_ins]
    in_specs += [pl.BlockSpec(a.shape, const_map(a.ndim)) for a in const_ins]
    out_specs = [pl.BlockSpec((tm, w), row_map) for (w, _) in row_outs]
    out_specs += [pl.BlockSpec(s, const_map(len(s))) for (s, _) in const_outs]
    out_shape = [jax.ShapeDtypeStruct((n_rows, w), dt) for (w, dt) in row_outs]
    out_shape += [jax.ShapeDtypeStruct(s, dt) for (s, dt) in const_outs]
    return _hosted_call(body, name, nb, list(row_ins) + list(const_ins), in_specs, out_specs, out_shape,
                        scratch, rider)


def _ple_fwd(p_ref, xin, wple_ref, wgate_ref, e_ref, gt_ref):
    pb = p_ref[...].astype(BF16)
    for j in range(N_DEV):
        e_ref[:, j * 128:(j + 1) * 128] = _dot(pb, wple_ref[j])
    gt = _sigmoid(_dot(xin.astype(BF16), wgate_ref[...]))
    gt_ref[...] = gt
    return xin + e_ref[...] * gt


def _layer_a_fwd(x0, p0, a_norm, a_scale, w_in, w_group, w_out, w_ple, w_gate, rider=None):
    s = x0.shape[0]
    tm = TM

    def body(x_ref, p_ref, an_ref, as_ref, win_ref, wg_ref, wout_ref, wple_ref, wgate_ref,
             h_ref, z_ref, pooled_ref, m_ref, y_ref, x1_ref, e_ref, gt_ref, x2_ref, uext):
        i = pl.program_id(0)

        @pl.when(i == 0)
        def _():
            uext[0:HALO, :] = jnp.zeros((HALO, D), F32)

        x = x_ref[...]
        h = (x * _rms(x) * an_ref[...]).astype(BF16)
        h_ref[...] = h
        for j in range(N_DEV):
            uz = _dot(h, win_ref[j])
            if j < 4:
                uext[HALO:HALO + tm, j * 256:(j + 1) * 256] = uz
            else:
                z_ref[:, (j - 4) * 256:(j - 3) * 256] = uz
        t = i * tm + lax.broadcasted_iota(jnp.int32, (tm, 1), 0)
        for g in range(N_GROUPS):
            w = 2 ** (g + 1)
            cols = slice(g * GROUP_DIM, (g + 1) * GROUP_DIM)
            ext = uext[:, cols]
            acc = ext
            k = 1
            while k < w:
                acc = acc + pltpu.roll(acc, k, 0)
                k *= 2
            cnt = jnp.minimum(t + 1, w).astype(F32)
            pooled = (acc[HALO:] / cnt - ext[HALO:]).astype(BF16)
            pooled_ref[:, cols] = pooled
            m_ref[:, cols] = _dot(pooled, wg_ref[g])
        uext[0:HALO, :] = uext[tm:tm + HALO, :]
        z = z_ref[...]
        y = (m_ref[...] * as_ref[...] * (z * _sigmoid(z))).astype(BF16)
        y_ref[...] = y
        x1 = x + _dot(y, wout_ref[...])
        x1_ref[...] = x1
        x2_ref[...] = _ple_fwd(p_ref, x1, wple_ref, wgate_ref, e_ref, gt_ref)

    row_outs = [(D, BF16), (D, F32), (D, BF16), (D, F32), (D, BF16), (D, F32), (D, F32), (D, F32), (D, F32)]
    return _rows_call(body, "layer_a_fwd", s, [x0, p0], [a_norm, a_scale, w_in, w_group, w_out, w_ple, w_gate],
                      row_outs, scratch=[pltpu.VMEM((tm + HALO, D), F32)], rider=rider)


def _layer_b_in_fwd(x2, kv_norm, b_norm, w_kv, w_bin):
    s = x2.shape[0]

    def body(x_ref, kvn_ref, bn_ref, wkv_ref, wbin_ref, hkv_ref, hb_ref, k_ref, v_ref, q_ref, zb_ref):
        x = x_ref[...]
        n = x * _rms(x)
        hkv = (n * kvn_ref[...]).astype(BF16)
        hb = (n * bn_ref[...]).astype(BF16)
        hkv_ref[...] = hkv
        hb_ref[...] = hb
        for j in range(N_DEV):
            kv = _dot(hkv, wkv_ref[j])
            qz = _dot(hb, wbin_ref[j])
            if j < 4:
                cols = slice(j * 256, (j + 1) * 256)
                k_ref[:, cols] = kv
                q_ref[:, cols] = qz
            else:
                cols = slice((j - 4) * 256, (j - 3) * 256)
                v_ref[:, cols] = kv.astype(BF16)
                zb_ref[:, cols] = qz

    row_outs = [(D, BF16), (D, BF16), (D, F32), (D, BF16), (D, F32), (D, F32)]
    return _rows_call(body, "layer_b_in_fwd", s, [x2], [kv_norm, b_norm, w_kv, w_bin], row_outs)


def _tri(after):
    r = lax.broadcasted_iota(jnp.int32, (TILE, TILE), 0)
    c = lax.broadcasted_iota(jnp.int32, (TILE, TILE), 1)
    return jnp.where((r > c) if after else (r < c), 1.0, 0.0).astype(BF16)


def _half_sums(v):
    r = lax.broadcasted_iota(jnp.int32, (LANES, LANES), 0) < HEAD_DIM
    c = lax.broadcasted_iota(jnp.int32, (LANES, LANES), 1) < HEAD_DIM
    same_head = jnp.where(r == c, 1.0, 0.0).astype(BF16)
    return _split_dot(v, same_head)


def _pair_norm(x):
    r = lax.rsqrt(_half_sums(x * x) * (1.0 / HEAD_DIM) + EPS)
    return x * r, r


def _tile_logits(qblk, kblk, diagonal):
    l = _dot_nt(qblk, kblk)
    sp = jnp.maximum(l, 0.0) + jnp.log(1.0 + jnp.exp(-jnp.abs(l)))
    ls = l - sp
    if not diagonal:
        return None, -sp, ls
    mask = lax.broadcasted_iota(jnp.int32, l.shape, 1) < lax.broadcasted_iota(jnp.int32, l.shape, 0)
    return mask, jnp.where(mask, -sp, 0.0), ls


def _attn_fwd(q_all, k_all, v_all, q_gain2, k_gain2, rider=None):
    s = q_all.shape[0]
    nt = s // TILE

    def body(q_ref, k_ref, v_ref, qg_ref, kg_ref, o_ref, c_ref, qs, ks, vs, tri, acc, right, cmat):
        tri[...] = _tri(True)
        lane = lax.broadcasted_iota(jnp.int32, (TILE, LANES), 1)
        qn, _ = _pair_norm(q_ref[...])
        kn, _ = _pair_norm(k_ref[...])
        qsc = (qn * qg_ref[...] * SB_SCALE).astype(BF16)
        ksc = (kn * kg_ref[...]).astype(BF16)
        for hh in range(2):
            sl = slice(hh * HEAD_DIM, (hh + 1) * HEAD_DIM)
            qs[hh] = qsc[:, sl]
            ks[hh] = ksc[:, sl]
            vs[hh] = v_ref[:, sl]

        def tile(qrows, kb, diagonal):
            rows = pl.ds(pl.multiple_of(kb * TILE, TILE), TILE)
            loaded = [(qs[hh, qrows, :], ks[hh, rows, :], vs[hh, rows, :], right[hh], cmat[hh], acc[hh])
                      for hh in range(2)]
            logits = [_tile_logits(q, k, diagonal) for q, k, _, _, _, _ in loaded]
            later = _split_dot_many([lk for _, lk, _ in logits], tri[...])
            results = []
            for (q, k, v, rt, cm, ac), (mask, lk, ls), lt in zip(loaded, logits, later):
                a = jnp.exp(ls + lt + rt)
                if diagonal:
                    a = jnp.where(mask, a, 0.0)
                results.append((ac + _dot(a.astype(BF16), v), jnp.where(lane == kb, rt[:, :LANES], cm),
                                rt + jnp.sum(lk, axis=1, keepdims=True)))
            for hh, (ac, cm, rt) in enumerate(results):
                acc[hh] = ac
                cmat[hh] = cm
                right[hh] = rt

        def q_step(qb, _):
            r0 = pl.multiple_of(qb * TILE, TILE)
            qrows = pl.ds(r0, TILE)
            acc[...] = jnp.zeros((2, TILE, HEAD_DIM), F32)
            right[...] = jnp.zeros((2, TILE, TILE), F32)
            cmat[...] = jnp.zeros((2, TILE, LANES), F32)
            tile(qrows, qb, True)

            def live():
                return (jnp.max(right[:, :, :LANES]) > DEAD_LOG).astype(jnp.int32)

            def k_step(c):
                kb = c[0] - 1
                tile(qrows, kb, False)
                return kb, live()

            first, _ = lax.while_loop(lambda c: (c[0] > 0) & (c[1] > 0), k_step, (qb, live()))
            for hh in range(2):
                o_ref[qrows, hh * HEAD_DIM:(hh + 1) * HEAD_DIM] = acc[hh]
                c_ref[hh, qrows, :] = jnp.where(lane == LANES - 1, first.astype(F32), cmat[hh])
            return 0

        lax.fori_loop(0, nt, q_step, 0)

    pair = pl.BlockSpec((s, LANES), lambda h: (0, h))
    gain = pl.BlockSpec((1, LANES), lambda h: (0, 0))
    return _hosted_call(
        body, "attn_fwd", HEADS // 2, [q_all, k_all, v_all, q_gain2, k_gain2],
        [pair, pair, pair, gain, gain], [pair, pl.BlockSpec((2, s, LANES), lambda h: (h, 0, 0))],
        [jax.ShapeDtypeStruct((s, D), F32), jax.ShapeDtypeStruct((HEADS, s, LANES), F32)],
        [pltpu.VMEM((2, s, HEAD_DIM), BF16)] * 3
        + [pltpu.VMEM((TILE, TILE), BF16), pltpu.VMEM((2, TILE, HEAD_DIM), F32), pltpu.VMEM((2, TILE, TILE), F32),
           pltpu.VMEM((2, TILE, LANES), F32)], rider)


def _layer_b_out_fwd(o, zb, x2, p1, target, w_out, w_ple, w_gate):
    s = o.shape[0]

    def body(o_ref, zb_ref, x2_ref, p_ref, t_ref, wout_ref, wple_ref, wgate_ref,
             yb_ref, x3_ref, e_ref, gt_ref, dx4_ref, loss_ref):
        zb = zb_ref[...]
        yb = (o_ref[...] * (zb * _sigmoid(zb))).astype(BF16)
        yb_ref[...] = yb
        x3 = x2_ref[...] + _dot(yb, wout_ref[...])
        x3_ref[...] = x3
        x4 = _ple_fwd(p_ref, x3, wple_ref, wgate_ref, e_ref, gt_ref)
        d = x4 - t_ref[...]
        dx4_ref[...] = d * (1.0 / D)

        @pl.when(pl.program_id(0) == 0)
        def _():
            loss_ref[...] = jnp.zeros((1, D), F32)

        loss_ref[...] += jnp.sum(d * d, axis=0, keepdims=True)

    row_outs = [(D, BF16), (D, F32), (D, F32), (D, F32), (D, F32)]
    return _rows_call(body, "layer_b_out_fwd", s, [o, zb, x2, p1, target], [w_out, w_ple, w_gate], row_outs,
                      const_outs=[((1, D), F32)])


def _ple_bwd(dxo, e_ref, gt_ref, wgate_ref, de_ref, dgp_ref):
    e = e_ref[...]
    gt = gt_ref[...]
    de_ref[...] = (dxo * gt).astype(BF16)
    dgp = (dxo * e * gt * (1.0 - gt)).astype(BF16)
    dgp_ref[...] = dgp
    return dxo + _dot_nt(dgp, wgate_ref[...])


def _silu_grads(z):
    sg = _sigmoid(z)
    return z * sg, sg * (1.0 + z * (1.0 - sg))


def _layer_b_out_bwd(dx4, e1, gt1, o, zb, w_gate, w_out):
    s = dx4.shape[0]

    def body(dx4_ref, e_ref, gt_ref, o_ref, zb_ref, wgate_ref, wout_ref,
             de_ref, dgp_ref, dx3_ref, do_ref, dzb_ref):
        dx3 = _ple_bwd(dx4_ref[...], e_ref, gt_ref, wgate_ref, de_ref, dgp_ref)
        dx3_ref[...] = dx3
        dyb = _dot_nt(dx3.astype(BF16), wout_ref[...])
        silu, dsilu = _silu_grads(zb_ref[...])
        do_ref[...] = (dyb * silu).astype(BF16)
        dzb_ref[...] = (dyb * o_ref[...] * dsilu).astype(BF16)

    row_outs = [(D, BF16), (D, BF16), (D, F32), (D, BF16), (D, BF16)]
    return _rows_call(body, "layer_b_out_bwd", s, [dx4, e1, gt1, o, zb], [w_gate, w_out], row_outs)


def _attn_bwd(q_all, k_all, v_all, q_gain2, k_gain2, d_o, csave, rider=None):
    s = q_all.shape[0]
    nt = s // TILE

    def body(q_ref, k_ref, v_ref, qg_ref, kg_ref, do_ref, c_ref,
             dq_ref, dk_ref, dv_ref, dqg_ref, dkg_ref,
             qs, ks, vs, dos, qt, dot_t, tri_a, tri_b, dqa, dkt, dvt, dqb, left):
        tri_a[...] = _tri(True)
        tri_b[...] = _tri(False)
        lane = lax.broadcasted_iota(jnp.int32, (TILE, LANES), 1)
        qn, qr = _pair_norm(q_ref[...])
        kn, kr = _pair_norm(k_ref[...])
        qsc = qn * qg_ref[...] * SB_SCALE
        ksc = (kn * kg_ref[...]).astype(BF16)
        q_t = qsc.T.astype(BF16)
        do_t = do_ref[...].astype(F32).T.astype(BF16)
        for j in range(nt):
            qt[j] = q_t[:, j * TILE:(j + 1) * TILE]
            dot_t[j] = do_t[:, j * TILE:(j + 1) * TILE]
        dkt[...] = jnp.zeros((nt, LANES, TILE), F32)
        dvt[...] = jnp.zeros((nt, LANES, TILE), F32)
        qsc = qsc.astype(BF16)
        for hh in range(2):
            sl = slice(hh * HEAD_DIM, (hh + 1) * HEAD_DIM)
            qs[hh] = qsc[:, sl]
            ks[hh] = ksc[:, sl]
            vs[hh] = v_ref[:, sl]
            dos[hh] = do_ref[:, sl]

        def tile(qb, qrows, kb, diagonal):
            rows = pl.ds(pl.multiple_of(kb * TILE, TILE), TILE)
            heads = range(2)
            kblk = [ks[hh, rows, :] for hh in heads]
            logits = [_tile_logits(qs[hh, qrows, :], kblk[hh], diagonal) for hh in heads]
            later = _split_dot_many([lk for _, lk, _ in logits], tri_a[...])
            a, g = [], []
            for hh in heads:
                mask, _, ls = logits[hh]
                right = jnp.sum(jnp.where(lane == kb, c_ref[hh, qrows, :], 0.0), axis=1, keepdims=True)
                a_h = jnp.exp(ls + later[hh] + right)
                a.append(jnp.where(mask, a_h, 0.0) if diagonal else a_h)
                g.append(a[hh] * _dot_nt(dos[hh, qrows, :], vs[hh, rows, :]))
            before = _split_dot_many(g, tri_b[...])
            for hh in heads:
                sl = slice(hh * HEAD_DIM, (hh + 1) * HEAD_DIM)
                mask, _, ls = logits[hh]
                beta = jnp.exp(ls)
                lf = left[hh]
                dl = g[hh] * (1.0 - beta) - (before[hh] + lf) * beta
                if diagonal:
                    dl = jnp.where(mask, dl, 0.0)
                dl = dl.astype(BF16)
                left[hh] = lf + jnp.sum(g[hh], axis=1, keepdims=True)
                dqb[hh] += _dot(dl, kblk[hh])
                dkt[kb, sl, :] += _dot(qt[qb, sl, :], dl)
                dvt[kb, sl, :] += _dot(dot_t[qb, sl, :], a[hh].astype(BF16))

        def q_step(qb, _):
            qrows = pl.ds(pl.multiple_of(qb * TILE, TILE), TILE)
            dqb[...] = jnp.zeros((2, TILE, HEAD_DIM), F32)
            left[...] = jnp.zeros((2, TILE, TILE), F32)

            def k_step(kb, _):
                tile(qb, qrows, kb, False)
                return 0

            first = jnp.max(jnp.where(lane == LANES - 1, c_ref[0, qrows, :], 0.0)).astype(jnp.int32)
            lax.fori_loop(first, qb, k_step, 0)
            tile(qb, qrows, qb, True)
            for hh in range(2):
                dqa[qrows, hh * HEAD_DIM:(hh + 1) * HEAD_DIM] = dqb[hh] * SB_SCALE
            return 0

        lax.fori_loop(0, nt, q_step, 0)

        def norm_bwd(dy, xn, r, g_ref, dx_ref, dg_ref):
            dg_ref[...] = jnp.sum(dy * xn, axis=0, keepdims=True)
            dxn = dy * g_ref[...]
            dx_ref[...] = r * (dxn - xn * (_half_sums(dxn * xn) * (1.0 / HEAD_DIM)))

        norm_bwd(dqa[...], qn, qr, qg_ref, dq_ref, dqg_ref)
        for j in range(nt):
            dqa[j * TILE:(j + 1) * TILE, :] = dkt[j].T
            dv_ref[j * TILE:(j + 1) * TILE, :] = dvt[j].T
        norm_bwd(dqa[...], kn, kr, kg_ref, dk_ref, dkg_ref)

    pair = pl.BlockSpec((s, LANES), lambda h: (0, h))
    gain = pl.BlockSpec((1, LANES), lambda h: (0, 0))
    dgain = pl.BlockSpec((None, 1, LANES), lambda h: (h, 0, 0))
    return _hosted_call(
        body, "attn_bwd", HEADS // 2, [q_all, k_all, v_all, q_gain2, k_gain2, d_o, csave],
        [pair, pair, pair, gain, gain, pair, pl.BlockSpec((2, s, LANES), lambda h: (h, 0, 0))],
        [pair, pair, pair, dgain, dgain],
        [jax.ShapeDtypeStruct((s, D), F32)] * 3 + [jax.ShapeDtypeStruct((HEADS // 2, 1, LANES), F32)] * 2,
        [pltpu.VMEM((2, s, HEAD_DIM), BF16)] * 4
        + [pltpu.VMEM((nt, LANES, TILE), BF16)] * 2 + [pltpu.VMEM((TILE, TILE), BF16)] * 2
        + [pltpu.VMEM((s, LANES), F32)] + [pltpu.VMEM((nt, LANES, TILE), F32)] * 2
        + [pltpu.VMEM((2, TILE, HEAD_DIM), F32), pltpu.VMEM((2, TILE, TILE), F32)], rider)


def _norm_bwd_rows(dh, x, gain, dgain_ref):
    r = _rms(x)
    n = x * r
    dgain_ref[...] += jnp.sum(dh * n, axis=0, keepdims=True)
    dn = dh * gain
    return r * (dn - n * jnp.mean(dn * n, axis=-1, keepdims=True))


def _layer_b_in_bwd(dq, dzb, dk, dv, x2, dx3, w_bin, w_kv, b_norm, kv_norm):
    s = x2.shape[0]

    def body(dq_ref, dzb_ref, dk_ref, dv_ref, x_ref, dx3_ref, wbin_ref, wkv_ref, bn_ref, kvn_ref,
             dqz_ref, dkv_ref, dx2_ref, dbn_ref, dkvn_ref):
        @pl.when(pl.program_id(0) == 0)
        def _():
            dbn_ref[...] = jnp.zeros((1, D), F32)
            dkvn_ref[...] = jnp.zeros((1, D), F32)

        dqz_ref[:, :D] = dq_ref[...].astype(BF16)
        dqz_ref[:, D:] = dzb_ref[...]
        dkv_ref[:, :D] = dk_ref[...].astype(BF16)
        dkv_ref[:, D:] = dv_ref[...].astype(BF16)
        dhb = jnp.zeros((TM, D), F32)
        dhkv = jnp.zeros((TM, D), F32)
        for j in range(N_DEV):
            cols = slice(j * 256, (j + 1) * 256)
            dhb = dhb + _dot_nt(dqz_ref[:, cols], wbin_ref[j])
            dhkv = dhkv + _dot_nt(dkv_ref[:, cols], wkv_ref[j])
        x = x_ref[...]
        dx2 = dx3_ref[...] + _norm_bwd_rows(dhb, x, bn_ref[...], dbn_ref)
        dx2_ref[...] = dx2 + _norm_bwd_rows(dhkv, x, kvn_ref[...], dkvn_ref)

    row_outs = [(2 * D, BF16), (2 * D, BF16), (D, F32)]
    return _rows_call(body, "layer_b_in_bwd", s, [dq, dzb, dk, dv, x2, dx3], [w_bin, w_kv, b_norm, kv_norm],
                      row_outs, const_outs=[((1, D), F32), ((1, D), F32)])


def _layer_a_out_bwd(dx2, e0, gt0, z, m, w_gate, w_out, a_scale, w_group, rider=None):
    s = dx2.shape[0]
    tm = TM
    nb = s // tm

    def body(dx2_ref, e_ref, gt_ref, z_ref, m_ref, wgate_ref, wout_ref, as_ref, wg_ref,
             de_ref, dgp_ref, dx1_ref, dm_ref, duz_ref, das_ref, ext):
        i = pl.program_id(0)

        @pl.when(i == 0)
        def _():
            das_ref[...] = jnp.zeros((1, D), F32)
            ext[tm:tm + HALO, :] = jnp.zeros((HALO, D), F32)

        dx1 = _ple_bwd(dx2_ref[...], e_ref, gt_ref, wgate_ref, de_ref, dgp_ref)
        dx1_ref[...] = dx1
        dy = _dot_nt(dx1.astype(BF16), wout_ref[...])
        silu, dsilu = _silu_grads(z_ref[...])
        m = m_ref[...]
        dmixed = dy * silu
        duz_ref[:, D:] = (dy * (m * as_ref[...]) * dsilu).astype(BF16)
        das_ref[...] += jnp.sum(dmixed * m, axis=0, keepdims=True)
        dm_ref[...] = (dmixed * as_ref[...]).astype(BF16)
        t = (nb - 1 - i) * tm + lax.broadcasted_iota(jnp.int32, (tm, 1), 0)
        n_ext = tm + HALO
        for g in range(N_GROUPS):
            w = 2 ** (g + 1)
            cols = slice(g * GROUP_DIM, (g + 1) * GROUP_DIM)
            dpool = _dot_nt(dm_ref[:, cols], wg_ref[g])
            ext[0:tm, cols] = dpool / jnp.minimum(t + 1, w).astype(F32)
            acc = ext[:, cols]
            k = 1
            while k < w:
                acc = acc + pltpu.roll(acc, n_ext - k, 0)
                k *= 2
            duz_ref[:, cols] = (acc[:tm] - dpool).astype(BF16)
        ext[tm:tm + HALO, :] = ext[0:HALO, :]

    row_outs = [(D, BF16), (D, BF16), (D, F32), (D, BF16), (2 * D, BF16)]
    return _rows_call(body, "layer_a_out_bwd", s, [dx2, e0, gt0, z, m], [w_gate, w_out, a_scale, w_group],
                      row_outs, const_outs=[((1, D), F32)], scratch=[pltpu.VMEM((tm + HALO, D), F32)],
                      reverse=True, rider=rider)


def _layer_a_in_bwd(duz, x0, dx1, w_in, a_norm, rider=None):
    s = x0.shape[0]

    def body(duz_ref, x_ref, dx1_ref, win_ref, an_ref, dx0_ref, dan_ref):
        @pl.when(pl.program_id(0) == 0)
        def _():
            dan_ref[...] = jnp.zeros((1, D), F32)

        dh = jnp.zeros((TM, D), F32)
        for j in range(N_DEV):
            dh = dh + _dot_nt(duz_ref[:, j * 256:(j + 1) * 256], win_ref[j])
        dx0_ref[...] = dx1_ref[...] + _norm_bwd_rows(dh, x_ref[...], an_ref[...], dan_ref)

    return _rows_call(body, "layer_a_in_bwd", s, [duz, x0, dx1], [w_in, a_norm], [(D, F32)],
                      const_outs=[((1, D), F32)], rider=rider)


def _wgrad(a, b, name, n_split=1, a_blocked_b=False):
    s, k = a.shape
    n = b.shape[1]
    tk = 256
    nb = n // n_split

    def body(a_ref, b_ref, o_ref):
        res = _dot_tn(a_ref[...].astype(BF16), b_ref[...].astype(BF16))
        if n_split == 1:
            o_ref[...] = res.astype(BF16)
        else:
            for j in range(n_split):
                o_ref[j] = res[:, j * nb:(j + 1) * nb].astype(BF16)

    if a_blocked_b:
        b_spec = pl.BlockSpec((s, tk), lambda i: (0, i))
        out_spec = pl.BlockSpec((None, tk, tk), lambda i: (i, 0, 0))
        out_shape = jax.ShapeDtypeStruct((k // tk, tk, tk), BF16)
    elif n_split == 1:
        b_spec = pl.BlockSpec((s, n), lambda i: (0, 0))
        out_spec = pl.BlockSpec((tk, n), lambda i: (i, 0))
        out_shape = jax.ShapeDtypeStruct((k, n), BF16)
    else:
        b_spec = pl.BlockSpec((s, n), lambda i: (0, 0))
        out_spec = pl.BlockSpec((n_split, tk, nb), lambda i: (0, i, 0))
        out_shape = jax.ShapeDtypeStruct((n_split, k, nb), BF16)
    return pl.pallas_call(
        body, name=name, grid=(k // tk,),
        in_specs=[pl.BlockSpec((s, tk), lambda i: (0, i)), b_spec],
        out_specs=out_spec, out_shape=out_shape,
        compiler_params=pltpu.CompilerParams(dimension_semantics=("arbitrary",), vmem_limit_bytes=VMEM_LIMIT),
    )(a, b)


def _cast_shards(shards):
    n = len(shards)

    def body(*refs):
        for a in range(n):
            refs[n + a][...] = refs[a][...].astype(BF16)

    vmem = pl.BlockSpec(memory_space=pltpu.VMEM)
    return pl.pallas_call(
        body, name="cast_shards", in_specs=[vmem] * n, out_specs=[vmem] * n,
        out_shape=[jax.ShapeDtypeStruct(a.shape, BF16) for a in shards],
        compiler_params=pltpu.CompilerParams(vmem_limit_bytes=VMEM_LIMIT),
    )(*shards)


def _adamw(w, g, m, v):
    m = ADAM_B1 * m + (1.0 - ADAM_B1) * g
    v = ADAM_B2 * v + (1.0 - ADAM_B2) * jnp.square(g)
    m_hat = m / (1.0 - ADAM_B1 ** ADAM_STEP)
    v_hat = v / (1.0 - ADAM_B2 ** ADAM_STEP)
    delta = -ADAM_LR * (m_hat / (jnp.sqrt(v_hat) + ADAM_EPS) + ADAM_WD * w)
    return delta, m, v


def _adamw_small(w, g, m, v, name):
    def body(w_ref, g_ref, m_ref, v_ref, d_ref, m2_ref, v2_ref):
        d_ref[...], m2_ref[...], v2_ref[...] = _adamw(w_ref[...], g_ref[...], m_ref[...], v_ref[...])

    vmem = pl.BlockSpec(memory_space=pltpu.VMEM)
    return pl.pallas_call(
        body, name=name, in_specs=[vmem] * 4, out_specs=[vmem] * 3,
        out_shape=[jax.ShapeDtypeStruct(w.shape, F32)] * 3,
    )(w, g, m, v)


def _place():
    return lax.axis_index("x"), lax.axis_index("y"), lax.axis_index("c")


def _all_gather(shards):
    return _alone("all_gather_weights", _GatherRider(shards))


def _alone(name, rider):
    n_in, n_out = len(rider.arrays), len(rider.out_shape())

    def body(*refs):
        for phase in rider.bind(refs[:n_in], refs[n_in:n_in + n_out], refs[n_in + n_out:]):
            if phase is not None:
                phase()

    return pl.pallas_call(
        body, name=name, in_specs=[HBM_SPEC] * n_in, out_specs=[HBM_SPEC] * n_out,
        out_shape=rider.out_shape(), scratch_shapes=rider.scratch(),
        compiler_params=pltpu.CompilerParams(vmem_limit_bytes=VMEM_LIMIT),
    )(*rider.arrays)


class _GatherRider:
    def __init__(self, shards):
        self.arrays = list(shards)

    def out_shape(self):
        return _Gather.out_shape(self.arrays)

    def scratch(self):
        return _Gather.semaphores(len(self.arrays))

    def bind(self, ins, outs, scratch):
        moving = _Gather(ins, outs, *scratch)
        return moving.start, moving.forward, moving.finish


class _ReduceRider:
    def __init__(self, partials):
        self.arrays = list(partials)

    def out_shape(self):
        return [jax.ShapeDtypeStruct(a.shape[1:], F32) for a in self.arrays]

    def scratch(self):
        n = len(self.arrays)
        dma = pltpu.SemaphoreType.DMA

        def blocks(k):
            return [pltpu.VMEM((k,) + a.shape[1:], BF16) for a in self.arrays]

        return (blocks(4) + blocks(4) + blocks(3) + [pltpu.VMEM(a.shape[1:], F32) for a in self.arrays]
                + [dma((4 * n,)), dma((4 * n,)), dma((4 * n,)), dma((3 * n,)), dma((3 * n,)), dma((n,))])

    def bind(self, ins, outs, scratch):
        n = len(ins)
        mine, landed, arrived, total = (scratch[i * n:(i + 1) * n] for i in range(4))
        send1, recv1, local1, send2, recv2, out_sems = scratch[4 * n:]
        x, y, c = _place()
        plane = 2 * x + y
        peers = [(1 - x, y), (x, 1 - y), (1 - x, 1 - y)]

        def to_sibling(a, k):
            return pltpu.make_async_remote_copy(
                src_ref=ins[a].at[2 * k + (1 - c)], dst_ref=landed[a].at[k],
                send_sem=send1.at[4 * a + k], recv_sem=recv1.at[4 * a + k],
                device_id=(x, y, 1 - c), device_id_type=MESH)

        def own_block(a, k):
            return pltpu.make_async_copy(ins[a].at[2 * k + c], mine[a].at[k], local1.at[4 * a + k])

        def to_owner(a, j):
            px, py = peers[j]
            return pltpu.make_async_remote_copy(
                src_ref=mine[a].at[2 * px + py], dst_ref=arrived[a].at[j],
                send_sem=send2.at[3 * a + j], recv_sem=recv2.at[3 * a + j],
                device_id=(px, py, c), device_id_type=MESH)

        def result(a):
            return pltpu.make_async_copy(total[a], outs[a], out_sems.at[a])

        def first():
            for a in range(n):
                for k in range(4):
                    to_sibling(a, k).start()
                    own_block(a, k).start()

        def middle():
            for a in range(n):
                for k in range(4):
                    own_block(a, k).wait()
                    to_sibling(a, k).wait_recv()
                total[a][...] = mine[a][plane].astype(F32) + landed[a][plane].astype(F32)
                for k in range(4):
                    mine[a][k] = (mine[a][k].astype(F32) + landed[a][k].astype(F32)).astype(BF16)
                for j in range(3):
                    to_owner(a, j).start()

        def last():
            for a in range(n):
                for j in range(3):
                    to_owner(a, j).wait_recv()
                    total[a][...] += arrived[a][j].astype(F32)
                result(a).start()
            for a in range(n):
                for k in range(4):
                    to_sibling(a, k).wait_send()
                for j in range(3):
                    to_owner(a, j).wait_send()
                result(a).wait()

        return first, middle, last


class _Gather:
    COPIES = 9

    def __init__(self, ins, outs, send_sems, recv_sems, local_sems):
        self.ins, self.outs = ins, outs
        self.send_sems, self.recv_sems, self.local_sems = send_sems, recv_sems, local_sems
        self.x, self.y, self.c = _place()

    @staticmethod
    def out_shape(shards):
        return [jax.ShapeDtypeStruct((N_DEV,) + a.shape, a.dtype) for a in shards]

    @staticmethod
    def semaphores(n):
        dma = pltpu.SemaphoreType.DMA
        return [dma((_Gather.COPIES * n,)), dma((_Gather.COPIES * n,)), dma((n,))]

    def _copy(self, a, k, block, to, own=False, half=None):
        px, py, pc = block
        slot = self.outs[a].at[4 * px + 2 * py + pc]
        if half is not None:
            rows = slot.shape[0] // 2
            slot = slot.at[pl.ds(half * rows, rows)]
        return pltpu.make_async_remote_copy(
            src_ref=self.ins[a] if own else slot, dst_ref=slot,
            send_sem=self.send_sems.at[self.COPIES * a + k], recv_sem=self.recv_sems.at[self.COPIES * a + k],
            device_id=to, device_id_type=MESH)

    def _local(self, a):
        return pltpu.make_async_copy(self.ins[a], self.outs[a].at[4 * self.x + 2 * self.y + self.c],
                                     self.local_sems.at[a])

    def _plan(self, a, c):
        x, y = self.x, self.y
        me, sibling = (x, y, c), (x, y, 1 - c)
        xn, yn, dg = (1 - x, y, c), (x, 1 - y, c), (1 - x, 1 - y, c)
        return [
            self._copy(a, 0, me, sibling, own=True), self._copy(a, 1, me, xn, own=True),
            self._copy(a, 2, me, yn, own=True),
            self._copy(a, 3, xn, yn, half=0), self._copy(a, 4, yn, xn, half=1),
            self._copy(a, 5, xn, sibling), self._copy(a, 6, yn, sibling),
            self._copy(a, 7, dg, sibling, half=0), self._copy(a, 8, dg, sibling, half=1),
        ]

    def _arrivals(self, a):
        x, y, c = self.x, self.y, self.c
        me = (x, y, c)
        xn, yn, dg = (1 - x, y, c), (x, 1 - y, c), (1 - x, 1 - y, c)
        other = 1 - c
        return [
            self._copy(a, 0, (x, y, other), me), self._copy(a, 1, xn, me), self._copy(a, 2, yn, me),
            self._copy(a, 3, dg, me, half=0), self._copy(a, 4, dg, me, half=1),
            self._copy(a, 5, (1 - x, y, other), me), self._copy(a, 6, (x, 1 - y, other), me),
            self._copy(a, 7, (1 - x, 1 - y, other), me, half=0), self._copy(a, 8, (1 - x, 1 - y, other), me, half=1),
        ]

    def start(self):
        for a in range(len(self.ins)):
            self._local(a).start()
            for cp in self._plan(a, self.c)[:3]:
                cp.start()

    def forward(self):
        for a in range(len(self.ins)):
            sends, lands = self._plan(a, self.c), self._arrivals(a)
            lands[1].wait_recv()
            sends[3].start()
            sends[5].start()
            lands[2].wait_recv()
            sends[4].start()
            sends[6].start()

    def finish(self):
        n = len(self.ins)
        for a in range(n):
            sends, lands = self._plan(a, self.c), self._arrivals(a)
            lands[3].wait_recv()
            sends[7].start()
            lands[4].wait_recv()
            sends[8].start()
        for a in range(n):
            lands = self._arrivals(a)
            for k in (0, 5, 6, 7, 8):
                lands[k].wait_recv()
        for a in range(n):
            for cp in self._plan(a, self.c):
                cp.wait_send()
            self._local(a).wait()


def _adamw_shards(ws, gs, ms, vs):
    n = len(ws)

    def body(*refs):
        w, g, m, v = refs[:n], refs[n:2 * n], refs[2 * n:3 * n], refs[3 * n:4 * n]
        outs = refs[4 * n:]
        for a in range(n):
            outs[a][...], outs[n + a][...], outs[2 * n + a][...] = _adamw(w[a][...], g[a][...], m[a][...], v[a][...])

    vmem = pl.BlockSpec(memory_space=pltpu.VMEM)
    res = pl.pallas_call(
        body, name="adamw_shards", in_specs=[vmem] * (4 * n), out_specs=[vmem] * (3 * n),
        out_shape=[jax.ShapeDtypeStruct(a.shape, F32) for a in ws] * 3,
        compiler_params=pltpu.CompilerParams(vmem_limit_bytes=VMEM_LIMIT),
    )(*ws, *gs, *ms, *vs)
    return res[:n], res[n:2 * n], res[2 * n:]


def _all_reduce_small(rows, gain_parts):
    def body(rows_ref, dqg_ref, dkg_ref, out_ref, buf, send_sems, recv_sems):
        x, y, c = _place()
        me = 4 * x + 2 * y + c
        buf[0] = rows_ref[...]
        for row, part in ((4, dqg_ref), (5, dkg_ref)):
            both = jnp.sum(part[...].reshape(HEADS // 2, LANES), axis=0, keepdims=True)
            buf[0, row:row + 1, 0:HEAD_DIM] = both[:, :HEAD_DIM] + both[:, HEAD_DIM:]
        copies = []
        for r in range(1, N_DEV):
            bx, by, bc = (r >> 2) & 1, (r >> 1) & 1, r & 1
            to = (x ^ bx, y ^ by, c ^ bc)
            copies.append(pltpu.make_async_remote_copy(
                src_ref=buf.at[0], dst_ref=buf.at[r], send_sem=send_sems.at[r - 1], recv_sem=recv_sems.at[r - 1],
                device_id=to, device_id_type=MESH))
        for cp in copies:
            cp.start()
        for cp in copies:
            cp.wait_recv()
        for cp in copies:
            cp.wait_send()
        tot = buf[me]
        for j in range(1, N_DEV):
            tot = tot + buf[j ^ me]
        out_ref[...] = tot
        loss = (0.5 / D) * jnp.sum(tot[6:7, :], axis=1, keepdims=True)
        out_ref[6:7, :] = jnp.broadcast_to(loss, (1, D))

    vmem = pl.BlockSpec(memory_space=pltpu.VMEM)
    return pl.pallas_call(
        body, name="all_reduce_small", in_specs=[vmem] * 3, out_specs=vmem,
        out_shape=jax.ShapeDtypeStruct((8, D), F32),
        scratch_shapes=[pltpu.VMEM((N_DEV, 8, D), F32), pltpu.SemaphoreType.DMA((N_DEV - 1,)),
                        pltpu.SemaphoreType.DMA((N_DEV - 1,))],
    )(rows, *gain_parts)


def kernel(x, p, a_norm, a_w_in, a_w_group, a_scale, a_w_out, kv_norm, w_kv, k_norm, b_norm, b_w_in, b_q_norm, b_w_out, ple_w, ple_gate_w, loss_target, m_a_norm, m_a_w_in, m_a_w_group, m_a_scale, m_a_w_out, m_kv_norm, m_w_kv, m_k_norm, m_b_norm, m_b_w_in, m_b_q_norm, m_b_w_out, m_ple_w, m_ple_gate_w, v_a_norm, v_a_w_in, v_a_w_group, v_a_scale, v_a_w_out, v_kv_norm, v_w_kv, v_k_norm, v_b_norm, v_b_w_in, v_b_q_norm, v_b_w_out, v_ple_w, v_ple_gate_w):
    s = x.shape[1]
    xi, yi, ci = _place()
    me = 4 * xi + 2 * yi + ci
    plane = 2 * xi + yi

    big = {
        "a_w_in": a_w_in.reshape(D, 256), "a_w_group": a_w_group.reshape(128, 256),
        "a_w_out": a_w_out.reshape(128, D), "w_kv": w_kv, "b_w_in": b_w_in.reshape(D, 256),
        "b_w_out": b_w_out.reshape(128, D), "ple_w0": ple_w[0], "ple_w1": ple_w[1],
        "gate0": ple_gate_w[0], "gate1": ple_gate_w[1],
    }
    names = list(big)
    cast = dict(zip(names, _cast_shards([big[k] for k in names])))
    small = jnp.concatenate([a_norm, a_scale, jnp.zeros((14, 128), F32)], axis=0)
    first = ["a_w_in", "a_w_group", "a_w_out", "ple_w0", "gate0"]
    behind_a = ["w_kv", "b_w_in"]
    behind_attn = ["b_w_out", "ple_w1", "gate1"]
    gathered = _all_gather([cast[k] for k in first] + [small])
    full = dict(zip(first, gathered[:-1]))
    small_all = gathered[-1]
    a_norm_f = small_all[:, 0, :].reshape(1, D)
    a_scale_f = small_all[:, 1, :].reshape(1, D)
    w_a_in = full["a_w_in"]
    w_a_out = full["a_w_out"].reshape(D, D)
    w_gate0 = full["gate0"].reshape(D, D)
    w_ple0 = full["ple_w0"]
    w_group = full["a_w_group"].reshape(N_DEV, 4, 32, 256).transpose(1, 0, 2, 3).reshape(4, 256, 256)
    kvn, bn = kv_norm.reshape(1, D), b_norm
    kg, qg = k_norm.reshape(1, HEAD_DIM), b_q_norm

    x0, p0, p1, target = x[0], p[0, 0], p[1, 0], loss_target[0]
    h0, z, pooled, mcat, y, x1, e0, gt0, x2, w_kv_f, w_b_in = _layer_a_fwd(
        x0, p0, a_norm_f, a_scale_f, w_a_in, w_group, w_a_out, w_ple0, w_gate0,
        rider=_GatherRider([cast[k] for k in behind_a]))
    hkv, hb, k_all, v_all, q_all, zb = _layer_b_in_fwd(x2, kvn, bn, w_kv_f, w_b_in)
    qg2, kg2 = jnp.concatenate([qg, qg], axis=1), jnp.concatenate([kg, kg], axis=1)
    o, csave, w_b_out, w_ple1, w_gate1 = _attn_fwd(
        q_all, k_all, v_all, qg2, kg2, rider=_GatherRider([cast[k] for k in behind_attn]))
    w_b_out, w_gate1 = w_b_out.reshape(D, D), w_gate1.reshape(D, D)
    yb, x3, e1, gt1, dx4, sq_err = _layer_b_out_fwd(o, zb, x2, p1, target, w_b_out, w_ple1, w_gate1)

    de1, dgp1, dx3, d_o, dzb = _layer_b_out_bwd(dx4, e1, gt1, o, zb, w_gate1, w_b_out)
    partial = {
        "b_w_out": _wgrad(yb, dx3, "wgrad_b_w_out").reshape(N_DEV, 128, D),
        "ple_w1": _wgrad(p1, de1, "wgrad_ple_w1", n_split=8),
        "gate1": _wgrad(x3, dgp1, "wgrad_gate1").reshape(N_DEV, 128, D),
    }
    grad = {}
    dq, dk, dv, dqg, dkg, grad["b_w_out"], grad["ple_w1"], grad["gate1"] = _attn_bwd(
        q_all, k_all, v_all, qg2, kg2, d_o, csave,
        rider=_ReduceRider([partial[k] for k in ("b_w_out", "ple_w1", "gate1")]))
    dqz, dkv, dx2, d_bn, d_kvn = _layer_b_in_bwd(dq, dzb, dk, dv, x2, dx3, w_b_in, w_kv_f, bn, kvn)
    partial["w_kv"] = _wgrad(hkv, dkv, "wgrad_w_kv", n_split=8)
    partial["b_w_in"] = _wgrad(hb, dqz, "wgrad_b_w_in", n_split=8)
    de0, dgp0, dx1, dm, duz, d_as, grad["w_kv"], grad["b_w_in"] = _layer_a_out_bwd(
        dx2, e0, gt0, z, mcat, w_gate0, w_a_out, a_scale_f, w_group,
        rider=_ReduceRider([partial["w_kv"], partial["b_w_in"]]))
    dw_group = _wgrad(pooled, dm, "wgrad_a_w_group", a_blocked_b=True)
    partial.update({
        "a_w_group": dw_group.reshape(4, N_DEV, 32, 256).transpose(1, 0, 2, 3).reshape(N_DEV, 128, 256),
        "a_w_out": _wgrad(y, dx1, "wgrad_a_w_out").reshape(N_DEV, 128, D),
        "ple_w0": _wgrad(p0, de0, "wgrad_ple_w0", n_split=8),
        "gate0": _wgrad(x1, dgp0, "wgrad_gate0").reshape(N_DEV, 128, D),
    })
    behind_a_in = ["a_w_group", "a_w_out", "ple_w0", "gate0"]
    dx0, d_an, *done = _layer_a_in_bwd(duz, x0, dx1, w_a_in, a_norm_f,
                                      rider=_ReduceRider([partial[k] for k in behind_a_in]))
    grad.update(zip(behind_a_in, done))
    partial["a_w_in"] = _wgrad(h0, duz, "wgrad_a_w_in", n_split=8)
    grad["a_w_in"], = _alone("reduce_scatter_a_w_in", _ReduceRider([partial["a_w_in"]]))

    grads = [grad[k] for k in names]
    mom = {
        "a_w_in": (m_a_w_in, v_a_w_in), "a_w_group": (m_a_w_group, v_a_w_group), "a_w_out": (m_a_w_out, v_a_w_out),
        "w_kv": (m_w_kv, v_w_kv), "b_w_in": (m_b_w_in, v_b_w_in), "b_w_out": (m_b_w_out, v_b_w_out),
        "ple_w0": (m_ple_w[0], v_ple_w[0]), "ple_w1": (m_ple_w[1], v_ple_w[1]),
        "gate0": (m_ple_gate_w[0], v_ple_gate_w[0]), "gate1": (m_ple_gate_w[1], v_ple_gate_w[1]),
    }
    deltas, new_ms, new_vs = _adamw_shards(
        [big[k] for k in names], grads, [mom[k][0].reshape(big[k].shape) for k in names],
        [mom[k][1].reshape(big[k].shape) for k in names])
    res = {k: (grads[i], deltas[i], new_ms[i], new_vs[i]) for i, k in enumerate(names)}

    rows = jnp.concatenate([d_kvn, d_bn, d_an, d_as, jnp.zeros((2, D), F32), sq_err, jnp.zeros((1, D), F32)], axis=0)
    tot = _all_reduce_small(rows, (dqg, dkg))
    loss = tot[6, 0]
    g_kvn, g_bn = tot[0:1], tot[1:2]
    g_an = lax.dynamic_slice_in_dim(tot[2:3], me * 128, 128, axis=1)
    g_as = lax.dynamic_slice_in_dim(tot[3:4], me * 128, 128, axis=1)
    g_qg, g_kg = tot[4:5, :HEAD_DIM], tot[5:6, :HEAD_DIM]
    sm = {
        "a_norm": (g_an,) + _adamw_small(a_norm, g_an, m_a_norm, v_a_norm, "adamw_a_norm"),
        "a_scale": (g_as,) + _adamw_small(a_scale, g_as, m_a_scale, v_a_scale, "adamw_a_scale"),
        "kv_norm": tuple(t.reshape(D) for t in (g_kvn,) + _adamw_small(
            kvn, g_kvn, m_kv_norm.reshape(1, D), v_kv_norm.reshape(1, D), "adamw_kv_norm")),
        "k_norm": tuple(t.reshape(HEAD_DIM) for t in (g_kg,) + _adamw_small(
            kg, g_kg, m_k_norm.reshape(1, HEAD_DIM), v_k_norm.reshape(1, HEAD_DIM), "adamw_k_norm")),
        "b_norm": (g_bn,) + _adamw_small(b_norm, g_bn, m_b_norm, v_b_norm, "adamw_b_norm"),
        "b_q_norm": (g_qg,) + _adamw_small(b_q_norm, g_qg, m_b_q_norm, v_b_q_norm, "adamw_b_q_norm"),
    }

    def out(kind):
        def big_one(k, shape):
            return res[k][kind].reshape(shape)

        return [
            sm["a_norm"][kind], big_one("a_w_in", a_w_in.shape), big_one("a_w_group", a_w_group.shape),
            sm["a_scale"][kind], big_one("a_w_out", a_w_out.shape), sm["kv_norm"][kind],
            big_one("w_kv", w_kv.shape), sm["k_norm"][kind], sm["b_norm"][kind],
            big_one("b_w_in", b_w_in.shape), sm["b_q_norm"][kind], big_one("b_w_out", b_w_out.shape),
            jnp.stack([res["ple_w0"][kind], res["ple_w1"][kind]]),
            jnp.stack([res["gate0"][kind], res["gate1"][kind]]),
        ]

    return (loss, dx0.reshape(x.shape), *out(0), *out(1), *out(2), *out(3))
```

```python
import functools

import jax
import jax.numpy as jnp
from jax import lax
from jax.experimental import pallas as pl
from jax.experimental.pallas import tpu as pltpu

F32 = jnp.float32
BF16 = jnp.bfloat16
MESH = pl.DeviceIdType.MESH

N_DEV = 8
D = 1024
N_GROUPS = 4
GROUP_DIM = D // N_GROUPS
HALO = 16
HEADS = 16
HEAD_DIM = D // HEADS
SB_SCALE = HEAD_DIM ** -0.5
TILE = 256
LANES = 128
DEAD_LOG = -120.0
EPS = 1e-6
ADAM_LR = 0.001
ADAM_B1 = 0.9
ADAM_B2 = 0.999
ADAM_EPS = 1e-08
ADAM_WD = 0.01
ADAM_STEP = 10
TM = 256
VMEM_LIMIT = 56 * 1024 * 1024

HBM_SPEC = pl.BlockSpec(memory_space=pltpu.HBM)


def _dot(a, b):
    return jnp.dot(a, b, preferred_element_type=F32)


def _dot_nt(a, b):
    return lax.dot_general(a, b, (((1,), (1,)), ((), ())), preferred_element_type=F32)


def _dot_tn(a, b):
    return lax.dot_general(a, b, (((0,), (0,)), ((), ())), preferred_element_type=F32)


def _sigmoid(x):
    return jax.nn.sigmoid(x)


def _split_dot(x, mat):
    hi = x.astype(BF16)
    lo = (x - hi.astype(F32)).astype(BF16)
    return _dot(hi, mat) + _dot(lo, mat)


def _split_dot_many(xs, mat):
    rows = xs[0].shape[0]
    his = [x.astype(BF16) for x in xs]
    los = [(x - hi.astype(F32)).astype(BF16) for x, hi in zip(xs, his)]
    out = _dot(jnp.concatenate(his + los, axis=0), mat)
    n = len(xs)
    return [out[i * rows:(i + 1) * rows] + out[(n + i) * rows:(n + i + 1) * rows] for i in range(n)]


def _rms(x):
    return lax.rsqrt(jnp.mean(x * x, axis=-1, keepdims=True) + EPS)


def _hosted_call(body, name, n_steps, ins, in_specs, out_specs, out_shape, scratch, rider=None):
    ins, scratch = list(ins), list(scratch)
    if rider is None:
        wrapped, extra_in, extra_out, extra_scratch = body, [], [], []
    else:
        extra_in, extra_out, extra_scratch = rider.arrays, rider.out_shape(), rider.scratch()
        n_in, n_out, n_scr = len(ins), len(out_shape), len(scratch)
        k_in, k_out = len(extra_in), len(extra_out)

        def wrapped(*refs):
            own_in, r_in = refs[:n_in], refs[n_in:n_in + k_in]
            own_out = refs[n_in + k_in:n_in + k_in + n_out]
            r_out = refs[n_in + k_in + n_out:n_in + k_in + n_out + k_out]
            rest = refs[n_in + k_in + n_out + k_out:]
            phases = rider.bind(r_in, r_out, rest[n_scr:])
            step = pl.program_id(0)
            pl.when(step == 0)(phases[0])
            body(*own_in, *own_out, *rest[:n_scr])
            for share, phase in zip(rider.WHEN[1:], phases[1:]):
                at = min(n_steps - 1, max(1, round(share * (n_steps - 1))))
                pl.when(step == at)(phase)

    return pl.pallas_call(
        wrapped, name=name, grid=(n_steps,),
        in_specs=list(in_specs) + [HBM_SPEC] * len(extra_in),
        out_specs=list(out_specs) + [HBM_SPEC] * len(extra_out),
        out_shape=list(out_shape) + list(extra_out), scratch_shapes=scratch + list(extra_scratch),
        compiler_params=pltpu.CompilerParams(dimension_semantics=("arbitrary",), vmem_limit_bytes=VMEM_LIMIT),
    )(*ins, *extra_in)


def _rows_call(body, name, n_rows, row_ins, const_ins, row_outs, const_outs=(), scratch=(),
               reverse=False, tm=TM, rider=None):
    nb = n_rows // tm

    def row_map(i):
        return ((nb - 1 - i) if reverse else i, 0)

    def const_map(nd):
        return lambda i: (0,) * nd

    in_specs = [pl.BlockSpec((tm, a.shape[1]), row_map) for a in row_ins]
    in_specs += [pl.BlockSpec(a.shape, const_map(a.ndim)) for a in const_ins]
    out_specs = [pl.BlockSpec((tm, w), row_map) for (w, _) in row_outs]
    out_specs += [pl.BlockSpec(s, const_map(len(s))) for (s, _) in const_outs]
    out_shape = [jax.ShapeDtypeStruct((n_rows, w), dt) for (w, dt) in row_outs]
    out_shape += [jax.ShapeDtypeStruct(s, dt) for (s, dt) in const_outs]
    return _hosted_call(body, name, nb, list(row_ins) + list(const_ins), in_specs, out_specs, out_shape,
                        scratch, rider)


def _ple_fwd(p_ref, xin, wple_ref, wgate_ref, e_ref, gt_ref):
    pb = p_ref[...].astype(BF16)
    for j in range(N_DEV):
        e_ref[:, j * 128:(j + 1) * 128] = _dot(pb, wple_ref[j])
    gt = _sigmoid(_dot(xin.astype(BF16), wgate_ref[...]))
    gt_ref[...] = gt
    return xin + e_ref[...] * gt


def _layer_a_fwd(x0, p0, a_norm, a_scale, w_in, w_group, w_out, w_ple, w_gate, rider=None):
    s = x0.shape[0]
    tm = TM

    def body(x_ref, p_ref, an_ref, as_ref, win_ref, wg_ref, wout_ref, wple_ref, wgate_ref,
             h_ref, z_ref, pooled_ref, m_ref, y_ref, x1_ref, e_ref, gt_ref, x2_ref, uext):
        i = pl.program_id(0)

        @pl.when(i == 0)
        def _():
            uext[0:HALO, :] = jnp.zeros((HALO, D), F32)

        x = x_ref[...]
        h = (x * _rms(x) * an_ref[...]).astype(BF16)
        h_ref[...] = h
        for j in range(N_DEV):
            uz = _dot(h, win_ref[j])
            if j < 4:
                uext[HALO:HALO + tm, j * 256:(j + 1) * 256] = uz
            else:
                z_ref[:, (j - 4) * 256:(j - 3) * 256] = uz
        t = i * tm + lax.broadcasted_iota(jnp.int32, (tm, 1), 0)
        for g in range(N_GROUPS):
            w = 2 ** (g + 1)
            cols = slice(g * GROUP_DIM, (g + 1) * GROUP_DIM)
            ext = uext[:, cols]
            acc = ext
            k = 1
            while k < w:
                acc = acc + pltpu.roll(acc, k, 0)
                k *= 2
            cnt = jnp.minimum(t + 1, w).astype(F32)
            pooled = (acc[HALO:] / cnt - ext[HALO:]).astype(BF16)
            pooled_ref[:, cols] = pooled
            m_ref[:, cols] = _dot(pooled, wg_ref[g])
        uext[0:HALO, :] = uext[tm:tm + HALO, :]
        z = z_ref[...]
        y = (m_ref[...] * as_ref[...] * (z * _sigmoid(z))).astype(BF16)
        y_ref[...] = y
        x1 = x + _dot(y, wout_ref[...])
        x1_ref[...] = x1
        x2_ref[...] = _ple_fwd(p_ref, x1, wple_ref, wgate_ref, e_ref, gt_ref)

    row_outs = [(D, BF16), (D, F32), (D, BF16), (D, F32), (D, BF16), (D, F32), (D, F32), (D, F32), (D, F32)]
    return _rows_call(body, "layer_a_fwd", s, [x0, p0], [a_norm, a_scale, w_in, w_group, w_out, w_ple, w_gate],
                      row_outs, scratch=[pltpu.VMEM((tm + HALO, D), F32)], rider=rider)


def _layer_b_in_fwd(x2, kv_norm, b_norm, w_kv, w_bin):
    s = x2.shape[0]

    def body(x_ref, kvn_ref, bn_ref, wkv_ref, wbin_ref, hkv_ref, hb_ref, k_ref, v_ref, q_ref, zb_ref):
        x = x_ref[...]
        n = x * _rms(x)
        hkv = (n * kvn_ref[...]).astype(BF16)
        hb = (n * bn_ref[...]).astype(BF16)
        hkv_ref[...] = hkv
        hb_ref[...] = hb
        for j in range(N_DEV):
            kv = _dot(hkv, wkv_ref[j])
            qz = _dot(hb, wbin_ref[j])
            if j < 4:
                cols = slice(j * 256, (j + 1) * 256)
                k_ref[:, cols] = kv
                q_ref[:, cols] = qz
            else:
                cols = slice((j - 4) * 256, (j - 3) * 256)
                v_ref[:, cols] = kv.astype(BF16)
                zb_ref[:, cols] = qz

    row_outs = [(D, BF16), (D, BF16), (D, F32), (D, BF16), (D, F32), (D, F32)]
    return _rows_call(body, "layer_b_in_fwd", s, [x2], [kv_norm, b_norm, w_kv, w_bin], row_outs)


def _tri(after):
    r = lax.broadcasted_iota(jnp.int32, (TILE, TILE), 0)
    c = lax.broadcasted_iota(jnp.int32, (TILE, TILE), 1)
    return jnp.where((r > c) if after else (r < c), 1.0, 0.0).astype(BF16)


def _half_sums(v):
    r = lax.broadcasted_iota(jnp.int32, (LANES, LANES), 0) < HEAD_DIM
    c = lax.broadcasted_iota(jnp.int32, (LANES, LANES), 1) < HEAD_DIM
    same_head = jnp.where(r == c, 1.0, 0.0).astype(BF16)
    return _split_dot(v, same_head)


def _pair_norm(x):
    r = lax.rsqrt(_half_sums(x * x) * (1.0 / HEAD_DIM) + EPS)
    return x * r, r


def _tile_logits(qblk, kblk, diagonal):
    l = _dot_nt(qblk, kblk)
    sp = jnp.maximum(l, 0.0) + jnp.log(1.0 + jnp.exp(-jnp.abs(l)))
    ls = l - sp
    if not diagonal:
        return None, -sp, ls
    mask = lax.broadcasted_iota(jnp.int32, l.shape, 1) < lax.broadcasted_iota(jnp.int32, l.shape, 0)
    return mask, jnp.where(mask, -sp, 0.0), ls


def _attn_fwd(q_all, k_all, v_all, q_gain2, k_gain2, rider=None):
    s = q_all.shape[0]
    nt = s // TILE

    def body(q_ref, k_ref, v_ref, qg_ref, kg_ref, o_ref, c_ref, qs, ks, vs, tri, acc, right, cmat):
        tri[...] = _tri(True)
        lane = lax.broadcasted_iota(jnp.int32, (TILE, LANES), 1)
        qn, _ = _pair_norm(q_ref[...])
        kn, _ = _pair_norm(k_ref[...])
        qsc = (qn * qg_ref[...] * SB_SCALE).astype(BF16)
        ksc = (kn * kg_ref[...]).astype(BF16)
        for hh in range(2):
            sl = slice(hh * HEAD_DIM, (hh + 1) * HEAD_DIM)
            qs[hh] = qsc[:, sl]
            ks[hh] = ksc[:, sl]
            vs[hh] = v_ref[:, sl]

        def tile(qrows, kb, diagonal):
            rows = pl.ds(pl.multiple_of(kb * TILE, TILE), TILE)
            loaded = [(qs[hh, qrows, :], ks[hh, rows, :], vs[hh, rows, :], right[hh], cmat[hh], acc[hh])
                      for hh in range(2)]
            logits = [_tile_logits(q, k, diagonal) for q, k, _, _, _, _ in loaded]
            later = _split_dot_many([lk for _, lk, _ in logits], tri[...])
            results = []
            for (q, k, v, rt, cm, ac), (mask, lk, ls), lt in zip(loaded, logits, later):
                a = jnp.exp(ls + lt + rt)
                if diagonal:
                    a = jnp.where(mask, a, 0.0)
                results.append((ac + _dot(a.astype(BF16), v), jnp.where(lane == kb, rt[:, :LANES], cm),
                                rt + jnp.sum(lk, axis=1, keepdims=True)))
            for hh, (ac, cm, rt) in enumerate(results):
                acc[hh] = ac
                cmat[hh] = cm
                right[hh] = rt

        def q_step(qb, _):
            r0 = pl.multiple_of(qb * TILE, TILE)
            qrows = pl.ds(r0, TILE)
            acc[...] = jnp.zeros((2, TILE, HEAD_DIM), F32)
            right[...] = jnp.zeros((2, TILE, TILE), F32)
            cmat[...] = jnp.zeros((2, TILE, LANES), F32)
            tile(qrows, qb, True)

            def live():
                return (jnp.max(right[:, :, :LANES]) > DEAD_LOG).astype(jnp.int32)

            def k_step(c):
                kb = c[0] - 1
                tile(qrows, kb, False)
                return kb, live()

            first, _ = lax.while_loop(lambda c: (c[0] > 0) & (c[1] > 0), k_step, (qb, live()))
            for hh in range(2):
                o_ref[qrows, hh * HEAD_DIM:(hh + 1) * HEAD_DIM] = acc[hh]
                c_ref[hh, qrows, :] = jnp.where(lane == LANES - 1, first.astype(F32), cmat[hh])
            return 0

        lax.fori_loop(0, nt, q_step, 0)

    pair = pl.BlockSpec((s, LANES), lambda h: (0, h))
    gain = pl.BlockSpec((1, LANES), lambda h: (0, 0))
    return _hosted_call(
        body, "attn_fwd", HEADS // 2, [q_all, k_all, v_all, q_gain2, k_gain2],
        [pair, pair, pair, gain, gain], [pair, pl.BlockSpec((2, s, LANES), lambda h: (h, 0, 0))],
        [jax.ShapeDtypeStruct((s, D), F32), jax.ShapeDtypeStruct((HEADS, s, LANES), F32)],
        [pltpu.VMEM((2, s, HEAD_DIM), BF16)] * 3
        + [pltpu.VMEM((TILE, TILE), BF16), pltpu.VMEM((2, TILE, HEAD_DIM), F32), pltpu.VMEM((2, TILE, TILE), F32),
           pltpu.VMEM((2, TILE, LANES), F32)], rider)


def _layer_b_out_fwd(o, zb, x2, p1, target, w_out, w_ple, w_gate):
    s = o.shape[0]

    def body(o_ref, zb_ref, x2_ref, p_ref, t_ref, wout_ref, wple_ref, wgate_ref,
             yb_ref, x3_ref, e_ref, gt_ref, dx4_ref, loss_ref):
        zb = zb_ref[...]
        yb = (o_ref[...] * (zb * _sigmoid(zb))).astype(BF16)
        yb_ref[...] = yb
        x3 = x2_ref[...] + _dot(yb, wout_ref[...])
        x3_ref[...] = x3
        x4 = _ple_fwd(p_ref, x3, wple_ref, wgate_ref, e_ref, gt_ref)
        d = x4 - t_ref[...]
        dx4_ref[...] = d * (1.0 / D)

        @pl.when(pl.program_id(0) == 0)
        def _():
            loss_ref[...] = jnp.zeros((1, D), F32)

        loss_ref[...] += jnp.sum(d * d, axis=0, keepdims=True)

    row_outs = [(D, BF16), (D, F32), (D, F32), (D, F32), (D, F32)]
    return _rows_call(body, "layer_b_out_fwd", s, [o, zb, x2, p1, target], [w_out, w_ple, w_gate], row_outs,
                      const_outs=[((1, D), F32)])


def _ple_bwd(dxo, e_ref, gt_ref, wgate_ref, de_ref, dgp_ref):
    e = e_ref[...]
    gt = gt_ref[...]
    de_ref[...] = (dxo * gt).astype(BF16)
    dgp = (dxo * e * gt * (1.0 - gt)).astype(BF16)
    dgp_ref[...] = dgp
    return dxo + _dot_nt(dgp, wgate_ref[...])


def _silu_grads(z):
    sg = _sigmoid(z)
    return z * sg, sg * (1.0 + z * (1.0 - sg))


def _layer_b_out_bwd(dx4, e1, gt1, o, zb, w_gate, w_out):
    s = dx4.shape[0]

    def body(dx4_ref, e_ref, gt_ref, o_ref, zb_ref, wgate_ref, wout_ref,
             de_ref, dgp_ref, dx3_ref, do_ref, dzb_ref):
        dx3 = _ple_bwd(dx4_ref[...], e_ref, gt_ref, wgate_ref, de_ref, dgp_ref)
        dx3_ref[...] = dx3
        dyb = _dot_nt(dx3.astype(BF16), wout_ref[...])
        silu, dsilu = _silu_grads(zb_ref[...])
        do_ref[...] = (dyb * silu).astype(BF16)
        dzb_ref[...] = (dyb * o_ref[...] * dsilu).astype(BF16)

    row_outs = [(D, BF16), (D, BF16), (D, F32), (D, BF16), (D, BF16)]
    return _rows_call(body, "layer_b_out_bwd", s, [dx4, e1, gt1, o, zb], [w_gate, w_out], row_outs)


def _attn_bwd(q_all, k_all, v_all, q_gain2, k_gain2, d_o, csave, rider=None):
    s = q_all.shape[0]
    nt = s // TILE

    def body(q_ref, k_ref, v_ref, qg_ref, kg_ref, do_ref, c_ref,
             dq_ref, dk_ref, dv_ref, dqg_ref, dkg_ref,
             qs, ks, vs, dos, qt, dot_t, tri_a, tri_b, dqa, dkt, dvt, dqb, left):
        tri_a[...] = _tri(True)
        tri_b[...] = _tri(False)
        lane = lax.broadcasted_iota(jnp.int32, (TILE, LANES), 1)
        qn, qr = _pair_norm(q_ref[...])
        kn, kr = _pair_norm(k_ref[...])
        qsc = qn * qg_ref[...] * SB_SCALE
        ksc = (kn * kg_ref[...]).astype(BF16)
        q_t = qsc.T.astype(BF16)
        do_t = do_ref[...].astype(F32).T.astype(BF16)
        for j in range(nt):
            qt[j] = q_t[:, j * TILE:(j + 1) * TILE]
            dot_t[j] = do_t[:, j * TILE:(j + 1) * TILE]
        dkt[...] = jnp.zeros((nt, LANES, TILE), F32)
        dvt[...] = jnp.zeros((nt, LANES, TILE), F32)
        qsc = qsc.astype(BF16)
        for hh in range(2):
            sl = slice(hh * HEAD_DIM, (hh + 1) * HEAD_DIM)
            qs[hh] = qsc[:, sl]
            ks[hh] = ksc[:, sl]
            vs[hh] = v_ref[:, sl]
            dos[hh] = do_ref[:, sl]

        def tile(qb, qrows, kb, diagonal):
            rows = pl.ds(pl.multiple_of(kb * TILE, TILE), TILE)
            heads = range(2)
            kblk = [ks[hh, rows, :] for hh in heads]
            logits = [_tile_logits(qs[hh, qrows, :], kblk[hh], diagonal) for hh in heads]
            later = _split_dot_many([lk for _, lk, _ in logits], tri_a[...])
            a, g = [], []
            for hh in heads:
                mask, _, ls = logits[hh]
                right = jnp.sum(jnp.where(lane == kb, c_ref[hh, qrows, :], 0.0), axis=1, keepdims=True)
                a_h = jnp.exp(ls + later[hh] + right)
                a.append(jnp.where(mask, a_h, 0.0) if diagonal else a_h)
                g.append(a[hh] * _dot_nt(dos[hh, qrows, :], vs[hh, rows, :]))
            before = _split_dot_many(g, tri_b[...])
            for hh in heads:
                sl = slice(hh * HEAD_DIM, (hh + 1) * HEAD_DIM)
                mask, _, ls = logits[hh]
                beta = jnp.exp(ls)
                lf = left[hh]
                dl = g[hh] * (1.0 - beta) - (before[hh] + lf) * beta
                if diagonal:
                    dl = jnp.where(mask, dl, 0.0)
                dl = dl.astype(BF16)
                left[hh] = lf + jnp.sum(g[hh], axis=1, keepdims=True)
                dqb[hh] += _dot(dl, kblk[hh])
                dkt[kb, sl, :] += _dot(qt[qb, sl, :], dl)
                dvt[kb, sl, :] += _dot(dot_t[qb, sl, :], a[hh].astype(BF16))

        def q_step(qb, _):
            qrows = pl.ds(pl.multiple_of(qb * TILE, TILE), TILE)
            dqb[...] = jnp.zeros((2, TILE, HEAD_DIM), F32)
            left[...] = jnp.zeros((2, TILE, TILE), F32)

            def k_step(kb, _):
                tile(qb, qrows, kb, False)
                return 0

            first = jnp.max(jnp.where(lane == LANES - 1, c_ref[0, qrows, :], 0.0)).astype(jnp.int32)
            lax.fori_loop(first, qb, k_step, 0)
            tile(qb, qrows, qb, True)
            for hh in range(2):
                dqa[qrows, hh * HEAD_DIM:(hh + 1) * HEAD_DIM] = dqb[hh] * SB_SCALE
            return 0

        lax.fori_loop(0, nt, q_step, 0)

        def norm_bwd(dy, xn, r, g_ref, dx_ref, dg_ref):
            dg_ref[...] = jnp.sum(dy * xn, axis=0, keepdims=True)
            dxn = dy * g_ref[...]
            dx_ref[...] = r * (dxn - xn * (_half_sums(dxn * xn) * (1.0 / HEAD_DIM)))

        norm_bwd(dqa[...], qn, qr, qg_ref, dq_ref, dqg_ref)
        for j in range(nt):
            dqa[j * TILE:(j + 1) * TILE, :] = dkt[j].T
            dv_ref[j * TILE:(j + 1) * TILE, :] = dvt[j].T
        norm_bwd(dqa[...], kn, kr, kg_ref, dk_ref, dkg_ref)

    pair = pl.BlockSpec((s, LANES), lambda h: (0, h))
    gain = pl.BlockSpec((1, LANES), lambda h: (0, 0))
    dgain = pl.BlockSpec((None, 1, LANES), lambda h: (h, 0, 0))
    return _hosted_call(
        body, "attn_bwd", HEADS // 2, [q_all, k_all, v_all, q_gain2, k_gain2, d_o, csave],
        [pair, pair, pair, gain, gain, pair, pl.BlockSpec((2, s, LANES), lambda h: (h, 0, 0))],
        [pair, pair, pair, dgain, dgain],
        [jax.ShapeDtypeStruct((s, D), F32)] * 3 + [jax.ShapeDtypeStruct((HEADS // 2, 1, LANES), F32)] * 2,
        [pltpu.VMEM((2, s, HEAD_DIM), BF16)] * 4
        + [pltpu.VMEM((nt, LANES, TILE), BF16)] * 2 + [pltpu.VMEM((TILE, TILE), BF16)] * 2
        + [pltpu.VMEM((s, LANES), F32)] + [pltpu.VMEM((nt, LANES, TILE), F32)] * 2
        + [pltpu.VMEM((2, TILE, HEAD_DIM), F32), pltpu.VMEM((2, TILE, TILE), F32)], rider)


def _norm_bwd_rows(dh, x, gain, dgain_ref):
    r = _rms(x)
    n = x * r
    dgain_ref[...] += jnp.sum(dh * n, axis=0, keepdims=True)
    dn = dh * gain
    return r * (dn - n * jnp.mean(dn * n, axis=-1, keepdims=True))


def _layer_b_in_bwd(dq, dzb, dk, dv, x2, dx3, w_bin, w_kv, b_norm, kv_norm):
    s = x2.shape[0]

    def body(dq_ref, dzb_ref, dk_ref, dv_ref, x_ref, dx3_ref, wbin_ref, wkv_ref, bn_ref, kvn_ref,
             dqz_ref, dkv_ref, dx2_ref, dbn_ref, dkvn_ref):
        @pl.when(pl.program_id(0) == 0)
        def _():
            dbn_ref[...] = jnp.zeros((1, D), F32)
            dkvn_ref[...] = jnp.zeros((1, D), F32)

        dqz_ref[:, :D] = dq_ref[...].astype(BF16)
        dqz_ref[:, D:] = dzb_ref[...]
        dkv_ref[:, :D] = dk_ref[...].astype(BF16)
        dkv_ref[:, D:] = dv_ref[...].astype(BF16)
        dhb = jnp.zeros((TM, D), F32)
        dhkv = jnp.zeros((TM, D), F32)
        for j in range(N_DEV):
            cols = slice(j * 256, (j + 1) * 256)
            dhb = dhb + _dot_nt(dqz_ref[:, cols], wbin_ref[j])
            dhkv = dhkv + _dot_nt(dkv_ref[:, cols], wkv_ref[j])
        x = x_ref[...]
        dx2 = dx3_ref[...] + _norm_bwd_rows(dhb, x, bn_ref[...], dbn_ref)
        dx2_ref[...] = dx2 + _norm_bwd_rows(dhkv, x, kvn_ref[...], dkvn_ref)

    row_outs = [(2 * D, BF16), (2 * D, BF16), (D, F32)]
    return _rows_call(body, "layer_b_in_bwd", s, [dq, dzb, dk, dv, x2, dx3], [w_bin, w_kv, b_norm, kv_norm],
                      row_outs, const_outs=[((1, D), F32), ((1, D), F32)])


def _layer_a_out_bwd(dx2, e0, gt0, z, m, w_gate, w_out, a_scale, w_group, rider=None):
    s = dx2.shape[0]
    tm = TM
    nb = s // tm

    def body(dx2_ref, e_ref, gt_ref, z_ref, m_ref, wgate_ref, wout_ref, as_ref, wg_ref,
             de_ref, dgp_ref, dx1_ref, dm_ref, duz_ref, das_ref, ext):
        i = pl.program_id(0)

        @pl.when(i == 0)
        def _():
            das_ref[...] = jnp.zeros((1, D), F32)
            ext[tm:tm + HALO, :] = jnp.zeros((HALO, D), F32)

        dx1 = _ple_bwd(dx2_ref[...], e_ref, gt_ref, wgate_ref, de_ref, dgp_ref)
        dx1_ref[...] = dx1
        dy = _dot_nt(dx1.astype(BF16), wout_ref[...])
        silu, dsilu = _silu_grads(z_ref[...])
        m = m_ref[...]
        dmixed = dy * silu
        duz_ref[:, D:] = (dy * (m * as_ref[...]) * dsilu).astype(BF16)
        das_ref[...] += jnp.sum(dmixed * m, axis=0, keepdims=True)
        dm_ref[...] = (dmixed * as_ref[...]).astype(BF16)
        t = (nb - 1 - i) * tm + lax.broadcasted_iota(jnp.int32, (tm, 1), 0)
        n_ext = tm + HALO
        for g in range(N_GROUPS):
            w = 2 ** (g + 1)
            cols = slice(g * GROUP_DIM, (g + 1) * GROUP_DIM)
            dpool = _dot_nt(dm_ref[:, cols], wg_ref[g])
            ext[0:tm, cols] = dpool / jnp.minimum(t + 1, w).astype(F32)
            acc = ext[:, cols]
            k = 1
            while k < w:
                acc = acc + pltpu.roll(acc, n_ext - k, 0)
                k *= 2
            duz_ref[:, cols] = (acc[:tm] - dpool).astype(BF16)
        ext[tm:tm + HALO, :] = ext[0:HALO, :]

    row_outs = [(D, BF16), (D, BF16), (D, F32), (D, BF16), (2 * D, BF16)]
    return _rows_call(body, "layer_a_out_bwd", s, [dx2, e0, gt0, z, m], [w_gate, w_out, a_scale, w_group],
                      row_outs, const_outs=[((1, D), F32)], scratch=[pltpu.VMEM((tm + HALO, D), F32)],
                      reverse=True, rider=rider)


def _layer_a_in_bwd(duz, x0, dx1, w_in, a_norm, rider=None):
    s = x0.shape[0]

    def body(duz_ref, x_ref, dx1_ref, win_ref, an_ref, dx0_ref, dan_ref):
        @pl.when(pl.program_id(0) == 0)
        def _():
            dan_ref[...] = jnp.zeros((1, D), F32)

        dh = jnp.zeros((TM, D), F32)
        for j in range(N_DEV):
            dh = dh + _dot_nt(duz_ref[:, j * 256:(j + 1) * 256], win_ref[j])
        dx0_ref[...] = dx1_ref[...] + _norm_bwd_rows(dh, x_ref[...], an_ref[...], dan_ref)

    return _rows_call(body, "layer_a_in_bwd", s, [duz, x0, dx1], [w_in, a_norm], [(D, F32)],
                      const_outs=[((1, D), F32)], rider=rider)


def _wgrad(a, b, name, n_split=1, a_blocked_b=False, rider=None):
    s, k = a.shape
    n = b.shape[1]
    tk = 256
    nb = n // n_split

    def body(a_ref, b_ref, o_ref):
        res = _dot_tn(a_ref[...].astype(BF16), b_ref[...].astype(BF16))
        if n_split == 1:
            o_ref[...] = res.astype(BF16)
        else:
            for j in range(n_split):
                o_ref[j] = res[:, j * nb:(j + 1) * nb].astype(BF16)

    if a_blocked_b:
        b_spec = pl.BlockSpec((s, tk), lambda i: (0, i))
        out_spec = pl.BlockSpec((None, tk, tk), lambda i: (i, 0, 0))
        out_shape = jax.ShapeDtypeStruct((k // tk, tk, tk), BF16)
    elif n_split == 1:
        b_spec = pl.BlockSpec((s, n), lambda i: (0, 0))
        out_spec = pl.BlockSpec((tk, n), lambda i: (i, 0))
        out_shape = jax.ShapeDtypeStruct((k, n), BF16)
    else:
        b_spec = pl.BlockSpec((s, n), lambda i: (0, 0))
        out_spec = pl.BlockSpec((n_split, tk, nb), lambda i: (0, i, 0))
        out_shape = jax.ShapeDtypeStruct((n_split, k, nb), BF16)
    res = _hosted_call(body, name, k // tk, [a, b], [pl.BlockSpec((s, tk), lambda i: (0, i)), b_spec],
                       [out_spec], [out_shape], [], rider)
    return res[0] if rider is None else res


def _cast_shards(shards):
    n = len(shards)

    def body(*refs):
        for a in range(n):
            refs[n + a][...] = refs[a][...].astype(BF16)

    vmem = pl.BlockSpec(memory_space=pltpu.VMEM)
    return pl.pallas_call(
        body, name="cast_shards", in_specs=[vmem] * n, out_specs=[vmem] * n,
        out_shape=[jax.ShapeDtypeStruct(a.shape, BF16) for a in shards],
        compiler_params=pltpu.CompilerParams(vmem_limit_bytes=VMEM_LIMIT),
    )(*shards)


def _adamw(w, g, m, v):
    m = ADAM_B1 * m + (1.0 - ADAM_B1) * g
    v = ADAM_B2 * v + (1.0 - ADAM_B2) * jnp.square(g)
    m_hat = m / (1.0 - ADAM_B1 ** ADAM_STEP)
    v_hat = v / (1.0 - ADAM_B2 ** ADAM_STEP)
    delta = -ADAM_LR * (m_hat / (jnp.sqrt(v_hat) + ADAM_EPS) + ADAM_WD * w)
    return delta, m, v


def _adamw_small(w, g, m, v, name):
    def body(w_ref, g_ref, m_ref, v_ref, d_ref, m2_ref, v2_ref):
        d_ref[...], m2_ref[...], v2_ref[...] = _adamw(w_ref[...], g_ref[...], m_ref[...], v_ref[...])

    vmem = pl.BlockSpec(memory_space=pltpu.VMEM)
    return pl.pallas_call(
        body, name=name, in_specs=[vmem] * 4, out_specs=[vmem] * 3,
        out_shape=[jax.ShapeDtypeStruct(w.shape, F32)] * 3,
    )(w, g, m, v)


def _place():
    return lax.axis_index("x"), lax.axis_index("y"), lax.axis_index("c")


def _all_gather(shards):
    return _alone("all_gather_weights", _GatherRider(shards))


def _alone(name, rider):
    n_in, n_out = len(rider.arrays), len(rider.out_shape())

    def body(*refs):
        for phase in rider.bind(refs[:n_in], refs[n_in:n_in + n_out], refs[n_in + n_out:]):
            phase()

    return pl.pallas_call(
        body, name=name, in_specs=[HBM_SPEC] * n_in, out_specs=[HBM_SPEC] * n_out,
        out_shape=rider.out_shape(), scratch_shapes=rider.scratch(),
        compiler_params=pltpu.CompilerParams(vmem_limit_bytes=VMEM_LIMIT),
    )(*rider.arrays)


class _GatherRider:
    WHEN = (0.0, 0.7, 1.0)

    def __init__(self, shards):
        self.arrays = list(shards)

    def out_shape(self):
        return _Gather.out_shape(self.arrays)

    def scratch(self):
        return _Gather.semaphores(len(self.arrays))

    def bind(self, ins, outs, scratch):
        moving = _Gather(ins, outs, *scratch)
        return moving.start, moving.forward, moving.finish


class _ReduceRider:
    WHEN = (0.0, 0.15, 0.5, 1.0)

    def __init__(self, partials):
        self.arrays = list(partials)

    def out_shape(self):
        return [jax.ShapeDtypeStruct(a.shape[1:], F32) for a in self.arrays]

    def scratch(self):
        n = len(self.arrays)
        dma = pltpu.SemaphoreType.DMA

        def blocks(k):
            return [pltpu.VMEM((k,) + a.shape[1:], BF16) for a in self.arrays]

        halves = [pltpu.VMEM((2, a.shape[1] // 2) + a.shape[2:], BF16) for a in self.arrays]
        return (blocks(4) + blocks(4) + halves + blocks(2) + [pltpu.VMEM(a.shape[1:], F32) for a in self.arrays]
                + [dma((4 * n,)), dma((4 * n,)), dma((4 * n,)), dma((2 * n,)), dma((2 * n,)),
                   dma((2 * n,)), dma((2 * n,)), dma((n,))])

    def bind(self, ins, outs, scratch):
        n = len(ins)
        mine, landed, halves, arrived, total = (scratch[i * n:(i + 1) * n] for i in range(5))
        send1, recv1, local1, send_h, recv_h, send2, recv2, out_sems = scratch[5 * n:]
        x, y, c = _place()
        plane = 2 * x + y
        via = [(x, 1 - y, c), (1 - x, y, c)]
        nbr = [(1 - x, y, c), (x, 1 - y, c)]
        nbr_block = [2 * (1 - x) + y, 2 * x + (1 - y)]
        diag_block = 2 * (1 - x) + (1 - y)

        def to_sibling(a, k):
            return pltpu.make_async_remote_copy(
                src_ref=ins[a].at[2 * k + (1 - c)], dst_ref=landed[a].at[k],
                send_sem=send1.at[4 * a + k], recv_sem=recv1.at[4 * a + k],
                device_id=(x, y, 1 - c), device_id_type=MESH)

        def own_block(a, k):
            return pltpu.make_async_copy(ins[a].at[2 * k + c], mine[a].at[k], local1.at[4 * a + k])

        def half_of(a, ref, h):
            rows = self.arrays[a].shape[1] // 2
            return ref.at[pl.ds(h * rows, rows)]

        def half_out(a, h):
            return pltpu.make_async_remote_copy(
                src_ref=half_of(a, mine[a].at[diag_block], h), dst_ref=halves[a].at[h],
                send_sem=send_h.at[2 * a + h], recv_sem=recv_h.at[2 * a + h],
                device_id=via[h], device_id_type=MESH)

        def to_owner(a, h):
            return pltpu.make_async_remote_copy(
                src_ref=mine[a].at[nbr_block[h]], dst_ref=arrived[a].at[h],
                send_sem=send2.at[2 * a + h], recv_sem=recv2.at[2 * a + h],
                device_id=nbr[h], device_id_type=MESH)

        def result(a):
            return pltpu.make_async_copy(total[a], outs[a], out_sems.at[a])

        def exchange_cores():
            for a in range(n):
                for k in range(4):
                    to_sibling(a, k).start()
                    own_block(a, k).start()

        def pair_sums():
            for a in range(n):
                for k in range(4):
                    own_block(a, k).wait()
                    to_sibling(a, k).wait_recv()
                total[a][...] = mine[a][plane].astype(F32) + landed[a][plane].astype(F32)
                for k in range(4):
                    mine[a][k] = (mine[a][k].astype(F32) + landed[a][k].astype(F32)).astype(BF16)
                for h in range(2):
                    half_out(a, h).start()

        def fold_and_send():
            for a in range(n):
                rows = self.arrays[a].shape[1] // 2
                for h in range(2):
                    half_out(a, h).wait_recv()
                    part = mine[a].at[nbr_block[h]]
                    span = slice(h * rows, (h + 1) * rows)
                    part[span] = (part[span].astype(F32) + halves[a][h].astype(F32)).astype(BF16)
                    to_owner(a, h).start()

        def finish():
            for a in range(n):
                for h in range(2):
                    to_owner(a, h).wait_recv()
                    total[a][...] += arrived[a][h].astype(F32)
                result(a).start()
            for a in range(n):
                for k in range(4):
                    to_sibling(a, k).wait_send()
                for h in range(2):
                    half_out(a, h).wait_send()
                    to_owner(a, h).wait_send()
                result(a).wait()

        return exchange_cores, pair_sums, fold_and_send, finish


class _Gather:
    COPIES = 9

    def __init__(self, ins, outs, send_sems, recv_sems, local_sems):
        self.ins, self.outs = ins, outs
        self.send_sems, self.recv_sems, self.local_sems = send_sems, recv_sems, local_sems
        self.x, self.y, self.c = _place()

    @staticmethod
    def out_shape(shards):
        return [jax.ShapeDtypeStruct((N_DEV,) + a.shape, a.dtype) for a in shards]

    @staticmethod
    def semaphores(n):
        dma = pltpu.SemaphoreType.DMA
        return [dma((_Gather.COPIES * n,)), dma((_Gather.COPIES * n,)), dma((n,))]

    def _copy(self, a, k, block, to, own=False, half=None):
        px, py, pc = block
        slot = self.outs[a].at[4 * px + 2 * py + pc]
        if half is not None:
            rows = slot.shape[0] // 2
            slot = slot.at[pl.ds(half * rows, rows)]
        return pltpu.make_async_remote_copy(
            src_ref=self.ins[a] if own else slot, dst_ref=slot,
            send_sem=self.send_sems.at[self.COPIES * a + k], recv_sem=self.recv_sems.at[self.COPIES * a + k],
            device_id=to, device_id_type=MESH)

    def _local(self, a):
        return pltpu.make_async_copy(self.ins[a], self.outs[a].at[4 * self.x + 2 * self.y + self.c],
                                     self.local_sems.at[a])

    def _plan(self, a, c):
        x, y = self.x, self.y
        me, sibling = (x, y, c), (x, y, 1 - c)
        xn, yn, dg = (1 - x, y, c), (x, 1 - y, c), (1 - x, 1 - y, c)
        return [
            self._copy(a, 0, me, sibling, own=True), self._copy(a, 1, me, xn, own=True),
            self._copy(a, 2, me, yn, own=True),
            self._copy(a, 3, xn, yn, half=0), self._copy(a, 4, yn, xn, half=1),
            self._copy(a, 5, xn, sibling), self._copy(a, 6, yn, sibling),
            self._copy(a, 7, dg, sibling, half=0), self._copy(a, 8, dg, sibling, half=1),
        ]

    def _arrivals(self, a):
        x, y, c = self.x, self.y, self.c
        me = (x, y, c)
        xn, yn, dg = (1 - x, y, c), (x, 1 - y, c), (1 - x, 1 - y, c)
        other = 1 - c
        return [
            self._copy(a, 0, (x, y, other), me), self._copy(a, 1, xn, me), self._copy(a, 2, yn, me),
            self._copy(a, 3, dg, me, half=0), self._copy(a, 4, dg, me, half=1),
            self._copy(a, 5, (1 - x, y, other), me), self._copy(a, 6, (x, 1 - y, other), me),
            self._copy(a, 7, (1 - x, 1 - y, other), me, half=0), self._copy(a, 8, (1 - x, 1 - y, other), me, half=1),
        ]

    def start(self):
        for a in range(len(self.ins)):
            self._local(a).start()
            for cp in self._plan(a, self.c)[:3]:
                cp.start()

    def forward(self):
        for a in range(len(self.ins)):
            sends, lands = self._plan(a, self.c), self._arrivals(a)
            lands[1].wait_recv()
            sends[3].start()
            sends[5].start()
            lands[2].wait_recv()
            sends[4].start()
            sends[6].start()

    def finish(self):
        n = len(self.ins)
        for a in range(n):
            sends, lands = self._plan(a, self.c), self._arrivals(a)
            lands[3].wait_recv()
            sends[7].start()
            lands[4].wait_recv()
            sends[8].start()
        for a in range(n):
            lands = self._arrivals(a)
            for k in (0, 5, 6, 7, 8):
                lands[k].wait_recv()
        for a in range(n):
            for cp in self._plan(a, self.c):
                cp.wait_send()
            self._local(a).wait()


def _adamw_shards(ws, gs, ms, vs):
    n = len(ws)

    def body(*refs):
        w, g, m, v = refs[:n], refs[n:2 * n], refs[2 * n:3 * n], refs[3 * n:4 * n]
        outs = refs[4 * n:]
        for a in range(n):
            outs[a][...], outs[n + a][...], outs[2 * n + a][...] = _adamw(w[a][...], g[a][...], m[a][...], v[a][...])

    vmem = pl.BlockSpec(memory_space=pltpu.VMEM)
    res = pl.pallas_call(
        body, name="adamw_shards", in_specs=[vmem] * (4 * n), out_specs=[vmem] * (3 * n),
        out_shape=[jax.ShapeDtypeStruct(a.shape, F32) for a in ws] * 3,
        compiler_params=pltpu.CompilerParams(vmem_limit_bytes=VMEM_LIMIT),
    )(*ws, *gs, *ms, *vs)
    return res[:n], res[n:2 * n], res[2 * n:]


def _all_reduce_small(rows, gain_parts):
    def body(rows_ref, dqg_ref, dkg_ref, out_ref, buf, send_sems, recv_sems):
        x, y, c = _place()
        me = 4 * x + 2 * y + c
        buf[0] = rows_ref[...]
        for row, part in ((4, dqg_ref), (5, dkg_ref)):
            both = jnp.sum(part[...].reshape(HEADS // 2, LANES), axis=0, keepdims=True)
            buf[0, row:row + 1, 0:HEAD_DIM] = both[:, :HEAD_DIM] + both[:, HEAD_DIM:]
        copies = []
        for r in range(1, N_DEV):
            bx, by, bc = (r >> 2) & 1, (r >> 1) & 1, r & 1
            to = (x ^ bx, y ^ by, c ^ bc)
            copies.append(pltpu.make_async_remote_copy(
                src_ref=buf.at[0], dst_ref=buf.at[r], send_sem=send_sems.at[r - 1], recv_sem=recv_sems.at[r - 1],
                device_id=to, device_id_type=MESH))
        for cp in copies:
            cp.start()
        for cp in copies:
            cp.wait_recv()
        for cp in copies:
            cp.wait_send()
        tot = buf[me]
        for j in range(1, N_DEV):
            tot = tot + buf[j ^ me]
        out_ref[...] = tot
        loss = (0.5 / D) * jnp.sum(tot[6:7, :], axis=1, keepdims=True)
        out_ref[6:7, :] = jnp.broadcast_to(loss, (1, D))

    vmem = pl.BlockSpec(memory_space=pltpu.VMEM)
    return pl.pallas_call(
        body, name="all_reduce_small", in_specs=[vmem] * 3, out_specs=vmem,
        out_shape=jax.ShapeDtypeStruct((8, D), F32),
        scratch_shapes=[pltpu.VMEM((N_DEV, 8, D), F32), pltpu.SemaphoreType.DMA((N_DEV - 1,)),
                        pltpu.SemaphoreType.DMA((N_DEV - 1,))],
    )(rows, *gain_parts)


def kernel(x, p, a_norm, a_w_in, a_w_group, a_scale, a_w_out, kv_norm, w_kv, k_norm, b_norm, b_w_in, b_q_norm, b_w_out, ple_w, ple_gate_w, loss_target, m_a_norm, m_a_w_in, m_a_w_group, m_a_scale, m_a_w_out, m_kv_norm, m_w_kv, m_k_norm, m_b_norm, m_b_w_in, m_b_q_norm, m_b_w_out, m_ple_w, m_ple_gate_w, v_a_norm, v_a_w_in, v_a_w_group, v_a_scale, v_a_w_out, v_kv_norm, v_w_kv, v_k_norm, v_b_norm, v_b_w_in, v_b_q_norm, v_b_w_out, v_ple_w, v_ple_gate_w):
    s = x.shape[1]
    xi, yi, ci = _place()
    me = 4 * xi + 2 * yi + ci
    plane = 2 * xi + yi

    big = {
        "a_w_in": a_w_in.reshape(D, 256), "a_w_group": a_w_group.reshape(128, 256),
        "a_w_out": a_w_out.reshape(128, D), "w_kv": w_kv, "b_w_in": b_w_in.reshape(D, 256),
        "b_w_out": b_w_out.reshape(128, D), "ple_w0": ple_w[0], "ple_w1": ple_w[1],
        "gate0": ple_gate_w[0], "gate1": ple_gate_w[1],
    }
    names = list(big)
    cast = dict(zip(names, _cast_shards([big[k] for k in names])))
    small = jnp.concatenate([a_norm, a_scale, jnp.zeros((14, 128), F32)], axis=0)
    first = ["a_w_in", "a_w_group", "a_w_out", "ple_w0", "gate0"]
    behind_a = ["w_kv", "b_w_in"]
    behind_attn = ["b_w_out", "ple_w1", "gate1"]
    gathered = _all_gather([cast[k] for k in first] + [small])
    full = dict(zip(first, gathered[:-1]))
    small_all = gathered[-1]
    a_norm_f = small_all[:, 0, :].reshape(1, D)
    a_scale_f = small_all[:, 1, :].reshape(1, D)
    w_a_in = full["a_w_in"]
    w_a_out = full["a_w_out"].reshape(D, D)
    w_gate0 = full["gate0"].reshape(D, D)
    w_ple0 = full["ple_w0"]
    w_group = full["a_w_group"].reshape(N_DEV, 4, 32, 256).transpose(1, 0, 2, 3).reshape(4, 256, 256)
    kvn, bn = kv_norm.reshape(1, D), b_norm
    kg, qg = k_norm.reshape(1, HEAD_DIM), b_q_norm

    x0, p0, p1, target = x[0], p[0, 0], p[1, 0], loss_target[0]
    h0, z, pooled, mcat, y, x1, e0, gt0, x2, w_kv_f, w_b_in = _layer_a_fwd(
        x0, p0, a_norm_f, a_scale_f, w_a_in, w_group, w_a_out, w_ple0, w_gate0,
        rider=_GatherRider([cast[k] for k in behind_a]))
    hkv, hb, k_all, v_all, q_all, zb = _layer_b_in_fwd(x2, kvn, bn, w_kv_f, w_b_in)
    qg2, kg2 = jnp.concatenate([qg, qg], axis=1), jnp.concatenate([kg, kg], axis=1)
    o, csave, w_b_out, w_ple1, w_gate1 = _attn_fwd(
        q_all, k_all, v_all, qg2, kg2, rider=_GatherRider([cast[k] for k in behind_attn]))
    w_b_out, w_gate1 = w_b_out.reshape(D, D), w_gate1.reshape(D, D)
    yb, x3, e1, gt1, dx4, sq_err = _layer_b_out_fwd(o, zb, x2, p1, target, w_b_out, w_ple1, w_gate1)

    de1, dgp1, dx3, d_o, dzb = _layer_b_out_bwd(dx4, e1, gt1, o, zb, w_gate1, w_b_out)
    partial = {
        "b_w_out": _wgrad(yb, dx3, "wgrad_b_w_out").reshape(N_DEV, 128, D),
        "ple_w1": _wgrad(p1, de1, "wgrad_ple_w1", n_split=8),
        "gate1": _wgrad(x3, dgp1, "wgrad_gate1").reshape(N_DEV, 128, D),
    }
    grad = {}
    dq, dk, dv, dqg, dkg, grad["b_w_out"], grad["ple_w1"], grad["gate1"] = _attn_bwd(
        q_all, k_all, v_all, qg2, kg2, d_o, csave,
        rider=_ReduceRider([partial[k] for k in ("b_w_out", "ple_w1", "gate1")]))
    dqz, dkv, dx2, d_bn, d_kvn = _layer_b_in_bwd(dq, dzb, dk, dv, x2, dx3, w_b_in, w_kv_f, bn, kvn)
    partial["w_kv"] = _wgrad(hkv, dkv, "wgrad_w_kv", n_split=8)
    partial["b_w_in"], grad["w_kv"] = _wgrad(hb, dqz, "wgrad_b_w_in", n_split=8,
                                             rider=_ReduceRider([partial["w_kv"]]))
    de0, dgp0, dx1, dm, duz, d_as, grad["b_w_in"] = _layer_a_out_bwd(
        dx2, e0, gt0, z, mcat, w_gate0, w_a_out, a_scale_f, w_group, rider=_ReduceRider([partial["b_w_in"]]))
    partial["gate0"] = _wgrad(x1, dgp0, "wgrad_gate0").reshape(N_DEV, 128, D)
    dw_a_out, grad["gate0"] = _wgrad(y, dx1, "wgrad_a_w_out", rider=_ReduceRider([partial["gate0"]]))
    partial["a_w_out"] = dw_a_out.reshape(N_DEV, 128, D)
    partial["a_w_in"], grad["a_w_out"] = _wgrad(h0, duz, "wgrad_a_w_in", n_split=8,
                                                rider=_ReduceRider([partial["a_w_out"]]))
    dw_group = _wgrad(pooled, dm, "wgrad_a_w_group", a_blocked_b=True)
    partial["a_w_group"] = dw_group.reshape(4, N_DEV, 32, 256).transpose(1, 0, 2, 3).reshape(N_DEV, 128, 256)
    partial["ple_w0"] = _wgrad(p0, de0, "wgrad_ple_w0", n_split=8)
    behind_a_in = ["a_w_in", "a_w_group", "ple_w0"]
    dx0, d_an, *done = _layer_a_in_bwd(duz, x0, dx1, w_a_in, a_norm_f,
                                      rider=_ReduceRider([partial[k] for k in behind_a_in]))
    grad.update(zip(behind_a_in, done))

    grads = [grad[k] for k in names]
    mom = {
        "a_w_in": (m_a_w_in, v_a_w_in), "a_w_group": (m_a_w_group, v_a_w_group), "a_w_out": (m_a_w_out, v_a_w_out),
        "w_kv": (m_w_kv, v_w_kv), "b_w_in": (m_b_w_in, v_b_w_in), "b_w_out": (m_b_w_out, v_b_w_out),
        "ple_w0": (m_ple_w[0], v_ple_w[0]), "ple_w1": (m_ple_w[1], v_ple_w[1]),
        "gate0": (m_ple_gate_w[0], v_ple_gate_w[0]), "gate1": (m_ple_gate_w[1], v_ple_gate_w[1]),
    }
    deltas, new_ms, new_vs = _adamw_shards(
        [big[k] for k in names], grads, [mom[k][0].reshape(big[k].shape) for k in names],
        [mom[k][1].reshape(big[k].shape) for k in names])
    res = {k: (grads[i], deltas[i], new_ms[i], new_vs[i]) for i, k in enumerate(names)}

    rows = jnp.concatenate([d_kvn, d_bn, d_an, d_as, jnp.zeros((2, D), F32), sq_err, jnp.zeros((1, D), F32)], axis=0)
    tot = _all_reduce_small(rows, (dqg, dkg))
    loss = tot[6, 0]
    g_kvn, g_bn = tot[0:1], tot[1:2]
    g_an = lax.dynamic_slice_in_dim(tot[2:3], me * 128, 128, axis=1)
    g_as = lax.dynamic_slice_in_dim(tot[3:4], me * 128, 128, axis=1)
    g_qg, g_kg = tot[4:5, :HEAD_DIM], tot[5:6, :HEAD_DIM]
    sm = {
        "a_norm": (g_an,) + _adamw_small(a_norm, g_an, m_a_norm, v_a_norm, "adamw_a_norm"),
        "a_scale": (g_as,) + _adamw_small(a_scale, g_as, m_a_scale, v_a_scale, "adamw_a_scale"),
        "kv_norm": tuple(t.reshape(D) for t in (g_kvn,) + _adamw_small(
            kvn, g_kvn, m_kv_norm.reshape(1, D), v_kv_norm.reshape(1, D), "adamw_kv_norm")),
        "k_norm": tuple(t.reshape(HEAD_DIM) for t in (g_kg,) + _adamw_small(
            kg, g_kg, m_k_norm.reshape(1, HEAD_DIM), v_k_norm.reshape(1, HEAD_DIM), "adamw_k_norm")),
        "b_norm": (g_bn,) + _adamw_small(b_norm, g_bn, m_b_norm, v_b_norm, "adamw_b_norm"),
        "b_q_norm": (g_qg,) + _adamw_small(b_q_norm, g_qg, m_b_q_norm, v_b_q_norm, "adamw_b_q_norm"),
    }

    def out(kind):
        def big_one(k, shape):
            return res[k][kind].reshape(shape)

        return [
            sm["a_norm"][kind], big_one("a_w_in", a_w_in.shape), big_one("a_w_group", a_w_group.shape),
            sm["a_scale"][kind], big_one("a_w_out", a_w_out.shape), sm["kv_norm"][kind],
            big_one("w_kv", w_kv.shape), sm["k_norm"][kind], sm["b_norm"][kind],
            big_one("b_w_in", b_w_in.shape), sm["b_q_norm"][kind], big_one("b_w_out", b_w_out.shape),
            jnp.stack([res["ple_w0"][kind], res["ple_w1"][kind]]),
            jnp.stack([res["gate0"][kind], res["gate1"][kind]]),
        ]

    return (loss, dx0.reshape(x.shape), *out(0), *out(1), *out(2), *out(3))
```

```python
import functools

import jax
import jax.numpy as jnp
from jax import lax
from jax.experimental import pallas as pl
from jax.experimental.pallas import tpu as pltpu

F32 = jnp.float32
BF16 = jnp.bfloat16
MESH = pl.DeviceIdType.MESH

N_DEV = 8
D = 1024
N_GROUPS = 4
GROUP_DIM = D // N_GROUPS
HALO = 16
HEADS = 16
HEAD_DIM = D // HEADS
SB_SCALE = HEAD_DIM ** -0.5
TILE = 256
LANES = 128
DEAD_LOG = -120.0
EPS = 1e-6
ADAM_LR = 0.001
ADAM_B1 = 0.9
ADAM_B2 = 0.999
ADAM_EPS = 1e-08
ADAM_WD = 0.01
ADAM_STEP = 10
TM = 256
VMEM_LIMIT = 56 * 1024 * 1024

HBM_SPEC = pl.BlockSpec(memory_space=pltpu.HBM)


def _dot(a, b):
    return jnp.dot(a, b, preferred_element_type=F32)


def _dot_nt(a, b):
    return lax.dot_general(a, b, (((1,), (1,)), ((), ())), preferred_element_type=F32)


def _dot_tn(a, b):
    return lax.dot_general(a, b, (((0,), (0,)), ((), ())), preferred_element_type=F32)


def _sigmoid(x):
    return jax.nn.sigmoid(x)


def _split_dot(x, mat):
    hi = x.astype(BF16)
    lo = (x - hi.astype(F32)).astype(BF16)
    return _dot(hi, mat) + _dot(lo, mat)


def _split_dot_many(xs, mat):
    rows = xs[0].shape[0]
    his = [x.astype(BF16) for x in xs]
    los = [(x - hi.astype(F32)).astype(BF16) for x, hi in zip(xs, his)]
    out = _dot(jnp.concatenate(his + los, axis=0), mat)
    n = len(xs)
    return [out[i * rows:(i + 1) * rows] + out[(n + i) * rows:(n + i + 1) * rows] for i in range(n)]


def _rms(x):
    return lax.rsqrt(jnp.mean(x * x, axis=-1, keepdims=True) + EPS)


def _hosted_call(body, name, n_steps, ins, in_specs, out_specs, out_shape, scratch, rider=None):
    ins, scratch = list(ins), list(scratch)
    if rider is None:
        wrapped, extra_in, extra_out, extra_scratch = body, [], [], []
    else:
        extra_in, extra_out, extra_scratch = rider.arrays, rider.out_shape(), rider.scratch()
        n_in, n_out, n_scr = len(ins), len(out_shape), len(scratch)
        k_in, k_out = len(extra_in), len(extra_out)

        def wrapped(*refs):
            own_in, r_in = refs[:n_in], refs[n_in:n_in + k_in]
            own_out = refs[n_in + k_in:n_in + k_in + n_out]
            r_out = refs[n_in + k_in + n_out:n_in + k_in + n_out + k_out]
            rest = refs[n_in + k_in + n_out + k_out:]
            phases = rider.bind(r_in, r_out, rest[n_scr:])
            step = pl.program_id(0)
            pl.when(step == 0)(phases[0])
            body(*own_in, *own_out, *rest[:n_scr])
            for share, phase in zip(rider.WHEN[1:], phases[1:]):
                at = min(n_steps - 1, max(1, round(share * (n_steps - 1))))
                pl.when(step == at)(phase)

    return pl.pallas_call(
        wrapped, name=name, grid=(n_steps,),
        in_specs=list(in_specs) + [HBM_SPEC] * len(extra_in),
        out_specs=list(out_specs) + [HBM_SPEC] * len(extra_out),
        out_shape=list(out_shape) + list(extra_out), scratch_shapes=scratch + list(extra_scratch),
        compiler_params=pltpu.CompilerParams(dimension_semantics=("arbitrary",), vmem_limit_bytes=VMEM_LIMIT),
    )(*ins, *extra_in)


def _rows_call(body, name, n_rows, row_ins, const_ins, row_outs, const_outs=(), scratch=(),
               reverse=False, tm=TM, rider=None):
    nb = n_rows // tm

    def row_map(i):
        return ((nb - 1 - i) if reverse else i, 0)

    def const_map(nd):
        return lambda i: (0,) * nd

    in_specs = [pl.BlockSpec((tm, a.shape[1]), row_map) for a in row_ins]
    in_specs += [pl.BlockSpec(a.shape, const_map(a.ndim)) for a in const_ins]
    out_specs = [pl.BlockSpec((tm, w), row_map) for (w, _) in row_outs]
    out_specs += [pl.BlockSpec(s, const_map(len(s))) for (s, _) in const_outs]
    out_shape = [jax.ShapeDtypeStruct((n_rows, w), dt) for (w, dt) in row_outs]
    out_shape += [jax.ShapeDtypeStruct(s, dt) for (s, dt) in const_outs]
    return _hosted_call(body, name, nb, list(row_ins) + list(const_ins), in_specs, out_specs, out_shape,
                        scratch, rider)


def _ple_fwd(p_ref, xin, wple_ref, wgate_ref, e_ref, gt_ref):
    pb = p_ref[...].astype(BF16)
    for j in range(N_DEV):
        e_ref[:, j * 128:(j + 1) * 128] = _dot(pb, wple_ref[j])
    gt = _sigmoid(_dot(xin.astype(BF16), wgate_ref[...]))
    gt_ref[...] = gt
    return xin + e_ref[...] * gt


def _layer_a_fwd(x0, p0, a_norm, a_scale, w_in, w_group, w_out, w_ple, w_gate, rider=None):
    s = x0.shape[0]
    tm = TM

    def body(x_ref, p_ref, an_ref, as_ref, win_ref, wg_ref, wout_ref, wple_ref, wgate_ref,
             h_ref, z_ref, pooled_ref, m_ref, y_ref, x1_ref, e_ref, gt_ref, x2_ref, uext):
        i = pl.program_id(0)

        @pl.when(i == 0)
        def _():
            uext[0:HALO, :] = jnp.zeros((HALO, D), F32)

        x = x_ref[...]
        h = (x * _rms(x) * an_ref[...]).astype(BF16)
        h_ref[...] = h
        for j in range(N_DEV):
            uz = _dot(h, win_ref[j])
            if j < 4:
                uext[HALO:HALO + tm, j * 256:(j + 1) * 256] = uz
            else:
                z_ref[:, (j - 4) * 256:(j - 3) * 256] = uz
        t = i * tm + lax.broadcasted_iota(jnp.int32, (tm, 1), 0)
        for g in range(N_GROUPS):
            w = 2 ** (g + 1)
            cols = slice(g * GROUP_DIM, (g + 1) * GROUP_DIM)
            ext = uext[:, cols]
            acc = ext
            k = 1
            while k < w:
                acc = acc + pltpu.roll(acc, k, 0)
                k *= 2
            cnt = jnp.minimum(t + 1, w).astype(F32)
            pooled = (acc[HALO:] / cnt - ext[HALO:]).astype(BF16)
            pooled_ref[:, cols] = pooled
            m_ref[:, cols] = _dot(pooled, wg_ref[g])
        uext[0:HALO, :] = uext[tm:tm + HALO, :]
        z = z_ref[...]
        y = (m_ref[...] * as_ref[...] * (z * _sigmoid(z))).astype(BF16)
        y_ref[...] = y
        x1 = x + _dot(y, wout_ref[...])
        x1_ref[...] = x1
        x2_ref[...] = _ple_fwd(p_ref, x1, wple_ref, wgate_ref, e_ref, gt_ref)

    row_outs = [(D, BF16), (D, F32), (D, BF16), (D, F32), (D, BF16), (D, F32), (D, F32), (D, F32), (D, F32)]
    return _rows_call(body, "layer_a_fwd", s, [x0, p0], [a_norm, a_scale, w_in, w_group, w_out, w_ple, w_gate],
                      row_outs, scratch=[pltpu.VMEM((tm + HALO, D), F32)], rider=rider)


def _layer_b_in_fwd(x2, kv_norm, b_norm, w_kv, w_bin):
    s = x2.shape[0]

    def body(x_ref, kvn_ref, bn_ref, wkv_ref, wbin_ref, hkv_ref, hb_ref, k_ref, v_ref, q_ref, zb_ref):
        x = x_ref[...]
        n = x * _rms(x)
        hkv = (n * kvn_ref[...]).astype(BF16)
        hb = (n * bn_ref[...]).astype(BF16)
        hkv_ref[...] = hkv
        hb_ref[...] = hb
        for j in range(N_DEV):
            kv = _dot(hkv, wkv_ref[j])
            qz = _dot(hb, wbin_ref[j])
            if j < 4:
                cols = slice(j * 256, (j + 1) * 256)
                k_ref[:, cols] = kv
                q_ref[:, cols] = qz
            else:
                cols = slice((j - 4) * 256, (j - 3) * 256)
                v_ref[:, cols] = kv.astype(BF16)
                zb_ref[:, cols] = qz

    row_outs = [(D, BF16), (D, BF16), (D, F32), (D, BF16), (D, F32), (D, F32)]
    return _rows_call(body, "layer_b_in_fwd", s, [x2], [kv_norm, b_norm, w_kv, w_bin], row_outs)


def _tri(after):
    r = lax.broadcasted_iota(jnp.int32, (TILE, TILE), 0)
    c = lax.broadcasted_iota(jnp.int32, (TILE, TILE), 1)
    return jnp.where((r > c) if after else (r < c), 1.0, 0.0).astype(BF16)


def _half_sums(v):
    r = lax.broadcasted_iota(jnp.int32, (LANES, LANES), 0) < HEAD_DIM
    c = lax.broadcasted_iota(jnp.int32, (LANES, LANES), 1) < HEAD_DIM
    same_head = jnp.where(r == c, 1.0, 0.0).astype(BF16)
    return _split_dot(v, same_head)


def _pair_norm(x):
    r = lax.rsqrt(_half_sums(x * x) * (1.0 / HEAD_DIM) + EPS)
    return x * r, r


def _tile_logits(qblk, kblk, diagonal):
    l = _dot_nt(qblk, kblk)
    sp = jnp.maximum(l, 0.0) + jnp.log(1.0 + jnp.exp(-jnp.abs(l)))
    ls = l - sp
    if not diagonal:
        return None, -sp, ls
    mask = lax.broadcasted_iota(jnp.int32, l.shape, 1) < lax.broadcasted_iota(jnp.int32, l.shape, 0)
    return mask, jnp.where(mask, -sp, 0.0), ls


def _attn_fwd(q_all, k_all, v_all, q_gain2, k_gain2, rider=None):
    s = q_all.shape[0]
    nt = s // TILE

    def body(q_ref, k_ref, v_ref, qg_ref, kg_ref, o_ref, c_ref, qs, ks, vs, tri, acc, right, cmat):
        tri[...] = _tri(True)
        lane = lax.broadcasted_iota(jnp.int32, (TILE, LANES), 1)
        qn, _ = _pair_norm(q_ref[...])
        kn, _ = _pair_norm(k_ref[...])
        qsc = (qn * qg_ref[...] * SB_SCALE).astype(BF16)
        ksc = (kn * kg_ref[...]).astype(BF16)
        for hh in range(2):
            sl = slice(hh * HEAD_DIM, (hh + 1) * HEAD_DIM)
            qs[hh] = qsc[:, sl]
            ks[hh] = ksc[:, sl]
            vs[hh] = v_ref[:, sl]

        def tile(qrows, kb, diagonal):
            rows = pl.ds(pl.multiple_of(kb * TILE, TILE), TILE)
            loaded = [(qs[hh, qrows, :], ks[hh, rows, :], vs[hh, rows, :], right[hh], cmat[hh], acc[hh])
                      for hh in range(2)]
            logits = [_tile_logits(q, k, diagonal) for q, k, _, _, _, _ in loaded]
            later = _split_dot_many([lk for _, lk, _ in logits], tri[...])
            results = []
            for (q, k, v, rt, cm, ac), (mask, lk, ls), lt in zip(loaded, logits, later):
                a = jnp.exp(ls + lt + rt)
                if diagonal:
                    a = jnp.where(mask, a, 0.0)
                results.append((ac + _dot(a.astype(BF16), v), jnp.where(lane == kb, rt[:, :LANES], cm),
                                rt + jnp.sum(lk, axis=1, keepdims=True)))
            for hh, (ac, cm, rt) in enumerate(results):
                acc[hh] = ac
                cmat[hh] = cm
                right[hh] = rt

        def q_step(qb, _):
            r0 = pl.multiple_of(qb * TILE, TILE)
            qrows = pl.ds(r0, TILE)
            acc[...] = jnp.zeros((2, TILE, HEAD_DIM), F32)
            right[...] = jnp.zeros((2, TILE, TILE), F32)
            cmat[...] = jnp.zeros((2, TILE, LANES), F32)
            tile(qrows, qb, True)

            def live():
                return (jnp.max(right[:, :, :LANES]) > DEAD_LOG).astype(jnp.int32)

            def k_step(c):
                kb = c[0] - 1
                tile(qrows, kb, False)
                return kb, live()

            first, _ = lax.while_loop(lambda c: (c[0] > 0) & (c[1] > 0), k_step, (qb, live()))
            for hh in range(2):
                o_ref[qrows, hh * HEAD_DIM:(hh + 1) * HEAD_DIM] = acc[hh]
                c_ref[hh, qrows, :] = jnp.where(lane == LANES - 1, first.astype(F32), cmat[hh])
            return 0

        lax.fori_loop(0, nt, q_step, 0)

    pair = pl.BlockSpec((s, LANES), lambda h: (0, h))
    gain = pl.BlockSpec((1, LANES), lambda h: (0, 0))
    return _hosted_call(
        body, "attn_fwd", HEADS // 2, [q_all, k_all, v_all, q_gain2, k_gain2],
        [pair, pair, pair, gain, gain], [pair, pl.BlockSpec((2, s, LANES), lambda h: (h, 0, 0))],
        [jax.ShapeDtypeStruct((s, D), F32), jax.ShapeDtypeStruct((HEADS, s, LANES), F32)],
        [pltpu.VMEM((2, s, HEAD_DIM), BF16)] * 3
        + [pltpu.VMEM((TILE, TILE), BF16), pltpu.VMEM((2, TILE, HEAD_DIM), F32), pltpu.VMEM((2, TILE, TILE), F32),
           pltpu.VMEM((2, TILE, LANES), F32)], rider)


def _layer_b_out_fwd(o, zb, x2, p1, target, w_out, w_ple, w_gate):
    s = o.shape[0]

    def body(o_ref, zb_ref, x2_ref, p_ref, t_ref, wout_ref, wple_ref, wgate_ref,
             yb_ref, x3_ref, e_ref, gt_ref, dx4_ref, loss_ref):
        zb = zb_ref[...]
        yb = (o_ref[...] * (zb * _sigmoid(zb))).astype(BF16)
        yb_ref[...] = yb
        x3 = x2_ref[...] + _dot(yb, wout_ref[...])
        x3_ref[...] = x3
        x4 = _ple_fwd(p_ref, x3, wple_ref, wgate_ref, e_ref, gt_ref)
        d = x4 - t_ref[...]
        dx4_ref[...] = d * (1.0 / D)

        @pl.when(pl.program_id(0) == 0)
        def _():
            loss_ref[...] = jnp.zeros((1, D), F32)

        loss_ref[...] += jnp.sum(d * d, axis=0, keepdims=True)

    row_outs = [(D, BF16), (D, F32), (D, F32), (D, F32), (D, F32)]
    return _rows_call(body, "layer_b_out_fwd", s, [o, zb, x2, p1, target], [w_out, w_ple, w_gate], row_outs,
                      const_outs=[((1, D), F32)])


def _ple_bwd(dxo, e_ref, gt_ref, wgate_ref, de_ref, dgp_ref):
    e = e_ref[...]
    gt = gt_ref[...]
    de_ref[...] = (dxo * gt).astype(BF16)
    dgp = (dxo * e * gt * (1.0 - gt)).astype(BF16)
    dgp_ref[...] = dgp
    return dxo + _dot_nt(dgp, wgate_ref[...])


def _silu_grads(z):
    sg = _sigmoid(z)
    return z * sg, sg * (1.0 + z * (1.0 - sg))


def _layer_b_out_bwd(dx4, e1, gt1, o, zb, w_gate, w_out):
    s = dx4.shape[0]

    def body(dx4_ref, e_ref, gt_ref, o_ref, zb_ref, wgate_ref, wout_ref,
             de_ref, dgp_ref, dx3_ref, do_ref, dzb_ref):
        dx3 = _ple_bwd(dx4_ref[...], e_ref, gt_ref, wgate_ref, de_ref, dgp_ref)
        dx3_ref[...] = dx3
        dyb = _dot_nt(dx3.astype(BF16), wout_ref[...])
        silu, dsilu = _silu_grads(zb_ref[...])
        do_ref[...] = (dyb * silu).astype(BF16)
        dzb_ref[...] = (dyb * o_ref[...] * dsilu).astype(BF16)

    row_outs = [(D, BF16), (D, BF16), (D, F32), (D, BF16), (D, BF16)]
    return _rows_call(body, "layer_b_out_bwd", s, [dx4, e1, gt1, o, zb], [w_gate, w_out], row_outs)


def _attn_bwd(q_all, k_all, v_all, q_gain2, k_gain2, d_o, csave, rider=None):
    s = q_all.shape[0]
    nt = s // TILE

    def body(q_ref, k_ref, v_ref, qg_ref, kg_ref, do_ref, c_ref,
             dq_ref, dk_ref, dv_ref, dqg_ref, dkg_ref,
             qs, ks, vs, dos, qt, dot_t, tri_a, tri_b, dqa, dkt, dvt, dqb, left):
        tri_a[...] = _tri(True)
        tri_b[...] = _tri(False)
        lane = lax.broadcasted_iota(jnp.int32, (TILE, LANES), 1)
        qn, qr = _pair_norm(q_ref[...])
        kn, kr = _pair_norm(k_ref[...])
        qsc = qn * qg_ref[...] * SB_SCALE
        ksc = (kn * kg_ref[...]).astype(BF16)
        q_t = qsc.T.astype(BF16)
        do_t = do_ref[...].astype(F32).T.astype(BF16)
        for j in range(nt):
            qt[j] = q_t[:, j * TILE:(j + 1) * TILE]
            dot_t[j] = do_t[:, j * TILE:(j + 1) * TILE]
        dkt[...] = jnp.zeros((nt, LANES, TILE), F32)
        dvt[...] = jnp.zeros((nt, LANES, TILE), F32)
        qsc = qsc.astype(BF16)
        for hh in range(2):
            sl = slice(hh * HEAD_DIM, (hh + 1) * HEAD_DIM)
            qs[hh] = qsc[:, sl]
            ks[hh] = ksc[:, sl]
            vs[hh] = v_ref[:, sl]
            dos[hh] = do_ref[:, sl]

        def tile(qb, qrows, kb, diagonal):
            rows = pl.ds(pl.multiple_of(kb * TILE, TILE), TILE)
            heads = range(2)
            kblk = [ks[hh, rows, :] for hh in heads]
            logits = [_tile_logits(qs[hh, qrows, :], kblk[hh], diagonal) for hh in heads]
            later = _split_dot_many([lk for _, lk, _ in logits], tri_a[...])
            a, g = [], []
            for hh in heads:
                mask, _, ls = logits[hh]
                right = jnp.sum(jnp.where(lane == kb, c_ref[hh, qrows, :], 0.0), axis=1, keepdims=True)
                a_h = jnp.exp(ls + later[hh] + right)
                a.append(jnp.where(mask, a_h, 0.0) if diagonal else a_h)
                g.append(a[hh] * _dot_nt(dos[hh, qrows, :], vs[hh, rows, :]))
            before = _split_dot_many(g, tri_b[...])
            for hh in heads:
                sl = slice(hh * HEAD_DIM, (hh + 1) * HEAD_DIM)
                mask, _, ls = logits[hh]
                beta = jnp.exp(ls)
                lf = left[hh]
                dl = g[hh] * (1.0 - beta) - (before[hh] + lf) * beta
                if diagonal:
                    dl = jnp.where(mask, dl, 0.0)
                dl = dl.astype(BF16)
                left[hh] = lf + jnp.sum(g[hh], axis=1, keepdims=True)
                dqb[hh] += _dot(dl, kblk[hh])
                dkt[kb, sl, :] += _dot(qt[qb, sl, :], dl)
                dvt[kb, sl, :] += _dot(dot_t[qb, sl, :], a[hh].astype(BF16))

        def q_step(qb, _):
            qrows = pl.ds(pl.multiple_of(qb * TILE, TILE), TILE)
            dqb[...] = jnp.zeros((2, TILE, HEAD_DIM), F32)
            left[...] = jnp.zeros((2, TILE, TILE), F32)

            def k_step(kb, _):
                tile(qb, qrows, kb, False)
                return 0

            first = jnp.max(jnp.where(lane == LANES - 1, c_ref[0, qrows, :], 0.0)).astype(jnp.int32)
            lax.fori_loop(first, qb, k_step, 0)
            tile(qb, qrows, qb, True)
            for hh in range(2):
                dqa[qrows, hh * HEAD_DIM:(hh + 1) * HEAD_DIM] = dqb[hh] * SB_SCALE
            return 0

        lax.fori_loop(0, nt, q_step, 0)

        def norm_bwd(dy, xn, r, g_ref, dx_ref, dg_ref):
            dg_ref[...] = jnp.sum(dy * xn, axis=0, keepdims=True)
            dxn = dy * g_ref[...]
            dx_ref[...] = r * (dxn - xn * (_half_sums(dxn * xn) * (1.0 / HEAD_DIM)))

        norm_bwd(dqa[...], qn, qr, qg_ref, dq_ref, dqg_ref)
        for j in range(nt):
            dqa[j * TILE:(j + 1) * TILE, :] = dkt[j].T
            dv_ref[j * TILE:(j + 1) * TILE, :] = dvt[j].T
        norm_bwd(dqa[...], kn, kr, kg_ref, dk_ref, dkg_ref)

    pair = pl.BlockSpec((s, LANES), lambda h: (0, h))
    gain = pl.BlockSpec((1, LANES), lambda h: (0, 0))
    dgain = pl.BlockSpec((None, 1, LANES), lambda h: (h, 0, 0))
    return _hosted_call(
        body, "attn_bwd", HEADS // 2, [q_all, k_all, v_all, q_gain2, k_gain2, d_o, csave],
        [pair, pair, pair, gain, gain, pair, pl.BlockSpec((2, s, LANES), lambda h: (h, 0, 0))],
        [pair, pair, pair, dgain, dgain],
        [jax.ShapeDtypeStruct((s, D), F32)] * 3 + [jax.ShapeDtypeStruct((HEADS // 2, 1, LANES), F32)] * 2,
        [pltpu.VMEM((2, s, HEAD_DIM), BF16)] * 4
        + [pltpu.VMEM((nt, LANES, TILE), BF16)] * 2 + [pltpu.VMEM((TILE, TILE), BF16)] * 2
        + [pltpu.VMEM((s, LANES), F32)] + [pltpu.VMEM((nt, LANES, TILE), F32)] * 2
        + [pltpu.VMEM((2, TILE, HEAD_DIM), F32), pltpu.VMEM((2, TILE, TILE), F32)], rider)


def _norm_bwd_rows(dh, x, gain, dgain_ref):
    r = _rms(x)
    n = x * r
    dgain_ref[...] += jnp.sum(dh * n, axis=0, keepdims=True)
    dn = dh * gain
    return r * (dn - n * jnp.mean(dn * n, axis=-1, keepdims=True))


def _layer_b_in_bwd(dq, dzb, dk, dv, x2, dx3, w_bin, w_kv, b_norm, kv_norm):
    s = x2.shape[0]

    def body(dq_ref, dzb_ref, dk_ref, dv_ref, x_ref, dx3_ref, wbin_ref, wkv_ref, bn_ref, kvn_ref,
             dqz_ref, dkv_ref, dx2_ref, dbn_ref, dkvn_ref):
        @pl.when(pl.program_id(0) == 0)
        def _():
            dbn_ref[...] = jnp.zeros((1, D), F32)
            dkvn_ref[...] = jnp.zeros((1, D), F32)

        dqz_ref[:, :D] = dq_ref[...].astype(BF16)
        dqz_ref[:, D:] = dzb_ref[...]
        dkv_ref[:, :D] = dk_ref[...].astype(BF16)
        dkv_ref[:, D:] = dv_ref[...].astype(BF16)
        dhb = jnp.zeros((TM, D), F32)
        dhkv = jnp.zeros((TM, D), F32)
        for j in range(N_DEV):
            cols = slice(j * 256, (j + 1) * 256)
            dhb = dhb + _dot_nt(dqz_ref[:, cols], wbin_ref[j])
            dhkv = dhkv + _dot_nt(dkv_ref[:, cols], wkv_ref[j])
        x = x_ref[...]
        dx2 = dx3_ref[...] + _norm_bwd_rows(dhb, x, bn_ref[...], dbn_ref)
        dx2_ref[...] = dx2 + _norm_bwd_rows(dhkv, x, kvn_ref[...], dkvn_ref)

    row_outs = [(2 * D, BF16), (2 * D, BF16), (D, F32)]
    return _rows_call(body, "layer_b_in_bwd", s, [dq, dzb, dk, dv, x2, dx3], [w_bin, w_kv, b_norm, kv_norm],
                      row_outs, const_outs=[((1, D), F32), ((1, D), F32)])


def _layer_a_out_bwd(dx2, e0, gt0, z, m, w_gate, w_out, a_scale, w_group, rider=None):
    s = dx2.shape[0]
    tm = TM
    nb = s // tm

    def body(dx2_ref, e_ref, gt_ref, z_ref, m_ref, wgate_ref, wout_ref, as_ref, wg_ref,
             de_ref, dgp_ref, dx1_ref, dm_ref, duz_ref, das_ref, ext):
        i = pl.program_id(0)

        @pl.when(i == 0)
        def _():
            das_ref[...] = jnp.zeros((1, D), F32)
            ext[tm:tm + HALO, :] = jnp.zeros((HALO, D), F32)

        dx1 = _ple_bwd(dx2_ref[...], e_ref, gt_ref, wgate_ref, de_ref, dgp_ref)
        dx1_ref[...] = dx1
        dy = _dot_nt(dx1.astype(BF16), wout_ref[...])
        silu, dsilu = _silu_grads(z_ref[...])
        m = m_ref[...]
        dmixed = dy * silu
        duz_ref[:, D:] = (dy * (m * as_ref[...]) * dsilu).astype(BF16)
        das_ref[...] += jnp.sum(dmixed * m, axis=0, keepdims=True)
        dm_ref[...] = (dmixed * as_ref[...]).astype(BF16)
        t = (nb - 1 - i) * tm + lax.broadcasted_iota(jnp.int32, (tm, 1), 0)
        n_ext = tm + HALO
        for g in range(N_GROUPS):
            w = 2 ** (g + 1)
            cols = slice(g * GROUP_DIM, (g + 1) * GROUP_DIM)
            dpool = _dot_nt(dm_ref[:, cols], wg_ref[g])
            ext[0:tm, cols] = dpool / jnp.minimum(t + 1, w).astype(F32)
            acc = ext[:, cols]
            k = 1
            while k < w:
                acc = acc + pltpu.roll(acc, n_ext - k, 0)
                k *= 2
            duz_ref[:, cols] = (acc[:tm] - dpool).astype(BF16)
        ext[tm:tm + HALO, :] = ext[0:HALO, :]

    row_outs = [(D, BF16), (D, BF16), (D, F32), (D, BF16), (2 * D, BF16)]
    return _rows_call(body, "layer_a_out_bwd", s, [dx2, e0, gt0, z, m], [w_gate, w_out, a_scale, w_group],
                      row_outs, const_outs=[((1, D), F32)], scratch=[pltpu.VMEM((tm + HALO, D), F32)],
                      reverse=True, rider=rider)


def _layer_a_in_bwd(duz, x0, dx1, w_in, a_norm, rider=None):
    s = x0.shape[0]

    def body(duz_ref, x_ref, dx1_ref, win_ref, an_ref, dx0_ref, dan_ref):
        @pl.when(pl.program_id(0) == 0)
        def _():
            dan_ref[...] = jnp.zeros((1, D), F32)

        dh = jnp.zeros((TM, D), F32)
        for j in range(N_DEV):
            dh = dh + _dot_nt(duz_ref[:, j * 256:(j + 1) * 256], win_ref[j])
        dx0_ref[...] = dx1_ref[...] + _norm_bwd_rows(dh, x_ref[...], an_ref[...], dan_ref)

    return _rows_call(body, "layer_a_in_bwd", s, [duz, x0, dx1], [w_in, a_norm], [(D, F32)],
                      const_outs=[((1, D), F32)], rider=rider)


def _wgrad(a, b, name, n_split=1, a_blocked_b=False, rider=None):
    s, k = a.shape
    n = b.shape[1]
    tk = 256
    nb = n // n_split

    def body(a_ref, b_ref, o_ref):
        res = _dot_tn(a_ref[...].astype(BF16), b_ref[...].astype(BF16))
        if n_split == 1:
            o_ref[...] = res.astype(BF16)
        else:
            for j in range(n_split):
                o_ref[j] = res[:, j * nb:(j + 1) * nb].astype(BF16)

    if a_blocked_b:
        b_spec = pl.BlockSpec((s, tk), lambda i: (0, i))
        out_spec = pl.BlockSpec((None, tk, tk), lambda i: (i, 0, 0))
        out_shape = jax.ShapeDtypeStruct((k // tk, tk, tk), BF16)
    elif n_split == 1:
        b_spec = pl.BlockSpec((s, n), lambda i: (0, 0))
        out_spec = pl.BlockSpec((tk, n), lambda i: (i, 0))
        out_shape = jax.ShapeDtypeStruct((k, n), BF16)
    else:
        b_spec = pl.BlockSpec((s, n), lambda i: (0, 0))
        out_spec = pl.BlockSpec((n_split, tk, nb), lambda i: (0, i, 0))
        out_shape = jax.ShapeDtypeStruct((n_split, k, nb), BF16)
    res = _hosted_call(body, name, k // tk, [a, b], [pl.BlockSpec((s, tk), lambda i: (0, i)), b_spec],
                       [out_spec], [out_shape], [], rider)
    return res[0] if rider is None else res


def _cast_shards(shards):
    n = len(shards)
    layers = [a.shape[0] if a.ndim == 3 else 0 for a in shards]

    def body(*refs):
        outs = iter(refs[n:])
        for a in range(n):
            if layers[a]:
                for t in range(layers[a]):
                    next(outs)[...] = refs[a][t].astype(BF16)
            else:
                next(outs)[...] = refs[a][...].astype(BF16)

    out_shape = []
    for a, k in zip(shards, layers):
        out_shape += [jax.ShapeDtypeStruct(a.shape[-2:], BF16)] * max(k, 1)
    vmem = pl.BlockSpec(memory_space=pltpu.VMEM)
    return pl.pallas_call(
        body, name="cast_shards", in_specs=[vmem] * n, out_specs=[vmem] * len(out_shape), out_shape=out_shape,
        compiler_params=pltpu.CompilerParams(vmem_limit_bytes=VMEM_LIMIT),
    )(*shards)


def _adamw(w, g, m, v):
    m = ADAM_B1 * m + (1.0 - ADAM_B1) * g
    v = ADAM_B2 * v + (1.0 - ADAM_B2) * jnp.square(g)
    m_hat = m / (1.0 - ADAM_B1 ** ADAM_STEP)
    v_hat = v / (1.0 - ADAM_B2 ** ADAM_STEP)
    delta = -ADAM_LR * (m_hat / (jnp.sqrt(v_hat) + ADAM_EPS) + ADAM_WD * w)
    return delta, m, v


def _place():
    return lax.axis_index("x"), lax.axis_index("y"), lax.axis_index("c")


def _all_gather(shards):
    return _alone("all_gather_weights", _GatherRider(shards))


def _alone(name, rider):
    n_in, n_out = len(rider.arrays), len(rider.out_shape())

    def body(*refs):
        for phase in rider.bind(refs[:n_in], refs[n_in:n_in + n_out], refs[n_in + n_out:]):
            phase()

    return pl.pallas_call(
        body, name=name, in_specs=[HBM_SPEC] * n_in, out_specs=[HBM_SPEC] * n_out,
        out_shape=rider.out_shape(), scratch_shapes=rider.scratch(),
        compiler_params=pltpu.CompilerParams(vmem_limit_bytes=VMEM_LIMIT),
    )(*rider.arrays)


class _GatherRider:
    WHEN = (0.0, 0.7, 1.0)

    def __init__(self, shards):
        self.arrays = list(shards)

    def out_shape(self):
        return _Gather.out_shape(self.arrays)

    def scratch(self):
        return _Gather.semaphores(len(self.arrays))

    def bind(self, ins, outs, scratch):
        moving = _Gather(ins, outs, *scratch)
        return moving.start, moving.forward, moving.finish


class _ReduceRider:
    WHEN = (0.0, 0.15, 0.5, 1.0)

    def __init__(self, partials):
        self.arrays = list(partials)

    def out_shape(self):
        return [jax.ShapeDtypeStruct(a.shape[1:], F32) for a in self.arrays]

    def scratch(self):
        n = len(self.arrays)
        dma = pltpu.SemaphoreType.DMA

        def blocks(k):
            return [pltpu.VMEM((k,) + a.shape[1:], BF16) for a in self.arrays]

        halves = [pltpu.VMEM((2, a.shape[1] // 2) + a.shape[2:], BF16) for a in self.arrays]
        return (blocks(4) + blocks(4) + halves + blocks(2) + [pltpu.VMEM(a.shape[1:], F32) for a in self.arrays]
                + [dma((4 * n,)), dma((4 * n,)), dma((4 * n,)), dma((2 * n,)), dma((2 * n,)),
                   dma((2 * n,)), dma((2 * n,)), dma((n,))])

    def bind(self, ins, outs, scratch):
        n = len(ins)
        mine, landed, halves, arrived, total = (scratch[i * n:(i + 1) * n] for i in range(5))
        send1, recv1, local1, send_h, recv_h, send2, recv2, out_sems = scratch[5 * n:]
        x, y, c = _place()
        plane = 2 * x + y
        via = [(x, 1 - y, c), (1 - x, y, c)]
        nbr = [(1 - x, y, c), (x, 1 - y, c)]
        nbr_block = [2 * (1 - x) + y, 2 * x + (1 - y)]
        diag_block = 2 * (1 - x) + (1 - y)

        def to_sibling(a, k):
            return pltpu.make_async_remote_copy(
                src_ref=ins[a].at[2 * k + (1 - c)], dst_ref=landed[a].at[k],
                send_sem=send1.at[4 * a + k], recv_sem=recv1.at[4 * a + k],
                device_id=(x, y, 1 - c), device_id_type=MESH)

        def own_block(a, k):
            return pltpu.make_async_copy(ins[a].at[2 * k + c], mine[a].at[k], local1.at[4 * a + k])

        def half_of(a, ref, h):
            rows = self.arrays[a].shape[1] // 2
            return ref.at[pl.ds(h * rows, rows)]

        def half_out(a, h):
            return pltpu.make_async_remote_copy(
                src_ref=half_of(a, mine[a].at[diag_block], h), dst_ref=halves[a].at[h],
                send_sem=send_h.at[2 * a + h], recv_sem=recv_h.at[2 * a + h],
                device_id=via[h], device_id_type=MESH)

        def to_owner(a, h):
            return pltpu.make_async_remote_copy(
                src_ref=mine[a].at[nbr_block[h]], dst_ref=arrived[a].at[h],
                send_sem=send2.at[2 * a + h], recv_sem=recv2.at[2 * a + h],
                device_id=nbr[h], device_id_type=MESH)

        def result(a):
            return pltpu.make_async_copy(total[a], outs[a], out_sems.at[a])

        def exchange_cores():
            for a in range(n):
                for k in range(4):
                    to_sibling(a, k).start()
                    own_block(a, k).start()

        def pair_sums():
            for a in range(n):
                for k in range(4):
                    own_block(a, k).wait()
                    to_sibling(a, k).wait_recv()
                total[a][...] = mine[a][plane].astype(F32) + landed[a][plane].astype(F32)
                for k in range(4):
                    mine[a][k] = (mine[a][k].astype(F32) + landed[a][k].astype(F32)).astype(BF16)
                for h in range(2):
                    half_out(a, h).start()

        def fold_and_send():
            for a in range(n):
                rows = self.arrays[a].shape[1] // 2
                for h in range(2):
                    half_out(a, h).wait_recv()
                    part = mine[a].at[nbr_block[h]]
                    span = slice(h * rows, (h + 1) * rows)
                    part[span] = (part[span].astype(F32) + halves[a][h].astype(F32)).astype(BF16)
                    to_owner(a, h).start()

        def finish():
            for a in range(n):
                for h in range(2):
                    to_owner(a, h).wait_recv()
                    total[a][...] += arrived[a][h].astype(F32)
                result(a).start()
            for a in range(n):
                for k in range(4):
                    to_sibling(a, k).wait_send()
                for h in range(2):
                    half_out(a, h).wait_send()
                    to_owner(a, h).wait_send()
                result(a).wait()

        return exchange_cores, pair_sums, fold_and_send, finish


class _Gather:
    COPIES = 9

    def __init__(self, ins, outs, send_sems, recv_sems, local_sems):
        self.ins, self.outs = ins, outs
        self.send_sems, self.recv_sems, self.local_sems = send_sems, recv_sems, local_sems
        self.x, self.y, self.c = _place()

    @staticmethod
    def out_shape(shards):
        return [jax.ShapeDtypeStruct((N_DEV,) + a.shape, a.dtype) for a in shards]

    @staticmethod
    def semaphores(n):
        dma = pltpu.SemaphoreType.DMA
        return [dma((_Gather.COPIES * n,)), dma((_Gather.COPIES * n,)), dma((n,))]

    def _copy(self, a, k, block, to, own=False, half=None):
        px, py, pc = block
        slot = self.outs[a].at[4 * px + 2 * py + pc]
        if half is not None:
            rows = slot.shape[0] // 2
            slot = slot.at[pl.ds(half * rows, rows)]
        return pltpu.make_async_remote_copy(
            src_ref=self.ins[a] if own else slot, dst_ref=slot,
            send_sem=self.send_sems.at[self.COPIES * a + k], recv_sem=self.recv_sems.at[self.COPIES * a + k],
            device_id=to, device_id_type=MESH)

    def _local(self, a):
        return pltpu.make_async_copy(self.ins[a], self.outs[a].at[4 * self.x + 2 * self.y + self.c],
                                     self.local_sems.at[a])

    def _plan(self, a, c):
        x, y = self.x, self.y
        me, sibling = (x, y, c), (x, y, 1 - c)
        xn, yn, dg = (1 - x, y, c), (x, 1 - y, c), (1 - x, 1 - y, c)
        return [
            self._copy(a, 0, me, sibling, own=True), self._copy(a, 1, me, xn, own=True),
            self._copy(a, 2, me, yn, own=True),
            self._copy(a, 3, xn, yn, half=0), self._copy(a, 4, yn, xn, half=1),
            self._copy(a, 5, xn, sibling), self._copy(a, 6, yn, sibling),
            self._copy(a, 7, dg, sibling, half=0), self._copy(a, 8, dg, sibling, half=1),
        ]

    def _arrivals(self, a):
        x, y, c = self.x, self.y, self.c
        me = (x, y, c)
        xn, yn, dg = (1 - x, y, c), (x, 1 - y, c), (1 - x, 1 - y, c)
        other = 1 - c
        return [
            self._copy(a, 0, (x, y, other), me), self._copy(a, 1, xn, me), self._copy(a, 2, yn, me),
            self._copy(a, 3, dg, me, half=0), self._copy(a, 4, dg, me, half=1),
            self._copy(a, 5, (1 - x, y, other), me), self._copy(a, 6, (x, 1 - y, other), me),
            self._copy(a, 7, (1 - x, 1 - y, other), me, half=0), self._copy(a, 8, (1 - x, 1 - y, other), me, half=1),
        ]

    def start(self):
        for a in range(len(self.ins)):
            self._local(a).start()
            for cp in self._plan(a, self.c)[:3]:
                cp.start()

    def forward(self):
        for a in range(len(self.ins)):
            sends, lands = self._plan(a, self.c), self._arrivals(a)
            lands[1].wait_recv()
            sends[3].start()
            sends[5].start()
            lands[2].wait_recv()
            sends[4].start()
            sends[6].start()

    def finish(self):
        n = len(self.ins)
        for a in range(n):
            sends, lands = self._plan(a, self.c), self._arrivals(a)
            lands[3].wait_recv()
            sends[7].start()
            lands[4].wait_recv()
            sends[8].start()
        for a in range(n):
            lands = self._arrivals(a)
            for k in (0, 5, 6, 7, 8):
                lands[k].wait_recv()
        for a in range(n):
            for cp in self._plan(a, self.c):
                cp.wait_send()
            self._local(a).wait()


def _adamw_all(name, ws, gs, ms, vs):
    n = len(ws)
    per_layer = [isinstance(g, tuple) for g in gs]
    flat_g = [part for g in gs for part in (g if isinstance(g, tuple) else (g,))]

    def body(*refs):
        w, refs = refs[:n], refs[n:]
        g, refs = refs[:len(flat_g)], refs[len(flat_g):]
        m, v, outs = refs[:n], refs[n:2 * n], refs[2 * n:]
        stacked = iter(outs[3 * n:])
        parts = iter(g)
        for a in range(n):
            if per_layer[a]:
                whole = next(stacked)
                for t in range(len(gs[a])):
                    grad = next(parts)[...]
                    whole[t] = grad
                    outs[a][t], outs[n + a][t], outs[2 * n + a][t] = _adamw(w[a][t], grad, m[a][t], v[a][t])
            else:
                outs[a][...], outs[n + a][...], outs[2 * n + a][...] = _adamw(
                    w[a][...], next(parts)[...], m[a][...], v[a][...])

    shapes = [jax.ShapeDtypeStruct(a.shape, F32) for a in ws]
    vmem = pl.BlockSpec(memory_space=pltpu.VMEM)
    n_out = 3 * n + sum(per_layer)
    res = pl.pallas_call(
        body, name=name, in_specs=[vmem] * (3 * n + len(flat_g)), out_specs=[vmem] * n_out,
        out_shape=shapes * 3 + [s for s, p in zip(shapes, per_layer) if p],
        compiler_params=pltpu.CompilerParams(vmem_limit_bytes=VMEM_LIMIT),
    )(*ws, *flat_g, *ms, *vs)
    stacked = iter(res[3 * n:])
    return [(next(stacked) if per_layer[a] else gs[a], res[a], res[n + a], res[2 * n + a]) for a in range(n)]


def _all_reduce_small(rows, gain_parts):
    def body(rows_ref, dqg_ref, dkg_ref, out_ref, buf, send_sems, recv_sems):
        x, y, c = _place()
        me = 4 * x + 2 * y + c
        buf[0] = rows_ref[...]
        for row, part in ((4, dqg_ref), (5, dkg_ref)):
            both = jnp.sum(part[...].reshape(HEADS // 2, LANES), axis=0, keepdims=True)
            buf[0, row:row + 1, 0:HEAD_DIM] = both[:, :HEAD_DIM] + both[:, HEAD_DIM:]
        copies = []
        for r in range(1, N_DEV):
            bx, by, bc = (r >> 2) & 1, (r >> 1) & 1, r & 1
            to = (x ^ bx, y ^ by, c ^ bc)
            copies.append(pltpu.make_async_remote_copy(
                src_ref=buf.at[0], dst_ref=buf.at[r], send_sem=send_sems.at[r - 1], recv_sem=recv_sems.at[r - 1],
                device_id=to, device_id_type=MESH))
        for cp in copies:
            cp.start()
        for cp in copies:
            cp.wait_recv()
        for cp in copies:
            cp.wait_send()
        tot = buf[me]
        for j in range(1, N_DEV):
            tot = tot + buf[j ^ me]
        out_ref[...] = tot
        loss = (0.5 / D) * jnp.sum(tot[6:7, :], axis=1, keepdims=True)
        out_ref[6:7, :] = jnp.broadcast_to(loss, (1, D))

    vmem = pl.BlockSpec(memory_space=pltpu.VMEM)
    return pl.pallas_call(
        body, name="all_reduce_small", in_specs=[vmem] * 3, out_specs=vmem,
        out_shape=jax.ShapeDtypeStruct((8, D), F32),
        scratch_shapes=[pltpu.VMEM((N_DEV, 8, D), F32), pltpu.SemaphoreType.DMA((N_DEV - 1,)),
                        pltpu.SemaphoreType.DMA((N_DEV - 1,))],
    )(rows, *gain_parts)


def kernel(x, p, a_norm, a_w_in, a_w_group, a_scale, a_w_out, kv_norm, w_kv, k_norm, b_norm, b_w_in, b_q_norm, b_w_out, ple_w, ple_gate_w, loss_target, m_a_norm, m_a_w_in, m_a_w_group, m_a_scale, m_a_w_out, m_kv_norm, m_w_kv, m_k_norm, m_b_norm, m_b_w_in, m_b_q_norm, m_b_w_out, m_ple_w, m_ple_gate_w, v_a_norm, v_a_w_in, v_a_w_group, v_a_scale, v_a_w_out, v_kv_norm, v_w_kv, v_k_norm, v_b_norm, v_b_w_in, v_b_q_norm, v_b_w_out, v_ple_w, v_ple_gate_w):
    xi, yi, ci = _place()
    me = 4 * xi + 2 * yi + ci

    big = {
        "a_w_in": a_w_in.reshape(D, 256), "a_w_group": a_w_group.reshape(128, 256),
        "a_w_out": a_w_out.reshape(128, D), "w_kv": w_kv, "b_w_in": b_w_in.reshape(D, 256),
        "b_w_out": b_w_out.reshape(128, D), "ple_w": ple_w, "ple_gate_w": ple_gate_w,
    }
    names = ["a_w_in", "a_w_group", "a_w_out", "w_kv", "b_w_in", "b_w_out", "ple_w0", "ple_w1", "gate0", "gate1"]
    cast = dict(zip(names, _cast_shards(list(big.values()))))
    small = jnp.concatenate([a_norm, a_scale, jnp.zeros((14, 128), F32)], axis=0)
    first = ["a_w_in", "a_w_group", "a_w_out", "ple_w0", "gate0"]
    behind_a = ["w_kv", "b_w_in"]
    behind_attn = ["b_w_out", "ple_w1", "gate1"]
    gathered = _all_gather([cast[k] for k in first] + [small])
    full = dict(zip(first, gathered[:-1]))
    small_all = gathered[-1]
    a_norm_f = small_all[:, 0, :].reshape(1, D)
    a_scale_f = small_all[:, 1, :].reshape(1, D)
    w_a_in = full["a_w_in"]
    w_a_out = full["a_w_out"].reshape(D, D)
    w_gate0 = full["gate0"].reshape(D, D)
    w_ple0 = full["ple_w0"]
    w_group = full["a_w_group"].reshape(N_DEV, 4, 32, 256).transpose(1, 0, 2, 3).reshape(4, 256, 256)
    kvn, bn = kv_norm.reshape(1, D), b_norm
    kg, qg = k_norm.reshape(1, HEAD_DIM), b_q_norm

    x0, p0, p1, target = x[0], p[0, 0], p[1, 0], loss_target[0]
    h0, z, pooled, mcat, y, x1, e0, gt0, x2, w_kv_f, w_b_in = _layer_a_fwd(
        x0, p0, a_norm_f, a_scale_f, w_a_in, w_group, w_a_out, w_ple0, w_gate0,
        rider=_GatherRider([cast[k] for k in behind_a]))
    hkv, hb, k_all, v_all, q_all, zb = _layer_b_in_fwd(x2, kvn, bn, w_kv_f, w_b_in)
    qg2, kg2 = jnp.concatenate([qg, qg], axis=1), jnp.concatenate([kg, kg], axis=1)
    o, csave, w_b_out, w_ple1, w_gate1 = _attn_fwd(
        q_all, k_all, v_all, qg2, kg2, rider=_GatherRider([cast[k] for k in behind_attn]))
    w_b_out, w_gate1 = w_b_out.reshape(D, D), w_gate1.reshape(D, D)
    yb, x3, e1, gt1, dx4, sq_err = _layer_b_out_fwd(o, zb, x2, p1, target, w_b_out, w_ple1, w_gate1)

    de1, dgp1, dx3, d_o, dzb = _layer_b_out_bwd(dx4, e1, gt1, o, zb, w_gate1, w_b_out)
    partial = {
        "b_w_out": _wgrad(yb, dx3, "wgrad_b_w_out").reshape(N_DEV, 128, D),
        "ple_w1": _wgrad(p1, de1, "wgrad_ple_w1", n_split=8),
        "gate1": _wgrad(x3, dgp1, "wgrad_gate1").reshape(N_DEV, 128, D),
    }
    grad = {}
    dq, dk, dv, dqg, dkg, grad["b_w_out"], grad["ple_w1"], grad["gate1"] = _attn_bwd(
        q_all, k_all, v_all, qg2, kg2, d_o, csave,
        rider=_ReduceRider([partial[k] for k in ("b_w_out", "ple_w1", "gate1")]))
    dqz, dkv, dx2, d_bn, d_kvn = _layer_b_in_bwd(dq, dzb, dk, dv, x2, dx3, w_b_in, w_kv_f, bn, kvn)
    partial["w_kv"] = _wgrad(hkv, dkv, "wgrad_w_kv", n_split=8)
    partial["b_w_in"], grad["w_kv"] = _wgrad(hb, dqz, "wgrad_b_w_in", n_split=8,
                                             rider=_ReduceRider([partial["w_kv"]]))
    de0, dgp0, dx1, dm, duz, d_as, grad["b_w_in"] = _layer_a_out_bwd(
        dx2, e0, gt0, z, mcat, w_gate0, w_a_out, a_scale_f, w_group, rider=_ReduceRider([partial["b_w_in"]]))
    partial["gate0"] = _wgrad(x1, dgp0, "wgrad_gate0").reshape(N_DEV, 128, D)
    dw_a_out, grad["gate0"] = _wgrad(y, dx1, "wgrad_a_w_out", rider=_ReduceRider([partial["gate0"]]))
    partial["a_w_out"] = dw_a_out.reshape(N_DEV, 128, D)
    partial["a_w_in"], grad["a_w_out"] = _wgrad(h0, duz, "wgrad_a_w_in", n_split=8,
                                                rider=_ReduceRider([partial["a_w_out"]]))
    dw_group = _wgrad(pooled, dm, "wgrad_a_w_group", a_blocked_b=True)
    partial["a_w_group"] = dw_group.reshape(4, N_DEV, 32, 256).transpose(1, 0, 2, 3).reshape(N_DEV, 128, 256)
    partial["ple_w0"] = _wgrad(p0, de0, "wgrad_ple_w0", n_split=8)
    behind_a_in = ["a_w_in", "a_w_group", "ple_w0"]
    dx0, d_an, *done = _layer_a_in_bwd(duz, x0, dx1, w_a_in, a_norm_f,
                                      rider=_ReduceRider([partial[k] for k in behind_a_in]))
    grad.update(zip(behind_a_in, done))

    given = {
        "a_w_in": (a_w_in, m_a_w_in, v_a_w_in), "a_w_group": (a_w_group, m_a_w_group, v_a_w_group),
        "a_w_out": (a_w_out, m_a_w_out, v_a_w_out), "w_kv": (w_kv, m_w_kv, v_w_kv),
        "b_w_in": (b_w_in, m_b_w_in, v_b_w_in), "b_w_out": (b_w_out, m_b_w_out, v_b_w_out),
        "ple_w": (ple_w, m_ple_w, v_ple_w), "ple_gate_w": (ple_gate_w, m_ple_gate_w, v_ple_gate_w),
    }
    grad["ple_w"] = (grad["ple_w0"], grad["ple_w1"])
    grad["ple_gate_w"] = (grad["gate0"], grad["gate1"])
    updated = _adamw_all(
        "adamw_shards", list(big.values()), [grad[k] for k in big],
        [given[k][1].reshape(big[k].shape) for k in big], [given[k][2].reshape(big[k].shape) for k in big])
    res = {k: tuple(t.reshape(given[k][0].shape) for t in four) for k, four in zip(big, updated)}

    rows = jnp.concatenate([d_kvn, d_bn, d_an, d_as, jnp.zeros((2, D), F32), sq_err, jnp.zeros((1, D), F32)], axis=0)
    tot = _all_reduce_small(rows, (dqg, dkg))
    loss = tot[6, 0]
    small_grad = {
        "kv_norm": tot[0:1], "b_norm": tot[1:2],
        "a_norm": lax.dynamic_slice_in_dim(tot[2:3], me * 128, 128, axis=1),
        "a_scale": lax.dynamic_slice_in_dim(tot[3:4], me * 128, 128, axis=1),
        "b_q_norm": tot[4:5, :HEAD_DIM], "k_norm": tot[5:6, :HEAD_DIM],
    }
    small_given = {
        "a_norm": (a_norm, m_a_norm, v_a_norm), "a_scale": (a_scale, m_a_scale, v_a_scale),
        "kv_norm": (kv_norm, m_kv_norm, v_kv_norm), "k_norm": (k_norm, m_k_norm, v_k_norm),
        "b_norm": (b_norm, m_b_norm, v_b_norm), "b_q_norm": (b_q_norm, m_b_q_norm, v_b_q_norm),
    }
    rows_of = {k: [t.reshape(1, -1) for t in three] for k, three in small_given.items()}
    updated = _adamw_all(
        "adamw_gains", [rows_of[k][0] for k in small_given], [small_grad[k] for k in small_given],
        [rows_of[k][1] for k in small_given], [rows_of[k][2] for k in small_given])
    res.update({k: tuple(t.reshape(small_given[k][0].shape) for t in four) for k, four in zip(small_given, updated)})

    order = ["a_norm", "a_w_in", "a_w_group", "a_scale", "a_w_out", "kv_norm", "w_kv", "k_norm", "b_norm",
             "b_w_in", "b_q_norm", "b_w_out", "ple_w", "ple_gate_w"]
    outs = [res[k][kind] for kind in range(4) for k in order]
    return (loss, dx0.reshape(x.shape), *outs)
```

```python
import functools

import jax
import jax.numpy as jnp
from jax import lax
from jax.experimental import pallas as pl
from jax.experimental.pallas import tpu as pltpu

F32 = jnp.float32
BF16 = jnp.bfloat16
MESH = pl.DeviceIdType.MESH

N_DEV = 8
D = 1024
N_GROUPS = 4
GROUP_DIM = D // N_GROUPS
HALO = 16
HEADS = 16
HEAD_DIM = D // HEADS
SB_SCALE = HEAD_DIM ** -0.5
TILE = 256
LANES = 128
DEAD_LOG = -120.0
EPS = 1e-6
ADAM_LR = 0.001
ADAM_B1 = 0.9
ADAM_B2 = 0.999
ADAM_EPS = 1e-08
ADAM_WD = 0.01
ADAM_STEP = 10
TM = 256
VMEM_LIMIT = 56 * 1024 * 1024

HBM_SPEC = pl.BlockSpec(memory_space=pltpu.HBM)


def _dot(a, b):
    return jnp.dot(a, b, preferred_element_type=F32)


def _dot_nt(a, b):
    return lax.dot_general(a, b, (((1,), (1,)), ((), ())), preferred_element_type=F32)


def _dot_tn(a, b):
    return lax.dot_general(a, b, (((0,), (0,)), ((), ())), preferred_element_type=F32)


def _sigmoid(x):
    return jax.nn.sigmoid(x)


def _split_dot(x, mat):
    hi = x.astype(BF16)
    lo = (x - hi.astype(F32)).astype(BF16)
    return _dot(hi, mat) + _dot(lo, mat)


def _split_dot_many(xs, mat):
    rows = xs[0].shape[0]
    his = [x.astype(BF16) for x in xs]
    los = [(x - hi.astype(F32)).astype(BF16) for x, hi in zip(xs, his)]
    out = _dot(jnp.concatenate(his + los, axis=0), mat)
    n = len(xs)
    return [out[i * rows:(i + 1) * rows] + out[(n + i) * rows:(n + i + 1) * rows] for i in range(n)]


def _rms(x):
    return lax.rsqrt(jnp.mean(x * x, axis=-1, keepdims=True) + EPS)


def _hosted_call(body, name, n_steps, ins, in_specs, out_specs, out_shape, scratch, rider=None):
    ins, scratch = list(ins), list(scratch)
    if rider is None:
        wrapped, extra_in, extra_out, extra_scratch = body, [], [], []
    else:
        extra_in, extra_out, extra_scratch = rider.arrays, rider.out_shape(), rider.scratch()
        n_in, n_out, n_scr = len(ins), len(out_shape), len(scratch)
        k_in, k_out = len(extra_in), len(extra_out)

        def wrapped(*refs):
            own_in, r_in = refs[:n_in], refs[n_in:n_in + k_in]
            own_out = refs[n_in + k_in:n_in + k_in + n_out]
            r_out = refs[n_in + k_in + n_out:n_in + k_in + n_out + k_out]
            rest = refs[n_in + k_in + n_out + k_out:]
            phases = rider.bind(r_in, r_out, rest[n_scr:])
            step = pl.program_id(0)
            pl.when(step == 0)(phases[0])
            body(*own_in, *own_out, *rest[:n_scr])
            for share, phase in zip(rider.WHEN[1:], phases[1:]):
                at = min(n_steps - 1, max(1, round(share * (n_steps - 1))))
                pl.when(step == at)(phase)

    return pl.pallas_call(
        wrapped, name=name, grid=(n_steps,),
        in_specs=list(in_specs) + [HBM_SPEC] * len(extra_in),
        out_specs=list(out_specs) + [HBM_SPEC] * len(extra_out),
        out_shape=list(out_shape) + list(extra_out), scratch_shapes=scratch + list(extra_scratch),
        compiler_params=pltpu.CompilerParams(dimension_semantics=("arbitrary",), vmem_limit_bytes=VMEM_LIMIT),
    )(*ins, *extra_in)


def _rows_call(body, name, n_rows, row_ins, const_ins, row_outs, const_outs=(), scratch=(),
               reverse=False, tm=TM, rider=None):
    nb = n_rows // tm

    def row_map(i):
        return ((nb - 1 - i) if reverse else i, 0)

    def const_map(nd):
        return lambda i: (0,) * nd

    in_specs = [pl.BlockSpec((tm, a.shape[1]), row_map) for a in row_ins]
    in_specs += [pl.BlockSpec(a.shape, const_map(a.ndim)) for a in const_ins]
    out_specs = [pl.BlockSpec((tm, w), row_map) for (w, _) in row_outs]
    out_specs += [pl.BlockSpec(s, const_map(len(s))) for (s, _) in const_outs]
    out_shape = [jax.ShapeDtypeStruct((n_rows, w), dt) for (w, dt) in row_outs]
    out_shape += [jax.ShapeDtypeStruct(s, dt) for (s, dt) in const_outs]
    return _hosted_call(body, name, nb, list(row_ins) + list(const_ins), in_specs, out_specs, out_shape,
                        scratch, rider)


def _ple_fwd(p_ref, xin, wple_ref, wgate_ref, e_ref, gt_ref):
    pb = p_ref[...].astype(BF16)
    for j in range(N_DEV):
        e_ref[:, j * 128:(j + 1) * 128] = _dot(pb, wple_ref[j])
    gt = _sigmoid(_dot(xin.astype(BF16), wgate_ref[...]))
    gt_ref[...] = gt
    return xin + e_ref[...] * gt


def _layer_a_fwd(x0, p0, a_norm, a_scale, w_in, w_group, w_out, w_ple, w_gate, rider=None):
    s = x0.shape[0]
    tm = TM

    def body(x_ref, p_ref, an_ref, as_ref, win_ref, wg_ref, wout_ref, wple_ref, wgate_ref,
             h_ref, z_ref, pooled_ref, m_ref, y_ref, x1_ref, e_ref, gt_ref, x2_ref, uext):
        i = pl.program_id(0)

        @pl.when(i == 0)
        def _():
            uext[0:HALO, :] = jnp.zeros((HALO, D), F32)

        x = x_ref[...]
        h = (x * _rms(x) * an_ref[...]).astype(BF16)
        h_ref[...] = h
        for j in range(N_DEV):
            uz = _dot(h, win_ref[j])
            if j < 4:
                uext[HALO:HALO + tm, j * 256:(j + 1) * 256] = uz
            else:
                z_ref[:, (j - 4) * 256:(j - 3) * 256] = uz
        t = i * tm + lax.broadcasted_iota(jnp.int32, (tm, 1), 0)
        for g in range(N_GROUPS):
            w = 2 ** (g + 1)
            cols = slice(g * GROUP_DIM, (g + 1) * GROUP_DIM)
            ext = uext[:, cols]
            acc = ext
            k = 1
            while k < w:
                acc = acc + pltpu.roll(acc, k, 0)
                k *= 2
            cnt = jnp.minimum(t + 1, w).astype(F32)
            pooled = (acc[HALO:] / cnt - ext[HALO:]).astype(BF16)
            pooled_ref[:, cols] = pooled
            m_ref[:, cols] = _dot(pooled, wg_ref[g])
        uext[0:HALO, :] = uext[tm:tm + HALO, :]
        z = z_ref[...]
        y = (m_ref[...] * as_ref[...] * (z * _sigmoid(z))).astype(BF16)
        y_ref[...] = y
        x1 = x + _dot(y, wout_ref[...])
        x1_ref[...] = x1
        x2_ref[...] = _ple_fwd(p_ref, x1, wple_ref, wgate_ref, e_ref, gt_ref)

    row_outs = [(D, BF16), (D, F32), (D, BF16), (D, F32), (D, BF16), (D, F32), (D, F32), (D, F32), (D, F32)]
    return _rows_call(body, "layer_a_fwd", s, [x0, p0], [a_norm, a_scale, w_in, w_group, w_out, w_ple, w_gate],
                      row_outs, scratch=[pltpu.VMEM((tm + HALO, D), F32)], rider=rider)


def _layer_b_in_fwd(x2, kv_norm, b_norm, w_kv, w_bin):
    s = x2.shape[0]

    def body(x_ref, kvn_ref, bn_ref, wkv_ref, wbin_ref, hkv_ref, hb_ref, k_ref, v_ref, q_ref, zb_ref):
        x = x_ref[...]
        n = x * _rms(x)
        hkv = (n * kvn_ref[...]).astype(BF16)
        hb = (n * bn_ref[...]).astype(BF16)
        hkv_ref[...] = hkv
        hb_ref[...] = hb
        for j in range(N_DEV):
            kv = _dot(hkv, wkv_ref[j])
            qz = _dot(hb, wbin_ref[j])
            if j < 4:
                cols = slice(j * 256, (j + 1) * 256)
                k_ref[:, cols] = kv
                q_ref[:, cols] = qz
            else:
                cols = slice((j - 4) * 256, (j - 3) * 256)
                v_ref[:, cols] = kv.astype(BF16)
                zb_ref[:, cols] = qz

    row_outs = [(D, BF16), (D, BF16), (D, F32), (D, BF16), (D, F32), (D, F32)]
    return _rows_call(body, "layer_b_in_fwd", s, [x2], [kv_norm, b_norm, w_kv, w_bin], row_outs)


def _tri(after):
    r = lax.broadcasted_iota(jnp.int32, (TILE, TILE), 0)
    c = lax.broadcasted_iota(jnp.int32, (TILE, TILE), 1)
    return jnp.where((r > c) if after else (r < c), 1.0, 0.0).astype(BF16)


def _half_sums(v):
    r = lax.broadcasted_iota(jnp.int32, (LANES, LANES), 0) < HEAD_DIM
    c = lax.broadcasted_iota(jnp.int32, (LANES, LANES), 1) < HEAD_DIM
    same_head = jnp.where(r == c, 1.0, 0.0).astype(BF16)
    return _split_dot(v, same_head)


def _pair_norm(x):
    r = lax.rsqrt(_half_sums(x * x) * (1.0 / HEAD_DIM) + EPS)
    return x * r, r


def _tile_logits(qblk, kblk, diagonal):
    l = _dot_nt(qblk, kblk)
    sp = jnp.maximum(l, 0.0) + jnp.log(1.0 + jnp.exp(-jnp.abs(l)))
    ls = l - sp
    if not diagonal:
        return None, -sp, ls
    mask = lax.broadcasted_iota(jnp.int32, l.shape, 1) < lax.broadcasted_iota(jnp.int32, l.shape, 0)
    return mask, jnp.where(mask, -sp, 0.0), ls


def _attn_fwd(q_all, k_all, v_all, q_gain2, k_gain2, rider=None):
    s = q_all.shape[0]
    nt = s // TILE

    def body(q_ref, k_ref, v_ref, qg_ref, kg_ref, o_ref, c_ref, qs, ks, vs, tri, acc, right, cmat):
        tri[...] = _tri(True)
        lane = lax.broadcasted_iota(jnp.int32, (TILE, LANES), 1)
        qn, _ = _pair_norm(q_ref[...])
        kn, _ = _pair_norm(k_ref[...])
        qsc = (qn * qg_ref[...] * SB_SCALE).astype(BF16)
        ksc = (kn * kg_ref[...]).astype(BF16)
        for hh in range(2):
            sl = slice(hh * HEAD_DIM, (hh + 1) * HEAD_DIM)
            qs[hh] = qsc[:, sl]
            ks[hh] = ksc[:, sl]
            vs[hh] = v_ref[:, sl]

        def tile(qrows, kb, diagonal):
            rows = pl.ds(pl.multiple_of(kb * TILE, TILE), TILE)
            loaded = [(qs[hh, qrows, :], ks[hh, rows, :], vs[hh, rows, :], right[hh], cmat[hh], acc[hh])
                      for hh in range(2)]
            logits = [_tile_logits(q, k, diagonal) for q, k, _, _, _, _ in loaded]
            later = _split_dot_many([lk for _, lk, _ in logits], tri[...])
            results = []
            for (q, k, v, rt, cm, ac), (mask, lk, ls), lt in zip(loaded, logits, later):
                a = jnp.exp(ls + lt + rt)
                if diagonal:
                    a = jnp.where(mask, a, 0.0)
                results.append((ac + _dot(a.astype(BF16), v), jnp.where(lane == kb, rt[:, :LANES], cm),
                                rt + jnp.sum(lk, axis=1, keepdims=True)))
            for hh, (ac, cm, rt) in enumerate(results):
                acc[hh] = ac
                cmat[hh] = cm
                right[hh] = rt

        def diagonal_and_left(qrows, qb):
            here = pl.ds(pl.multiple_of(qb * TILE, TILE), TILE)
            left = pl.ds(pl.multiple_of((qb - 1) * TILE, TILE), TILE)
            q = [qs[hh, qrows, :] for hh in range(2)]
            on_diag = [_tile_logits(q[hh], ks[hh, here, :], True) for hh in range(2)]
            beside = [_tile_logits(q[hh], ks[hh, left, :], False) for hh in range(2)]
            later = _split_dot_many([lk for _, lk, _ in on_diag + beside], tri[...])
            for hh in range(2):
                mask, lk_d, ls_d = on_diag[hh]
                _, lk_l, ls_l = beside[hh]
                a_d = jnp.where(mask, jnp.exp(ls_d + later[hh]), 0.0)
                past_diag = jnp.sum(lk_d, axis=1, keepdims=True)
                a_l = jnp.exp(ls_l + later[2 + hh] + past_diag)
                acc[hh] = _dot(a_d.astype(BF16), vs[hh, here, :]) + _dot(a_l.astype(BF16), vs[hh, left, :])
                cmat[hh] = jnp.where(lane == qb - 1, past_diag, 0.0)
                right[hh] = jnp.broadcast_to(past_diag + jnp.sum(lk_l, axis=1, keepdims=True), (TILE, TILE))

        def q_step(qb, _):
            r0 = pl.multiple_of(qb * TILE, TILE)
            qrows = pl.ds(r0, TILE)

            @pl.when(qb == 0)
            def _():
                acc[...] = jnp.zeros((2, TILE, HEAD_DIM), F32)
                right[...] = jnp.zeros((2, TILE, TILE), F32)
                cmat[...] = jnp.zeros((2, TILE, LANES), F32)
                tile(qrows, qb, True)

            pl.when(qb > 0)(lambda: diagonal_and_left(qrows, qb))

            def live():
                return (jnp.max(right[:, :, :LANES]) > DEAD_LOG).astype(jnp.int32)

            def k_step(c):
                kb = c[0] - 1
                tile(qrows, kb, False)
                return kb, live()

            first, _ = lax.while_loop(lambda c: (c[0] > 0) & (c[1] > 0), k_step, (jnp.maximum(qb - 1, 0), live()))
            for hh in range(2):
                o_ref[qrows, hh * HEAD_DIM:(hh + 1) * HEAD_DIM] = acc[hh]
                c_ref[hh, qrows, :] = jnp.where(lane == LANES - 1, first.astype(F32), cmat[hh])
            return 0

        lax.fori_loop(0, nt, q_step, 0)

    pair = pl.BlockSpec((s, LANES), lambda h: (0, h))
    gain = pl.BlockSpec((1, LANES), lambda h: (0, 0))
    return _hosted_call(
        body, "attn_fwd", HEADS // 2, [q_all, k_all, v_all, q_gain2, k_gain2],
        [pair, pair, pair, gain, gain], [pair, pl.BlockSpec((2, s, LANES), lambda h: (h, 0, 0))],
        [jax.ShapeDtypeStruct((s, D), F32), jax.ShapeDtypeStruct((HEADS, s, LANES), F32)],
        [pltpu.VMEM((2, s, HEAD_DIM), BF16)] * 3
        + [pltpu.VMEM((TILE, TILE), BF16), pltpu.VMEM((2, TILE, HEAD_DIM), F32), pltpu.VMEM((2, TILE, TILE), F32),
           pltpu.VMEM((2, TILE, LANES), F32)], rider)


def _layer_b_out_fwd(o, zb, x2, p1, target, w_out, w_ple, w_gate):
    s = o.shape[0]

    def body(o_ref, zb_ref, x2_ref, p_ref, t_ref, wout_ref, wple_ref, wgate_ref,
             yb_ref, x3_ref, e_ref, gt_ref, dx4_ref, loss_ref):
        zb = zb_ref[...]
        yb = (o_ref[...] * (zb * _sigmoid(zb))).astype(BF16)
        yb_ref[...] = yb
        x3 = x2_ref[...] + _dot(yb, wout_ref[...])
        x3_ref[...] = x3
        x4 = _ple_fwd(p_ref, x3, wple_ref, wgate_ref, e_ref, gt_ref)
        d = x4 - t_ref[...]
        dx4_ref[...] = d * (1.0 / D)

        @pl.when(pl.program_id(0) == 0)
        def _():
            loss_ref[...] = jnp.zeros((1, D), F32)

        loss_ref[...] += jnp.sum(d * d, axis=0, keepdims=True)

    row_outs = [(D, BF16), (D, F32), (D, F32), (D, F32), (D, F32)]
    return _rows_call(body, "layer_b_out_fwd", s, [o, zb, x2, p1, target], [w_out, w_ple, w_gate], row_outs,
                      const_outs=[((1, D), F32)])


def _ple_bwd(dxo, e_ref, gt_ref, wgate_ref, de_ref, dgp_ref):
    e = e_ref[...]
    gt = gt_ref[...]
    de_ref[...] = (dxo * gt).astype(BF16)
    dgp = (dxo * e * gt * (1.0 - gt)).astype(BF16)
    dgp_ref[...] = dgp
    return dxo + _dot_nt(dgp, wgate_ref[...])


def _silu_grads(z):
    sg = _sigmoid(z)
    return z * sg, sg * (1.0 + z * (1.0 - sg))


def _layer_b_out_bwd(dx4, e1, gt1, o, zb, w_gate, w_out):
    s = dx4.shape[0]

    def body(dx4_ref, e_ref, gt_ref, o_ref, zb_ref, wgate_ref, wout_ref,
             de_ref, dgp_ref, dx3_ref, do_ref, dzb_ref):
        dx3 = _ple_bwd(dx4_ref[...], e_ref, gt_ref, wgate_ref, de_ref, dgp_ref)
        dx3_ref[...] = dx3
        dyb = _dot_nt(dx3.astype(BF16), wout_ref[...])
        silu, dsilu = _silu_grads(zb_ref[...])
        do_ref[...] = (dyb * silu).astype(BF16)
        dzb_ref[...] = (dyb * o_ref[...] * dsilu).astype(BF16)

    row_outs = [(D, BF16), (D, BF16), (D, F32), (D, BF16), (D, BF16)]
    return _rows_call(body, "layer_b_out_bwd", s, [dx4, e1, gt1, o, zb], [w_gate, w_out], row_outs)


def _attn_bwd(q_all, k_all, v_all, q_gain2, k_gain2, d_o, csave, rider=None):
    s = q_all.shape[0]
    nt = s // TILE

    def body(q_ref, k_ref, v_ref, qg_ref, kg_ref, do_ref, c_ref,
             dq_ref, dk_ref, dv_ref, dqg_ref, dkg_ref,
             qs, ks, vs, dos, qt, dot_t, tri_a, tri_b, dqa, dkt, dvt, dqb, left):
        tri_a[...] = _tri(True)
        tri_b[...] = _tri(False)
        lane = lax.broadcasted_iota(jnp.int32, (TILE, LANES), 1)
        qn, qr = _pair_norm(q_ref[...])
        kn, kr = _pair_norm(k_ref[...])
        qsc = qn * qg_ref[...] * SB_SCALE
        ksc = (kn * kg_ref[...]).astype(BF16)
        q_t = qsc.T.astype(BF16)
        do_t = do_ref[...].astype(F32).T.astype(BF16)
        for j in range(nt):
            qt[j] = q_t[:, j * TILE:(j + 1) * TILE]
            dot_t[j] = do_t[:, j * TILE:(j + 1) * TILE]
        dkt[...] = jnp.zeros((nt, LANES, TILE), F32)
        dvt[...] = jnp.zeros((nt, LANES, TILE), F32)
        qsc = qsc.astype(BF16)
        for hh in range(2):
            sl = slice(hh * HEAD_DIM, (hh + 1) * HEAD_DIM)
            qs[hh] = qsc[:, sl]
            ks[hh] = ksc[:, sl]
            vs[hh] = v_ref[:, sl]
            dos[hh] = do_ref[:, sl]

        def tile(qb, qrows, kb, diagonal):
            rows = pl.ds(pl.multiple_of(kb * TILE, TILE), TILE)
            heads = range(2)
            kblk = [ks[hh, rows, :] for hh in heads]
            logits = [_tile_logits(qs[hh, qrows, :], kblk[hh], diagonal) for hh in heads]
            later = _split_dot_many([lk for _, lk, _ in logits], tri_a[...])
            a, g = [], []
            for hh in heads:
                mask, _, ls = logits[hh]
                right = jnp.sum(jnp.where(lane == kb, c_ref[hh, qrows, :], 0.0), axis=1, keepdims=True)
                a_h = jnp.exp(ls + later[hh] + right)
                a.append(jnp.where(mask, a_h, 0.0) if diagonal else a_h)
                g.append(a[hh] * _dot_nt(dos[hh, qrows, :], vs[hh, rows, :]))
            before = _split_dot_many(g, tri_b[...])
            for hh in heads:
                sl = slice(hh * HEAD_DIM, (hh + 1) * HEAD_DIM)
                mask, _, ls = logits[hh]
                beta = jnp.exp(ls)
                lf = left[hh]
                dl = g[hh] * (1.0 - beta) - (before[hh] + lf) * beta
                if diagonal:
                    dl = jnp.where(mask, dl, 0.0)
                dl = dl.astype(BF16)
                left[hh] = lf + jnp.sum(g[hh], axis=1, keepdims=True)
                dqb[hh] += _dot(dl, kblk[hh])
                dkt[kb, sl, :] += _dot(qt[qb, sl, :], dl)
                dvt[kb, sl, :] += _dot(dot_t[qb, sl, :], a[hh].astype(BF16))

        def left_and_diagonal(qb, qrows):
            here = pl.ds(pl.multiple_of(qb * TILE, TILE), TILE)
            beside = pl.ds(pl.multiple_of((qb - 1) * TILE, TILE), TILE)
            heads = range(2)
            q = [qs[hh, qrows, :] for hh in heads]
            do = [dos[hh, qrows, :] for hh in heads]
            k_d, k_l = [ks[hh, here, :] for hh in heads], [ks[hh, beside, :] for hh in heads]
            on_diag = [_tile_logits(q[hh], k_d[hh], True) for hh in heads]
            on_left = [_tile_logits(q[hh], k_l[hh], False) for hh in heads]
            later = _split_dot_many([lk for _, lk, _ in on_diag + on_left], tri_a[...])
            a_d, a_l, g_d, g_l = [], [], [], []
            for hh in heads:
                mask, lk_d, ls_d = on_diag[hh]
                a_d.append(jnp.where(mask, jnp.exp(ls_d + later[hh]), 0.0))
                a_l.append(jnp.exp(on_left[hh][2] + later[2 + hh] + jnp.sum(lk_d, axis=1, keepdims=True)))
                g_d.append(a_d[hh] * _dot_nt(do[hh], vs[hh, here, :]))
                g_l.append(a_l[hh] * _dot_nt(do[hh], vs[hh, beside, :]))
            before = _split_dot_many(g_l + g_d, tri_b[...])
            for hh in heads:
                sl = slice(hh * HEAD_DIM, (hh + 1) * HEAD_DIM)
                beta_l, beta_d = jnp.exp(on_left[hh][2]), jnp.exp(on_diag[hh][2])
                dl_l = (g_l[hh] * (1.0 - beta_l) - before[hh] * beta_l).astype(BF16)
                carried = jnp.sum(g_l[hh], axis=1, keepdims=True)
                dl_d = g_d[hh] * (1.0 - beta_d) - (before[2 + hh] + carried) * beta_d
                dl_d = jnp.where(on_diag[hh][0], dl_d, 0.0).astype(BF16)
                dqa[qrows, sl] = (_dot(dl_l, k_l[hh]) + _dot(dl_d, k_d[hh])) * SB_SCALE
                dkt[qb - 1, sl, :] += _dot(qt[qb, sl, :], dl_l)
                dkt[qb, sl, :] += _dot(qt[qb, sl, :], dl_d)
                dvt[qb - 1, sl, :] += _dot(dot_t[qb, sl, :], a_l[hh].astype(BF16))
                dvt[qb, sl, :] += _dot(dot_t[qb, sl, :], a_d[hh].astype(BF16))

        def q_step(qb, _):
            qrows = pl.ds(pl.multiple_of(qb * TILE, TILE), TILE)
            first = jnp.max(jnp.where(lane == LANES - 1, c_ref[0, qrows, :], 0.0)).astype(jnp.int32)
            usual = (qb > 0) & (first == qb - 1)

            @pl.when(usual)
            def _():
                left_and_diagonal(qb, qrows)

            @pl.when(jnp.logical_not(usual))
            def _():
                dqb[...] = jnp.zeros((2, TILE, HEAD_DIM), F32)
                left[...] = jnp.zeros((2, TILE, TILE), F32)

                def k_step(kb, _):
                    tile(qb, qrows, kb, False)
                    return 0

                lax.fori_loop(first, qb, k_step, 0)
                tile(qb, qrows, qb, True)
                for hh in range(2):
                    dqa[qrows, hh * HEAD_DIM:(hh + 1) * HEAD_DIM] = dqb[hh] * SB_SCALE

            return 0

        lax.fori_loop(0, nt, q_step, 0)

        def norm_bwd(dy, xn, r, g_ref, dx_ref, dg_ref):
            dg_ref[...] = jnp.sum(dy * xn, axis=0, keepdims=True)
            dxn = dy * g_ref[...]
            dx_ref[...] = r * (dxn - xn * (_half_sums(dxn * xn) * (1.0 / HEAD_DIM)))

        norm_bwd(dqa[...], qn, qr, qg_ref, dq_ref, dqg_ref)
        for j in range(nt):
            dqa[j * TILE:(j + 1) * TILE, :] = dkt[j].T
            dv_ref[j * TILE:(j + 1) * TILE, :] = dvt[j].T
        norm_bwd(dqa[...], kn, kr, kg_ref, dk_ref, dkg_ref)

    pair = pl.BlockSpec((s, LANES), lambda h: (0, h))
    gain = pl.BlockSpec((1, LANES), lambda h: (0, 0))
    dgain = pl.BlockSpec((None, 1, LANES), lambda h: (h, 0, 0))
    return _hosted_call(
        body, "attn_bwd", HEADS // 2, [q_all, k_all, v_all, q_gain2, k_gain2, d_o, csave],
        [pair, pair, pair, gain, gain, pair, pl.BlockSpec((2, s, LANES), lambda h: (h, 0, 0))],
        [pair, pair, pair, dgain, dgain],
        [jax.ShapeDtypeStruct((s, D), F32)] * 3 + [jax.ShapeDtypeStruct((HEADS // 2, 1, LANES), F32)] * 2,
        [pltpu.VMEM((2, s, HEAD_DIM), BF16)] * 4
        + [pltpu.VMEM((nt, LANES, TILE), BF16)] * 2 + [pltpu.VMEM((TILE, TILE), BF16)] * 2
        + [pltpu.VMEM((s, LANES), F32)] + [pltpu.VMEM((nt, LANES, TILE), F32)] * 2
        + [pltpu.VMEM((2, TILE, HEAD_DIM), F32), pltpu.VMEM((2, TILE, TILE), F32)], rider)


def _norm_bwd_rows(dh, x, gain, dgain_ref):
    r = _rms(x)
    n = x * r
    dgain_ref[...] += jnp.sum(dh * n, axis=0, keepdims=True)
    dn = dh * gain
    return r * (dn - n * jnp.mean(dn * n, axis=-1, keepdims=True))


def _layer_b_in_bwd(dq, dzb, dk, dv, x2, dx3, w_bin, w_kv, b_norm, kv_norm):
    s = x2.shape[0]

    def body(dq_ref, dzb_ref, dk_ref, dv_ref, x_ref, dx3_ref, wbin_ref, wkv_ref, bn_ref, kvn_ref,
             dqz_ref, dkv_ref, dx2_ref, dbn_ref, dkvn_ref):
        @pl.when(pl.program_id(0) == 0)
        def _():
            dbn_ref[...] = jnp.zeros((1, D), F32)
            dkvn_ref[...] = jnp.zeros((1, D), F32)

        dqz_ref[:, :D] = dq_ref[...].astype(BF16)
        dqz_ref[:, D:] = dzb_ref[...]
        dkv_ref[:, :D] = dk_ref[...].astype(BF16)
        dkv_ref[:, D:] = dv_ref[...].astype(BF16)
        dhb = jnp.zeros((TM, D), F32)
        dhkv = jnp.zeros((TM, D), F32)
        for j in range(N_DEV):
            cols = slice(j * 256, (j + 1) * 256)
            dhb = dhb + _dot_nt(dqz_ref[:, cols], wbin_ref[j])
            dhkv = dhkv + _dot_nt(dkv_ref[:, cols], wkv_ref[j])
        x = x_ref[...]
        dx2 = dx3_ref[...] + _norm_bwd_rows(dhb, x, bn_ref[...], dbn_ref)
        dx2_ref[...] = dx2 + _norm_bwd_rows(dhkv, x, kvn_ref[...], dkvn_ref)

    row_outs = [(2 * D, BF16), (2 * D, BF16), (D, F32)]
    return _rows_call(body, "layer_b_in_bwd", s, [dq, dzb, dk, dv, x2, dx3], [w_bin, w_kv, b_norm, kv_norm],
                      row_outs, const_outs=[((1, D), F32), ((1, D), F32)])


def _layer_a_out_bwd(dx2, e0, gt0, z, m, w_gate, w_out, a_scale, w_group, rider=None):
    s = dx2.shape[0]
    tm = TM
    nb = s // tm

    def body(dx2_ref, e_ref, gt_ref, z_ref, m_ref, wgate_ref, wout_ref, as_ref, wg_ref,
             de_ref, dgp_ref, dx1_ref, dm_ref, duz_ref, das_ref, ext):
        i = pl.program_id(0)

        @pl.when(i == 0)
        def _():
            das_ref[...] = jnp.zeros((1, D), F32)
            ext[tm:tm + HALO, :] = jnp.zeros((HALO, D), F32)

        dx1 = _ple_bwd(dx2_ref[...], e_ref, gt_ref, wgate_ref, de_ref, dgp_ref)
        dx1_ref[...] = dx1
        dy = _dot_nt(dx1.astype(BF16), wout_ref[...])
        silu, dsilu = _silu_grads(z_ref[...])
        m = m_ref[...]
        dmixed = dy * silu
        duz_ref[:, D:] = (dy * (m * as_ref[...]) * dsilu).astype(BF16)
        das_ref[...] += jnp.sum(dmixed * m, axis=0, keepdims=True)
        dm_ref[...] = (dmixed * as_ref[...]).astype(BF16)
        t = (nb - 1 - i) * tm + lax.broadcasted_iota(jnp.int32, (tm, 1), 0)
        n_ext = tm + HALO
        for g in range(N_GROUPS):
            w = 2 ** (g + 1)
            cols = slice(g * GROUP_DIM, (g + 1) * GROUP_DIM)
            dpool = _dot_nt(dm_ref[:, cols], wg_ref[g])
            ext[0:tm, cols] = dpool / jnp.minimum(t + 1, w).astype(F32)
            acc = ext[:, cols]
            k = 1
            while k < w:
                acc = acc + pltpu.roll(acc, n_ext - k, 0)
                k *= 2
            duz_ref[:, cols] = (acc[:tm] - dpool).astype(BF16)
        ext[tm:tm + HALO, :] = ext[0:HALO, :]

    row_outs = [(D, BF16), (D, BF16), (D, F32), (D, BF16), (2 * D, BF16)]
    return _rows_call(body, "layer_a_out_bwd", s, [dx2, e0, gt0, z, m], [w_gate, w_out, a_scale, w_group],
                      row_outs, const_outs=[((1, D), F32)], scratch=[pltpu.VMEM((tm + HALO, D), F32)],
                      reverse=True, rider=rider)


def _layer_a_in_bwd(duz, x0, dx1, w_in, a_norm, rider=None):
    s = x0.shape[0]

    def body(duz_ref, x_ref, dx1_ref, win_ref, an_ref, dx0_ref, dan_ref):
        @pl.when(pl.program_id(0) == 0)
        def _():
            dan_ref[...] = jnp.zeros((1, D), F32)

        dh = jnp.zeros((TM, D), F32)
        for j in range(N_DEV):
            dh = dh + _dot_nt(duz_ref[:, j * 256:(j + 1) * 256], win_ref[j])
        dx0_ref[...] = dx1_ref[...] + _norm_bwd_rows(dh, x_ref[...], an_ref[...], dan_ref)

    return _rows_call(body, "layer_a_in_bwd", s, [duz, x0, dx1], [w_in, a_norm], [(D, F32)],
                      const_outs=[((1, D), F32)], rider=rider)


def _wgrad(a, b, name, n_split=1, a_blocked_b=False, rider=None):
    s, k = a.shape
    n = b.shape[1]
    tk = 256
    nb = n // n_split

    def body(a_ref, b_ref, o_ref):
        res = _dot_tn(a_ref[...].astype(BF16), b_ref[...].astype(BF16))
        if n_split == 1:
            o_ref[...] = res.astype(BF16)
        else:
            for j in range(n_split):
                o_ref[j] = res[:, j * nb:(j + 1) * nb].astype(BF16)

    if a_blocked_b:
        b_spec = pl.BlockSpec((s, tk), lambda i: (0, i))
        out_spec = pl.BlockSpec((None, tk, tk), lambda i: (i, 0, 0))
        out_shape = jax.ShapeDtypeStruct((k // tk, tk, tk), BF16)
    elif n_split == 1:
        b_spec = pl.BlockSpec((s, n), lambda i: (0, 0))
        out_spec = pl.BlockSpec((tk, n), lambda i: (i, 0))
        out_shape = jax.ShapeDtypeStruct((k, n), BF16)
    else:
        b_spec = pl.BlockSpec((s, n), lambda i: (0, 0))
        out_spec = pl.BlockSpec((n_split, tk, nb), lambda i: (0, i, 0))
        out_shape = jax.ShapeDtypeStruct((n_split, k, nb), BF16)
    res = _hosted_call(body, name, k // tk, [a, b], [pl.BlockSpec((s, tk), lambda i: (0, i)), b_spec],
                       [out_spec], [out_shape], [], rider)
    return res[0] if rider is None else res


def _cast_shards(shards):
    n = len(shards)
    layers = [a.shape[0] if a.ndim == 3 else 0 for a in shards]

    def body(*refs):
        outs = iter(refs[n:])
        for a in range(n):
            if layers[a]:
                for t in range(layers[a]):
                    next(outs)[...] = refs[a][t].astype(BF16)
            else:
                next(outs)[...] = refs[a][...].astype(BF16)

    out_shape = []
    for a, k in zip(shards, layers):
        out_shape += [jax.ShapeDtypeStruct(a.shape[-2:], BF16)] * max(k, 1)
    vmem = pl.BlockSpec(memory_space=pltpu.VMEM)
    return pl.pallas_call(
        body, name="cast_shards", in_specs=[vmem] * n, out_specs=[vmem] * len(out_shape), out_shape=out_shape,
        compiler_params=pltpu.CompilerParams(vmem_limit_bytes=VMEM_LIMIT),
    )(*shards)


def _adamw(w, g, m, v):
    m = ADAM_B1 * m + (1.0 - ADAM_B1) * g
    v = ADAM_B2 * v + (1.0 - ADAM_B2) * jnp.square(g)
    m_hat = m / (1.0 - ADAM_B1 ** ADAM_STEP)
    v_hat = v / (1.0 - ADAM_B2 ** ADAM_STEP)
    delta = -ADAM_LR * (m_hat / (jnp.sqrt(v_hat) + ADAM_EPS) + ADAM_WD * w)
    return delta, m, v


def _place():
    return lax.axis_index("x"), lax.axis_index("y"), lax.axis_index("c")


def _all_gather(shards):
    return _alone("all_gather_weights", _GatherRider(shards))


def _alone(name, rider):
    n_in, n_out = len(rider.arrays), len(rider.out_shape())

    def body(*refs):
        for phase in rider.bind(refs[:n_in], refs[n_in:n_in + n_out], refs[n_in + n_out:]):
            phase()

    return pl.pallas_call(
        body, name=name, in_specs=[HBM_SPEC] * n_in, out_specs=[HBM_SPEC] * n_out,
        out_shape=rider.out_shape(), scratch_shapes=rider.scratch(),
        compiler_params=pltpu.CompilerParams(vmem_limit_bytes=VMEM_LIMIT),
    )(*rider.arrays)


class _GatherRider:
    WHEN = (0.0, 0.7, 1.0)

    def __init__(self, shards):
        self.arrays = list(shards)

    def out_shape(self):
        return _Gather.out_shape(self.arrays)

    def scratch(self):
        return _Gather.semaphores(len(self.arrays))

    def bind(self, ins, outs, scratch):
        moving = _Gather(ins, outs, *scratch)
        return moving.start, moving.forward, moving.finish


class _ReduceRider:
    WHEN = (0.0, 0.15, 0.5, 1.0)

    def __init__(self, partials):
        self.arrays = list(partials)

    def out_shape(self):
        return [jax.ShapeDtypeStruct(a.shape[1:], F32) for a in self.arrays]

    def scratch(self):
        n = len(self.arrays)
        dma = pltpu.SemaphoreType.DMA

        def blocks(k):
            return [pltpu.VMEM((k,) + a.shape[1:], BF16) for a in self.arrays]

        halves = [pltpu.VMEM((2, a.shape[1] // 2) + a.shape[2:], BF16) for a in self.arrays]
        return (blocks(4) + blocks(4) + halves + blocks(2) + [pltpu.VMEM(a.shape[1:], F32) for a in self.arrays]
                + [dma((4 * n,)), dma((4 * n,)), dma((4 * n,)), dma((2 * n,)), dma((2 * n,)),
                   dma((2 * n,)), dma((2 * n,)), dma((n,))])

    def bind(self, ins, outs, scratch):
        n = len(ins)
        mine, landed, halves, arrived, total = (scratch[i * n:(i + 1) * n] for i in range(5))
        send1, recv1, local1, send_h, recv_h, send2, recv2, out_sems = scratch[5 * n:]
        x, y, c = _place()
        plane = 2 * x + y
        via = [(x, 1 - y, c), (1 - x, y, c)]
        nbr = [(1 - x, y, c), (x, 1 - y, c)]
        nbr_block = [2 * (1 - x) + y, 2 * x + (1 - y)]
        diag_block = 2 * (1 - x) + (1 - y)

        def to_sibling(a, k):
            return pltpu.make_async_remote_copy(
                src_ref=ins[a].at[2 * k + (1 - c)], dst_ref=landed[a].at[k],
                send_sem=send1.at[4 * a + k], recv_sem=recv1.at[4 * a + k],
                device_id=(x, y, 1 - c), device_id_type=MESH)

        def own_block(a, k):
            return pltpu.make_async_copy(ins[a].at[2 * k + c], mine[a].at[k], local1.at[4 * a + k])

        def half_of(a, ref, h):
            rows = self.arrays[a].shape[1] // 2
            return ref.at[pl.ds(h * rows, rows)]

        def half_out(a, h):
            return pltpu.make_async_remote_copy(
                src_ref=half_of(a, mine[a].at[diag_block], h), dst_ref=halves[a].at[h],
                send_sem=send_h.at[2 * a + h], recv_sem=recv_h.at[2 * a + h],
                device_id=via[h], device_id_type=MESH)

        def to_owner(a, h):
            return pltpu.make_async_remote_copy(
                src_ref=mine[a].at[nbr_block[h]], dst_ref=arrived[a].at[h],
                send_sem=send2.at[2 * a + h], recv_sem=recv2.at[2 * a + h],
                device_id=nbr[h], device_id_type=MESH)

        def result(a):
            return pltpu.make_async_copy(total[a], outs[a], out_sems.at[a])

        def exchange_cores():
            for a in range(n):
                for k in range(4):
                    to_sibling(a, k).start()
                    own_block(a, k).start()

        def pair_sums():
            for a in range(n):
                for k in range(4):
                    own_block(a, k).wait()
                    to_sibling(a, k).wait_recv()
                total[a][...] = mine[a][plane].astype(F32) + landed[a][plane].astype(F32)
                for k in range(4):
                    mine[a][k] = (mine[a][k].astype(F32) + landed[a][k].astype(F32)).astype(BF16)
                for h in range(2):
                    half_out(a, h).start()

        def fold_and_send():
            for a in range(n):
                rows = self.arrays[a].shape[1] // 2
                for h in range(2):
                    half_out(a, h).wait_recv()
                    part = mine[a].at[nbr_block[h]]
                    span = slice(h * rows, (h + 1) * rows)
                    part[span] = (part[span].astype(F32) + halves[a][h].astype(F32)).astype(BF16)
                    to_owner(a, h).start()

        def finish():
            for a in range(n):
                for h in range(2):
                    to_owner(a, h).wait_recv()
                    total[a][...] += arrived[a][h].astype(F32)
                result(a).start()
            for a in range(n):
                for k in range(4):
                    to_sibling(a, k).wait_send()
                for h in range(2):
                    half_out(a, h).wait_send()
                    to_owner(a, h).wait_send()
                result(a).wait()

        return exchange_cores, pair_sums, fold_and_send, finish


class _Gather:
    COPIES = 9

    def __init__(self, ins, outs, send_sems, recv_sems, local_sems):
        self.ins, self.outs = ins, outs
        self.send_sems, self.recv_sems, self.local_sems = send_sems, recv_sems, local_sems
        self.x, self.y, self.c = _place()

    @staticmethod
    def out_shape(shards):
        return [jax.ShapeDtypeStruct((N_DEV,) + a.shape, a.dtype) for a in shards]

    @staticmethod
    def semaphores(n):
        dma = pltpu.SemaphoreType.DMA
        return [dma((_Gather.COPIES * n,)), dma((_Gather.COPIES * n,)), dma((n,))]

    def _copy(self, a, k, block, to, own=False, half=None):
        px, py, pc = block
        slot = self.outs[a].at[4 * px + 2 * py + pc]
        if half is not None:
            rows = slot.shape[0] // 2
            slot = slot.at[pl.ds(half * rows, rows)]
        return pltpu.make_async_remote_copy(
            src_ref=self.ins[a] if own else slot, dst_ref=slot,
            send_sem=self.send_sems.at[self.COPIES * a + k], recv_sem=self.recv_sems.at[self.COPIES * a + k],
            device_id=to, device_id_type=MESH)

    def _local(self, a):
        return pltpu.make_async_copy(self.ins[a], self.outs[a].at[4 * self.x + 2 * self.y + self.c],
                                     self.local_sems.at[a])

    def _plan(self, a, c):
        x, y = self.x, self.y
        me, sibling = (x, y, c), (x, y, 1 - c)
        xn, yn, dg = (1 - x, y, c), (x, 1 - y, c), (1 - x, 1 - y, c)
        return [
            self._copy(a, 0, me, sibling, own=True), self._copy(a, 1, me, xn, own=True),
            self._copy(a, 2, me, yn, own=True),
            self._copy(a, 3, xn, yn, half=0), self._copy(a, 4, yn, xn, half=1),
            self._copy(a, 5, xn, sibling), self._copy(a, 6, yn, sibling),
            self._copy(a, 7, dg, sibling, half=0), self._copy(a, 8, dg, sibling, half=1),
        ]

    def _arrivals(self, a):
        x, y, c = self.x, self.y, self.c
        me = (x, y, c)
        xn, yn, dg = (1 - x, y, c), (x, 1 - y, c), (1 - x, 1 - y, c)
        other = 1 - c
        return [
            self._copy(a, 0, (x, y, other), me), self._copy(a, 1, xn, me), self._copy(a, 2, yn, me),
            self._copy(a, 3, dg, me, half=0), self._copy(a, 4, dg, me, half=1),
            self._copy(a, 5, (1 - x, y, other), me), self._copy(a, 6, (x, 1 - y, other), me),
            self._copy(a, 7, (1 - x, 1 - y, other), me, half=0), self._copy(a, 8, (1 - x, 1 - y, other), me, half=1),
        ]

    def start(self):
        for a in range(len(self.ins)):
            self._local(a).start()
            for cp in self._plan(a, self.c)[:3]:
                cp.start()

    def forward(self):
        for a in range(len(self.ins)):
            sends, lands = self._plan(a, self.c), self._arrivals(a)
            lands[1].wait_recv()
            sends[3].start()
            sends[5].start()
            lands[2].wait_recv()
            sends[4].start()
            sends[6].start()

    def finish(self):
        n = len(self.ins)
        for a in range(n):
            sends, lands = self._plan(a, self.c), self._arrivals(a)
            lands[3].wait_recv()
            sends[7].start()
            lands[4].wait_recv()
            sends[8].start()
        for a in range(n):
            lands = self._arrivals(a)
            for k in (0, 5, 6, 7, 8):
                lands[k].wait_recv()
        for a in range(n):
            for cp in self._plan(a, self.c):
                cp.wait_send()
            self._local(a).wait()


def _adamw_all(name, ws, gs, ms, vs):
    n = len(ws)
    per_layer = [isinstance(g, tuple) for g in gs]
    flat_g = [part for g in gs for part in (g if isinstance(g, tuple) else (g,))]

    def body(*refs):
        w, refs = refs[:n], refs[n:]
        g, refs = refs[:len(flat_g)], refs[len(flat_g):]
        m, v, outs = refs[:n], refs[n:2 * n], refs[2 * n:]
        stacked = iter(outs[3 * n:])
        parts = iter(g)
        for a in range(n):
            if per_layer[a]:
                whole = next(stacked)
                for t in range(len(gs[a])):
                    grad = next(parts)[...]
                    whole[t] = grad
                    outs[a][t], outs[n + a][t], outs[2 * n + a][t] = _adamw(w[a][t], grad, m[a][t], v[a][t])
            else:
                outs[a][...], outs[n + a][...], outs[2 * n + a][...] = _adamw(
                    w[a][...], next(parts)[...], m[a][...], v[a][...])

    shapes = [jax.ShapeDtypeStruct(a.shape, F32) for a in ws]
    vmem = pl.BlockSpec(memory_space=pltpu.VMEM)
    n_out = 3 * n + sum(per_layer)
    res = pl.pallas_call(
        body, name=name, in_specs=[vmem] * (3 * n + len(flat_g)), out_specs=[vmem] * n_out,
        out_shape=shapes * 3 + [s for s, p in zip(shapes, per_layer) if p],
        compiler_params=pltpu.CompilerParams(vmem_limit_bytes=VMEM_LIMIT),
    )(*ws, *flat_g, *ms, *vs)
    stacked = iter(res[3 * n:])
    return [(next(stacked) if per_layer[a] else gs[a], res[a], res[n + a], res[2 * n + a]) for a in range(n)]


def _all_reduce_small(rows, gain_parts):
    def body(rows_ref, dqg_ref, dkg_ref, out_ref, buf, send_sems, recv_sems):
        x, y, c = _place()
        me = 4 * x + 2 * y + c
        buf[0] = rows_ref[...]
        for row, part in ((4, dqg_ref), (5, dkg_ref)):
            both = jnp.sum(part[...].reshape(HEADS // 2, LANES), axis=0, keepdims=True)
            buf[0, row:row + 1, 0:HEAD_DIM] = both[:, :HEAD_DIM] + both[:, HEAD_DIM:]
        copies = []
        for r in range(1, N_DEV):
            bx, by, bc = (r >> 2) & 1, (r >> 1) & 1, r & 1
            to = (x ^ bx, y ^ by, c ^ bc)
            copies.append(pltpu.make_async_remote_copy(
                src_ref=buf.at[0], dst_ref=buf.at[r], send_sem=send_sems.at[r - 1], recv_sem=recv_sems.at[r - 1],
                device_id=to, device_id_type=MESH))
        for cp in copies:
            cp.start()
        for cp in copies:
            cp.wait_recv()
        for cp in copies:
            cp.wait_send()
        tot = buf[me]
        for j in range(1, N_DEV):
            tot = tot + buf[j ^ me]
        out_ref[...] = tot
        loss = (0.5 / D) * jnp.sum(tot[6:7, :], axis=1, keepdims=True)
        out_ref[6:7, :] = jnp.broadcast_to(loss, (1, D))

    vmem = pl.BlockSpec(memory_space=pltpu.VMEM)
    return pl.pallas_call(
        body, name="all_reduce_small", in_specs=[vmem] * 3, out_specs=vmem,
        out_shape=jax.ShapeDtypeStruct((8, D), F32),
        scratch_shapes=[pltpu.VMEM((N_DEV, 8, D), F32), pltpu.SemaphoreType.DMA((N_DEV - 1,)),
                        pltpu.SemaphoreType.DMA((N_DEV - 1,))],
    )(rows, *gain_parts)


def kernel(x, p, a_norm, a_w_in, a_w_group, a_scale, a_w_out, kv_norm, w_kv, k_norm, b_norm, b_w_in, b_q_norm, b_w_out, ple_w, ple_gate_w, loss_target, m_a_norm, m_a_w_in, m_a_w_group, m_a_scale, m_a_w_out, m_kv_norm, m_w_kv, m_k_norm, m_b_norm, m_b_w_in, m_b_q_norm, m_b_w_out, m_ple_w, m_ple_gate_w, v_a_norm, v_a_w_in, v_a_w_group, v_a_scale, v_a_w_out, v_kv_norm, v_w_kv, v_k_norm, v_b_norm, v_b_w_in, v_b_q_norm, v_b_w_out, v_ple_w, v_ple_gate_w):
    xi, yi, ci = _place()
    me = 4 * xi + 2 * yi + ci

    big = {
        "a_w_in": a_w_in.reshape(D, 256), "a_w_group": a_w_group.reshape(128, 256),
        "a_w_out": a_w_out.reshape(128, D), "w_kv": w_kv, "b_w_in": b_w_in.reshape(D, 256),
        "b_w_out": b_w_out.reshape(128, D), "ple_w": ple_w, "ple_gate_w": ple_gate_w,
    }
    names = ["a_w_in", "a_w_group", "a_w_out", "w_kv", "b_w_in", "b_w_out", "ple_w0", "ple_w1", "gate0", "gate1"]
    cast = dict(zip(names, _cast_shards(list(big.values()))))
    small = jnp.concatenate([a_norm, a_scale, jnp.zeros((14, 128), F32)], axis=0)
    first = ["a_w_in", "a_w_group", "a_w_out", "ple_w0", "gate0"]
    behind_a = ["w_kv", "b_w_in"]
    behind_attn = ["b_w_out", "ple_w1", "gate1"]
    gathered = _all_gather([cast[k] for k in first] + [small])
    full = dict(zip(first, gathered[:-1]))
    small_all = gathered[-1]
    a_norm_f = small_all[:, 0, :].reshape(1, D)
    a_scale_f = small_all[:, 1, :].reshape(1, D)
    w_a_in = full["a_w_in"]
    w_a_out = full["a_w_out"].reshape(D, D)
    w_gate0 = full["gate0"].reshape(D, D)
    w_ple0 = full["ple_w0"]
    w_group = full["a_w_group"].reshape(N_DEV, 4, 32, 256).transpose(1, 0, 2, 3).reshape(4, 256, 256)
    kvn, bn = kv_norm.reshape(1, D), b_norm
    kg, qg = k_norm.reshape(1, HEAD_DIM), b_q_norm

    x0, p0, p1, target = x[0], p[0, 0], p[1, 0], loss_target[0]
    h0, z, pooled, mcat, y, x1, e0, gt0, x2, w_kv_f, w_b_in = _layer_a_fwd(
        x0, p0, a_norm_f, a_scale_f, w_a_in, w_group, w_a_out, w_ple0, w_gate0,
        rider=_GatherRider([cast[k] for k in behind_a]))
    hkv, hb, k_all, v_all, q_all, zb = _layer_b_in_fwd(x2, kvn, bn, w_kv_f, w_b_in)
    qg2, kg2 = jnp.concatenate([qg, qg], axis=1), jnp.concatenate([kg, kg], axis=1)
    o, csave, w_b_out, w_ple1, w_gate1 = _attn_fwd(
        q_all, k_all, v_all, qg2, kg2, rider=_GatherRider([cast[k] for k in behind_attn]))
    w_b_out, w_gate1 = w_b_out.reshape(D, D), w_gate1.reshape(D, D)
    yb, x3, e1, gt1, dx4, sq_err = _layer_b_out_fwd(o, zb, x2, p1, target, w_b_out, w_ple1, w_gate1)

    de1, dgp1, dx3, d_o, dzb = _layer_b_out_bwd(dx4, e1, gt1, o, zb, w_gate1, w_b_out)
    partial = {
        "b_w_out": _wgrad(yb, dx3, "wgrad_b_w_out").reshape(N_DEV, 128, D),
        "ple_w1": _wgrad(p1, de1, "wgrad_ple_w1", n_split=8),
        "gate1": _wgrad(x3, dgp1, "wgrad_gate1").reshape(N_DEV, 128, D),
    }
    grad = {}
    dq, dk, dv, dqg, dkg, grad["b_w_out"], grad["ple_w1"], grad["gate1"] = _attn_bwd(
        q_all, k_all, v_all, qg2, kg2, d_o, csave,
        rider=_ReduceRider([partial[k] for k in ("b_w_out", "ple_w1", "gate1")]))
    dqz, dkv, dx2, d_bn, d_kvn = _layer_b_in_bwd(dq, dzb, dk, dv, x2, dx3, w_b_in, w_kv_f, bn, kvn)
    partial["w_kv"] = _wgrad(hkv, dkv, "wgrad_w_kv", n_split=8)
    partial["b_w_in"], grad["w_kv"] = _wgrad(hb, dqz, "wgrad_b_w_in", n_split=8,
                                             rider=_ReduceRider([partial["w_kv"]]))
    de0, dgp0, dx1, dm, duz, d_as, grad["b_w_in"] = _layer_a_out_bwd(
        dx2, e0, gt0, z, mcat, w_gate0, w_a_out, a_scale_f, w_group, rider=_ReduceRider([partial["b_w_in"]]))
    partial["gate0"] = _wgrad(x1, dgp0, "wgrad_gate0").reshape(N_DEV, 128, D)
    dw_a_out, grad["gate0"] = _wgrad(y, dx1, "wgrad_a_w_out", rider=_ReduceRider([partial["gate0"]]))
    partial["a_w_out"] = dw_a_out.reshape(N_DEV, 128, D)
    partial["a_w_in"], grad["a_w_out"] = _wgrad(h0, duz, "wgrad_a_w_in", n_split=8,
                                                rider=_ReduceRider([partial["a_w_out"]]))
    dw_group = _wgrad(pooled, dm, "wgrad_a_w_group", a_blocked_b=True)
    partial["a_w_group"] = dw_group.reshape(4, N_DEV, 32, 256).transpose(1, 0, 2, 3).reshape(N_DEV, 128, 256)
    partial["ple_w0"] = _wgrad(p0, de0, "wgrad_ple_w0", n_split=8)
    behind_a_in = ["a_w_in", "a_w_group", "ple_w0"]
    dx0, d_an, *done = _layer_a_in_bwd(duz, x0, dx1, w_a_in, a_norm_f,
                                      rider=_ReduceRider([partial[k] for k in behind_a_in]))
    grad.update(zip(behind_a_in, done))

    given = {
        "a_w_in": (a_w_in, m_a_w_in, v_a_w_in), "a_w_group": (a_w_group, m_a_w_group, v_a_w_group),
        "a_w_out": (a_w_out, m_a_w_out, v_a_w_out), "w_kv": (w_kv, m_w_kv, v_w_kv),
        "b_w_in": (b_w_in, m_b_w_in, v_b_w_in), "b_w_out": (b_w_out, m_b_w_out, v_b_w_out),
        "ple_w": (ple_w, m_ple_w, v_ple_w), "ple_gate_w": (ple_gate_w, m_ple_gate_w, v_ple_gate_w),
    }
    grad["ple_w"] = (grad["ple_w0"], grad["ple_w1"])
    grad["ple_gate_w"] = (grad["gate0"], grad["gate1"])
    updated = _adamw_all(
        "adamw_shards", list(big.values()), [grad[k] for k in big],
        [given[k][1].reshape(big[k].shape) for k in big], [given[k][2].reshape(big[k].shape) for k in big])
    res = {k: tuple(t.reshape(given[k][0].shape) for t in four) for k, four in zip(big, updated)}

    rows = jnp.concatenate([d_kvn, d_bn, d_an, d_as, jnp.zeros((2, D), F32), sq_err, jnp.zeros((1, D), F32)], axis=0)
    tot = _all_reduce_small(rows, (dqg, dkg))
    loss = tot[6, 0]
    small_grad = {
        "kv_norm": tot[0:1], "b_norm": tot[1:2],
        "a_norm": lax.dynamic_slice_in_dim(tot[2:3], me * 128, 128, axis=1),
        "a_scale": lax.dynamic_slice_in_dim(tot[3:4], me * 128, 128, axis=1),
        "b_q_norm": tot[4:5, :HEAD_DIM], "k_norm": tot[5:6, :HEAD_DIM],
    }
    small_given = {
        "a_norm": (a_norm, m_a_norm, v_a_norm), "a_scale": (a_scale, m_a_scale, v_a_scale),
        "kv_norm": (kv_norm, m_kv_norm, v_kv_norm), "k_norm": (k_norm, m_k_norm, v_k_norm),
        "b_norm": (b_norm, m_b_norm, v_b_norm), "b_q_norm": (b_q_norm, m_b_q_norm, v_b_q_norm),
    }
    rows_of = {k: [t.reshape(1, -1) for t in three] for k, three in small_given.items()}
    updated = _adamw_all(
        "adamw_gains", [rows_of[k][0] for k in small_given], [small_grad[k] for k in small_given],
        [rows_of[k][1] for k in small_given], [rows_of[k][2] for k in small_given])
    res.update({k: tuple(t.reshape(small_given[k][0].shape) for t in four) for k, four in zip(small_given, updated)})

    order = ["a_norm", "a_w_in", "a_w_group", "a_scale", "a_w_out", "kv_norm", "w_kv", "k_norm", "b_norm",
             "b_w_in", "b_q_norm", "b_w_out", "ple_w", "ple_gate_w"]
    outs = [res[k][kind] for kind in range(4) for k in order]
    return (loss, dx0.reshape(x.shape), *outs)
```

```python
import functools

import jax
import jax.numpy as jnp
from jax import lax
from jax.experimental import pallas as pl
from jax.experimental.pallas import tpu as pltpu

F32 = jnp.float32
BF16 = jnp.bfloat16
MESH = pl.DeviceIdType.MESH

N_DEV = 8
D = 1024
N_GROUPS = 4
GROUP_DIM = D // N_GROUPS
HALO = 16
HEADS = 16
HEAD_DIM = D // HEADS
SB_SCALE = HEAD_DIM ** -0.5
TILE = 256
LANES = 128
DEAD_LOG = -120.0
EPS = 1e-6
ADAM_LR = 0.001
ADAM_B1 = 0.9
ADAM_B2 = 0.999
ADAM_EPS = 1e-08
ADAM_WD = 0.01
ADAM_STEP = 10
TM = 256
VMEM_LIMIT = 56 * 1024 * 1024

HBM_SPEC = pl.BlockSpec(memory_space=pltpu.HBM)


def _dot(a, b):
    return jnp.dot(a, b, preferred_element_type=F32)


def _dot_nt(a, b):
    return lax.dot_general(a, b, (((1,), (1,)), ((), ())), preferred_element_type=F32)


def _dot_tn(a, b):
    return lax.dot_general(a, b, (((0,), (0,)), ((), ())), preferred_element_type=F32)


def _sigmoid(x):
    return jax.nn.sigmoid(x)


def _split_dot(x, mat):
    hi = x.astype(BF16)
    lo = (x - hi.astype(F32)).astype(BF16)
    return _dot(hi, mat) + _dot(lo, mat)


def _split_dot_many(xs, mat):
    rows = xs[0].shape[0]
    his = [x.astype(BF16) for x in xs]
    los = [(x - hi.astype(F32)).astype(BF16) for x, hi in zip(xs, his)]
    out = _dot(jnp.concatenate(his + los, axis=0), mat)
    n = len(xs)
    return [out[i * rows:(i + 1) * rows] + out[(n + i) * rows:(n + i + 1) * rows] for i in range(n)]


def _rms(x):
    return lax.rsqrt(jnp.mean(x * x, axis=-1, keepdims=True) + EPS)


def _hosted_call(body, name, n_steps, ins, in_specs, out_specs, out_shape, scratch, rider=None):
    ins, scratch = list(ins), list(scratch)
    if rider is None:
        wrapped, extra_in, extra_out, extra_scratch = body, [], [], []
    else:
        extra_in, extra_out, extra_scratch = rider.arrays, rider.out_shape(), rider.scratch()
        n_in, n_out, n_scr = len(ins), len(out_shape), len(scratch)
        k_in, k_out = len(extra_in), len(extra_out)

        def wrapped(*refs):
            own_in, r_in = refs[:n_in], refs[n_in:n_in + k_in]
            own_out = refs[n_in + k_in:n_in + k_in + n_out]
            r_out = refs[n_in + k_in + n_out:n_in + k_in + n_out + k_out]
            rest = refs[n_in + k_in + n_out + k_out:]
            phases = rider.bind(r_in, r_out, rest[n_scr:])
            step = pl.program_id(0)
            pl.when(step == 0)(phases[0])
            body(*own_in, *own_out, *rest[:n_scr])
            for share, phase in zip(rider.WHEN[1:], phases[1:]):
                at = min(n_steps - 1, max(1, round(share * (n_steps - 1))))
                pl.when(step == at)(phase)

    return pl.pallas_call(
        wrapped, name=name, grid=(n_steps,),
        in_specs=list(in_specs) + [HBM_SPEC] * len(extra_in),
        out_specs=list(out_specs) + [HBM_SPEC] * len(extra_out),
        out_shape=list(out_shape) + list(extra_out), scratch_shapes=scratch + list(extra_scratch),
        compiler_params=pltpu.CompilerParams(dimension_semantics=("arbitrary",), vmem_limit_bytes=VMEM_LIMIT),
    )(*ins, *extra_in)


def _rows_call(body, name, n_rows, row_ins, const_ins, row_outs, const_outs=(), scratch=(),
               reverse=False, tm=TM, rider=None):
    nb = n_rows // tm

    def row_map(i):
        return ((nb - 1 - i) if reverse else i, 0)

    def const_map(nd):
        return lambda i: (0,) * nd

    in_specs = [pl.BlockSpec((tm, a.shape[1]), row_map) for a in row_ins]
    in_specs += [pl.BlockSpec(a.shape, const_map(a.ndim)) for a in const_ins]
    out_specs = [pl.BlockSpec((tm, w), row_map) for (w, _) in row_outs]
    out_specs += [pl.BlockSpec(s, const_map(len(s))) for (s, _) in const_outs]
    out_shape = [jax.ShapeDtypeStruct((n_rows, w), dt) for (w, dt) in row_outs]
    out_shape += [jax.ShapeDtypeStruct(s, dt) for (s, dt) in const_outs]
    return _hosted_call(body, name, nb, list(row_ins) + list(const_ins), in_specs, out_specs, out_shape,
                        scratch, rider)


def _ple_fwd(p_ref, xin, wple_ref, wgate_ref, e_ref, gt_ref):
    pb = p_ref[...].astype(BF16)
    for j in range(N_DEV):
        e_ref[:, j * 128:(j + 1) * 128] = _dot(pb, wple_ref[j])
    gt = _sigmoid(_dot(xin.astype(BF16), wgate_ref[...]))
    gt_ref[...] = gt
    return xin + e_ref[...] * gt


def _layer_a_fwd(x0, p0, a_norm, a_scale, w_in, w_group, w_out, w_ple, w_gate, rider=None):
    s = x0.shape[0]
    tm = TM

    def body(x_ref, p_ref, an_ref, as_ref, win_ref, wg_ref, wout_ref, wple_ref, wgate_ref,
             h_ref, z_ref, pooled_ref, m_ref, y_ref, x1_ref, e_ref, gt_ref, x2_ref, uext):
        i = pl.program_id(0)

        @pl.when(i == 0)
        def _():
            uext[0:HALO, :] = jnp.zeros((HALO, D), F32)

        x = x_ref[...]
        h = (x * _rms(x) * an_ref[...]).astype(BF16)
        h_ref[...] = h
        for j in range(N_DEV):
            uz = _dot(h, win_ref[j])
            if j < 4:
                uext[HALO:HALO + tm, j * 256:(j + 1) * 256] = uz
            else:
                z_ref[:, (j - 4) * 256:(j - 3) * 256] = uz
        t = i * tm + lax.broadcasted_iota(jnp.int32, (tm, 1), 0)
        for g in range(N_GROUPS):
            w = 2 ** (g + 1)
            cols = slice(g * GROUP_DIM, (g + 1) * GROUP_DIM)
            ext = uext[:, cols]
            acc = ext
            k = 1
            while k < w:
                acc = acc + pltpu.roll(acc, k, 0)
                k *= 2
            cnt = jnp.minimum(t + 1, w).astype(F32)
            pooled = (acc[HALO:] / cnt - ext[HALO:]).astype(BF16)
            pooled_ref[:, cols] = pooled
            m_ref[:, cols] = _dot(pooled, wg_ref[g])
        uext[0:HALO, :] = uext[tm:tm + HALO, :]
        z = z_ref[...]
        y = (m_ref[...] * as_ref[...] * (z * _sigmoid(z))).astype(BF16)
        y_ref[...] = y
        x1 = x + _dot(y, wout_ref[...])
        x1_ref[...] = x1
        x2_ref[...] = _ple_fwd(p_ref, x1, wple_ref, wgate_ref, e_ref, gt_ref)

    row_outs = [(D, BF16), (D, F32), (D, BF16), (D, F32), (D, BF16), (D, F32), (D, F32), (D, F32), (D, F32)]
    return _rows_call(body, "layer_a_fwd", s, [x0, p0], [a_norm, a_scale, w_in, w_group, w_out, w_ple, w_gate],
                      row_outs, scratch=[pltpu.VMEM((tm + HALO, D), F32)], rider=rider)


def _layer_b_in_fwd(x2, kv_norm, b_norm, w_kv, w_bin):
    s = x2.shape[0]

    def body(x_ref, kvn_ref, bn_ref, wkv_ref, wbin_ref, hkv_ref, hb_ref, k_ref, v_ref, q_ref, zb_ref):
        x = x_ref[...]
        n = x * _rms(x)
        hkv = (n * kvn_ref[...]).astype(BF16)
        hb = (n * bn_ref[...]).astype(BF16)
        hkv_ref[...] = hkv
        hb_ref[...] = hb
        for j in range(N_DEV):
            kv = _dot(hkv, wkv_ref[j])
            qz = _dot(hb, wbin_ref[j])
            if j < 4:
                cols = slice(j * 256, (j + 1) * 256)
                k_ref[:, cols] = kv
                q_ref[:, cols] = qz
            else:
                cols = slice((j - 4) * 256, (j - 3) * 256)
                v_ref[:, cols] = kv.astype(BF16)
                zb_ref[:, cols] = qz

    row_outs = [(D, BF16), (D, BF16), (D, F32), (D, BF16), (D, F32), (D, F32)]
    return _rows_call(body, "layer_b_in_fwd", s, [x2], [kv_norm, b_norm, w_kv, w_bin], row_outs)


def _tri(after):
    r = lax.broadcasted_iota(jnp.int32, (TILE, TILE), 0)
    c = lax.broadcasted_iota(jnp.int32, (TILE, TILE), 1)
    return jnp.where((r > c) if after else (r < c), 1.0, 0.0).astype(BF16)


def _half_sums(v):
    r = lax.broadcasted_iota(jnp.int32, (LANES, LANES), 0) < HEAD_DIM
    c = lax.broadcasted_iota(jnp.int32, (LANES, LANES), 1) < HEAD_DIM
    same_head = jnp.where(r == c, 1.0, 0.0).astype(BF16)
    return _split_dot(v, same_head)


def _pair_norm(x):
    r = lax.rsqrt(_half_sums(x * x) * (1.0 / HEAD_DIM) + EPS)
    return x * r, r


def _tile_logits(qblk, kblk, diagonal):
    l = _dot_nt(qblk, kblk)
    sp = jnp.maximum(l, 0.0) + jnp.log(1.0 + jnp.exp(-jnp.abs(l)))
    ls = l - sp
    if not diagonal:
        return None, -sp, ls
    mask = lax.broadcasted_iota(jnp.int32, l.shape, 1) < lax.broadcasted_iota(jnp.int32, l.shape, 0)
    return mask, jnp.where(mask, -sp, 0.0), ls


def _attn_fwd(q_all, k_all, v_all, q_gain2, k_gain2, rider=None):
    s = q_all.shape[0]
    nt = s // TILE

    def body(q_ref, k_ref, v_ref, qg_ref, kg_ref, o_ref, c_ref, qs, ks, vs, tri, acc, right, cmat):
        tri[...] = _tri(True)
        lane = lax.broadcasted_iota(jnp.int32, (TILE, LANES), 1)
        qn, _ = _pair_norm(q_ref[...])
        kn, _ = _pair_norm(k_ref[...])
        qsc = (qn * qg_ref[...] * SB_SCALE).astype(BF16)
        ksc = (kn * kg_ref[...]).astype(BF16)
        for hh in range(2):
            sl = slice(hh * HEAD_DIM, (hh + 1) * HEAD_DIM)
            qs[hh] = qsc[:, sl]
            ks[hh] = ksc[:, sl]
            vs[hh] = v_ref[:, sl]

        def tile(qrows, kb, diagonal):
            rows = pl.ds(pl.multiple_of(kb * TILE, TILE), TILE)
            loaded = [(qs[hh, qrows, :], ks[hh, rows, :], vs[hh, rows, :], right[hh], cmat[hh], acc[hh])
                      for hh in range(2)]
            logits = [_tile_logits(q, k, diagonal) for q, k, _, _, _, _ in loaded]
            later = _split_dot_many([lk for _, lk, _ in logits], tri[...])
            results = []
            for (q, k, v, rt, cm, ac), (mask, lk, ls), lt in zip(loaded, logits, later):
                a = jnp.exp(ls + lt + rt)
                if diagonal:
                    a = jnp.where(mask, a, 0.0)
                results.append((ac + _dot(a.astype(BF16), v), jnp.where(lane == kb, rt[:, :LANES], cm),
                                rt + jnp.sum(lk, axis=1, keepdims=True)))
            for hh, (ac, cm, rt) in enumerate(results):
                acc[hh] = ac
                cmat[hh] = cm
                right[hh] = rt

        def diagonal_and_left(qrows, qb):
            here = pl.ds(pl.multiple_of(qb * TILE, TILE), TILE)
            left = pl.ds(pl.multiple_of((qb - 1) * TILE, TILE), TILE)
            q = [qs[hh, qrows, :] for hh in range(2)]
            on_diag = [_tile_logits(q[hh], ks[hh, here, :], True) for hh in range(2)]
            beside = [_tile_logits(q[hh], ks[hh, left, :], False) for hh in range(2)]
            later = _split_dot_many([lk for _, lk, _ in on_diag + beside], tri[...])
            for hh in range(2):
                mask, lk_d, ls_d = on_diag[hh]
                _, lk_l, ls_l = beside[hh]
                a_d = jnp.where(mask, jnp.exp(ls_d + later[hh]), 0.0)
                past_diag = jnp.sum(lk_d, axis=1, keepdims=True)
                a_l = jnp.exp(ls_l + later[2 + hh] + past_diag)
                acc[hh] = _dot(a_d.astype(BF16), vs[hh, here, :]) + _dot(a_l.astype(BF16), vs[hh, left, :])
                cmat[hh] = jnp.where(lane == qb - 1, past_diag, 0.0)
                right[hh] = jnp.broadcast_to(past_diag + jnp.sum(lk_l, axis=1, keepdims=True), (TILE, TILE))

        def q_step(qb, _):
            r0 = pl.multiple_of(qb * TILE, TILE)
            qrows = pl.ds(r0, TILE)

            @pl.when(qb == 0)
            def _():
                acc[...] = jnp.zeros((2, TILE, HEAD_DIM), F32)
                right[...] = jnp.zeros((2, TILE, TILE), F32)
                cmat[...] = jnp.zeros((2, TILE, LANES), F32)
                tile(qrows, qb, True)

            pl.when(qb > 0)(lambda: diagonal_and_left(qrows, qb))

            def live():
                return (jnp.max(right[:, :, :LANES]) > DEAD_LOG).astype(jnp.int32)

            def k_step(c):
                kb = c[0] - 1
                tile(qrows, kb, False)
                return kb, live()

            first, _ = lax.while_loop(lambda c: (c[0] > 0) & (c[1] > 0), k_step, (jnp.maximum(qb - 1, 0), live()))
            for hh in range(2):
                o_ref[qrows, hh * HEAD_DIM:(hh + 1) * HEAD_DIM] = acc[hh]
                c_ref[hh, qrows, :] = jnp.where(lane == LANES - 1, first.astype(F32), cmat[hh])
            return 0

        lax.fori_loop(0, nt, q_step, 0)

    pair = pl.BlockSpec((s, LANES), lambda h: (0, h))
    gain = pl.BlockSpec((1, LANES), lambda h: (0, 0))
    return _hosted_call(
        body, "attn_fwd", HEADS // 2, [q_all, k_all, v_all, q_gain2, k_gain2],
        [pair, pair, pair, gain, gain], [pair, pl.BlockSpec((2, s, LANES), lambda h: (h, 0, 0))],
        [jax.ShapeDtypeStruct((s, D), F32), jax.ShapeDtypeStruct((HEADS, s, LANES), F32)],
        [pltpu.VMEM((2, s, HEAD_DIM), BF16)] * 3
        + [pltpu.VMEM((TILE, TILE), BF16), pltpu.VMEM((2, TILE, HEAD_DIM), F32), pltpu.VMEM((2, TILE, TILE), F32),
           pltpu.VMEM((2, TILE, LANES), F32)], rider)


def _layer_b_out_fwd(o, zb, x2, p1, target, w_out, w_ple, w_gate):
    s = o.shape[0]

    def body(o_ref, zb_ref, x2_ref, p_ref, t_ref, wout_ref, wple_ref, wgate_ref,
             yb_ref, x3_ref, e_ref, gt_ref, dx4_ref, loss_ref):
        zb = zb_ref[...]
        yb = (o_ref[...] * (zb * _sigmoid(zb))).astype(BF16)
        yb_ref[...] = yb
        x3 = x2_ref[...] + _dot(yb, wout_ref[...])
        x3_ref[...] = x3
        x4 = _ple_fwd(p_ref, x3, wple_ref, wgate_ref, e_ref, gt_ref)
        d = x4 - t_ref[...]
        dx4_ref[...] = d * (1.0 / D)

        @pl.when(pl.program_id(0) == 0)
        def _():
            loss_ref[...] = jnp.zeros((1, D), F32)

        loss_ref[...] += jnp.sum(d * d, axis=0, keepdims=True)

    row_outs = [(D, BF16), (D, F32), (D, F32), (D, F32), (D, F32)]
    return _rows_call(body, "layer_b_out_fwd", s, [o, zb, x2, p1, target], [w_out, w_ple, w_gate], row_outs,
                      const_outs=[((1, D), F32)])


def _ple_bwd(dxo, e_ref, gt_ref, wgate_ref, de_ref, dgp_ref):
    e = e_ref[...]
    gt = gt_ref[...]
    de_ref[...] = (dxo * gt).astype(BF16)
    dgp = (dxo * e * gt * (1.0 - gt)).astype(BF16)
    dgp_ref[...] = dgp
    return dxo + _dot_nt(dgp, wgate_ref[...])


def _silu_grads(z):
    sg = _sigmoid(z)
    return z * sg, sg * (1.0 + z * (1.0 - sg))


def _layer_b_out_bwd(dx4, e1, gt1, o, zb, w_gate, w_out):
    s = dx4.shape[0]

    def body(dx4_ref, e_ref, gt_ref, o_ref, zb_ref, wgate_ref, wout_ref,
             de_ref, dgp_ref, dx3_ref, do_ref, dzb_ref):
        dx3 = _ple_bwd(dx4_ref[...], e_ref, gt_ref, wgate_ref, de_ref, dgp_ref)
        dx3_ref[...] = dx3
        dyb = _dot_nt(dx3.astype(BF16), wout_ref[...])
        silu, dsilu = _silu_grads(zb_ref[...])
        do_ref[...] = (dyb * silu).astype(BF16)
        dzb_ref[...] = (dyb * o_ref[...] * dsilu).astype(BF16)

    row_outs = [(D, BF16), (D, BF16), (D, F32), (D, BF16), (D, BF16)]
    return _rows_call(body, "layer_b_out_bwd", s, [dx4, e1, gt1, o, zb], [w_gate, w_out], row_outs)


def _attn_bwd(q_all, k_all, v_all, q_gain2, k_gain2, d_o, csave, rider=None):
    s = q_all.shape[0]
    nt = s // TILE

    def body(q_ref, k_ref, v_ref, qg_ref, kg_ref, do_ref, c_ref,
             dq_ref, dk_ref, dv_ref, dqg_ref, dkg_ref,
             qs, ks, vs, dos, qt, dot_t, tri_a, tri_b, dqa, dkt, dvt, dqb, left):
        tri_a[...] = _tri(True)
        tri_b[...] = _tri(False)
        lane = lax.broadcasted_iota(jnp.int32, (TILE, LANES), 1)
        qn, qr = _pair_norm(q_ref[...])
        kn, kr = _pair_norm(k_ref[...])
        qsc = qn * qg_ref[...] * SB_SCALE
        ksc = (kn * kg_ref[...]).astype(BF16)
        q_t = qsc.T.astype(BF16)
        do_t = do_ref[...].astype(F32).T.astype(BF16)
        for j in range(nt):
            qt[j] = q_t[:, j * TILE:(j + 1) * TILE]
            dot_t[j] = do_t[:, j * TILE:(j + 1) * TILE]
        dkt[...] = jnp.zeros((nt, LANES, TILE), F32)
        dvt[...] = jnp.zeros((nt, LANES, TILE), F32)
        qsc = qsc.astype(BF16)
        for hh in range(2):
            sl = slice(hh * HEAD_DIM, (hh + 1) * HEAD_DIM)
            qs[hh] = qsc[:, sl]
            ks[hh] = ksc[:, sl]
            vs[hh] = v_ref[:, sl]
            dos[hh] = do_ref[:, sl]

        def tile(qb, qrows, kb, diagonal):
            rows = pl.ds(pl.multiple_of(kb * TILE, TILE), TILE)
            heads = range(2)
            kblk = [ks[hh, rows, :] for hh in heads]
            logits = [_tile_logits(qs[hh, qrows, :], kblk[hh], diagonal) for hh in heads]
            later = _split_dot_many([lk for _, lk, _ in logits], tri_a[...])
            a, g = [], []
            for hh in heads:
                mask, _, ls = logits[hh]
                right = jnp.sum(jnp.where(lane == kb, c_ref[hh, qrows, :], 0.0), axis=1, keepdims=True)
                a_h = jnp.exp(ls + later[hh] + right)
                a.append(jnp.where(mask, a_h, 0.0) if diagonal else a_h)
                g.append(a[hh] * _dot_nt(dos[hh, qrows, :], vs[hh, rows, :]))
            before = _split_dot_many(g, tri_b[...])
            for hh in heads:
                sl = slice(hh * HEAD_DIM, (hh + 1) * HEAD_DIM)
                mask, _, ls = logits[hh]
                beta = jnp.exp(ls)
                lf = left[hh]
                dl = g[hh] * (1.0 - beta) - (before[hh] + lf) * beta
                if diagonal:
                    dl = jnp.where(mask, dl, 0.0)
                dl = dl.astype(BF16)
                left[hh] = lf + jnp.sum(g[hh], axis=1, keepdims=True)
                dqb[hh] += _dot(dl, kblk[hh])
                dkt[kb, sl, :] += _dot(qt[qb, sl, :], dl)
                dvt[kb, sl, :] += _dot(dot_t[qb, sl, :], a[hh].astype(BF16))

        def left_and_diagonal(qb, qrows):
            here = pl.ds(pl.multiple_of(qb * TILE, TILE), TILE)
            beside = pl.ds(pl.multiple_of((qb - 1) * TILE, TILE), TILE)
            heads = range(2)
            q = [qs[hh, qrows, :] for hh in heads]
            do = [dos[hh, qrows, :] for hh in heads]
            k_d, k_l = [ks[hh, here, :] for hh in heads], [ks[hh, beside, :] for hh in heads]
            on_diag = [_tile_logits(q[hh], k_d[hh], True) for hh in heads]
            on_left = [_tile_logits(q[hh], k_l[hh], False) for hh in heads]
            later = _split_dot_many([lk for _, lk, _ in on_diag + on_left], tri_a[...])
            a_d, a_l, g_d, g_l = [], [], [], []
            for hh in heads:
                mask, lk_d, ls_d = on_diag[hh]
                a_d.append(jnp.where(mask, jnp.exp(ls_d + later[hh]), 0.0))
                a_l.append(jnp.exp(on_left[hh][2] + later[2 + hh] + jnp.sum(lk_d, axis=1, keepdims=True)))
                g_d.append(a_d[hh] * _dot_nt(do[hh], vs[hh, here, :]))
                g_l.append(a_l[hh] * _dot_nt(do[hh], vs[hh, beside, :]))
            before = _split_dot_many(g_l + g_d, tri_b[...])
            for hh in heads:
                sl = slice(hh * HEAD_DIM, (hh + 1) * HEAD_DIM)
                beta_l, beta_d = jnp.exp(on_left[hh][2]), jnp.exp(on_diag[hh][2])
                dl_l = (g_l[hh] * (1.0 - beta_l) - before[hh] * beta_l).astype(BF16)
                carried = jnp.sum(g_l[hh], axis=1, keepdims=True)
                dl_d = g_d[hh] * (1.0 - beta_d) - (before[2 + hh] + carried) * beta_d
                dl_d = jnp.where(on_diag[hh][0], dl_d, 0.0).astype(BF16)
                dqa[qrows, sl] = (_dot(dl_l, k_l[hh]) + _dot(dl_d, k_d[hh])) * SB_SCALE
                dkt[qb - 1, sl, :] += _dot(qt[qb, sl, :], dl_l)
                dkt[qb, sl, :] += _dot(qt[qb, sl, :], dl_d)
                dvt[qb - 1, sl, :] += _dot(dot_t[qb, sl, :], a_l[hh].astype(BF16))
                dvt[qb, sl, :] += _dot(dot_t[qb, sl, :], a_d[hh].astype(BF16))

        def q_step(qb, _):
            qrows = pl.ds(pl.multiple_of(qb * TILE, TILE), TILE)
            first = jnp.max(jnp.where(lane == LANES - 1, c_ref[0, qrows, :], 0.0)).astype(jnp.int32)
            usual = (qb > 0) & (first == qb - 1)

            @pl.when(usual)
            def _():
                left_and_diagonal(qb, qrows)

            @pl.when(jnp.logical_not(usual))
            def _():
                dqb[...] = jnp.zeros((2, TILE, HEAD_DIM), F32)
                left[...] = jnp.zeros((2, TILE, TILE), F32)

                def k_step(kb, _):
                    tile(qb, qrows, kb, False)
                    return 0

                lax.fori_loop(first, qb, k_step, 0)
                tile(qb, qrows, qb, True)
                for hh in range(2):
                    dqa[qrows, hh * HEAD_DIM:(hh + 1) * HEAD_DIM] = dqb[hh] * SB_SCALE

            return 0

        lax.fori_loop(0, nt, q_step, 0)

        def norm_bwd(dy, xn, r, g_ref, dx_ref, dg_ref):
            dg_ref[...] = jnp.sum(dy * xn, axis=0, keepdims=True)
            dxn = dy * g_ref[...]
            dx_ref[...] = (r * (dxn - xn * (_half_sums(dxn * xn) * (1.0 / HEAD_DIM)))).astype(BF16)

        norm_bwd(dqa[...], qn, qr, qg_ref, dq_ref, dqg_ref)
        for j in range(nt):
            dqa[j * TILE:(j + 1) * TILE, :] = dkt[j].T
            dv_ref[j * TILE:(j + 1) * TILE, :] = dvt[j].T.astype(BF16)
        norm_bwd(dqa[...], kn, kr, kg_ref, dk_ref, dkg_ref)

    pair = pl.BlockSpec((s, LANES), lambda h: (0, h))
    gain = pl.BlockSpec((1, LANES), lambda h: (0, 0))
    dgain = pl.BlockSpec((None, 1, LANES), lambda h: (h, 0, 0))
    return _hosted_call(
        body, "attn_bwd", HEADS // 2, [q_all, k_all, v_all, q_gain2, k_gain2, d_o, csave],
        [pair, pair, pair, gain, gain, pair, pl.BlockSpec((2, s, LANES), lambda h: (h, 0, 0))],
        [pair, pair, pair, dgain, dgain],
        [jax.ShapeDtypeStruct((s, D), BF16)] * 3 + [jax.ShapeDtypeStruct((HEADS // 2, 1, LANES), F32)] * 2,
        [pltpu.VMEM((2, s, HEAD_DIM), BF16)] * 4
        + [pltpu.VMEM((nt, LANES, TILE), BF16)] * 2 + [pltpu.VMEM((TILE, TILE), BF16)] * 2
        + [pltpu.VMEM((s, LANES), F32)] + [pltpu.VMEM((nt, LANES, TILE), F32)] * 2
        + [pltpu.VMEM((2, TILE, HEAD_DIM), F32), pltpu.VMEM((2, TILE, TILE), F32)], rider)


def _norm_bwd_rows(dh, x, gain, dgain_ref):
    r = _rms(x)
    n = x * r
    dgain_ref[...] += jnp.sum(dh * n, axis=0, keepdims=True)
    dn = dh * gain
    return r * (dn - n * jnp.mean(dn * n, axis=-1, keepdims=True))


def _layer_b_in_bwd(dq, dzb, dk, dv, x2, dx3, w_bin, w_kv, b_norm, kv_norm, rider=None):
    s = x2.shape[0]

    def body(dq_ref, dzb_ref, dk_ref, dv_ref, x_ref, dx3_ref, wbin_ref, wkv_ref, bn_ref, kvn_ref,
             dx2_ref, dbn_ref, dkvn_ref):
        @pl.when(pl.program_id(0) == 0)
        def _():
            dbn_ref[...] = jnp.zeros((1, D), F32)
            dkvn_ref[...] = jnp.zeros((1, D), F32)

        dhb = jnp.zeros((TM, D), F32)
        dhkv = jnp.zeros((TM, D), F32)
        for j in range(N_DEV):
            cols = slice((j % 4) * 256, (j % 4 + 1) * 256)
            dhb = dhb + _dot_nt((dq_ref if j < 4 else dzb_ref)[:, cols], wbin_ref[j])
            dhkv = dhkv + _dot_nt((dk_ref if j < 4 else dv_ref)[:, cols], wkv_ref[j])
        x = x_ref[...]
        dx2 = dx3_ref[...] + _norm_bwd_rows(dhb, x, bn_ref[...], dbn_ref)
        dx2_ref[...] = dx2 + _norm_bwd_rows(dhkv, x, kvn_ref[...], dkvn_ref)

    return _rows_call(body, "layer_b_in_bwd", s, [dq, dzb, dk, dv, x2, dx3], [w_bin, w_kv, b_norm, kv_norm],
                      [(D, F32)], const_outs=[((1, D), F32), ((1, D), F32)], rider=rider)


def _layer_a_out_bwd(dx2, e0, gt0, z, m, w_gate, w_out, a_scale, w_group, rider=None):
    s = dx2.shape[0]
    tm = TM
    nb = s // tm

    def body(dx2_ref, e_ref, gt_ref, z_ref, m_ref, wgate_ref, wout_ref, as_ref, wg_ref,
             de_ref, dgp_ref, dx1_ref, dm_ref, duz_ref, das_ref, ext):
        i = pl.program_id(0)

        @pl.when(i == 0)
        def _():
            das_ref[...] = jnp.zeros((1, D), F32)
            ext[tm:tm + HALO, :] = jnp.zeros((HALO, D), F32)

        dx1 = _ple_bwd(dx2_ref[...], e_ref, gt_ref, wgate_ref, de_ref, dgp_ref)
        dx1_ref[...] = dx1
        dy = _dot_nt(dx1.astype(BF16), wout_ref[...])
        silu, dsilu = _silu_grads(z_ref[...])
        m = m_ref[...]
        dmixed = dy * silu
        duz_ref[:, D:] = (dy * (m * as_ref[...]) * dsilu).astype(BF16)
        das_ref[...] += jnp.sum(dmixed * m, axis=0, keepdims=True)
        dm_ref[...] = (dmixed * as_ref[...]).astype(BF16)
        t = (nb - 1 - i) * tm + lax.broadcasted_iota(jnp.int32, (tm, 1), 0)
        n_ext = tm + HALO
        for g in range(N_GROUPS):
            w = 2 ** (g + 1)
            cols = slice(g * GROUP_DIM, (g + 1) * GROUP_DIM)
            dpool = _dot_nt(dm_ref[:, cols], wg_ref[g])
            ext[0:tm, cols] = dpool / jnp.minimum(t + 1, w).astype(F32)
            acc = ext[:, cols]
            k = 1
            while k < w:
                acc = acc + pltpu.roll(acc, n_ext - k, 0)
                k *= 2
            duz_ref[:, cols] = (acc[:tm] - dpool).astype(BF16)
        ext[tm:tm + HALO, :] = ext[0:HALO, :]

    row_outs = [(D, BF16), (D, BF16), (D, F32), (D, BF16), (2 * D, BF16)]
    return _rows_call(body, "layer_a_out_bwd", s, [dx2, e0, gt0, z, m], [w_gate, w_out, a_scale, w_group],
                      row_outs, const_outs=[((1, D), F32)], scratch=[pltpu.VMEM((tm + HALO, D), F32)],
                      reverse=True, rider=rider)


def _layer_a_in_bwd(duz, x0, dx1, w_in, a_norm, rider=None):
    s = x0.shape[0]

    def body(duz_ref, x_ref, dx1_ref, win_ref, an_ref, dx0_ref, dan_ref):
        @pl.when(pl.program_id(0) == 0)
        def _():
            dan_ref[...] = jnp.zeros((1, D), F32)

        dh = jnp.zeros((TM, D), F32)
        for j in range(N_DEV):
            dh = dh + _dot_nt(duz_ref[:, j * 256:(j + 1) * 256], win_ref[j])
        dx0_ref[...] = dx1_ref[...] + _norm_bwd_rows(dh, x_ref[...], an_ref[...], dan_ref)

    return _rows_call(body, "layer_a_in_bwd", s, [duz, x0, dx1], [w_in, a_norm], [(D, F32)],
                      const_outs=[((1, D), F32)], rider=rider)


def _wgrad(a, b, name, n_split=1, a_blocked_b=False, rider=None):
    bs = list(b) if isinstance(b, (list, tuple)) else [b]
    s, k = a.shape
    n = sum(part.shape[1] for part in bs)
    tk = 256
    nb = n // n_split

    def body(a_ref, *refs):
        o_ref = refs[-1]
        lhs = a_ref[...].astype(BF16)
        done = 0
        for b_ref in refs[:-1]:
            res = _dot_tn(lhs, b_ref[...].astype(BF16))
            if n_split == 1:
                o_ref[...] = res.astype(BF16)
            else:
                for j in range(res.shape[1] // nb):
                    o_ref[done + j] = res[:, j * nb:(j + 1) * nb].astype(BF16)
                done += res.shape[1] // nb

    if a_blocked_b:
        b_specs = [pl.BlockSpec((s, tk), lambda i: (0, i))]
        out_spec = pl.BlockSpec((None, tk, tk), lambda i: (i, 0, 0))
        out_shape = jax.ShapeDtypeStruct((k // tk, tk, tk), BF16)
    elif n_split == 1:
        b_specs = [pl.BlockSpec((s, n), lambda i: (0, 0))]
        out_spec = pl.BlockSpec((tk, n), lambda i: (i, 0))
        out_shape = jax.ShapeDtypeStruct((k, n), BF16)
    else:
        b_specs = [pl.BlockSpec(part.shape, lambda i: (0, 0)) for part in bs]
        out_spec = pl.BlockSpec((n_split, tk, nb), lambda i: (0, i, 0))
        out_shape = jax.ShapeDtypeStruct((n_split, k, nb), BF16)
    res = _hosted_call(body, name, k // tk, [a] + bs, [pl.BlockSpec((s, tk), lambda i: (0, i))] + b_specs,
                       [out_spec], [out_shape], [], rider)
    return res[0] if rider is None else res


def _cast_shards(shards):
    n = len(shards)
    layers = [a.shape[0] if a.ndim == 3 else 0 for a in shards]

    def body(*refs):
        outs = iter(refs[n:])
        for a in range(n):
            if layers[a]:
                for t in range(layers[a]):
                    next(outs)[...] = refs[a][t].astype(BF16)
            else:
                next(outs)[...] = refs[a][...].astype(BF16)

    out_shape = []
    for a, k in zip(shards, layers):
        out_shape += [jax.ShapeDtypeStruct(a.shape[-2:], BF16)] * max(k, 1)
    vmem = pl.BlockSpec(memory_space=pltpu.VMEM)
    return pl.pallas_call(
        body, name="cast_shards", in_specs=[vmem] * n, out_specs=[vmem] * len(out_shape), out_shape=out_shape,
        compiler_params=pltpu.CompilerParams(vmem_limit_bytes=VMEM_LIMIT),
    )(*shards)


def _adamw(w, g, m, v):
    m = ADAM_B1 * m + (1.0 - ADAM_B1) * g
    v = ADAM_B2 * v + (1.0 - ADAM_B2) * jnp.square(g)
    m_hat = m / (1.0 - ADAM_B1 ** ADAM_STEP)
    v_hat = v / (1.0 - ADAM_B2 ** ADAM_STEP)
    delta = -ADAM_LR * (m_hat / (jnp.sqrt(v_hat) + ADAM_EPS) + ADAM_WD * w)
    return delta, m, v


def _place():
    return lax.axis_index("x"), lax.axis_index("y"), lax.axis_index("c")


def _all_gather(shards):
    return _alone("all_gather_weights", _GatherRider(shards))


def _alone(name, rider):
    n_in, n_out = len(rider.arrays), len(rider.out_shape())

    def body(*refs):
        for phase in rider.bind(refs[:n_in], refs[n_in:n_in + n_out], refs[n_in + n_out:]):
            phase()

    return pl.pallas_call(
        body, name=name, in_specs=[HBM_SPEC] * n_in, out_specs=[HBM_SPEC] * n_out,
        out_shape=rider.out_shape(), scratch_shapes=rider.scratch(),
        compiler_params=pltpu.CompilerParams(vmem_limit_bytes=VMEM_LIMIT),
    )(*rider.arrays)


class _GatherRider:
    WHEN = (0.0, 0.7, 1.0)

    def __init__(self, shards):
        self.arrays = list(shards)

    def out_shape(self):
        return _Gather.out_shape(self.arrays)

    def scratch(self):
        return _Gather.semaphores(len(self.arrays))

    def bind(self, ins, outs, scratch):
        moving = _Gather(ins, outs, *scratch)
        return moving.start, moving.forward, moving.finish


class _ReduceRider:
    WHEN = (0.0, 0.15, 0.5, 1.0)

    def __init__(self, partials):
        self.arrays = list(partials)

    def out_shape(self):
        return [jax.ShapeDtypeStruct(a.shape[1:], F32) for a in self.arrays]

    def scratch(self):
        n = len(self.arrays)
        dma = pltpu.SemaphoreType.DMA

        def blocks(k):
            return [pltpu.VMEM((k,) + a.shape[1:], BF16) for a in self.arrays]

        halves = [pltpu.VMEM((2, a.shape[1] // 2) + a.shape[2:], BF16) for a in self.arrays]
        return (blocks(4) + blocks(4) + halves + blocks(2) + [pltpu.VMEM(a.shape[1:], F32) for a in self.arrays]
                + [dma((4 * n,)), dma((4 * n,)), dma((4 * n,)), dma((2 * n,)), dma((2 * n,)),
                   dma((2 * n,)), dma((2 * n,)), dma((n,))])

    def bind(self, ins, outs, scratch):
        n = len(ins)
        mine, landed, halves, arrived, total = (scratch[i * n:(i + 1) * n] for i in range(5))
        send1, recv1, local1, send_h, recv_h, send2, recv2, out_sems = scratch[5 * n:]
        x, y, c = _place()
        plane = 2 * x + y
        via = [(x, 1 - y, c), (1 - x, y, c)]
        nbr = [(1 - x, y, c), (x, 1 - y, c)]
        nbr_block = [2 * (1 - x) + y, 2 * x + (1 - y)]
        diag_block = 2 * (1 - x) + (1 - y)

        def to_sibling(a, k):
            return pltpu.make_async_remote_copy(
                src_ref=ins[a].at[2 * k + (1 - c)], dst_ref=landed[a].at[k],
                send_sem=send1.at[4 * a + k], recv_sem=recv1.at[4 * a + k],
                device_id=(x, y, 1 - c), device_id_type=MESH)

        def own_block(a, k):
            return pltpu.make_async_copy(ins[a].at[2 * k + c], mine[a].at[k], local1.at[4 * a + k])

        def half_of(a, ref, h):
            rows = self.arrays[a].shape[1] // 2
            return ref.at[pl.ds(h * rows, rows)]

        def half_out(a, h):
            return pltpu.make_async_remote_copy(
                src_ref=half_of(a, mine[a].at[diag_block], h), dst_ref=halves[a].at[h],
                send_sem=send_h.at[2 * a + h], recv_sem=recv_h.at[2 * a + h],
                device_id=via[h], device_id_type=MESH)

        def to_owner(a, h):
            return pltpu.make_async_remote_copy(
                src_ref=mine[a].at[nbr_block[h]], dst_ref=arrived[a].at[h],
                send_sem=send2.at[2 * a + h], recv_sem=recv2.at[2 * a + h],
                device_id=nbr[h], device_id_type=MESH)

        def result(a):
            return pltpu.make_async_copy(total[a], outs[a], out_sems.at[a])

        def exchange_cores():
            for a in range(n):
                for k in range(4):
                    to_sibling(a, k).start()
                    own_block(a, k).start()

        def pair_sums():
            for a in range(n):
                for k in range(4):
                    own_block(a, k).wait()
                    to_sibling(a, k).wait_recv()
                total[a][...] = mine[a][plane].astype(F32) + landed[a][plane].astype(F32)
                for k in range(4):
                    mine[a][k] = (mine[a][k].astype(F32) + landed[a][k].astype(F32)).astype(BF16)
                for h in range(2):
                    half_out(a, h).start()

        def fold_and_send():
            for a in range(n):
                rows = self.arrays[a].shape[1] // 2
                for h in range(2):
                    half_out(a, h).wait_recv()
                    part = mine[a].at[nbr_block[h]]
                    span = slice(h * rows, (h + 1) * rows)
                    part[span] = (part[span].astype(F32) + halves[a][h].astype(F32)).astype(BF16)
                    to_owner(a, h).start()

        def finish():
            for a in range(n):
                for h in range(2):
                    to_owner(a, h).wait_recv()
                    total[a][...] += arrived[a][h].astype(F32)
                result(a).start()
            for a in range(n):
                for k in range(4):
                    to_sibling(a, k).wait_send()
                for h in range(2):
                    half_out(a, h).wait_send()
                    to_owner(a, h).wait_send()
                result(a).wait()

        return exchange_cores, pair_sums, fold_and_send, finish


class _Gather:
    COPIES = 9

    def __init__(self, ins, outs, send_sems, recv_sems, local_sems):
        self.ins, self.outs = ins, outs
        self.send_sems, self.recv_sems, self.local_sems = send_sems, recv_sems, local_sems
        self.x, self.y, self.c = _place()

    @staticmethod
    def out_shape(shards):
        return [jax.ShapeDtypeStruct((N_DEV,) + a.shape, a.dtype) for a in shards]

    @staticmethod
    def semaphores(n):
        dma = pltpu.SemaphoreType.DMA
        return [dma((_Gather.COPIES * n,)), dma((_Gather.COPIES * n,)), dma((n,))]

    def _copy(self, a, k, block, to, own=False, half=None):
        px, py, pc = block
        slot = self.outs[a].at[4 * px + 2 * py + pc]
        if half is not None:
            rows = slot.shape[0] // 2
            slot = slot.at[pl.ds(half * rows, rows)]
        return pltpu.make_async_remote_copy(
            src_ref=self.ins[a] if own else slot, dst_ref=slot,
            send_sem=self.send_sems.at[self.COPIES * a + k], recv_sem=self.recv_sems.at[self.COPIES * a + k],
            device_id=to, device_id_type=MESH)

    def _local(self, a):
        return pltpu.make_async_copy(self.ins[a], self.outs[a].at[4 * self.x + 2 * self.y + self.c],
                                     self.local_sems.at[a])

    def _plan(self, a, c):
        x, y = self.x, self.y
        me, sibling = (x, y, c), (x, y, 1 - c)
        xn, yn, dg = (1 - x, y, c), (x, 1 - y, c), (1 - x, 1 - y, c)
        return [
            self._copy(a, 0, me, sibling, own=True), self._copy(a, 1, me, xn, own=True),
            self._copy(a, 2, me, yn, own=True),
            self._copy(a, 3, xn, yn, half=0), self._copy(a, 4, yn, xn, half=1),
            self._copy(a, 5, xn, sibling), self._copy(a, 6, yn, sibling),
            self._copy(a, 7, dg, sibling, half=0), self._copy(a, 8, dg, sibling, half=1),
        ]

    def _arrivals(self, a):
        x, y, c = self.x, self.y, self.c
        me = (x, y, c)
        xn, yn, dg = (1 - x, y, c), (x, 1 - y, c), (1 - x, 1 - y, c)
        other = 1 - c
        return [
            self._copy(a, 0, (x, y, other), me), self._copy(a, 1, xn, me), self._copy(a, 2, yn, me),
            self._copy(a, 3, dg, me, half=0), self._copy(a, 4, dg, me, half=1),
            self._copy(a, 5, (1 - x, y, other), me), self._copy(a, 6, (x, 1 - y, other), me),
            self._copy(a, 7, (1 - x, 1 - y, other), me, half=0), self._copy(a, 8, (1 - x, 1 - y, other), me, half=1),
        ]

    def start(self):
        for a in range(len(self.ins)):
            self._local(a).start()
            for cp in self._plan(a, self.c)[:3]:
                cp.start()

    def forward(self):
        for a in range(len(self.ins)):
            sends, lands = self._plan(a, self.c), self._arrivals(a)
            lands[1].wait_recv()
            sends[3].start()
            sends[5].start()
            lands[2].wait_recv()
            sends[4].start()
            sends[6].start()

    def finish(self):
        n = len(self.ins)
        for a in range(n):
            sends, lands = self._plan(a, self.c), self._arrivals(a)
            lands[3].wait_recv()
            sends[7].start()
            lands[4].wait_recv()
            sends[8].start()
        for a in range(n):
            lands = self._arrivals(a)
            for k in (0, 5, 6, 7, 8):
                lands[k].wait_recv()
        for a in range(n):
            for cp in self._plan(a, self.c):
                cp.wait_send()
            self._local(a).wait()


def _adamw_all(name, ws, gs, ms, vs):
    n = len(ws)
    per_layer = [isinstance(g, tuple) for g in gs]
    flat_g = [part for g in gs for part in (g if isinstance(g, tuple) else (g,))]

    def body(*refs):
        w, refs = refs[:n], refs[n:]
        g, refs = refs[:len(flat_g)], refs[len(flat_g):]
        m, v, outs = refs[:n], refs[n:2 * n], refs[2 * n:]
        stacked = iter(outs[3 * n:])
        parts = iter(g)
        for a in range(n):
            if per_layer[a]:
                whole = next(stacked)
                for t in range(len(gs[a])):
                    grad = next(parts)[...]
                    whole[t] = grad
                    outs[a][t], outs[n + a][t], outs[2 * n + a][t] = _adamw(w[a][t], grad, m[a][t], v[a][t])
            else:
                outs[a][...], outs[n + a][...], outs[2 * n + a][...] = _adamw(
                    w[a][...], next(parts)[...], m[a][...], v[a][...])

    shapes = [jax.ShapeDtypeStruct(a.shape, F32) for a in ws]
    vmem = pl.BlockSpec(memory_space=pltpu.VMEM)
    n_out = 3 * n + sum(per_layer)
    res = pl.pallas_call(
        body, name=name, in_specs=[vmem] * (3 * n + len(flat_g)), out_specs=[vmem] * n_out,
        out_shape=shapes * 3 + [s for s, p in zip(shapes, per_layer) if p],
        compiler_params=pltpu.CompilerParams(vmem_limit_bytes=VMEM_LIMIT),
    )(*ws, *flat_g, *ms, *vs)
    stacked = iter(res[3 * n:])
    return [(next(stacked) if per_layer[a] else gs[a], res[a], res[n + a], res[2 * n + a]) for a in range(n)]


def _all_reduce_small(rows, gain_parts):
    def body(rows_ref, dqg_ref, dkg_ref, out_ref, buf, send_sems, recv_sems):
        x, y, c = _place()
        me = 4 * x + 2 * y + c
        buf[0] = rows_ref[...]
        for row, part in ((4, dqg_ref), (5, dkg_ref)):
            both = jnp.sum(part[...].reshape(HEADS // 2, LANES), axis=0, keepdims=True)
            buf[0, row:row + 1, 0:HEAD_DIM] = both[:, :HEAD_DIM] + both[:, HEAD_DIM:]
        copies = []
        for r in range(1, N_DEV):
            bx, by, bc = (r >> 2) & 1, (r >> 1) & 1, r & 1
            to = (x ^ bx, y ^ by, c ^ bc)
            copies.append(pltpu.make_async_remote_copy(
                src_ref=buf.at[0], dst_ref=buf.at[r], send_sem=send_sems.at[r - 1], recv_sem=recv_sems.at[r - 1],
                device_id=to, device_id_type=MESH))
        for cp in copies:
            cp.start()
        for cp in copies:
            cp.wait_recv()
        for cp in copies:
            cp.wait_send()
        tot = buf[me]
        for j in range(1, N_DEV):
            tot = tot + buf[j ^ me]
        out_ref[...] = tot
        loss = (0.5 / D) * jnp.sum(tot[6:7, :], axis=1, keepdims=True)
        out_ref[6:7, :] = jnp.broadcast_to(loss, (1, D))

    vmem = pl.BlockSpec(memory_space=pltpu.VMEM)
    return pl.pallas_call(
        body, name="all_reduce_small", in_specs=[vmem] * 3, out_specs=vmem,
        out_shape=jax.ShapeDtypeStruct((8, D), F32),
        scratch_shapes=[pltpu.VMEM((N_DEV, 8, D), F32), pltpu.SemaphoreType.DMA((N_DEV - 1,)),
                        pltpu.SemaphoreType.DMA((N_DEV - 1,))],
    )(rows, *gain_parts)


def kernel(x, p, a_norm, a_w_in, a_w_group, a_scale, a_w_out, kv_norm, w_kv, k_norm, b_norm, b_w_in, b_q_norm, b_w_out, ple_w, ple_gate_w, loss_target, m_a_norm, m_a_w_in, m_a_w_group, m_a_scale, m_a_w_out, m_kv_norm, m_w_kv, m_k_norm, m_b_norm, m_b_w_in, m_b_q_norm, m_b_w_out, m_ple_w, m_ple_gate_w, v_a_norm, v_a_w_in, v_a_w_group, v_a_scale, v_a_w_out, v_kv_norm, v_w_kv, v_k_norm, v_b_norm, v_b_w_in, v_b_q_norm, v_b_w_out, v_ple_w, v_ple_gate_w):
    xi, yi, ci = _place()
    me = 4 * xi + 2 * yi + ci

    big = {
        "a_w_in": a_w_in.reshape(D, 256), "a_w_group": a_w_group.reshape(128, 256),
        "a_w_out": a_w_out.reshape(128, D), "w_kv": w_kv, "b_w_in": b_w_in.reshape(D, 256),
        "b_w_out": b_w_out.reshape(128, D), "ple_w": ple_w, "ple_gate_w": ple_gate_w,
    }
    names = ["a_w_in", "a_w_group", "a_w_out", "w_kv", "b_w_in", "b_w_out", "ple_w0", "ple_w1", "gate0", "gate1"]
    cast = dict(zip(names, _cast_shards(list(big.values()))))
    small = jnp.concatenate([a_norm, a_scale, jnp.zeros((14, 128), F32)], axis=0)
    first = ["a_w_in", "a_w_group", "a_w_out", "ple_w0", "gate0"]
    behind_a = ["w_kv", "b_w_in"]
    behind_attn = ["b_w_out", "ple_w1", "gate1"]
    gathered = _all_gather([cast[k] for k in first] + [small])
    full = dict(zip(first, gathered[:-1]))
    small_all = gathered[-1]
    a_norm_f = small_all[:, 0, :].reshape(1, D)
    a_scale_f = small_all[:, 1, :].reshape(1, D)
    w_a_in = full["a_w_in"]
    w_a_out = full["a_w_out"].reshape(D, D)
    w_gate0 = full["gate0"].reshape(D, D)
    w_ple0 = full["ple_w0"]
    w_group = full["a_w_group"].reshape(N_DEV, 4, 32, 256).transpose(1, 0, 2, 3).reshape(4, 256, 256)
    kvn, bn = kv_norm.reshape(1, D), b_norm
    kg, qg = k_norm.reshape(1, HEAD_DIM), b_q_norm

    x0, p0, p1, target = x[0], p[0, 0], p[1, 0], loss_target[0]
    h0, z, pooled, mcat, y, x1, e0, gt0, x2, w_kv_f, w_b_in = _layer_a_fwd(
        x0, p0, a_norm_f, a_scale_f, w_a_in, w_group, w_a_out, w_ple0, w_gate0,
        rider=_GatherRider([cast[k] for k in behind_a]))
    hkv, hb, k_all, v_all, q_all, zb = _layer_b_in_fwd(x2, kvn, bn, w_kv_f, w_b_in)
    qg2, kg2 = jnp.concatenate([qg, qg], axis=1), jnp.concatenate([kg, kg], axis=1)
    o, csave, w_b_out, w_ple1, w_gate1 = _attn_fwd(
        q_all, k_all, v_all, qg2, kg2, rider=_GatherRider([cast[k] for k in behind_attn]))
    w_b_out, w_gate1 = w_b_out.reshape(D, D), w_gate1.reshape(D, D)
    yb, x3, e1, gt1, dx4, sq_err = _layer_b_out_fwd(o, zb, x2, p1, target, w_b_out, w_ple1, w_gate1)

    de1, dgp1, dx3, d_o, dzb = _layer_b_out_bwd(dx4, e1, gt1, o, zb, w_gate1, w_b_out)
    partial = {
        "b_w_out": _wgrad(yb, dx3, "wgrad_b_w_out").reshape(N_DEV, 128, D),
        "ple_w1": _wgrad(p1, de1, "wgrad_ple_w1", n_split=8),
        "gate1": _wgrad(x3, dgp1, "wgrad_gate1").reshape(N_DEV, 128, D),
    }
    grad = {}
    dq, dk, dv, dqg, dkg, grad["b_w_out"], grad["ple_w1"], grad["gate1"] = _attn_bwd(
        q_all, k_all, v_all, qg2, kg2, d_o, csave,
        rider=_ReduceRider([partial[k] for k in ("b_w_out", "ple_w1", "gate1")]))
    partial["w_kv"] = _wgrad(hkv, [dk, dv], "wgrad_w_kv", n_split=8)
    partial["b_w_in"] = _wgrad(hb, [dq, dzb], "wgrad_b_w_in", n_split=8)
    dx2, d_bn, d_kvn, grad["w_kv"] = _layer_b_in_bwd(
        dq, dzb, dk, dv, x2, dx3, w_b_in, w_kv_f, bn, kvn, rider=_ReduceRider([partial["w_kv"]]))
    de0, dgp0, dx1, dm, duz, d_as, grad["b_w_in"] = _layer_a_out_bwd(
        dx2, e0, gt0, z, mcat, w_gate0, w_a_out, a_scale_f, w_group, rider=_ReduceRider([partial["b_w_in"]]))
    partial["gate0"] = _wgrad(x1, dgp0, "wgrad_gate0").reshape(N_DEV, 128, D)
    dw_a_out, grad["gate0"] = _wgrad(y, dx1, "wgrad_a_w_out", rider=_ReduceRider([partial["gate0"]]))
    partial["a_w_out"] = dw_a_out.reshape(N_DEV, 128, D)
    partial["a_w_in"], grad["a_w_out"] = _wgrad(h0, duz, "wgrad_a_w_in", n_split=8,
                                                rider=_ReduceRider([partial["a_w_out"]]))
    dw_group = _wgrad(pooled, dm, "wgrad_a_w_group", a_blocked_b=True)
    partial["a_w_group"] = dw_group.reshape(4, N_DEV, 32, 256).transpose(1, 0, 2, 3).reshape(N_DEV, 128, 256)
    partial["ple_w0"] = _wgrad(p0, de0, "wgrad_ple_w0", n_split=8)
    behind_a_in = ["a_w_in", "a_w_group", "ple_w0"]
    dx0, d_an, *done = _layer_a_in_bwd(duz, x0, dx1, w_a_in, a_norm_f,
                                      rider=_ReduceRider([partial[k] for k in behind_a_in]))
    grad.update(zip(behind_a_in, done))

    given = {
        "a_w_in": (a_w_in, m_a_w_in, v_a_w_in), "a_w_group": (a_w_group, m_a_w_group, v_a_w_group),
        "a_w_out": (a_w_out, m_a_w_out, v_a_w_out), "w_kv": (w_kv, m_w_kv, v_w_kv),
        "b_w_in": (b_w_in, m_b_w_in, v_b_w_in), "b_w_out": (b_w_out, m_b_w_out, v_b_w_out),
        "ple_w": (ple_w, m_ple_w, v_ple_w), "ple_gate_w": (ple_gate_w, m_ple_gate_w, v_ple_gate_w),
    }
    grad["ple_w"] = (grad["ple_w0"], grad["ple_w1"])
    grad["ple_gate_w"] = (grad["gate0"], grad["gate1"])
    updated = _adamw_all(
        "adamw_shards", list(big.values()), [grad[k] for k in big],
        [given[k][1].reshape(big[k].shape) for k in big], [given[k][2].reshape(big[k].shape) for k in big])
    res = {k: tuple(t.reshape(given[k][0].shape) for t in four) for k, four in zip(big, updated)}

    rows = jnp.concatenate([d_kvn, d_bn, d_an, d_as, jnp.zeros((2, D), F32), sq_err, jnp.zeros((1, D), F32)], axis=0)
    tot = _all_reduce_small(rows, (dqg, dkg))
    loss = tot[6, 0]
    small_grad = {
        "kv_norm": tot[0:1], "b_norm": tot[1:2],
        "a_norm": lax.dynamic_slice_in_dim(tot[2:3], me * 128, 128, axis=1),
        "a_scale": lax.dynamic_slice_in_dim(tot[3:4], me * 128, 128, axis=1),
        "b_q_norm": tot[4:5, :HEAD_DIM], "k_norm": tot[5:6, :HEAD_DIM],
    }
    small_given = {
        "a_norm": (a_norm, m_a_norm, v_a_norm), "a_scale": (a_scale, m_a_scale, v_a_scale),
        "kv_norm": (kv_norm, m_kv_norm, v_kv_norm), "k_norm": (k_norm, m_k_norm, v_k_norm),
        "b_norm": (b_norm, m_b_norm, v_b_norm), "b_q_norm": (b_q_norm, m_b_q_norm, v_b_q_norm),
    }
    rows_of = {k: [t.reshape(1, -1) for t in three] for k, three in small_given.items()}
    updated = _adamw_all(
        "adamw_gains", [rows_of[k][0] for k in small_given], [small_grad[k] for k in small_given],
        [rows_of[k][1] for k in small_given], [rows_of[k][2] for k in small_given])
    res.update({k: tuple(t.reshape(small_given[k][0].shape) for t in four) for k, four in zip(small_given, updated)})

    order = ["a_norm", "a_w_in", "a_w_group", "a_scale", "a_w_out", "kv_norm", "w_kv", "k_norm", "b_norm",
             "b_w_in", "b_q_norm", "b_w_out", "ple_w", "ple_gate_w"]
    outs = [res[k][kind] for kind in range(4) for k in order]
    return (loss, dx0.reshape(x.shape), *outs)
```

```python
import jax
import jax.numpy as jnp
from jax import lax
from jax.experimental import pallas as pl
from jax.experimental.pallas import tpu as pltpu

F32 = jnp.float32
BF16 = jnp.bfloat16
MESH = pl.DeviceIdType.MESH

N_DEV = 8
D = 1024
N_GROUPS = 4
GROUP_DIM = D // N_GROUPS
HALO = 16
HEADS = 16
HEAD_DIM = D // HEADS
SB_SCALE = HEAD_DIM ** -0.5
TILE = 256
LANES = 128
DEAD_LOG = -120.0
EPS = 1e-6
ADAM_LR = 0.001
ADAM_B1 = 0.9
ADAM_B2 = 0.999
ADAM_EPS = 1e-08
ADAM_WD = 0.01
ADAM_STEP = 10
TM = 256
COLS = 2 * D // N_DEV
ROWS = D // N_DEV
GROUP_ROWS = GROUP_DIM // N_DEV
VMEM_LIMIT = 56 * 1024 * 1024

HBM_SPEC = pl.BlockSpec(memory_space=pltpu.HBM)


def _dot(a, b):
    return jnp.dot(a, b, preferred_element_type=F32)


def _dot_nt(a, b):
    return lax.dot_general(a, b, (((1,), (1,)), ((), ())), preferred_element_type=F32)


def _dot_tn(a, b):
    return lax.dot_general(a, b, (((0,), (0,)), ((), ())), preferred_element_type=F32)


def _sigmoid(x):
    return jax.nn.sigmoid(x)


def _split_dot(x, mat):
    hi = x.astype(BF16)
    lo = (x - hi.astype(F32)).astype(BF16)
    return _dot(hi, mat) + _dot(lo, mat)


def _split_dot_many(xs, mat):
    rows = xs[0].shape[0]
    his = [x.astype(BF16) for x in xs]
    los = [(x - hi.astype(F32)).astype(BF16) for x, hi in zip(xs, his)]
    out = _dot(jnp.concatenate(his + los, axis=0), mat)
    n = len(xs)
    return [out[i * rows:(i + 1) * rows] + out[(n + i) * rows:(n + i + 1) * rows] for i in range(n)]


def _rms(x):
    return lax.rsqrt(jnp.mean(x * x, axis=-1, keepdims=True) + EPS)


def _hosted_call(body, name, n_steps, ins, in_specs, out_specs, out_shape, scratch, rider=None):
    ins, scratch = list(ins), list(scratch)
    if rider is None:
        wrapped, extra_in, extra_out, extra_scratch = body, [], [], []
    else:
        extra_in, extra_out, extra_scratch = rider.arrays, rider.out_shape(), rider.scratch()
        n_in, n_out, n_scr = len(ins), len(out_shape), len(scratch)
        k_in, k_out = len(extra_in), len(extra_out)

        def wrapped(*refs):
            own_in, r_in = refs[:n_in], refs[n_in:n_in + k_in]
            own_out = refs[n_in + k_in:n_in + k_in + n_out]
            r_out = refs[n_in + k_in + n_out:n_in + k_in + n_out + k_out]
            rest = refs[n_in + k_in + n_out + k_out:]
            phases = rider.bind(r_in, r_out, rest[n_scr:])
            step = pl.program_id(0)
            pl.when(step == 0)(phases[0])
            body(*own_in, *own_out, *rest[:n_scr])
            for share, phase in zip(rider.WHEN[1:], phases[1:]):
                at = min(n_steps - 1, max(1, round(share * (n_steps - 1))))
                pl.when(step == at)(phase)

    return pl.pallas_call(
        wrapped, name=name, grid=(n_steps,),
        in_specs=list(in_specs) + [HBM_SPEC] * len(extra_in),
        out_specs=list(out_specs) + [HBM_SPEC] * len(extra_out),
        out_shape=list(out_shape) + list(extra_out), scratch_shapes=scratch + list(extra_scratch),
        compiler_params=pltpu.CompilerParams(dimension_semantics=("arbitrary",), vmem_limit_bytes=VMEM_LIMIT),
    )(*ins, *extra_in)


def _rows_call(body, name, n_rows, row_ins, const_ins, row_outs, const_outs=(), scratch=(),
               reverse=False, tm=TM, rider=None):
    nb = n_rows // tm

    def row_map(i):
        return ((nb - 1 - i) if reverse else i, 0)

    def const_map(nd):
        return lambda i: (0,) * nd

    in_specs = [pl.BlockSpec((tm, a.shape[1]), row_map) for a in row_ins]
    in_specs += [pl.BlockSpec(a.shape, const_map(a.ndim)) for a in const_ins]
    out_specs = [pl.BlockSpec((tm, w), row_map) for (w, _) in row_outs]
    out_specs += [pl.BlockSpec(s, const_map(len(s))) for (s, _) in const_outs]
    out_shape = [jax.ShapeDtypeStruct((n_rows, w), dt) for (w, dt) in row_outs]
    out_shape += [jax.ShapeDtypeStruct(s, dt) for (s, dt) in const_outs]
    return _hosted_call(body, name, nb, list(row_ins) + list(const_ins), in_specs, out_specs, out_shape,
                        scratch, rider)


def _ple_fwd(p_ref, xin, wple_ref, wgate_ref, e_ref, gt_ref):
    pb = p_ref[...].astype(BF16)
    for j in range(N_DEV):
        e_ref[:, j * ROWS:(j + 1) * ROWS] = _dot(pb, wple_ref[j])
    gt = _sigmoid(_dot(xin.astype(BF16), wgate_ref[...]))
    gt_ref[...] = gt
    return xin + e_ref[...] * gt


def _layer_a_fwd(x0, p0, a_norm, a_scale, w_in, w_group, w_out, w_ple, w_gate, rider=None):
    s = x0.shape[0]
    tm = TM

    def body(x_ref, p_ref, an_ref, as_ref, win_ref, wg_ref, wout_ref, wple_ref, wgate_ref,
             h_ref, z_ref, pooled_ref, m_ref, y_ref, x1_ref, e_ref, gt_ref, x2_ref, uext):
        i = pl.program_id(0)

        @pl.when(i == 0)
        def _():
            uext[0:HALO, :] = jnp.zeros((HALO, D), F32)

        x = x_ref[...]
        h = (x * _rms(x) * an_ref[...]).astype(BF16)
        h_ref[...] = h
        for j in range(N_DEV):
            uz = _dot(h, win_ref[j])
            if j < 4:
                uext[HALO:HALO + tm, j * COLS:(j + 1) * COLS] = uz
            else:
                z_ref[:, (j - 4) * COLS:(j - 3) * COLS] = uz
        t = i * tm + lax.broadcasted_iota(jnp.int32, (tm, 1), 0)
        for g in range(N_GROUPS):
            w = 2 ** (g + 1)
            cols = slice(g * GROUP_DIM, (g + 1) * GROUP_DIM)
            ext = uext[:, cols]
            acc = ext
            k = 1
            while k < w:
                acc = acc + pltpu.roll(acc, k, 0)
                k *= 2
            cnt = jnp.minimum(t + 1, w).astype(F32)
            pooled = (acc[HALO:] / cnt - ext[HALO:]).astype(BF16)
            pooled_ref[:, cols] = pooled
            m_ref[:, cols] = _dot(pooled, wg_ref[g])
        uext[0:HALO, :] = uext[tm:tm + HALO, :]
        z = z_ref[...]
        y = (m_ref[...] * as_ref[...] * (z * _sigmoid(z))).astype(BF16)
        y_ref[...] = y
        x1 = x + _dot(y, wout_ref[...])
        x1_ref[...] = x1
        x2_ref[...] = _ple_fwd(p_ref, x1, wple_ref, wgate_ref, e_ref, gt_ref)

    row_outs = [(D, BF16), (D, F32), (D, BF16), (D, F32), (D, BF16), (D, F32), (D, F32), (D, F32), (D, F32)]
    return _rows_call(body, "layer_a_fwd", s, [x0, p0], [a_norm, a_scale, w_in, w_group, w_out, w_ple, w_gate],
                      row_outs, scratch=[pltpu.VMEM((tm + HALO, D), F32)], rider=rider)


def _layer_b_in_fwd(x2, kv_norm, b_norm, w_kv, w_bin):
    s = x2.shape[0]

    def body(x_ref, kvn_ref, bn_ref, wkv_ref, wbin_ref, hkv_ref, hb_ref, k_ref, v_ref, q_ref, zb_ref):
        x = x_ref[...]
        n = x * _rms(x)
        hkv = (n * kvn_ref[...]).astype(BF16)
        hb = (n * bn_ref[...]).astype(BF16)
        hkv_ref[...] = hkv
        hb_ref[...] = hb
        for j in range(N_DEV):
            kv = _dot(hkv, wkv_ref[j])
            qz = _dot(hb, wbin_ref[j])
            if j < 4:
                cols = slice(j * COLS, (j + 1) * COLS)
                k_ref[:, cols] = kv
                q_ref[:, cols] = qz
            else:
                cols = slice((j - 4) * COLS, (j - 3) * COLS)
                v_ref[:, cols] = kv.astype(BF16)
                zb_ref[:, cols] = qz

    row_outs = [(D, BF16), (D, BF16), (D, F32), (D, BF16), (D, F32), (D, F32)]
    return _rows_call(body, "layer_b_in_fwd", s, [x2], [kv_norm, b_norm, w_kv, w_bin], row_outs)


def _tri(after):
    r = lax.broadcasted_iota(jnp.int32, (TILE, TILE), 0)
    c = lax.broadcasted_iota(jnp.int32, (TILE, TILE), 1)
    return jnp.where((r > c) if after else (r < c), 1.0, 0.0).astype(BF16)


def _half_sums(v):
    r = lax.broadcasted_iota(jnp.int32, (LANES, LANES), 0) < HEAD_DIM
    c = lax.broadcasted_iota(jnp.int32, (LANES, LANES), 1) < HEAD_DIM
    same_head = jnp.where(r == c, 1.0, 0.0).astype(BF16)
    return _split_dot(v, same_head)


def _pair_norm(x):
    r = lax.rsqrt(_half_sums(x * x) * (1.0 / HEAD_DIM) + EPS)
    return x * r, r


def _tile_logits(qblk, kblk, diagonal):
    l = _dot_nt(qblk, kblk)
    sp = jnp.maximum(l, 0.0) + jnp.log(1.0 + jnp.exp(-jnp.abs(l)))
    ls = l - sp
    if not diagonal:
        return None, -sp, ls
    mask = lax.broadcasted_iota(jnp.int32, l.shape, 1) < lax.broadcasted_iota(jnp.int32, l.shape, 0)
    return mask, jnp.where(mask, -sp, 0.0), ls


def _attn_fwd(q_all, k_all, v_all, q_gain2, k_gain2, rider=None):
    s = q_all.shape[0]
    nt = s // TILE

    def body(q_ref, k_ref, v_ref, qg_ref, kg_ref, o_ref, c_ref, qs, ks, vs, tri, acc, right, cmat):
        tri[...] = _tri(True)
        lane = lax.broadcasted_iota(jnp.int32, (TILE, LANES), 1)
        qn, _ = _pair_norm(q_ref[...])
        kn, _ = _pair_norm(k_ref[...])
        qsc = (qn * qg_ref[...] * SB_SCALE).astype(BF16)
        ksc = (kn * kg_ref[...]).astype(BF16)
        for hh in range(2):
            sl = slice(hh * HEAD_DIM, (hh + 1) * HEAD_DIM)
            qs[hh] = qsc[:, sl]
            ks[hh] = ksc[:, sl]
            vs[hh] = v_ref[:, sl]

        def tile(qrows, kb, diagonal):
            rows = pl.ds(pl.multiple_of(kb * TILE, TILE), TILE)
            loaded = [(qs[hh, qrows, :], ks[hh, rows, :], vs[hh, rows, :], right[hh], cmat[hh], acc[hh])
                      for hh in range(2)]
            logits = [_tile_logits(q, k, diagonal) for q, k, _, _, _, _ in loaded]
            later = _split_dot_many([lk for _, lk, _ in logits], tri[...])
            results = []
            for (q, k, v, rt, cm, ac), (mask, lk, ls), lt in zip(loaded, logits, later):
                a = jnp.exp(ls + lt + rt)
                if diagonal:
                    a = jnp.where(mask, a, 0.0)
                results.append((ac + _dot(a.astype(BF16), v), jnp.where(lane == kb, rt[:, :LANES], cm),
                                rt + jnp.sum(lk, axis=1, keepdims=True)))
            for hh, (ac, cm, rt) in enumerate(results):
                acc[hh] = ac
                cmat[hh] = cm
                right[hh] = rt

        def diagonal_and_left(qrows, qb):
            here = pl.ds(pl.multiple_of(qb * TILE, TILE), TILE)
            left = pl.ds(pl.multiple_of((qb - 1) * TILE, TILE), TILE)
            q = [qs[hh, qrows, :] for hh in range(2)]
            on_diag = [_tile_logits(q[hh], ks[hh, here, :], True) for hh in range(2)]
            beside = [_tile_logits(q[hh], ks[hh, left, :], False) for hh in range(2)]
            later = _split_dot_many([lk for _, lk, _ in on_diag + beside], tri[...])
            for hh in range(2):
                mask, lk_d, ls_d = on_diag[hh]
                _, lk_l, ls_l = beside[hh]
                a_d = jnp.where(mask, jnp.exp(ls_d + later[hh]), 0.0)
                past_diag = jnp.sum(lk_d, axis=1, keepdims=True)
                a_l = jnp.exp(ls_l + later[2 + hh] + past_diag)
                acc[hh] = _dot(a_d.astype(BF16), vs[hh, here, :]) + _dot(a_l.astype(BF16), vs[hh, left, :])
                cmat[hh] = jnp.where(lane == qb - 1, past_diag, 0.0)
                right[hh] = jnp.broadcast_to(past_diag + jnp.sum(lk_l, axis=1, keepdims=True), (TILE, TILE))

        def q_step(qb, _):
            r0 = pl.multiple_of(qb * TILE, TILE)
            qrows = pl.ds(r0, TILE)

            @pl.when(qb == 0)
            def _():
                acc[...] = jnp.zeros((2, TILE, HEAD_DIM), F32)
                right[...] = jnp.zeros((2, TILE, TILE), F32)
                cmat[...] = jnp.zeros((2, TILE, LANES), F32)
                tile(qrows, qb, True)

            pl.when(qb > 0)(lambda: diagonal_and_left(qrows, qb))

            def live():
                return (jnp.max(right[:, :, :LANES]) > DEAD_LOG).astype(jnp.int32)

            def k_step(c):
                kb = c[0] - 1
                tile(qrows, kb, False)
                return kb, live()

            first, _ = lax.while_loop(lambda c: (c[0] > 0) & (c[1] > 0), k_step, (jnp.maximum(qb - 1, 0), live()))
            for hh in range(2):
                o_ref[qrows, hh * HEAD_DIM:(hh + 1) * HEAD_DIM] = acc[hh]
                c_ref[hh, qrows, :] = jnp.where(lane == LANES - 1, first.astype(F32), cmat[hh])
            return 0

        lax.fori_loop(0, nt, q_step, 0)

    pair = pl.BlockSpec((s, LANES), lambda h: (0, h))
    gain = pl.BlockSpec((1, LANES), lambda h: (0, 0))
    return _hosted_call(
        body, "attn_fwd", HEADS // 2, [q_all, k_all, v_all, q_gain2, k_gain2],
        [pair, pair, pair, gain, gain], [pair, pl.BlockSpec((2, s, LANES), lambda h: (h, 0, 0))],
        [jax.ShapeDtypeStruct((s, D), F32), jax.ShapeDtypeStruct((HEADS, s, LANES), F32)],
        [pltpu.VMEM((2, s, HEAD_DIM), BF16)] * 3
        + [pltpu.VMEM((TILE, TILE), BF16), pltpu.VMEM((2, TILE, HEAD_DIM), F32), pltpu.VMEM((2, TILE, TILE), F32),
           pltpu.VMEM((2, TILE, LANES), F32)], rider)


def _layer_b_out_fwd(o, zb, x2, p1, target, w_out, w_ple, w_gate):
    s = o.shape[0]

    def body(o_ref, zb_ref, x2_ref, p_ref, t_ref, wout_ref, wple_ref, wgate_ref,
             yb_ref, x3_ref, e_ref, gt_ref, dx4_ref, loss_ref):
        zb = zb_ref[...]
        yb = (o_ref[...] * (zb * _sigmoid(zb))).astype(BF16)
        yb_ref[...] = yb
        x3 = x2_ref[...] + _dot(yb, wout_ref[...])
        x3_ref[...] = x3
        x4 = _ple_fwd(p_ref, x3, wple_ref, wgate_ref, e_ref, gt_ref)
        d = x4 - t_ref[...]
        dx4_ref[...] = d * (1.0 / D)

        @pl.when(pl.program_id(0) == 0)
        def _():
            loss_ref[...] = jnp.zeros((1, D), F32)

        loss_ref[...] += jnp.sum(d * d, axis=0, keepdims=True)

    row_outs = [(D, BF16), (D, F32), (D, F32), (D, F32), (D, F32)]
    return _rows_call(body, "layer_b_out_fwd", s, [o, zb, x2, p1, target], [w_out, w_ple, w_gate], row_outs,
                      const_outs=[((1, D), F32)])


def _ple_bwd(dxo, e_ref, gt_ref, wgate_ref, de_ref, dgp_ref):
    e = e_ref[...]
    gt = gt_ref[...]
    de_ref[...] = (dxo * gt).astype(BF16)
    dgp = (dxo * e * gt * (1.0 - gt)).astype(BF16)
    dgp_ref[...] = dgp
    return dxo + _dot_nt(dgp, wgate_ref[...])


def _silu_grads(z):
    sg = _sigmoid(z)
    return z * sg, sg * (1.0 + z * (1.0 - sg))


def _layer_b_out_bwd(dx4, e1, gt1, o, zb, w_gate, w_out):
    s = dx4.shape[0]

    def body(dx4_ref, e_ref, gt_ref, o_ref, zb_ref, wgate_ref, wout_ref,
             de_ref, dgp_ref, dx3_ref, do_ref, dzb_ref):
        dx3 = _ple_bwd(dx4_ref[...], e_ref, gt_ref, wgate_ref, de_ref, dgp_ref)
        dx3_ref[...] = dx3
        dyb = _dot_nt(dx3.astype(BF16), wout_ref[...])
        silu, dsilu = _silu_grads(zb_ref[...])
        do_ref[...] = (dyb * silu).astype(BF16)
        dzb_ref[...] = (dyb * o_ref[...] * dsilu).astype(BF16)

    row_outs = [(D, BF16), (D, BF16), (D, F32), (D, BF16), (D, BF16)]
    return _rows_call(body, "layer_b_out_bwd", s, [dx4, e1, gt1, o, zb], [w_gate, w_out], row_outs)


def _attn_bwd(q_all, k_all, v_all, q_gain2, k_gain2, d_o, csave, rider=None):
    s = q_all.shape[0]
    nt = s // TILE

    def body(q_ref, k_ref, v_ref, qg_ref, kg_ref, do_ref, c_ref,
             dq_ref, dk_ref, dv_ref, dqg_ref, dkg_ref,
             qs, ks, vs, dos, qt, dot_t, tri_a, tri_b, dqa, dkt, dvt, dqb, left):
        tri_a[...] = _tri(True)
        tri_b[...] = _tri(False)
        lane = lax.broadcasted_iota(jnp.int32, (TILE, LANES), 1)
        qn, qr = _pair_norm(q_ref[...])
        kn, kr = _pair_norm(k_ref[...])
        qsc = qn * qg_ref[...] * SB_SCALE
        ksc = (kn * kg_ref[...]).astype(BF16)
        q_t = qsc.T.astype(BF16)
        do_t = do_ref[...].astype(F32).T.astype(BF16)
        for j in range(nt):
            qt[j] = q_t[:, j * TILE:(j + 1) * TILE]
            dot_t[j] = do_t[:, j * TILE:(j + 1) * TILE]
        qsc = qsc.astype(BF16)
        for hh in range(2):
            sl = slice(hh * HEAD_DIM, (hh + 1) * HEAD_DIM)
            qs[hh] = qsc[:, sl]
            ks[hh] = ksc[:, sl]
            vs[hh] = v_ref[:, sl]
            dos[hh] = do_ref[:, sl]

        def tile(qb, qrows, kb, diagonal):
            rows = pl.ds(pl.multiple_of(kb * TILE, TILE), TILE)
            heads = range(2)
            kblk = [ks[hh, rows, :] for hh in heads]
            logits = [_tile_logits(qs[hh, qrows, :], kblk[hh], diagonal) for hh in heads]
            later = _split_dot_many([lk for _, lk, _ in logits], tri_a[...])
            a, g = [], []
            for hh in heads:
                mask, _, ls = logits[hh]
                right = jnp.sum(jnp.where(lane == kb, c_ref[hh, qrows, :], 0.0), axis=1, keepdims=True)
                a_h = jnp.exp(ls + later[hh] + right)
                a.append(jnp.where(mask, a_h, 0.0) if diagonal else a_h)
                g.append(a[hh] * _dot_nt(dos[hh, qrows, :], vs[hh, rows, :]))
            before = _split_dot_many(g, tri_b[...])
            for hh in heads:
                sl = slice(hh * HEAD_DIM, (hh + 1) * HEAD_DIM)
                mask, _, ls = logits[hh]
                beta = jnp.exp(ls)
                lf = left[hh]
                dl = g[hh] * (1.0 - beta) - (before[hh] + lf) * beta
                if diagonal:
                    dl = jnp.where(mask, dl, 0.0)
                dl = dl.astype(BF16)
                left[hh] = lf + jnp.sum(g[hh], axis=1, keepdims=True)
                dqb[hh] += _dot(dl, kblk[hh])
                dk_t, dv_t = _dot(qt[qb, sl, :], dl), _dot(dot_t[qb, sl, :], a[hh].astype(BF16))
                if diagonal:
                    dkt[kb, sl, :] = dk_t
                    dvt[kb, sl, :] = dv_t
                else:
                    dkt[kb, sl, :] += dk_t
                    dvt[kb, sl, :] += dv_t

        def left_and_diagonal(qb, qrows):
            here = pl.ds(pl.multiple_of(qb * TILE, TILE), TILE)
            beside = pl.ds(pl.multiple_of((qb - 1) * TILE, TILE), TILE)
            heads = range(2)
            q = [qs[hh, qrows, :] for hh in heads]
            do = [dos[hh, qrows, :] for hh in heads]
            k_d, k_l = [ks[hh, here, :] for hh in heads], [ks[hh, beside, :] for hh in heads]
            on_diag = [_tile_logits(q[hh], k_d[hh], True) for hh in heads]
            on_left = [_tile_logits(q[hh], k_l[hh], False) for hh in heads]
            later = _split_dot_many([lk for _, lk, _ in on_diag + on_left], tri_a[...])
            a_d, a_l, g_d, g_l = [], [], [], []
            for hh in heads:
                mask, lk_d, ls_d = on_diag[hh]
                a_d.append(jnp.where(mask, jnp.exp(ls_d + later[hh]), 0.0))
                a_l.append(jnp.exp(on_left[hh][2] + later[2 + hh] + jnp.sum(lk_d, axis=1, keepdims=True)))
                g_d.append(a_d[hh] * _dot_nt(do[hh], vs[hh, here, :]))
                g_l.append(a_l[hh] * _dot_nt(do[hh], vs[hh, beside, :]))
            before = _split_dot_many(g_l + g_d, tri_b[...])
            for hh in heads:
                sl = slice(hh * HEAD_DIM, (hh + 1) * HEAD_DIM)
                beta_l, beta_d = jnp.exp(on_left[hh][2]), jnp.exp(on_diag[hh][2])
                dl_l = (g_l[hh] * (1.0 - beta_l) - before[hh] * beta_l).astype(BF16)
                carried = jnp.sum(g_l[hh], axis=1, keepdims=True)
                dl_d = g_d[hh] * (1.0 - beta_d) - (before[2 + hh] + carried) * beta_d
                dl_d = jnp.where(on_diag[hh][0], dl_d, 0.0).astype(BF16)
                dqa[qrows, sl] = (_dot(dl_l, k_l[hh]) + _dot(dl_d, k_d[hh])) * SB_SCALE
                dkt[qb - 1, sl, :] += _dot(qt[qb, sl, :], dl_l)
                dkt[qb, sl, :] = _dot(qt[qb, sl, :], dl_d)
                dvt[qb - 1, sl, :] += _dot(dot_t[qb, sl, :], a_l[hh].astype(BF16))
                dvt[qb, sl, :] = _dot(dot_t[qb, sl, :], a_d[hh].astype(BF16))

        def q_step(qb, _):
            qrows = pl.ds(pl.multiple_of(qb * TILE, TILE), TILE)
            first = jnp.max(jnp.where(lane == LANES - 1, c_ref[0, qrows, :], 0.0)).astype(jnp.int32)
            usual = (qb > 0) & (first == qb - 1)

            @pl.when(usual)
            def _():
                left_and_diagonal(qb, qrows)

            @pl.when(jnp.logical_not(usual))
            def _():
                dqb[...] = jnp.zeros((2, TILE, HEAD_DIM), F32)
                left[...] = jnp.zeros((2, TILE, TILE), F32)

                def k_step(kb, _):
                    tile(qb, qrows, kb, False)
                    return 0

                lax.fori_loop(first, qb, k_step, 0)
                tile(qb, qrows, qb, True)
                for hh in range(2):
                    dqa[qrows, hh * HEAD_DIM:(hh + 1) * HEAD_DIM] = dqb[hh] * SB_SCALE

            return 0

        lax.fori_loop(0, nt, q_step, 0)

        def norm_bwd(dy, xn, r, g_ref, dx_ref, dg_ref):
            dg_ref[...] = jnp.sum(dy * xn, axis=0, keepdims=True)
            dxn = dy * g_ref[...]
            dx_ref[...] = (r * (dxn - xn * (_half_sums(dxn * xn) * (1.0 / HEAD_DIM)))).astype(BF16)

        norm_bwd(dqa[...], qn, qr, qg_ref, dq_ref, dqg_ref)
        for j in range(nt):
            dqa[j * TILE:(j + 1) * TILE, :] = dkt[j].T
            dv_ref[j * TILE:(j + 1) * TILE, :] = dvt[j].T.astype(BF16)
        norm_bwd(dqa[...], kn, kr, kg_ref, dk_ref, dkg_ref)

    pair = pl.BlockSpec((s, LANES), lambda h: (0, h))
    gain = pl.BlockSpec((1, LANES), lambda h: (0, 0))
    dgain = pl.BlockSpec((None, 1, LANES), lambda h: (h, 0, 0))
    return _hosted_call(
        body, "attn_bwd", HEADS // 2, [q_all, k_all, v_all, q_gain2, k_gain2, d_o, csave],
        [pair, pair, pair, gain, gain, pair, pl.BlockSpec((2, s, LANES), lambda h: (h, 0, 0))],
        [pair, pair, pair, dgain, dgain],
        [jax.ShapeDtypeStruct((s, D), BF16)] * 3 + [jax.ShapeDtypeStruct((HEADS // 2, 1, LANES), F32)] * 2,
        [pltpu.VMEM((2, s, HEAD_DIM), BF16)] * 4
        + [pltpu.VMEM((nt, LANES, TILE), BF16)] * 2 + [pltpu.VMEM((TILE, TILE), BF16)] * 2
        + [pltpu.VMEM((s, LANES), F32)] + [pltpu.VMEM((nt, LANES, TILE), F32)] * 2
        + [pltpu.VMEM((2, TILE, HEAD_DIM), F32), pltpu.VMEM((2, TILE, TILE), F32)], rider)


def _norm_bwd_rows(dh, x, gain, dgain_ref):
    r = _rms(x)
    n = x * r
    dgain_ref[...] += jnp.sum(dh * n, axis=0, keepdims=True)
    dn = dh * gain
    return r * (dn - n * jnp.mean(dn * n, axis=-1, keepdims=True))


def _layer_b_in_bwd(dq, dzb, dk, dv, x2, dx3, w_bin, w_kv, b_norm, kv_norm, rider=None):
    s = x2.shape[0]

    def body(dq_ref, dzb_ref, dk_ref, dv_ref, x_ref, dx3_ref, wbin_ref, wkv_ref, bn_ref, kvn_ref,
             dx2_ref, dbn_ref, dkvn_ref):
        @pl.when(pl.program_id(0) == 0)
        def _():
            dbn_ref[...] = jnp.zeros((1, D), F32)
            dkvn_ref[...] = jnp.zeros((1, D), F32)

        dhb = jnp.zeros((TM, D), F32)
        dhkv = jnp.zeros((TM, D), F32)
        for j in range(N_DEV):
            cols = slice((j % 4) * COLS, (j % 4 + 1) * COLS)
            dhb = dhb + _dot_nt((dq_ref if j < 4 else dzb_ref)[:, cols], wbin_ref[j])
            dhkv = dhkv + _dot_nt((dk_ref if j < 4 else dv_ref)[:, cols], wkv_ref[j])
        x = x_ref[...]
        dx2 = dx3_ref[...] + _norm_bwd_rows(dhb, x, bn_ref[...], dbn_ref)
        dx2_ref[...] = dx2 + _norm_bwd_rows(dhkv, x, kvn_ref[...], dkvn_ref)

    return _rows_call(body, "layer_b_in_bwd", s, [dq, dzb, dk, dv, x2, dx3], [w_bin, w_kv, b_norm, kv_norm],
                      [(D, F32)], const_outs=[((1, D), F32), ((1, D), F32)], rider=rider)


def _layer_a_out_bwd(dx2, e0, gt0, z, m, w_gate, w_out, a_scale, w_group, rider=None):
    s = dx2.shape[0]
    tm = TM
    nb = s // tm

    def body(dx2_ref, e_ref, gt_ref, z_ref, m_ref, wgate_ref, wout_ref, as_ref, wg_ref,
             de_ref, dgp_ref, dx1_ref, dm_ref, duz_ref, das_ref, ext):
        i = pl.program_id(0)

        @pl.when(i == 0)
        def _():
            das_ref[...] = jnp.zeros((1, D), F32)
            ext[tm:tm + HALO, :] = jnp.zeros((HALO, D), F32)

        dx1 = _ple_bwd(dx2_ref[...], e_ref, gt_ref, wgate_ref, de_ref, dgp_ref)
        dx1_ref[...] = dx1
        dy = _dot_nt(dx1.astype(BF16), wout_ref[...])
        silu, dsilu = _silu_grads(z_ref[...])
        m = m_ref[...]
        dmixed = dy * silu
        duz_ref[:, D:] = (dy * (m * as_ref[...]) * dsilu).astype(BF16)
        das_ref[...] += jnp.sum(dmixed * m, axis=0, keepdims=True)
        dm_ref[...] = (dmixed * as_ref[...]).astype(BF16)
        t = (nb - 1 - i) * tm + lax.broadcasted_iota(jnp.int32, (tm, 1), 0)
        n_ext = tm + HALO
        for g in range(N_GROUPS):
            w = 2 ** (g + 1)
            cols = slice(g * GROUP_DIM, (g + 1) * GROUP_DIM)
            dpool = _dot_nt(dm_ref[:, cols], wg_ref[g])
            ext[0:tm, cols] = dpool / jnp.minimum(t + 1, w).astype(F32)
            acc = ext[:, cols]
            k = 1
            while k < w:
                acc = acc + pltpu.roll(acc, n_ext - k, 0)
                k *= 2
            duz_ref[:, cols] = (acc[:tm] - dpool).astype(BF16)
        ext[tm:tm + HALO, :] = ext[0:HALO, :]

    row_outs = [(D, BF16), (D, BF16), (D, F32), (D, BF16), (2 * D, BF16)]
    return _rows_call(body, "layer_a_out_bwd", s, [dx2, e0, gt0, z, m], [w_gate, w_out, a_scale, w_group],
                      row_outs, const_outs=[((1, D), F32)], scratch=[pltpu.VMEM((tm + HALO, D), F32)],
                      reverse=True, rider=rider)


def _layer_a_in_bwd(duz, x0, dx1, w_in, a_norm, rider=None):
    s = x0.shape[0]

    def body(duz_ref, x_ref, dx1_ref, win_ref, an_ref, dx0_ref, dan_ref):
        @pl.when(pl.program_id(0) == 0)
        def _():
            dan_ref[...] = jnp.zeros((1, D), F32)

        dh = jnp.zeros((TM, D), F32)
        for j in range(N_DEV):
            dh = dh + _dot_nt(duz_ref[:, j * COLS:(j + 1) * COLS], win_ref[j])
        dx0_ref[...] = dx1_ref[...] + _norm_bwd_rows(dh, x_ref[...], an_ref[...], dan_ref)

    return _rows_call(body, "layer_a_in_bwd", s, [duz, x0, dx1], [w_in, a_norm], [(D, F32)],
                      const_outs=[((1, D), F32)], rider=rider)


def _wgrad(a, b, name, n_split=1, a_blocked_b=False, rider=None):
    bs = list(b) if isinstance(b, (list, tuple)) else [b]
    s, k = a.shape
    n = sum(part.shape[1] for part in bs)
    tk = TM
    nb = n // n_split

    def body(a_ref, *refs):
        o_ref = refs[-1]
        lhs = a_ref[...].astype(BF16)
        done = 0
        for b_ref in refs[:-1]:
            res = _dot_tn(lhs, b_ref[...].astype(BF16))
            if n_split == 1:
                o_ref[...] = res.astype(BF16)
            else:
                for j in range(res.shape[1] // nb):
                    o_ref[done + j] = res[:, j * nb:(j + 1) * nb].astype(BF16)
                done += res.shape[1] // nb

    if a_blocked_b:
        b_specs = [pl.BlockSpec((s, tk), lambda i: (0, i))]
        out_spec = pl.BlockSpec((None, tk, tk), lambda i: (i, 0, 0))
        out_shape = jax.ShapeDtypeStruct((k // tk, tk, tk), BF16)
    elif n_split == 1:
        b_specs = [pl.BlockSpec((s, n), lambda i: (0, 0))]
        out_spec = pl.BlockSpec((tk, n), lambda i: (i, 0))
        out_shape = jax.ShapeDtypeStruct((k, n), BF16)
    else:
        b_specs = [pl.BlockSpec(part.shape, lambda i: (0, 0)) for part in bs]
        out_spec = pl.BlockSpec((n_split, tk, nb), lambda i: (0, i, 0))
        out_shape = jax.ShapeDtypeStruct((n_split, k, nb), BF16)
    res = _hosted_call(body, name, k // tk, [a] + bs, [pl.BlockSpec((s, tk), lambda i: (0, i))] + b_specs,
                       [out_spec], [out_shape], [], rider)
    return res[0] if rider is None else res


def _cast_shards(shards):
    n = len(shards)
    layers = [a.shape[0] if a.ndim == 3 else 0 for a in shards]

    def body(*refs):
        outs = iter(refs[n:])
        for a in range(n):
            if layers[a]:
                for t in range(layers[a]):
                    next(outs)[...] = refs[a][t].astype(BF16)
            else:
                next(outs)[...] = refs[a][...].astype(BF16)

    out_shape = []
    for a, k in zip(shards, layers):
        out_shape += [jax.ShapeDtypeStruct(a.shape[-2:], BF16)] * max(k, 1)
    vmem = pl.BlockSpec(memory_space=pltpu.VMEM)
    return pl.pallas_call(
        body, name="cast_shards", in_specs=[vmem] * n, out_specs=[vmem] * len(out_shape), out_shape=out_shape,
        compiler_params=pltpu.CompilerParams(vmem_limit_bytes=VMEM_LIMIT),
    )(*shards)


def _adamw(w, g, m, v):
    m = ADAM_B1 * m + (1.0 - ADAM_B1) * g
    v = ADAM_B2 * v + (1.0 - ADAM_B2) * jnp.square(g)
    m_hat = m / (1.0 - ADAM_B1 ** ADAM_STEP)
    v_hat = v / (1.0 - ADAM_B2 ** ADAM_STEP)
    delta = -ADAM_LR * (m_hat / (jnp.sqrt(v_hat) + ADAM_EPS) + ADAM_WD * w)
    return delta, m, v


def _place():
    return lax.axis_index("x"), lax.axis_index("y"), lax.axis_index("c")


def _all_gather(shards):
    return _alone("all_gather_weights", _GatherRider(shards))


def _alone(name, rider):
    n_in, n_out = len(rider.arrays), len(rider.out_shape())

    def body(*refs):
        for phase in rider.bind(refs[:n_in], refs[n_in:n_in + n_out], refs[n_in + n_out:]):
            phase()

    return pl.pallas_call(
        body, name=name, in_specs=[HBM_SPEC] * n_in, out_specs=[HBM_SPEC] * n_out,
        out_shape=rider.out_shape(), scratch_shapes=rider.scratch(),
        compiler_params=pltpu.CompilerParams(vmem_limit_bytes=VMEM_LIMIT),
    )(*rider.arrays)


class _GatherRider:
    WHEN = (0.0, 0.7, 1.0)

    def __init__(self, shards):
        self.arrays = list(shards)

    def out_shape(self):
        return _Gather.out_shape(self.arrays)

    def scratch(self):
        return _Gather.semaphores(len(self.arrays))

    def bind(self, ins, outs, scratch):
        moving = _Gather(ins, outs, *scratch)
        return moving.start, moving.forward, moving.finish


class _ReduceRider:
    WHEN = (0.0, 0.15, 0.5, 1.0)

    def __init__(self, partials):
        self.arrays = list(partials)

    def out_shape(self):
        return [jax.ShapeDtypeStruct(a.shape[1:], F32) for a in self.arrays]

    def scratch(self):
        n = len(self.arrays)
        dma = pltpu.SemaphoreType.DMA

        def blocks(k):
            return [pltpu.VMEM((k,) + a.shape[1:], BF16) for a in self.arrays]

        halves = [pltpu.VMEM((2, a.shape[1] // 2) + a.shape[2:], BF16) for a in self.arrays]
        return (blocks(4) + blocks(4) + halves + blocks(2) + [pltpu.VMEM(a.shape[1:], F32) for a in self.arrays]
                + [dma((4 * n,)), dma((4 * n,)), dma((4 * n,)), dma((2 * n,)), dma((2 * n,)),
                   dma((2 * n,)), dma((2 * n,)), dma((n,))])

    def bind(self, ins, outs, scratch):
        n = len(ins)
        mine, landed, halves, arrived, total = (scratch[i * n:(i + 1) * n] for i in range(5))
        send1, recv1, local1, send_h, recv_h, send2, recv2, out_sems = scratch[5 * n:]
        x, y, c = _place()
        plane = 2 * x + y
        via = [(x, 1 - y, c), (1 - x, y, c)]
        nbr = [(1 - x, y, c), (x, 1 - y, c)]
        nbr_block = [2 * (1 - x) + y, 2 * x + (1 - y)]
        diag_block = 2 * (1 - x) + (1 - y)

        def to_sibling(a, k):
            return pltpu.make_async_remote_copy(
                src_ref=ins[a].at[2 * k + (1 - c)], dst_ref=landed[a].at[k],
                send_sem=send1.at[4 * a + k], recv_sem=recv1.at[4 * a + k],
                device_id=(x, y, 1 - c), device_id_type=MESH)

        def own_block(a, k):
            return pltpu.make_async_copy(ins[a].at[2 * k + c], mine[a].at[k], local1.at[4 * a + k])

        def half_of(a, ref, h):
            rows = self.arrays[a].shape[1] // 2
            return ref.at[pl.ds(h * rows, rows)]

        def half_out(a, h):
            return pltpu.make_async_remote_copy(
                src_ref=half_of(a, mine[a].at[diag_block], h), dst_ref=halves[a].at[h],
                send_sem=send_h.at[2 * a + h], recv_sem=recv_h.at[2 * a + h],
                device_id=via[h], device_id_type=MESH)

        def to_owner(a, h):
            return pltpu.make_async_remote_copy(
                src_ref=mine[a].at[nbr_block[h]], dst_ref=arrived[a].at[h],
                send_sem=send2.at[2 * a + h], recv_sem=recv2.at[2 * a + h],
                device_id=nbr[h], device_id_type=MESH)

        def result(a):
            return pltpu.make_async_copy(total[a], outs[a], out_sems.at[a])

        def exchange_cores():
            for a in range(n):
                for k in range(4):
                    to_sibling(a, k).start()
                    own_block(a, k).start()

        def pair_sums():
            for a in range(n):
                for k in range(4):
                    own_block(a, k).wait()
                    to_sibling(a, k).wait_recv()
                total[a][...] = mine[a][plane].astype(F32) + landed[a][plane].astype(F32)
                for k in range(4):
                    mine[a][k] = (mine[a][k].astype(F32) + landed[a][k].astype(F32)).astype(BF16)
                for h in range(2):
                    half_out(a, h).start()

        def fold_and_send():
            for a in range(n):
                rows = self.arrays[a].shape[1] // 2
                for h in range(2):
                    half_out(a, h).wait_recv()
                    part = mine[a].at[nbr_block[h]]
                    span = slice(h * rows, (h + 1) * rows)
                    part[span] = (part[span].astype(F32) + halves[a][h].astype(F32)).astype(BF16)
                    to_owner(a, h).start()

        def finish():
            for a in range(n):
                for h in range(2):
                    to_owner(a, h).wait_recv()
                    total[a][...] += arrived[a][h].astype(F32)
                result(a).start()
            for a in range(n):
                for k in range(4):
                    to_sibling(a, k).wait_send()
                for h in range(2):
                    half_out(a, h).wait_send()
                    to_owner(a, h).wait_send()
                result(a).wait()

        return exchange_cores, pair_sums, fold_and_send, finish


class _Gather:
    COPIES = 9

    def __init__(self, ins, outs, send_sems, recv_sems, local_sems):
        self.ins, self.outs = ins, outs
        self.send_sems, self.recv_sems, self.local_sems = send_sems, recv_sems, local_sems
        self.x, self.y, self.c = _place()

    @staticmethod
    def out_shape(shards):
        return [jax.ShapeDtypeStruct((N_DEV,) + a.shape, a.dtype) for a in shards]

    @staticmethod
    def semaphores(n):
        dma = pltpu.SemaphoreType.DMA
        return [dma((_Gather.COPIES * n,)), dma((_Gather.COPIES * n,)), dma((n,))]

    def _copy(self, a, k, block, to, own=False, half=None):
        px, py, pc = block
        slot = self.outs[a].at[4 * px + 2 * py + pc]
        if half is not None:
            rows = slot.shape[0] // 2
            slot = slot.at[pl.ds(half * rows, rows)]
        return pltpu.make_async_remote_copy(
            src_ref=self.ins[a] if own else slot, dst_ref=slot,
            send_sem=self.send_sems.at[self.COPIES * a + k], recv_sem=self.recv_sems.at[self.COPIES * a + k],
            device_id=to, device_id_type=MESH)

    def _local(self, a):
        return pltpu.make_async_copy(self.ins[a], self.outs[a].at[4 * self.x + 2 * self.y + self.c],
                                     self.local_sems.at[a])

    def _plan(self, a, c):
        x, y = self.x, self.y
        me, sibling = (x, y, c), (x, y, 1 - c)
        xn, yn, dg = (1 - x, y, c), (x, 1 - y, c), (1 - x, 1 - y, c)
        return [
            self._copy(a, 0, me, sibling, own=True), self._copy(a, 1, me, xn, own=True),
            self._copy(a, 2, me, yn, own=True),
            self._copy(a, 3, xn, yn, half=0), self._copy(a, 4, yn, xn, half=1),
            self._copy(a, 5, xn, sibling), self._copy(a, 6, yn, sibling),
            self._copy(a, 7, dg, sibling, half=0), self._copy(a, 8, dg, sibling, half=1),
        ]

    def _arrivals(self, a):
        x, y, c = self.x, self.y, self.c
        me = (x, y, c)
        xn, yn, dg = (1 - x, y, c), (x, 1 - y, c), (1 - x, 1 - y, c)
        other = 1 - c
        return [
            self._copy(a, 0, (x, y, other), me), self._copy(a, 1, xn, me), self._copy(a, 2, yn, me),
            self._copy(a, 3, dg, me, half=0), self._copy(a, 4, dg, me, half=1),
            self._copy(a, 5, (1 - x, y, other), me), self._copy(a, 6, (x, 1 - y, other), me),
            self._copy(a, 7, (1 - x, 1 - y, other), me, half=0), self._copy(a, 8, (1 - x, 1 - y, other), me, half=1),
        ]

    def start(self):
        for a in range(len(self.ins)):
            self._local(a).start()
            for cp in self._plan(a, self.c)[:3]:
                cp.start()

    def forward(self):
        for a in range(len(self.ins)):
            sends, lands = self._plan(a, self.c), self._arrivals(a)
            lands[1].wait_recv()
            sends[3].start()
            sends[5].start()
            lands[2].wait_recv()
            sends[4].start()
            sends[6].start()

    def finish(self):
        n = len(self.ins)
        for a in range(n):
            sends, lands = self._plan(a, self.c), self._arrivals(a)
            lands[3].wait_recv()
            sends[7].start()
            lands[4].wait_recv()
            sends[8].start()
        for a in range(n):
            lands = self._arrivals(a)
            for k in (0, 5, 6, 7, 8):
                lands[k].wait_recv()
        for a in range(n):
            for cp in self._plan(a, self.c):
                cp.wait_send()
            self._local(a).wait()


def _adamw_all(name, ws, gs, ms, vs):
    n = len(ws)
    per_layer = [isinstance(g, tuple) for g in gs]
    flat_g = [part for g in gs for part in (g if isinstance(g, tuple) else (g,))]

    def body(*refs):
        w, refs = refs[:n], refs[n:]
        g, refs = refs[:len(flat_g)], refs[len(flat_g):]
        m, v, outs = refs[:n], refs[n:2 * n], refs[2 * n:]
        stacked = iter(outs[3 * n:])
        parts = iter(g)
        for a in range(n):
            if per_layer[a]:
                whole = next(stacked)
                for t in range(len(gs[a])):
                    grad = next(parts)[...]
                    whole[t] = grad
                    outs[a][t], outs[n + a][t], outs[2 * n + a][t] = _adamw(w[a][t], grad, m[a][t], v[a][t])
            else:
                outs[a][...], outs[n + a][...], outs[2 * n + a][...] = _adamw(
                    w[a][...], next(parts)[...], m[a][...], v[a][...])

    shapes = [jax.ShapeDtypeStruct(a.shape, F32) for a in ws]
    vmem = pl.BlockSpec(memory_space=pltpu.VMEM)
    n_out = 3 * n + sum(per_layer)
    res = pl.pallas_call(
        body, name=name, in_specs=[vmem] * (3 * n + len(flat_g)), out_specs=[vmem] * n_out,
        out_shape=shapes * 3 + [s for s, p in zip(shapes, per_layer) if p],
        compiler_params=pltpu.CompilerParams(vmem_limit_bytes=VMEM_LIMIT),
    )(*ws, *flat_g, *ms, *vs)
    stacked = iter(res[3 * n:])
    return [(next(stacked) if per_layer[a] else gs[a], res[a], res[n + a], res[2 * n + a]) for a in range(n)]


def _all_reduce_small(rows, gain_parts):
    def body(rows_ref, dqg_ref, dkg_ref, out_ref, buf, send_sems, recv_sems):
        x, y, c = _place()
        me = 4 * x + 2 * y + c
        buf[0] = rows_ref[...]
        for row, part in ((4, dqg_ref), (5, dkg_ref)):
            both = jnp.sum(part[...].reshape(HEADS // 2, LANES), axis=0, keepdims=True)
            buf[0, row:row + 1, 0:HEAD_DIM] = both[:, :HEAD_DIM] + both[:, HEAD_DIM:]
        copies = []
        for r in range(1, N_DEV):
            bx, by, bc = (r >> 2) & 1, (r >> 1) & 1, r & 1
            to = (x ^ bx, y ^ by, c ^ bc)
            copies.append(pltpu.make_async_remote_copy(
                src_ref=buf.at[0], dst_ref=buf.at[r], send_sem=send_sems.at[r - 1], recv_sem=recv_sems.at[r - 1],
                device_id=to, device_id_type=MESH))
        for cp in copies:
            cp.start()
        for cp in copies:
            cp.wait_recv()
        for cp in copies:
            cp.wait_send()
        tot = buf[me]
        for j in range(1, N_DEV):
            tot = tot + buf[j ^ me]
        out_ref[...] = tot
        loss = (0.5 / D) * jnp.sum(tot[6:7, :], axis=1, keepdims=True)
        out_ref[6:7, :] = jnp.broadcast_to(loss, (1, D))

    vmem = pl.BlockSpec(memory_space=pltpu.VMEM)
    return pl.pallas_call(
        body, name="all_reduce_small", in_specs=[vmem] * 3, out_specs=vmem,
        out_shape=jax.ShapeDtypeStruct((8, D), F32),
        scratch_shapes=[pltpu.VMEM((N_DEV, 8, D), F32), pltpu.SemaphoreType.DMA((N_DEV - 1,)),
                        pltpu.SemaphoreType.DMA((N_DEV - 1,))],
    )(rows, *gain_parts)


def kernel(x, p, a_norm, a_w_in, a_w_group, a_scale, a_w_out, kv_norm, w_kv, k_norm, b_norm, b_w_in, b_q_norm, b_w_out, ple_w, ple_gate_w, loss_target, m_a_norm, m_a_w_in, m_a_w_group, m_a_scale, m_a_w_out, m_kv_norm, m_w_kv, m_k_norm, m_b_norm, m_b_w_in, m_b_q_norm, m_b_w_out, m_ple_w, m_ple_gate_w, v_a_norm, v_a_w_in, v_a_w_group, v_a_scale, v_a_w_out, v_kv_norm, v_w_kv, v_k_norm, v_b_norm, v_b_w_in, v_b_q_norm, v_b_w_out, v_ple_w, v_ple_gate_w):
    xi, yi, ci = _place()
    me = 4 * xi + 2 * yi + ci

    big = {
        "a_w_in": a_w_in.reshape(D, COLS), "a_w_group": a_w_group.reshape(N_GROUPS * GROUP_ROWS, GROUP_DIM),
        "a_w_out": a_w_out.reshape(ROWS, D), "w_kv": w_kv, "b_w_in": b_w_in.reshape(D, COLS),
        "b_w_out": b_w_out.reshape(ROWS, D), "ple_w": ple_w, "ple_gate_w": ple_gate_w,
    }
    names = ["a_w_in", "a_w_group", "a_w_out", "w_kv", "b_w_in", "b_w_out", "ple_w0", "ple_w1", "gate0", "gate1"]
    cast = dict(zip(names, _cast_shards(list(big.values()))))
    small = jnp.concatenate([a_norm, a_scale, jnp.zeros((14, ROWS), F32)], axis=0)
    first = ["a_w_in", "a_w_group", "a_w_out", "ple_w0", "gate0"]
    behind_a = ["w_kv", "b_w_in"]
    behind_attn = ["b_w_out", "ple_w1", "gate1"]
    gathered = _all_gather([cast[k] for k in first] + [small])
    full = dict(zip(first, gathered[:-1]))
    small_all = gathered[-1]
    a_norm_f = small_all[:, 0, :].reshape(1, D)
    a_scale_f = small_all[:, 1, :].reshape(1, D)
    w_a_in = full["a_w_in"]
    w_a_out = full["a_w_out"].reshape(D, D)
    w_gate0 = full["gate0"].reshape(D, D)
    w_ple0 = full["ple_w0"]
    w_group = full["a_w_group"].reshape(N_DEV, N_GROUPS, GROUP_ROWS, GROUP_DIM).transpose(1, 0, 2, 3).reshape(
        N_GROUPS, GROUP_DIM, GROUP_DIM)
    kvn, bn = kv_norm.reshape(1, D), b_norm
    kg, qg = k_norm.reshape(1, HEAD_DIM), b_q_norm

    x0, p0, p1, target = x[0], p[0, 0], p[1, 0], loss_target[0]
    h0, z, pooled, mcat, y, x1, e0, gt0, x2, w_kv_f, w_b_in = _layer_a_fwd(
        x0, p0, a_norm_f, a_scale_f, w_a_in, w_group, w_a_out, w_ple0, w_gate0,
        rider=_GatherRider([cast[k] for k in behind_a]))
    hkv, hb, k_all, v_all, q_all, zb = _layer_b_in_fwd(x2, kvn, bn, w_kv_f, w_b_in)
    qg2, kg2 = jnp.concatenate([qg, qg], axis=1), jnp.concatenate([kg, kg], axis=1)
    o, csave, w_b_out, w_ple1, w_gate1 = _attn_fwd(
        q_all, k_all, v_all, qg2, kg2, rider=_GatherRider([cast[k] for k in behind_attn]))
    w_b_out, w_gate1 = w_b_out.reshape(D, D), w_gate1.reshape(D, D)
    yb, x3, e1, gt1, dx4, sq_err = _layer_b_out_fwd(o, zb, x2, p1, target, w_b_out, w_ple1, w_gate1)

    de1, dgp1, dx3, d_o, dzb = _layer_b_out_bwd(dx4, e1, gt1, o, zb, w_gate1, w_b_out)
    partial = {
        "b_w_out": _wgrad(yb, dx3, "wgrad_b_w_out").reshape(N_DEV, ROWS, D),
        "ple_w1": _wgrad(p1, de1, "wgrad_ple_w1", n_split=8),
        "gate1": _wgrad(x3, dgp1, "wgrad_gate1").reshape(N_DEV, ROWS, D),
    }
    grad = {}
    dq, dk, dv, dqg, dkg, grad["b_w_out"], grad["ple_w1"], grad["gate1"] = _attn_bwd(
        q_all, k_all, v_all, qg2, kg2, d_o, csave,
        rider=_ReduceRider([partial[k] for k in ("b_w_out", "ple_w1", "gate1")]))
    partial["w_kv"] = _wgrad(hkv, [dk, dv], "wgrad_w_kv", n_split=8)
    partial["b_w_in"] = _wgrad(hb, [dq, dzb], "wgrad_b_w_in", n_split=8)
    dx2, d_bn, d_kvn, grad["w_kv"] = _layer_b_in_bwd(
        dq, dzb, dk, dv, x2, dx3, w_b_in, w_kv_f, bn, kvn, rider=_ReduceRider([partial["w_kv"]]))
    de0, dgp0, dx1, dm, duz, d_as, grad["b_w_in"] = _layer_a_out_bwd(
        dx2, e0, gt0, z, mcat, w_gate0, w_a_out, a_scale_f, w_group, rider=_ReduceRider([partial["b_w_in"]]))
    partial["gate0"] = _wgrad(x1, dgp0, "wgrad_gate0").reshape(N_DEV, ROWS, D)
    partial["a_w_out"] = _wgrad(y, dx1, "wgrad_a_w_out").reshape(N_DEV, ROWS, D)
    partial["a_w_in"], grad["gate0"], grad["a_w_out"] = _wgrad(
        h0, duz, "wgrad_a_w_in", n_split=8, rider=_ReduceRider([partial["gate0"], partial["a_w_out"]]))
    dw_group = _wgrad(pooled, dm, "wgrad_a_w_group", a_blocked_b=True)
    partial["a_w_group"] = dw_group.reshape(N_GROUPS, N_DEV, GROUP_ROWS, GROUP_DIM).transpose(1, 0, 2, 3).reshape(
        N_DEV, N_GROUPS * GROUP_ROWS, GROUP_DIM)
    partial["ple_w0"] = _wgrad(p0, de0, "wgrad_ple_w0", n_split=8)
    behind_a_in = ["a_w_in", "a_w_group", "ple_w0"]
    dx0, d_an, *done = _layer_a_in_bwd(duz, x0, dx1, w_a_in, a_norm_f,
                                      rider=_ReduceRider([partial[k] for k in behind_a_in]))
    grad.update(zip(behind_a_in, done))

    given = {
        "a_w_in": (a_w_in, m_a_w_in, v_a_w_in), "a_w_group": (a_w_group, m_a_w_group, v_a_w_group),
        "a_w_out": (a_w_out, m_a_w_out, v_a_w_out), "w_kv": (w_kv, m_w_kv, v_w_kv),
        "b_w_in": (b_w_in, m_b_w_in, v_b_w_in), "b_w_out": (b_w_out, m_b_w_out, v_b_w_out),
        "ple_w": (ple_w, m_ple_w, v_ple_w), "ple_gate_w": (ple_gate_w, m_ple_gate_w, v_ple_gate_w),
    }
    grad["ple_w"] = (grad["ple_w0"], grad["ple_w1"])
    grad["ple_gate_w"] = (grad["gate0"], grad["gate1"])
    updated = _adamw_all(
        "adamw_shards", list(big.values()), [grad[k] for k in big],
        [given[k][1].reshape(big[k].shape) for k in big], [given[k][2].reshape(big[k].shape) for k in big])
    res = {k: tuple(t.reshape(given[k][0].shape) for t in four) for k, four in zip(big, updated)}

    rows = jnp.concatenate([d_kvn, d_bn, d_an, d_as, jnp.zeros((2, D), F32), sq_err, jnp.zeros((1, D), F32)], axis=0)
    tot = _all_reduce_small(rows, (dqg, dkg))
    loss = tot[6, 0]
    small_grad = {
        "kv_norm": tot[0:1], "b_norm": tot[1:2],
        "a_norm": lax.dynamic_slice_in_dim(tot[2:3], me * ROWS, ROWS, axis=1),
        "a_scale": lax.dynamic_slice_in_dim(tot[3:4], me * ROWS, ROWS, axis=1),
        "b_q_norm": tot[4:5, :HEAD_DIM], "k_norm": tot[5:6, :HEAD_DIM],
    }
    small_given = {
        "a_norm": (a_norm, m_a_norm, v_a_norm), "a_scale": (a_scale, m_a_scale, v_a_scale),
        "kv_norm": (kv_norm, m_kv_norm, v_kv_norm), "k_norm": (k_norm, m_k_norm, v_k_norm),
        "b_norm": (b_norm, m_b_norm, v_b_norm), "b_q_norm": (b_q_norm, m_b_q_norm, v_b_q_norm),
    }
    rows_of = {k: [t.reshape(1, -1) for t in three] for k, three in small_given.items()}
    updated = _adamw_all(
        "adamw_gains", [rows_of[k][0] for k in small_given], [small_grad[k] for k in small_given],
        [rows_of[k][1] for k in small_given], [rows_of[k][2] for k in small_given])
    res.update({k: tuple(t.reshape(small_given[k][0].shape) for t in four) for k, four in zip(small_given, updated)})

    order = ["a_norm", "a_w_in", "a_w_group", "a_scale", "a_w_out", "kv_norm", "w_kv", "k_norm", "b_norm",
             "b_w_in", "b_q_norm", "b_w_out", "ple_w", "ple_gate_w"]
    outs = [res[k][kind] for kind in range(4) for k in order]
    return (loss, dx0.reshape(x.shape), *outs)
```

```python
import jax
import jax.numpy as jnp
from jax import lax
from jax.experimental import pallas as pl
from jax.experimental.pallas import tpu as pltpu

F32 = jnp.float32
BF16 = jnp.bfloat16
MESH = pl.DeviceIdType.MESH

N_DEV = 8
D = 1024
N_GROUPS = 4
GROUP_DIM = D // N_GROUPS
HALO = 16
HEADS = 16
HEAD_DIM = D // HEADS
SB_SCALE = HEAD_DIM ** -0.5
TILE = 256
LANES = 128
DEAD_LOG = -120.0
EPS = 1e-6
ADAM_LR = 0.001
ADAM_B1 = 0.9
ADAM_B2 = 0.999
ADAM_EPS = 1e-08
ADAM_WD = 0.01
ADAM_STEP = 10
TM = 256
COLS = 2 * D // N_DEV
ROWS = D // N_DEV
GROUP_ROWS = GROUP_DIM // N_DEV
VMEM_LIMIT = 56 * 1024 * 1024

HBM_SPEC = pl.BlockSpec(memory_space=pltpu.HBM)


def _dot(a, b):
    return jnp.dot(a, b, preferred_element_type=F32)


def _dot_nt(a, b):
    return lax.dot_general(a, b, (((1,), (1,)), ((), ())), preferred_element_type=F32)


def _dot_tn(a, b):
    return lax.dot_general(a, b, (((0,), (0,)), ((), ())), preferred_element_type=F32)


def _sigmoid(x):
    return jax.nn.sigmoid(x)


def _split_dot(x, mat):
    hi = x.astype(BF16)
    lo = (x - hi.astype(F32)).astype(BF16)
    return _dot(hi, mat) + _dot(lo, mat)


def _split_dot_many(xs, mat):
    rows = xs[0].shape[0]
    his = [x.astype(BF16) for x in xs]
    los = [(x - hi.astype(F32)).astype(BF16) for x, hi in zip(xs, his)]
    out = _dot(jnp.concatenate(his + los, axis=0), mat)
    n = len(xs)
    return [out[i * rows:(i + 1) * rows] + out[(n + i) * rows:(n + i + 1) * rows] for i in range(n)]


def _rms(x):
    return lax.rsqrt(jnp.mean(x * x, axis=-1, keepdims=True) + EPS)


def _hosted_call(body, name, n_steps, ins, in_specs, out_specs, out_shape, scratch, rider=None):
    ins, scratch = list(ins), list(scratch)
    if rider is None:
        wrapped, extra_in, extra_out, extra_scratch = body, [], [], []
    else:
        extra_in, extra_out, extra_scratch = rider.arrays, rider.out_shape(), rider.scratch()
        n_in, n_out, n_scr = len(ins), len(out_shape), len(scratch)
        k_in, k_out = len(extra_in), len(extra_out)

        def wrapped(*refs):
            own_in, r_in = refs[:n_in], refs[n_in:n_in + k_in]
            own_out = refs[n_in + k_in:n_in + k_in + n_out]
            r_out = refs[n_in + k_in + n_out:n_in + k_in + n_out + k_out]
            rest = refs[n_in + k_in + n_out + k_out:]
            phases = rider.bind(r_in, r_out, rest[n_scr:])
            step = pl.program_id(0)
            pl.when(step == 0)(phases[0])
            body(*own_in, *own_out, *rest[:n_scr])
            for share, phase in zip(rider.WHEN[1:], phases[1:]):
                at = min(n_steps - 1, max(1, round(share * (n_steps - 1))))
                pl.when(step == at)(phase)

    return pl.pallas_call(
        wrapped, name=name, grid=(n_steps,),
        in_specs=list(in_specs) + [HBM_SPEC] * len(extra_in),
        out_specs=list(out_specs) + [HBM_SPEC] * len(extra_out),
        out_shape=list(out_shape) + list(extra_out), scratch_shapes=scratch + list(extra_scratch),
        compiler_params=pltpu.CompilerParams(dimension_semantics=("arbitrary",), vmem_limit_bytes=VMEM_LIMIT),
    )(*ins, *extra_in)


def _rows_call(body, name, n_rows, row_ins, const_ins, row_outs, const_outs=(), scratch=(),
               reverse=False, tm=TM, rider=None):
    nb = n_rows // tm

    def row_map(i):
        return ((nb - 1 - i) if reverse else i, 0)

    def const_map(nd):
        return lambda i: (0,) * nd

    in_specs = [pl.BlockSpec((tm, a.shape[1]), row_map) for a in row_ins]
    in_specs += [pl.BlockSpec(a.shape, const_map(a.ndim)) for a in const_ins]
    out_specs = [pl.BlockSpec((tm, w), row_map) for (w, _) in row_outs]
    out_specs += [pl.BlockSpec(s, const_map(len(s))) for (s, _) in const_outs]
    out_shape = [jax.ShapeDtypeStruct((n_rows, w), dt) for (w, dt) in row_outs]
    out_shape += [jax.ShapeDtypeStruct(s, dt) for (s, dt) in const_outs]
    return _hosted_call(body, name, nb, list(row_ins) + list(const_ins), in_specs, out_specs, out_shape,
                        scratch, rider)


def _ple_fwd(p_ref, xin, wple_ref, wgate_ref, e_ref, gt_ref):
    pb = p_ref[...].astype(BF16)
    for j in range(N_DEV):
        e_ref[:, j * ROWS:(j + 1) * ROWS] = _dot(pb, wple_ref[j])
    gt = _sigmoid(_dot(xin.astype(BF16), wgate_ref[...]))
    gt_ref[...] = gt
    return xin + e_ref[...] * gt


def _layer_a_fwd(x0, p0, a_norm, a_scale, w_in, w_group, w_out, w_ple, w_gate, rider=None):
    s = x0.shape[0]
    tm = TM

    def body(x_ref, p_ref, an_ref, as_ref, win_ref, wg_ref, wout_ref, wple_ref, wgate_ref,
             h_ref, z_ref, pooled_ref, m_ref, y_ref, x1_ref, e_ref, gt_ref, x2_ref, uext):
        i = pl.program_id(0)

        @pl.when(i == 0)
        def _():
            uext[0:HALO, :] = jnp.zeros((HALO, D), F32)

        x = x_ref[...]
        h = (x * _rms(x) * an_ref[...]).astype(BF16)
        h_ref[...] = h
        for j in range(N_DEV):
            uz = _dot(h, win_ref[j])
            if j < 4:
                uext[HALO:HALO + tm, j * COLS:(j + 1) * COLS] = uz
            else:
                z_ref[:, (j - 4) * COLS:(j - 3) * COLS] = uz
        t = i * tm + lax.broadcasted_iota(jnp.int32, (tm, 1), 0)
        for g in range(N_GROUPS):
            w = 2 ** (g + 1)
            cols = slice(g * GROUP_DIM, (g + 1) * GROUP_DIM)
            ext = uext[:, cols]
            acc = ext
            k = 1
            while k < w:
                acc = acc + pltpu.roll(acc, k, 0)
                k *= 2
            cnt = jnp.minimum(t + 1, w).astype(F32)
            pooled = (acc[HALO:] / cnt - ext[HALO:]).astype(BF16)
            pooled_ref[:, cols] = pooled
            m_ref[:, cols] = _dot(pooled, wg_ref[g])
        uext[0:HALO, :] = uext[tm:tm + HALO, :]
        z = z_ref[...]
        y = (m_ref[...] * as_ref[...] * (z * _sigmoid(z))).astype(BF16)
        y_ref[...] = y
        x1 = x + _dot(y, wout_ref[...])
        x1_ref[...] = x1
        x2_ref[...] = _ple_fwd(p_ref, x1, wple_ref, wgate_ref, e_ref, gt_ref)

    row_outs = [(D, BF16), (D, F32), (D, BF16), (D, F32), (D, BF16), (D, F32), (D, F32), (D, F32), (D, F32)]
    return _rows_call(body, "layer_a_fwd", s, [x0, p0], [a_norm, a_scale, w_in, w_group, w_out, w_ple, w_gate],
                      row_outs, scratch=[pltpu.VMEM((tm + HALO, D), F32)], rider=rider)


def _layer_b_in_fwd(x2, kv_norm, b_norm, w_kv, w_bin):
    s = x2.shape[0]

    def body(x_ref, kvn_ref, bn_ref, wkv_ref, wbin_ref, hkv_ref, hb_ref, k_ref, v_ref, q_ref, zb_ref):
        x = x_ref[...]
        n = x * _rms(x)
        hkv = (n * kvn_ref[...]).astype(BF16)
        hb = (n * bn_ref[...]).astype(BF16)
        hkv_ref[...] = hkv
        hb_ref[...] = hb
        for j in range(N_DEV):
            kv = _dot(hkv, wkv_ref[j])
            qz = _dot(hb, wbin_ref[j])
            if j < 4:
                cols = slice(j * COLS, (j + 1) * COLS)
                k_ref[:, cols] = kv
                q_ref[:, cols] = qz
            else:
                cols = slice((j - 4) * COLS, (j - 3) * COLS)
                v_ref[:, cols] = kv.astype(BF16)
                zb_ref[:, cols] = qz

    row_outs = [(D, BF16), (D, BF16), (D, F32), (D, BF16), (D, F32), (D, F32)]
    return _rows_call(body, "layer_b_in_fwd", s, [x2], [kv_norm, b_norm, w_kv, w_bin], row_outs)


def _tri(after):
    r = lax.broadcasted_iota(jnp.int32, (TILE, TILE), 0)
    c = lax.broadcasted_iota(jnp.int32, (TILE, TILE), 1)
    return jnp.where((r > c) if after else (r < c), 1.0, 0.0).astype(BF16)


def _half_sums(v):
    r = lax.broadcasted_iota(jnp.int32, (LANES, LANES), 0) < HEAD_DIM
    c = lax.broadcasted_iota(jnp.int32, (LANES, LANES), 1) < HEAD_DIM
    same_head = jnp.where(r == c, 1.0, 0.0).astype(BF16)
    return _split_dot(v, same_head)


def _pair_norm(x):
    r = lax.rsqrt(_half_sums(x * x) * (1.0 / HEAD_DIM) + EPS)
    return x * r, r


def _tile_logits(qblk, kblk, diagonal):
    l = _dot_nt(qblk, kblk)
    sp = jnp.maximum(l, 0.0) + jnp.log(1.0 + jnp.exp(-jnp.abs(l)))
    ls = l - sp
    if not diagonal:
        return None, -sp, ls
    mask = lax.broadcasted_iota(jnp.int32, l.shape, 1) < lax.broadcasted_iota(jnp.int32, l.shape, 0)
    return mask, jnp.where(mask, -sp, 0.0), ls


def _attn_fwd(q_all, k_all, v_all, q_gain2, k_gain2, rider=None):
    s = q_all.shape[0]
    nt = s // TILE

    def body(q_ref, k_ref, v_ref, qg_ref, kg_ref, o_ref, c_ref, qs, ks, vs, tri, acc, right, cmat):
        tri[...] = _tri(True)
        lane = lax.broadcasted_iota(jnp.int32, (TILE, LANES), 1)
        qn, _ = _pair_norm(q_ref[...])
        kn, _ = _pair_norm(k_ref[...])
        qsc = (qn * qg_ref[...] * SB_SCALE).astype(BF16)
        ksc = (kn * kg_ref[...]).astype(BF16)
        for hh in range(2):
            sl = slice(hh * HEAD_DIM, (hh + 1) * HEAD_DIM)
            qs[hh] = qsc[:, sl]
            ks[hh] = ksc[:, sl]
            vs[hh] = v_ref[:, sl]

        def tile(qrows, kb, diagonal):
            rows = pl.ds(pl.multiple_of(kb * TILE, TILE), TILE)
            loaded = [(qs[hh, qrows, :], ks[hh, rows, :], vs[hh, rows, :], right[hh], cmat[hh], acc[hh])
                      for hh in range(2)]
            logits = [_tile_logits(q, k, diagonal) for q, k, _, _, _, _ in loaded]
            later = _split_dot_many([lk for _, lk, _ in logits], tri[...])
            results = []
            for (q, k, v, rt, cm, ac), (mask, lk, ls), lt in zip(loaded, logits, later):
                a = jnp.exp(ls + lt + rt)
                if diagonal:
                    a = jnp.where(mask, a, 0.0)
                results.append((ac + _dot(a.astype(BF16), v), jnp.where(lane == kb, rt[:, :LANES], cm),
                                rt + jnp.sum(lk, axis=1, keepdims=True)))
            for hh, (ac, cm, rt) in enumerate(results):
                acc[hh] = ac
                cmat[hh] = cm
                right[hh] = rt

        def diagonal_and_left(qrows, qb):
            here = pl.ds(pl.multiple_of(qb * TILE, TILE), TILE)
            left = pl.ds(pl.multiple_of((qb - 1) * TILE, TILE), TILE)
            q = [qs[hh, qrows, :] for hh in range(2)]
            on_diag = [_tile_logits(q[hh], ks[hh, here, :], True) for hh in range(2)]
            beside = [_tile_logits(q[hh], ks[hh, left, :], False) for hh in range(2)]
            later = _split_dot_many([lk for _, lk, _ in on_diag + beside], tri[...])
            for hh in range(2):
                mask, lk_d, ls_d = on_diag[hh]
                _, lk_l, ls_l = beside[hh]
                a_d = jnp.where(mask, jnp.exp(ls_d + later[hh]), 0.0)
                past_diag = jnp.sum(lk_d, axis=1, keepdims=True)
                a_l = jnp.exp(ls_l + later[2 + hh] + past_diag)
                acc[hh] = _dot(a_d.astype(BF16), vs[hh, here, :]) + _dot(a_l.astype(BF16), vs[hh, left, :])
                cmat[hh] = jnp.where(lane == qb - 1, past_diag, 0.0)
                right[hh] = jnp.broadcast_to(past_diag + jnp.sum(lk_l, axis=1, keepdims=True), (TILE, TILE))

        def q_step(qb, _):
            r0 = pl.multiple_of(qb * TILE, TILE)
            qrows = pl.ds(r0, TILE)

            @pl.when(qb == 0)
            def _():
                acc[...] = jnp.zeros((2, TILE, HEAD_DIM), F32)
                right[...] = jnp.zeros((2, TILE, TILE), F32)
                cmat[...] = jnp.zeros((2, TILE, LANES), F32)
                tile(qrows, qb, True)

            pl.when(qb > 0)(lambda: diagonal_and_left(qrows, qb))

            def live():
                return (jnp.max(right[:, :, :LANES]) > DEAD_LOG).astype(jnp.int32)

            def k_step(c):
                kb = c[0] - 1
                tile(qrows, kb, False)
                return kb, live()

            first, _ = lax.while_loop(lambda c: (c[0] > 0) & (c[1] > 0), k_step, (jnp.maximum(qb - 1, 0), live()))
            for hh in range(2):
                o_ref[qrows, hh * HEAD_DIM:(hh + 1) * HEAD_DIM] = acc[hh]
                c_ref[hh, qrows, :] = jnp.where(lane == LANES - 1, first.astype(F32), cmat[hh])
            return 0

        lax.fori_loop(0, nt, q_step, 0)

    pair = pl.BlockSpec((s, LANES), lambda h: (0, h))
    gain = pl.BlockSpec((1, LANES), lambda h: (0, 0))
    return _hosted_call(
        body, "attn_fwd", HEADS // 2, [q_all, k_all, v_all, q_gain2, k_gain2],
        [pair, pair, pair, gain, gain], [pair, pl.BlockSpec((2, s, LANES), lambda h: (h, 0, 0))],
        [jax.ShapeDtypeStruct((s, D), F32), jax.ShapeDtypeStruct((HEADS, s, LANES), F32)],
        [pltpu.VMEM((2, s, HEAD_DIM), BF16)] * 3
        + [pltpu.VMEM((TILE, TILE), BF16), pltpu.VMEM((2, TILE, HEAD_DIM), F32), pltpu.VMEM((2, TILE, TILE), F32),
           pltpu.VMEM((2, TILE, LANES), F32)], rider)


def _layer_b_out_fwd(o, zb, x2, p1, target, w_out, w_ple, w_gate):
    s = o.shape[0]

    def body(o_ref, zb_ref, x2_ref, p_ref, t_ref, wout_ref, wple_ref, wgate_ref,
             yb_ref, x3_ref, e_ref, gt_ref, dx4_ref, loss_ref):
        zb = zb_ref[...]
        yb = (o_ref[...] * (zb * _sigmoid(zb))).astype(BF16)
        yb_ref[...] = yb
        x3 = x2_ref[...] + _dot(yb, wout_ref[...])
        x3_ref[...] = x3
        x4 = _ple_fwd(p_ref, x3, wple_ref, wgate_ref, e_ref, gt_ref)
        d = x4 - t_ref[...]
        dx4_ref[...] = d * (1.0 / D)

        @pl.when(pl.program_id(0) == 0)
        def _():
            loss_ref[...] = jnp.zeros((1, D), F32)

        loss_ref[...] += jnp.sum(d * d, axis=0, keepdims=True)

    row_outs = [(D, BF16), (D, F32), (D, F32), (D, F32), (D, F32)]
    return _rows_call(body, "layer_b_out_fwd", s, [o, zb, x2, p1, target], [w_out, w_ple, w_gate], row_outs,
                      const_outs=[((1, D), F32)])


def _ple_bwd(dxo, e_ref, gt_ref, wgate_ref, de_ref, dgp_ref):
    e = e_ref[...]
    gt = gt_ref[...]
    de_ref[...] = (dxo * gt).astype(BF16)
    dgp = (dxo * e * gt * (1.0 - gt)).astype(BF16)
    dgp_ref[...] = dgp
    return dxo + _dot_nt(dgp, wgate_ref[...])


def _silu_grads(z):
    sg = _sigmoid(z)
    return z * sg, sg * (1.0 + z * (1.0 - sg))


def _layer_b_out_bwd(dx4, e1, gt1, o, zb, w_gate, w_out):
    s = dx4.shape[0]

    def body(dx4_ref, e_ref, gt_ref, o_ref, zb_ref, wgate_ref, wout_ref,
             de_ref, dgp_ref, dx3_ref, do_ref, dzb_ref):
        dx3 = _ple_bwd(dx4_ref[...], e_ref, gt_ref, wgate_ref, de_ref, dgp_ref)
        dx3_ref[...] = dx3
        dyb = _dot_nt(dx3.astype(BF16), wout_ref[...])
        silu, dsilu = _silu_grads(zb_ref[...])
        do_ref[...] = (dyb * silu).astype(BF16)
        dzb_ref[...] = (dyb * o_ref[...] * dsilu).astype(BF16)

    row_outs = [(D, BF16), (D, BF16), (D, F32), (D, BF16), (D, BF16)]
    return _rows_call(body, "layer_b_out_bwd", s, [dx4, e1, gt1, o, zb], [w_gate, w_out], row_outs)


def _attn_bwd(q_all, k_all, v_all, q_gain2, k_gain2, d_o, csave, rider=None):
    s = q_all.shape[0]
    nt = s // TILE

    def body(q_ref, k_ref, v_ref, qg_ref, kg_ref, do_ref, c_ref,
             dq_ref, dk_ref, dv_ref, dqg_ref, dkg_ref,
             qs, ks, vs, dos, qt, dot_t, tri_a, tri_b, dqa, dkt, dvt, dqb, left):
        tri_a[...] = _tri(True)
        tri_b[...] = _tri(False)
        lane = lax.broadcasted_iota(jnp.int32, (TILE, LANES), 1)
        qn, qr = _pair_norm(q_ref[...])
        kn, kr = _pair_norm(k_ref[...])
        qsc = qn * qg_ref[...] * SB_SCALE
        ksc = (kn * kg_ref[...]).astype(BF16)
        q_t = qsc.T.astype(BF16)
        do_t = do_ref[...].astype(F32).T.astype(BF16)
        for j in range(nt):
            qt[j] = q_t[:, j * TILE:(j + 1) * TILE]
            dot_t[j] = do_t[:, j * TILE:(j + 1) * TILE]
        qsc = qsc.astype(BF16)
        for hh in range(2):
            sl = slice(hh * HEAD_DIM, (hh + 1) * HEAD_DIM)
            qs[hh] = qsc[:, sl]
            ks[hh] = ksc[:, sl]
            vs[hh] = v_ref[:, sl]
            dos[hh] = do_ref[:, sl]

        def tile(qb, qrows, kb, diagonal):
            rows = pl.ds(pl.multiple_of(kb * TILE, TILE), TILE)
            heads = range(2)
            kblk = [ks[hh, rows, :] for hh in heads]
            logits = [_tile_logits(qs[hh, qrows, :], kblk[hh], diagonal) for hh in heads]
            later = _split_dot_many([lk for _, lk, _ in logits], tri_a[...])
            a, g = [], []
            for hh in heads:
                mask, _, ls = logits[hh]
                right = jnp.sum(jnp.where(lane == kb, c_ref[hh, qrows, :], 0.0), axis=1, keepdims=True)
                a_h = jnp.exp(ls + later[hh] + right)
                a.append(jnp.where(mask, a_h, 0.0) if diagonal else a_h)
                g.append(a[hh] * _dot_nt(dos[hh, qrows, :], vs[hh, rows, :]))
            before = _split_dot_many(g, tri_b[...])
            for hh in heads:
                sl = slice(hh * HEAD_DIM, (hh + 1) * HEAD_DIM)
                mask, _, ls = logits[hh]
                beta = jnp.exp(ls)
                lf = left[hh]
                dl = g[hh] * (1.0 - beta) - (before[hh] + lf) * beta
                if diagonal:
                    dl = jnp.where(mask, dl, 0.0)
                dl = dl.astype(BF16)
                left[hh] = lf + jnp.sum(g[hh], axis=1, keepdims=True)
                dqb[hh] += _dot(dl, kblk[hh])
                dk_t, dv_t = _dot(qt[qb, sl, :], dl), _dot(dot_t[qb, sl, :], a[hh].astype(BF16))
                if diagonal:
                    dkt[kb, sl, :] = dk_t
                    dvt[kb, sl, :] = dv_t
                else:
                    dkt[kb, sl, :] += dk_t
                    dvt[kb, sl, :] += dv_t

        def left_and_diagonal(qb, qrows):
            here = pl.ds(pl.multiple_of(qb * TILE, TILE), TILE)
            beside = pl.ds(pl.multiple_of((qb - 1) * TILE, TILE), TILE)
            heads = range(2)
            q = [qs[hh, qrows, :] for hh in heads]
            do = [dos[hh, qrows, :] for hh in heads]
            k_d, k_l = [ks[hh, here, :] for hh in heads], [ks[hh, beside, :] for hh in heads]
            on_diag = [_tile_logits(q[hh], k_d[hh], True) for hh in heads]
            on_left = [_tile_logits(q[hh], k_l[hh], False) for hh in heads]
            later = _split_dot_many([lk for _, lk, _ in on_diag + on_left], tri_a[...])
            a_d, a_l, g_d, g_l = [], [], [], []
            for hh in heads:
                mask, lk_d, ls_d = on_diag[hh]
                a_d.append(jnp.where(mask, jnp.exp(ls_d + later[hh]), 0.0))
                a_l.append(jnp.exp(on_left[hh][2] + later[2 + hh] + jnp.sum(lk_d, axis=1, keepdims=True)))
                g_d.append(a_d[hh] * _dot_nt(do[hh], vs[hh, here, :]))
                g_l.append(a_l[hh] * _dot_nt(do[hh], vs[hh, beside, :]))
            before = _split_dot_many(g_l + g_d, tri_b[...])
            for hh in heads:
                sl = slice(hh * HEAD_DIM, (hh + 1) * HEAD_DIM)
                beta_l, beta_d = jnp.exp(on_left[hh][2]), jnp.exp(on_diag[hh][2])
                dl_l = (g_l[hh] * (1.0 - beta_l) - before[hh] * beta_l).astype(BF16)
                carried = jnp.sum(g_l[hh], axis=1, keepdims=True)
                dl_d = g_d[hh] * (1.0 - beta_d) - (before[2 + hh] + carried) * beta_d
                dl_d = jnp.where(on_diag[hh][0], dl_d, 0.0).astype(BF16)
                dqa[qrows, sl] = (_dot(dl_l, k_l[hh]) + _dot(dl_d, k_d[hh])) * SB_SCALE
                dkt[qb - 1, sl, :] += _dot(qt[qb, sl, :], dl_l)
                dkt[qb, sl, :] = _dot(qt[qb, sl, :], dl_d)
                dvt[qb - 1, sl, :] += _dot(dot_t[qb, sl, :], a_l[hh].astype(BF16))
                dvt[qb, sl, :] = _dot(dot_t[qb, sl, :], a_d[hh].astype(BF16))

        def q_step(qb, _):
            qrows = pl.ds(pl.multiple_of(qb * TILE, TILE), TILE)
            first = jnp.max(jnp.where(lane == LANES - 1, c_ref[0, qrows, :], 0.0)).astype(jnp.int32)
            usual = (qb > 0) & (first == qb - 1)

            @pl.when(usual)
            def _():
                left_and_diagonal(qb, qrows)

            @pl.when(jnp.logical_not(usual))
            def _():
                dqb[...] = jnp.zeros((2, TILE, HEAD_DIM), F32)
                left[...] = jnp.zeros((2, TILE, TILE), F32)

                def k_step(kb, _):
                    tile(qb, qrows, kb, False)
                    return 0

                lax.fori_loop(first, qb, k_step, 0)
                tile(qb, qrows, qb, True)
                for hh in range(2):
                    dqa[qrows, hh * HEAD_DIM:(hh + 1) * HEAD_DIM] = dqb[hh] * SB_SCALE

            return 0

        lax.fori_loop(0, nt, q_step, 0)

        def norm_bwd(dy, xn, r, g_ref, dx_ref, dg_ref):
            dg_ref[...] = jnp.sum(dy * xn, axis=0, keepdims=True)
            dxn = dy * g_ref[...]
            dx_ref[...] = (r * (dxn - xn * (_half_sums(dxn * xn) * (1.0 / HEAD_DIM)))).astype(BF16)

        norm_bwd(dqa[...], qn, qr, qg_ref, dq_ref, dqg_ref)
        for j in range(nt):
            dqa[j * TILE:(j + 1) * TILE, :] = dkt[j].T
            dv_ref[j * TILE:(j + 1) * TILE, :] = dvt[j].T.astype(BF16)
        norm_bwd(dqa[...], kn, kr, kg_ref, dk_ref, dkg_ref)

    pair = pl.BlockSpec((s, LANES), lambda h: (0, h))
    gain = pl.BlockSpec((1, LANES), lambda h: (0, 0))
    dgain = pl.BlockSpec((None, 1, LANES), lambda h: (h, 0, 0))
    return _hosted_call(
        body, "attn_bwd", HEADS // 2, [q_all, k_all, v_all, q_gain2, k_gain2, d_o, csave],
        [pair, pair, pair, gain, gain, pair, pl.BlockSpec((2, s, LANES), lambda h: (h, 0, 0))],
        [pair, pair, pair, dgain, dgain],
        [jax.ShapeDtypeStruct((s, D), BF16)] * 3 + [jax.ShapeDtypeStruct((HEADS // 2, 1, LANES), F32)] * 2,
        [pltpu.VMEM((2, s, HEAD_DIM), BF16)] * 4
        + [pltpu.VMEM((nt, LANES, TILE), BF16)] * 2 + [pltpu.VMEM((TILE, TILE), BF16)] * 2
        + [pltpu.VMEM((s, LANES), F32)] + [pltpu.VMEM((nt, LANES, TILE), F32)] * 2
        + [pltpu.VMEM((2, TILE, HEAD_DIM), F32), pltpu.VMEM((2, TILE, TILE), F32)], rider)


def _norm_bwd_rows(dh, x, gain, dgain_ref):
    r = _rms(x)
    n = x * r
    dgain_ref[...] += jnp.sum(dh * n, axis=0, keepdims=True)
    dn = dh * gain
    return r * (dn - n * jnp.mean(dn * n, axis=-1, keepdims=True))


def _layer_b_in_bwd(dq, dzb, dk, dv, x2, dx3, w_bin, w_kv, b_norm, kv_norm, rider=None):
    s = x2.shape[0]

    def body(dq_ref, dzb_ref, dk_ref, dv_ref, x_ref, dx3_ref, wbin_ref, wkv_ref, bn_ref, kvn_ref,
             dx2_ref, dbn_ref, dkvn_ref):
        @pl.when(pl.program_id(0) == 0)
        def _():
            dbn_ref[...] = jnp.zeros((1, D), F32)
            dkvn_ref[...] = jnp.zeros((1, D), F32)

        dhb = jnp.zeros((TM, D), F32)
        dhkv = jnp.zeros((TM, D), F32)
        for j in range(N_DEV):
            cols = slice((j % 4) * COLS, (j % 4 + 1) * COLS)
            dhb = dhb + _dot_nt((dq_ref if j < 4 else dzb_ref)[:, cols], wbin_ref[j])
            dhkv = dhkv + _dot_nt((dk_ref if j < 4 else dv_ref)[:, cols], wkv_ref[j])
        x = x_ref[...]
        dx2 = dx3_ref[...] + _norm_bwd_rows(dhb, x, bn_ref[...], dbn_ref)
        dx2_ref[...] = dx2 + _norm_bwd_rows(dhkv, x, kvn_ref[...], dkvn_ref)

    return _rows_call(body, "layer_b_in_bwd", s, [dq, dzb, dk, dv, x2, dx3], [w_bin, w_kv, b_norm, kv_norm],
                      [(D, F32)], const_outs=[((1, D), F32), ((1, D), F32)], rider=rider)


def _layer_a_out_bwd(dx2, e0, gt0, z, m, w_gate, w_out, a_scale, w_group, rider=None):
    s = dx2.shape[0]
    tm = TM
    nb = s // tm

    def body(dx2_ref, e_ref, gt_ref, z_ref, m_ref, wgate_ref, wout_ref, as_ref, wg_ref,
             de_ref, dgp_ref, dx1_ref, dm_ref, duz_ref, das_ref, ext):
        i = pl.program_id(0)

        @pl.when(i == 0)
        def _():
            das_ref[...] = jnp.zeros((1, D), F32)
            ext[tm:tm + HALO, :] = jnp.zeros((HALO, D), F32)

        dx1 = _ple_bwd(dx2_ref[...], e_ref, gt_ref, wgate_ref, de_ref, dgp_ref)
        dx1_ref[...] = dx1
        dy = _dot_nt(dx1.astype(BF16), wout_ref[...])
        silu, dsilu = _silu_grads(z_ref[...])
        m = m_ref[...]
        dmixed = dy * silu
        duz_ref[:, D:] = (dy * (m * as_ref[...]) * dsilu).astype(BF16)
        das_ref[...] += jnp.sum(dmixed * m, axis=0, keepdims=True)
        dm_ref[...] = (dmixed * as_ref[...]).astype(BF16)
        t = (nb - 1 - i) * tm + lax.broadcasted_iota(jnp.int32, (tm, 1), 0)
        n_ext = tm + HALO
        for g in range(N_GROUPS):
            w = 2 ** (g + 1)
            cols = slice(g * GROUP_DIM, (g + 1) * GROUP_DIM)
            dpool = _dot_nt(dm_ref[:, cols], wg_ref[g])
            ext[0:tm, cols] = dpool / jnp.minimum(t + 1, w).astype(F32)
            acc = ext[:, cols]
            k = 1
            while k < w:
                acc = acc + pltpu.roll(acc, n_ext - k, 0)
                k *= 2
            duz_ref[:, cols] = (acc[:tm] - dpool).astype(BF16)
        ext[tm:tm + HALO, :] = ext[0:HALO, :]

    row_outs = [(D, BF16), (D, BF16), (D, F32), (D, BF16), (2 * D, BF16)]
    return _rows_call(body, "layer_a_out_bwd", s, [dx2, e0, gt0, z, m], [w_gate, w_out, a_scale, w_group],
                      row_outs, const_outs=[((1, D), F32)], scratch=[pltpu.VMEM((tm + HALO, D), F32)],
                      reverse=True, rider=rider)


def _layer_a_in_bwd(duz, x0, dx1, w_in, a_norm, rider=None):
    s = x0.shape[0]

    def body(duz_ref, x_ref, dx1_ref, win_ref, an_ref, dx0_ref, dan_ref):
        @pl.when(pl.program_id(0) == 0)
        def _():
            dan_ref[...] = jnp.zeros((1, D), F32)

        dh = jnp.zeros((TM, D), F32)
        for j in range(N_DEV):
            dh = dh + _dot_nt(duz_ref[:, j * COLS:(j + 1) * COLS], win_ref[j])
        dx0_ref[...] = dx1_ref[...] + _norm_bwd_rows(dh, x_ref[...], an_ref[...], dan_ref)

    return _rows_call(body, "layer_a_in_bwd", s, [duz, x0, dx1], [w_in, a_norm], [(D, F32)],
                      const_outs=[((1, D), F32)], rider=rider)


def _wgrad(a, b, name, n_split=1, rider=None):
    bs = list(b) if isinstance(b, (list, tuple)) else [b]
    s, k = a.shape
    n = sum(part.shape[1] for part in bs)
    tk = TM
    nb = n // n_split

    def body(a_ref, *refs):
        o_ref = refs[-1]
        lhs = a_ref[...].astype(BF16)
        done = 0
        for b_ref in refs[:-1]:
            res = _dot_tn(lhs, b_ref[...].astype(BF16))
            if n_split == 1:
                o_ref[...] = res.astype(BF16)
            else:
                for j in range(res.shape[1] // nb):
                    o_ref[done + j] = res[:, j * nb:(j + 1) * nb].astype(BF16)
                done += res.shape[1] // nb

    if n_split == 1:
        b_specs = [pl.BlockSpec((s, n), lambda i: (0, 0))]
        out_spec = pl.BlockSpec((tk, n), lambda i: (i, 0))
        out_shape = jax.ShapeDtypeStruct((k, n), BF16)
    else:
        b_specs = [pl.BlockSpec(part.shape, lambda i: (0, 0)) for part in bs]
        out_spec = pl.BlockSpec((n_split, tk, nb), lambda i: (0, i, 0))
        out_shape = jax.ShapeDtypeStruct((n_split, k, nb), BF16)
    res = _hosted_call(body, name, k // tk, [a] + bs, [pl.BlockSpec((s, tk), lambda i: (0, i))] + b_specs,
                       [out_spec], [out_shape], [], rider)
    return res[0] if rider is None else res


def _wgrad_layer_a_in(h0, duz, pooled, dm, p0, de0, rider):
    s = h0.shape[0]
    n_ple = p0.shape[1]

    def body(h_ref, duz_ref, pooled_ref, dm_ref, p_ref, de_ref, in_ref, group_ref, ple_ref):
        res = _dot_tn(h_ref[...], duz_ref[...])
        for j in range(N_DEV):
            in_ref[j] = res[:, j * COLS:(j + 1) * COLS].astype(BF16)
        group_ref[...] = _dot_tn(pooled_ref[...], dm_ref[...]).astype(BF16)

        @pl.when(pl.program_id(0) == 0)
        def _():
            ple = _dot_tn(p_ref[...].astype(BF16), de_ref[...])
            for j in range(N_DEV):
                ple_ref[j] = ple[:, j * ROWS:(j + 1) * ROWS].astype(BF16)

    block = pl.BlockSpec((s, TM), lambda i: (0, i))
    return _hosted_call(
        body, "wgrad_layer_a_in", D // TM, [h0, duz, pooled, dm, p0, de0],
        [block, pl.BlockSpec(duz.shape, lambda i: (0, 0)), block, block,
         pl.BlockSpec(p0.shape, lambda i: (0, 0)), pl.BlockSpec(de0.shape, lambda i: (0, 0))],
        [pl.BlockSpec((N_DEV, TM, COLS), lambda i: (0, i, 0)), pl.BlockSpec((None, TM, GROUP_DIM), lambda i: (i, 0, 0)),
         pl.BlockSpec((N_DEV, n_ple, ROWS), lambda i: (0, 0, 0))],
        [jax.ShapeDtypeStruct((N_DEV, D, COLS), BF16), jax.ShapeDtypeStruct((N_GROUPS, GROUP_DIM, GROUP_DIM), BF16),
         jax.ShapeDtypeStruct((N_DEV, n_ple, ROWS), BF16)], [], rider)


def _cast_shards(shards):
    n = len(shards)
    layers = [a.shape[0] if a.ndim == 3 else 0 for a in shards]

    def body(*refs):
        outs = iter(refs[n:])
        for a in range(n):
            if layers[a]:
                for t in range(layers[a]):
                    next(outs)[...] = refs[a][t].astype(BF16)
            else:
                next(outs)[...] = refs[a][...].astype(BF16)

    out_shape = []
    for a, k in zip(shards, layers):
        out_shape += [jax.ShapeDtypeStruct(a.shape[-2:], BF16)] * max(k, 1)
    vmem = pl.BlockSpec(memory_space=pltpu.VMEM)
    return pl.pallas_call(
        body, name="cast_shards", in_specs=[vmem] * n, out_specs=[vmem] * len(out_shape), out_shape=out_shape,
        compiler_params=pltpu.CompilerParams(vmem_limit_bytes=VMEM_LIMIT),
    )(*shards)


def _adamw(w, g, m, v):
    m = ADAM_B1 * m + (1.0 - ADAM_B1) * g
    v = ADAM_B2 * v + (1.0 - ADAM_B2) * jnp.square(g)
    m_hat = m / (1.0 - ADAM_B1 ** ADAM_STEP)
    v_hat = v / (1.0 - ADAM_B2 ** ADAM_STEP)
    delta = -ADAM_LR * (m_hat / (jnp.sqrt(v_hat) + ADAM_EPS) + ADAM_WD * w)
    return delta, m, v


def _place():
    return lax.axis_index("x"), lax.axis_index("y"), lax.axis_index("c")


def _all_gather(shards):
    return _alone("all_gather_weights", _GatherRider(shards))


def _alone(name, rider):
    n_in, n_out = len(rider.arrays), len(rider.out_shape())

    def body(*refs):
        for phase in rider.bind(refs[:n_in], refs[n_in:n_in + n_out], refs[n_in + n_out:]):
            phase()

    return pl.pallas_call(
        body, name=name, in_specs=[HBM_SPEC] * n_in, out_specs=[HBM_SPEC] * n_out,
        out_shape=rider.out_shape(), scratch_shapes=rider.scratch(),
        compiler_params=pltpu.CompilerParams(vmem_limit_bytes=VMEM_LIMIT),
    )(*rider.arrays)


class _GatherRider:
    WHEN = (0.0, 0.7, 1.0)

    def __init__(self, shards):
        self.arrays = list(shards)

    def out_shape(self):
        return _Gather.out_shape(self.arrays)

    def scratch(self):
        return _Gather.semaphores(len(self.arrays))

    def bind(self, ins, outs, scratch):
        moving = _Gather(ins, outs, *scratch)
        return moving.start, moving.forward, moving.finish


class _ReduceRider:
    WHEN = (0.0, 0.15, 0.5, 1.0)

    def __init__(self, partials):
        self.arrays = list(partials)

    def out_shape(self):
        return [jax.ShapeDtypeStruct(a.shape[1:], F32) for a in self.arrays]

    def scratch(self):
        n = len(self.arrays)
        dma = pltpu.SemaphoreType.DMA

        def blocks(k):
            return [pltpu.VMEM((k,) + a.shape[1:], BF16) for a in self.arrays]

        halves = [pltpu.VMEM((2, a.shape[1] // 2) + a.shape[2:], BF16) for a in self.arrays]
        return (blocks(4) + blocks(4) + halves + blocks(2) + [pltpu.VMEM(a.shape[1:], F32) for a in self.arrays]
                + [dma((4 * n,)), dma((4 * n,)), dma((4 * n,)), dma((2 * n,)), dma((2 * n,)),
                   dma((2 * n,)), dma((2 * n,)), dma((n,))])

    def bind(self, ins, outs, scratch):
        n = len(ins)
        mine, landed, halves, arrived, total = (scratch[i * n:(i + 1) * n] for i in range(5))
        send1, recv1, local1, send_h, recv_h, send2, recv2, out_sems = scratch[5 * n:]
        x, y, c = _place()
        plane = 2 * x + y
        via = [(x, 1 - y, c), (1 - x, y, c)]
        nbr = [(1 - x, y, c), (x, 1 - y, c)]
        nbr_block = [2 * (1 - x) + y, 2 * x + (1 - y)]
        diag_block = 2 * (1 - x) + (1 - y)

        def to_sibling(a, k):
            return pltpu.make_async_remote_copy(
                src_ref=ins[a].at[2 * k + (1 - c)], dst_ref=landed[a].at[k],
                send_sem=send1.at[4 * a + k], recv_sem=recv1.at[4 * a + k],
                device_id=(x, y, 1 - c), device_id_type=MESH)

        def own_block(a, k):
            return pltpu.make_async_copy(ins[a].at[2 * k + c], mine[a].at[k], local1.at[4 * a + k])

        def half_of(a, ref, h):
            rows = self.arrays[a].shape[1] // 2
            return ref.at[pl.ds(h * rows, rows)]

        def half_out(a, h):
            return pltpu.make_async_remote_copy(
                src_ref=half_of(a, mine[a].at[diag_block], h), dst_ref=halves[a].at[h],
                send_sem=send_h.at[2 * a + h], recv_sem=recv_h.at[2 * a + h],
                device_id=via[h], device_id_type=MESH)

        def to_owner(a, h):
            return pltpu.make_async_remote_copy(
                src_ref=mine[a].at[nbr_block[h]], dst_ref=arrived[a].at[h],
                send_sem=send2.at[2 * a + h], recv_sem=recv2.at[2 * a + h],
                device_id=nbr[h], device_id_type=MESH)

        def result(a):
            return pltpu.make_async_copy(total[a], outs[a], out_sems.at[a])

        def exchange_cores():
            for a in range(n):
                for k in range(4):
                    to_sibling(a, k).start()
                    own_block(a, k).start()

        def pair_sums():
            for a in range(n):
                for k in range(4):
                    own_block(a, k).wait()
                    to_sibling(a, k).wait_recv()
                total[a][...] = mine[a][plane].astype(F32) + landed[a][plane].astype(F32)
                for k in range(4):
                    mine[a][k] = (mine[a][k].astype(F32) + landed[a][k].astype(F32)).astype(BF16)
                for h in range(2):
                    half_out(a, h).start()

        def fold_and_send():
            for a in range(n):
                rows = self.arrays[a].shape[1] // 2
                for h in range(2):
                    half_out(a, h).wait_recv()
                    part = mine[a].at[nbr_block[h]]
                    span = slice(h * rows, (h + 1) * rows)
                    part[span] = (part[span].astype(F32) + halves[a][h].astype(F32)).astype(BF16)
                    to_owner(a, h).start()

        def finish():
            for a in range(n):
                for h in range(2):
                    to_owner(a, h).wait_recv()
                    total[a][...] += arrived[a][h].astype(F32)
                result(a).start()
            for a in range(n):
                for k in range(4):
                    to_sibling(a, k).wait_send()
                for h in range(2):
                    half_out(a, h).wait_send()
                    to_owner(a, h).wait_send()
                result(a).wait()

        return exchange_cores, pair_sums, fold_and_send, finish


class _Gather:
    COPIES = 9

    def __init__(self, ins, outs, send_sems, recv_sems, local_sems):
        self.ins, self.outs = ins, outs
        self.send_sems, self.recv_sems, self.local_sems = send_sems, recv_sems, local_sems
        self.x, self.y, self.c = _place()

    @staticmethod
    def out_shape(shards):
        return [jax.ShapeDtypeStruct((N_DEV,) + a.shape, a.dtype) for a in shards]

    @staticmethod
    def semaphores(n):
        dma = pltpu.SemaphoreType.DMA
        return [dma((_Gather.COPIES * n,)), dma((_Gather.COPIES * n,)), dma((n,))]

    def _copy(self, a, k, block, to, own=False, half=None):
        px, py, pc = block
        slot = self.outs[a].at[4 * px + 2 * py + pc]
        if half is not None:
            rows = slot.shape[0] // 2
            slot = slot.at[pl.ds(half * rows, rows)]
        return pltpu.make_async_remote_copy(
            src_ref=self.ins[a] if own else slot, dst_ref=slot,
            send_sem=self.send_sems.at[self.COPIES * a + k], recv_sem=self.recv_sems.at[self.COPIES * a + k],
            device_id=to, device_id_type=MESH)

    def _local(self, a):
        return pltpu.make_async_copy(self.ins[a], self.outs[a].at[4 * self.x + 2 * self.y + self.c],
                                     self.local_sems.at[a])

    def _plan(self, a, c):
        x, y = self.x, self.y
        me, sibling = (x, y, c), (x, y, 1 - c)
        xn, yn, dg = (1 - x, y, c), (x, 1 - y, c), (1 - x, 1 - y, c)
        return [
            self._copy(a, 0, me, sibling, own=True), self._copy(a, 1, me, xn, own=True),
            self._copy(a, 2, me, yn, own=True),
            self._copy(a, 3, xn, yn, half=0), self._copy(a, 4, yn, xn, half=1),
            self._copy(a, 5, xn, sibling), self._copy(a, 6, yn, sibling),
            self._copy(a, 7, dg, sibling, half=0), self._copy(a, 8, dg, sibling, half=1),
        ]

    def _arrivals(self, a):
        x, y, c = self.x, self.y, self.c
        me = (x, y, c)
        xn, yn, dg = (1 - x, y, c), (x, 1 - y, c), (1 - x, 1 - y, c)
        other = 1 - c
        return [
            self._copy(a, 0, (x, y, other), me), self._copy(a, 1, xn, me), self._copy(a, 2, yn, me),
            self._copy(a, 3, dg, me, half=0), self._copy(a, 4, dg, me, half=1),
            self._copy(a, 5, (1 - x, y, other), me), self._copy(a, 6, (x, 1 - y, other), me),
            self._copy(a, 7, (1 - x, 1 - y, other), me, half=0), self._copy(a, 8, (1 - x, 1 - y, other), me, half=1),
        ]

    def start(self):
        for a in range(len(self.ins)):
            self._local(a).start()
            for cp in self._plan(a, self.c)[:3]:
                cp.start()

    def forward(self):
        for a in range(len(self.ins)):
            sends, lands = self._plan(a, self.c), self._arrivals(a)
            lands[1].wait_recv()
            sends[3].start()
            sends[5].start()
            lands[2].wait_recv()
            sends[4].start()
            sends[6].start()

    def finish(self):
        n = len(self.ins)
        for a in range(n):
            sends, lands = self._plan(a, self.c), self._arrivals(a)
            lands[3].wait_recv()
            sends[7].start()
            lands[4].wait_recv()
            sends[8].start()
        for a in range(n):
            lands = self._arrivals(a)
            for k in (0, 5, 6, 7, 8):
                lands[k].wait_recv()
        for a in range(n):
            for cp in self._plan(a, self.c):
                cp.wait_send()
            self._local(a).wait()


def _adamw_all(name, ws, gs, ms, vs):
    n = len(ws)
    per_layer = [isinstance(g, tuple) for g in gs]
    flat_g = [part for g in gs for part in (g if isinstance(g, tuple) else (g,))]

    def body(*refs):
        w, refs = refs[:n], refs[n:]
        g, refs = refs[:len(flat_g)], refs[len(flat_g):]
        m, v, outs = refs[:n], refs[n:2 * n], refs[2 * n:]
        stacked = iter(outs[3 * n:])
        parts = iter(g)
        for a in range(n):
            if per_layer[a]:
                whole = next(stacked)
                for t in range(len(gs[a])):
                    grad = next(parts)[...]
                    whole[t] = grad
                    outs[a][t], outs[n + a][t], outs[2 * n + a][t] = _adamw(w[a][t], grad, m[a][t], v[a][t])
            else:
                outs[a][...], outs[n + a][...], outs[2 * n + a][...] = _adamw(
                    w[a][...], next(parts)[...], m[a][...], v[a][...])

    shapes = [jax.ShapeDtypeStruct(a.shape, F32) for a in ws]
    vmem = pl.BlockSpec(memory_space=pltpu.VMEM)
    n_out = 3 * n + sum(per_layer)
    res = pl.pallas_call(
        body, name=name, in_specs=[vmem] * (3 * n + len(flat_g)), out_specs=[vmem] * n_out,
        out_shape=shapes * 3 + [s for s, p in zip(shapes, per_layer) if p],
        compiler_params=pltpu.CompilerParams(vmem_limit_bytes=VMEM_LIMIT),
    )(*ws, *flat_g, *ms, *vs)
    stacked = iter(res[3 * n:])
    return [(next(stacked) if per_layer[a] else gs[a], res[a], res[n + a], res[2 * n + a]) for a in range(n)]


def _all_reduce_small(rows, gain_parts):
    def body(rows_ref, dqg_ref, dkg_ref, out_ref, buf, send_sems, recv_sems):
        x, y, c = _place()
        me = 4 * x + 2 * y + c
        buf[0] = rows_ref[...]
        for row, part in ((4, dqg_ref), (5, dkg_ref)):
            both = jnp.sum(part[...].reshape(HEADS // 2, LANES), axis=0, keepdims=True)
            buf[0, row:row + 1, 0:HEAD_DIM] = both[:, :HEAD_DIM] + both[:, HEAD_DIM:]
        copies = []
        for r in range(1, N_DEV):
            bx, by, bc = (r >> 2) & 1, (r >> 1) & 1, r & 1
            to = (x ^ bx, y ^ by, c ^ bc)
            copies.append(pltpu.make_async_remote_copy(
                src_ref=buf.at[0], dst_ref=buf.at[r], send_sem=send_sems.at[r - 1], recv_sem=recv_sems.at[r - 1],
                device_id=to, device_id_type=MESH))
        for cp in copies:
            cp.start()
        for cp in copies:
            cp.wait_recv()
        for cp in copies:
            cp.wait_send()
        tot = buf[me]
        for j in range(1, N_DEV):
            tot = tot + buf[j ^ me]
        out_ref[...] = tot
        loss = (0.5 / D) * jnp.sum(tot[6:7, :], axis=1, keepdims=True)
        out_ref[6:7, :] = jnp.broadcast_to(loss, (1, D))

    vmem = pl.BlockSpec(memory_space=pltpu.VMEM)
    return pl.pallas_call(
        body, name="all_reduce_small", in_specs=[vmem] * 3, out_specs=vmem,
        out_shape=jax.ShapeDtypeStruct((8, D), F32),
        scratch_shapes=[pltpu.VMEM((N_DEV, 8, D), F32), pltpu.SemaphoreType.DMA((N_DEV - 1,)),
                        pltpu.SemaphoreType.DMA((N_DEV - 1,))],
    )(rows, *gain_parts)


def kernel(x, p, a_norm, a_w_in, a_w_group, a_scale, a_w_out, kv_norm, w_kv, k_norm, b_norm, b_w_in, b_q_norm, b_w_out, ple_w, ple_gate_w, loss_target, m_a_norm, m_a_w_in, m_a_w_group, m_a_scale, m_a_w_out, m_kv_norm, m_w_kv, m_k_norm, m_b_norm, m_b_w_in, m_b_q_norm, m_b_w_out, m_ple_w, m_ple_gate_w, v_a_norm, v_a_w_in, v_a_w_group, v_a_scale, v_a_w_out, v_kv_norm, v_w_kv, v_k_norm, v_b_norm, v_b_w_in, v_b_q_norm, v_b_w_out, v_ple_w, v_ple_gate_w):
    xi, yi, ci = _place()
    me = 4 * xi + 2 * yi + ci

    big = {
        "a_w_in": a_w_in.reshape(D, COLS), "a_w_group": a_w_group.reshape(N_GROUPS * GROUP_ROWS, GROUP_DIM),
        "a_w_out": a_w_out.reshape(ROWS, D), "w_kv": w_kv, "b_w_in": b_w_in.reshape(D, COLS),
        "b_w_out": b_w_out.reshape(ROWS, D), "ple_w": ple_w, "ple_gate_w": ple_gate_w,
    }
    names = ["a_w_in", "a_w_group", "a_w_out", "w_kv", "b_w_in", "b_w_out", "ple_w0", "ple_w1", "gate0", "gate1"]
    cast = dict(zip(names, _cast_shards(list(big.values()))))
    small = jnp.concatenate([a_norm, a_scale, jnp.zeros((14, ROWS), F32)], axis=0)
    first = ["a_w_in", "a_w_group", "a_w_out", "ple_w0", "gate0"]
    behind_a = ["w_kv", "b_w_in"]
    behind_attn = ["b_w_out", "ple_w1", "gate1"]
    gathered = _all_gather([cast[k] for k in first] + [small])
    full = dict(zip(first, gathered[:-1]))
    small_all = gathered[-1]
    a_norm_f = small_all[:, 0, :].reshape(1, D)
    a_scale_f = small_all[:, 1, :].reshape(1, D)
    w_a_in = full["a_w_in"]
    w_a_out = full["a_w_out"].reshape(D, D)
    w_gate0 = full["gate0"].reshape(D, D)
    w_ple0 = full["ple_w0"]
    w_group = full["a_w_group"].reshape(N_DEV, N_GROUPS, GROUP_ROWS, GROUP_DIM).transpose(1, 0, 2, 3).reshape(
        N_GROUPS, GROUP_DIM, GROUP_DIM)
    kvn, bn = kv_norm.reshape(1, D), b_norm
    kg, qg = k_norm.reshape(1, HEAD_DIM), b_q_norm

    x0, p0, p1, target = x[0], p[0, 0], p[1, 0], loss_target[0]
    h0, z, pooled, mcat, y, x1, e0, gt0, x2, w_kv_f, w_b_in = _layer_a_fwd(
        x0, p0, a_norm_f, a_scale_f, w_a_in, w_group, w_a_out, w_ple0, w_gate0,
        rider=_GatherRider([cast[k] for k in behind_a]))
    hkv, hb, k_all, v_all, q_all, zb = _layer_b_in_fwd(x2, kvn, bn, w_kv_f, w_b_in)
    qg2, kg2 = jnp.concatenate([qg, qg], axis=1), jnp.concatenate([kg, kg], axis=1)
    o, csave, w_b_out, w_ple1, w_gate1 = _attn_fwd(
        q_all, k_all, v_all, qg2, kg2, rider=_GatherRider([cast[k] for k in behind_attn]))
    w_b_out, w_gate1 = w_b_out.reshape(D, D), w_gate1.reshape(D, D)
    yb, x3, e1, gt1, dx4, sq_err = _layer_b_out_fwd(o, zb, x2, p1, target, w_b_out, w_ple1, w_gate1)

    de1, dgp1, dx3, d_o, dzb = _layer_b_out_bwd(dx4, e1, gt1, o, zb, w_gate1, w_b_out)
    partial = {
        "b_w_out": _wgrad(yb, dx3, "wgrad_b_w_out").reshape(N_DEV, ROWS, D),
        "ple_w1": _wgrad(p1, de1, "wgrad_ple_w1", n_split=8),
        "gate1": _wgrad(x3, dgp1, "wgrad_gate1").reshape(N_DEV, ROWS, D),
    }
    grad = {}
    dq, dk, dv, dqg, dkg, grad["b_w_out"], grad["ple_w1"], grad["gate1"] = _attn_bwd(
        q_all, k_all, v_all, qg2, kg2, d_o, csave,
        rider=_ReduceRider([partial[k] for k in ("b_w_out", "ple_w1", "gate1")]))
    partial["w_kv"] = _wgrad(hkv, [dk, dv], "wgrad_w_kv", n_split=8)
    partial["b_w_in"] = _wgrad(hb, [dq, dzb], "wgrad_b_w_in", n_split=8)
    dx2, d_bn, d_kvn, grad["w_kv"] = _layer_b_in_bwd(
        dq, dzb, dk, dv, x2, dx3, w_b_in, w_kv_f, bn, kvn, rider=_ReduceRider([partial["w_kv"]]))
    de0, dgp0, dx1, dm, duz, d_as, grad["b_w_in"] = _layer_a_out_bwd(
        dx2, e0, gt0, z, mcat, w_gate0, w_a_out, a_scale_f, w_group, rider=_ReduceRider([partial["b_w_in"]]))
    partial["gate0"] = _wgrad(x1, dgp0, "wgrad_gate0").reshape(N_DEV, ROWS, D)
    partial["a_w_out"] = _wgrad(y, dx1, "wgrad_a_w_out").reshape(N_DEV, ROWS, D)
    partial["a_w_in"], dw_group, partial["ple_w0"], grad["gate0"], grad["a_w_out"] = _wgrad_layer_a_in(
        h0, duz, pooled, dm, p0, de0, rider=_ReduceRider([partial["gate0"], partial["a_w_out"]]))
    partial["a_w_group"] = dw_group.reshape(N_GROUPS, N_DEV, GROUP_ROWS, GROUP_DIM).transpose(1, 0, 2, 3).reshape(
        N_DEV, N_GROUPS * GROUP_ROWS, GROUP_DIM)
    behind_a_in = ["a_w_in", "a_w_group", "ple_w0"]
    dx0, d_an, *done = _layer_a_in_bwd(duz, x0, dx1, w_a_in, a_norm_f,
                                      rider=_ReduceRider([partial[k] for k in behind_a_in]))
    grad.update(zip(behind_a_in, done))

    given = {
        "a_w_in": (a_w_in, m_a_w_in, v_a_w_in), "a_w_group": (a_w_group, m_a_w_group, v_a_w_group),
        "a_w_out": (a_w_out, m_a_w_out, v_a_w_out), "w_kv": (w_kv, m_w_kv, v_w_kv),
        "b_w_in": (b_w_in, m_b_w_in, v_b_w_in), "b_w_out": (b_w_out, m_b_w_out, v_b_w_out),
        "ple_w": (ple_w, m_ple_w, v_ple_w), "ple_gate_w": (ple_gate_w, m_ple_gate_w, v_ple_gate_w),
    }
    grad["ple_w"] = (grad["ple_w0"], grad["ple_w1"])
    grad["ple_gate_w"] = (grad["gate0"], grad["gate1"])
    updated = _adamw_all(
        "adamw_shards", list(big.values()), [grad[k] for k in big],
        [given[k][1].reshape(big[k].shape) for k in big], [given[k][2].reshape(big[k].shape) for k in big])
    res = {k: tuple(t.reshape(given[k][0].shape) for t in four) for k, four in zip(big, updated)}

    rows = jnp.concatenate([d_kvn, d_bn, d_an, d_as, jnp.zeros((2, D), F32), sq_err, jnp.zeros((1, D), F32)], axis=0)
    tot = _all_reduce_small(rows, (dqg, dkg))
    loss = tot[6, 0]
    small_grad = {
        "kv_norm": tot[0:1], "b_norm": tot[1:2],
        "a_norm": lax.dynamic_slice_in_dim(tot[2:3], me * ROWS, ROWS, axis=1),
        "a_scale": lax.dynamic_slice_in_dim(tot[3:4], me * ROWS, ROWS, axis=1),
        "b_q_norm": tot[4:5, :HEAD_DIM], "k_norm": tot[5:6, :HEAD_DIM],
    }
    small_given = {
        "a_norm": (a_norm, m_a_norm, v_a_norm), "a_scale": (a_scale, m_a_scale, v_a_scale),
        "kv_norm": (kv_norm, m_kv_norm, v_kv_norm), "k_norm": (k_norm, m_k_norm, v_k_norm),
        "b_norm": (b_norm, m_b_norm, v_b_norm), "b_q_norm": (b_q_norm, m_b_q_norm, v_b_q_norm),
    }
    rows_of = {k: [t.reshape(1, -1) for t in three] for k, three in small_given.items()}
    updated = _adamw_all(
        "adamw_gains", [rows_of[k][0] for k in small_given], [small_grad[k] for k in small_given],
        [rows_of[k][1] for k in small_given], [rows_of[k][2] for k in small_given])
    res.update({k: tuple(t.reshape(small_given[k][0].shape) for t in four) for k, four in zip(small_given, updated)})

    order = ["a_norm", "a_w_in", "a_w_group", "a_scale", "a_w_out", "kv_norm", "w_kv", "k_norm", "b_norm",
             "b_w_in", "b_q_norm", "b_w_out", "ple_w", "ple_gate_w"]
    outs = [res[k][kind] for kind in range(4) for k in order]
    return (loss, dx0.reshape(x.shape), *outs)
```

```python
import jax
import jax.numpy as jnp
from jax import lax
from jax.experimental import pallas as pl
from jax.experimental.pallas import tpu as pltpu

F32 = jnp.float32
BF16 = jnp.bfloat16
MESH = pl.DeviceIdType.MESH

N_DEV = 8
D = 1024
N_GROUPS = 4
GROUP_DIM = D // N_GROUPS
HALO = 16
HEADS = 16
HEAD_DIM = D // HEADS
SB_SCALE = HEAD_DIM ** -0.5
TILE = 256
LANES = 128
DEAD_LOG = -120.0
EPS = 1e-6
ADAM_LR = 0.001
ADAM_B1 = 0.9
ADAM_B2 = 0.999
ADAM_EPS = 1e-08
ADAM_WD = 0.01
ADAM_STEP = 10
TM = 256
COLS = 2 * D // N_DEV
ROWS = D // N_DEV
GROUP_ROWS = GROUP_DIM // N_DEV
VMEM_LIMIT = 56 * 1024 * 1024

HBM_SPEC = pl.BlockSpec(memory_space=pltpu.HBM)


def _dot(a, b):
    return jnp.dot(a, b, preferred_element_type=F32)


def _dot_nt(a, b):
    return lax.dot_general(a, b, (((1,), (1,)), ((), ())), preferred_element_type=F32)


def _dot_tn(a, b):
    return lax.dot_general(a, b, (((0,), (0,)), ((), ())), preferred_element_type=F32)


def _sigmoid(x):
    return jax.nn.sigmoid(x)


def _split_dot(x, mat):
    hi = x.astype(BF16)
    lo = (x - hi.astype(F32)).astype(BF16)
    return _dot(hi, mat) + _dot(lo, mat)


def _split_dot_many(xs, mat):
    rows = xs[0].shape[0]
    his = [x.astype(BF16) for x in xs]
    los = [(x - hi.astype(F32)).astype(BF16) for x, hi in zip(xs, his)]
    out = _dot(jnp.concatenate(his + los, axis=0), mat)
    n = len(xs)
    return [out[i * rows:(i + 1) * rows] + out[(n + i) * rows:(n + i + 1) * rows] for i in range(n)]


def _rms(x):
    return lax.rsqrt(jnp.mean(x * x, axis=-1, keepdims=True) + EPS)


def _hosted_call(body, name, n_steps, ins, in_specs, out_specs, out_shape, scratch, rider=None):
    ins, scratch = list(ins), list(scratch)
    if rider is None:
        wrapped, extra_in, extra_out, extra_scratch = body, [], [], []
    else:
        extra_in, extra_out, extra_scratch = rider.arrays, rider.out_shape(), rider.scratch()
        n_in, n_out, n_scr = len(ins), len(out_shape), len(scratch)
        k_in, k_out = len(extra_in), len(extra_out)

        def wrapped(*refs):
            own_in, r_in = refs[:n_in], refs[n_in:n_in + k_in]
            own_out = refs[n_in + k_in:n_in + k_in + n_out]
            r_out = refs[n_in + k_in + n_out:n_in + k_in + n_out + k_out]
            rest = refs[n_in + k_in + n_out + k_out:]
            phases = rider.bind(r_in, r_out, rest[n_scr:])
            step = pl.program_id(0)
            pl.when(step == 0)(phases[0])
            body(*own_in, *own_out, *rest[:n_scr])
            for share, phase in zip(rider.WHEN[1:], phases[1:]):
                at = min(n_steps - 1, max(1, round(share * (n_steps - 1))))
                pl.when(step == at)(phase)

    return pl.pallas_call(
        wrapped, name=name, grid=(n_steps,),
        in_specs=list(in_specs) + [HBM_SPEC] * len(extra_in),
        out_specs=list(out_specs) + [HBM_SPEC] * len(extra_out),
        out_shape=list(out_shape) + list(extra_out), scratch_shapes=scratch + list(extra_scratch),
        compiler_params=pltpu.CompilerParams(dimension_semantics=("arbitrary",), vmem_limit_bytes=VMEM_LIMIT),
    )(*ins, *extra_in)


def _rows_call(body, name, n_rows, row_ins, const_ins, row_outs, const_outs=(), scratch=(),
               reverse=False, tm=TM, rider=None):
    nb = n_rows // tm

    def row_map(i):
        return ((nb - 1 - i) if reverse else i, 0)

    def const_map(nd):
        return lambda i: (0,) * nd

    in_specs = [pl.BlockSpec((tm, a.shape[1]), row_map) for a in row_ins]
    in_specs += [pl.BlockSpec(a.shape, const_map(a.ndim)) for a in const_ins]
    out_specs = [pl.BlockSpec((tm, w), row_map) for (w, _) in row_outs]
    out_specs += [pl.BlockSpec(s, const_map(len(s))) for (s, _) in const_outs]
    out_shape = [jax.ShapeDtypeStruct((n_rows, w), dt) for (w, dt) in row_outs]
    out_shape += [jax.ShapeDtypeStruct(s, dt) for (s, dt) in const_outs]
    return _hosted_call(body, name, nb, list(row_ins) + list(const_ins), in_specs, out_specs, out_shape,
                        scratch, rider)


def _ple_fwd(p_ref, xin, wple_ref, wgate_ref, e_ref, gt_ref):
    pb = p_ref[...].astype(BF16)
    for j in range(N_DEV):
        e_ref[:, j * ROWS:(j + 1) * ROWS] = _dot(pb, wple_ref[j])
    gt = _sigmoid(_dot(xin.astype(BF16), wgate_ref[...]))
    gt_ref[...] = gt
    return xin + e_ref[...] * gt


def _layer_a_fwd(x0, p0, a_norm, a_scale, w_in, w_group, w_out, w_ple, w_gate, rider=None):
    s = x0.shape[0]
    tm = TM

    def body(x_ref, p_ref, an_ref, as_ref, win_ref, wg_ref, wout_ref, wple_ref, wgate_ref,
             h_ref, z_ref, pooled_ref, m_ref, y_ref, x1_ref, e_ref, gt_ref, x2_ref, uext):
        i = pl.program_id(0)

        @pl.when(i == 0)
        def _():
            uext[0:HALO, :] = jnp.zeros((HALO, D), F32)

        x = x_ref[...]
        h = (x * _rms(x) * an_ref[...]).astype(BF16)
        h_ref[...] = h
        for j in range(N_DEV):
            uz = _dot(h, win_ref[j])
            if j < 4:
                uext[HALO:HALO + tm, j * COLS:(j + 1) * COLS] = uz
            else:
                z_ref[:, (j - 4) * COLS:(j - 3) * COLS] = uz
        t = i * tm + lax.broadcasted_iota(jnp.int32, (tm, 1), 0)
        for g in range(N_GROUPS):
            w = 2 ** (g + 1)
            cols = slice(g * GROUP_DIM, (g + 1) * GROUP_DIM)
            ext = uext[:, cols]
            acc = ext
            k = 1
            while k < w:
                acc = acc + pltpu.roll(acc, k, 0)
                k *= 2
            cnt = jnp.minimum(t + 1, w).astype(F32)
            pooled = (acc[HALO:] / cnt - ext[HALO:]).astype(BF16)
            pooled_ref[:, cols] = pooled
            m_ref[:, cols] = _dot(pooled, wg_ref[g])
        uext[0:HALO, :] = uext[tm:tm + HALO, :]
        z = z_ref[...]
        y = (m_ref[...] * as_ref[...] * (z * _sigmoid(z))).astype(BF16)
        y_ref[...] = y
        x1 = x + _dot(y, wout_ref[...])
        x1_ref[...] = x1
        x2_ref[...] = _ple_fwd(p_ref, x1, wple_ref, wgate_ref, e_ref, gt_ref)

    row_outs = [(D, BF16), (D, F32), (D, BF16), (D, F32), (D, BF16), (D, F32), (D, F32), (D, F32), (D, F32)]
    return _rows_call(body, "layer_a_fwd", s, [x0, p0], [a_norm, a_scale, w_in, w_group, w_out, w_ple, w_gate],
                      row_outs, scratch=[pltpu.VMEM((tm + HALO, D), F32)], rider=rider)


def _layer_b_in_fwd(x2, kv_norm, b_norm, w_kv, w_bin):
    s = x2.shape[0]

    def body(x_ref, kvn_ref, bn_ref, wkv_ref, wbin_ref, hkv_ref, hb_ref, k_ref, v_ref, q_ref, zb_ref):
        x = x_ref[...]
        n = x * _rms(x)
        hkv = (n * kvn_ref[...]).astype(BF16)
        hb = (n * bn_ref[...]).astype(BF16)
        hkv_ref[...] = hkv
        hb_ref[...] = hb
        for j in range(N_DEV):
            kv = _dot(hkv, wkv_ref[j])
            qz = _dot(hb, wbin_ref[j])
            if j < 4:
                cols = slice(j * COLS, (j + 1) * COLS)
                k_ref[:, cols] = kv
                q_ref[:, cols] = qz
            else:
                cols = slice((j - 4) * COLS, (j - 3) * COLS)
                v_ref[:, cols] = kv.astype(BF16)
                zb_ref[:, cols] = qz

    row_outs = [(D, BF16), (D, BF16), (D, F32), (D, BF16), (D, F32), (D, F32)]
    return _rows_call(body, "layer_b_in_fwd", s, [x2], [kv_norm, b_norm, w_kv, w_bin], row_outs)


def _tri(after):
    r = lax.broadcasted_iota(jnp.int32, (TILE, TILE), 0)
    c = lax.broadcasted_iota(jnp.int32, (TILE, TILE), 1)
    return jnp.where((r > c) if after else (r < c), 1.0, 0.0).astype(BF16)


def _half_sums(v):
    r = lax.broadcasted_iota(jnp.int32, (LANES, LANES), 0) < HEAD_DIM
    c = lax.broadcasted_iota(jnp.int32, (LANES, LANES), 1) < HEAD_DIM
    same_head = jnp.where(r == c, 1.0, 0.0).astype(BF16)
    return _split_dot(v, same_head)


def _pair_norm(x):
    r = lax.rsqrt(_half_sums(x * x) * (1.0 / HEAD_DIM) + EPS)
    return x * r, r


def _tile_logits(qblk, kblk, diagonal):
    l = _dot_nt(qblk, kblk)
    sp = jnp.maximum(l, 0.0) + jnp.log(1.0 + jnp.exp(-jnp.abs(l)))
    ls = l - sp
    if not diagonal:
        return None, -sp, ls
    mask = lax.broadcasted_iota(jnp.int32, l.shape, 1) < lax.broadcasted_iota(jnp.int32, l.shape, 0)
    return mask, jnp.where(mask, -sp, 0.0), ls


def _attn_fwd(q_all, k_all, v_all, q_gain2, k_gain2, rider=None):
    s = q_all.shape[0]
    nt = s // TILE

    def body(q_ref, k_ref, v_ref, qg_ref, kg_ref, o_ref, c_ref, qs, ks, vs, tri, acc, right, cmat):
        tri[...] = _tri(True)
        lane = lax.broadcasted_iota(jnp.int32, (TILE, LANES), 1)
        qn, _ = _pair_norm(q_ref[...])
        kn, _ = _pair_norm(k_ref[...])
        qsc = (qn * qg_ref[...] * SB_SCALE).astype(BF16)
        ksc = (kn * kg_ref[...]).astype(BF16)
        for hh in range(2):
            sl = slice(hh * HEAD_DIM, (hh + 1) * HEAD_DIM)
            qs[hh] = qsc[:, sl]
            ks[hh] = ksc[:, sl]
            vs[hh] = v_ref[:, sl]

        def tile(qrows, kb, diagonal):
            rows = pl.ds(pl.multiple_of(kb * TILE, TILE), TILE)
            loaded = [(qs[hh, qrows, :], ks[hh, rows, :], vs[hh, rows, :], right[hh], cmat[hh], acc[hh])
                      for hh in range(2)]
            logits = [_tile_logits(q, k, diagonal) for q, k, _, _, _, _ in loaded]
            later = _split_dot_many([lk for _, lk, _ in logits], tri[...])
            results = []
            for (q, k, v, rt, cm, ac), (mask, lk, ls), lt in zip(loaded, logits, later):
                a = jnp.exp(ls + lt + rt)
                if diagonal:
                    a = jnp.where(mask, a, 0.0)
                results.append((ac + _dot(a.astype(BF16), v), jnp.where(lane == kb, rt[:, :LANES], cm),
                                rt + jnp.sum(lk, axis=1, keepdims=True)))
            for hh, (ac, cm, rt) in enumerate(results):
                acc[hh] = ac
                cmat[hh] = cm
                right[hh] = rt

        def diagonal_and_left(qrows, qb):
            here = pl.ds(pl.multiple_of(qb * TILE, TILE), TILE)
            left = pl.ds(pl.multiple_of((qb - 1) * TILE, TILE), TILE)
            q = [qs[hh, qrows, :] for hh in range(2)]
            on_diag = [_tile_logits(q[hh], ks[hh, here, :], True) for hh in range(2)]
            beside = [_tile_logits(q[hh], ks[hh, left, :], False) for hh in range(2)]
            later = _split_dot_many([lk for _, lk, _ in on_diag + beside], tri[...])
            for hh in range(2):
                mask, lk_d, ls_d = on_diag[hh]
                _, lk_l, ls_l = beside[hh]
                a_d = jnp.where(mask, jnp.exp(ls_d + later[hh]), 0.0)
                past_diag = jnp.sum(lk_d, axis=1, keepdims=True)
                a_l = jnp.exp(ls_l + later[2 + hh] + past_diag)
                acc[hh] = _dot(a_d.astype(BF16), vs[hh, here, :]) + _dot(a_l.astype(BF16), vs[hh, left, :])
                cmat[hh] = jnp.where(lane == qb - 1, past_diag, 0.0)
                right[hh] = jnp.broadcast_to(past_diag + jnp.sum(lk_l, axis=1, keepdims=True), (TILE, TILE))

        def q_step(qb, _):
            r0 = pl.multiple_of(qb * TILE, TILE)
            qrows = pl.ds(r0, TILE)

            @pl.when(qb == 0)
            def _():
                acc[...] = jnp.zeros((2, TILE, HEAD_DIM), F32)
                right[...] = jnp.zeros((2, TILE, TILE), F32)
                cmat[...] = jnp.zeros((2, TILE, LANES), F32)
                tile(qrows, qb, True)

            pl.when(qb > 0)(lambda: diagonal_and_left(qrows, qb))

            def live():
                return (jnp.max(right[:, :, :LANES]) > DEAD_LOG).astype(jnp.int32)

            def k_step(c):
                kb = c[0] - 1
                tile(qrows, kb, False)
                return kb, live()

            first, _ = lax.while_loop(lambda c: (c[0] > 0) & (c[1] > 0), k_step, (jnp.maximum(qb - 1, 0), live()))
            for hh in range(2):
                o_ref[qrows, hh * HEAD_DIM:(hh + 1) * HEAD_DIM] = acc[hh]
                c_ref[hh, qrows, :] = jnp.where(lane == LANES - 1, first.astype(F32), cmat[hh])
            return 0

        lax.fori_loop(0, nt, q_step, 0)

    pair = pl.BlockSpec((s, LANES), lambda h: (0, h))
    gain = pl.BlockSpec((1, LANES), lambda h: (0, 0))
    return _hosted_call(
        body, "attn_fwd", HEADS // 2, [q_all, k_all, v_all, q_gain2, k_gain2],
        [pair, pair, pair, gain, gain], [pair, pl.BlockSpec((2, s, LANES), lambda h: (h, 0, 0))],
        [jax.ShapeDtypeStruct((s, D), F32), jax.ShapeDtypeStruct((HEADS, s, LANES), F32)],
        [pltpu.VMEM((2, s, HEAD_DIM), BF16)] * 3
        + [pltpu.VMEM((TILE, TILE), BF16), pltpu.VMEM((2, TILE, HEAD_DIM), F32), pltpu.VMEM((2, TILE, TILE), F32),
           pltpu.VMEM((2, TILE, LANES), F32)], rider)


def _layer_b_out_fwd(o, zb, x2, p1, target, w_out, w_ple, w_gate):
    s = o.shape[0]

    def body(o_ref, zb_ref, x2_ref, p_ref, t_ref, wout_ref, wple_ref, wgate_ref,
             yb_ref, x3_ref, e_ref, gt_ref, dx4_ref, loss_ref):
        zb = zb_ref[...]
        yb = (o_ref[...] * (zb * _sigmoid(zb))).astype(BF16)
        yb_ref[...] = yb
        x3 = x2_ref[...] + _dot(yb, wout_ref[...])
        x3_ref[...] = x3
        x4 = _ple_fwd(p_ref, x3, wple_ref, wgate_ref, e_ref, gt_ref)
        d = x4 - t_ref[...]
        dx4_ref[...] = d * (1.0 / D)

        @pl.when(pl.program_id(0) == 0)
        def _():
            loss_ref[...] = jnp.zeros((1, D), F32)

        loss_ref[...] += jnp.sum(d * d, axis=0, keepdims=True)

    row_outs = [(D, BF16), (D, F32), (D, F32), (D, F32), (D, F32)]
    return _rows_call(body, "layer_b_out_fwd", s, [o, zb, x2, p1, target], [w_out, w_ple, w_gate], row_outs,
                      const_outs=[((1, D), F32)])


def _ple_bwd(dxo, e_ref, gt_ref, wgate_ref, de_ref, dgp_ref):
    e = e_ref[...]
    gt = gt_ref[...]
    de_ref[...] = (dxo * gt).astype(BF16)
    dgp = (dxo * e * gt * (1.0 - gt)).astype(BF16)
    dgp_ref[...] = dgp
    return dxo + _dot_nt(dgp, wgate_ref[...])


def _silu_grads(z):
    sg = _sigmoid(z)
    return z * sg, sg * (1.0 + z * (1.0 - sg))


def _layer_b_out_bwd(dx4, e1, gt1, o, zb, w_gate, w_out):
    s = dx4.shape[0]

    def body(dx4_ref, e_ref, gt_ref, o_ref, zb_ref, wgate_ref, wout_ref,
             de_ref, dgp_ref, dx3_ref, do_ref, dzb_ref):
        dx3 = _ple_bwd(dx4_ref[...], e_ref, gt_ref, wgate_ref, de_ref, dgp_ref)
        dx3_ref[...] = dx3
        dyb = _dot_nt(dx3.astype(BF16), wout_ref[...])
        silu, dsilu = _silu_grads(zb_ref[...])
        do_ref[...] = (dyb * silu).astype(BF16)
        dzb_ref[...] = (dyb * o_ref[...] * dsilu).astype(BF16)

    row_outs = [(D, BF16), (D, BF16), (D, F32), (D, BF16), (D, BF16)]
    return _rows_call(body, "layer_b_out_bwd", s, [dx4, e1, gt1, o, zb], [w_gate, w_out], row_outs)


def _attn_bwd(q_all, k_all, v_all, q_gain2, k_gain2, d_o, csave, rider=None):
    s = q_all.shape[0]
    nt = s // TILE

    def body(q_ref, k_ref, v_ref, qg_ref, kg_ref, do_ref, c_ref,
             dq_ref, dk_ref, dv_ref, dqg_ref, dkg_ref,
             qs, ks, vs, dos, qt, dot_t, tri_a, tri_b, dqa, dkt, dvt, dqb, left):
        tri_a[...] = _tri(True)
        tri_b[...] = _tri(False)
        lane = lax.broadcasted_iota(jnp.int32, (TILE, LANES), 1)
        qn, qr = _pair_norm(q_ref[...])
        kn, kr = _pair_norm(k_ref[...])
        qsc = qn * qg_ref[...] * SB_SCALE
        ksc = (kn * kg_ref[...]).astype(BF16)
        q_t = qsc.T.astype(BF16)
        do_t = do_ref[...].astype(F32).T.astype(BF16)
        for j in range(nt):
            qt[j] = q_t[:, j * TILE:(j + 1) * TILE]
            dot_t[j] = do_t[:, j * TILE:(j + 1) * TILE]
        qsc = qsc.astype(BF16)
        for hh in range(2):
            sl = slice(hh * HEAD_DIM, (hh + 1) * HEAD_DIM)
            qs[hh] = qsc[:, sl]
            ks[hh] = ksc[:, sl]
            vs[hh] = v_ref[:, sl]
            dos[hh] = do_ref[:, sl]

        def tile(qb, qrows, kb, diagonal):
            rows = pl.ds(pl.multiple_of(kb * TILE, TILE), TILE)
            heads = range(2)
            kblk = [ks[hh, rows, :] for hh in heads]
            logits = [_tile_logits(qs[hh, qrows, :], kblk[hh], diagonal) for hh in heads]
            later = _split_dot_many([lk for _, lk, _ in logits], tri_a[...])
            a, g = [], []
            for hh in heads:
                mask, _, ls = logits[hh]
                right = jnp.sum(jnp.where(lane == kb, c_ref[hh, qrows, :], 0.0), axis=1, keepdims=True)
                a_h = jnp.exp(ls + later[hh] + right)
                a.append(jnp.where(mask, a_h, 0.0) if diagonal else a_h)
                g.append(a[hh] * _dot_nt(dos[hh, qrows, :], vs[hh, rows, :]))
            before = _split_dot_many(g, tri_b[...])
            for hh in heads:
                sl = slice(hh * HEAD_DIM, (hh + 1) * HEAD_DIM)
                mask, _, ls = logits[hh]
                beta = jnp.exp(ls)
                lf = left[hh]
                dl = g[hh] * (1.0 - beta) - (before[hh] + lf) * beta
                if diagonal:
                    dl = jnp.where(mask, dl, 0.0)
                dl = dl.astype(BF16)
                left[hh] = lf + jnp.sum(g[hh], axis=1, keepdims=True)
                dqb[hh] += _dot(dl, kblk[hh])
                dk_t, dv_t = _dot(qt[qb, sl, :], dl), _dot(dot_t[qb, sl, :], a[hh].astype(BF16))
                if diagonal:
                    dkt[kb, sl, :] = dk_t
                    dvt[kb, sl, :] = dv_t
                else:
                    dkt[kb, sl, :] += dk_t
                    dvt[kb, sl, :] += dv_t

        def left_and_diagonal(qb, qrows):
            here = pl.ds(pl.multiple_of(qb * TILE, TILE), TILE)
            beside = pl.ds(pl.multiple_of((qb - 1) * TILE, TILE), TILE)
            heads = range(2)
            q = [qs[hh, qrows, :] for hh in heads]
            do = [dos[hh, qrows, :] for hh in heads]
            k_d, k_l = [ks[hh, here, :] for hh in heads], [ks[hh, beside, :] for hh in heads]
            on_diag = [_tile_logits(q[hh], k_d[hh], True) for hh in heads]
            on_left = [_tile_logits(q[hh], k_l[hh], False) for hh in heads]
            later = _split_dot_many([lk for _, lk, _ in on_diag + on_left], tri_a[...])
            a_d, a_l, g_d, g_l = [], [], [], []
            for hh in heads:
                mask, lk_d, ls_d = on_diag[hh]
                a_d.append(jnp.where(mask, jnp.exp(ls_d + later[hh]), 0.0))
                a_l.append(jnp.exp(on_left[hh][2] + later[2 + hh] + jnp.sum(lk_d, axis=1, keepdims=True)))
                g_d.append(a_d[hh] * _dot_nt(do[hh], vs[hh, here, :]))
                g_l.append(a_l[hh] * _dot_nt(do[hh], vs[hh, beside, :]))
            before = _split_dot_many(g_l + g_d, tri_b[...])
            for hh in heads:
                sl = slice(hh * HEAD_DIM, (hh + 1) * HEAD_DIM)
                beta_l, beta_d = jnp.exp(on_left[hh][2]), jnp.exp(on_diag[hh][2])
                dl_l = (g_l[hh] * (1.0 - beta_l) - before[hh] * beta_l).astype(BF16)
                carried = jnp.sum(g_l[hh], axis=1, keepdims=True)
                dl_d = g_d[hh] * (1.0 - beta_d) - (before[2 + hh] + carried) * beta_d
                dl_d = jnp.where(on_diag[hh][0], dl_d, 0.0).astype(BF16)
                dqa[qrows, sl] = (_dot(dl_l, k_l[hh]) + _dot(dl_d, k_d[hh])) * SB_SCALE
                dkt[qb - 1, sl, :] += _dot(qt[qb, sl, :], dl_l)
                dkt[qb, sl, :] = _dot(qt[qb, sl, :], dl_d)
                dvt[qb - 1, sl, :] += _dot(dot_t[qb, sl, :], a_l[hh].astype(BF16))
                dvt[qb, sl, :] = _dot(dot_t[qb, sl, :], a_d[hh].astype(BF16))

        def q_step(qb, _):
            qrows = pl.ds(pl.multiple_of(qb * TILE, TILE), TILE)
            first = jnp.max(jnp.where(lane == LANES - 1, c_ref[0, qrows, :], 0.0)).astype(jnp.int32)
            usual = (qb > 0) & (first == qb - 1)

            @pl.when(usual)
            def _():
                left_and_diagonal(qb, qrows)

            @pl.when(jnp.logical_not(usual))
            def _():
                dqb[...] = jnp.zeros((2, TILE, HEAD_DIM), F32)
                left[...] = jnp.zeros((2, TILE, TILE), F32)

                def k_step(kb, _):
                    tile(qb, qrows, kb, False)
                    return 0

                lax.fori_loop(first, qb, k_step, 0)
                tile(qb, qrows, qb, True)
                for hh in range(2):
                    dqa[qrows, hh * HEAD_DIM:(hh + 1) * HEAD_DIM] = dqb[hh] * SB_SCALE

            return 0

        lax.fori_loop(0, nt, q_step, 0)

        def norm_bwd(dy, xn, r, g_ref, dx_ref, dg_ref):
            dg_ref[...] = jnp.sum(dy * xn, axis=0, keepdims=True)
            dxn = dy * g_ref[...]
            dx_ref[...] = (r * (dxn - xn * (_half_sums(dxn * xn) * (1.0 / HEAD_DIM)))).astype(BF16)

        norm_bwd(dqa[...], qn, qr, qg_ref, dq_ref, dqg_ref)
        for j in range(nt):
            dqa[j * TILE:(j + 1) * TILE, :] = dkt[j].T
            dv_ref[j * TILE:(j + 1) * TILE, :] = dvt[j].T.astype(BF16)
        norm_bwd(dqa[...], kn, kr, kg_ref, dk_ref, dkg_ref)

    pair = pl.BlockSpec((s, LANES), lambda h: (0, h))
    gain = pl.BlockSpec((1, LANES), lambda h: (0, 0))
    dgain = pl.BlockSpec((None, 1, LANES), lambda h: (h, 0, 0))
    return _hosted_call(
        body, "attn_bwd", HEADS // 2, [q_all, k_all, v_all, q_gain2, k_gain2, d_o, csave],
        [pair, pair, pair, gain, gain, pair, pl.BlockSpec((2, s, LANES), lambda h: (h, 0, 0))],
        [pair, pair, pair, dgain, dgain],
        [jax.ShapeDtypeStruct((s, D), BF16)] * 3 + [jax.ShapeDtypeStruct((HEADS // 2, 1, LANES), F32)] * 2,
        [pltpu.VMEM((2, s, HEAD_DIM), BF16)] * 4
        + [pltpu.VMEM((nt, LANES, TILE), BF16)] * 2 + [pltpu.VMEM((TILE, TILE), BF16)] * 2
        + [pltpu.VMEM((s, LANES), F32)] + [pltpu.VMEM((nt, LANES, TILE), F32)] * 2
        + [pltpu.VMEM((2, TILE, HEAD_DIM), F32), pltpu.VMEM((2, TILE, TILE), F32)], rider)


def _norm_bwd_rows(dh, x, gain, dgain_ref):
    r = _rms(x)
    n = x * r
    dgain_ref[...] += jnp.sum(dh * n, axis=0, keepdims=True)
    dn = dh * gain
    return r * (dn - n * jnp.mean(dn * n, axis=-1, keepdims=True))


def _layer_b_in_bwd(dq, dzb, dk, dv, x2, dx3, w_bin, w_kv, b_norm, kv_norm, rider=None):
    s = x2.shape[0]

    def body(dq_ref, dzb_ref, dk_ref, dv_ref, x_ref, dx3_ref, wbin_ref, wkv_ref, bn_ref, kvn_ref,
             dx2_ref, dbn_ref, dkvn_ref):
        @pl.when(pl.program_id(0) == 0)
        def _():
            dbn_ref[...] = jnp.zeros((1, D), F32)
            dkvn_ref[...] = jnp.zeros((1, D), F32)

        dhb = jnp.zeros((TM, D), F32)
        dhkv = jnp.zeros((TM, D), F32)
        for j in range(N_DEV):
            cols = slice((j % 4) * COLS, (j % 4 + 1) * COLS)
            dhb = dhb + _dot_nt((dq_ref if j < 4 else dzb_ref)[:, cols], wbin_ref[j])
            dhkv = dhkv + _dot_nt((dk_ref if j < 4 else dv_ref)[:, cols], wkv_ref[j])
        x = x_ref[...]
        dx2 = dx3_ref[...] + _norm_bwd_rows(dhb, x, bn_ref[...], dbn_ref)
        dx2_ref[...] = dx2 + _norm_bwd_rows(dhkv, x, kvn_ref[...], dkvn_ref)

    return _rows_call(body, "layer_b_in_bwd", s, [dq, dzb, dk, dv, x2, dx3], [w_bin, w_kv, b_norm, kv_norm],
                      [(D, F32)], const_outs=[((1, D), F32), ((1, D), F32)], rider=rider)


def _layer_a_out_bwd(dx2, e0, gt0, z, m, w_gate, w_out, a_scale, w_group, rider=None):
    s = dx2.shape[0]
    tm = TM
    nb = s // tm

    def body(dx2_ref, e_ref, gt_ref, z_ref, m_ref, wgate_ref, wout_ref, as_ref, wg_ref,
             de_ref, dgp_ref, dx1_ref, dm_ref, duz_ref, das_ref, ext):
        i = pl.program_id(0)

        @pl.when(i == 0)
        def _():
            das_ref[...] = jnp.zeros((1, D), F32)
            ext[tm:tm + HALO, :] = jnp.zeros((HALO, D), F32)

        dx1 = _ple_bwd(dx2_ref[...], e_ref, gt_ref, wgate_ref, de_ref, dgp_ref)
        dx1_ref[...] = dx1
        dy = _dot_nt(dx1.astype(BF16), wout_ref[...])
        silu, dsilu = _silu_grads(z_ref[...])
        m = m_ref[...]
        dmixed = dy * silu
        duz_ref[:, D:] = (dy * (m * as_ref[...]) * dsilu).astype(BF16)
        das_ref[...] += jnp.sum(dmixed * m, axis=0, keepdims=True)
        dm_ref[...] = (dmixed * as_ref[...]).astype(BF16)
        t = (nb - 1 - i) * tm + lax.broadcasted_iota(jnp.int32, (tm, 1), 0)
        n_ext = tm + HALO
        for g in range(N_GROUPS):
            w = 2 ** (g + 1)
            cols = slice(g * GROUP_DIM, (g + 1) * GROUP_DIM)
            dpool = _dot_nt(dm_ref[:, cols], wg_ref[g])
            ext[0:tm, cols] = dpool / jnp.minimum(t + 1, w).astype(F32)
            acc = ext[:, cols]
            k = 1
            while k < w:
                acc = acc + pltpu.roll(acc, n_ext - k, 0)
                k *= 2
            duz_ref[:, cols] = (acc[:tm] - dpool).astype(BF16)
        ext[tm:tm + HALO, :] = ext[0:HALO, :]

    row_outs = [(D, BF16), (D, BF16), (D, F32), (D, BF16), (2 * D, BF16)]
    return _rows_call(body, "layer_a_out_bwd", s, [dx2, e0, gt0, z, m], [w_gate, w_out, a_scale, w_group],
                      row_outs, const_outs=[((1, D), F32)], scratch=[pltpu.VMEM((tm + HALO, D), F32)],
                      reverse=True, rider=rider)


def _layer_a_in_bwd(duz, x0, dx1, w_in, a_norm, rider=None):
    s = x0.shape[0]

    def body(duz_ref, x_ref, dx1_ref, win_ref, an_ref, dx0_ref, dan_ref):
        @pl.when(pl.program_id(0) == 0)
        def _():
            dan_ref[...] = jnp.zeros((1, D), F32)

        dh = jnp.zeros((TM, D), F32)
        for j in range(N_DEV):
            dh = dh + _dot_nt(duz_ref[:, j * COLS:(j + 1) * COLS], win_ref[j])
        dx0_ref[...] = dx1_ref[...] + _norm_bwd_rows(dh, x_ref[...], an_ref[...], dan_ref)

    return _rows_call(body, "layer_a_in_bwd", s, [duz, x0, dx1], [w_in, a_norm], [(D, F32)],
                      const_outs=[((1, D), F32)], rider=rider)


def _wgrad(a, b, name, n_split=1, rider=None):
    bs = list(b) if isinstance(b, (list, tuple)) else [b]
    s, k = a.shape
    n = sum(part.shape[1] for part in bs)
    tk = TM
    nb = n // n_split

    def body(a_ref, *refs):
        o_ref = refs[-1]
        lhs = a_ref[...].astype(BF16)
        done = 0
        for b_ref in refs[:-1]:
            res = _dot_tn(lhs, b_ref[...].astype(BF16))
            if n_split == 1:
                o_ref[...] = res.astype(BF16)
            else:
                for j in range(res.shape[1] // nb):
                    o_ref[done + j] = res[:, j * nb:(j + 1) * nb].astype(BF16)
                done += res.shape[1] // nb

    if n_split == 1:
        b_specs = [pl.BlockSpec((s, n), lambda i: (0, 0))]
        out_spec = pl.BlockSpec((tk, n), lambda i: (i, 0))
        out_shape = jax.ShapeDtypeStruct((k, n), BF16)
    else:
        b_specs = [pl.BlockSpec(part.shape, lambda i: (0, 0)) for part in bs]
        out_spec = pl.BlockSpec((n_split, tk, nb), lambda i: (0, i, 0))
        out_shape = jax.ShapeDtypeStruct((n_split, k, nb), BF16)
    res = _hosted_call(body, name, k // tk, [a] + bs, [pl.BlockSpec((s, tk), lambda i: (0, i))] + b_specs,
                       [out_spec], [out_shape], [], rider)
    return res[0] if rider is None else res


def _wgrad_layer_a_in(h0, duz, pooled, dm, p0, de0, rider):
    s = h0.shape[0]
    n_ple = p0.shape[1]

    def body(h_ref, duz_ref, pooled_ref, dm_ref, p_ref, de_ref, in_ref, group_ref, ple_ref):
        res = _dot_tn(h_ref[...], duz_ref[...])
        for j in range(N_DEV):
            in_ref[j] = res[:, j * COLS:(j + 1) * COLS].astype(BF16)
        group_ref[...] = _dot_tn(pooled_ref[...], dm_ref[...]).astype(BF16)

        @pl.when(pl.program_id(0) == 0)
        def _():
            ple = _dot_tn(p_ref[...].astype(BF16), de_ref[...])
            for j in range(N_DEV):
                ple_ref[j] = ple[:, j * ROWS:(j + 1) * ROWS].astype(BF16)

    block = pl.BlockSpec((s, TM), lambda i: (0, i))
    return _hosted_call(
        body, "wgrad_layer_a_in", D // TM, [h0, duz, pooled, dm, p0, de0],
        [block, pl.BlockSpec(duz.shape, lambda i: (0, 0)), block, block,
         pl.BlockSpec(p0.shape, lambda i: (0, 0)), pl.BlockSpec(de0.shape, lambda i: (0, 0))],
        [pl.BlockSpec((N_DEV, TM, COLS), lambda i: (0, i, 0)), pl.BlockSpec((None, TM, GROUP_DIM), lambda i: (i, 0, 0)),
         pl.BlockSpec((N_DEV, n_ple, ROWS), lambda i: (0, 0, 0))],
        [jax.ShapeDtypeStruct((N_DEV, D, COLS), BF16), jax.ShapeDtypeStruct((N_GROUPS, GROUP_DIM, GROUP_DIM), BF16),
         jax.ShapeDtypeStruct((N_DEV, n_ple, ROWS), BF16)], [], rider)


def _cast_shards(shards):
    n = len(shards)
    layers = [a.shape[0] if a.ndim == 3 else 0 for a in shards]

    def body(*refs):
        outs = iter(refs[n:])
        for a in range(n):
            if layers[a]:
                for t in range(layers[a]):
                    next(outs)[...] = refs[a][t].astype(BF16)
            else:
                next(outs)[...] = refs[a][...].astype(BF16)

    out_shape = []
    for a, k in zip(shards, layers):
        out_shape += [jax.ShapeDtypeStruct(a.shape[-2:], BF16)] * max(k, 1)
    vmem = pl.BlockSpec(memory_space=pltpu.VMEM)
    return pl.pallas_call(
        body, name="cast_shards", in_specs=[vmem] * n, out_specs=[vmem] * len(out_shape), out_shape=out_shape,
        compiler_params=pltpu.CompilerParams(vmem_limit_bytes=VMEM_LIMIT),
    )(*shards)


def _adamw(w, g, m, v):
    m = ADAM_B1 * m + (1.0 - ADAM_B1) * g
    v = ADAM_B2 * v + (1.0 - ADAM_B2) * jnp.square(g)
    m_hat = m / (1.0 - ADAM_B1 ** ADAM_STEP)
    v_hat = v / (1.0 - ADAM_B2 ** ADAM_STEP)
    delta = -ADAM_LR * (m_hat / (jnp.sqrt(v_hat) + ADAM_EPS) + ADAM_WD * w)
    return delta, m, v


def _place():
    return lax.axis_index("x"), lax.axis_index("y"), lax.axis_index("c")


def _all_gather(shards):
    return _alone("all_gather_weights", _GatherRider(shards))


def _alone(name, rider):
    n_in, n_out = len(rider.arrays), len(rider.out_shape())

    def body(*refs):
        for phase in rider.bind(refs[:n_in], refs[n_in:n_in + n_out], refs[n_in + n_out:]):
            phase()

    return pl.pallas_call(
        body, name=name, in_specs=[HBM_SPEC] * n_in, out_specs=[HBM_SPEC] * n_out,
        out_shape=rider.out_shape(), scratch_shapes=rider.scratch(),
        compiler_params=pltpu.CompilerParams(vmem_limit_bytes=VMEM_LIMIT),
    )(*rider.arrays)


class _GatherRider:
    def __init__(self, shards):
        self.arrays = list(shards)
        n = len(self.arrays)
        self.WHEN = (0.0, *[0.7 * (a + 1) / n for a in range(n)], 1.0)

    def out_shape(self):
        return _Gather.out_shape(self.arrays)

    def scratch(self):
        return _Gather.semaphores(len(self.arrays))

    def bind(self, ins, outs, scratch):
        moving = _Gather(ins, outs, *scratch)
        forwards = [(lambda a=a: moving.forward(a)) for a in range(len(ins))]
        return (moving.start, *forwards, moving.finish)


class _ReduceRider:
    WHEN = (0.0, 0.15, 0.5, 1.0)

    def __init__(self, partials):
        self.arrays = list(partials)

    def out_shape(self):
        return [jax.ShapeDtypeStruct(a.shape[1:], F32) for a in self.arrays]

    def scratch(self):
        n = len(self.arrays)
        dma = pltpu.SemaphoreType.DMA

        def blocks(k):
            return [pltpu.VMEM((k,) + a.shape[1:], BF16) for a in self.arrays]

        halves = [pltpu.VMEM((2, a.shape[1] // 2) + a.shape[2:], BF16) for a in self.arrays]
        return (blocks(4) + blocks(4) + halves + blocks(2) + [pltpu.VMEM(a.shape[1:], F32) for a in self.arrays]
                + [dma((4 * n,)), dma((4 * n,)), dma((4 * n,)), dma((2 * n,)), dma((2 * n,)),
                   dma((2 * n,)), dma((2 * n,)), dma((n,))])

    def bind(self, ins, outs, scratch):
        n = len(ins)
        mine, landed, halves, arrived, total = (scratch[i * n:(i + 1) * n] for i in range(5))
        send1, recv1, local1, send_h, recv_h, send2, recv2, out_sems = scratch[5 * n:]
        x, y, c = _place()
        plane = 2 * x + y
        via = [(x, 1 - y, c), (1 - x, y, c)]
        nbr = [(1 - x, y, c), (x, 1 - y, c)]
        nbr_block = [2 * (1 - x) + y, 2 * x + (1 - y)]
        diag_block = 2 * (1 - x) + (1 - y)

        def to_sibling(a, k):
            return pltpu.make_async_remote_copy(
                src_ref=ins[a].at[2 * k + (1 - c)], dst_ref=landed[a].at[k],
                send_sem=send1.at[4 * a + k], recv_sem=recv1.at[4 * a + k],
                device_id=(x, y, 1 - c), device_id_type=MESH)

        def own_block(a, k):
            return pltpu.make_async_copy(ins[a].at[2 * k + c], mine[a].at[k], local1.at[4 * a + k])

        def half_of(a, ref, h):
            rows = self.arrays[a].shape[1] // 2
            return ref.at[pl.ds(h * rows, rows)]

        def half_out(a, h):
            return pltpu.make_async_remote_copy(
                src_ref=half_of(a, mine[a].at[diag_block], h), dst_ref=halves[a].at[h],
                send_sem=send_h.at[2 * a + h], recv_sem=recv_h.at[2 * a + h],
                device_id=via[h], device_id_type=MESH)

        def to_owner(a, h):
            return pltpu.make_async_remote_copy(
                src_ref=mine[a].at[nbr_block[h]], dst_ref=arrived[a].at[h],
                send_sem=send2.at[2 * a + h], recv_sem=recv2.at[2 * a + h],
                device_id=nbr[h], device_id_type=MESH)

        def result(a):
            return pltpu.make_async_copy(total[a], outs[a], out_sems.at[a])

        def exchange_cores():
            for a in range(n):
                for k in range(4):
                    to_sibling(a, k).start()
                    own_block(a, k).start()

        def pair_sums():
            for a in range(n):
                for k in range(4):
                    own_block(a, k).wait()
                    to_sibling(a, k).wait_recv()
                total[a][...] = mine[a][plane].astype(F32) + landed[a][plane].astype(F32)
                for k in range(4):
                    mine[a][k] = (mine[a][k].astype(F32) + landed[a][k].astype(F32)).astype(BF16)
                for h in range(2):
                    half_out(a, h).start()

        def fold_and_send():
            for a in range(n):
                rows = self.arrays[a].shape[1] // 2
                for h in range(2):
                    half_out(a, h).wait_recv()
                    part = mine[a].at[nbr_block[h]]
                    span = slice(h * rows, (h + 1) * rows)
                    part[span] = (part[span].astype(F32) + halves[a][h].astype(F32)).astype(BF16)
                    to_owner(a, h).start()

        def finish():
            for a in range(n):
                for h in range(2):
                    to_owner(a, h).wait_recv()
                    total[a][...] += arrived[a][h].astype(F32)
                result(a).start()
            for a in range(n):
                for k in range(4):
                    to_sibling(a, k).wait_send()
                for h in range(2):
                    half_out(a, h).wait_send()
                    to_owner(a, h).wait_send()
                result(a).wait()

        return exchange_cores, pair_sums, fold_and_send, finish


class _Gather:
    COPIES = 9

    def __init__(self, ins, outs, send_sems, recv_sems, local_sems):
        self.ins, self.outs = ins, outs
        self.send_sems, self.recv_sems, self.local_sems = send_sems, recv_sems, local_sems
        self.x, self.y, self.c = _place()

    @staticmethod
    def out_shape(shards):
        return [jax.ShapeDtypeStruct((N_DEV,) + a.shape, a.dtype) for a in shards]

    @staticmethod
    def semaphores(n):
        dma = pltpu.SemaphoreType.DMA
        return [dma((_Gather.COPIES * n,)), dma((_Gather.COPIES * n,)), dma((n,))]

    def _copy(self, a, k, block, to, own=False, half=None):
        px, py, pc = block
        slot = self.outs[a].at[4 * px + 2 * py + pc]
        if half is not None:
            rows = slot.shape[0] // 2
            slot = slot.at[pl.ds(half * rows, rows)]
        return pltpu.make_async_remote_copy(
            src_ref=self.ins[a] if own else slot, dst_ref=slot,
            send_sem=self.send_sems.at[self.COPIES * a + k], recv_sem=self.recv_sems.at[self.COPIES * a + k],
            device_id=to, device_id_type=MESH)

    def _local(self, a):
        return pltpu.make_async_copy(self.ins[a], self.outs[a].at[4 * self.x + 2 * self.y + self.c],
                                     self.local_sems.at[a])

    def _plan(self, a, c):
        x, y = self.x, self.y
        me, sibling = (x, y, c), (x, y, 1 - c)
        xn, yn, dg = (1 - x, y, c), (x, 1 - y, c), (1 - x, 1 - y, c)
        return [
            self._copy(a, 0, me, sibling, own=True), self._copy(a, 1, me, xn, own=True),
            self._copy(a, 2, me, yn, own=True),
            self._copy(a, 3, xn, yn, half=0), self._copy(a, 4, yn, xn, half=1),
            self._copy(a, 5, xn, sibling), self._copy(a, 6, yn, sibling),
            self._copy(a, 7, dg, sibling, half=0), self._copy(a, 8, dg, sibling, half=1),
        ]

    def _arrivals(self, a):
        x, y, c = self.x, self.y, self.c
        me = (x, y, c)
        xn, yn, dg = (1 - x, y, c), (x, 1 - y, c), (1 - x, 1 - y, c)
        other = 1 - c
        return [
            self._copy(a, 0, (x, y, other), me), self._copy(a, 1, xn, me), self._copy(a, 2, yn, me),
            self._copy(a, 3, dg, me, half=0), self._copy(a, 4, dg, me, half=1),
            self._copy(a, 5, (1 - x, y, other), me), self._copy(a, 6, (x, 1 - y, other), me),
            self._copy(a, 7, (1 - x, 1 - y, other), me, half=0), self._copy(a, 8, (1 - x, 1 - y, other), me, half=1),
        ]

    def start(self):
        for a in range(len(self.ins)):
            self._local(a).start()
            for cp in self._plan(a, self.c)[:3]:
                cp.start()

    def forward(self, a):
        sends, lands = self._plan(a, self.c), self._arrivals(a)
        lands[1].wait_recv()
        sends[3].start()
        sends[5].start()
        lands[2].wait_recv()
        sends[4].start()
        sends[6].start()

    def finish(self):
        n = len(self.ins)
        for a in range(n):
            sends, lands = self._plan(a, self.c), self._arrivals(a)
            lands[3].wait_recv()
            sends[7].start()
            lands[4].wait_recv()
            sends[8].start()
        for a in range(n):
            lands = self._arrivals(a)
            for k in (0, 5, 6, 7, 8):
                lands[k].wait_recv()
        for a in range(n):
            for cp in self._plan(a, self.c):
                cp.wait_send()
            self._local(a).wait()


def _adamw_all(name, ws, gs, ms, vs):
    n = len(ws)
    per_layer = [isinstance(g, tuple) for g in gs]
    flat_g = [part for g in gs for part in (g if isinstance(g, tuple) else (g,))]

    def body(*refs):
        w, refs = refs[:n], refs[n:]
        g, refs = refs[:len(flat_g)], refs[len(flat_g):]
        m, v, outs = refs[:n], refs[n:2 * n], refs[2 * n:]
        stacked = iter(outs[3 * n:])
        parts = iter(g)
        for a in range(n):
            if per_layer[a]:
                whole = next(stacked)
                for t in range(len(gs[a])):
                    grad = next(parts)[...]
                    whole[t] = grad
                    outs[a][t], outs[n + a][t], outs[2 * n + a][t] = _adamw(w[a][t], grad, m[a][t], v[a][t])
            else:
                outs[a][...], outs[n + a][...], outs[2 * n + a][...] = _adamw(
                    w[a][...], next(parts)[...], m[a][...], v[a][...])

    shapes = [jax.ShapeDtypeStruct(a.shape, F32) for a in ws]
    vmem = pl.BlockSpec(memory_space=pltpu.VMEM)
    n_out = 3 * n + sum(per_layer)
    res = pl.pallas_call(
        body, name=name, in_specs=[vmem] * (3 * n + len(flat_g)), out_specs=[vmem] * n_out,
        out_shape=shapes * 3 + [s for s, p in zip(shapes, per_layer) if p],
        compiler_params=pltpu.CompilerParams(vmem_limit_bytes=VMEM_LIMIT),
    )(*ws, *flat_g, *ms, *vs)
    stacked = iter(res[3 * n:])
    return [(next(stacked) if per_layer[a] else gs[a], res[a], res[n + a], res[2 * n + a]) for a in range(n)]


def _all_reduce_small(rows, gain_parts):
    def body(rows_ref, dqg_ref, dkg_ref, out_ref, buf, send_sems, recv_sems):
        x, y, c = _place()
        me = 4 * x + 2 * y + c
        buf[0] = rows_ref[...]
        for row, part in ((4, dqg_ref), (5, dkg_ref)):
            both = jnp.sum(part[...].reshape(HEADS // 2, LANES), axis=0, keepdims=True)
            buf[0, row:row + 1, 0:HEAD_DIM] = both[:, :HEAD_DIM] + both[:, HEAD_DIM:]
        copies = []
        for r in range(1, N_DEV):
            bx, by, bc = (r >> 2) & 1, (r >> 1) & 1, r & 1
            to = (x ^ bx, y ^ by, c ^ bc)
            copies.append(pltpu.make_async_remote_copy(
                src_ref=buf.at[0], dst_ref=buf.at[r], send_sem=send_sems.at[r - 1], recv_sem=recv_sems.at[r - 1],
                device_id=to, device_id_type=MESH))
        for cp in copies:
            cp.start()
        for cp in copies:
            cp.wait_recv()
        for cp in copies:
            cp.wait_send()
        tot = buf[me]
        for j in range(1, N_DEV):
            tot = tot + buf[j ^ me]
        out_ref[...] = tot
        loss = (0.5 / D) * jnp.sum(tot[6:7, :], axis=1, keepdims=True)
        out_ref[6:7, :] = jnp.broadcast_to(loss, (1, D))

    vmem = pl.BlockSpec(memory_space=pltpu.VMEM)
    return pl.pallas_call(
        body, name="all_reduce_small", in_specs=[vmem] * 3, out_specs=vmem,
        out_shape=jax.ShapeDtypeStruct((8, D), F32),
        scratch_shapes=[pltpu.VMEM((N_DEV, 8, D), F32), pltpu.SemaphoreType.DMA((N_DEV - 1,)),
                        pltpu.SemaphoreType.DMA((N_DEV - 1,))],
    )(rows, *gain_parts)


def kernel(x, p, a_norm, a_w_in, a_w_group, a_scale, a_w_out, kv_norm, w_kv, k_norm, b_norm, b_w_in, b_q_norm, b_w_out, ple_w, ple_gate_w, loss_target, m_a_norm, m_a_w_in, m_a_w_group, m_a_scale, m_a_w_out, m_kv_norm, m_w_kv, m_k_norm, m_b_norm, m_b_w_in, m_b_q_norm, m_b_w_out, m_ple_w, m_ple_gate_w, v_a_norm, v_a_w_in, v_a_w_group, v_a_scale, v_a_w_out, v_kv_norm, v_w_kv, v_k_norm, v_b_norm, v_b_w_in, v_b_q_norm, v_b_w_out, v_ple_w, v_ple_gate_w):
    xi, yi, ci = _place()
    me = 4 * xi + 2 * yi + ci

    big = {
        "a_w_in": a_w_in.reshape(D, COLS), "a_w_group": a_w_group.reshape(N_GROUPS * GROUP_ROWS, GROUP_DIM),
        "a_w_out": a_w_out.reshape(ROWS, D), "w_kv": w_kv, "b_w_in": b_w_in.reshape(D, COLS),
        "b_w_out": b_w_out.reshape(ROWS, D), "ple_w": ple_w, "ple_gate_w": ple_gate_w,
    }
    names = ["a_w_in", "a_w_group", "a_w_out", "w_kv", "b_w_in", "b_w_out", "ple_w0", "ple_w1", "gate0", "gate1"]
    cast = dict(zip(names, _cast_shards(list(big.values()))))
    small = jnp.concatenate([a_norm, a_scale, jnp.zeros((14, ROWS), F32)], axis=0)
    first = ["a_w_in", "a_w_group", "a_w_out", "ple_w0", "gate0"]
    behind_a = ["w_kv", "b_w_in"]
    behind_attn = ["b_w_out", "ple_w1", "gate1"]
    gathered = _all_gather([cast[k] for k in first] + [small])
    full = dict(zip(first, gathered[:-1]))
    small_all = gathered[-1]
    a_norm_f = small_all[:, 0, :].reshape(1, D)
    a_scale_f = small_all[:, 1, :].reshape(1, D)
    w_a_in = full["a_w_in"]
    w_a_out = full["a_w_out"].reshape(D, D)
    w_gate0 = full["gate0"].reshape(D, D)
    w_ple0 = full["ple_w0"]
    w_group = full["a_w_group"].reshape(N_DEV, N_GROUPS, GROUP_ROWS, GROUP_DIM).transpose(1, 0, 2, 3).reshape(
        N_GROUPS, GROUP_DIM, GROUP_DIM)
    kvn, bn = kv_norm.reshape(1, D), b_norm
    kg, qg = k_norm.reshape(1, HEAD_DIM), b_q_norm

    x0, p0, p1, target = x[0], p[0, 0], p[1, 0], loss_target[0]
    h0, z, pooled, mcat, y, x1, e0, gt0, x2, w_kv_f, w_b_in = _layer_a_fwd(
        x0, p0, a_norm_f, a_scale_f, w_a_in, w_group, w_a_out, w_ple0, w_gate0,
        rider=_GatherRider([cast[k] for k in behind_a]))
    hkv, hb, k_all, v_all, q_all, zb = _layer_b_in_fwd(x2, kvn, bn, w_kv_f, w_b_in)
    qg2, kg2 = jnp.concatenate([qg, qg], axis=1), jnp.concatenate([kg, kg], axis=1)
    o, csave, w_b_out, w_ple1, w_gate1 = _attn_fwd(
        q_all, k_all, v_all, qg2, kg2, rider=_GatherRider([cast[k] for k in behind_attn]))
    w_b_out, w_gate1 = w_b_out.reshape(D, D), w_gate1.reshape(D, D)
    yb, x3, e1, gt1, dx4, sq_err = _layer_b_out_fwd(o, zb, x2, p1, target, w_b_out, w_ple1, w_gate1)

    de1, dgp1, dx3, d_o, dzb = _layer_b_out_bwd(dx4, e1, gt1, o, zb, w_gate1, w_b_out)
    partial = {
        "b_w_out": _wgrad(yb, dx3, "wgrad_b_w_out").reshape(N_DEV, ROWS, D),
        "ple_w1": _wgrad(p1, de1, "wgrad_ple_w1", n_split=8),
        "gate1": _wgrad(x3, dgp1, "wgrad_gate1").reshape(N_DEV, ROWS, D),
    }
    grad = {}
    dq, dk, dv, dqg, dkg, grad["b_w_out"], grad["ple_w1"], grad["gate1"] = _attn_bwd(
        q_all, k_all, v_all, qg2, kg2, d_o, csave,
        rider=_ReduceRider([partial[k] for k in ("b_w_out", "ple_w1", "gate1")]))
    partial["w_kv"] = _wgrad(hkv, [dk, dv], "wgrad_w_kv", n_split=8)
    partial["b_w_in"] = _wgrad(hb, [dq, dzb], "wgrad_b_w_in", n_split=8)
    dx2, d_bn, d_kvn, grad["w_kv"] = _layer_b_in_bwd(
        dq, dzb, dk, dv, x2, dx3, w_b_in, w_kv_f, bn, kvn, rider=_ReduceRider([partial["w_kv"]]))
    de0, dgp0, dx1, dm, duz, d_as, grad["b_w_in"] = _layer_a_out_bwd(
        dx2, e0, gt0, z, mcat, w_gate0, w_a_out, a_scale_f, w_group, rider=_ReduceRider([partial["b_w_in"]]))
    partial["gate0"] = _wgrad(x1, dgp0, "wgrad_gate0").reshape(N_DEV, ROWS, D)
    partial["a_w_out"] = _wgrad(y, dx1, "wgrad_a_w_out").reshape(N_DEV, ROWS, D)
    partial["a_w_in"], dw_group, partial["ple_w0"], grad["gate0"], grad["a_w_out"] = _wgrad_layer_a_in(
        h0, duz, pooled, dm, p0, de0, rider=_ReduceRider([partial["gate0"], partial["a_w_out"]]))
    partial["a_w_group"] = dw_group.reshape(N_GROUPS, N_DEV, GROUP_ROWS, GROUP_DIM).transpose(1, 0, 2, 3).reshape(
        N_DEV, N_GROUPS * GROUP_ROWS, GROUP_DIM)
    behind_a_in = ["a_w_in", "a_w_group", "ple_w0"]
    dx0, d_an, *done = _layer_a_in_bwd(duz, x0, dx1, w_a_in, a_norm_f,
                                      rider=_ReduceRider([partial[k] for k in behind_a_in]))
    grad.update(zip(behind_a_in, done))

    given = {
        "a_w_in": (a_w_in, m_a_w_in, v_a_w_in), "a_w_group": (a_w_group, m_a_w_group, v_a_w_group),
        "a_w_out": (a_w_out, m_a_w_out, v_a_w_out), "w_kv": (w_kv, m_w_kv, v_w_kv),
        "b_w_in": (b_w_in, m_b_w_in, v_b_w_in), "b_w_out": (b_w_out, m_b_w_out, v_b_w_out),
        "ple_w": (ple_w, m_ple_w, v_ple_w), "ple_gate_w": (ple_gate_w, m_ple_gate_w, v_ple_gate_w),
    }
    grad["ple_w"] = (grad["ple_w0"], grad["ple_w1"])
    grad["ple_gate_w"] = (grad["gate0"], grad["gate1"])
    updated = _adamw_all(
        "adamw_shards", list(big.values()), [grad[k] for k in big],
        [given[k][1].reshape(big[k].shape) for k in big], [given[k][2].reshape(big[k].shape) for k in big])
    res = {k: tuple(t.reshape(given[k][0].shape) for t in four) for k, four in zip(big, updated)}

    rows = jnp.concatenate([d_kvn, d_bn, d_an, d_as, jnp.zeros((2, D), F32), sq_err, jnp.zeros((1, D), F32)], axis=0)
    tot = _all_reduce_small(rows, (dqg, dkg))
    loss = tot[6, 0]
    small_grad = {
        "kv_norm": tot[0:1], "b_norm": tot[1:2],
        "a_norm": lax.dynamic_slice_in_dim(tot[2:3], me * ROWS, ROWS, axis=1),
        "a_scale": lax.dynamic_slice_in_dim(tot[3:4], me * ROWS, ROWS, axis=1),
        "b_q_norm": tot[4:5, :HEAD_DIM], "k_norm": tot[5:6, :HEAD_DIM],
    }
    small_given = {
        "a_norm": (a_norm, m_a_norm, v_a_norm), "a_scale": (a_scale, m_a_scale, v_a_scale),
        "kv_norm": (kv_norm, m_kv_norm, v_kv_norm), "k_norm": (k_norm, m_k_norm, v_k_norm),
        "b_norm": (b_norm, m_b_norm, v_b_norm), "b_q_norm": (b_q_norm, m_b_q_norm, v_b_q_norm),
    }
    rows_of = {k: [t.reshape(1, -1) for t in three] for k, three in small_given.items()}
    updated = _adamw_all(
        "adamw_gains", [rows_of[k][0] for k in small_given], [small_grad[k] for k in small_given],
        [rows_of[k][1] for k in small_given], [rows_of[k][2] for k in small_given])
    res.update({k: tuple(t.reshape(small_given[k][0].shape) for t in four) for k, four in zip(small_given, updated)})

    order = ["a_norm", "a_w_in", "a_w_group", "a_scale", "a_w_out", "kv_norm", "w_kv", "k_norm", "b_norm",
             "b_w_in", "b_q_norm", "b_w_out", "ple_w", "ple_gate_w"]
    outs = [res[k][kind] for kind in range(4) for k in order]
    return (loss, dx0.reshape(x.shape), *outs)
```

```python
import jax
import jax.numpy as jnp
from jax import lax
from jax.experimental import pallas as pl
from jax.experimental.pallas import tpu as pltpu

F32 = jnp.float32
BF16 = jnp.bfloat16
MESH = pl.DeviceIdType.MESH

N_DEV = 8
D = 1024
N_GROUPS = 4
GROUP_DIM = D // N_GROUPS
HALO = 16
HEADS = 16
HEAD_DIM = D // HEADS
SB_SCALE = HEAD_DIM ** -0.5
TILE = 256
LANES = 128
DEAD_LOG = -120.0
EPS = 1e-6
ADAM_LR = 0.001
ADAM_B1 = 0.9
ADAM_B2 = 0.999
ADAM_EPS = 1e-08
ADAM_WD = 0.01
ADAM_STEP = 10
TM = 256
COLS = 2 * D // N_DEV
ROWS = D // N_DEV
GROUP_ROWS = GROUP_DIM // N_DEV
VMEM_LIMIT = 56 * 1024 * 1024

HBM_SPEC = pl.BlockSpec(memory_space=pltpu.HBM)


def _dot(a, b):
    return jnp.dot(a, b, preferred_element_type=F32)


def _dot_nt(a, b):
    return lax.dot_general(a, b, (((1,), (1,)), ((), ())), preferred_element_type=F32)


def _dot_tn(a, b):
    return lax.dot_general(a, b, (((0,), (0,)), ((), ())), preferred_element_type=F32)


def _sigmoid(x):
    return jax.nn.sigmoid(x)


def _split_dot(x, mat):
    hi = x.astype(BF16)
    lo = (x - hi.astype(F32)).astype(BF16)
    return _dot(hi, mat) + _dot(lo, mat)


def _split_dot_many(xs, mat):
    rows = xs[0].shape[0]
    his = [x.astype(BF16) for x in xs]
    los = [(x - hi.astype(F32)).astype(BF16) for x, hi in zip(xs, his)]
    out = _dot(jnp.concatenate(his + los, axis=0), mat)
    n = len(xs)
    return [out[i * rows:(i + 1) * rows] + out[(n + i) * rows:(n + i + 1) * rows] for i in range(n)]


def _rms(x):
    return lax.rsqrt(jnp.mean(x * x, axis=-1, keepdims=True) + EPS)


def _hosted_call(body, name, n_steps, ins, in_specs, out_specs, out_shape, scratch, rider=None):
    ins, scratch = list(ins), list(scratch)
    if rider is None:
        wrapped, extra_in, extra_out, extra_scratch = body, [], [], []
    else:
        extra_in, extra_out, extra_scratch = rider.arrays, rider.out_shape(), rider.scratch()
        n_in, n_out, n_scr = len(ins), len(out_shape), len(scratch)
        k_in, k_out = len(extra_in), len(extra_out)

        def wrapped(*refs):
            own_in, r_in = refs[:n_in], refs[n_in:n_in + k_in]
            own_out = refs[n_in + k_in:n_in + k_in + n_out]
            r_out = refs[n_in + k_in + n_out:n_in + k_in + n_out + k_out]
            rest = refs[n_in + k_in + n_out + k_out:]
            phases = rider.bind(r_in, r_out, rest[n_scr:])
            step = pl.program_id(0)
            pl.when(step == 0)(phases[0])
            body(*own_in, *own_out, *rest[:n_scr])
            at = 0
            for share, phase in zip(rider.WHEN[1:], phases[1:]):
                at = min(n_steps - 1, max(at + 1, round(share * (n_steps - 1))))
                pl.when(step == at)(phase)

    return pl.pallas_call(
        wrapped, name=name, grid=(n_steps,),
        in_specs=list(in_specs) + [HBM_SPEC] * len(extra_in),
        out_specs=list(out_specs) + [HBM_SPEC] * len(extra_out),
        out_shape=list(out_shape) + list(extra_out), scratch_shapes=scratch + list(extra_scratch),
        compiler_params=pltpu.CompilerParams(dimension_semantics=("arbitrary",), vmem_limit_bytes=VMEM_LIMIT),
    )(*ins, *extra_in)


def _rows_call(body, name, n_rows, row_ins, const_ins, row_outs, const_outs=(), scratch=(),
               reverse=False, tm=TM, rider=None):
    nb = n_rows // tm

    def row_map(i):
        return ((nb - 1 - i) if reverse else i, 0)

    def const_map(nd):
        return lambda i: (0,) * nd

    in_specs = [pl.BlockSpec((tm, a.shape[1]), row_map) for a in row_ins]
    in_specs += [pl.BlockSpec(a.shape, const_map(a.ndim)) for a in const_ins]
    out_specs = [pl.BlockSpec((tm, w), row_map) for (w, _) in row_outs]
    out_specs += [pl.BlockSpec(s, const_map(len(s))) for (s, _) in const_outs]
    out_shape = [jax.ShapeDtypeStruct((n_rows, w), dt) for (w, dt) in row_outs]
    out_shape += [jax.ShapeDtypeStruct(s, dt) for (s, dt) in const_outs]
    return _hosted_call(body, name, nb, list(row_ins) + list(const_ins), in_specs, out_specs, out_shape,
                        scratch, rider)


def _ple_fwd(p_ref, xin, wple_ref, wgate_ref, e_ref, gt_ref):
    pb = p_ref[...].astype(BF16)
    for j in range(N_DEV):
        e_ref[:, j * ROWS:(j + 1) * ROWS] = _dot(pb, wple_ref[j])
    gt = _sigmoid(_dot(xin.astype(BF16), wgate_ref[...]))
    gt_ref[...] = gt
    return xin + e_ref[...] * gt


def _layer_a_fwd(x0, p0, a_norm, a_scale, w_in, w_group, w_out, w_ple, w_gate, rider=None):
    s = x0.shape[0]
    tm = TM

    def body(x_ref, p_ref, an_ref, as_ref, win_ref, wg_ref, wout_ref, wple_ref, wgate_ref,
             h_ref, z_ref, pooled_ref, m_ref, y_ref, x1_ref, e_ref, gt_ref, x2_ref, uext):
        i = pl.program_id(0)

        @pl.when(i == 0)
        def _():
            uext[0:HALO, :] = jnp.zeros((HALO, D), F32)

        x = x_ref[...]
        h = (x * _rms(x) * an_ref[...]).astype(BF16)
        h_ref[...] = h
        for j in range(N_DEV):
            uz = _dot(h, win_ref[j])
            if j < 4:
                uext[HALO:HALO + tm, j * COLS:(j + 1) * COLS] = uz
            else:
                z_ref[:, (j - 4) * COLS:(j - 3) * COLS] = uz
        t = i * tm + lax.broadcasted_iota(jnp.int32, (tm, 1), 0)
        for g in range(N_GROUPS):
            w = 2 ** (g + 1)
            cols = slice(g * GROUP_DIM, (g + 1) * GROUP_DIM)
            ext = uext[:, cols]
            acc = ext
            k = 1
            while k < w:
                acc = acc + pltpu.roll(acc, k, 0)
                k *= 2
            cnt = jnp.minimum(t + 1, w).astype(F32)
            pooled = (acc[HALO:] / cnt - ext[HALO:]).astype(BF16)
            pooled_ref[:, cols] = pooled
            m_ref[:, cols] = _dot(pooled, wg_ref[g])
        uext[0:HALO, :] = uext[tm:tm + HALO, :]
        z = z_ref[...]
        y = (m_ref[...] * as_ref[...] * (z * _sigmoid(z))).astype(BF16)
        y_ref[...] = y
        x1 = x + _dot(y, wout_ref[...])
        x1_ref[...] = x1
        x2_ref[...] = _ple_fwd(p_ref, x1, wple_ref, wgate_ref, e_ref, gt_ref)

    row_outs = [(D, BF16), (D, F32), (D, BF16), (D, F32), (D, BF16), (D, F32), (D, F32), (D, F32), (D, F32)]
    return _rows_call(body, "layer_a_fwd", s, [x0, p0], [a_norm, a_scale, w_in, w_group, w_out, w_ple, w_gate],
                      row_outs, scratch=[pltpu.VMEM((tm + HALO, D), F32)], rider=rider)


def _layer_b_in_fwd(x2, kv_norm, b_norm, w_kv, w_bin):
    s = x2.shape[0]

    def body(x_ref, kvn_ref, bn_ref, wkv_ref, wbin_ref, hkv_ref, hb_ref, k_ref, v_ref, q_ref, zb_ref):
        x = x_ref[...]
        n = x * _rms(x)
        hkv = (n * kvn_ref[...]).astype(BF16)
        hb = (n * bn_ref[...]).astype(BF16)
        hkv_ref[...] = hkv
        hb_ref[...] = hb
        for j in range(N_DEV):
            kv = _dot(hkv, wkv_ref[j])
            qz = _dot(hb, wbin_ref[j])
            if j < 4:
                cols = slice(j * COLS, (j + 1) * COLS)
                k_ref[:, cols] = kv
                q_ref[:, cols] = qz
            else:
                cols = slice((j - 4) * COLS, (j - 3) * COLS)
                v_ref[:, cols] = kv.astype(BF16)
                zb_ref[:, cols] = qz

    row_outs = [(D, BF16), (D, BF16), (D, F32), (D, BF16), (D, F32), (D, F32)]
    return _rows_call(body, "layer_b_in_fwd", s, [x2], [kv_norm, b_norm, w_kv, w_bin], row_outs)


def _tri(after):
    r = lax.broadcasted_iota(jnp.int32, (TILE, TILE), 0)
    c = lax.broadcasted_iota(jnp.int32, (TILE, TILE), 1)
    return jnp.where((r > c) if after else (r < c), 1.0, 0.0).astype(BF16)


def _half_sums(v):
    r = lax.broadcasted_iota(jnp.int32, (LANES, LANES), 0) < HEAD_DIM
    c = lax.broadcasted_iota(jnp.int32, (LANES, LANES), 1) < HEAD_DIM
    same_head = jnp.where(r == c, 1.0, 0.0).astype(BF16)
    return _split_dot(v, same_head)


def _pair_norm(x):
    r = lax.rsqrt(_half_sums(x * x) * (1.0 / HEAD_DIM) + EPS)
    return x * r, r


def _tile_logits(qblk, kblk, diagonal):
    l = _dot_nt(qblk, kblk)
    sp = jnp.maximum(l, 0.0) + jnp.log(1.0 + jnp.exp(-jnp.abs(l)))
    ls = l - sp
    if not diagonal:
        return None, -sp, ls
    mask = lax.broadcasted_iota(jnp.int32, l.shape, 1) < lax.broadcasted_iota(jnp.int32, l.shape, 0)
    return mask, jnp.where(mask, -sp, 0.0), ls


def _attn_fwd(q_all, k_all, v_all, q_gain2, k_gain2, rider=None):
    s = q_all.shape[0]
    nt = s // TILE

    def body(q_ref, k_ref, v_ref, qg_ref, kg_ref, o_ref, c_ref, qs, ks, vs, tri, acc, right, cmat):
        tri[...] = _tri(True)
        lane = lax.broadcasted_iota(jnp.int32, (TILE, LANES), 1)
        qn, _ = _pair_norm(q_ref[...])
        kn, _ = _pair_norm(k_ref[...])
        qsc = (qn * qg_ref[...] * SB_SCALE).astype(BF16)
        ksc = (kn * kg_ref[...]).astype(BF16)
        for hh in range(2):
            sl = slice(hh * HEAD_DIM, (hh + 1) * HEAD_DIM)
            qs[hh] = qsc[:, sl]
            ks[hh] = ksc[:, sl]
            vs[hh] = v_ref[:, sl]

        def tile(qrows, kb, diagonal):
            rows = pl.ds(pl.multiple_of(kb * TILE, TILE), TILE)
            loaded = [(qs[hh, qrows, :], ks[hh, rows, :], vs[hh, rows, :], right[hh], cmat[hh], acc[hh])
                      for hh in range(2)]
            logits = [_tile_logits(q, k, diagonal) for q, k, _, _, _, _ in loaded]
            later = _split_dot_many([lk for _, lk, _ in logits], tri[...])
            results = []
            for (q, k, v, rt, cm, ac), (mask, lk, ls), lt in zip(loaded, logits, later):
                a = jnp.exp(ls + lt + rt)
                if diagonal:
                    a = jnp.where(mask, a, 0.0)
                results.append((ac + _dot(a.astype(BF16), v), jnp.where(lane == kb, rt[:, :LANES], cm),
                                rt + jnp.sum(lk, axis=1, keepdims=True)))
            for hh, (ac, cm, rt) in enumerate(results):
                acc[hh] = ac
                cmat[hh] = cm
                right[hh] = rt

        def diagonal_and_left(qrows, qb):
            here = pl.ds(pl.multiple_of(qb * TILE, TILE), TILE)
            left = pl.ds(pl.multiple_of((qb - 1) * TILE, TILE), TILE)
            q = [qs[hh, qrows, :] for hh in range(2)]
            on_diag = [_tile_logits(q[hh], ks[hh, here, :], True) for hh in range(2)]
            beside = [_tile_logits(q[hh], ks[hh, left, :], False) for hh in range(2)]
            later = _split_dot_many([lk for _, lk, _ in on_diag + beside], tri[...])
            for hh in range(2):
                mask, lk_d, ls_d = on_diag[hh]
                _, lk_l, ls_l = beside[hh]
                a_d = jnp.where(mask, jnp.exp(ls_d + later[hh]), 0.0)
                past_diag = jnp.sum(lk_d, axis=1, keepdims=True)
                a_l = jnp.exp(ls_l + later[2 + hh] + past_diag)
                acc[hh] = _dot(a_d.astype(BF16), vs[hh, here, :]) + _dot(a_l.astype(BF16), vs[hh, left, :])
                cmat[hh] = jnp.where(lane == qb - 1, past_diag, 0.0)
                right[hh] = jnp.broadcast_to(past_diag + jnp.sum(lk_l, axis=1, keepdims=True), (TILE, TILE))

        def q_step(qb, _):
            r0 = pl.multiple_of(qb * TILE, TILE)
            qrows = pl.ds(r0, TILE)

            @pl.when(qb == 0)
            def _():
                acc[...] = jnp.zeros((2, TILE, HEAD_DIM), F32)
                right[...] = jnp.zeros((2, TILE, TILE), F32)
                cmat[...] = jnp.zeros((2, TILE, LANES), F32)
                tile(qrows, qb, True)

            pl.when(qb > 0)(lambda: diagonal_and_left(qrows, qb))

            def live():
                return (jnp.max(right[:, :, :LANES]) > DEAD_LOG).astype(jnp.int32)

            def k_step(c):
                kb = c[0] - 1
                tile(qrows, kb, False)
                return kb, live()

            first, _ = lax.while_loop(lambda c: (c[0] > 0) & (c[1] > 0), k_step, (jnp.maximum(qb - 1, 0), live()))
            for hh in range(2):
                o_ref[qrows, hh * HEAD_DIM:(hh + 1) * HEAD_DIM] = acc[hh]
                c_ref[hh, qrows, :] = jnp.where(lane == LANES - 1, first.astype(F32), cmat[hh])
            return 0

        lax.fori_loop(0, nt, q_step, 0)

    pair = pl.BlockSpec((s, LANES), lambda h: (0, h))
    gain = pl.BlockSpec((1, LANES), lambda h: (0, 0))
    return _hosted_call(
        body, "attn_fwd", HEADS // 2, [q_all, k_all, v_all, q_gain2, k_gain2],
        [pair, pair, pair, gain, gain], [pair, pl.BlockSpec((2, s, LANES), lambda h: (h, 0, 0))],
        [jax.ShapeDtypeStruct((s, D), F32), jax.ShapeDtypeStruct((HEADS, s, LANES), F32)],
        [pltpu.VMEM((2, s, HEAD_DIM), BF16)] * 3
        + [pltpu.VMEM((TILE, TILE), BF16), pltpu.VMEM((2, TILE, HEAD_DIM), F32), pltpu.VMEM((2, TILE, TILE), F32),
           pltpu.VMEM((2, TILE, LANES), F32)], rider)


def _layer_b_out_fwd(o, zb, x2, p1, target, w_out, w_ple, w_gate):
    s = o.shape[0]

    def body(o_ref, zb_ref, x2_ref, p_ref, t_ref, wout_ref, wple_ref, wgate_ref,
             yb_ref, x3_ref, e_ref, gt_ref, dx4_ref, loss_ref):
        zb = zb_ref[...]
        yb = (o_ref[...] * (zb * _sigmoid(zb))).astype(BF16)
        yb_ref[...] = yb
        x3 = x2_ref[...] + _dot(yb, wout_ref[...])
        x3_ref[...] = x3
        x4 = _ple_fwd(p_ref, x3, wple_ref, wgate_ref, e_ref, gt_ref)
        d = x4 - t_ref[...]
        dx4_ref[...] = d * (1.0 / D)

        @pl.when(pl.program_id(0) == 0)
        def _():
            loss_ref[...] = jnp.zeros((1, D), F32)

        loss_ref[...] += jnp.sum(d * d, axis=0, keepdims=True)

    row_outs = [(D, BF16), (D, F32), (D, F32), (D, F32), (D, F32)]
    return _rows_call(body, "layer_b_out_fwd", s, [o, zb, x2, p1, target], [w_out, w_ple, w_gate], row_outs,
                      const_outs=[((1, D), F32)])


def _ple_bwd(dxo, e_ref, gt_ref, wgate_ref, de_ref, dgp_ref):
    e = e_ref[...]
    gt = gt_ref[...]
    de_ref[...] = (dxo * gt).astype(BF16)
    dgp = (dxo * e * gt * (1.0 - gt)).astype(BF16)
    dgp_ref[...] = dgp
    return dxo + _dot_nt(dgp, wgate_ref[...])


def _silu_grads(z):
    sg = _sigmoid(z)
    return z * sg, sg * (1.0 + z * (1.0 - sg))


def _layer_b_out_bwd(dx4, e1, gt1, o, zb, w_gate, w_out):
    s = dx4.shape[0]

    def body(dx4_ref, e_ref, gt_ref, o_ref, zb_ref, wgate_ref, wout_ref,
             de_ref, dgp_ref, dx3_ref, do_ref, dzb_ref):
        dx3 = _ple_bwd(dx4_ref[...], e_ref, gt_ref, wgate_ref, de_ref, dgp_ref)
        dx3_ref[...] = dx3
        dyb = _dot_nt(dx3.astype(BF16), wout_ref[...])
        silu, dsilu = _silu_grads(zb_ref[...])
        do_ref[...] = (dyb * silu).astype(BF16)
        dzb_ref[...] = (dyb * o_ref[...] * dsilu).astype(BF16)

    row_outs = [(D, BF16), (D, BF16), (D, F32), (D, BF16), (D, BF16)]
    return _rows_call(body, "layer_b_out_bwd", s, [dx4, e1, gt1, o, zb], [w_gate, w_out], row_outs)


def _attn_bwd(q_all, k_all, v_all, q_gain2, k_gain2, d_o, csave, rider=None):
    s = q_all.shape[0]
    nt = s // TILE

    def body(q_ref, k_ref, v_ref, qg_ref, kg_ref, do_ref, c_ref,
             dq_ref, dk_ref, dv_ref, dqg_ref, dkg_ref,
             qs, ks, vs, dos, qt, dot_t, tri_a, tri_b, dqa, dkt, dvt, dqb, left):
        tri_a[...] = _tri(True)
        tri_b[...] = _tri(False)
        lane = lax.broadcasted_iota(jnp.int32, (TILE, LANES), 1)
        qn, qr = _pair_norm(q_ref[...])
        kn, kr = _pair_norm(k_ref[...])
        qsc = qn * qg_ref[...] * SB_SCALE
        ksc = (kn * kg_ref[...]).astype(BF16)
        q_t = qsc.T.astype(BF16)
        do_t = do_ref[...].astype(F32).T.astype(BF16)
        for j in range(nt):
            qt[j] = q_t[:, j * TILE:(j + 1) * TILE]
            dot_t[j] = do_t[:, j * TILE:(j + 1) * TILE]
        qsc = qsc.astype(BF16)
        for hh in range(2):
            sl = slice(hh * HEAD_DIM, (hh + 1) * HEAD_DIM)
            qs[hh] = qsc[:, sl]
            ks[hh] = ksc[:, sl]
            vs[hh] = v_ref[:, sl]
            dos[hh] = do_ref[:, sl]

        def tile(qb, qrows, kb, diagonal):
            rows = pl.ds(pl.multiple_of(kb * TILE, TILE), TILE)
            heads = range(2)
            kblk = [ks[hh, rows, :] for hh in heads]
            logits = [_tile_logits(qs[hh, qrows, :], kblk[hh], diagonal) for hh in heads]
            later = _split_dot_many([lk for _, lk, _ in logits], tri_a[...])
            a, g = [], []
            for hh in heads:
                mask, _, ls = logits[hh]
                right = jnp.sum(jnp.where(lane == kb, c_ref[hh, qrows, :], 0.0), axis=1, keepdims=True)
                a_h = jnp.exp(ls + later[hh] + right)
                a.append(jnp.where(mask, a_h, 0.0) if diagonal else a_h)
                g.append(a[hh] * _dot_nt(dos[hh, qrows, :], vs[hh, rows, :]))
            before = _split_dot_many(g, tri_b[...])
            for hh in heads:
                sl = slice(hh * HEAD_DIM, (hh + 1) * HEAD_DIM)
                mask, _, ls = logits[hh]
                beta = jnp.exp(ls)
                lf = left[hh]
                dl = g[hh] * (1.0 - beta) - (before[hh] + lf) * beta
                if diagonal:
                    dl = jnp.where(mask, dl, 0.0)
                dl = dl.astype(BF16)
                left[hh] = lf + jnp.sum(g[hh], axis=1, keepdims=True)
                dqb[hh] += _dot(dl, kblk[hh])
                dk_t, dv_t = _dot(qt[qb, sl, :], dl), _dot(dot_t[qb, sl, :], a[hh].astype(BF16))
                if diagonal:
                    dkt[kb, sl, :] = dk_t
                    dvt[kb, sl, :] = dv_t
                else:
                    dkt[kb, sl, :] += dk_t
                    dvt[kb, sl, :] += dv_t

        def left_and_diagonal(qb, qrows):
            here = pl.ds(pl.multiple_of(qb * TILE, TILE), TILE)
            beside = pl.ds(pl.multiple_of((qb - 1) * TILE, TILE), TILE)
            heads = range(2)
            q = [qs[hh, qrows, :] for hh in heads]
            do = [dos[hh, qrows, :] for hh in heads]
            k_d, k_l = [ks[hh, here, :] for hh in heads], [ks[hh, beside, :] for hh in heads]
            on_diag = [_tile_logits(q[hh], k_d[hh], True) for hh in heads]
            on_left = [_tile_logits(q[hh], k_l[hh], False) for hh in heads]
            later = _split_dot_many([lk for _, lk, _ in on_diag + on_left], tri_a[...])
            a_d, a_l, g_d, g_l = [], [], [], []
            for hh in heads:
                mask, lk_d, ls_d = on_diag[hh]
                a_d.append(jnp.where(mask, jnp.exp(ls_d + later[hh]), 0.0))
                a_l.append(jnp.exp(on_left[hh][2] + later[2 + hh] + jnp.sum(lk_d, axis=1, keepdims=True)))
                g_d.append(a_d[hh] * _dot_nt(do[hh], vs[hh, here, :]))
                g_l.append(a_l[hh] * _dot_nt(do[hh], vs[hh, beside, :]))
            before = _split_dot_many(g_l + g_d, tri_b[...])
            for hh in heads:
                sl = slice(hh * HEAD_DIM, (hh + 1) * HEAD_DIM)
                beta_l, beta_d = jnp.exp(on_left[hh][2]), jnp.exp(on_diag[hh][2])
                dl_l = (g_l[hh] * (1.0 - beta_l) - before[hh] * beta_l).astype(BF16)
                carried = jnp.sum(g_l[hh], axis=1, keepdims=True)
                dl_d = g_d[hh] * (1.0 - beta_d) - (before[2 + hh] + carried) * beta_d
                dl_d = jnp.where(on_diag[hh][0], dl_d, 0.0).astype(BF16)
                dqa[qrows, sl] = (_dot(dl_l, k_l[hh]) + _dot(dl_d, k_d[hh])) * SB_SCALE
                dkt[qb - 1, sl, :] += _dot(qt[qb, sl, :], dl_l)
                dkt[qb, sl, :] = _dot(qt[qb, sl, :], dl_d)
                dvt[qb - 1, sl, :] += _dot(dot_t[qb, sl, :], a_l[hh].astype(BF16))
                dvt[qb, sl, :] = _dot(dot_t[qb, sl, :], a_d[hh].astype(BF16))

        def q_step(qb, _):
            qrows = pl.ds(pl.multiple_of(qb * TILE, TILE), TILE)
            first = jnp.max(jnp.where(lane == LANES - 1, c_ref[0, qrows, :], 0.0)).astype(jnp.int32)
            usual = (qb > 0) & (first == qb - 1)

            @pl.when(usual)
            def _():
                left_and_diagonal(qb, qrows)

            @pl.when(jnp.logical_not(usual))
            def _():
                dqb[...] = jnp.zeros((2, TILE, HEAD_DIM), F32)
                left[...] = jnp.zeros((2, TILE, TILE), F32)

                def k_step(kb, _):
                    tile(qb, qrows, kb, False)
                    return 0

                lax.fori_loop(first, qb, k_step, 0)
                tile(qb, qrows, qb, True)
                for hh in range(2):
                    dqa[qrows, hh * HEAD_DIM:(hh + 1) * HEAD_DIM] = dqb[hh] * SB_SCALE

            return 0

        lax.fori_loop(0, nt, q_step, 0)

        def norm_bwd(dy, xn, r, g_ref, dx_ref, dg_ref):
            dg_ref[...] = jnp.sum(dy * xn, axis=0, keepdims=True)
            dxn = dy * g_ref[...]
            dx_ref[...] = (r * (dxn - xn * (_half_sums(dxn * xn) * (1.0 / HEAD_DIM)))).astype(BF16)

        norm_bwd(dqa[...], qn, qr, qg_ref, dq_ref, dqg_ref)
        for j in range(nt):
            dqa[j * TILE:(j + 1) * TILE, :] = dkt[j].T
            dv_ref[j * TILE:(j + 1) * TILE, :] = dvt[j].T.astype(BF16)
        norm_bwd(dqa[...], kn, kr, kg_ref, dk_ref, dkg_ref)

    pair = pl.BlockSpec((s, LANES), lambda h: (0, h))
    gain = pl.BlockSpec((1, LANES), lambda h: (0, 0))
    dgain = pl.BlockSpec((None, 1, LANES), lambda h: (h, 0, 0))
    return _hosted_call(
        body, "attn_bwd", HEADS // 2, [q_all, k_all, v_all, q_gain2, k_gain2, d_o, csave],
        [pair, pair, pair, gain, gain, pair, pl.BlockSpec((2, s, LANES), lambda h: (h, 0, 0))],
        [pair, pair, pair, dgain, dgain],
        [jax.ShapeDtypeStruct((s, D), BF16)] * 3 + [jax.ShapeDtypeStruct((HEADS // 2, 1, LANES), F32)] * 2,
        [pltpu.VMEM((2, s, HEAD_DIM), BF16)] * 4
        + [pltpu.VMEM((nt, LANES, TILE), BF16)] * 2 + [pltpu.VMEM((TILE, TILE), BF16)] * 2
        + [pltpu.VMEM((s, LANES), F32)] + [pltpu.VMEM((nt, LANES, TILE), F32)] * 2
        + [pltpu.VMEM((2, TILE, HEAD_DIM), F32), pltpu.VMEM((2, TILE, TILE), F32)], rider)


def _norm_bwd_rows(dh, x, gain, dgain_ref):
    r = _rms(x)
    n = x * r
    dgain_ref[...] += jnp.sum(dh * n, axis=0, keepdims=True)
    dn = dh * gain
    return r * (dn - n * jnp.mean(dn * n, axis=-1, keepdims=True))


def _layer_b_in_bwd(dq, dzb, dk, dv, x2, dx3, w_bin, w_kv, b_norm, kv_norm, rider=None):
    s = x2.shape[0]

    def body(dq_ref, dzb_ref, dk_ref, dv_ref, x_ref, dx3_ref, wbin_ref, wkv_ref, bn_ref, kvn_ref,
             dx2_ref, dbn_ref, dkvn_ref):
        @pl.when(pl.program_id(0) == 0)
        def _():
            dbn_ref[...] = jnp.zeros((1, D), F32)
            dkvn_ref[...] = jnp.zeros((1, D), F32)

        dhb = jnp.zeros((TM, D), F32)
        dhkv = jnp.zeros((TM, D), F32)
        for j in range(N_DEV):
            cols = slice((j % 4) * COLS, (j % 4 + 1) * COLS)
            dhb = dhb + _dot_nt((dq_ref if j < 4 else dzb_ref)[:, cols], wbin_ref[j])
            dhkv = dhkv + _dot_nt((dk_ref if j < 4 else dv_ref)[:, cols], wkv_ref[j])
        x = x_ref[...]
        dx2 = dx3_ref[...] + _norm_bwd_rows(dhb, x, bn_ref[...], dbn_ref)
        dx2_ref[...] = dx2 + _norm_bwd_rows(dhkv, x, kvn_ref[...], dkvn_ref)

    return _rows_call(body, "layer_b_in_bwd", s, [dq, dzb, dk, dv, x2, dx3], [w_bin, w_kv, b_norm, kv_norm],
                      [(D, F32)], const_outs=[((1, D), F32), ((1, D), F32)], rider=rider)


def _layer_a_out_bwd(dx2, e0, gt0, z, m, w_gate, w_out, a_scale, w_group, rider=None):
    s = dx2.shape[0]
    tm = TM
    nb = s // tm

    def body(dx2_ref, e_ref, gt_ref, z_ref, m_ref, wgate_ref, wout_ref, as_ref, wg_ref,
             de_ref, dgp_ref, dx1_ref, dm_ref, duz_ref, das_ref, ext):
        i = pl.program_id(0)

        @pl.when(i == 0)
        def _():
            das_ref[...] = jnp.zeros((1, D), F32)
            ext[tm:tm + HALO, :] = jnp.zeros((HALO, D), F32)

        dx1 = _ple_bwd(dx2_ref[...], e_ref, gt_ref, wgate_ref, de_ref, dgp_ref)
        dx1_ref[...] = dx1
        dy = _dot_nt(dx1.astype(BF16), wout_ref[...])
        silu, dsilu = _silu_grads(z_ref[...])
        m = m_ref[...]
        dmixed = dy * silu
        duz_ref[:, D:] = (dy * (m * as_ref[...]) * dsilu).astype(BF16)
        das_ref[...] += jnp.sum(dmixed * m, axis=0, keepdims=True)
        dm_ref[...] = (dmixed * as_ref[...]).astype(BF16)
        t = (nb - 1 - i) * tm + lax.broadcasted_iota(jnp.int32, (tm, 1), 0)
        n_ext = tm + HALO
        for g in range(N_GROUPS):
            w = 2 ** (g + 1)
            cols = slice(g * GROUP_DIM, (g + 1) * GROUP_DIM)
            dpool = _dot_nt(dm_ref[:, cols], wg_ref[g])
            ext[0:tm, cols] = dpool / jnp.minimum(t + 1, w).astype(F32)
            acc = ext[:, cols]
            k = 1
            while k < w:
                acc = acc + pltpu.roll(acc, n_ext - k, 0)
                k *= 2
            duz_ref[:, cols] = (acc[:tm] - dpool).astype(BF16)
        ext[tm:tm + HALO, :] = ext[0:HALO, :]

    row_outs = [(D, BF16), (D, BF16), (D, F32), (D, BF16), (2 * D, BF16)]
    return _rows_call(body, "layer_a_out_bwd", s, [dx2, e0, gt0, z, m], [w_gate, w_out, a_scale, w_group],
                      row_outs, const_outs=[((1, D), F32)], scratch=[pltpu.VMEM((tm + HALO, D), F32)],
                      reverse=True, rider=rider)


def _layer_a_in_bwd(duz, x0, dx1, w_in, a_norm, rider=None):
    s = x0.shape[0]

    def body(duz_ref, x_ref, dx1_ref, win_ref, an_ref, dx0_ref, dan_ref):
        @pl.when(pl.program_id(0) == 0)
        def _():
            dan_ref[...] = jnp.zeros((1, D), F32)

        dh = jnp.zeros((TM, D), F32)
        for j in range(N_DEV):
            dh = dh + _dot_nt(duz_ref[:, j * COLS:(j + 1) * COLS], win_ref[j])
        dx0_ref[...] = dx1_ref[...] + _norm_bwd_rows(dh, x_ref[...], an_ref[...], dan_ref)

    return _rows_call(body, "layer_a_in_bwd", s, [duz, x0, dx1], [w_in, a_norm], [(D, F32)],
                      const_outs=[((1, D), F32)], rider=rider)


def _wgrad(a, b, name, n_split=1, rider=None):
    bs = list(b) if isinstance(b, (list, tuple)) else [b]
    s, k = a.shape
    n = sum(part.shape[1] for part in bs)
    tk = TM
    nb = n // n_split

    def body(a_ref, *refs):
        o_ref = refs[-1]
        lhs = a_ref[...].astype(BF16)
        done = 0
        for b_ref in refs[:-1]:
            res = _dot_tn(lhs, b_ref[...].astype(BF16))
            if n_split == 1:
                o_ref[...] = res.astype(BF16)
            else:
                for j in range(res.shape[1] // nb):
                    o_ref[done + j] = res[:, j * nb:(j + 1) * nb].astype(BF16)
                done += res.shape[1] // nb

    if n_split == 1:
        b_specs = [pl.BlockSpec((s, n), lambda i: (0, 0))]
        out_spec = pl.BlockSpec((tk, n), lambda i: (i, 0))
        out_shape = jax.ShapeDtypeStruct((k, n), BF16)
    else:
        b_specs = [pl.BlockSpec(part.shape, lambda i: (0, 0)) for part in bs]
        out_spec = pl.BlockSpec((n_split, tk, nb), lambda i: (0, i, 0))
        out_shape = jax.ShapeDtypeStruct((n_split, k, nb), BF16)
    res = _hosted_call(body, name, k // tk, [a] + bs, [pl.BlockSpec((s, tk), lambda i: (0, i))] + b_specs,
                       [out_spec], [out_shape], [], rider)
    return res[0] if rider is None else res


def _wgrad_layer_a_in(h0, duz, pooled, dm, p0, de0, rider):
    s = h0.shape[0]
    n_ple = p0.shape[1]

    def body(h_ref, duz_ref, pooled_ref, dm_ref, p_ref, de_ref, in_ref, group_ref, ple_ref):
        res = _dot_tn(h_ref[...], duz_ref[...])
        for j in range(N_DEV):
            in_ref[j] = res[:, j * COLS:(j + 1) * COLS].astype(BF16)
        group_ref[...] = _dot_tn(pooled_ref[...], dm_ref[...]).astype(BF16)

        @pl.when(pl.program_id(0) == 0)
        def _():
            ple = _dot_tn(p_ref[...].astype(BF16), de_ref[...])
            for j in range(N_DEV):
                ple_ref[j] = ple[:, j * ROWS:(j + 1) * ROWS].astype(BF16)

    block = pl.BlockSpec((s, TM), lambda i: (0, i))
    return _hosted_call(
        body, "wgrad_layer_a_in", D // TM, [h0, duz, pooled, dm, p0, de0],
        [block, pl.BlockSpec(duz.shape, lambda i: (0, 0)), block, block,
         pl.BlockSpec(p0.shape, lambda i: (0, 0)), pl.BlockSpec(de0.shape, lambda i: (0, 0))],
        [pl.BlockSpec((N_DEV, TM, COLS), lambda i: (0, i, 0)), pl.BlockSpec((None, TM, GROUP_DIM), lambda i: (i, 0, 0)),
         pl.BlockSpec((N_DEV, n_ple, ROWS), lambda i: (0, 0, 0))],
        [jax.ShapeDtypeStruct((N_DEV, D, COLS), BF16), jax.ShapeDtypeStruct((N_GROUPS, GROUP_DIM, GROUP_DIM), BF16),
         jax.ShapeDtypeStruct((N_DEV, n_ple, ROWS), BF16)], [], rider)


def _cast_shards(shards):
    n = len(shards)
    layers = [a.shape[0] if a.ndim == 3 else 0 for a in shards]

    def body(*refs):
        outs = iter(refs[n:])
        for a in range(n):
            if layers[a]:
                for t in range(layers[a]):
                    next(outs)[...] = refs[a][t].astype(BF16)
            else:
                next(outs)[...] = refs[a][...].astype(BF16)

    out_shape = []
    for a, k in zip(shards, layers):
        out_shape += [jax.ShapeDtypeStruct(a.shape[-2:], BF16)] * max(k, 1)
    vmem = pl.BlockSpec(memory_space=pltpu.VMEM)
    return pl.pallas_call(
        body, name="cast_shards", in_specs=[vmem] * n, out_specs=[vmem] * len(out_shape), out_shape=out_shape,
        compiler_params=pltpu.CompilerParams(vmem_limit_bytes=VMEM_LIMIT),
    )(*shards)


def _adamw(w, g, m, v):
    m = ADAM_B1 * m + (1.0 - ADAM_B1) * g
    v = ADAM_B2 * v + (1.0 - ADAM_B2) * jnp.square(g)
    m_hat = m / (1.0 - ADAM_B1 ** ADAM_STEP)
    v_hat = v / (1.0 - ADAM_B2 ** ADAM_STEP)
    delta = -ADAM_LR * (m_hat / (jnp.sqrt(v_hat) + ADAM_EPS) + ADAM_WD * w)
    return delta, m, v


def _place():
    return lax.axis_index("x"), lax.axis_index("y"), lax.axis_index("c")


def _all_gather(shards):
    return _alone("all_gather_weights", _GatherRider(shards))


def _alone(name, rider):
    n_in, n_out = len(rider.arrays), len(rider.out_shape())

    def body(*refs):
        for phase in rider.bind(refs[:n_in], refs[n_in:n_in + n_out], refs[n_in + n_out:]):
            phase()

    return pl.pallas_call(
        body, name=name, in_specs=[HBM_SPEC] * n_in, out_specs=[HBM_SPEC] * n_out,
        out_shape=rider.out_shape(), scratch_shapes=rider.scratch(),
        compiler_params=pltpu.CompilerParams(vmem_limit_bytes=VMEM_LIMIT),
    )(*rider.arrays)


class _GatherRider:
    WHEN = (0.0, 0.7, 1.0)

    def __init__(self, shards):
        self.arrays = list(shards)

    def out_shape(self):
        return _Gather.out_shape(self.arrays)

    def scratch(self):
        return _Gather.semaphores(len(self.arrays))

    def bind(self, ins, outs, scratch):
        moving = _Gather(ins, outs, *scratch)
        return moving.start, moving.forward, moving.finish


class _ReduceRider:
    WHEN = (0.0, 0.15, 0.4, 1.0)

    def __init__(self, partials):
        self.arrays = list(partials)

    def out_shape(self):
        return [jax.ShapeDtypeStruct(a.shape[1:], F32) for a in self.arrays]

    def scratch(self):
        n = len(self.arrays)
        dma = pltpu.SemaphoreType.DMA

        def blocks(k):
            return [pltpu.VMEM((k,) + a.shape[1:], BF16) for a in self.arrays]

        halves = [pltpu.VMEM((2, a.shape[1] // 2) + a.shape[2:], BF16) for a in self.arrays]
        return (blocks(4) + blocks(4) + halves + blocks(2) + [pltpu.VMEM(a.shape[1:], F32) for a in self.arrays]
                + [dma((4 * n,)), dma((4 * n,)), dma((4 * n,)), dma((2 * n,)), dma((2 * n,)),
                   dma((2 * n,)), dma((2 * n,)), dma((n,))])

    def bind(self, ins, outs, scratch):
        n = len(ins)
        mine, landed, halves, arrived, total = (scratch[i * n:(i + 1) * n] for i in range(5))
        send1, recv1, local1, send_h, recv_h, send2, recv2, out_sems = scratch[5 * n:]
        x, y, c = _place()
        plane = 2 * x + y
        via = [(x, 1 - y, c), (1 - x, y, c)]
        nbr = [(1 - x, y, c), (x, 1 - y, c)]
        nbr_block = [2 * (1 - x) + y, 2 * x + (1 - y)]
        diag_block = 2 * (1 - x) + (1 - y)

        def to_sibling(a, k):
            return pltpu.make_async_remote_copy(
                src_ref=ins[a].at[2 * k + (1 - c)], dst_ref=landed[a].at[k],
                send_sem=send1.at[4 * a + k], recv_sem=recv1.at[4 * a + k],
                device_id=(x, y, 1 - c), device_id_type=MESH)

        def own_block(a, k):
            return pltpu.make_async_copy(ins[a].at[2 * k + c], mine[a].at[k], local1.at[4 * a + k])

        def half_of(a, ref, h):
            rows = self.arrays[a].shape[1] // 2
            return ref.at[pl.ds(h * rows, rows)]

        def half_out(a, h):
            return pltpu.make_async_remote_copy(
                src_ref=half_of(a, mine[a].at[diag_block], h), dst_ref=halves[a].at[h],
                send_sem=send_h.at[2 * a + h], recv_sem=recv_h.at[2 * a + h],
                device_id=via[h], device_id_type=MESH)

        def to_owner(a, h):
            return pltpu.make_async_remote_copy(
                src_ref=mine[a].at[nbr_block[h]], dst_ref=arrived[a].at[h],
                send_sem=send2.at[2 * a + h], recv_sem=recv2.at[2 * a + h],
                device_id=nbr[h], device_id_type=MESH)

        def result(a):
            return pltpu.make_async_copy(total[a], outs[a], out_sems.at[a])

        def exchange_cores():
            for a in range(n):
                for k in range(4):
                    to_sibling(a, k).start()
                    own_block(a, k).start()

        def pair_sums():
            for a in range(n):
                for k in range(4):
                    own_block(a, k).wait()
                    to_sibling(a, k).wait_recv()
                total[a][...] = mine[a][plane].astype(F32) + landed[a][plane].astype(F32)
                for k in range(4):
                    mine[a][k] = (mine[a][k].astype(F32) + landed[a][k].astype(F32)).astype(BF16)
                for h in range(2):
                    half_out(a, h).start()

        def fold_and_send():
            for a in range(n):
                rows = self.arrays[a].shape[1] // 2
                for h in range(2):
                    half_out(a, h).wait_recv()
                    part = mine[a].at[nbr_block[h]]
                    span = slice(h * rows, (h + 1) * rows)
                    part[span] = (part[span].astype(F32) + halves[a][h].astype(F32)).astype(BF16)
                    to_owner(a, h).start()

        def finish():
            for a in range(n):
                for h in range(2):
                    to_owner(a, h).wait_recv()
                    total[a][...] += arrived[a][h].astype(F32)
                result(a).start()
            for a in range(n):
                for k in range(4):
                    to_sibling(a, k).wait_send()
                for h in range(2):
                    half_out(a, h).wait_send()
                    to_owner(a, h).wait_send()
                result(a).wait()

        return exchange_cores, pair_sums, fold_and_send, finish


class _Gather:
    COPIES = 9

    def __init__(self, ins, outs, send_sems, recv_sems, local_sems):
        self.ins, self.outs = ins, outs
        self.send_sems, self.recv_sems, self.local_sems = send_sems, recv_sems, local_sems
        self.x, self.y, self.c = _place()

    @staticmethod
    def out_shape(shards):
        return [jax.ShapeDtypeStruct((N_DEV,) + a.shape, a.dtype) for a in shards]

    @staticmethod
    def semaphores(n):
        dma = pltpu.SemaphoreType.DMA
        return [dma((_Gather.COPIES * n,)), dma((_Gather.COPIES * n,)), dma((n,))]

    def _copy(self, a, k, block, to, own=False, half=None):
        px, py, pc = block
        slot = self.outs[a].at[4 * px + 2 * py + pc]
        if half is not None:
            rows = slot.shape[0] // 2
            slot = slot.at[pl.ds(half * rows, rows)]
        return pltpu.make_async_remote_copy(
            src_ref=self.ins[a] if own else slot, dst_ref=slot,
            send_sem=self.send_sems.at[self.COPIES * a + k], recv_sem=self.recv_sems.at[self.COPIES * a + k],
            device_id=to, device_id_type=MESH)

    def _local(self, a):
        return pltpu.make_async_copy(self.ins[a], self.outs[a].at[4 * self.x + 2 * self.y + self.c],
                                     self.local_sems.at[a])

    def _plan(self, a, c):
        x, y = self.x, self.y
        me, sibling = (x, y, c), (x, y, 1 - c)
        xn, yn, dg = (1 - x, y, c), (x, 1 - y, c), (1 - x, 1 - y, c)
        return [
            self._copy(a, 0, me, sibling, own=True), self._copy(a, 1, me, xn, own=True),
            self._copy(a, 2, me, yn, own=True),
            self._copy(a, 3, xn, yn, half=0), self._copy(a, 4, yn, xn, half=1),
            self._copy(a, 5, xn, sibling), self._copy(a, 6, yn, sibling),
            self._copy(a, 7, dg, sibling, half=0), self._copy(a, 8, dg, sibling, half=1),
        ]

    def _arrivals(self, a):
        x, y, c = self.x, self.y, self.c
        me = (x, y, c)
        xn, yn, dg = (1 - x, y, c), (x, 1 - y, c), (1 - x, 1 - y, c)
        other = 1 - c
        return [
            self._copy(a, 0, (x, y, other), me), self._copy(a, 1, xn, me), self._copy(a, 2, yn, me),
            self._copy(a, 3, dg, me, half=0), self._copy(a, 4, dg, me, half=1),
            self._copy(a, 5, (1 - x, y, other), me), self._copy(a, 6, (x, 1 - y, other), me),
            self._copy(a, 7, (1 - x, 1 - y, other), me, half=0), self._copy(a, 8, (1 - x, 1 - y, other), me, half=1),
        ]

    def start(self):
        for a in range(len(self.ins)):
            self._local(a).start()
            for cp in self._plan(a, self.c)[:3]:
                cp.start()

    def forward(self):
        for a in range(len(self.ins)):
            sends, lands = self._plan(a, self.c), self._arrivals(a)
            lands[1].wait_recv()
            sends[3].start()
            sends[5].start()
            lands[2].wait_recv()
            sends[4].start()
            sends[6].start()

    def finish(self):
        n = len(self.ins)
        for a in range(n):
            sends, lands = self._plan(a, self.c), self._arrivals(a)
            lands[3].wait_recv()
            sends[7].start()
            lands[4].wait_recv()
            sends[8].start()
        for a in range(n):
            lands = self._arrivals(a)
            for k in (0, 5, 6, 7, 8):
                lands[k].wait_recv()
        for a in range(n):
            for cp in self._plan(a, self.c):
                cp.wait_send()
            self._local(a).wait()


def _adamw_all(name, ws, gs, ms, vs):
    n = len(ws)
    per_layer = [isinstance(g, tuple) for g in gs]
    flat_g = [part for g in gs for part in (g if isinstance(g, tuple) else (g,))]

    def body(*refs):
        w, refs = refs[:n], refs[n:]
        g, refs = refs[:len(flat_g)], refs[len(flat_g):]
        m, v, outs = refs[:n], refs[n:2 * n], refs[2 * n:]
        stacked = iter(outs[3 * n:])
        parts = iter(g)
        for a in range(n):
            if per_layer[a]:
                whole = next(stacked)
                for t in range(len(gs[a])):
                    grad = next(parts)[...]
                    whole[t] = grad
                    outs[a][t], outs[n + a][t], outs[2 * n + a][t] = _adamw(w[a][t], grad, m[a][t], v[a][t])
            else:
                outs[a][...], outs[n + a][...], outs[2 * n + a][...] = _adamw(
                    w[a][...], next(parts)[...], m[a][...], v[a][...])

    shapes = [jax.ShapeDtypeStruct(a.shape, F32) for a in ws]
    vmem = pl.BlockSpec(memory_space=pltpu.VMEM)
    n_out = 3 * n + sum(per_layer)
    res = pl.pallas_call(
        body, name=name, in_specs=[vmem] * (3 * n + len(flat_g)), out_specs=[vmem] * n_out,
        out_shape=shapes * 3 + [s for s, p in zip(shapes, per_layer) if p],
        compiler_params=pltpu.CompilerParams(vmem_limit_bytes=VMEM_LIMIT),
    )(*ws, *flat_g, *ms, *vs)
    stacked = iter(res[3 * n:])
    return [(next(stacked) if per_layer[a] else gs[a], res[a], res[n + a], res[2 * n + a]) for a in range(n)]


def _all_reduce_small(rows, gain_parts):
    def body(rows_ref, dqg_ref, dkg_ref, out_ref, buf, send_sems, recv_sems):
        x, y, c = _place()
        me = 4 * x + 2 * y + c
        buf[0] = rows_ref[...]
        for row, part in ((4, dqg_ref), (5, dkg_ref)):
            both = jnp.sum(part[...].reshape(HEADS // 2, LANES), axis=0, keepdims=True)
            buf[0, row:row + 1, 0:HEAD_DIM] = both[:, :HEAD_DIM] + both[:, HEAD_DIM:]
        copies = []
        for r in range(1, N_DEV):
            bx, by, bc = (r >> 2) & 1, (r >> 1) & 1, r & 1
            to = (x ^ bx, y ^ by, c ^ bc)
            copies.append(pltpu.make_async_remote_copy(
                src_ref=buf.at[0], dst_ref=buf.at[r], send_sem=send_sems.at[r - 1], recv_sem=recv_sems.at[r - 1],
                device_id=to, device_id_type=MESH))
        for cp in copies:
            cp.start()
        for cp in copies:
            cp.wait_recv()
        for cp in copies:
            cp.wait_send()
        tot = buf[me]
        for j in range(1, N_DEV):
            tot = tot + buf[j ^ me]
        out_ref[...] = tot
        loss = (0.5 / D) * jnp.sum(tot[6:7, :], axis=1, keepdims=True)
        out_ref[6:7, :] = jnp.broadcast_to(loss, (1, D))

    vmem = pl.BlockSpec(memory_space=pltpu.VMEM)
    return pl.pallas_call(
        body, name="all_reduce_small", in_specs=[vmem] * 3, out_specs=vmem,
        out_shape=jax.ShapeDtypeStruct((8, D), F32),
        scratch_shapes=[pltpu.VMEM((N_DEV, 8, D), F32), pltpu.SemaphoreType.DMA((N_DEV - 1,)),
                        pltpu.SemaphoreType.DMA((N_DEV - 1,))],
    )(rows, *gain_parts)


def kernel(x, p, a_norm, a_w_in, a_w_group, a_scale, a_w_out, kv_norm, w_kv, k_norm, b_norm, b_w_in, b_q_norm, b_w_out, ple_w, ple_gate_w, loss_target, m_a_norm, m_a_w_in, m_a_w_group, m_a_scale, m_a_w_out, m_kv_norm, m_w_kv, m_k_norm, m_b_norm, m_b_w_in, m_b_q_norm, m_b_w_out, m_ple_w, m_ple_gate_w, v_a_norm, v_a_w_in, v_a_w_group, v_a_scale, v_a_w_out, v_kv_norm, v_w_kv, v_k_norm, v_b_norm, v_b_w_in, v_b_q_norm, v_b_w_out, v_ple_w, v_ple_gate_w):
    xi, yi, ci = _place()
    me = 4 * xi + 2 * yi + ci

    big = {
        "a_w_in": a_w_in.reshape(D, COLS), "a_w_group": a_w_group.reshape(N_GROUPS * GROUP_ROWS, GROUP_DIM),
        "a_w_out": a_w_out.reshape(ROWS, D), "w_kv": w_kv, "b_w_in": b_w_in.reshape(D, COLS),
        "b_w_out": b_w_out.reshape(ROWS, D), "ple_w": ple_w, "ple_gate_w": ple_gate_w,
    }
    names = ["a_w_in", "a_w_group", "a_w_out", "w_kv", "b_w_in", "b_w_out", "ple_w0", "ple_w1", "gate0", "gate1"]
    cast = dict(zip(names, _cast_shards(list(big.values()))))
    small = jnp.concatenate([a_norm, a_scale, jnp.zeros((14, ROWS), F32)], axis=0)
    first = ["a_w_in", "a_w_group", "a_w_out", "ple_w0", "gate0"]
    behind_a = ["w_kv", "b_w_in"]
    behind_attn = ["b_w_out", "ple_w1", "gate1"]
    gathered = _all_gather([cast[k] for k in first] + [small])
    full = dict(zip(first, gathered[:-1]))
    small_all = gathered[-1]
    a_norm_f = small_all[:, 0, :].reshape(1, D)
    a_scale_f = small_all[:, 1, :].reshape(1, D)
    w_a_in = full["a_w_in"]
    w_a_out = full["a_w_out"].reshape(D, D)
    w_gate0 = full["gate0"].reshape(D, D)
    w_ple0 = full["ple_w0"]
    w_group = full["a_w_group"].reshape(N_DEV, N_GROUPS, GROUP_ROWS, GROUP_DIM).transpose(1, 0, 2, 3).reshape(
        N_GROUPS, GROUP_DIM, GROUP_DIM)
    kvn, bn = kv_norm.reshape(1, D), b_norm
    kg, qg = k_norm.reshape(1, HEAD_DIM), b_q_norm

    x0, p0, p1, target = x[0], p[0, 0], p[1, 0], loss_target[0]
    h0, z, pooled, mcat, y, x1, e0, gt0, x2, w_kv_f, w_b_in = _layer_a_fwd(
        x0, p0, a_norm_f, a_scale_f, w_a_in, w_group, w_a_out, w_ple0, w_gate0,
        rider=_GatherRider([cast[k] for k in behind_a]))
    hkv, hb, k_all, v_all, q_all, zb = _layer_b_in_fwd(x2, kvn, bn, w_kv_f, w_b_in)
    qg2, kg2 = jnp.concatenate([qg, qg], axis=1), jnp.concatenate([kg, kg], axis=1)
    o, csave, w_b_out, w_ple1, w_gate1 = _attn_fwd(
        q_all, k_all, v_all, qg2, kg2, rider=_GatherRider([cast[k] for k in behind_attn]))
    w_b_out, w_gate1 = w_b_out.reshape(D, D), w_gate1.reshape(D, D)
    yb, x3, e1, gt1, dx4, sq_err = _layer_b_out_fwd(o, zb, x2, p1, target, w_b_out, w_ple1, w_gate1)

    de1, dgp1, dx3, d_o, dzb = _layer_b_out_bwd(dx4, e1, gt1, o, zb, w_gate1, w_b_out)
    partial = {
        "b_w_out": _wgrad(yb, dx3, "wgrad_b_w_out").reshape(N_DEV, ROWS, D),
        "ple_w1": _wgrad(p1, de1, "wgrad_ple_w1", n_split=8),
        "gate1": _wgrad(x3, dgp1, "wgrad_gate1").reshape(N_DEV, ROWS, D),
    }
    grad = {}
    dq, dk, dv, dqg, dkg, grad["b_w_out"], grad["ple_w1"], grad["gate1"] = _attn_bwd(
        q_all, k_all, v_all, qg2, kg2, d_o, csave,
        rider=_ReduceRider([partial[k] for k in ("b_w_out", "ple_w1", "gate1")]))
    partial["w_kv"] = _wgrad(hkv, [dk, dv], "wgrad_w_kv", n_split=8)
    partial["b_w_in"] = _wgrad(hb, [dq, dzb], "wgrad_b_w_in", n_split=8)
    dx2, d_bn, d_kvn, grad["w_kv"] = _layer_b_in_bwd(
        dq, dzb, dk, dv, x2, dx3, w_b_in, w_kv_f, bn, kvn, rider=_ReduceRider([partial["w_kv"]]))
    de0, dgp0, dx1, dm, duz, d_as, grad["b_w_in"] = _layer_a_out_bwd(
        dx2, e0, gt0, z, mcat, w_gate0, w_a_out, a_scale_f, w_group, rider=_ReduceRider([partial["b_w_in"]]))
    partial["gate0"] = _wgrad(x1, dgp0, "wgrad_gate0").reshape(N_DEV, ROWS, D)
    partial["a_w_out"] = _wgrad(y, dx1, "wgrad_a_w_out").reshape(N_DEV, ROWS, D)
    partial["a_w_in"], dw_group, partial["ple_w0"], grad["gate0"], grad["a_w_out"] = _wgrad_layer_a_in(
        h0, duz, pooled, dm, p0, de0, rider=_ReduceRider([partial["gate0"], partial["a_w_out"]]))
    partial["a_w_group"] = dw_group.reshape(N_GROUPS, N_DEV, GROUP_ROWS, GROUP_DIM).transpose(1, 0, 2, 3).reshape(
        N_DEV, N_GROUPS * GROUP_ROWS, GROUP_DIM)
    behind_a_in = ["a_w_in", "a_w_group", "ple_w0"]
    dx0, d_an, *done = _layer_a_in_bwd(duz, x0, dx1, w_a_in, a_norm_f,
                                      rider=_ReduceRider([partial[k] for k in behind_a_in]))
    grad.update(zip(behind_a_in, done))

    given = {
        "a_w_in": (a_w_in, m_a_w_in, v_a_w_in), "a_w_group": (a_w_group, m_a_w_group, v_a_w_group),
        "a_w_out": (a_w_out, m_a_w_out, v_a_w_out), "w_kv": (w_kv, m_w_kv, v_w_kv),
        "b_w_in": (b_w_in, m_b_w_in, v_b_w_in), "b_w_out": (b_w_out, m_b_w_out, v_b_w_out),
        "ple_w": (ple_w, m_ple_w, v_ple_w), "ple_gate_w": (ple_gate_w, m_ple_gate_w, v_ple_gate_w),
    }
    grad["ple_w"] = (grad["ple_w0"], grad["ple_w1"])
    grad["ple_gate_w"] = (grad["gate0"], grad["gate1"])
    updated = _adamw_all(
        "adamw_shards", list(big.values()), [grad[k] for k in big],
        [given[k][1].reshape(big[k].shape) for k in big], [given[k][2].reshape(big[k].shape) for k in big])
    res = {k: tuple(t.reshape(given[k][0].shape) for t in four) for k, four in zip(big, updated)}

    rows = jnp.concatenate([d_kvn, d_bn, d_an, d_as, jnp.zeros((2, D), F32), sq_err, jnp.zeros((1, D), F32)], axis=0)
    tot = _all_reduce_small(rows, (dqg, dkg))
    loss = tot[6, 0]
    small_grad = {
        "kv_norm": tot[0:1], "b_norm": tot[1:2],
        "a_norm": lax.dynamic_slice_in_dim(tot[2:3], me * ROWS, ROWS, axis=1),
        "a_scale": lax.dynamic_slice_in_dim(tot[3:4], me * ROWS, ROWS, axis=1),
        "b_q_norm": tot[4:5, :HEAD_DIM], "k_norm": tot[5:6, :HEAD_DIM],
    }
    small_given = {
        "a_norm": (a_norm, m_a_norm, v_a_norm), "a_scale": (a_scale, m_a_scale, v_a_scale),
        "kv_norm": (kv_norm, m_kv_norm, v_kv_norm), "k_norm": (k_norm, m_k_norm, v_k_norm),
        "b_norm": (b_norm, m_b_norm, v_b_norm), "b_q_norm": (b_q_norm, m_b_q_norm, v_b_q_norm),
    }
    rows_of = {k: [t.reshape(1, -1) for t in three] for k, three in small_given.items()}
    updated = _adamw_all(
        "adamw_gains", [rows_of[k][0] for k in small_given], [small_grad[k] for k in small_given],
        [rows_of[k][1] for k in small_given], [rows_of[k][2] for k in small_given])
    res.update({k: tuple(t.reshape(small_given[k][0].shape) for t in four) for k, four in zip(small_given, updated)})

    order = ["a_norm", "a_w_in", "a_w_group", "a_scale", "a_w_out", "kv_norm", "w_kv", "k_norm", "b_norm",
             "b_w_in", "b_q_norm", "b_w_out", "ple_w", "ple_gate_w"]
    outs = [res[k][kind] for kind in range(4) for k in order]
    return (loss, dx0.reshape(x.shape), *outs)
```

```python
import jax
import jax.numpy as jnp
from jax import lax
from jax.experimental import pallas as pl
from jax.experimental.pallas import tpu as pltpu

F32 = jnp.float32
BF16 = jnp.bfloat16
MESH = pl.DeviceIdType.MESH

N_DEV = 8
D = 1024
N_GROUPS = 4
GROUP_DIM = D // N_GROUPS
HALO = 16
HEADS = 16
HEAD_DIM = D // HEADS
SB_SCALE = HEAD_DIM ** -0.5
TILE = 256
LANES = 128
DEAD_LOG = -120.0
EPS = 1e-6
ADAM_LR = 0.001
ADAM_B1 = 0.9
ADAM_B2 = 0.999
ADAM_EPS = 1e-08
ADAM_WD = 0.01
ADAM_STEP = 10
TM = 256
COLS = 2 * D // N_DEV
ROWS = D // N_DEV
GROUP_ROWS = GROUP_DIM // N_DEV
VMEM_LIMIT = 56 * 1024 * 1024

HBM_SPEC = pl.BlockSpec(memory_space=pltpu.HBM)


def _dot(a, b):
    return jnp.dot(a, b, preferred_element_type=F32)


def _dot_nt(a, b):
    return lax.dot_general(a, b, (((1,), (1,)), ((), ())), preferred_element_type=F32)


def _dot_tn(a, b):
    return lax.dot_general(a, b, (((0,), (0,)), ((), ())), preferred_element_type=F32)


def _sigmoid(x):
    return jax.nn.sigmoid(x)


def _split_dot(x, mat):
    hi = x.astype(BF16)
    lo = (x - hi.astype(F32)).astype(BF16)
    return _dot(hi, mat) + _dot(lo, mat)


def _split_dot_many(xs, mat):
    rows = xs[0].shape[0]
    his = [x.astype(BF16) for x in xs]
    los = [(x - hi.astype(F32)).astype(BF16) for x, hi in zip(xs, his)]
    out = _dot(jnp.concatenate(his + los, axis=0), mat)
    n = len(xs)
    return [out[i * rows:(i + 1) * rows] + out[(n + i) * rows:(n + i + 1) * rows] for i in range(n)]


def _stack_dot(xs, mat):
    rows = xs[0].shape[0]
    out = _dot(jnp.concatenate([x.astype(BF16) for x in xs], axis=0), mat)
    return [out[i * rows:(i + 1) * rows] for i in range(len(xs))]


def _rms(x):
    return lax.rsqrt(jnp.mean(x * x, axis=-1, keepdims=True) + EPS)


def _hosted_call(body, name, n_steps, ins, in_specs, out_specs, out_shape, scratch, rider=None):
    ins, scratch = list(ins), list(scratch)
    if rider is None:
        wrapped, extra_in, extra_out, extra_scratch = body, [], [], []
    else:
        extra_in, extra_out, extra_scratch = rider.arrays, rider.out_shape(), rider.scratch()
        n_in, n_out, n_scr = len(ins), len(out_shape), len(scratch)
        k_in, k_out = len(extra_in), len(extra_out)

        def wrapped(*refs):
            own_in, r_in = refs[:n_in], refs[n_in:n_in + k_in]
            own_out = refs[n_in + k_in:n_in + k_in + n_out]
            r_out = refs[n_in + k_in + n_out:n_in + k_in + n_out + k_out]
            rest = refs[n_in + k_in + n_out + k_out:]
            phases = rider.bind(r_in, r_out, rest[n_scr:])
            step = pl.program_id(0)
            pl.when(step == 0)(phases[0])
            body(*own_in, *own_out, *rest[:n_scr])
            at = 0
            for share, phase in zip(rider.WHEN[1:], phases[1:]):
                at = min(n_steps - 1, max(at + 1, round(share * (n_steps - 1))))
                pl.when(step == at)(phase)

    return pl.pallas_call(
        wrapped, name=name, grid=(n_steps,),
        in_specs=list(in_specs) + [HBM_SPEC] * len(extra_in),
        out_specs=list(out_specs) + [HBM_SPEC] * len(extra_out),
        out_shape=list(out_shape) + list(extra_out), scratch_shapes=scratch + list(extra_scratch),
        compiler_params=pltpu.CompilerParams(dimension_semantics=("arbitrary",), vmem_limit_bytes=VMEM_LIMIT),
    )(*ins, *extra_in)


def _rows_call(body, name, n_rows, row_ins, const_ins, row_outs, const_outs=(), scratch=(),
               reverse=False, tm=TM, rider=None):
    nb = n_rows // tm

    def row_map(i):
        return ((nb - 1 - i) if reverse else i, 0)

    def const_map(nd):
        return lambda i: (0,) * nd

    in_specs = [pl.BlockSpec((tm, a.shape[1]), row_map) for a in row_ins]
    in_specs += [pl.BlockSpec(a.shape, const_map(a.ndim)) for a in const_ins]
    out_specs = [pl.BlockSpec((tm, w), row_map) for (w, _) in row_outs]
    out_specs += [pl.BlockSpec(s, const_map(len(s))) for (s, _) in const_outs]
    out_shape = [jax.ShapeDtypeStruct((n_rows, w), dt) for (w, dt) in row_outs]
    out_shape += [jax.ShapeDtypeStruct(s, dt) for (s, dt) in const_outs]
    return _hosted_call(body, name, nb, list(row_ins) + list(const_ins), in_specs, out_specs, out_shape,
                        scratch, rider)


def _ple_fwd(p_ref, xin, wple_ref, wgate_ref, e_ref, gt_ref):
    pb = p_ref[...].astype(BF16)
    for j in range(N_DEV):
        e_ref[:, j * ROWS:(j + 1) * ROWS] = _dot(pb, wple_ref[j])
    gt = _sigmoid(_dot(xin.astype(BF16), wgate_ref[...]))
    gt_ref[...] = gt
    return xin + e_ref[...] * gt


def _layer_a_fwd(x0, p0, a_norm, a_scale, w_in, w_group, w_out, w_ple, w_gate, rider=None):
    s = x0.shape[0]
    tm = TM

    def body(x_ref, p_ref, an_ref, as_ref, win_ref, wg_ref, wout_ref, wple_ref, wgate_ref,
             h_ref, z_ref, pooled_ref, m_ref, y_ref, x1_ref, e_ref, gt_ref, x2_ref, uext):
        i = pl.program_id(0)

        @pl.when(i == 0)
        def _():
            uext[0:HALO, :] = jnp.zeros((HALO, D), F32)

        x = x_ref[...]
        h = (x * _rms(x) * an_ref[...]).astype(BF16)
        h_ref[...] = h
        for j in range(N_DEV):
            uz = _dot(h, win_ref[j])
            if j < 4:
                uext[HALO:HALO + tm, j * COLS:(j + 1) * COLS] = uz
            else:
                z_ref[:, (j - 4) * COLS:(j - 3) * COLS] = uz
        t = i * tm + lax.broadcasted_iota(jnp.int32, (tm, 1), 0)
        for g in range(N_GROUPS):
            w = 2 ** (g + 1)
            cols = slice(g * GROUP_DIM, (g + 1) * GROUP_DIM)
            ext = uext[:, cols]
            acc = ext
            k = 1
            while k < w:
                acc = acc + pltpu.roll(acc, k, 0)
                k *= 2
            cnt = jnp.minimum(t + 1, w).astype(F32)
            pooled = (acc[HALO:] / cnt - ext[HALO:]).astype(BF16)
            pooled_ref[:, cols] = pooled
            m_ref[:, cols] = _dot(pooled, wg_ref[g])
        uext[0:HALO, :] = uext[tm:tm + HALO, :]
        z = z_ref[...]
        y = (m_ref[...] * as_ref[...] * (z * _sigmoid(z))).astype(BF16)
        y_ref[...] = y
        x1 = x + _dot(y, wout_ref[...])
        x1_ref[...] = x1
        x2_ref[...] = _ple_fwd(p_ref, x1, wple_ref, wgate_ref, e_ref, gt_ref)

    row_outs = [(D, BF16), (D, F32), (D, BF16), (D, F32), (D, BF16), (D, F32), (D, F32), (D, F32), (D, F32)]
    return _rows_call(body, "layer_a_fwd", s, [x0, p0], [a_norm, a_scale, w_in, w_group, w_out, w_ple, w_gate],
                      row_outs, scratch=[pltpu.VMEM((tm + HALO, D), F32)], rider=rider)


def _layer_b_in_fwd(x2, kv_norm, b_norm, w_kv, w_bin):
    s = x2.shape[0]

    def body(x_ref, kvn_ref, bn_ref, wkv_ref, wbin_ref, hkv_ref, hb_ref, k_ref, v_ref, q_ref, zb_ref):
        x = x_ref[...]
        n = x * _rms(x)
        hkv = (n * kvn_ref[...]).astype(BF16)
        hb = (n * bn_ref[...]).astype(BF16)
        hkv_ref[...] = hkv
        hb_ref[...] = hb
        for j in range(N_DEV):
            kv = _dot(hkv, wkv_ref[j])
            qz = _dot(hb, wbin_ref[j])
            if j < 4:
                cols = slice(j * COLS, (j + 1) * COLS)
                k_ref[:, cols] = kv
                q_ref[:, cols] = qz
            else:
                cols = slice((j - 4) * COLS, (j - 3) * COLS)
                v_ref[:, cols] = kv.astype(BF16)
                zb_ref[:, cols] = qz

    row_outs = [(D, BF16), (D, BF16), (D, F32), (D, BF16), (D, F32), (D, F32)]
    return _rows_call(body, "layer_b_in_fwd", s, [x2], [kv_norm, b_norm, w_kv, w_bin], row_outs)


def _tri(after):
    r = lax.broadcasted_iota(jnp.int32, (TILE, TILE), 0)
    c = lax.broadcasted_iota(jnp.int32, (TILE, TILE), 1)
    return jnp.where((r > c) if after else (r < c), 1.0, 0.0).astype(BF16)


def _half_sums(v):
    r = lax.broadcasted_iota(jnp.int32, (LANES, LANES), 0) < HEAD_DIM
    c = lax.broadcasted_iota(jnp.int32, (LANES, LANES), 1) < HEAD_DIM
    same_head = jnp.where(r == c, 1.0, 0.0).astype(BF16)
    return _split_dot(v, same_head)


def _pair_norm(x):
    r = lax.rsqrt(_half_sums(x * x) * (1.0 / HEAD_DIM) + EPS)
    return x * r, r


def _tile_logits(qblk, kblk, diagonal):
    l = _dot_nt(qblk, kblk)
    sp = jnp.maximum(l, 0.0) + jnp.log(1.0 + jnp.exp(-jnp.abs(l)))
    ls = l - sp
    if not diagonal:
        return None, -sp, ls
    mask = lax.broadcasted_iota(jnp.int32, l.shape, 1) < lax.broadcasted_iota(jnp.int32, l.shape, 0)
    return mask, jnp.where(mask, -sp, 0.0), ls


def _attn_fwd(q_all, k_all, v_all, q_gain2, k_gain2, rider=None):
    s = q_all.shape[0]
    nt = s // TILE

    def body(q_ref, k_ref, v_ref, qg_ref, kg_ref, o_ref, c_ref, qs, ks, vs, tri, acc, right, cmat):
        tri[...] = _tri(True)
        lane = lax.broadcasted_iota(jnp.int32, (TILE, LANES), 1)
        qn, _ = _pair_norm(q_ref[...])
        kn, _ = _pair_norm(k_ref[...])
        qsc = (qn * qg_ref[...] * SB_SCALE).astype(BF16)
        ksc = (kn * kg_ref[...]).astype(BF16)
        for hh in range(2):
            sl = slice(hh * HEAD_DIM, (hh + 1) * HEAD_DIM)
            qs[hh] = qsc[:, sl]
            ks[hh] = ksc[:, sl]
            vs[hh] = v_ref[:, sl]

        def tile(qrows, kb, diagonal):
            rows = pl.ds(pl.multiple_of(kb * TILE, TILE), TILE)
            loaded = [(qs[hh, qrows, :], ks[hh, rows, :], vs[hh, rows, :], right[hh], cmat[hh], acc[hh])
                      for hh in range(2)]
            logits = [_tile_logits(q, k, diagonal) for q, k, _, _, _, _ in loaded]
            later = _split_dot_many([lk for _, lk, _ in logits], tri[...])
            results = []
            for (q, k, v, rt, cm, ac), (mask, lk, ls), lt in zip(loaded, logits, later):
                a = jnp.exp(ls + lt + rt)
                if diagonal:
                    a = jnp.where(mask, a, 0.0)
                results.append((ac + _dot(a.astype(BF16), v), jnp.where(lane == kb, rt[:, :LANES], cm),
                                rt + jnp.sum(lk, axis=1, keepdims=True)))
            for hh, (ac, cm, rt) in enumerate(results):
                acc[hh] = ac
                cmat[hh] = cm
                right[hh] = rt

        def diagonal_and_left(qrows, qb):
            here = pl.ds(pl.multiple_of(qb * TILE, TILE), TILE)
            left = pl.ds(pl.multiple_of((qb - 1) * TILE, TILE), TILE)
            q = [qs[hh, qrows, :] for hh in range(2)]
            on_diag = [_tile_logits(q[hh], ks[hh, here, :], True) for hh in range(2)]
            beside = [_tile_logits(q[hh], ks[hh, left, :], False) for hh in range(2)]
            later = _split_dot_many([lk for _, lk, _ in on_diag + beside], tri[...])
            for hh in range(2):
                mask, lk_d, ls_d = on_diag[hh]
                _, lk_l, ls_l = beside[hh]
                a_d = jnp.where(mask, jnp.exp(ls_d + later[hh]), 0.0)
                past_diag = jnp.sum(lk_d, axis=1, keepdims=True)
                a_l = jnp.exp(ls_l + later[2 + hh] + past_diag)
                acc[hh] = _dot(a_d.astype(BF16), vs[hh, here, :]) + _dot(a_l.astype(BF16), vs[hh, left, :])
                cmat[hh] = jnp.where(lane == qb - 1, past_diag, 0.0)
                right[hh] = jnp.broadcast_to(past_diag + jnp.sum(lk_l, axis=1, keepdims=True), (TILE, TILE))

        def q_step(qb, _):
            r0 = pl.multiple_of(qb * TILE, TILE)
            qrows = pl.ds(r0, TILE)

            @pl.when(qb == 0)
            def _():
                acc[...] = jnp.zeros((2, TILE, HEAD_DIM), F32)
                right[...] = jnp.zeros((2, TILE, TILE), F32)
                cmat[...] = jnp.zeros((2, TILE, LANES), F32)
                tile(qrows, qb, True)

            pl.when(qb > 0)(lambda: diagonal_and_left(qrows, qb))

            def live():
                return (jnp.max(right[:, :, :LANES]) > DEAD_LOG).astype(jnp.int32)

            def k_step(c):
                kb = c[0] - 1
                tile(qrows, kb, False)
                return kb, live()

            first, _ = lax.while_loop(lambda c: (c[0] > 0) & (c[1] > 0), k_step, (jnp.maximum(qb - 1, 0), live()))
            for hh in range(2):
                o_ref[qrows, hh * HEAD_DIM:(hh + 1) * HEAD_DIM] = acc[hh]
                c_ref[hh, qrows, :] = jnp.where(lane == LANES - 1, first.astype(F32), cmat[hh])
            return 0

        lax.fori_loop(0, nt, q_step, 0)

    pair = pl.BlockSpec((s, LANES), lambda h: (0, h))
    gain = pl.BlockSpec((1, LANES), lambda h: (0, 0))
    return _hosted_call(
        body, "attn_fwd", HEADS // 2, [q_all, k_all, v_all, q_gain2, k_gain2],
        [pair, pair, pair, gain, gain], [pair, pl.BlockSpec((2, s, LANES), lambda h: (h, 0, 0))],
        [jax.ShapeDtypeStruct((s, D), F32), jax.ShapeDtypeStruct((HEADS, s, LANES), F32)],
        [pltpu.VMEM((2, s, HEAD_DIM), BF16)] * 3
        + [pltpu.VMEM((TILE, TILE), BF16), pltpu.VMEM((2, TILE, HEAD_DIM), F32), pltpu.VMEM((2, TILE, TILE), F32),
           pltpu.VMEM((2, TILE, LANES), F32)], rider)


def _layer_b_out_fwd(o, zb, x2, p1, target, w_out, w_ple, w_gate):
    s = o.shape[0]

    def body(o_ref, zb_ref, x2_ref, p_ref, t_ref, wout_ref, wple_ref, wgate_ref,
             yb_ref, x3_ref, e_ref, gt_ref, dx4_ref, loss_ref):
        zb = zb_ref[...]
        yb = (o_ref[...] * (zb * _sigmoid(zb))).astype(BF16)
        yb_ref[...] = yb
        x3 = x2_ref[...] + _dot(yb, wout_ref[...])
        x3_ref[...] = x3
        x4 = _ple_fwd(p_ref, x3, wple_ref, wgate_ref, e_ref, gt_ref)
        d = x4 - t_ref[...]
        dx4_ref[...] = d * (1.0 / D)

        @pl.when(pl.program_id(0) == 0)
        def _():
            loss_ref[...] = jnp.zeros((1, D), F32)

        loss_ref[...] += jnp.sum(d * d, axis=0, keepdims=True)

    row_outs = [(D, BF16), (D, F32), (D, F32), (D, F32), (D, F32)]
    return _rows_call(body, "layer_b_out_fwd", s, [o, zb, x2, p1, target], [w_out, w_ple, w_gate], row_outs,
                      const_outs=[((1, D), F32)])


def _ple_bwd(dxo, e_ref, gt_ref, wgate_ref, de_ref, dgp_ref):
    e = e_ref[...]
    gt = gt_ref[...]
    de_ref[...] = (dxo * gt).astype(BF16)
    dgp = (dxo * e * gt * (1.0 - gt)).astype(BF16)
    dgp_ref[...] = dgp
    return dxo + _dot_nt(dgp, wgate_ref[...])


def _silu_grads(z):
    sg = _sigmoid(z)
    return z * sg, sg * (1.0 + z * (1.0 - sg))


def _layer_b_out_bwd(dx4, e1, gt1, o, zb, w_gate, w_out):
    s = dx4.shape[0]

    def body(dx4_ref, e_ref, gt_ref, o_ref, zb_ref, wgate_ref, wout_ref,
             de_ref, dgp_ref, dx3_ref, do_ref, dzb_ref):
        dx3 = _ple_bwd(dx4_ref[...], e_ref, gt_ref, wgate_ref, de_ref, dgp_ref)
        dx3_ref[...] = dx3
        dyb = _dot_nt(dx3.astype(BF16), wout_ref[...])
        silu, dsilu = _silu_grads(zb_ref[...])
        do_ref[...] = (dyb * silu).astype(BF16)
        dzb_ref[...] = (dyb * o_ref[...] * dsilu).astype(BF16)

    row_outs = [(D, BF16), (D, BF16), (D, F32), (D, BF16), (D, BF16)]
    return _rows_call(body, "layer_b_out_bwd", s, [dx4, e1, gt1, o, zb], [w_gate, w_out], row_outs)


def _attn_bwd(q_all, k_all, v_all, q_gain2, k_gain2, d_o, csave, rider=None):
    s = q_all.shape[0]
    nt = s // TILE

    def body(q_ref, k_ref, v_ref, qg_ref, kg_ref, do_ref, c_ref,
             dq_ref, dk_ref, dv_ref, dqg_ref, dkg_ref,
             qs, ks, vs, dos, qt, dot_t, tri_a, tri_b, dqa, dkt, dvt, dqb, left):
        tri_a[...] = _tri(True)
        tri_b[...] = _tri(False)
        lane = lax.broadcasted_iota(jnp.int32, (TILE, LANES), 1)
        qn, qr = _pair_norm(q_ref[...])
        kn, kr = _pair_norm(k_ref[...])
        qsc = qn * qg_ref[...] * SB_SCALE
        ksc = (kn * kg_ref[...]).astype(BF16)
        q_t = qsc.T.astype(BF16)
        do_t = do_ref[...].astype(F32).T.astype(BF16)
        for j in range(nt):
            qt[j] = q_t[:, j * TILE:(j + 1) * TILE]
            dot_t[j] = do_t[:, j * TILE:(j + 1) * TILE]
        qsc = qsc.astype(BF16)
        for hh in range(2):
            sl = slice(hh * HEAD_DIM, (hh + 1) * HEAD_DIM)
            qs[hh] = qsc[:, sl]
            ks[hh] = ksc[:, sl]
            vs[hh] = v_ref[:, sl]
            dos[hh] = do_ref[:, sl]

        def tile(qb, qrows, kb, diagonal):
            rows = pl.ds(pl.multiple_of(kb * TILE, TILE), TILE)
            heads = range(2)
            kblk = [ks[hh, rows, :] for hh in heads]
            logits = [_tile_logits(qs[hh, qrows, :], kblk[hh], diagonal) for hh in heads]
            later = _split_dot_many([lk for _, lk, _ in logits], tri_a[...])
            a, g = [], []
            for hh in heads:
                mask, _, ls = logits[hh]
                right = jnp.sum(jnp.where(lane == kb, c_ref[hh, qrows, :], 0.0), axis=1, keepdims=True)
                a_h = jnp.exp(ls + later[hh] + right)
                a.append(jnp.where(mask, a_h, 0.0) if diagonal else a_h)
                g.append(a[hh] * _dot_nt(dos[hh, qrows, :], vs[hh, rows, :]))
            before = _stack_dot(g, tri_b[...])
            for hh in heads:
                sl = slice(hh * HEAD_DIM, (hh + 1) * HEAD_DIM)
                mask, _, ls = logits[hh]
                beta = jnp.exp(ls)
                lf = left[hh]
                dl = g[hh] * (1.0 - beta) - (before[hh] + lf) * beta
                if diagonal:
                    dl = jnp.where(mask, dl, 0.0)
                dl = dl.astype(BF16)
                left[hh] = lf + jnp.sum(g[hh], axis=1, keepdims=True)
                dqb[hh] += _dot(dl, kblk[hh])
                dk_t, dv_t = _dot(qt[qb, sl, :], dl), _dot(dot_t[qb, sl, :], a[hh].astype(BF16))
                if diagonal:
                    dkt[kb, sl, :] = dk_t
                    dvt[kb, sl, :] = dv_t
                else:
                    dkt[kb, sl, :] += dk_t
                    dvt[kb, sl, :] += dv_t

        def left_and_diagonal(qb, qrows):
            here = pl.ds(pl.multiple_of(qb * TILE, TILE), TILE)
            beside = pl.ds(pl.multiple_of((qb - 1) * TILE, TILE), TILE)
            heads = range(2)
            q = [qs[hh, qrows, :] for hh in heads]
            do = [dos[hh, qrows, :] for hh in heads]
            k_d, k_l = [ks[hh, here, :] for hh in heads], [ks[hh, beside, :] for hh in heads]
            on_diag = [_tile_logits(q[hh], k_d[hh], True) for hh in heads]
            on_left = [_tile_logits(q[hh], k_l[hh], False) for hh in heads]
            later = _split_dot_many([lk for _, lk, _ in on_diag + on_left], tri_a[...])
            a_d, a_l, g_d, g_l = [], [], [], []
            for hh in heads:
                mask, lk_d, ls_d = on_diag[hh]
                a_d.append(jnp.where(mask, jnp.exp(ls_d + later[hh]), 0.0))
                a_l.append(jnp.exp(on_left[hh][2] + later[2 + hh] + jnp.sum(lk_d, axis=1, keepdims=True)))
                g_d.append(a_d[hh] * _dot_nt(do[hh], vs[hh, here, :]))
                g_l.append(a_l[hh] * _dot_nt(do[hh], vs[hh, beside, :]))
            before = _stack_dot(g_l + g_d, tri_b[...])
            for hh in heads:
                sl = slice(hh * HEAD_DIM, (hh + 1) * HEAD_DIM)
                beta_l, beta_d = jnp.exp(on_left[hh][2]), jnp.exp(on_diag[hh][2])
                dl_l = (g_l[hh] * (1.0 - beta_l) - before[hh] * beta_l).astype(BF16)
                carried = jnp.sum(g_l[hh], axis=1, keepdims=True)
                dl_d = g_d[hh] * (1.0 - beta_d) - (before[2 + hh] + carried) * beta_d
                dl_d = jnp.where(on_diag[hh][0], dl_d, 0.0).astype(BF16)
                dqa[qrows, sl] = (_dot(dl_l, k_l[hh]) + _dot(dl_d, k_d[hh])) * SB_SCALE
                dkt[qb - 1, sl, :] += _dot(qt[qb, sl, :], dl_l)
                dkt[qb, sl, :] = _dot(qt[qb, sl, :], dl_d)
                dvt[qb - 1, sl, :] += _dot(dot_t[qb, sl, :], a_l[hh].astype(BF16))
                dvt[qb, sl, :] = _dot(dot_t[qb, sl, :], a_d[hh].astype(BF16))

        def q_step(qb, _):
            qrows = pl.ds(pl.multiple_of(qb * TILE, TILE), TILE)
            first = jnp.max(jnp.where(lane == LANES - 1, c_ref[0, qrows, :], 0.0)).astype(jnp.int32)
            usual = (qb > 0) & (first == qb - 1)

            @pl.when(usual)
            def _():
                left_and_diagonal(qb, qrows)

            @pl.when(jnp.logical_not(usual))
            def _():
                dqb[...] = jnp.zeros((2, TILE, HEAD_DIM), F32)
                left[...] = jnp.zeros((2, TILE, TILE), F32)

                def k_step(kb, _):
                    tile(qb, qrows, kb, False)
                    return 0

                lax.fori_loop(first, qb, k_step, 0)
                tile(qb, qrows, qb, True)
                for hh in range(2):
                    dqa[qrows, hh * HEAD_DIM:(hh + 1) * HEAD_DIM] = dqb[hh] * SB_SCALE

            return 0

        lax.fori_loop(0, nt, q_step, 0)

        def norm_bwd(dy, xn, r, g_ref, dx_ref, dg_ref):
            dg_ref[...] = jnp.sum(dy * xn, axis=0, keepdims=True)
            dxn = dy * g_ref[...]
            dx_ref[...] = (r * (dxn - xn * (_half_sums(dxn * xn) * (1.0 / HEAD_DIM)))).astype(BF16)

        norm_bwd(dqa[...], qn, qr, qg_ref, dq_ref, dqg_ref)
        for j in range(nt):
            dqa[j * TILE:(j + 1) * TILE, :] = dkt[j].T
            dv_ref[j * TILE:(j + 1) * TILE, :] = dvt[j].T.astype(BF16)
        norm_bwd(dqa[...], kn, kr, kg_ref, dk_ref, dkg_ref)

    pair = pl.BlockSpec((s, LANES), lambda h: (0, h))
    gain = pl.BlockSpec((1, LANES), lambda h: (0, 0))
    dgain = pl.BlockSpec((None, 1, LANES), lambda h: (h, 0, 0))
    return _hosted_call(
        body, "attn_bwd", HEADS // 2, [q_all, k_all, v_all, q_gain2, k_gain2, d_o, csave],
        [pair, pair, pair, gain, gain, pair, pl.BlockSpec((2, s, LANES), lambda h: (h, 0, 0))],
        [pair, pair, pair, dgain, dgain],
        [jax.ShapeDtypeStruct((s, D), BF16)] * 3 + [jax.ShapeDtypeStruct((HEADS // 2, 1, LANES), F32)] * 2,
        [pltpu.VMEM((2, s, HEAD_DIM), BF16)] * 4
        + [pltpu.VMEM((nt, LANES, TILE), BF16)] * 2 + [pltpu.VMEM((TILE, TILE), BF16)] * 2
        + [pltpu.VMEM((s, LANES), F32)] + [pltpu.VMEM((nt, LANES, TILE), F32)] * 2
        + [pltpu.VMEM((2, TILE, HEAD_DIM), F32), pltpu.VMEM((2, TILE, TILE), F32)], rider)


def _norm_bwd_rows(dh, x, gain, dgain_ref):
    r = _rms(x)
    n = x * r
    dgain_ref[...] += jnp.sum(dh * n, axis=0, keepdims=True)
    dn = dh * gain
    return r * (dn - n * jnp.mean(dn * n, axis=-1, keepdims=True))


def _layer_b_in_bwd(dq, dzb, dk, dv, x2, dx3, w_bin, w_kv, b_norm, kv_norm, rider=None):
    s = x2.shape[0]

    def body(dq_ref, dzb_ref, dk_ref, dv_ref, x_ref, dx3_ref, wbin_ref, wkv_ref, bn_ref, kvn_ref,
             dx2_ref, dbn_ref, dkvn_ref):
        @pl.when(pl.program_id(0) == 0)
        def _():
            dbn_ref[...] = jnp.zeros((1, D), F32)
            dkvn_ref[...] = jnp.zeros((1, D), F32)

        dhb = jnp.zeros((TM, D), F32)
        dhkv = jnp.zeros((TM, D), F32)
        for j in range(N_DEV):
            cols = slice((j % 4) * COLS, (j % 4 + 1) * COLS)
            dhb = dhb + _dot_nt((dq_ref if j < 4 else dzb_ref)[:, cols], wbin_ref[j])
            dhkv = dhkv + _dot_nt((dk_ref if j < 4 else dv_ref)[:, cols], wkv_ref[j])
        x = x_ref[...]
        dx2 = dx3_ref[...] + _norm_bwd_rows(dhb, x, bn_ref[...], dbn_ref)
        dx2_ref[...] = dx2 + _norm_bwd_rows(dhkv, x, kvn_ref[...], dkvn_ref)

    return _rows_call(body, "layer_b_in_bwd", s, [dq, dzb, dk, dv, x2, dx3], [w_bin, w_kv, b_norm, kv_norm],
                      [(D, F32)], const_outs=[((1, D), F32), ((1, D), F32)], rider=rider)


def _layer_a_out_bwd(dx2, e0, gt0, z, m, w_gate, w_out, a_scale, w_group, rider=None):
    s = dx2.shape[0]
    tm = TM
    nb = s // tm

    def body(dx2_ref, e_ref, gt_ref, z_ref, m_ref, wgate_ref, wout_ref, as_ref, wg_ref,
             de_ref, dgp_ref, dx1_ref, dm_ref, duz_ref, das_ref, ext):
        i = pl.program_id(0)

        @pl.when(i == 0)
        def _():
            das_ref[...] = jnp.zeros((1, D), F32)
            ext[tm:tm + HALO, :] = jnp.zeros((HALO, D), F32)

        dx1 = _ple_bwd(dx2_ref[...], e_ref, gt_ref, wgate_ref, de_ref, dgp_ref)
        dx1_ref[...] = dx1
        dy = _dot_nt(dx1.astype(BF16), wout_ref[...])
        silu, dsilu = _silu_grads(z_ref[...])
        m = m_ref[...]
        dmixed = dy * silu
        duz_ref[:, D:] = (dy * (m * as_ref[...]) * dsilu).astype(BF16)
        das_ref[...] += jnp.sum(dmixed * m, axis=0, keepdims=True)
        dm_ref[...] = (dmixed * as_ref[...]).astype(BF16)
        t = (nb - 1 - i) * tm + lax.broadcasted_iota(jnp.int32, (tm, 1), 0)
        n_ext = tm + HALO
        for g in range(N_GROUPS):
            w = 2 ** (g + 1)
            cols = slice(g * GROUP_DIM, (g + 1) * GROUP_DIM)
            dpool = _dot_nt(dm_ref[:, cols], wg_ref[g])
            ext[0:tm, cols] = dpool / jnp.minimum(t + 1, w).astype(F32)
            acc = ext[:, cols]
            k = 1
            while k < w:
                acc = acc + pltpu.roll(acc, n_ext - k, 0)
                k *= 2
            duz_ref[:, cols] = (acc[:tm] - dpool).astype(BF16)
        ext[tm:tm + HALO, :] = ext[0:HALO, :]

    row_outs = [(D, BF16), (D, BF16), (D, F32), (D, BF16), (2 * D, BF16)]
    return _rows_call(body, "layer_a_out_bwd", s, [dx2, e0, gt0, z, m], [w_gate, w_out, a_scale, w_group],
                      row_outs, const_outs=[((1, D), F32)], scratch=[pltpu.VMEM((tm + HALO, D), F32)],
                      reverse=True, rider=rider)


def _layer_a_in_bwd(duz, x0, dx1, w_in, a_norm, rider=None):
    s = x0.shape[0]

    def body(duz_ref, x_ref, dx1_ref, win_ref, an_ref, dx0_ref, dan_ref):
        @pl.when(pl.program_id(0) == 0)
        def _():
            dan_ref[...] = jnp.zeros((1, D), F32)

        dh = jnp.zeros((TM, D), F32)
        for j in range(N_DEV):
            dh = dh + _dot_nt(duz_ref[:, j * COLS:(j + 1) * COLS], win_ref[j])
        dx0_ref[...] = dx1_ref[...] + _norm_bwd_rows(dh, x_ref[...], an_ref[...], dan_ref)

    return _rows_call(body, "layer_a_in_bwd", s, [duz, x0, dx1], [w_in, a_norm], [(D, F32)],
                      const_outs=[((1, D), F32)], rider=rider)


def _wgrad(a, b, name, n_split=1, rider=None):
    bs = list(b) if isinstance(b, (list, tuple)) else [b]
    s, k = a.shape
    n = sum(part.shape[1] for part in bs)
    tk = TM
    nb = n // n_split

    def body(a_ref, *refs):
        o_ref = refs[-1]
        lhs = a_ref[...].astype(BF16)
        done = 0
        for b_ref in refs[:-1]:
            res = _dot_tn(lhs, b_ref[...].astype(BF16))
            if n_split == 1:
                o_ref[...] = res.astype(BF16)
            else:
                for j in range(res.shape[1] // nb):
                    o_ref[done + j] = res[:, j * nb:(j + 1) * nb].astype(BF16)
                done += res.shape[1] // nb

    if n_split == 1:
        b_specs = [pl.BlockSpec((s, n), lambda i: (0, 0))]
        out_spec = pl.BlockSpec((tk, n), lambda i: (i, 0))
        out_shape = jax.ShapeDtypeStruct((k, n), BF16)
    else:
        b_specs = [pl.BlockSpec(part.shape, lambda i: (0, 0)) for part in bs]
        out_spec = pl.BlockSpec((n_split, tk, nb), lambda i: (0, i, 0))
        out_shape = jax.ShapeDtypeStruct((n_split, k, nb), BF16)
    res = _hosted_call(body, name, k // tk, [a] + bs, [pl.BlockSpec((s, tk), lambda i: (0, i))] + b_specs,
                       [out_spec], [out_shape], [], rider)
    return res[0] if rider is None else res


def _wgrad_layer_a_in(h0, duz, pooled, dm, p0, de0, rider):
    s = h0.shape[0]
    n_ple = p0.shape[1]

    def body(h_ref, duz_ref, pooled_ref, dm_ref, p_ref, de_ref, in_ref, group_ref, ple_ref):
        res = _dot_tn(h_ref[...], duz_ref[...])
        for j in range(N_DEV):
            in_ref[j] = res[:, j * COLS:(j + 1) * COLS].astype(BF16)
        group_ref[...] = _dot_tn(pooled_ref[...], dm_ref[...]).astype(BF16)

        @pl.when(pl.program_id(0) == 0)
        def _():
            ple = _dot_tn(p_ref[...].astype(BF16), de_ref[...])
            for j in range(N_DEV):
                ple_ref[j] = ple[:, j * ROWS:(j + 1) * ROWS].astype(BF16)

    block = pl.BlockSpec((s, TM), lambda i: (0, i))
    return _hosted_call(
        body, "wgrad_layer_a_in", D // TM, [h0, duz, pooled, dm, p0, de0],
        [block, pl.BlockSpec(duz.shape, lambda i: (0, 0)), block, block,
         pl.BlockSpec(p0.shape, lambda i: (0, 0)), pl.BlockSpec(de0.shape, lambda i: (0, 0))],
        [pl.BlockSpec((N_DEV, TM, COLS), lambda i: (0, i, 0)), pl.BlockSpec((None, TM, GROUP_DIM), lambda i: (i, 0, 0)),
         pl.BlockSpec((N_DEV, n_ple, ROWS), lambda i: (0, 0, 0))],
        [jax.ShapeDtypeStruct((N_DEV, D, COLS), BF16), jax.ShapeDtypeStruct((N_GROUPS, GROUP_DIM, GROUP_DIM), BF16),
         jax.ShapeDtypeStruct((N_DEV, n_ple, ROWS), BF16)], [], rider)


def _cast_shards(shards):
    n = len(shards)
    layers = [a.shape[0] if a.ndim == 3 else 0 for a in shards]

    def body(*refs):
        outs = iter(refs[n:])
        for a in range(n):
            if layers[a]:
                for t in range(layers[a]):
                    next(outs)[...] = refs[a][t].astype(BF16)
            else:
                next(outs)[...] = refs[a][...].astype(BF16)

    out_shape = []
    for a, k in zip(shards, layers):
        out_shape += [jax.ShapeDtypeStruct(a.shape[-2:], BF16)] * max(k, 1)
    vmem = pl.BlockSpec(memory_space=pltpu.VMEM)
    return pl.pallas_call(
        body, name="cast_shards", in_specs=[vmem] * n, out_specs=[vmem] * len(out_shape), out_shape=out_shape,
        compiler_params=pltpu.CompilerParams(vmem_limit_bytes=VMEM_LIMIT),
    )(*shards)


def _adamw(w, g, m, v):
    m = ADAM_B1 * m + (1.0 - ADAM_B1) * g
    v = ADAM_B2 * v + (1.0 - ADAM_B2) * jnp.square(g)
    m_hat = m / (1.0 - ADAM_B1 ** ADAM_STEP)
    v_hat = v / (1.0 - ADAM_B2 ** ADAM_STEP)
    delta = -ADAM_LR * (m_hat / (jnp.sqrt(v_hat) + ADAM_EPS) + ADAM_WD * w)
    return delta, m, v


def _place():
    return lax.axis_index("x"), lax.axis_index("y"), lax.axis_index("c")


def _all_gather(shards):
    return _alone("all_gather_weights", _GatherRider(shards))


def _alone(name, rider):
    n_in, n_out = len(rider.arrays), len(rider.out_shape())

    def body(*refs):
        for phase in rider.bind(refs[:n_in], refs[n_in:n_in + n_out], refs[n_in + n_out:]):
            phase()

    return pl.pallas_call(
        body, name=name, in_specs=[HBM_SPEC] * n_in, out_specs=[HBM_SPEC] * n_out,
        out_shape=rider.out_shape(), scratch_shapes=rider.scratch(),
        compiler_params=pltpu.CompilerParams(vmem_limit_bytes=VMEM_LIMIT),
    )(*rider.arrays)


class _GatherRider:
    WHEN = (0.0, 0.7, 1.0)

    def __init__(self, shards):
        self.arrays = list(shards)

    def out_shape(self):
        return _Gather.out_shape(self.arrays)

    def scratch(self):
        return _Gather.semaphores(len(self.arrays))

    def bind(self, ins, outs, scratch):
        moving = _Gather(ins, outs, *scratch)
        return moving.start, moving.forward, moving.finish


class _ReduceRider:
    WHEN = (0.0, 0.15, 0.4, 1.0)

    def __init__(self, partials):
        self.arrays = list(partials)

    def out_shape(self):
        return [jax.ShapeDtypeStruct(a.shape[1:], F32) for a in self.arrays]

    def scratch(self):
        n = len(self.arrays)
        dma = pltpu.SemaphoreType.DMA

        def blocks(k):
            return [pltpu.VMEM((k,) + a.shape[1:], BF16) for a in self.arrays]

        halves = [pltpu.VMEM((2, a.shape[1] // 2) + a.shape[2:], BF16) for a in self.arrays]
        return (blocks(4) + blocks(4) + halves + blocks(2) + [pltpu.VMEM(a.shape[1:], F32) for a in self.arrays]
                + [dma((4 * n,)), dma((4 * n,)), dma((4 * n,)), dma((2 * n,)), dma((2 * n,)),
                   dma((2 * n,)), dma((2 * n,)), dma((n,))])

    def bind(self, ins, outs, scratch):
        n = len(ins)
        mine, landed, halves, arrived, total = (scratch[i * n:(i + 1) * n] for i in range(5))
        send1, recv1, local1, send_h, recv_h, send2, recv2, out_sems = scratch[5 * n:]
        x, y, c = _place()
        plane = 2 * x + y
        via = [(x, 1 - y, c), (1 - x, y, c)]
        nbr = [(1 - x, y, c), (x, 1 - y, c)]
        nbr_block = [2 * (1 - x) + y, 2 * x + (1 - y)]
        diag_block = 2 * (1 - x) + (1 - y)

        def to_sibling(a, k):
            return pltpu.make_async_remote_copy(
                src_ref=ins[a].at[2 * k + (1 - c)], dst_ref=landed[a].at[k],
                send_sem=send1.at[4 * a + k], recv_sem=recv1.at[4 * a + k],
                device_id=(x, y, 1 - c), device_id_type=MESH)

        def own_block(a, k):
            return pltpu.make_async_copy(ins[a].at[2 * k + c], mine[a].at[k], local1.at[4 * a + k])

        def half_of(a, ref, h):
            rows = self.arrays[a].shape[1] // 2
            return ref.at[pl.ds(h * rows, rows)]

        def half_out(a, h):
            return pltpu.make_async_remote_copy(
                src_ref=half_of(a, mine[a].at[diag_block], h), dst_ref=halves[a].at[h],
                send_sem=send_h.at[2 * a + h], recv_sem=recv_h.at[2 * a + h],
                device_id=via[h], device_id_type=MESH)

        def to_owner(a, h):
            return pltpu.make_async_remote_copy(
                src_ref=mine[a].at[nbr_block[h]], dst_ref=arrived[a].at[h],
                send_sem=send2.at[2 * a + h], recv_sem=recv2.at[2 * a + h],
                device_id=nbr[h], device_id_type=MESH)

        def result(a):
            return pltpu.make_async_copy(total[a], outs[a], out_sems.at[a])

        def exchange_cores():
            for a in range(n):
                for k in range(4):
                    to_sibling(a, k).start()
                    own_block(a, k).start()

        def pair_sums():
            for a in range(n):
                for k in range(4):
                    own_block(a, k).wait()
                    to_sibling(a, k).wait_recv()
                total[a][...] = mine[a][plane].astype(F32) + landed[a][plane].astype(F32)
                for k in range(4):
                    mine[a][k] = (mine[a][k].astype(F32) + landed[a][k].astype(F32)).astype(BF16)
                for h in range(2):
                    half_out(a, h).start()

        def fold_and_send():
            for a in range(n):
                rows = self.arrays[a].shape[1] // 2
                for h in range(2):
                    half_out(a, h).wait_recv()
                    part = mine[a].at[nbr_block[h]]
                    span = slice(h * rows, (h + 1) * rows)
                    part[span] = (part[span].astype(F32) + halves[a][h].astype(F32)).astype(BF16)
                    to_owner(a, h).start()

        def finish():
            for a in range(n):
                for h in range(2):
                    to_owner(a, h).wait_recv()
                    total[a][...] += arrived[a][h].astype(F32)
                result(a).start()
            for a in range(n):
                for k in range(4):
                    to_sibling(a, k).wait_send()
                for h in range(2):
                    half_out(a, h).wait_send()
                    to_owner(a, h).wait_send()
                result(a).wait()

        return exchange_cores, pair_sums, fold_and_send, finish


class _Gather:
    COPIES = 9

    def __init__(self, ins, outs, send_sems, recv_sems, local_sems):
        self.ins, self.outs = ins, outs
        self.send_sems, self.recv_sems, self.local_sems = send_sems, recv_sems, local_sems
        self.x, self.y, self.c = _place()

    @staticmethod
    def out_shape(shards):
        return [jax.ShapeDtypeStruct((N_DEV,) + a.shape, a.dtype) for a in shards]

    @staticmethod
    def semaphores(n):
        dma = pltpu.SemaphoreType.DMA
        return [dma((_Gather.COPIES * n,)), dma((_Gather.COPIES * n,)), dma((n,))]

    def _copy(self, a, k, block, to, own=False, half=None):
        px, py, pc = block
        slot = self.outs[a].at[4 * px + 2 * py + pc]
        if half is not None:
            rows = slot.shape[0] // 2
            slot = slot.at[pl.ds(half * rows, rows)]
        return pltpu.make_async_remote_copy(
            src_ref=self.ins[a] if own else slot, dst_ref=slot,
            send_sem=self.send_sems.at[self.COPIES * a + k], recv_sem=self.recv_sems.at[self.COPIES * a + k],
            device_id=to, device_id_type=MESH)

    def _local(self, a):
        return pltpu.make_async_copy(self.ins[a], self.outs[a].at[4 * self.x + 2 * self.y + self.c],
                                     self.local_sems.at[a])

    def _plan(self, a, c):
        x, y = self.x, self.y
        me, sibling = (x, y, c), (x, y, 1 - c)
        xn, yn, dg = (1 - x, y, c), (x, 1 - y, c), (1 - x, 1 - y, c)
        return [
            self._copy(a, 0, me, sibling, own=True), self._copy(a, 1, me, xn, own=True),
            self._copy(a, 2, me, yn, own=True),
            self._copy(a, 3, xn, yn, half=0), self._copy(a, 4, yn, xn, half=1),
            self._copy(a, 5, xn, sibling), self._copy(a, 6, yn, sibling),
            self._copy(a, 7, dg, sibling, half=0), self._copy(a, 8, dg, sibling, half=1),
        ]

    def _arrivals(self, a):
        x, y, c = self.x, self.y, self.c
        me = (x, y, c)
        xn, yn, dg = (1 - x, y, c), (x, 1 - y, c), (1 - x, 1 - y, c)
        other = 1 - c
        return [
            self._copy(a, 0, (x, y, other), me), self._copy(a, 1, xn, me), self._copy(a, 2, yn, me),
            self._copy(a, 3, dg, me, half=0), self._copy(a, 4, dg, me, half=1),
            self._copy(a, 5, (1 - x, y, other), me), self._copy(a, 6, (x, 1 - y, other), me),
            self._copy(a, 7, (1 - x, 1 - y, other), me, half=0), self._copy(a, 8, (1 - x, 1 - y, other), me, half=1),
        ]

    def start(self):
        for a in range(len(self.ins)):
            self._local(a).start()
            for cp in self._plan(a, self.c)[:3]:
                cp.start()

    def forward(self):
        for a in range(len(self.ins)):
            sends, lands = self._plan(a, self.c), self._arrivals(a)
            lands[1].wait_recv()
            sends[3].start()
            sends[5].start()
            lands[2].wait_recv()
            sends[4].start()
            sends[6].start()

    def finish(self):
        n = len(self.ins)
        for a in range(n):
            sends, lands = self._plan(a, self.c), self._arrivals(a)
            lands[3].wait_recv()
            sends[7].start()
            lands[4].wait_recv()
            sends[8].start()
        for a in range(n):
            lands = self._arrivals(a)
            for k in (0, 5, 6, 7, 8):
                lands[k].wait_recv()
        for a in range(n):
            for cp in self._plan(a, self.c):
                cp.wait_send()
            self._local(a).wait()


def _adamw_all(name, ws, gs, ms, vs):
    n = len(ws)
    per_layer = [isinstance(g, tuple) for g in gs]
    flat_g = [part for g in gs for part in (g if isinstance(g, tuple) else (g,))]

    def body(*refs):
        w, refs = refs[:n], refs[n:]
        g, refs = refs[:len(flat_g)], refs[len(flat_g):]
        m, v, outs = refs[:n], refs[n:2 * n], refs[2 * n:]
        stacked = iter(outs[3 * n:])
        parts = iter(g)
        for a in range(n):
            if per_layer[a]:
                whole = next(stacked)
                for t in range(len(gs[a])):
                    grad = next(parts)[...]
                    whole[t] = grad
                    outs[a][t], outs[n + a][t], outs[2 * n + a][t] = _adamw(w[a][t], grad, m[a][t], v[a][t])
            else:
                outs[a][...], outs[n + a][...], outs[2 * n + a][...] = _adamw(
                    w[a][...], next(parts)[...], m[a][...], v[a][...])

    shapes = [jax.ShapeDtypeStruct(a.shape, F32) for a in ws]
    vmem = pl.BlockSpec(memory_space=pltpu.VMEM)
    n_out = 3 * n + sum(per_layer)
    res = pl.pallas_call(
        body, name=name, in_specs=[vmem] * (3 * n + len(flat_g)), out_specs=[vmem] * n_out,
        out_shape=shapes * 3 + [s for s, p in zip(shapes, per_layer) if p],
        compiler_params=pltpu.CompilerParams(vmem_limit_bytes=VMEM_LIMIT),
    )(*ws, *flat_g, *ms, *vs)
    stacked = iter(res[3 * n:])
    return [(next(stacked) if per_layer[a] else gs[a], res[a], res[n + a], res[2 * n + a]) for a in range(n)]


def _all_reduce_small(rows, gain_parts):
    def body(rows_ref, dqg_ref, dkg_ref, out_ref, buf, send_sems, recv_sems):
        x, y, c = _place()
        me = 4 * x + 2 * y + c
        buf[0] = rows_ref[...]
        for row, part in ((4, dqg_ref), (5, dkg_ref)):
            both = jnp.sum(part[...].reshape(HEADS // 2, LANES), axis=0, keepdims=True)
            buf[0, row:row + 1, 0:HEAD_DIM] = both[:, :HEAD_DIM] + both[:, HEAD_DIM:]
        copies = []
        for r in range(1, N_DEV):
            bx, by, bc = (r >> 2) & 1, (r >> 1) & 1, r & 1
            to = (x ^ bx, y ^ by, c ^ bc)
            copies.append(pltpu.make_async_remote_copy(
                src_ref=buf.at[0], dst_ref=buf.at[r], send_sem=send_sems.at[r - 1], recv_sem=recv_sems.at[r - 1],
                device_id=to, device_id_type=MESH))
        for cp in copies:
            cp.start()
        for cp in copies:
            cp.wait_recv()
        for cp in copies:
            cp.wait_send()
        tot = buf[me]
        for j in range(1, N_DEV):
            tot = tot + buf[j ^ me]
        out_ref[...] = tot
        loss = (0.5 / D) * jnp.sum(tot[6:7, :], axis=1, keepdims=True)
        out_ref[6:7, :] = jnp.broadcast_to(loss, (1, D))

    vmem = pl.BlockSpec(memory_space=pltpu.VMEM)
    return pl.pallas_call(
        body, name="all_reduce_small", in_specs=[vmem] * 3, out_specs=vmem,
        out_shape=jax.ShapeDtypeStruct((8, D), F32),
        scratch_shapes=[pltpu.VMEM((N_DEV, 8, D), F32), pltpu.SemaphoreType.DMA((N_DEV - 1,)),
                        pltpu.SemaphoreType.DMA((N_DEV - 1,))],
    )(rows, *gain_parts)


def kernel(x, p, a_norm, a_w_in, a_w_group, a_scale, a_w_out, kv_norm, w_kv, k_norm, b_norm, b_w_in, b_q_norm, b_w_out, ple_w, ple_gate_w, loss_target, m_a_norm, m_a_w_in, m_a_w_group, m_a_scale, m_a_w_out, m_kv_norm, m_w_kv, m_k_norm, m_b_norm, m_b_w_in, m_b_q_norm, m_b_w_out, m_ple_w, m_ple_gate_w, v_a_norm, v_a_w_in, v_a_w_group, v_a_scale, v_a_w_out, v_kv_norm, v_w_kv, v_k_norm, v_b_norm, v_b_w_in, v_b_q_norm, v_b_w_out, v_ple_w, v_ple_gate_w):
    xi, yi, ci = _place()
    me = 4 * xi + 2 * yi + ci

    big = {
        "a_w_in": a_w_in.reshape(D, COLS), "a_w_group": a_w_group.reshape(N_GROUPS * GROUP_ROWS, GROUP_DIM),
        "a_w_out": a_w_out.reshape(ROWS, D), "w_kv": w_kv, "b_w_in": b_w_in.reshape(D, COLS),
        "b_w_out": b_w_out.reshape(ROWS, D), "ple_w": ple_w, "ple_gate_w": ple_gate_w,
    }
    names = ["a_w_in", "a_w_group", "a_w_out", "w_kv", "b_w_in", "b_w_out", "ple_w0", "ple_w1", "gate0", "gate1"]
    cast = dict(zip(names, _cast_shards(list(big.values()))))
    small = jnp.concatenate([a_norm, a_scale, jnp.zeros((14, ROWS), F32)], axis=0)
    first = ["a_w_in", "a_w_group", "a_w_out", "ple_w0", "gate0"]
    behind_a = ["w_kv", "b_w_in"]
    behind_attn = ["b_w_out", "ple_w1", "gate1"]
    gathered = _all_gather([cast[k] for k in first] + [small])
    full = dict(zip(first, gathered[:-1]))
    small_all = gathered[-1]
    a_norm_f = small_all[:, 0, :].reshape(1, D)
    a_scale_f = small_all[:, 1, :].reshape(1, D)
    w_a_in = full["a_w_in"]
    w_a_out = full["a_w_out"].reshape(D, D)
    w_gate0 = full["gate0"].reshape(D, D)
    w_ple0 = full["ple_w0"]
    w_group = full["a_w_group"].reshape(N_DEV, N_GROUPS, GROUP_ROWS, GROUP_DIM).transpose(1, 0, 2, 3).reshape(
        N_GROUPS, GROUP_DIM, GROUP_DIM)
    kvn, bn = kv_norm.reshape(1, D), b_norm
    kg, qg = k_norm.reshape(1, HEAD_DIM), b_q_norm

    x0, p0, p1, target = x[0], p[0, 0], p[1, 0], loss_target[0]
    h0, z, pooled, mcat, y, x1, e0, gt0, x2, w_kv_f, w_b_in = _layer_a_fwd(
        x0, p0, a_norm_f, a_scale_f, w_a_in, w_group, w_a_out, w_ple0, w_gate0,
        rider=_GatherRider([cast[k] for k in behind_a]))
    hkv, hb, k_all, v_all, q_all, zb = _layer_b_in_fwd(x2, kvn, bn, w_kv_f, w_b_in)
    qg2, kg2 = jnp.concatenate([qg, qg], axis=1), jnp.concatenate([kg, kg], axis=1)
    o, csave, w_b_out, w_ple1, w_gate1 = _attn_fwd(
        q_all, k_all, v_all, qg2, kg2, rider=_GatherRider([cast[k] for k in behind_attn]))
    w_b_out, w_gate1 = w_b_out.reshape(D, D), w_gate1.reshape(D, D)
    yb, x3, e1, gt1, dx4, sq_err = _layer_b_out_fwd(o, zb, x2, p1, target, w_b_out, w_ple1, w_gate1)

    de1, dgp1, dx3, d_o, dzb = _layer_b_out_bwd(dx4, e1, gt1, o, zb, w_gate1, w_b_out)
    partial = {
        "b_w_out": _wgrad(yb, dx3, "wgrad_b_w_out").reshape(N_DEV, ROWS, D),
        "ple_w1": _wgrad(p1, de1, "wgrad_ple_w1", n_split=8),
        "gate1": _wgrad(x3, dgp1, "wgrad_gate1").reshape(N_DEV, ROWS, D),
    }
    grad = {}
    dq, dk, dv, dqg, dkg, grad["b_w_out"], grad["ple_w1"], grad["gate1"] = _attn_bwd(
        q_all, k_all, v_all, qg2, kg2, d_o, csave,
        rider=_ReduceRider([partial[k] for k in ("b_w_out", "ple_w1", "gate1")]))
    partial["w_kv"] = _wgrad(hkv, [dk, dv], "wgrad_w_kv", n_split=8)
    partial["b_w_in"] = _wgrad(hb, [dq, dzb], "wgrad_b_w_in", n_split=8)
    dx2, d_bn, d_kvn, grad["w_kv"] = _layer_b_in_bwd(
        dq, dzb, dk, dv, x2, dx3, w_b_in, w_kv_f, bn, kvn, rider=_ReduceRider([partial["w_kv"]]))
    de0, dgp0, dx1, dm, duz, d_as, grad["b_w_in"] = _layer_a_out_bwd(
        dx2, e0, gt0, z, mcat, w_gate0, w_a_out, a_scale_f, w_group, rider=_ReduceRider([partial["b_w_in"]]))
    partial["gate0"] = _wgrad(x1, dgp0, "wgrad_gate0").reshape(N_DEV, ROWS, D)
    partial["a_w_out"] = _wgrad(y, dx1, "wgrad_a_w_out").reshape(N_DEV, ROWS, D)
    partial["a_w_in"], dw_group, partial["ple_w0"], grad["gate0"], grad["a_w_out"] = _wgrad_layer_a_in(
        h0, duz, pooled, dm, p0, de0, rider=_ReduceRider([partial["gate0"], partial["a_w_out"]]))
    partial["a_w_group"] = dw_group.reshape(N_GROUPS, N_DEV, GROUP_ROWS, GROUP_DIM).transpose(1, 0, 2, 3).reshape(
        N_DEV, N_GROUPS * GROUP_ROWS, GROUP_DIM)
    behind_a_in = ["a_w_in", "a_w_group", "ple_w0"]
    dx0, d_an, *done = _layer_a_in_bwd(duz, x0, dx1, w_a_in, a_norm_f,
                                      rider=_ReduceRider([partial[k] for k in behind_a_in]))
    grad.update(zip(behind_a_in, done))

    given = {
        "a_w_in": (a_w_in, m_a_w_in, v_a_w_in), "a_w_group": (a_w_group, m_a_w_group, v_a_w_group),
        "a_w_out": (a_w_out, m_a_w_out, v_a_w_out), "w_kv": (w_kv, m_w_kv, v_w_kv),
        "b_w_in": (b_w_in, m_b_w_in, v_b_w_in), "b_w_out": (b_w_out, m_b_w_out, v_b_w_out),
        "ple_w": (ple_w, m_ple_w, v_ple_w), "ple_gate_w": (ple_gate_w, m_ple_gate_w, v_ple_gate_w),
    }
    grad["ple_w"] = (grad["ple_w0"], grad["ple_w1"])
    grad["ple_gate_w"] = (grad["gate0"], grad["gate1"])
    updated = _adamw_all(
        "adamw_shards", list(big.values()), [grad[k] for k in big],
        [given[k][1].reshape(big[k].shape) for k in big], [given[k][2].reshape(big[k].shape) for k in big])
    res = {k: tuple(t.reshape(given[k][0].shape) for t in four) for k, four in zip(big, updated)}

    rows = jnp.concatenate([d_kvn, d_bn, d_an, d_as, jnp.zeros((2, D), F32), sq_err, jnp.zeros((1, D), F32)], axis=0)
    tot = _all_reduce_small(rows, (dqg, dkg))
    loss = tot[6, 0]
    small_grad = {
        "kv_norm": tot[0:1], "b_norm": tot[1:2],
        "a_norm": lax.dynamic_slice_in_dim(tot[2:3], me * ROWS, ROWS, axis=1),
        "a_scale": lax.dynamic_slice_in_dim(tot[3:4], me * ROWS, ROWS, axis=1),
        "b_q_norm": tot[4:5, :HEAD_DIM], "k_norm": tot[5:6, :HEAD_DIM],
    }
    small_given = {
        "a_norm": (a_norm, m_a_norm, v_a_norm), "a_scale": (a_scale, m_a_scale, v_a_scale),
        "kv_norm": (kv_norm, m_kv_norm, v_kv_norm), "k_norm": (k_norm, m_k_norm, v_k_norm),
        "b_norm": (b_norm, m_b_norm, v_b_norm), "b_q_norm": (b_q_norm, m_b_q_norm, v_b_q_norm),
    }
    rows_of = {k: [t.reshape(1, -1) for t in three] for k, three in small_given.items()}
    updated = _adamw_all(
        "adamw_gains", [rows_of[k][0] for k in small_given], [small_grad[k] for k in small_given],
        [rows_of[k][1] for k in small_given], [rows_of[k][2] for k in small_given])
    res.update({k: tuple(t.reshape(small_given[k][0].shape) for t in four) for k, four in zip(small_given, updated)})

    order = ["a_norm", "a_w_in", "a_w_group", "a_scale", "a_w_out", "kv_norm", "w_kv", "k_norm", "b_norm",
             "b_w_in", "b_q_norm", "b_w_out", "ple_w", "ple_gate_w"]
    outs = [res[k][kind] for kind in range(4) for k in order]
    return (loss, dx0.reshape(x.shape), *outs)
```

```python
import jax
import jax.numpy as jnp
from jax import lax
from jax.experimental import pallas as pl
from jax.experimental.pallas import tpu as pltpu

F32 = jnp.float32
BF16 = jnp.bfloat16
MESH = pl.DeviceIdType.MESH

N_DEV = 8
D = 1024
N_GROUPS = 4
GROUP_DIM = D // N_GROUPS
HALO = 16
HEADS = 16
HEAD_DIM = D // HEADS
SB_SCALE = HEAD_DIM ** -0.5
TILE = 256
LANES = 128
DEAD_LOG = -120.0
EPS = 1e-6
ADAM_LR = 0.001
ADAM_B1 = 0.9
ADAM_B2 = 0.999
ADAM_EPS = 1e-08
ADAM_WD = 0.01
ADAM_STEP = 10
TM = 256
COLS = 2 * D // N_DEV
ROWS = D // N_DEV
GROUP_ROWS = GROUP_DIM // N_DEV
VMEM_LIMIT = 56 * 1024 * 1024

HBM_SPEC = pl.BlockSpec(memory_space=pltpu.HBM)


def _dot(a, b):
    return jnp.dot(a, b, preferred_element_type=F32)


def _dot_nt(a, b):
    return lax.dot_general(a, b, (((1,), (1,)), ((), ())), preferred_element_type=F32)


def _dot_tn(a, b):
    return lax.dot_general(a, b, (((0,), (0,)), ((), ())), preferred_element_type=F32)


def _sigmoid(x):
    return jax.nn.sigmoid(x)


def _split_dot(x, mat):
    hi = x.astype(BF16)
    lo = (x - hi.astype(F32)).astype(BF16)
    return _dot(hi, mat) + _dot(lo, mat)


def _split_dot_many(xs, mat):
    rows = xs[0].shape[0]
    his = [x.astype(BF16) for x in xs]
    los = [(x - hi.astype(F32)).astype(BF16) for x, hi in zip(xs, his)]
    out = _dot(jnp.concatenate(his + los, axis=0), mat)
    n = len(xs)
    return [out[i * rows:(i + 1) * rows] + out[(n + i) * rows:(n + i + 1) * rows] for i in range(n)]


def _stack_dot(xs, mat):
    rows = xs[0].shape[0]
    out = _dot(jnp.concatenate([x.astype(BF16) for x in xs], axis=0), mat)
    return [out[i * rows:(i + 1) * rows] for i in range(len(xs))]


def _rms(x):
    return lax.rsqrt(jnp.mean(x * x, axis=-1, keepdims=True) + EPS)


def _hosted_call(body, name, n_steps, ins, in_specs, out_specs, out_shape, scratch, rider=None):
    ins, scratch = list(ins), list(scratch)
    if rider is None:
        wrapped, extra_in, extra_out, extra_scratch = body, [], [], []
    else:
        extra_in, extra_out, extra_scratch = rider.arrays, rider.out_shape(), rider.scratch()
        n_in, n_out, n_scr = len(ins), len(out_shape), len(scratch)
        k_in, k_out = len(extra_in), len(extra_out)

        def wrapped(*refs):
            own_in, r_in = refs[:n_in], refs[n_in:n_in + k_in]
            own_out = refs[n_in + k_in:n_in + k_in + n_out]
            r_out = refs[n_in + k_in + n_out:n_in + k_in + n_out + k_out]
            rest = refs[n_in + k_in + n_out + k_out:]
            phases = rider.bind(r_in, r_out, rest[n_scr:])
            step = pl.program_id(0)
            pl.when(step == 0)(phases[0])
            body(*own_in, *own_out, *rest[:n_scr])
            at = 0
            for share, phase in zip(rider.WHEN[1:], phases[1:]):
                at = min(n_steps - 1, max(at + 1, round(share * (n_steps - 1))))
                pl.when(step == at)(phase)

    return pl.pallas_call(
        wrapped, name=name, grid=(n_steps,),
        in_specs=list(in_specs) + [HBM_SPEC] * len(extra_in),
        out_specs=list(out_specs) + [HBM_SPEC] * len(extra_out),
        out_shape=list(out_shape) + list(extra_out), scratch_shapes=scratch + list(extra_scratch),
        compiler_params=pltpu.CompilerParams(dimension_semantics=("arbitrary",), vmem_limit_bytes=VMEM_LIMIT),
    )(*ins, *extra_in)


def _rows_call(body, name, n_rows, row_ins, const_ins, row_outs, const_outs=(), scratch=(),
               reverse=False, tm=TM, rider=None):
    nb = n_rows // tm

    def row_map(i):
        return ((nb - 1 - i) if reverse else i, 0)

    def const_map(nd):
        return lambda i: (0,) * nd

    in_specs = [pl.BlockSpec((tm, a.shape[1]), row_map) for a in row_ins]
    in_specs += [pl.BlockSpec(a.shape, const_map(a.ndim)) for a in const_ins]
    out_specs = [pl.BlockSpec((tm, w), row_map) for (w, _) in row_outs]
    out_specs += [pl.BlockSpec(s, const_map(len(s))) for (s, _) in const_outs]
    out_shape = [jax.ShapeDtypeStruct((n_rows, w), dt) for (w, dt) in row_outs]
    out_shape += [jax.ShapeDtypeStruct(s, dt) for (s, dt) in const_outs]
    return _hosted_call(body, name, nb, list(row_ins) + list(const_ins), in_specs, out_specs, out_shape,
                        scratch, rider)


def _ple_fwd(p_ref, xin, wple_ref, wgate_ref, e_ref, gt_ref):
    pb = p_ref[...].astype(BF16)
    for j in range(N_DEV):
        e_ref[:, j * ROWS:(j + 1) * ROWS] = _dot(pb, wple_ref[j])
    gt = _sigmoid(_dot(xin.astype(BF16), wgate_ref[...]))
    gt_ref[...] = gt
    return xin + e_ref[...] * gt


def _layer_a_fwd(x0, p0, a_norm, a_scale, w_in, w_group, w_out, w_ple, w_gate, rider=None):
    s = x0.shape[0]
    tm = TM

    def body(x_ref, p_ref, an_ref, as_ref, win_ref, wg_ref, wout_ref, wple_ref, wgate_ref,
             h_ref, z_ref, pooled_ref, m_ref, y_ref, x1_ref, e_ref, gt_ref, x2_ref, uext):
        i = pl.program_id(0)

        @pl.when(i == 0)
        def _():
            uext[0:HALO, :] = jnp.zeros((HALO, D), F32)

        x = x_ref[...]
        h = (x * _rms(x) * an_ref[...]).astype(BF16)
        h_ref[...] = h
        for j in range(N_DEV):
            uz = _dot(h, win_ref[j])
            if j < 4:
                uext[HALO:HALO + tm, j * COLS:(j + 1) * COLS] = uz
            else:
                z_ref[:, (j - 4) * COLS:(j - 3) * COLS] = uz
        t = i * tm + lax.broadcasted_iota(jnp.int32, (tm, 1), 0)
        for g in range(N_GROUPS):
            w = 2 ** (g + 1)
            cols = slice(g * GROUP_DIM, (g + 1) * GROUP_DIM)
            ext = uext[:, cols]
            acc = ext
            k = 1
            while k < w:
                acc = acc + pltpu.roll(acc, k, 0)
                k *= 2
            cnt = jnp.minimum(t + 1, w).astype(F32)
            pooled = (acc[HALO:] / cnt - ext[HALO:]).astype(BF16)
            pooled_ref[:, cols] = pooled
            m_ref[:, cols] = _dot(pooled, wg_ref[g])
        uext[0:HALO, :] = uext[tm:tm + HALO, :]
        z = z_ref[...]
        y = (m_ref[...] * as_ref[...] * (z * _sigmoid(z))).astype(BF16)
        y_ref[...] = y
        x1 = x + _dot(y, wout_ref[...])
        x1_ref[...] = x1
        x2_ref[...] = _ple_fwd(p_ref, x1, wple_ref, wgate_ref, e_ref, gt_ref)

    row_outs = [(D, BF16), (D, F32), (D, BF16), (D, F32), (D, BF16), (D, F32), (D, F32), (D, F32), (D, F32)]
    return _rows_call(body, "layer_a_fwd", s, [x0, p0], [a_norm, a_scale, w_in, w_group, w_out, w_ple, w_gate],
                      row_outs, scratch=[pltpu.VMEM((tm + HALO, D), F32)], rider=rider)


def _layer_b_in_fwd(x2, kv_norm, b_norm, w_kv, w_bin):
    s = x2.shape[0]

    def body(x_ref, kvn_ref, bn_ref, wkv_ref, wbin_ref, hkv_ref, hb_ref, k_ref, v_ref, q_ref, zb_ref):
        x = x_ref[...]
        n = x * _rms(x)
        hkv = (n * kvn_ref[...]).astype(BF16)
        hb = (n * bn_ref[...]).astype(BF16)
        hkv_ref[...] = hkv
        hb_ref[...] = hb
        for j in range(N_DEV):
            kv = _dot(hkv, wkv_ref[j])
            qz = _dot(hb, wbin_ref[j])
            if j < 4:
                cols = slice(j * COLS, (j + 1) * COLS)
                k_ref[:, cols] = kv
                q_ref[:, cols] = qz
            else:
                cols = slice((j - 4) * COLS, (j - 3) * COLS)
                v_ref[:, cols] = kv.astype(BF16)
                zb_ref[:, cols] = qz

    row_outs = [(D, BF16), (D, BF16), (D, F32), (D, BF16), (D, F32), (D, F32)]
    return _rows_call(body, "layer_b_in_fwd", s, [x2], [kv_norm, b_norm, w_kv, w_bin], row_outs)


def _tri(after):
    r = lax.broadcasted_iota(jnp.int32, (TILE, TILE), 0)
    c = lax.broadcasted_iota(jnp.int32, (TILE, TILE), 1)
    return jnp.where((r > c) if after else (r < c), 1.0, 0.0).astype(BF16)


def _half_sums(v):
    r = lax.broadcasted_iota(jnp.int32, (LANES, LANES), 0) < HEAD_DIM
    c = lax.broadcasted_iota(jnp.int32, (LANES, LANES), 1) < HEAD_DIM
    same_head = jnp.where(r == c, 1.0, 0.0).astype(BF16)
    return _dot(v.astype(BF16), same_head)


def _pair_norm(x):
    r = lax.rsqrt(_half_sums(x * x) * (1.0 / HEAD_DIM) + EPS)
    return x * r, r


def _tile_logits(qblk, kblk, diagonal):
    l = _dot_nt(qblk, kblk)
    sp = jnp.maximum(l, 0.0) + jnp.log(1.0 + jnp.exp(-jnp.abs(l)))
    ls = l - sp
    if not diagonal:
        return None, -sp, ls
    mask = lax.broadcasted_iota(jnp.int32, l.shape, 1) < lax.broadcasted_iota(jnp.int32, l.shape, 0)
    return mask, jnp.where(mask, -sp, 0.0), ls


def _attn_fwd(q_all, k_all, v_all, q_gain2, k_gain2, rider=None):
    s = q_all.shape[0]
    nt = s // TILE

    def body(q_ref, k_ref, v_ref, qg_ref, kg_ref, o_ref, c_ref, qs, ks, vs, tri, acc, right, cmat):
        tri[...] = _tri(True)
        lane = lax.broadcasted_iota(jnp.int32, (TILE, LANES), 1)
        qn, _ = _pair_norm(q_ref[...])
        kn, _ = _pair_norm(k_ref[...])
        qsc = (qn * qg_ref[...] * SB_SCALE).astype(BF16)
        ksc = (kn * kg_ref[...]).astype(BF16)
        for hh in range(2):
            sl = slice(hh * HEAD_DIM, (hh + 1) * HEAD_DIM)
            qs[hh] = qsc[:, sl]
            ks[hh] = ksc[:, sl]
            vs[hh] = v_ref[:, sl]

        def tile(qrows, kb, diagonal):
            rows = pl.ds(pl.multiple_of(kb * TILE, TILE), TILE)
            loaded = [(qs[hh, qrows, :], ks[hh, rows, :], vs[hh, rows, :], right[hh], cmat[hh], acc[hh])
                      for hh in range(2)]
            logits = [_tile_logits(q, k, diagonal) for q, k, _, _, _, _ in loaded]
            later = _split_dot_many([lk for _, lk, _ in logits], tri[...])
            results = []
            for (q, k, v, rt, cm, ac), (mask, lk, ls), lt in zip(loaded, logits, later):
                a = jnp.exp(ls + lt + rt)
                if diagonal:
                    a = jnp.where(mask, a, 0.0)
                results.append((ac + _dot(a.astype(BF16), v), jnp.where(lane == kb, rt[:, :LANES], cm),
                                rt + jnp.sum(lk, axis=1, keepdims=True)))
            for hh, (ac, cm, rt) in enumerate(results):
                acc[hh] = ac
                cmat[hh] = cm
                right[hh] = rt

        def diagonal_and_left(qrows, qb):
            here = pl.ds(pl.multiple_of(qb * TILE, TILE), TILE)
            left = pl.ds(pl.multiple_of((qb - 1) * TILE, TILE), TILE)
            q = [qs[hh, qrows, :] for hh in range(2)]
            on_diag = [_tile_logits(q[hh], ks[hh, here, :], True) for hh in range(2)]
            beside = [_tile_logits(q[hh], ks[hh, left, :], False) for hh in range(2)]
            later = _split_dot_many([lk for _, lk, _ in on_diag + beside], tri[...])
            for hh in range(2):
                mask, lk_d, ls_d = on_diag[hh]
                _, lk_l, ls_l = beside[hh]
                a_d = jnp.where(mask, jnp.exp(ls_d + later[hh]), 0.0)
                past_diag = jnp.sum(lk_d, axis=1, keepdims=True)
                a_l = jnp.exp(ls_l + later[2 + hh] + past_diag)
                acc[hh] = _dot(a_d.astype(BF16), vs[hh, here, :]) + _dot(a_l.astype(BF16), vs[hh, left, :])
                cmat[hh] = jnp.where(lane == qb - 1, past_diag, 0.0)
                right[hh] = jnp.broadcast_to(past_diag + jnp.sum(lk_l, axis=1, keepdims=True), (TILE, TILE))

        def q_step(qb, _):
            r0 = pl.multiple_of(qb * TILE, TILE)
            qrows = pl.ds(r0, TILE)

            @pl.when(qb == 0)
            def _():
                acc[...] = jnp.zeros((2, TILE, HEAD_DIM), F32)
                right[...] = jnp.zeros((2, TILE, TILE), F32)
                cmat[...] = jnp.zeros((2, TILE, LANES), F32)
                tile(qrows, qb, True)

            pl.when(qb > 0)(lambda: diagonal_and_left(qrows, qb))

            def live():
                return (jnp.max(right[:, :, :LANES]) > DEAD_LOG).astype(jnp.int32)

            def k_step(c):
                kb = c[0] - 1
                tile(qrows, kb, False)
                return kb, live()

            first, _ = lax.while_loop(lambda c: (c[0] > 0) & (c[1] > 0), k_step, (jnp.maximum(qb - 1, 0), live()))
            for hh in range(2):
                o_ref[qrows, hh * HEAD_DIM:(hh + 1) * HEAD_DIM] = acc[hh]
                c_ref[hh, qrows, :] = jnp.where(lane == LANES - 1, first.astype(F32), cmat[hh])
            return 0

        lax.fori_loop(0, nt, q_step, 0)

    pair = pl.BlockSpec((s, LANES), lambda h: (0, h))
    gain = pl.BlockSpec((1, LANES), lambda h: (0, 0))
    return _hosted_call(
        body, "attn_fwd", HEADS // 2, [q_all, k_all, v_all, q_gain2, k_gain2],
        [pair, pair, pair, gain, gain], [pair, pl.BlockSpec((2, s, LANES), lambda h: (h, 0, 0))],
        [jax.ShapeDtypeStruct((s, D), F32), jax.ShapeDtypeStruct((HEADS, s, LANES), F32)],
        [pltpu.VMEM((2, s, HEAD_DIM), BF16)] * 3
        + [pltpu.VMEM((TILE, TILE), BF16), pltpu.VMEM((2, TILE, HEAD_DIM), F32), pltpu.VMEM((2, TILE, TILE), F32),
           pltpu.VMEM((2, TILE, LANES), F32)], rider)


def _layer_b_out_fwd(o, zb, x2, p1, target, w_out, w_ple, w_gate):
    s = o.shape[0]

    def body(o_ref, zb_ref, x2_ref, p_ref, t_ref, wout_ref, wple_ref, wgate_ref,
             yb_ref, x3_ref, e_ref, gt_ref, dx4_ref, loss_ref):
        zb = zb_ref[...]
        yb = (o_ref[...] * (zb * _sigmoid(zb))).astype(BF16)
        yb_ref[...] = yb
        x3 = x2_ref[...] + _dot(yb, wout_ref[...])
        x3_ref[...] = x3
        x4 = _ple_fwd(p_ref, x3, wple_ref, wgate_ref, e_ref, gt_ref)
        d = x4 - t_ref[...]
        dx4_ref[...] = d * (1.0 / D)

        @pl.when(pl.program_id(0) == 0)
        def _():
            loss_ref[...] = jnp.zeros((1, D), F32)

        loss_ref[...] += jnp.sum(d * d, axis=0, keepdims=True)

    row_outs = [(D, BF16), (D, F32), (D, F32), (D, F32), (D, F32)]
    return _rows_call(body, "layer_b_out_fwd", s, [o, zb, x2, p1, target], [w_out, w_ple, w_gate], row_outs,
                      const_outs=[((1, D), F32)])


def _ple_bwd(dxo, e_ref, gt_ref, wgate_ref, de_ref, dgp_ref):
    e = e_ref[...]
    gt = gt_ref[...]
    de_ref[...] = (dxo * gt).astype(BF16)
    dgp = (dxo * e * gt * (1.0 - gt)).astype(BF16)
    dgp_ref[...] = dgp
    return dxo + _dot_nt(dgp, wgate_ref[...])


def _silu_grads(z):
    sg = _sigmoid(z)
    return z * sg, sg * (1.0 + z * (1.0 - sg))


def _layer_b_out_bwd(dx4, e1, gt1, o, zb, w_gate, w_out):
    s = dx4.shape[0]

    def body(dx4_ref, e_ref, gt_ref, o_ref, zb_ref, wgate_ref, wout_ref,
             de_ref, dgp_ref, dx3_ref, do_ref, dzb_ref):
        dx3 = _ple_bwd(dx4_ref[...], e_ref, gt_ref, wgate_ref, de_ref, dgp_ref)
        dx3_ref[...] = dx3
        dyb = _dot_nt(dx3.astype(BF16), wout_ref[...])
        silu, dsilu = _silu_grads(zb_ref[...])
        do_ref[...] = (dyb * silu).astype(BF16)
        dzb_ref[...] = (dyb * o_ref[...] * dsilu).astype(BF16)

    row_outs = [(D, BF16), (D, BF16), (D, F32), (D, BF16), (D, BF16)]
    return _rows_call(body, "layer_b_out_bwd", s, [dx4, e1, gt1, o, zb], [w_gate, w_out], row_outs)


def _attn_bwd(q_all, k_all, v_all, q_gain2, k_gain2, d_o, csave, rider=None):
    s = q_all.shape[0]
    nt = s // TILE

    def body(q_ref, k_ref, v_ref, qg_ref, kg_ref, do_ref, c_ref,
             dq_ref, dk_ref, dv_ref, dqg_ref, dkg_ref,
             qs, ks, vs, dos, qt, dot_t, tri_a, tri_b, dqa, dkt, dvt, dqb, left):
        tri_a[...] = _tri(True)
        tri_b[...] = _tri(False)
        lane = lax.broadcasted_iota(jnp.int32, (TILE, LANES), 1)
        qn, qr = _pair_norm(q_ref[...])
        kn, kr = _pair_norm(k_ref[...])
        qsc = qn * qg_ref[...] * SB_SCALE
        ksc = (kn * kg_ref[...]).astype(BF16)
        q_t = qsc.T.astype(BF16)
        do_t = do_ref[...].astype(F32).T.astype(BF16)
        for j in range(nt):
            qt[j] = q_t[:, j * TILE:(j + 1) * TILE]
            dot_t[j] = do_t[:, j * TILE:(j + 1) * TILE]
        qsc = qsc.astype(BF16)
        for hh in range(2):
            sl = slice(hh * HEAD_DIM, (hh + 1) * HEAD_DIM)
            qs[hh] = qsc[:, sl]
            ks[hh] = ksc[:, sl]
            vs[hh] = v_ref[:, sl]
            dos[hh] = do_ref[:, sl]

        def tile(qb, qrows, kb, diagonal):
            rows = pl.ds(pl.multiple_of(kb * TILE, TILE), TILE)
            heads = range(2)
            kblk = [ks[hh, rows, :] for hh in heads]
            logits = [_tile_logits(qs[hh, qrows, :], kblk[hh], diagonal) for hh in heads]
            later = _split_dot_many([lk for _, lk, _ in logits], tri_a[...])
            a, g = [], []
            for hh in heads:
                mask, _, ls = logits[hh]
                right = jnp.sum(jnp.where(lane == kb, c_ref[hh, qrows, :], 0.0), axis=1, keepdims=True)
                a_h = jnp.exp(ls + later[hh] + right)
                a.append(jnp.where(mask, a_h, 0.0) if diagonal else a_h)
                g.append(a[hh] * _dot_nt(dos[hh, qrows, :], vs[hh, rows, :]))
            before = _stack_dot(g, tri_b[...])
            for hh in heads:
                sl = slice(hh * HEAD_DIM, (hh + 1) * HEAD_DIM)
                mask, _, ls = logits[hh]
                beta = jnp.exp(ls)
                lf = left[hh]
                dl = g[hh] * (1.0 - beta) - (before[hh] + lf) * beta
                if diagonal:
                    dl = jnp.where(mask, dl, 0.0)
                dl = dl.astype(BF16)
                left[hh] = lf + jnp.sum(g[hh], axis=1, keepdims=True)
                dqb[hh] += _dot(dl, kblk[hh])
                dk_t, dv_t = _dot(qt[qb, sl, :], dl), _dot(dot_t[qb, sl, :], a[hh].astype(BF16))
                if diagonal:
                    dkt[kb, sl, :] = dk_t
                    dvt[kb, sl, :] = dv_t
                else:
                    dkt[kb, sl, :] += dk_t
                    dvt[kb, sl, :] += dv_t

        def left_and_diagonal(qb, qrows):
            here = pl.ds(pl.multiple_of(qb * TILE, TILE), TILE)
            beside = pl.ds(pl.multiple_of((qb - 1) * TILE, TILE), TILE)
            heads = range(2)
            q = [qs[hh, qrows, :] for hh in heads]
            do = [dos[hh, qrows, :] for hh in heads]
            k_d, k_l = [ks[hh, here, :] for hh in heads], [ks[hh, beside, :] for hh in heads]
            on_diag = [_tile_logits(q[hh], k_d[hh], True) for hh in heads]
            on_left = [_tile_logits(q[hh], k_l[hh], False) for hh in heads]
            later = _split_dot_many([lk for _, lk, _ in on_diag + on_left], tri_a[...])
            a_d, a_l, g_d, g_l = [], [], [], []
            for hh in heads:
                mask, lk_d, ls_d = on_diag[hh]
                a_d.append(jnp.where(mask, jnp.exp(ls_d + later[hh]), 0.0))
                a_l.append(jnp.exp(on_left[hh][2] + later[2 + hh] + jnp.sum(lk_d, axis=1, keepdims=True)))
                g_d.append(a_d[hh] * _dot_nt(do[hh], vs[hh, here, :]))
                g_l.append(a_l[hh] * _dot_nt(do[hh], vs[hh, beside, :]))
            before = _stack_dot(g_l + g_d, tri_b[...])
            for hh in heads:
                sl = slice(hh * HEAD_DIM, (hh + 1) * HEAD_DIM)
                beta_l, beta_d = jnp.exp(on_left[hh][2]), jnp.exp(on_diag[hh][2])
                dl_l = (g_l[hh] * (1.0 - beta_l) - before[hh] * beta_l).astype(BF16)
                carried = jnp.sum(g_l[hh], axis=1, keepdims=True)
                dl_d = g_d[hh] * (1.0 - beta_d) - (before[2 + hh] + carried) * beta_d
                dl_d = jnp.where(on_diag[hh][0], dl_d, 0.0).astype(BF16)
                dqa[qrows, sl] = (_dot(dl_l, k_l[hh]) + _dot(dl_d, k_d[hh])) * SB_SCALE
                dkt[qb - 1, sl, :] += _dot(qt[qb, sl, :], dl_l)
                dkt[qb, sl, :] = _dot(qt[qb, sl, :], dl_d)
                dvt[qb - 1, sl, :] += _dot(dot_t[qb, sl, :], a_l[hh].astype(BF16))
                dvt[qb, sl, :] = _dot(dot_t[qb, sl, :], a_d[hh].astype(BF16))

        def q_step(qb, _):
            qrows = pl.ds(pl.multiple_of(qb * TILE, TILE), TILE)
            first = jnp.max(jnp.where(lane == LANES - 1, c_ref[0, qrows, :], 0.0)).astype(jnp.int32)
            usual = (qb > 0) & (first == qb - 1)

            @pl.when(usual)
            def _():
                left_and_diagonal(qb, qrows)

            @pl.when(jnp.logical_not(usual))
            def _():
                dqb[...] = jnp.zeros((2, TILE, HEAD_DIM), F32)
                left[...] = jnp.zeros((2, TILE, TILE), F32)

                def k_step(kb, _):
                    tile(qb, qrows, kb, False)
                    return 0

                lax.fori_loop(first, qb, k_step, 0)
                tile(qb, qrows, qb, True)
                for hh in range(2):
                    dqa[qrows, hh * HEAD_DIM:(hh + 1) * HEAD_DIM] = dqb[hh] * SB_SCALE

            return 0

        lax.fori_loop(0, nt, q_step, 0)

        def norm_bwd(dy, xn, r, g_ref, dx_ref, dg_ref):
            dg_ref[...] = jnp.sum(dy * xn, axis=0, keepdims=True)
            dxn = dy * g_ref[...]
            dx_ref[...] = (r * (dxn - xn * (_half_sums(dxn * xn) * (1.0 / HEAD_DIM)))).astype(BF16)

        norm_bwd(dqa[...], qn, qr, qg_ref, dq_ref, dqg_ref)
        for j in range(nt):
            dqa[j * TILE:(j + 1) * TILE, :] = dkt[j].T
            dv_ref[j * TILE:(j + 1) * TILE, :] = dvt[j].T.astype(BF16)
        norm_bwd(dqa[...], kn, kr, kg_ref, dk_ref, dkg_ref)

    pair = pl.BlockSpec((s, LANES), lambda h: (0, h))
    gain = pl.BlockSpec((1, LANES), lambda h: (0, 0))
    dgain = pl.BlockSpec((None, 1, LANES), lambda h: (h, 0, 0))
    return _hosted_call(
        body, "attn_bwd", HEADS // 2, [q_all, k_all, v_all, q_gain2, k_gain2, d_o, csave],
        [pair, pair, pair, gain, gain, pair, pl.BlockSpec((2, s, LANES), lambda h: (h, 0, 0))],
        [pair, pair, pair, dgain, dgain],
        [jax.ShapeDtypeStruct((s, D), BF16)] * 3 + [jax.ShapeDtypeStruct((HEADS // 2, 1, LANES), F32)] * 2,
        [pltpu.VMEM((2, s, HEAD_DIM), BF16)] * 4
        + [pltpu.VMEM((nt, LANES, TILE), BF16)] * 2 + [pltpu.VMEM((TILE, TILE), BF16)] * 2
        + [pltpu.VMEM((s, LANES), F32)] + [pltpu.VMEM((nt, LANES, TILE), F32)] * 2
        + [pltpu.VMEM((2, TILE, HEAD_DIM), F32), pltpu.VMEM((2, TILE, TILE), F32)], rider)


def _norm_bwd_rows(dh, x, gain, dgain_ref):
    r = _rms(x)
    n = x * r
    dgain_ref[...] += jnp.sum(dh * n, axis=0, keepdims=True)
    dn = dh * gain
    return r * (dn - n * jnp.mean(dn * n, axis=-1, keepdims=True))


def _layer_b_in_bwd(dq, dzb, dk, dv, x2, dx3, w_bin, w_kv, b_norm, kv_norm, rider=None):
    s = x2.shape[0]

    def body(dq_ref, dzb_ref, dk_ref, dv_ref, x_ref, dx3_ref, wbin_ref, wkv_ref, bn_ref, kvn_ref,
             dx2_ref, dbn_ref, dkvn_ref):
        @pl.when(pl.program_id(0) == 0)
        def _():
            dbn_ref[...] = jnp.zeros((1, D), F32)
            dkvn_ref[...] = jnp.zeros((1, D), F32)

        dhb = jnp.zeros((TM, D), F32)
        dhkv = jnp.zeros((TM, D), F32)
        for j in range(N_DEV):
            cols = slice((j % 4) * COLS, (j % 4 + 1) * COLS)
            dhb = dhb + _dot_nt((dq_ref if j < 4 else dzb_ref)[:, cols], wbin_ref[j])
            dhkv = dhkv + _dot_nt((dk_ref if j < 4 else dv_ref)[:, cols], wkv_ref[j])
        x = x_ref[...]
        dx2 = dx3_ref[...] + _norm_bwd_rows(dhb, x, bn_ref[...], dbn_ref)
        dx2_ref[...] = dx2 + _norm_bwd_rows(dhkv, x, kvn_ref[...], dkvn_ref)

    return _rows_call(body, "layer_b_in_bwd", s, [dq, dzb, dk, dv, x2, dx3], [w_bin, w_kv, b_norm, kv_norm],
                      [(D, F32)], const_outs=[((1, D), F32), ((1, D), F32)], rider=rider)


def _layer_a_out_bwd(dx2, e0, gt0, z, m, w_gate, w_out, a_scale, w_group, rider=None):
    s = dx2.shape[0]
    tm = TM
    nb = s // tm

    def body(dx2_ref, e_ref, gt_ref, z_ref, m_ref, wgate_ref, wout_ref, as_ref, wg_ref,
             de_ref, dgp_ref, dx1_ref, dm_ref, duz_ref, das_ref, ext):
        i = pl.program_id(0)

        @pl.when(i == 0)
        def _():
            das_ref[...] = jnp.zeros((1, D), F32)
            ext[tm:tm + HALO, :] = jnp.zeros((HALO, D), F32)

        dx1 = _ple_bwd(dx2_ref[...], e_ref, gt_ref, wgate_ref, de_ref, dgp_ref)
        dx1_ref[...] = dx1
        dy = _dot_nt(dx1.astype(BF16), wout_ref[...])
        silu, dsilu = _silu_grads(z_ref[...])
        m = m_ref[...]
        dmixed = dy * silu
        duz_ref[:, D:] = (dy * (m * as_ref[...]) * dsilu).astype(BF16)
        das_ref[...] += jnp.sum(dmixed * m, axis=0, keepdims=True)
        dm_ref[...] = (dmixed * as_ref[...]).astype(BF16)
        t = (nb - 1 - i) * tm + lax.broadcasted_iota(jnp.int32, (tm, 1), 0)
        n_ext = tm + HALO
        for g in range(N_GROUPS):
            w = 2 ** (g + 1)
            cols = slice(g * GROUP_DIM, (g + 1) * GROUP_DIM)
            dpool = _dot_nt(dm_ref[:, cols], wg_ref[g])
            ext[0:tm, cols] = dpool / jnp.minimum(t + 1, w).astype(F32)
            acc = ext[:, cols]
            k = 1
            while k < w:
                acc = acc + pltpu.roll(acc, n_ext - k, 0)
                k *= 2
            duz_ref[:, cols] = (acc[:tm] - dpool).astype(BF16)
        ext[tm:tm + HALO, :] = ext[0:HALO, :]

    row_outs = [(D, BF16), (D, BF16), (D, F32), (D, BF16), (2 * D, BF16)]
    return _rows_call(body, "layer_a_out_bwd", s, [dx2, e0, gt0, z, m], [w_gate, w_out, a_scale, w_group],
                      row_outs, const_outs=[((1, D), F32)], scratch=[pltpu.VMEM((tm + HALO, D), F32)],
                      reverse=True, rider=rider)


def _layer_a_in_bwd(duz, x0, dx1, w_in, a_norm, rider=None):
    s = x0.shape[0]

    def body(duz_ref, x_ref, dx1_ref, win_ref, an_ref, dx0_ref, dan_ref):
        @pl.when(pl.program_id(0) == 0)
        def _():
            dan_ref[...] = jnp.zeros((1, D), F32)

        dh = jnp.zeros((TM, D), F32)
        for j in range(N_DEV):
            dh = dh + _dot_nt(duz_ref[:, j * COLS:(j + 1) * COLS], win_ref[j])
        dx0_ref[...] = dx1_ref[...] + _norm_bwd_rows(dh, x_ref[...], an_ref[...], dan_ref)

    return _rows_call(body, "layer_a_in_bwd", s, [duz, x0, dx1], [w_in, a_norm], [(D, F32)],
                      const_outs=[((1, D), F32)], rider=rider)


def _wgrad(a, b, name, n_split=1, rider=None):
    bs = list(b) if isinstance(b, (list, tuple)) else [b]
    s, k = a.shape
    n = sum(part.shape[1] for part in bs)
    tk = TM
    nb = n // n_split

    def body(a_ref, *refs):
        o_ref = refs[-1]
        lhs = a_ref[...].astype(BF16)
        done = 0
        for b_ref in refs[:-1]:
            res = _dot_tn(lhs, b_ref[...].astype(BF16))
            if n_split == 1:
                o_ref[...] = res.astype(BF16)
            else:
                for j in range(res.shape[1] // nb):
                    o_ref[done + j] = res[:, j * nb:(j + 1) * nb].astype(BF16)
                done += res.shape[1] // nb

    if n_split == 1:
        b_specs = [pl.BlockSpec((s, n), lambda i: (0, 0))]
        out_spec = pl.BlockSpec((tk, n), lambda i: (i, 0))
        out_shape = jax.ShapeDtypeStruct((k, n), BF16)
    else:
        b_specs = [pl.BlockSpec(part.shape, lambda i: (0, 0)) for part in bs]
        out_spec = pl.BlockSpec((n_split, tk, nb), lambda i: (0, i, 0))
        out_shape = jax.ShapeDtypeStruct((n_split, k, nb), BF16)
    res = _hosted_call(body, name, k // tk, [a] + bs, [pl.BlockSpec((s, tk), lambda i: (0, i))] + b_specs,
                       [out_spec], [out_shape], [], rider)
    return res[0] if rider is None else res


def _wgrad_layer_a_in(h0, duz, pooled, dm, p0, de0, rider):
    s = h0.shape[0]
    n_ple = p0.shape[1]

    def body(h_ref, duz_ref, pooled_ref, dm_ref, p_ref, de_ref, in_ref, group_ref, ple_ref):
        res = _dot_tn(h_ref[...], duz_ref[...])
        for j in range(N_DEV):
            in_ref[j] = res[:, j * COLS:(j + 1) * COLS].astype(BF16)
        group_ref[...] = _dot_tn(pooled_ref[...], dm_ref[...]).astype(BF16)

        @pl.when(pl.program_id(0) == 0)
        def _():
            ple = _dot_tn(p_ref[...].astype(BF16), de_ref[...])
            for j in range(N_DEV):
                ple_ref[j] = ple[:, j * ROWS:(j + 1) * ROWS].astype(BF16)

    block = pl.BlockSpec((s, TM), lambda i: (0, i))
    return _hosted_call(
        body, "wgrad_layer_a_in", D // TM, [h0, duz, pooled, dm, p0, de0],
        [block, pl.BlockSpec(duz.shape, lambda i: (0, 0)), block, block,
         pl.BlockSpec(p0.shape, lambda i: (0, 0)), pl.BlockSpec(de0.shape, lambda i: (0, 0))],
        [pl.BlockSpec((N_DEV, TM, COLS), lambda i: (0, i, 0)), pl.BlockSpec((None, TM, GROUP_DIM), lambda i: (i, 0, 0)),
         pl.BlockSpec((N_DEV, n_ple, ROWS), lambda i: (0, 0, 0))],
        [jax.ShapeDtypeStruct((N_DEV, D, COLS), BF16), jax.ShapeDtypeStruct((N_GROUPS, GROUP_DIM, GROUP_DIM), BF16),
         jax.ShapeDtypeStruct((N_DEV, n_ple, ROWS), BF16)], [], rider)


def _cast_shards(shards):
    n = len(shards)
    layers = [a.shape[0] if a.ndim == 3 else 0 for a in shards]

    def body(*refs):
        outs = iter(refs[n:])
        for a in range(n):
            if layers[a]:
                for t in range(layers[a]):
                    next(outs)[...] = refs[a][t].astype(BF16)
            else:
                next(outs)[...] = refs[a][...].astype(BF16)

    out_shape = []
    for a, k in zip(shards, layers):
        out_shape += [jax.ShapeDtypeStruct(a.shape[-2:], BF16)] * max(k, 1)
    vmem = pl.BlockSpec(memory_space=pltpu.VMEM)
    return pl.pallas_call(
        body, name="cast_shards", in_specs=[vmem] * n, out_specs=[vmem] * len(out_shape), out_shape=out_shape,
        compiler_params=pltpu.CompilerParams(vmem_limit_bytes=VMEM_LIMIT),
    )(*shards)


def _adamw(w, g, m, v):
    m = ADAM_B1 * m + (1.0 - ADAM_B1) * g
    v = ADAM_B2 * v + (1.0 - ADAM_B2) * jnp.square(g)
    m_hat = m / (1.0 - ADAM_B1 ** ADAM_STEP)
    v_hat = v / (1.0 - ADAM_B2 ** ADAM_STEP)
    delta = -ADAM_LR * (m_hat / (jnp.sqrt(v_hat) + ADAM_EPS) + ADAM_WD * w)
    return delta, m, v


def _place():
    return lax.axis_index("x"), lax.axis_index("y"), lax.axis_index("c")


def _all_gather(shards):
    return _alone("all_gather_weights", _GatherRider(shards))


def _alone(name, rider):
    n_in, n_out = len(rider.arrays), len(rider.out_shape())

    def body(*refs):
        for phase in rider.bind(refs[:n_in], refs[n_in:n_in + n_out], refs[n_in + n_out:]):
            phase()

    return pl.pallas_call(
        body, name=name, in_specs=[HBM_SPEC] * n_in, out_specs=[HBM_SPEC] * n_out,
        out_shape=rider.out_shape(), scratch_shapes=rider.scratch(),
        compiler_params=pltpu.CompilerParams(vmem_limit_bytes=VMEM_LIMIT),
    )(*rider.arrays)


class _GatherRider:
    WHEN = (0.0, 0.7, 1.0)

    def __init__(self, shards):
        self.arrays = list(shards)

    def out_shape(self):
        return _Gather.out_shape(self.arrays)

    def scratch(self):
        return _Gather.semaphores(len(self.arrays))

    def bind(self, ins, outs, scratch):
        moving = _Gather(ins, outs, *scratch)
        return moving.start, moving.forward, moving.finish


class _ReduceRider:
    WHEN = (0.0, 0.15, 0.4, 1.0)

    def __init__(self, partials):
        self.arrays = list(partials)

    def out_shape(self):
        return [jax.ShapeDtypeStruct(a.shape[1:], F32) for a in self.arrays]

    def scratch(self):
        n = len(self.arrays)
        dma = pltpu.SemaphoreType.DMA

        def blocks(k):
            return [pltpu.VMEM((k,) + a.shape[1:], BF16) for a in self.arrays]

        halves = [pltpu.VMEM((2, a.shape[1] // 2) + a.shape[2:], BF16) for a in self.arrays]
        return (blocks(4) + blocks(4) + halves + blocks(2) + [pltpu.VMEM(a.shape[1:], F32) for a in self.arrays]
                + [dma((4 * n,)), dma((4 * n,)), dma((4 * n,)), dma((2 * n,)), dma((2 * n,)),
                   dma((2 * n,)), dma((2 * n,)), dma((n,))])

    def bind(self, ins, outs, scratch):
        n = len(ins)
        mine, landed, halves, arrived, total = (scratch[i * n:(i + 1) * n] for i in range(5))
        send1, recv1, local1, send_h, recv_h, send2, recv2, out_sems = scratch[5 * n:]
        x, y, c = _place()
        plane = 2 * x + y
        via = [(x, 1 - y, c), (1 - x, y, c)]
        nbr = [(1 - x, y, c), (x, 1 - y, c)]
        nbr_block = [2 * (1 - x) + y, 2 * x + (1 - y)]
        diag_block = 2 * (1 - x) + (1 - y)

        def to_sibling(a, k):
            return pltpu.make_async_remote_copy(
                src_ref=ins[a].at[2 * k + (1 - c)], dst_ref=landed[a].at[k],
                send_sem=send1.at[4 * a + k], recv_sem=recv1.at[4 * a + k],
                device_id=(x, y, 1 - c), device_id_type=MESH)

        def own_block(a, k):
            return pltpu.make_async_copy(ins[a].at[2 * k + c], mine[a].at[k], local1.at[4 * a + k])

        def half_of(a, ref, h):
            rows = self.arrays[a].shape[1] // 2
            return ref.at[pl.ds(h * rows, rows)]

        def half_out(a, h):
            return pltpu.make_async_remote_copy(
                src_ref=half_of(a, mine[a].at[diag_block], h), dst_ref=halves[a].at[h],
                send_sem=send_h.at[2 * a + h], recv_sem=recv_h.at[2 * a + h],
                device_id=via[h], device_id_type=MESH)

        def to_owner(a, h):
            return pltpu.make_async_remote_copy(
                src_ref=mine[a].at[nbr_block[h]], dst_ref=arrived[a].at[h],
                send_sem=send2.at[2 * a + h], recv_sem=recv2.at[2 * a + h],
                device_id=nbr[h], device_id_type=MESH)

        def result(a):
            return pltpu.make_async_copy(total[a], outs[a], out_sems.at[a])

        def exchange_cores():
            for a in range(n):
                for k in range(4):
                    to_sibling(a, k).start()
                    own_block(a, k).start()

        def pair_sums():
            for a in range(n):
                for k in range(4):
                    own_block(a, k).wait()
                    to_sibling(a, k).wait_recv()
                total[a][...] = mine[a][plane].astype(F32) + landed[a][plane].astype(F32)
                for k in range(4):
                    mine[a][k] = (mine[a][k].astype(F32) + landed[a][k].astype(F32)).astype(BF16)
                for h in range(2):
                    half_out(a, h).start()

        def fold_and_send():
            for a in range(n):
                rows = self.arrays[a].shape[1] // 2
                for h in range(2):
                    half_out(a, h).wait_recv()
                    part = mine[a].at[nbr_block[h]]
                    span = slice(h * rows, (h + 1) * rows)
                    part[span] = (part[span].astype(F32) + halves[a][h].astype(F32)).astype(BF16)
                    to_owner(a, h).start()

        def finish():
            for a in range(n):
                for h in range(2):
                    to_owner(a, h).wait_recv()
                    total[a][...] += arrived[a][h].astype(F32)
                result(a).start()
            for a in range(n):
                for k in range(4):
                    to_sibling(a, k).wait_send()
                for h in range(2):
                    half_out(a, h).wait_send()
                    to_owner(a, h).wait_send()
                result(a).wait()

        return exchange_cores, pair_sums, fold_and_send, finish


class _Gather:
    COPIES = 9

    def __init__(self, ins, outs, send_sems, recv_sems, local_sems):
        self.ins, self.outs = ins, outs
        self.send_sems, self.recv_sems, self.local_sems = send_sems, recv_sems, local_sems
        self.x, self.y, self.c = _place()

    @staticmethod
    def out_shape(shards):
        return [jax.ShapeDtypeStruct((N_DEV,) + a.shape, a.dtype) for a in shards]

    @staticmethod
    def semaphores(n):
        dma = pltpu.SemaphoreType.DMA
        return [dma((_Gather.COPIES * n,)), dma((_Gather.COPIES * n,)), dma((n,))]

    def _copy(self, a, k, block, to, own=False, half=None):
        px, py, pc = block
        slot = self.outs[a].at[4 * px + 2 * py + pc]
        if half is not None:
            rows = slot.shape[0] // 2
            slot = slot.at[pl.ds(half * rows, rows)]
        return pltpu.make_async_remote_copy(
            src_ref=self.ins[a] if own else slot, dst_ref=slot,
            send_sem=self.send_sems.at[self.COPIES * a + k], recv_sem=self.recv_sems.at[self.COPIES * a + k],
            device_id=to, device_id_type=MESH)

    def _local(self, a):
        return pltpu.make_async_copy(self.ins[a], self.outs[a].at[4 * self.x + 2 * self.y + self.c],
                                     self.local_sems.at[a])

    def _plan(self, a, c):
        x, y = self.x, self.y
        me, sibling = (x, y, c), (x, y, 1 - c)
        xn, yn, dg = (1 - x, y, c), (x, 1 - y, c), (1 - x, 1 - y, c)
        return [
            self._copy(a, 0, me, sibling, own=True), self._copy(a, 1, me, xn, own=True),
            self._copy(a, 2, me, yn, own=True),
            self._copy(a, 3, xn, yn, half=0), self._copy(a, 4, yn, xn, half=1),
            self._copy(a, 5, xn, sibling), self._copy(a, 6, yn, sibling),
            self._copy(a, 7, dg, sibling, half=0), self._copy(a, 8, dg, sibling, half=1),
        ]

    def _arrivals(self, a):
        x, y, c = self.x, self.y, self.c
        me = (x, y, c)
        xn, yn, dg = (1 - x, y, c), (x, 1 - y, c), (1 - x, 1 - y, c)
        other = 1 - c
        return [
            self._copy(a, 0, (x, y, other), me), self._copy(a, 1, xn, me), self._copy(a, 2, yn, me),
            self._copy(a, 3, dg, me, half=0), self._copy(a, 4, dg, me, half=1),
            self._copy(a, 5, (1 - x, y, other), me), self._copy(a, 6, (x, 1 - y, other), me),
            self._copy(a, 7, (1 - x, 1 - y, other), me, half=0), self._copy(a, 8, (1 - x, 1 - y, other), me, half=1),
        ]

    def start(self):
        for a in range(len(self.ins)):
            self._local(a).start()
            for cp in self._plan(a, self.c)[:3]:
                cp.start()

    def forward(self):
        for a in range(len(self.ins)):
            sends, lands = self._plan(a, self.c), self._arrivals(a)
            lands[1].wait_recv()
            sends[3].start()
            sends[5].start()
            lands[2].wait_recv()
            sends[4].start()
            sends[6].start()

    def finish(self):
        n = len(self.ins)
        for a in range(n):
            sends, lands = self._plan(a, self.c), self._arrivals(a)
            lands[3].wait_recv()
            sends[7].start()
            lands[4].wait_recv()
            sends[8].start()
        for a in range(n):
            lands = self._arrivals(a)
            for k in (0, 5, 6, 7, 8):
                lands[k].wait_recv()
        for a in range(n):
            for cp in self._plan(a, self.c):
                cp.wait_send()
            self._local(a).wait()


def _adamw_all(name, ws, gs, ms, vs):
    n = len(ws)
    per_layer = [isinstance(g, tuple) for g in gs]
    flat_g = [part for g in gs for part in (g if isinstance(g, tuple) else (g,))]

    def body(*refs):
        w, refs = refs[:n], refs[n:]
        g, refs = refs[:len(flat_g)], refs[len(flat_g):]
        m, v, outs = refs[:n], refs[n:2 * n], refs[2 * n:]
        stacked = iter(outs[3 * n:])
        parts = iter(g)
        for a in range(n):
            if per_layer[a]:
                whole = next(stacked)
                for t in range(len(gs[a])):
                    grad = next(parts)[...]
                    whole[t] = grad
                    outs[a][t], outs[n + a][t], outs[2 * n + a][t] = _adamw(w[a][t], grad, m[a][t], v[a][t])
            else:
                outs[a][...], outs[n + a][...], outs[2 * n + a][...] = _adamw(
                    w[a][...], next(parts)[...], m[a][...], v[a][...])

    shapes = [jax.ShapeDtypeStruct(a.shape, F32) for a in ws]
    vmem = pl.BlockSpec(memory_space=pltpu.VMEM)
    n_out = 3 * n + sum(per_layer)
    res = pl.pallas_call(
        body, name=name, in_specs=[vmem] * (3 * n + len(flat_g)), out_specs=[vmem] * n_out,
        out_shape=shapes * 3 + [s for s, p in zip(shapes, per_layer) if p],
        compiler_params=pltpu.CompilerParams(vmem_limit_bytes=VMEM_LIMIT),
    )(*ws, *flat_g, *ms, *vs)
    stacked = iter(res[3 * n:])
    return [(next(stacked) if per_layer[a] else gs[a], res[a], res[n + a], res[2 * n + a]) for a in range(n)]


def _all_reduce_small(rows, gain_parts):
    def body(rows_ref, dqg_ref, dkg_ref, out_ref, buf, send_sems, recv_sems):
        x, y, c = _place()
        me = 4 * x + 2 * y + c
        buf[0] = rows_ref[...]
        for row, part in ((4, dqg_ref), (5, dkg_ref)):
            both = jnp.sum(part[...].reshape(HEADS // 2, LANES), axis=0, keepdims=True)
            buf[0, row:row + 1, 0:HEAD_DIM] = both[:, :HEAD_DIM] + both[:, HEAD_DIM:]
        copies = []
        for r in range(1, N_DEV):
            bx, by, bc = (r >> 2) & 1, (r >> 1) & 1, r & 1
            to = (x ^ bx, y ^ by, c ^ bc)
            copies.append(pltpu.make_async_remote_copy(
                src_ref=buf.at[0], dst_ref=buf.at[r], send_sem=send_sems.at[r - 1], recv_sem=recv_sems.at[r - 1],
                device_id=to, device_id_type=MESH))
        for cp in copies:
            cp.start()
        for cp in copies:
            cp.wait_recv()
        for cp in copies:
            cp.wait_send()
        tot = buf[me]
        for j in range(1, N_DEV):
            tot = tot + buf[j ^ me]
        out_ref[...] = tot
        loss = (0.5 / D) * jnp.sum(tot[6:7, :], axis=1, keepdims=True)
        out_ref[6:7, :] = jnp.broadcast_to(loss, (1, D))

    vmem = pl.BlockSpec(memory_space=pltpu.VMEM)
    return pl.pallas_call(
        body, name="all_reduce_small", in_specs=[vmem] * 3, out_specs=vmem,
        out_shape=jax.ShapeDtypeStruct((8, D), F32),
        scratch_shapes=[pltpu.VMEM((N_DEV, 8, D), F32), pltpu.SemaphoreType.DMA((N_DEV - 1,)),
                        pltpu.SemaphoreType.DMA((N_DEV - 1,))],
    )(rows, *gain_parts)


def kernel(x, p, a_norm, a_w_in, a_w_group, a_scale, a_w_out, kv_norm, w_kv, k_norm, b_norm, b_w_in, b_q_norm, b_w_out, ple_w, ple_gate_w, loss_target, m_a_norm, m_a_w_in, m_a_w_group, m_a_scale, m_a_w_out, m_kv_norm, m_w_kv, m_k_norm, m_b_norm, m_b_w_in, m_b_q_norm, m_b_w_out, m_ple_w, m_ple_gate_w, v_a_norm, v_a_w_in, v_a_w_group, v_a_scale, v_a_w_out, v_kv_norm, v_w_kv, v_k_norm, v_b_norm, v_b_w_in, v_b_q_norm, v_b_w_out, v_ple_w, v_ple_gate_w):
    xi, yi, ci = _place()
    me = 4 * xi + 2 * yi + ci

    big = {
        "a_w_in": a_w_in.reshape(D, COLS), "a_w_group": a_w_group.reshape(N_GROUPS * GROUP_ROWS, GROUP_DIM),
        "a_w_out": a_w_out.reshape(ROWS, D), "w_kv": w_kv, "b_w_in": b_w_in.reshape(D, COLS),
        "b_w_out": b_w_out.reshape(ROWS, D), "ple_w": ple_w, "ple_gate_w": ple_gate_w,
    }
    names = ["a_w_in", "a_w_group", "a_w_out", "w_kv", "b_w_in", "b_w_out", "ple_w0", "ple_w1", "gate0", "gate1"]
    cast = dict(zip(names, _cast_shards(list(big.values()))))
    small = jnp.concatenate([a_norm, a_scale, jnp.zeros((14, ROWS), F32)], axis=0)
    first = ["a_w_in", "a_w_group", "a_w_out", "ple_w0", "gate0"]
    behind_a = ["w_kv", "b_w_in"]
    behind_attn = ["b_w_out", "ple_w1", "gate1"]
    gathered = _all_gather([cast[k] for k in first] + [small])
    full = dict(zip(first, gathered[:-1]))
    small_all = gathered[-1]
    a_norm_f = small_all[:, 0, :].reshape(1, D)
    a_scale_f = small_all[:, 1, :].reshape(1, D)
    w_a_in = full["a_w_in"]
    w_a_out = full["a_w_out"].reshape(D, D)
    w_gate0 = full["gate0"].reshape(D, D)
    w_ple0 = full["ple_w0"]
    w_group = full["a_w_group"].reshape(N_DEV, N_GROUPS, GROUP_ROWS, GROUP_DIM).transpose(1, 0, 2, 3).reshape(
        N_GROUPS, GROUP_DIM, GROUP_DIM)
    kvn, bn = kv_norm.reshape(1, D), b_norm
    kg, qg = k_norm.reshape(1, HEAD_DIM), b_q_norm

    x0, p0, p1, target = x[0], p[0, 0], p[1, 0], loss_target[0]
    h0, z, pooled, mcat, y, x1, e0, gt0, x2, w_kv_f, w_b_in = _layer_a_fwd(
        x0, p0, a_norm_f, a_scale_f, w_a_in, w_group, w_a_out, w_ple0, w_gate0,
        rider=_GatherRider([cast[k] for k in behind_a]))
    hkv, hb, k_all, v_all, q_all, zb = _layer_b_in_fwd(x2, kvn, bn, w_kv_f, w_b_in)
    qg2, kg2 = jnp.concatenate([qg, qg], axis=1), jnp.concatenate([kg, kg], axis=1)
    o, csave, w_b_out, w_ple1, w_gate1 = _attn_fwd(
        q_all, k_all, v_all, qg2, kg2, rider=_GatherRider([cast[k] for k in behind_attn]))
    w_b_out, w_gate1 = w_b_out.reshape(D, D), w_gate1.reshape(D, D)
    yb, x3, e1, gt1, dx4, sq_err = _layer_b_out_fwd(o, zb, x2, p1, target, w_b_out, w_ple1, w_gate1)

    de1, dgp1, dx3, d_o, dzb = _layer_b_out_bwd(dx4, e1, gt1, o, zb, w_gate1, w_b_out)
    partial = {
        "b_w_out": _wgrad(yb, dx3, "wgrad_b_w_out").reshape(N_DEV, ROWS, D),
        "ple_w1": _wgrad(p1, de1, "wgrad_ple_w1", n_split=8),
        "gate1": _wgrad(x3, dgp1, "wgrad_gate1").reshape(N_DEV, ROWS, D),
    }
    grad = {}
    dq, dk, dv, dqg, dkg, grad["b_w_out"], grad["ple_w1"], grad["gate1"] = _attn_bwd(
        q_all, k_all, v_all, qg2, kg2, d_o, csave,
        rider=_ReduceRider([partial[k] for k in ("b_w_out", "ple_w1", "gate1")]))
    partial["w_kv"] = _wgrad(hkv, [dk, dv], "wgrad_w_kv", n_split=8)
    partial["b_w_in"] = _wgrad(hb, [dq, dzb], "wgrad_b_w_in", n_split=8)
    dx2, d_bn, d_kvn, grad["w_kv"] = _layer_b_in_bwd(
        dq, dzb, dk, dv, x2, dx3, w_b_in, w_kv_f, bn, kvn, rider=_ReduceRider([partial["w_kv"]]))
    de0, dgp0, dx1, dm, duz, d_as, grad["b_w_in"] = _layer_a_out_bwd(
        dx2, e0, gt0, z, mcat, w_gate0, w_a_out, a_scale_f, w_group, rider=_ReduceRider([partial["b_w_in"]]))
    partial["gate0"] = _wgrad(x1, dgp0, "wgrad_gate0").reshape(N_DEV, ROWS, D)
    partial["a_w_out"] = _wgrad(y, dx1, "wgrad_a_w_out").reshape(N_DEV, ROWS, D)
    partial["a_w_in"], dw_group, partial["ple_w0"], grad["gate0"], grad["a_w_out"] = _wgrad_layer_a_in(
        h0, duz, pooled, dm, p0, de0, rider=_ReduceRider([partial["gate0"], partial["a_w_out"]]))
    partial["a_w_group"] = dw_group.reshape(N_GROUPS, N_DEV, GROUP_ROWS, GROUP_DIM).transpose(1, 0, 2, 3).reshape(
        N_DEV, N_GROUPS * GROUP_ROWS, GROUP_DIM)
    behind_a_in = ["a_w_in", "a_w_group", "ple_w0"]
    dx0, d_an, *done = _layer_a_in_bwd(duz, x0, dx1, w_a_in, a_norm_f,
                                      rider=_ReduceRider([partial[k] for k in behind_a_in]))
    grad.update(zip(behind_a_in, done))

    given = {
        "a_w_in": (a_w_in, m_a_w_in, v_a_w_in), "a_w_group": (a_w_group, m_a_w_group, v_a_w_group),
        "a_w_out": (a_w_out, m_a_w_out, v_a_w_out), "w_kv": (w_kv, m_w_kv, v_w_kv),
        "b_w_in": (b_w_in, m_b_w_in, v_b_w_in), "b_w_out": (b_w_out, m_b_w_out, v_b_w_out),
        "ple_w": (ple_w, m_ple_w, v_ple_w), "ple_gate_w": (ple_gate_w, m_ple_gate_w, v_ple_gate_w),
    }
    grad["ple_w"] = (grad["ple_w0"], grad["ple_w1"])
    grad["ple_gate_w"] = (grad["gate0"], grad["gate1"])
    updated = _adamw_all(
        "adamw_shards", list(big.values()), [grad[k] for k in big],
        [given[k][1].reshape(big[k].shape) for k in big], [given[k][2].reshape(big[k].shape) for k in big])
    res = {k: tuple(t.reshape(given[k][0].shape) for t in four) for k, four in zip(big, updated)}

    rows = jnp.concatenate([d_kvn, d_bn, d_an, d_as, jnp.zeros((2, D), F32), sq_err, jnp.zeros((1, D), F32)], axis=0)
    tot = _all_reduce_small(rows, (dqg, dkg))
    loss = tot[6, 0]
    small_grad = {
        "kv_norm": tot[0:1], "b_norm": tot[1:2],
        "a_norm": lax.dynamic_slice_in_dim(tot[2:3], me * ROWS, ROWS, axis=1),
        "a_scale": lax.dynamic_slice_in_dim(tot[3:4], me * ROWS, ROWS, axis=1),
        "b_q_norm": tot[4:5, :HEAD_DIM], "k_norm": tot[5:6, :HEAD_DIM],
    }
    small_given = {
        "a_norm": (a_norm, m_a_norm, v_a_norm), "a_scale": (a_scale, m_a_scale, v_a_scale),
        "kv_norm": (kv_norm, m_kv_norm, v_kv_norm), "k_norm": (k_norm, m_k_norm, v_k_norm),
        "b_norm": (b_norm, m_b_norm, v_b_norm), "b_q_norm": (b_q_norm, m_b_q_norm, v_b_q_norm),
    }
    rows_of = {k: [t.reshape(1, -1) for t in three] for k, three in small_given.items()}
    updated = _adamw_all(
        "adamw_gains", [rows_of[k][0] for k in small_given], [small_grad[k] for k in small_given],
        [rows_of[k][1] for k in small_given], [rows_of[k][2] for k in small_given])
    res.update({k: tuple(t.reshape(small_given[k][0].shape) for t in four) for k, four in zip(small_given, updated)})

    order = ["a_norm", "a_w_in", "a_w_group", "a_scale", "a_w_out", "kv_norm", "w_kv", "k_norm", "b_norm",
             "b_w_in", "b_q_norm", "b_w_out", "ple_w", "ple_gate_w"]
    outs = [res[k][kind] for kind in range(4) for k in order]
    return (loss, dx0.reshape(x.shape), *outs)
```

```python
import jax
import jax.numpy as jnp
from jax import lax
from jax.experimental import pallas as pl
from jax.experimental.pallas import tpu as pltpu

F32 = jnp.float32
BF16 = jnp.bfloat16
MESH = pl.DeviceIdType.MESH

N_DEV = 8
D = 1024
N_GROUPS = 4
GROUP_DIM = D // N_GROUPS
HALO = 16
HEADS = 16
HEAD_DIM = D // HEADS
SB_SCALE = HEAD_DIM ** -0.5
TILE = 256
LANES = 128
DEAD_LOG = -120.0
EPS = 1e-6
ADAM_LR = 0.001
ADAM_B1 = 0.9
ADAM_B2 = 0.999
ADAM_EPS = 1e-08
ADAM_WD = 0.01
ADAM_STEP = 10
TM = 256
COLS = 2 * D // N_DEV
ROWS = D // N_DEV
GROUP_ROWS = GROUP_DIM // N_DEV
VMEM_LIMIT = 56 * 1024 * 1024

HBM_SPEC = pl.BlockSpec(memory_space=pltpu.HBM)


def _dot(a, b):
    return jnp.dot(a, b, preferred_element_type=F32)


def _dot_nt(a, b):
    return lax.dot_general(a, b, (((1,), (1,)), ((), ())), preferred_element_type=F32)


def _dot_tn(a, b):
    return lax.dot_general(a, b, (((0,), (0,)), ((), ())), preferred_element_type=F32)


def _sigmoid(x):
    return jax.nn.sigmoid(x)


def _split_dot_many(xs, mat):
    rows = xs[0].shape[0]
    his = [x.astype(BF16) for x in xs]
    los = [(x - hi.astype(F32)).astype(BF16) for x, hi in zip(xs, his)]
    out = _dot(jnp.concatenate(his + los, axis=0), mat)
    n = len(xs)
    return [out[i * rows:(i + 1) * rows] + out[(n + i) * rows:(n + i + 1) * rows] for i in range(n)]


def _stack_dot(xs, mat):
    rows = xs[0].shape[0]
    out = _dot(jnp.concatenate([x.astype(BF16) for x in xs], axis=0), mat)
    return [out[i * rows:(i + 1) * rows] for i in range(len(xs))]


def _rms(x):
    return lax.rsqrt(jnp.mean(x * x, axis=-1, keepdims=True) + EPS)


def _hosted_call(body, name, n_steps, ins, in_specs, out_specs, out_shape, scratch, rider=None):
    ins, scratch = list(ins), list(scratch)
    if rider is None:
        wrapped, extra_in, extra_out, extra_scratch = body, [], [], []
    else:
        extra_in, extra_out, extra_scratch = rider.arrays, rider.out_shape(), rider.scratch()
        n_in, n_out, n_scr = len(ins), len(out_shape), len(scratch)
        k_in, k_out = len(extra_in), len(extra_out)

        def wrapped(*refs):
            own_in, r_in = refs[:n_in], refs[n_in:n_in + k_in]
            own_out = refs[n_in + k_in:n_in + k_in + n_out]
            r_out = refs[n_in + k_in + n_out:n_in + k_in + n_out + k_out]
            rest = refs[n_in + k_in + n_out + k_out:]
            phases = rider.bind(r_in, r_out, rest[n_scr:])
            step = pl.program_id(0)
            pl.when(step == 0)(phases[0])
            body(*own_in, *own_out, *rest[:n_scr])
            at = 0
            for share, phase in zip(rider.WHEN[1:], phases[1:]):
                at = min(n_steps - 1, max(at + 1, round(share * (n_steps - 1))))
                pl.when(step == at)(phase)

    return pl.pallas_call(
        wrapped, name=name, grid=(n_steps,),
        in_specs=list(in_specs) + [HBM_SPEC] * len(extra_in),
        out_specs=list(out_specs) + [HBM_SPEC] * len(extra_out),
        out_shape=list(out_shape) + list(extra_out), scratch_shapes=scratch + list(extra_scratch),
        compiler_params=pltpu.CompilerParams(dimension_semantics=("arbitrary",), vmem_limit_bytes=VMEM_LIMIT),
    )(*ins, *extra_in)


def _rows_call(body, name, n_rows, row_ins, const_ins, row_outs, const_outs=(), scratch=(),
               reverse=False, tm=TM, rider=None):
    nb = n_rows // tm

    def row_map(i):
        return ((nb - 1 - i) if reverse else i, 0)

    def const_map(nd):
        return lambda i: (0,) * nd

    in_specs = [pl.BlockSpec((tm, a.shape[1]), row_map) for a in row_ins]
    in_specs += [pl.BlockSpec(a.shape, const_map(a.ndim)) for a in const_ins]
    out_specs = [pl.BlockSpec((tm, w), row_map) for (w, _) in row_outs]
    out_specs += [pl.BlockSpec(s, const_map(len(s))) for (s, _) in const_outs]
    out_shape = [jax.ShapeDtypeStruct((n_rows, w), dt) for (w, dt) in row_outs]
    out_shape += [jax.ShapeDtypeStruct(s, dt) for (s, dt) in const_outs]
    return _hosted_call(body, name, nb, list(row_ins) + list(const_ins), in_specs, out_specs, out_shape,
                        scratch, rider)


def _ple_fwd(p_ref, xin, wple_ref, wgate_ref, e_ref, gt_ref):
    pb = p_ref[...].astype(BF16)
    for j in range(N_DEV):
        e_ref[:, j * ROWS:(j + 1) * ROWS] = _dot(pb, wple_ref[j])
    gt = _sigmoid(_dot(xin.astype(BF16), wgate_ref[...]))
    gt_ref[...] = gt
    return xin + e_ref[...] * gt


def _layer_a_fwd(x0, p0, a_norm, a_scale, w_in, w_group, w_out, w_ple, w_gate, rider=None):
    s = x0.shape[0]
    tm = TM

    def body(x_ref, p_ref, an_ref, as_ref, win_ref, wg_ref, wout_ref, wple_ref, wgate_ref,
             h_ref, z_ref, pooled_ref, m_ref, y_ref, x1_ref, e_ref, gt_ref, x2_ref, uext):
        i = pl.program_id(0)

        @pl.when(i == 0)
        def _():
            uext[0:HALO, :] = jnp.zeros((HALO, D), F32)

        x = x_ref[...]
        h = (x * _rms(x) * an_ref[...]).astype(BF16)
        h_ref[...] = h
        for j in range(N_DEV):
            uz = _dot(h, win_ref[j])
            if j < 4:
                uext[HALO:HALO + tm, j * COLS:(j + 1) * COLS] = uz
            else:
                z_ref[:, (j - 4) * COLS:(j - 3) * COLS] = uz
        t = i * tm + lax.broadcasted_iota(jnp.int32, (tm, 1), 0)
        for g in range(N_GROUPS):
            w = 2 ** (g + 1)
            cols = slice(g * GROUP_DIM, (g + 1) * GROUP_DIM)
            ext = uext[:, cols]
            acc = ext
            k = 1
            while k < w:
                acc = acc + pltpu.roll(acc, k, 0)
                k *= 2
            cnt = jnp.minimum(t + 1, w).astype(F32)
            pooled = (acc[HALO:] / cnt - ext[HALO:]).astype(BF16)
            pooled_ref[:, cols] = pooled
            m_ref[:, cols] = _dot(pooled, wg_ref[g])
        uext[0:HALO, :] = uext[tm:tm + HALO, :]
        z = z_ref[...]
        y = (m_ref[...] * as_ref[...] * (z * _sigmoid(z))).astype(BF16)
        y_ref[...] = y
        x1 = x + _dot(y, wout_ref[...])
        x1_ref[...] = x1
        x2_ref[...] = _ple_fwd(p_ref, x1, wple_ref, wgate_ref, e_ref, gt_ref)

    row_outs = [(D, BF16), (D, F32), (D, BF16), (D, F32), (D, BF16), (D, F32), (D, F32), (D, F32), (D, F32)]
    return _rows_call(body, "layer_a_fwd", s, [x0, p0], [a_norm, a_scale, w_in, w_group, w_out, w_ple, w_gate],
                      row_outs, scratch=[pltpu.VMEM((tm + HALO, D), F32)], rider=rider)


def _layer_b_in_fwd(x2, kv_norm, b_norm, w_kv, w_bin):
    s = x2.shape[0]

    def body(x_ref, kvn_ref, bn_ref, wkv_ref, wbin_ref, hkv_ref, hb_ref, k_ref, v_ref, q_ref, zb_ref):
        x = x_ref[...]
        n = x * _rms(x)
        hkv = (n * kvn_ref[...]).astype(BF16)
        hb = (n * bn_ref[...]).astype(BF16)
        hkv_ref[...] = hkv
        hb_ref[...] = hb
        for j in range(N_DEV):
            kv = _dot(hkv, wkv_ref[j])
            qz = _dot(hb, wbin_ref[j])
            if j < 4:
                cols = slice(j * COLS, (j + 1) * COLS)
                k_ref[:, cols] = kv
                q_ref[:, cols] = qz
            else:
                cols = slice((j - 4) * COLS, (j - 3) * COLS)
                v_ref[:, cols] = kv.astype(BF16)
                zb_ref[:, cols] = qz

    row_outs = [(D, BF16), (D, BF16), (D, F32), (D, BF16), (D, F32), (D, F32)]
    return _rows_call(body, "layer_b_in_fwd", s, [x2], [kv_norm, b_norm, w_kv, w_bin], row_outs)


def _tri(after):
    r = lax.broadcasted_iota(jnp.int32, (TILE, TILE), 0)
    c = lax.broadcasted_iota(jnp.int32, (TILE, TILE), 1)
    return jnp.where((r > c) if after else (r < c), 1.0, 0.0).astype(BF16)


def _half_sums(v):
    r = lax.broadcasted_iota(jnp.int32, (LANES, LANES), 0) < HEAD_DIM
    c = lax.broadcasted_iota(jnp.int32, (LANES, LANES), 1) < HEAD_DIM
    same_head = jnp.where(r == c, 1.0, 0.0).astype(BF16)
    return _dot(v.astype(BF16), same_head)


def _pair_norm(x):
    r = lax.rsqrt(_half_sums(x * x) * (1.0 / HEAD_DIM) + EPS)
    return x * r, r


def _tile_logits(qblk, kblk, diagonal):
    l = _dot_nt(qblk, kblk)
    sp = jnp.maximum(l, 0.0) + jnp.log(1.0 + jnp.exp(-jnp.abs(l)))
    ls = l - sp
    if not diagonal:
        return None, -sp, ls
    mask = lax.broadcasted_iota(jnp.int32, l.shape, 1) < lax.broadcasted_iota(jnp.int32, l.shape, 0)
    return mask, jnp.where(mask, -sp, 0.0), ls


def _attn_fwd(q_all, k_all, v_all, q_gain2, k_gain2, rider=None):
    s = q_all.shape[0]
    nt = s // TILE

    def body(q_ref, k_ref, v_ref, qg_ref, kg_ref, o_ref, c_ref, qs, ks, vs, tri, acc, right, cmat):
        tri[...] = _tri(True)
        lane = lax.broadcasted_iota(jnp.int32, (TILE, LANES), 1)
        qn, _ = _pair_norm(q_ref[...])
        kn, _ = _pair_norm(k_ref[...])
        qsc = (qn * qg_ref[...] * SB_SCALE).astype(BF16)
        ksc = (kn * kg_ref[...]).astype(BF16)
        for hh in range(2):
            sl = slice(hh * HEAD_DIM, (hh + 1) * HEAD_DIM)
            qs[hh] = qsc[:, sl]
            ks[hh] = ksc[:, sl]
            vs[hh] = v_ref[:, sl]

        def tile(qrows, kb, diagonal):
            rows = pl.ds(pl.multiple_of(kb * TILE, TILE), TILE)
            loaded = [(qs[hh, qrows, :], ks[hh, rows, :], vs[hh, rows, :], right[hh], cmat[hh], acc[hh])
                      for hh in range(2)]
            logits = [_tile_logits(q, k, diagonal) for q, k, _, _, _, _ in loaded]
            later = _split_dot_many([lk for _, lk, _ in logits], tri[...])
            results = []
            for (q, k, v, rt, cm, ac), (mask, lk, ls), lt in zip(loaded, logits, later):
                a = jnp.exp(ls + lt + rt)
                if diagonal:
                    a = jnp.where(mask, a, 0.0)
                results.append((ac + _dot(a.astype(BF16), v), jnp.where(lane == kb, rt[:, :LANES], cm),
                                rt + jnp.sum(lk, axis=1, keepdims=True)))
            for hh, (ac, cm, rt) in enumerate(results):
                acc[hh] = ac
                cmat[hh] = cm
                right[hh] = rt

        def diagonal_and_left(qrows, qb):
            here = pl.ds(pl.multiple_of(qb * TILE, TILE), TILE)
            left = pl.ds(pl.multiple_of((qb - 1) * TILE, TILE), TILE)
            q = [qs[hh, qrows, :] for hh in range(2)]
            on_diag = [_tile_logits(q[hh], ks[hh, here, :], True) for hh in range(2)]
            beside = [_tile_logits(q[hh], ks[hh, left, :], False) for hh in range(2)]
            later = _split_dot_many([lk for _, lk, _ in on_diag + beside], tri[...])
            for hh in range(2):
                mask, lk_d, ls_d = on_diag[hh]
                _, lk_l, ls_l = beside[hh]
                a_d = jnp.where(mask, jnp.exp(ls_d + later[hh]), 0.0)
                past_diag = jnp.sum(lk_d, axis=1, keepdims=True)
                a_l = jnp.exp(ls_l + later[2 + hh] + past_diag)
                acc[hh] = _dot(a_d.astype(BF16), vs[hh, here, :]) + _dot(a_l.astype(BF16), vs[hh, left, :])
                cmat[hh] = jnp.where(lane == qb - 1, past_diag, 0.0)
                right[hh] = jnp.broadcast_to(past_diag + jnp.sum(lk_l, axis=1, keepdims=True), (TILE, TILE))

        def q_step(qb, _):
            r0 = pl.multiple_of(qb * TILE, TILE)
            qrows = pl.ds(r0, TILE)

            @pl.when(qb == 0)
            def _():
                acc[...] = jnp.zeros((2, TILE, HEAD_DIM), F32)
                right[...] = jnp.zeros((2, TILE, TILE), F32)
                cmat[...] = jnp.zeros((2, TILE, LANES), F32)
                tile(qrows, qb, True)

            pl.when(qb > 0)(lambda: diagonal_and_left(qrows, qb))

            def live():
                return (jnp.max(right[:, :, :LANES]) > DEAD_LOG).astype(jnp.int32)

            def k_step(c):
                kb = c[0] - 1
                tile(qrows, kb, False)
                return kb, live()

            first, _ = lax.while_loop(lambda c: (c[0] > 0) & (c[1] > 0), k_step, (jnp.maximum(qb - 1, 0), live()))
            for hh in range(2):
                o_ref[qrows, hh * HEAD_DIM:(hh + 1) * HEAD_DIM] = acc[hh]
                c_ref[hh, qrows, :] = jnp.where(lane == LANES - 1, first.astype(F32), cmat[hh])
            return 0

        lax.fori_loop(0, nt, q_step, 0)

    pair = pl.BlockSpec((s, LANES), lambda h: (0, h))
    gain = pl.BlockSpec((1, LANES), lambda h: (0, 0))
    return _hosted_call(
        body, "attn_fwd", HEADS // 2, [q_all, k_all, v_all, q_gain2, k_gain2],
        [pair, pair, pair, gain, gain], [pair, pl.BlockSpec((2, s, LANES), lambda h: (h, 0, 0))],
        [jax.ShapeDtypeStruct((s, D), F32), jax.ShapeDtypeStruct((HEADS, s, LANES), F32)],
        [pltpu.VMEM((2, s, HEAD_DIM), BF16)] * 3
        + [pltpu.VMEM((TILE, TILE), BF16), pltpu.VMEM((2, TILE, HEAD_DIM), F32), pltpu.VMEM((2, TILE, TILE), F32),
           pltpu.VMEM((2, TILE, LANES), F32)], rider)


def _layer_b_out_fwd(o, zb, x2, p1, target, w_out, w_ple, w_gate):
    s = o.shape[0]

    def body(o_ref, zb_ref, x2_ref, p_ref, t_ref, wout_ref, wple_ref, wgate_ref,
             yb_ref, x3_ref, e_ref, gt_ref, dx4_ref, loss_ref):
        zb = zb_ref[...]
        yb = (o_ref[...] * (zb * _sigmoid(zb))).astype(BF16)
        yb_ref[...] = yb
        x3 = x2_ref[...] + _dot(yb, wout_ref[...])
        x3_ref[...] = x3
        x4 = _ple_fwd(p_ref, x3, wple_ref, wgate_ref, e_ref, gt_ref)
        d = x4 - t_ref[...]
        dx4_ref[...] = d * (1.0 / D)

        @pl.when(pl.program_id(0) == 0)
        def _():
            loss_ref[...] = jnp.zeros((1, D), F32)

        loss_ref[...] += jnp.sum(d * d, axis=0, keepdims=True)

    row_outs = [(D, BF16), (D, F32), (D, F32), (D, F32), (D, F32)]
    return _rows_call(body, "layer_b_out_fwd", s, [o, zb, x2, p1, target], [w_out, w_ple, w_gate], row_outs,
                      const_outs=[((1, D), F32)])


def _ple_bwd(dxo, e_ref, gt_ref, wgate_ref, de_ref, dgp_ref):
    e = e_ref[...]
    gt = gt_ref[...]
    de_ref[...] = (dxo * gt).astype(BF16)
    dgp = (dxo * e * gt * (1.0 - gt)).astype(BF16)
    dgp_ref[...] = dgp
    return dxo + _dot_nt(dgp, wgate_ref[...])


def _silu_grads(z):
    sg = _sigmoid(z)
    return z * sg, sg * (1.0 + z * (1.0 - sg))


def _layer_b_out_bwd(dx4, e1, gt1, o, zb, w_gate, w_out):
    s = dx4.shape[0]

    def body(dx4_ref, e_ref, gt_ref, o_ref, zb_ref, wgate_ref, wout_ref,
             de_ref, dgp_ref, dx3_ref, do_ref, dzb_ref):
        dx3 = _ple_bwd(dx4_ref[...], e_ref, gt_ref, wgate_ref, de_ref, dgp_ref)
        dx3_ref[...] = dx3
        dyb = _dot_nt(dx3.astype(BF16), wout_ref[...])
        silu, dsilu = _silu_grads(zb_ref[...])
        do_ref[...] = (dyb * silu).astype(BF16)
        dzb_ref[...] = (dyb * o_ref[...] * dsilu).astype(BF16)

    row_outs = [(D, BF16), (D, BF16), (D, F32), (D, BF16), (D, BF16)]
    return _rows_call(body, "layer_b_out_bwd", s, [dx4, e1, gt1, o, zb], [w_gate, w_out], row_outs)


def _attn_bwd(q_all, k_all, v_all, q_gain2, k_gain2, d_o, csave, rider=None):
    s = q_all.shape[0]
    nt = s // TILE

    def body(q_ref, k_ref, v_ref, qg_ref, kg_ref, do_ref, c_ref,
             dq_ref, dk_ref, dv_ref, dqg_ref, dkg_ref,
             qs, ks, vs, dos, qt, dot_t, tri_a, tri_b, dqa, dkt, dvt, dqb, left):
        tri_a[...] = _tri(True)
        tri_b[...] = _tri(False)
        lane = lax.broadcasted_iota(jnp.int32, (TILE, LANES), 1)
        qn, qr = _pair_norm(q_ref[...])
        kn, kr = _pair_norm(k_ref[...])
        qsc = qn * qg_ref[...] * SB_SCALE
        ksc = (kn * kg_ref[...]).astype(BF16)
        q_t = qsc.T.astype(BF16)
        do_t = do_ref[...].astype(F32).T.astype(BF16)
        for j in range(nt):
            qt[j] = q_t[:, j * TILE:(j + 1) * TILE]
            dot_t[j] = do_t[:, j * TILE:(j + 1) * TILE]
        qsc = qsc.astype(BF16)
        for hh in range(2):
            sl = slice(hh * HEAD_DIM, (hh + 1) * HEAD_DIM)
            qs[hh] = qsc[:, sl]
            ks[hh] = ksc[:, sl]
            vs[hh] = v_ref[:, sl]
            dos[hh] = do_ref[:, sl]

        def tile(qb, qrows, kb, diagonal):
            rows = pl.ds(pl.multiple_of(kb * TILE, TILE), TILE)
            heads = range(2)
            kblk = [ks[hh, rows, :] for hh in heads]
            logits = [_tile_logits(qs[hh, qrows, :], kblk[hh], diagonal) for hh in heads]
            later = _split_dot_many([lk for _, lk, _ in logits], tri_a[...])
            a, g = [], []
            for hh in heads:
                mask, _, ls = logits[hh]
                right = jnp.sum(jnp.where(lane == kb, c_ref[hh, qrows, :], 0.0), axis=1, keepdims=True)
                a_h = jnp.exp(ls + later[hh] + right)
                a.append(jnp.where(mask, a_h, 0.0) if diagonal else a_h)
                g.append(a[hh] * _dot_nt(dos[hh, qrows, :], vs[hh, rows, :]))
            before = _stack_dot(g, tri_b[...])
            for hh in heads:
                sl = slice(hh * HEAD_DIM, (hh + 1) * HEAD_DIM)
                mask, _, ls = logits[hh]
                beta = jnp.exp(ls)
                lf = left[hh]
                dl = g[hh] * (1.0 - beta) - (before[hh] + lf) * beta
                if diagonal:
                    dl = jnp.where(mask, dl, 0.0)
                dl = dl.astype(BF16)
                left[hh] = lf + jnp.sum(g[hh], axis=1, keepdims=True)
                dqb[hh] += _dot(dl, kblk[hh])
                dk_t, dv_t = _dot(qt[qb, sl, :], dl), _dot(dot_t[qb, sl, :], a[hh].astype(BF16))
                if diagonal:
                    dkt[kb, sl, :] = dk_t
                    dvt[kb, sl, :] = dv_t
                else:
                    dkt[kb, sl, :] += dk_t
                    dvt[kb, sl, :] += dv_t

        def left_and_diagonal(qb, qrows):
            here = pl.ds(pl.multiple_of(qb * TILE, TILE), TILE)
            beside = pl.ds(pl.multiple_of((qb - 1) * TILE, TILE), TILE)
            heads = range(2)
            q = [qs[hh, qrows, :] for hh in heads]
            do = [dos[hh, qrows, :] for hh in heads]
            k_d, k_l = [ks[hh, here, :] for hh in heads], [ks[hh, beside, :] for hh in heads]
            on_diag = [_tile_logits(q[hh], k_d[hh], True) for hh in heads]
            on_left = [_tile_logits(q[hh], k_l[hh], False) for hh in heads]
            later = _split_dot_many([lk for _, lk, _ in on_diag + on_left], tri_a[...])
            a_d, a_l, g_d, g_l = [], [], [], []
            for hh in heads:
                mask, lk_d, ls_d = on_diag[hh]
                a_d.append(jnp.where(mask, jnp.exp(ls_d + later[hh]), 0.0))
                a_l.append(jnp.exp(on_left[hh][2] + later[2 + hh] + jnp.sum(lk_d, axis=1, keepdims=True)))
                g_d.append(a_d[hh] * _dot_nt(do[hh], vs[hh, here, :]))
                g_l.append(a_l[hh] * _dot_nt(do[hh], vs[hh, beside, :]))
            before = _stack_dot(g_l + g_d, tri_b[...])
            for hh in heads:
                sl = slice(hh * HEAD_DIM, (hh + 1) * HEAD_DIM)
                beta_l, beta_d = jnp.exp(on_left[hh][2]), jnp.exp(on_diag[hh][2])
                dl_l = (g_l[hh] * (1.0 - beta_l) - before[hh] * beta_l).astype(BF16)
                carried = jnp.sum(g_l[hh], axis=1, keepdims=True)
                dl_d = g_d[hh] * (1.0 - beta_d) - (before[2 + hh] + carried) * beta_d
                dl_d = jnp.where(on_diag[hh][0], dl_d, 0.0).astype(BF16)
                dqa[qrows, sl] = (_dot(dl_l, k_l[hh]) + _dot(dl_d, k_d[hh])) * SB_SCALE
                dkt[qb - 1, sl, :] += _dot(qt[qb, sl, :], dl_l)
                dkt[qb, sl, :] = _dot(qt[qb, sl, :], dl_d)
                dvt[qb - 1, sl, :] += _dot(dot_t[qb, sl, :], a_l[hh].astype(BF16))
                dvt[qb, sl, :] = _dot(dot_t[qb, sl, :], a_d[hh].astype(BF16))

        def q_step(qb, _):
            qrows = pl.ds(pl.multiple_of(qb * TILE, TILE), TILE)
            first = jnp.max(jnp.where(lane == LANES - 1, c_ref[0, qrows, :], 0.0)).astype(jnp.int32)
            usual = (qb > 0) & (first == qb - 1)

            @pl.when(usual)
            def _():
                left_and_diagonal(qb, qrows)

            @pl.when(jnp.logical_not(usual))
            def _():
                dqb[...] = jnp.zeros((2, TILE, HEAD_DIM), F32)
                left[...] = jnp.zeros((2, TILE, TILE), F32)

                def k_step(kb, _):
                    tile(qb, qrows, kb, False)
                    return 0

                lax.fori_loop(first, qb, k_step, 0)
                tile(qb, qrows, qb, True)
                for hh in range(2):
                    dqa[qrows, hh * HEAD_DIM:(hh + 1) * HEAD_DIM] = dqb[hh] * SB_SCALE

            return 0

        lax.fori_loop(0, nt, q_step, 0)

        def norm_bwd(dy, xn, r, g_ref, dx_ref, dg_ref):
            dg_ref[...] = jnp.sum(dy * xn, axis=0, keepdims=True)
            dxn = dy * g_ref[...]
            dx_ref[...] = (r * (dxn - xn * (_half_sums(dxn * xn) * (1.0 / HEAD_DIM)))).astype(BF16)

        norm_bwd(dqa[...], qn, qr, qg_ref, dq_ref, dqg_ref)
        for j in range(nt):
            dqa[j * TILE:(j + 1) * TILE, :] = dkt[j].T
            dv_ref[j * TILE:(j + 1) * TILE, :] = dvt[j].T.astype(BF16)
        norm_bwd(dqa[...], kn, kr, kg_ref, dk_ref, dkg_ref)

    pair = pl.BlockSpec((s, LANES), lambda h: (0, h))
    gain = pl.BlockSpec((1, LANES), lambda h: (0, 0))
    dgain = pl.BlockSpec((None, 1, LANES), lambda h: (h, 0, 0))
    return _hosted_call(
        body, "attn_bwd", HEADS // 2, [q_all, k_all, v_all, q_gain2, k_gain2, d_o, csave],
        [pair, pair, pair, gain, gain, pair, pl.BlockSpec((2, s, LANES), lambda h: (h, 0, 0))],
        [pair, pair, pair, dgain, dgain],
        [jax.ShapeDtypeStruct((s, D), BF16)] * 3 + [jax.ShapeDtypeStruct((HEADS // 2, 1, LANES), F32)] * 2,
        [pltpu.VMEM((2, s, HEAD_DIM), BF16)] * 4
        + [pltpu.VMEM((nt, LANES, TILE), BF16)] * 2 + [pltpu.VMEM((TILE, TILE), BF16)] * 2
        + [pltpu.VMEM((s, LANES), F32)] + [pltpu.VMEM((nt, LANES, TILE), F32)] * 2
        + [pltpu.VMEM((2, TILE, HEAD_DIM), F32), pltpu.VMEM((2, TILE, TILE), F32)], rider)


def _norm_bwd_rows(dh, x, gain, dgain_ref):
    r = _rms(x)
    n = x * r
    dgain_ref[...] += jnp.sum(dh * n, axis=0, keepdims=True)
    dn = dh * gain
    return r * (dn - n * jnp.mean(dn * n, axis=-1, keepdims=True))


def _layer_b_in_bwd(dq, dzb, dk, dv, x2, dx3, w_bin, w_kv, b_norm, kv_norm, rider=None):
    s = x2.shape[0]

    def body(dq_ref, dzb_ref, dk_ref, dv_ref, x_ref, dx3_ref, wbin_ref, wkv_ref, bn_ref, kvn_ref,
             dx2_ref, dbn_ref, dkvn_ref):
        @pl.when(pl.program_id(0) == 0)
        def _():
            dbn_ref[...] = jnp.zeros((1, D), F32)
            dkvn_ref[...] = jnp.zeros((1, D), F32)

        dhb = jnp.zeros((TM, D), F32)
        dhkv = jnp.zeros((TM, D), F32)
        for j in range(N_DEV):
            cols = slice((j % 4) * COLS, (j % 4 + 1) * COLS)
            dhb = dhb + _dot_nt((dq_ref if j < 4 else dzb_ref)[:, cols], wbin_ref[j])
            dhkv = dhkv + _dot_nt((dk_ref if j < 4 else dv_ref)[:, cols], wkv_ref[j])
        x = x_ref[...]
        dx2 = dx3_ref[...] + _norm_bwd_rows(dhb, x, bn_ref[...], dbn_ref)
        dx2_ref[...] = dx2 + _norm_bwd_rows(dhkv, x, kvn_ref[...], dkvn_ref)

    return _rows_call(body, "layer_b_in_bwd", s, [dq, dzb, dk, dv, x2, dx3], [w_bin, w_kv, b_norm, kv_norm],
                      [(D, F32)], const_outs=[((1, D), F32), ((1, D), F32)], rider=rider)


def _layer_a_out_bwd(dx2, e0, gt0, z, m, w_gate, w_out, a_scale, w_group, rider=None):
    s = dx2.shape[0]
    tm = TM
    nb = s // tm

    def body(dx2_ref, e_ref, gt_ref, z_ref, m_ref, wgate_ref, wout_ref, as_ref, wg_ref,
             de_ref, dgp_ref, dx1_ref, dm_ref, duz_ref, das_ref, ext):
        i = pl.program_id(0)

        @pl.when(i == 0)
        def _():
            das_ref[...] = jnp.zeros((1, D), F32)
            ext[tm:tm + HALO, :] = jnp.zeros((HALO, D), F32)

        dx1 = _ple_bwd(dx2_ref[...], e_ref, gt_ref, wgate_ref, de_ref, dgp_ref)
        dx1_ref[...] = dx1
        dy = _dot_nt(dx1.astype(BF16), wout_ref[...])
        silu, dsilu = _silu_grads(z_ref[...])
        m = m_ref[...]
        dmixed = dy * silu
        duz_ref[:, D:] = (dy * (m * as_ref[...]) * dsilu).astype(BF16)
        das_ref[...] += jnp.sum(dmixed * m, axis=0, keepdims=True)
        dm_ref[...] = (dmixed * as_ref[...]).astype(BF16)
        t = (nb - 1 - i) * tm + lax.broadcasted_iota(jnp.int32, (tm, 1), 0)
        n_ext = tm + HALO
        for g in range(N_GROUPS):
            w = 2 ** (g + 1)
            cols = slice(g * GROUP_DIM, (g + 1) * GROUP_DIM)
            dpool = _dot_nt(dm_ref[:, cols], wg_ref[g])
            ext[0:tm, cols] = dpool / jnp.minimum(t + 1, w).astype(F32)
            acc = ext[:, cols]
            k = 1
            while k < w:
                acc = acc + pltpu.roll(acc, n_ext - k, 0)
                k *= 2
            duz_ref[:, cols] = (acc[:tm] - dpool).astype(BF16)
        ext[tm:tm + HALO, :] = ext[0:HALO, :]

    row_outs = [(D, BF16), (D, BF16), (D, F32), (D, BF16), (2 * D, BF16)]
    return _rows_call(body, "layer_a_out_bwd", s, [dx2, e0, gt0, z, m], [w_gate, w_out, a_scale, w_group],
                      row_outs, const_outs=[((1, D), F32)], scratch=[pltpu.VMEM((tm + HALO, D), F32)],
                      reverse=True, rider=rider)


def _layer_a_in_bwd(duz, x0, dx1, w_in, a_norm, rider=None):
    s = x0.shape[0]

    def body(duz_ref, x_ref, dx1_ref, win_ref, an_ref, dx0_ref, dan_ref):
        @pl.when(pl.program_id(0) == 0)
        def _():
            dan_ref[...] = jnp.zeros((1, D), F32)

        dh = jnp.zeros((TM, D), F32)
        for j in range(N_DEV):
            dh = dh + _dot_nt(duz_ref[:, j * COLS:(j + 1) * COLS], win_ref[j])
        dx0_ref[...] = dx1_ref[...] + _norm_bwd_rows(dh, x_ref[...], an_ref[...], dan_ref)

    return _rows_call(body, "layer_a_in_bwd", s, [duz, x0, dx1], [w_in, a_norm], [(D, F32)],
                      const_outs=[((1, D), F32)], rider=rider)


def _wgrad(a, b, name, n_split=1, rider=None):
    bs = list(b) if isinstance(b, (list, tuple)) else [b]
    s, k = a.shape
    n = sum(part.shape[1] for part in bs)
    tk = TM
    nb = n // n_split

    def body(a_ref, *refs):
        o_ref = refs[-1]
        lhs = a_ref[...].astype(BF16)
        done = 0
        for b_ref in refs[:-1]:
            res = _dot_tn(lhs, b_ref[...].astype(BF16))
            if n_split == 1:
                o_ref[...] = res.astype(BF16)
            else:
                for j in range(res.shape[1] // nb):
                    o_ref[done + j] = res[:, j * nb:(j + 1) * nb].astype(BF16)
                done += res.shape[1] // nb

    if n_split == 1:
        b_specs = [pl.BlockSpec((s, n), lambda i: (0, 0))]
        out_spec = pl.BlockSpec((tk, n), lambda i: (i, 0))
        out_shape = jax.ShapeDtypeStruct((k, n), BF16)
    else:
        b_specs = [pl.BlockSpec(part.shape, lambda i: (0, 0)) for part in bs]
        out_spec = pl.BlockSpec((n_split, tk, nb), lambda i: (0, i, 0))
        out_shape = jax.ShapeDtypeStruct((n_split, k, nb), BF16)
    res = _hosted_call(body, name, k // tk, [a] + bs, [pl.BlockSpec((s, tk), lambda i: (0, i))] + b_specs,
                       [out_spec], [out_shape], [], rider)
    return res[0] if rider is None else res


def _wgrad_layer_a_in(h0, duz, pooled, dm, p0, de0, rider):
    s = h0.shape[0]
    n_ple = p0.shape[1]

    def body(h_ref, duz_ref, pooled_ref, dm_ref, p_ref, de_ref, in_ref, group_ref, ple_ref):
        res = _dot_tn(h_ref[...], duz_ref[...])
        for j in range(N_DEV):
            in_ref[j] = res[:, j * COLS:(j + 1) * COLS].astype(BF16)
        group_ref[...] = _dot_tn(pooled_ref[...], dm_ref[...]).astype(BF16)

        @pl.when(pl.program_id(0) == 0)
        def _():
            ple = _dot_tn(p_ref[...].astype(BF16), de_ref[...])
            for j in range(N_DEV):
                ple_ref[j] = ple[:, j * ROWS:(j + 1) * ROWS].astype(BF16)

    block = pl.BlockSpec((s, TM), lambda i: (0, i))
    return _hosted_call(
        body, "wgrad_layer_a_in", D // TM, [h0, duz, pooled, dm, p0, de0],
        [block, pl.BlockSpec(duz.shape, lambda i: (0, 0)), block, block,
         pl.BlockSpec(p0.shape, lambda i: (0, 0)), pl.BlockSpec(de0.shape, lambda i: (0, 0))],
        [pl.BlockSpec((N_DEV, TM, COLS), lambda i: (0, i, 0)), pl.BlockSpec((None, TM, GROUP_DIM), lambda i: (i, 0, 0)),
         pl.BlockSpec((N_DEV, n_ple, ROWS), lambda i: (0, 0, 0))],
        [jax.ShapeDtypeStruct((N_DEV, D, COLS), BF16), jax.ShapeDtypeStruct((N_GROUPS, GROUP_DIM, GROUP_DIM), BF16),
         jax.ShapeDtypeStruct((N_DEV, n_ple, ROWS), BF16)], [], rider)


def _cast_shards(shards):
    n = len(shards)
    layers = [a.shape[0] if a.ndim == 3 else 0 for a in shards]

    def body(*refs):
        outs = iter(refs[n:])
        for a in range(n):
            if layers[a]:
                for t in range(layers[a]):
                    next(outs)[...] = refs[a][t].astype(BF16)
            else:
                next(outs)[...] = refs[a][...].astype(BF16)

    out_shape = []
    for a, k in zip(shards, layers):
        out_shape += [jax.ShapeDtypeStruct(a.shape[-2:], BF16)] * max(k, 1)
    vmem = pl.BlockSpec(memory_space=pltpu.VMEM)
    return pl.pallas_call(
        body, name="cast_shards", in_specs=[vmem] * n, out_specs=[vmem] * len(out_shape), out_shape=out_shape,
        compiler_params=pltpu.CompilerParams(vmem_limit_bytes=VMEM_LIMIT),
    )(*shards)


def _adamw(w, g, m, v):
    m = ADAM_B1 * m + (1.0 - ADAM_B1) * g
    v = ADAM_B2 * v + (1.0 - ADAM_B2) * jnp.square(g)
    m_hat = m / (1.0 - ADAM_B1 ** ADAM_STEP)
    v_hat = v / (1.0 - ADAM_B2 ** ADAM_STEP)
    delta = -ADAM_LR * (m_hat / (jnp.sqrt(v_hat) + ADAM_EPS) + ADAM_WD * w)
    return delta, m, v


def _place():
    return lax.axis_index("x"), lax.axis_index("y"), lax.axis_index("c")


def _all_gather(shards):
    return _alone("all_gather_weights", _GatherRider(shards))


def _alone(name, rider):
    n_in, n_out = len(rider.arrays), len(rider.out_shape())

    def body(*refs):
        for phase in rider.bind(refs[:n_in], refs[n_in:n_in + n_out], refs[n_in + n_out:]):
            phase()

    return pl.pallas_call(
        body, name=name, in_specs=[HBM_SPEC] * n_in, out_specs=[HBM_SPEC] * n_out,
        out_shape=rider.out_shape(), scratch_shapes=rider.scratch(),
        compiler_params=pltpu.CompilerParams(vmem_limit_bytes=VMEM_LIMIT),
    )(*rider.arrays)


class _GatherRider:
    WHEN = (0.0, 0.7, 1.0)

    def __init__(self, shards):
        self.arrays = list(shards)

    def out_shape(self):
        return _Gather.out_shape(self.arrays)

    def scratch(self):
        return _Gather.semaphores(len(self.arrays))

    def bind(self, ins, outs, scratch):
        moving = _Gather(ins, outs, *scratch)
        return moving.start, moving.forward, moving.finish


class _ReduceRider:
    WHEN = (0.0, 0.15, 0.4, 1.0)

    def __init__(self, partials):
        self.arrays = list(partials)

    def out_shape(self):
        return [jax.ShapeDtypeStruct(a.shape[1:], F32) for a in self.arrays]

    def scratch(self):
        n = len(self.arrays)
        dma = pltpu.SemaphoreType.DMA

        def blocks(k):
            return [pltpu.VMEM((k,) + a.shape[1:], BF16) for a in self.arrays]

        halves = [pltpu.VMEM((2, a.shape[1] // 2) + a.shape[2:], BF16) for a in self.arrays]
        return (blocks(4) + blocks(4) + halves + blocks(2) + [pltpu.VMEM(a.shape[1:], F32) for a in self.arrays]
                + [dma((4 * n,)), dma((4 * n,)), dma((4 * n,)), dma((2 * n,)), dma((2 * n,)),
                   dma((2 * n,)), dma((2 * n,)), dma((n,))])

    def bind(self, ins, outs, scratch):
        n = len(ins)
        mine, landed, halves, arrived, total = (scratch[i * n:(i + 1) * n] for i in range(5))
        send1, recv1, local1, send_h, recv_h, send2, recv2, out_sems = scratch[5 * n:]
        x, y, c = _place()
        plane = 2 * x + y
        via = [(x, 1 - y, c), (1 - x, y, c)]
        nbr = [(1 - x, y, c), (x, 1 - y, c)]
        nbr_block = [2 * (1 - x) + y, 2 * x + (1 - y)]
        diag_block = 2 * (1 - x) + (1 - y)

        def to_sibling(a, k):
            return pltpu.make_async_remote_copy(
                src_ref=ins[a].at[2 * k + (1 - c)], dst_ref=landed[a].at[k],
                send_sem=send1.at[4 * a + k], recv_sem=recv1.at[4 * a + k],
                device_id=(x, y, 1 - c), device_id_type=MESH)

        def own_block(a, k):
            return pltpu.make_async_copy(ins[a].at[2 * k + c], mine[a].at[k], local1.at[4 * a + k])

        def half_of(a, ref, h):
            rows = self.arrays[a].shape[1] // 2
            return ref.at[pl.ds(h * rows, rows)]

        def half_out(a, h):
            return pltpu.make_async_remote_copy(
                src_ref=half_of(a, mine[a].at[diag_block], h), dst_ref=halves[a].at[h],
                send_sem=send_h.at[2 * a + h], recv_sem=recv_h.at[2 * a + h],
                device_id=via[h], device_id_type=MESH)

        def to_owner(a, h):
            return pltpu.make_async_remote_copy(
                src_ref=mine[a].at[nbr_block[h]], dst_ref=arrived[a].at[h],
                send_sem=send2.at[2 * a + h], recv_sem=recv2.at[2 * a + h],
                device_id=nbr[h], device_id_type=MESH)

        def result(a):
            return pltpu.make_async_copy(total[a], outs[a], out_sems.at[a])

        def exchange_cores():
            for a in range(n):
                for k in range(4):
                    to_sibling(a, k).start()
                    own_block(a, k).start()

        def pair_sums():
            for a in range(n):
                for k in range(4):
                    own_block(a, k).wait()
                    to_sibling(a, k).wait_recv()
                total[a][...] = mine[a][plane].astype(F32) + landed[a][plane].astype(F32)
                for k in range(4):
                    mine[a][k] = (mine[a][k].astype(F32) + landed[a][k].astype(F32)).astype(BF16)
                for h in range(2):
                    half_out(a, h).start()

        def fold_and_send():
            for a in range(n):
                rows = self.arrays[a].shape[1] // 2
                for h in range(2):
                    half_out(a, h).wait_recv()
                    part = mine[a].at[nbr_block[h]]
                    span = slice(h * rows, (h + 1) * rows)
                    part[span] = (part[span].astype(F32) + halves[a][h].astype(F32)).astype(BF16)
                    to_owner(a, h).start()

        def finish():
            for a in range(n):
                for h in range(2):
                    to_owner(a, h).wait_recv()
                    total[a][...] += arrived[a][h].astype(F32)
                result(a).start()
            for a in range(n):
                for k in range(4):
                    to_sibling(a, k).wait_send()
                for h in range(2):
                    half_out(a, h).wait_send()
                    to_owner(a, h).wait_send()
                result(a).wait()

        return exchange_cores, pair_sums, fold_and_send, finish


class _Gather:
    COPIES = 9

    def __init__(self, ins, outs, send_sems, recv_sems, local_sems):
        self.ins, self.outs = ins, outs
        self.send_sems, self.recv_sems, self.local_sems = send_sems, recv_sems, local_sems
        self.x, self.y, self.c = _place()

    @staticmethod
    def out_shape(shards):
        return [jax.ShapeDtypeStruct((N_DEV,) + a.shape, a.dtype) for a in shards]

    @staticmethod
    def semaphores(n):
        dma = pltpu.SemaphoreType.DMA
        return [dma((_Gather.COPIES * n,)), dma((_Gather.COPIES * n,)), dma((n,))]

    def _copy(self, a, k, block, to, own=False, half=None):
        px, py, pc = block
        slot = self.outs[a].at[4 * px + 2 * py + pc]
        if half is not None:
            rows = slot.shape[0] // 2
            slot = slot.at[pl.ds(half * rows, rows)]
        return pltpu.make_async_remote_copy(
            src_ref=self.ins[a] if own else slot, dst_ref=slot,
            send_sem=self.send_sems.at[self.COPIES * a + k], recv_sem=self.recv_sems.at[self.COPIES * a + k],
            device_id=to, device_id_type=MESH)

    def _local(self, a):
        return pltpu.make_async_copy(self.ins[a], self.outs[a].at[4 * self.x + 2 * self.y + self.c],
                                     self.local_sems.at[a])

    def _plan(self, a, c):
        x, y = self.x, self.y
        me, sibling = (x, y, c), (x, y, 1 - c)
        xn, yn, dg = (1 - x, y, c), (x, 1 - y, c), (1 - x, 1 - y, c)
        return [
            self._copy(a, 0, me, sibling, own=True), self._copy(a, 1, me, xn, own=True),
            self._copy(a, 2, me, yn, own=True),
            self._copy(a, 3, xn, yn, half=0), self._copy(a, 4, yn, xn, half=1),
            self._copy(a, 5, xn, sibling), self._copy(a, 6, yn, sibling),
            self._copy(a, 7, dg, sibling, half=0), self._copy(a, 8, dg, sibling, half=1),
        ]

    def _arrivals(self, a):
        x, y, c = self.x, self.y, self.c
        me = (x, y, c)
        xn, yn, dg = (1 - x, y, c), (x, 1 - y, c), (1 - x, 1 - y, c)
        other = 1 - c
        return [
            self._copy(a, 0, (x, y, other), me), self._copy(a, 1, xn, me), self._copy(a, 2, yn, me),
            self._copy(a, 3, dg, me, half=0), self._copy(a, 4, dg, me, half=1),
            self._copy(a, 5, (1 - x, y, other), me), self._copy(a, 6, (x, 1 - y, other), me),
            self._copy(a, 7, (1 - x, 1 - y, other), me, half=0), self._copy(a, 8, (1 - x, 1 - y, other), me, half=1),
        ]

    def start(self):
        for a in range(len(self.ins)):
            self._local(a).start()
            for cp in self._plan(a, self.c)[:3]:
                cp.start()

    def forward(self):
        for a in range(len(self.ins)):
            sends, lands = self._plan(a, self.c), self._arrivals(a)
            lands[1].wait_recv()
            sends[3].start()
            sends[5].start()
            lands[2].wait_recv()
            sends[4].start()
            sends[6].start()

    def finish(self):
        n = len(self.ins)
        for a in range(n):
            sends, lands = self._plan(a, self.c), self._arrivals(a)
            lands[3].wait_recv()
            sends[7].start()
            lands[4].wait_recv()
            sends[8].start()
        for a in range(n):
            lands = self._arrivals(a)
            for k in (0, 5, 6, 7, 8):
                lands[k].wait_recv()
        for a in range(n):
            for cp in self._plan(a, self.c):
                cp.wait_send()
            self._local(a).wait()


def _adamw_all(name, ws, gs, ms, vs, steps=1):
    n = len(ws)
    per_layer = [isinstance(g, tuple) for g in gs]
    flat_g = [part for g in gs for part in (g if isinstance(g, tuple) else (g,))]

    def body(*refs):
        w, refs = refs[:n], refs[n:]
        g, refs = refs[:len(flat_g)], refs[len(flat_g):]
        m, v, outs = refs[:n], refs[n:2 * n], refs[2 * n:]
        stacked = iter(outs[3 * n:])
        parts = iter(g)
        for a in range(n):
            if per_layer[a]:
                whole = next(stacked)
                for t in range(len(gs[a])):
                    grad = next(parts)[...]
                    whole[t] = grad
                    outs[a][t], outs[n + a][t], outs[2 * n + a][t] = _adamw(w[a][t], grad, m[a][t], v[a][t])
            else:
                outs[a][...], outs[n + a][...], outs[2 * n + a][...] = _adamw(
                    w[a][...], next(parts)[...], m[a][...], v[a][...])

    def rows_of(a):
        if a.ndim == 3:
            return pl.BlockSpec((a.shape[0], a.shape[1] // steps, a.shape[2]), lambda i: (0, i, 0))
        return pl.BlockSpec((a.shape[0] // steps, a.shape[1]), lambda i: (i, 0))

    shapes = [jax.ShapeDtypeStruct(a.shape, F32) for a in ws]
    w_specs = [rows_of(a) for a in ws]
    res = pl.pallas_call(
        body, name=name, grid=(steps,),
        in_specs=w_specs + [rows_of(a) for a in flat_g] + w_specs + w_specs,
        out_specs=w_specs * 3 + [s for s, p in zip(w_specs, per_layer) if p],
        out_shape=shapes * 3 + [s for s, p in zip(shapes, per_layer) if p],
        compiler_params=pltpu.CompilerParams(dimension_semantics=("arbitrary",), vmem_limit_bytes=VMEM_LIMIT),
    )(*ws, *flat_g, *ms, *vs)
    stacked = iter(res[3 * n:])
    return [(next(stacked) if per_layer[a] else gs[a], res[a], res[n + a], res[2 * n + a]) for a in range(n)]


def _all_reduce_small(rows, gain_parts):
    def body(rows_ref, dqg_ref, dkg_ref, out_ref, buf, send_sems, recv_sems):
        x, y, c = _place()
        me = 4 * x + 2 * y + c
        buf[0] = rows_ref[...]
        for row, part in ((4, dqg_ref), (5, dkg_ref)):
            both = jnp.sum(part[...].reshape(HEADS // 2, LANES), axis=0, keepdims=True)
            buf[0, row:row + 1, 0:HEAD_DIM] = both[:, :HEAD_DIM] + both[:, HEAD_DIM:]
        copies = []
        for r in range(1, N_DEV):
            bx, by, bc = (r >> 2) & 1, (r >> 1) & 1, r & 1
            to = (x ^ bx, y ^ by, c ^ bc)
            copies.append(pltpu.make_async_remote_copy(
                src_ref=buf.at[0], dst_ref=buf.at[r], send_sem=send_sems.at[r - 1], recv_sem=recv_sems.at[r - 1],
                device_id=to, device_id_type=MESH))
        for cp in copies:
            cp.start()
        for cp in copies:
            cp.wait_recv()
        for cp in copies:
            cp.wait_send()
        tot = buf[me]
        for j in range(1, N_DEV):
            tot = tot + buf[j ^ me]
        out_ref[...] = tot
        loss = (0.5 / D) * jnp.sum(tot[6:7, :], axis=1, keepdims=True)
        out_ref[6:7, :] = jnp.broadcast_to(loss, (1, D))

    vmem = pl.BlockSpec(memory_space=pltpu.VMEM)
    return pl.pallas_call(
        body, name="all_reduce_small", in_specs=[vmem] * 3, out_specs=vmem,
        out_shape=jax.ShapeDtypeStruct((8, D), F32),
        scratch_shapes=[pltpu.VMEM((N_DEV, 8, D), F32), pltpu.SemaphoreType.DMA((N_DEV - 1,)),
                        pltpu.SemaphoreType.DMA((N_DEV - 1,))],
    )(rows, *gain_parts)


def kernel(x, p, a_norm, a_w_in, a_w_group, a_scale, a_w_out, kv_norm, w_kv, k_norm, b_norm, b_w_in, b_q_norm, b_w_out, ple_w, ple_gate_w, loss_target, m_a_norm, m_a_w_in, m_a_w_group, m_a_scale, m_a_w_out, m_kv_norm, m_w_kv, m_k_norm, m_b_norm, m_b_w_in, m_b_q_norm, m_b_w_out, m_ple_w, m_ple_gate_w, v_a_norm, v_a_w_in, v_a_w_group, v_a_scale, v_a_w_out, v_kv_norm, v_w_kv, v_k_norm, v_b_norm, v_b_w_in, v_b_q_norm, v_b_w_out, v_ple_w, v_ple_gate_w):
    xi, yi, ci = _place()
    me = 4 * xi + 2 * yi + ci

    big = {
        "a_w_in": a_w_in.reshape(D, COLS), "a_w_group": a_w_group.reshape(N_GROUPS * GROUP_ROWS, GROUP_DIM),
        "a_w_out": a_w_out.reshape(ROWS, D), "w_kv": w_kv, "b_w_in": b_w_in.reshape(D, COLS),
        "b_w_out": b_w_out.reshape(ROWS, D), "ple_w": ple_w, "ple_gate_w": ple_gate_w,
    }
    names = ["a_w_in", "a_w_group", "a_w_out", "w_kv", "b_w_in", "b_w_out", "ple_w0", "ple_w1", "gate0", "gate1"]
    cast = dict(zip(names, _cast_shards(list(big.values()))))
    small = jnp.concatenate([a_norm, a_scale, jnp.zeros((14, ROWS), F32)], axis=0)
    first = ["a_w_in", "a_w_group", "a_w_out", "ple_w0", "gate0"]
    behind_a = ["w_kv", "b_w_in"]
    behind_attn = ["b_w_out", "ple_w1", "gate1"]
    gathered = _all_gather([cast[k] for k in first] + [small])
    full = dict(zip(first, gathered[:-1]))
    small_all = gathered[-1]
    a_norm_f = small_all[:, 0, :].reshape(1, D)
    a_scale_f = small_all[:, 1, :].reshape(1, D)
    w_a_in = full["a_w_in"]
    w_a_out = full["a_w_out"].reshape(D, D)
    w_gate0 = full["gate0"].reshape(D, D)
    w_ple0 = full["ple_w0"]
    w_group = full["a_w_group"].reshape(N_DEV, N_GROUPS, GROUP_ROWS, GROUP_DIM).transpose(1, 0, 2, 3).reshape(
        N_GROUPS, GROUP_DIM, GROUP_DIM)
    kvn, bn = kv_norm.reshape(1, D), b_norm
    kg, qg = k_norm.reshape(1, HEAD_DIM), b_q_norm

    x0, p0, p1, target = x[0], p[0, 0], p[1, 0], loss_target[0]
    h0, z, pooled, mcat, y, x1, e0, gt0, x2, w_kv_f, w_b_in = _layer_a_fwd(
        x0, p0, a_norm_f, a_scale_f, w_a_in, w_group, w_a_out, w_ple0, w_gate0,
        rider=_GatherRider([cast[k] for k in behind_a]))
    hkv, hb, k_all, v_all, q_all, zb = _layer_b_in_fwd(x2, kvn, bn, w_kv_f, w_b_in)
    qg2, kg2 = jnp.concatenate([qg, qg], axis=1), jnp.concatenate([kg, kg], axis=1)
    o, csave, w_b_out, w_ple1, w_gate1 = _attn_fwd(
        q_all, k_all, v_all, qg2, kg2, rider=_GatherRider([cast[k] for k in behind_attn]))
    w_b_out, w_gate1 = w_b_out.reshape(D, D), w_gate1.reshape(D, D)
    yb, x3, e1, gt1, dx4, sq_err = _layer_b_out_fwd(o, zb, x2, p1, target, w_b_out, w_ple1, w_gate1)

    de1, dgp1, dx3, d_o, dzb = _layer_b_out_bwd(dx4, e1, gt1, o, zb, w_gate1, w_b_out)
    partial = {
        "b_w_out": _wgrad(yb, dx3, "wgrad_b_w_out").reshape(N_DEV, ROWS, D),
        "ple_w1": _wgrad(p1, de1, "wgrad_ple_w1", n_split=8),
        "gate1": _wgrad(x3, dgp1, "wgrad_gate1").reshape(N_DEV, ROWS, D),
    }
    grad = {}
    dq, dk, dv, dqg, dkg, grad["b_w_out"], grad["ple_w1"], grad["gate1"] = _attn_bwd(
        q_all, k_all, v_all, qg2, kg2, d_o, csave,
        rider=_ReduceRider([partial[k] for k in ("b_w_out", "ple_w1", "gate1")]))
    partial["w_kv"] = _wgrad(hkv, [dk, dv], "wgrad_w_kv", n_split=8)
    partial["b_w_in"] = _wgrad(hb, [dq, dzb], "wgrad_b_w_in", n_split=8)
    dx2, d_bn, d_kvn, grad["w_kv"] = _layer_b_in_bwd(
        dq, dzb, dk, dv, x2, dx3, w_b_in, w_kv_f, bn, kvn, rider=_ReduceRider([partial["w_kv"]]))
    de0, dgp0, dx1, dm, duz, d_as, grad["b_w_in"] = _layer_a_out_bwd(
        dx2, e0, gt0, z, mcat, w_gate0, w_a_out, a_scale_f, w_group, rider=_ReduceRider([partial["b_w_in"]]))
    partial["gate0"] = _wgrad(x1, dgp0, "wgrad_gate0").reshape(N_DEV, ROWS, D)
    partial["a_w_out"] = _wgrad(y, dx1, "wgrad_a_w_out").reshape(N_DEV, ROWS, D)
    partial["a_w_in"], dw_group, partial["ple_w0"], grad["gate0"], grad["a_w_out"] = _wgrad_layer_a_in(
        h0, duz, pooled, dm, p0, de0, rider=_ReduceRider([partial["gate0"], partial["a_w_out"]]))
    partial["a_w_group"] = dw_group.reshape(N_GROUPS, N_DEV, GROUP_ROWS, GROUP_DIM).transpose(1, 0, 2, 3).reshape(
        N_DEV, N_GROUPS * GROUP_ROWS, GROUP_DIM)
    behind_a_in = ["a_w_in", "a_w_group", "ple_w0"]
    dx0, d_an, *done = _layer_a_in_bwd(duz, x0, dx1, w_a_in, a_norm_f,
                                      rider=_ReduceRider([partial[k] for k in behind_a_in]))
    grad.update(zip(behind_a_in, done))

    given = {
        "a_w_in": (a_w_in, m_a_w_in, v_a_w_in), "a_w_group": (a_w_group, m_a_w_group, v_a_w_group),
        "a_w_out": (a_w_out, m_a_w_out, v_a_w_out), "w_kv": (w_kv, m_w_kv, v_w_kv),
        "b_w_in": (b_w_in, m_b_w_in, v_b_w_in), "b_w_out": (b_w_out, m_b_w_out, v_b_w_out),
        "ple_w": (ple_w, m_ple_w, v_ple_w), "ple_gate_w": (ple_gate_w, m_ple_gate_w, v_ple_gate_w),
    }
    grad["ple_w"] = (grad["ple_w0"], grad["ple_w1"])
    grad["ple_gate_w"] = (grad["gate0"], grad["gate1"])
    updated = _adamw_all(
        "adamw_shards", list(big.values()), [grad[k] for k in big],
        [given[k][1].reshape(big[k].shape) for k in big], [given[k][2].reshape(big[k].shape) for k in big], steps=4)
    res = {k: tuple(t.reshape(given[k][0].shape) for t in four) for k, four in zip(big, updated)}

    rows = jnp.concatenate([d_kvn, d_bn, d_an, d_as, jnp.zeros((2, D), F32), sq_err, jnp.zeros((1, D), F32)], axis=0)
    tot = _all_reduce_small(rows, (dqg, dkg))
    loss = tot[6, 0]
    small_grad = {
        "kv_norm": tot[0:1], "b_norm": tot[1:2],
        "a_norm": lax.dynamic_slice_in_dim(tot[2:3], me * ROWS, ROWS, axis=1),
        "a_scale": lax.dynamic_slice_in_dim(tot[3:4], me * ROWS, ROWS, axis=1),
        "b_q_norm": tot[4:5, :HEAD_DIM], "k_norm": tot[5:6, :HEAD_DIM],
    }
    small_given = {
        "a_norm": (a_norm, m_a_norm, v_a_norm), "a_scale": (a_scale, m_a_scale, v_a_scale),
        "kv_norm": (kv_norm, m_kv_norm, v_kv_norm), "k_norm": (k_norm, m_k_norm, v_k_norm),
        "b_norm": (b_norm, m_b_norm, v_b_norm), "b_q_norm": (b_q_norm, m_b_q_norm, v_b_q_norm),
    }
    rows_of = {k: [t.reshape(1, -1) for t in three] for k, three in small_given.items()}
    updated = _adamw_all(
        "adamw_gains", [rows_of[k][0] for k in small_given], [small_grad[k] for k in small_given],
        [rows_of[k][1] for k in small_given], [rows_of[k][2] for k in small_given])
    res.update({k: tuple(t.reshape(small_given[k][0].shape) for t in four) for k, four in zip(small_given, updated)})

    order = ["a_norm", "a_w_in", "a_w_group", "a_scale", "a_w_out", "kv_norm", "w_kv", "k_norm", "b_norm",
             "b_w_in", "b_q_norm", "b_w_out", "ple_w", "ple_gate_w"]
    outs = [res[k][kind] for kind in range(4) for k in order]
    return (loss, dx0.reshape(x.shape), *outs)
```
